```python
import math
import jax, jax.numpy as jnp
from jax import lax
import numpy as np

D_MODEL = 2048
BATCH = 8
SEQ = 8192
DEPTH = 1

N_META = 16
CHUNK = 64
EPS = 1e-6
HG_HEADS = 8
HG_DK = 128
HG_DV = 128
HG_KEY_W = HG_HEADS * HG_DK
HG_VAL_W = HG_HEADS * HG_DV
GD_HEADS = 8
GD_DK = 128
GD_DV = 128
GD_KEY_W = GD_HEADS * GD_DK
GD_VAL_W = GD_HEADS * GD_DV
CONV_K = 4
GD_CONV_DIM = 2 * GD_KEY_W + GD_VAL_W
D_FF = ((-(-8 * D_MODEL // 3) + 255) // 256) * 256
SPLIT_SIZES = (HG_KEY_W, HG_KEY_W, HG_VAL_W, HG_VAL_W,
               GD_KEY_W, GD_KEY_W, GD_VAL_W, GD_VAL_W, GD_HEADS, GD_HEADS,
               D_MODEL, D_MODEL)
SPLIT_POINTS = tuple(int(s) for s in np.cumsum(SPLIT_SIZES)[:-1])
IN_DIM = int(sum(SPLIT_SIZES))

kernel_name = 'hybrid_hgrn2_gdn_block'


def rms_norm(x, w):
    xf = x.astype(jnp.float32)
    y = xf * lax.rsqrt(jnp.mean(xf * xf, axis=-1, keepdims=True) + EPS)
    return (y * w.astype(jnp.float32)).astype(x.dtype)


def to_heads(t, heads, d):
    bsz, length, _ = t.shape
    return t.reshape(bsz, length, heads, d).transpose(0, 2, 1, 3)


def l2_normalize(t):
    return t * lax.rsqrt(jnp.sum(t * t, axis=-1, keepdims=True) + EPS)


def gated_head_norm(o, gate, w):
    bsz, heads, length, d = o.shape
    o = jnp.swapaxes(o, 1, 2)
    o = o * lax.rsqrt(jnp.mean(o * o, axis=-1, keepdims=True) + EPS) * w.astype(jnp.float32)
    g = gate.astype(jnp.float32).reshape(bsz, length, heads, d)
    return (o * jax.nn.silu(g)).reshape(bsz, length, heads * d)


def causal_depthwise_conv(x, w):
    channels = x.shape[-1]
    return lax.conv_general_dilated(
        x, w[:, None, :].astype(x.dtype), window_strides=(1,), padding=[(CONV_K - 1, 0)],
        dimension_numbers=('NWC', 'WIO', 'NWC'), feature_group_count=channels)


def run_chunked(step, state, xs):
    meta = tuple(t[:, :, :N_META] for t in xs)
    real = tuple(t[:, :, N_META:] for t in xs)
    state, o_meta = step(state, meta)
    n_chunks = real[0].shape[2] // CHUNK

    def to_chunks(t):
        t = t.reshape(t.shape[:2] + (n_chunks, CHUNK) + t.shape[3:])
        return jnp.moveaxis(t, 2, 0)

    _, o_real = lax.scan(step, state, tuple(to_chunks(t) for t in real))
    o_real = jnp.moveaxis(o_real, 0, 2)
    o_real = o_real.reshape(o_real.shape[:2] + (n_chunks * CHUNK,) + o_real.shape[4:])
    return jnp.concatenate([o_meta, o_real], axis=2)


def hgrn2_chunk(state, inp):
    q, k, v, log_f = inp
    c = q.shape[2]
    b = jnp.cumsum(log_f, axis=2)
    mask = jnp.tril(jnp.ones((c, c), dtype=bool))
    diff = b[:, :, :, None, :] - b[:, :, None, :, :]
    decay = jnp.where(mask[:, :, None], jnp.exp(jnp.minimum(diff, 0.0)), 0.0)
    scores = jnp.einsum('bhtd,bhjd,bhtjd->bhtj', q, k, decay)
    o = (jnp.einsum('bhtd,bhde->bhte', q * jnp.exp(b), state)
         + jnp.einsum('bhtj,bhje->bhte', scores, v))
    b_last = b[:, :, -1:, :]
    new_state = (jnp.exp(b_last[:, :, 0, :])[..., None] * state
                 + jnp.einsum('bhjd,bhje->bhde', k * jnp.exp(b_last - b), v))
    return new_state, o


def gdn_chunk(state, inp):
    q, k, v, g, beta = inp
    c = q.shape[2]
    gc = jnp.cumsum(g, axis=-1)
    diff = gc[..., :, None] - gc[..., None, :]
    rel = jnp.exp(jnp.minimum(diff, 0.0))
    strict = jnp.tril(jnp.ones((c, c), dtype=bool), -1)
    incl = jnp.tril(jnp.ones((c, c), dtype=bool))
    kk = jnp.einsum('bhid,bhjd->bhij', k, k)
    a = jnp.where(strict, beta[..., :, None] * kk * rel, 0.0)
    eye = jnp.eye(c, dtype=q.dtype)
    rhs = jnp.concatenate([(beta * jnp.exp(gc))[..., None] * k, beta[..., None] * v], axis=-1)
    wu = lax.linalg.triangular_solve(eye + a, rhs, left_side=True, lower=True)
    w, u = wu[..., :GD_DK], wu[..., GD_DK:]
    v_new = u - jnp.einsum('bhtd,bhde->bhte', w, state)
    attn = jnp.where(incl, jnp.einsum('bhtd,bhjd->bhtj', q, k) * rel, 0.0)
    o = (jnp.einsum('bhtd,bhde->bhte', q * jnp.exp(gc)[..., None], state)
         + jnp.einsum('bhtj,bhje->bhte', attn, v_new))
    g_last = gc[..., -1:]
    new_state = (jnp.exp(g_last)[..., None] * state
                 + jnp.einsum('bhjd,bhje->bhde', k * jnp.exp(g_last - gc)[..., None], v_new))
    return new_state, o


def hgrn2_mixer(q_raw, f_raw, i_raw, g_raw, lower_bound, norm_w):
    bsz = q_raw.shape[0]
    f32 = jnp.float32
    fs = f_raw.astype(f32)
    lb = lower_bound.astype(f32)
    q = jax.nn.silu(q_raw.astype(f32))
    log_f = jnp.log(lb + (1.0 - lb) * jax.nn.sigmoid(fs))
    k = (1.0 - lb) * jax.nn.sigmoid(-fs)
    v = i_raw.astype(f32)
    q = to_heads(q, HG_HEADS, HG_DK)
    k = to_heads(k, HG_HEADS, HG_DK)
    log_f = to_heads(log_f, HG_HEADS, HG_DK)
    v = to_heads(v, HG_HEADS, HG_DV)
    state0 = jnp.zeros((bsz, HG_HEADS, HG_DK, HG_DV), f32)
    o = run_chunked(hgrn2_chunk, state0, (q, k, v, log_f))
    return gated_head_norm(o, g_raw, norm_w)


def gated_deltanet_mixer(q_raw, k_raw, v_raw, z_raw, a_raw, b_raw, conv_w, a_log, dt_bias, norm_w):
    bsz = q_raw.shape[0]
    f32 = jnp.float32
    qkv = causal_depthwise_conv(jnp.concatenate([q_raw, k_raw, v_raw], axis=-1), conv_w)
    qkv = jax.nn.silu(qkv.astype(f32))
    q = l2_normalize(to_heads(qkv[..., :GD_KEY_W], GD_HEADS, GD_DK)) * (GD_DK ** -0.5)
    k = l2_normalize(to_heads(qkv[..., GD_KEY_W:2 * GD_KEY_W], GD_HEADS, GD_DK))
    v = to_heads(qkv[..., 2 * GD_KEY_W:], GD_HEADS, GD_DV)
    beta = jnp.swapaxes(jax.nn.sigmoid(b_raw.astype(f32)), 1, 2)
    g = -jnp.exp(a_log.astype(f32)) * jax.nn.softplus(a_raw.astype(f32) + dt_bias.astype(f32))
    g = jnp.swapaxes(g, 1, 2)
    state0 = jnp.zeros((bsz, GD_HEADS, GD_DK, GD_DV), f32)
    o = run_chunked(gdn_chunk, state0, (q, k, v, g, beta))
    return gated_head_norm(o, z_raw, norm_w)


def hybrid_mixer(h, norm_w, w_in, lower_bound, hg_norm_w, conv_w, a_log, dt_bias, gd_norm_w,
                 w_branch_a, w_branch_b, w_out):
    xn = rms_norm(h, norm_w)
    proj = jnp.einsum('bld,de->ble', xn, w_in)
    (hq, hf, hi, hg, gq, gk, gv, gz, ga, gb, gate_a, gate_b) = jnp.split(proj, SPLIT_POINTS, axis=-1)
    o_a = hgrn2_mixer(hq, hf, hi, hg, lower_bound, hg_norm_w).astype(h.dtype)
    o_b = gated_deltanet_mixer(gq, gk, gv, gz, ga, gb, conv_w, a_log, dt_bias, gd_norm_w).astype(h.dtype)
    merged = jax.nn.sigmoid(gate_a) * (o_a @ w_branch_a) + jax.nn.sigmoid(gate_b) * (o_b @ w_branch_b)
    return merged @ w_out


def swiglu_ffn(x, w_in, w_out):
    gate, up = jnp.split(x @ w_in, 2, axis=-1)
    return (jax.nn.silu(gate) * up) @ w_out


def _fwd_setup_inputs(seed: int = 0) -> dict:
    key = jax.random.key(seed)
    ks = jax.random.split(key, 17)
    f32 = jnp.float32

    def nrm(k, shape, scale):
        return jax.random.normal(k, shape, f32) * scale

    x = nrm(ks[0], (BATCH, SEQ, D_MODEL), 1.0)
    meta_tokens = nrm(ks[1], (N_META, D_MODEL), 1.0)
    lb_logits = nrm(ks[2], (DEPTH + 1, HG_KEY_W), 0.5)
    mix_norm_w = 1.0 + nrm(ks[3], (DEPTH, D_MODEL), 0.02)
    w_in = nrm(ks[4], (DEPTH, D_MODEL, IN_DIM), D_MODEL ** -0.5)
    hg_norm_w = 1.0 + nrm(ks[5], (DEPTH, HG_DV), 0.02)
    gd_conv_w = nrm(ks[6], (DEPTH, CONV_K, GD_CONV_DIM), CONV_K ** -0.5)
    gd_a_log = jnp.log(jax.random.uniform(ks[7], (DEPTH, GD_HEADS), f32, 1.0, 16.0))
    dt = jnp.exp(jax.random.uniform(ks[8], (DEPTH, GD_HEADS), f32, math.log(1e-3), math.log(1e-1)))
    gd_dt_bias = dt + jnp.log(-jnp.expm1(-dt))
    gd_norm_w = 1.0 + nrm(ks[9], (DEPTH, GD_DV), 0.02)
    w_branch_a = nrm(ks[10], (DEPTH, HG_VAL_W, D_MODEL), HG_VAL_W ** -0.5)
    w_branch_b = nrm(ks[11], (DEPTH, GD_VAL_W, D_MODEL), GD_VAL_W ** -0.5)
    w_out = nrm(ks[12], (DEPTH, D_MODEL, D_MODEL), D_MODEL ** -0.5)
    ffn_norm_w = 1.0 + nrm(ks[13], (DEPTH, D_MODEL), 0.02)
    w_ffn_in = nrm(ks[14], (DEPTH, D_MODEL, 2 * D_FF), D_MODEL ** -0.5)
    w_ffn_out = nrm(ks[15], (DEPTH, D_FF, D_MODEL), D_FF ** -0.5)
    final_norm_w = 1.0 + nrm(ks[16], (D_MODEL,), 0.02)
    return {'x': x, 'meta_tokens': meta_tokens, 'lb_logits': lb_logits, 'mix_norm_w': mix_norm_w,
            'w_in': w_in, 'hg_norm_w': hg_norm_w, 'gd_conv_w': gd_conv_w, 'gd_a_log': gd_a_log,
            'gd_dt_bias': gd_dt_bias, 'gd_norm_w': gd_norm_w, 'w_branch_a': w_branch_a,
            'w_branch_b': w_branch_b, 'w_out': w_out, 'ffn_norm_w': ffn_norm_w,
            'w_ffn_in': w_ffn_in, 'w_ffn_out': w_ffn_out, 'final_norm_w': final_norm_w}


def _fwd_reference(x, meta_tokens, lb_logits, mix_norm_w, w_in, hg_norm_w, gd_conv_w, gd_a_log,
              gd_dt_bias, gd_norm_w, w_branch_a, w_branch_b, w_out, ffn_norm_w, w_ffn_in,
              w_ffn_out, final_norm_w):
    bsz = x.shape[0]
    meta = jnp.broadcast_to(meta_tokens[None].astype(x.dtype), (bsz, N_META, D_MODEL))
    h = jnp.concatenate([meta, x], axis=1)
    lower_bounds = jnp.cumsum(jax.nn.softmax(lb_logits.astype(jnp.float32), axis=0), axis=0)
    for layer in range(DEPTH):
        h = h + hybrid_mixer(h, mix_norm_w[layer], w_in[layer], lower_bounds[layer],
                             hg_norm_w[layer], gd_conv_w[layer], gd_a_log[layer],
                             gd_dt_bias[layer], gd_norm_w[layer], w_branch_a[layer],
                             w_branch_b[layer], w_out[layer])
        if layer == DEPTH - 1:
            h = h[:, N_META:]
        h = h + swiglu_ffn(rms_norm(h, ffn_norm_w[layer]), w_ffn_in[layer], w_ffn_out[layer])
    return rms_norm(h, final_norm_w)


import jax as _jax
import jax.numpy as _jnp

TWIN_FORMAT = 'train_step'
FWD_PARAMS = ['x', 'meta_tokens', 'lb_logits', 'mix_norm_w', 'w_in', 'hg_norm_w', 'gd_conv_w', 'gd_a_log', 'gd_dt_bias', 'gd_norm_w', 'w_branch_a', 'w_branch_b', 'w_out', 'ffn_norm_w', 'w_ffn_in', 'w_ffn_out', 'final_norm_w']
TWIN_WEIGHTS = ['meta_tokens', 'lb_logits', 'mix_norm_w', 'w_in', 'hg_norm_w', 'gd_conv_w', 'gd_a_log', 'gd_dt_bias', 'gd_norm_w', 'w_branch_a', 'w_branch_b', 'w_out', 'ffn_norm_w', 'w_ffn_in', 'w_ffn_out', 'final_norm_w']
TWIN_DIFF_INPUT = 'x'
TWIN_INPUTS = ['x', 'meta_tokens', 'lb_logits', 'mix_norm_w', 'w_in', 'hg_norm_w', 'gd_conv_w', 'gd_a_log', 'gd_dt_bias', 'gd_norm_w', 'w_branch_a', 'w_branch_b', 'w_out', 'ffn_norm_w', 'w_ffn_in', 'w_ffn_out', 'final_norm_w', 'loss_target', 'm_meta_tokens', 'm_lb_logits', 'm_mix_norm_w', 'm_w_in', 'm_hg_norm_w', 'm_gd_conv_w', 'm_gd_a_log', 'm_gd_dt_bias', 'm_gd_norm_w', 'm_w_branch_a', 'm_w_branch_b', 'm_w_out', 'm_ffn_norm_w', 'm_w_ffn_in', 'm_w_ffn_out', 'm_final_norm_w', 'v_meta_tokens', 'v_lb_logits', 'v_mix_norm_w', 'v_w_in', 'v_hg_norm_w', 'v_gd_conv_w', 'v_gd_a_log', 'v_gd_dt_bias', 'v_gd_norm_w', 'v_w_branch_a', 'v_w_branch_b', 'v_w_out', 'v_ffn_norm_w', 'v_w_ffn_in', 'v_w_ffn_out', 'v_final_norm_w']
TWIN_OUTPUTS = ['loss', 'grad_x', 'grad_meta_tokens', 'grad_lb_logits', 'grad_mix_norm_w', 'grad_w_in', 'grad_hg_norm_w', 'grad_gd_conv_w', 'grad_gd_a_log', 'grad_gd_dt_bias', 'grad_gd_norm_w', 'grad_w_branch_a', 'grad_w_branch_b', 'grad_w_out', 'grad_ffn_norm_w', 'grad_w_ffn_in', 'grad_w_ffn_out', 'grad_final_norm_w', 'delta_meta_tokens', 'delta_lb_logits', 'delta_mix_norm_w', 'delta_w_in', 'delta_hg_norm_w', 'delta_gd_conv_w', 'delta_gd_a_log', 'delta_gd_dt_bias', 'delta_gd_norm_w', 'delta_w_branch_a', 'delta_w_branch_b', 'delta_w_out', 'delta_ffn_norm_w', 'delta_w_ffn_in', 'delta_w_ffn_out', 'delta_final_norm_w', 'new_m_meta_tokens', 'new_m_lb_logits', 'new_m_mix_norm_w', 'new_m_w_in', 'new_m_hg_norm_w', 'new_m_gd_conv_w', 'new_m_gd_a_log', 'new_m_gd_dt_bias', 'new_m_gd_norm_w', 'new_m_w_branch_a', 'new_m_w_branch_b', 'new_m_w_out', 'new_m_ffn_norm_w', 'new_m_w_ffn_in', 'new_m_w_ffn_out', 'new_m_final_norm_w', 'new_v_meta_tokens', 'new_v_lb_logits', 'new_v_mix_norm_w', 'new_v_w_in', 'new_v_hg_norm_w', 'new_v_gd_conv_w', 'new_v_gd_a_log', 'new_v_gd_dt_bias', 'new_v_gd_norm_w', 'new_v_w_branch_a', 'new_v_w_branch_b', 'new_v_w_out', 'new_v_ffn_norm_w', 'new_v_w_ffn_in', 'new_v_w_ffn_out', 'new_v_final_norm_w']
TWIN_LEAF_KINDS = {'loss': 'loss', 'grad_x': 'grad_x', 'grad_meta_tokens': 'grad_w', 'grad_lb_logits': 'grad_w', 'grad_mix_norm_w': 'grad_w', 'grad_w_in': 'grad_w', 'grad_hg_norm_w': 'grad_w', 'grad_gd_conv_w': 'grad_w', 'grad_gd_a_log': 'grad_w', 'grad_gd_dt_bias': 'grad_w', 'grad_gd_norm_w': 'grad_w', 'grad_w_branch_a': 'grad_w', 'grad_w_branch_b': 'grad_w', 'grad_w_out': 'grad_w', 'grad_ffn_norm_w': 'grad_w', 'grad_w_ffn_in': 'grad_w', 'grad_w_ffn_out': 'grad_w', 'grad_final_norm_w': 'grad_w', 'delta_meta_tokens': 'delta_w', 'delta_lb_logits': 'delta_w', 'delta_mix_norm_w': 'delta_w', 'delta_w_in': 'delta_w', 'delta_hg_norm_w': 'delta_w', 'delta_gd_conv_w': 'delta_w', 'delta_gd_a_log': 'delta_w', 'delta_gd_dt_bias': 'delta_w', 'delta_gd_norm_w': 'delta_w', 'delta_w_branch_a': 'delta_w', 'delta_w_branch_b': 'delta_w', 'delta_w_out': 'delta_w', 'delta_ffn_norm_w': 'delta_w', 'delta_w_ffn_in': 'delta_w', 'delta_w_ffn_out': 'delta_w', 'delta_final_norm_w': 'delta_w', 'new_m_meta_tokens': 'new_m', 'new_m_lb_logits': 'new_m', 'new_m_mix_norm_w': 'new_m', 'new_m_w_in': 'new_m', 'new_m_hg_norm_w': 'new_m', 'new_m_gd_conv_w': 'new_m', 'new_m_gd_a_log': 'new_m', 'new_m_gd_dt_bias': 'new_m', 'new_m_gd_norm_w': 'new_m', 'new_m_w_branch_a': 'new_m', 'new_m_w_branch_b': 'new_m', 'new_m_w_out': 'new_m', 'new_m_ffn_norm_w': 'new_m', 'new_m_w_ffn_in': 'new_m', 'new_m_w_ffn_out': 'new_m', 'new_m_final_norm_w': 'new_m', 'new_v_meta_tokens': 'new_v', 'new_v_lb_logits': 'new_v', 'new_v_mix_norm_w': 'new_v', 'new_v_w_in': 'new_v', 'new_v_hg_norm_w': 'new_v', 'new_v_gd_conv_w': 'new_v', 'new_v_gd_a_log': 'new_v', 'new_v_gd_dt_bias': 'new_v', 'new_v_gd_norm_w': 'new_v', 'new_v_w_branch_a': 'new_v', 'new_v_w_branch_b': 'new_v', 'new_v_w_out': 'new_v', 'new_v_ffn_norm_w': 'new_v', 'new_v_w_ffn_in': 'new_v', 'new_v_w_ffn_out': 'new_v', 'new_v_final_norm_w': 'new_v'}


def _forward(args):
    return _fwd_reference(*[args[k] for k in FWD_PARAMS])


def _output_shape():
    def fwd():
        inp = _fwd_setup_inputs(0)
        return _fwd_reference(*[inp[k] for k in FWD_PARAMS])
    out = _jax.eval_shape(fwd)
    return out.shape, out.dtype

N_MICROBATCH = 1
ADAM_LR = 0.001
ADAM_B1 = 0.9
ADAM_B2 = 0.999
ADAM_EPS = 1e-08
ADAM_WD = 0.01
ADAM_STEP = 10
PER_EXAMPLE_BATCH_AXIS = {'x': 0, 'loss_target': 0}
SHARED_INPUTS = []
_WEIGHT_DTYPES = {'meta_tokens': _jnp.float32, 'lb_logits': _jnp.float32, 'mix_norm_w': _jnp.float32, 'w_in': _jnp.float32, 'hg_norm_w': _jnp.float32, 'gd_conv_w': _jnp.float32, 'gd_a_log': _jnp.float32, 'gd_dt_bias': _jnp.float32, 'gd_norm_w': _jnp.float32, 'w_branch_a': _jnp.float32, 'w_branch_b': _jnp.float32, 'w_out': _jnp.float32, 'ffn_norm_w': _jnp.float32, 'w_ffn_in': _jnp.float32, 'w_ffn_out': _jnp.float32, 'final_norm_w': _jnp.float32}
MOMENT_SCALE = {'meta_tokens': 2.586696e-03, 'lb_logits': 5.772843e-03, 'mix_norm_w': 9.565686e-02, 'w_in': 3.817961e-02, 'hg_norm_w': 1.528377e-01, 'gd_conv_w': 4.498980e-02, 'gd_a_log': 1.722899e-01, 'gd_dt_bias': 1.731305e-01, 'gd_norm_w': 1.672194e-01, 'w_branch_a': 4.118563e-02, 'w_branch_b': 4.080172e-02, 'w_out': 5.807554e-02, 'ffn_norm_w': 9.061973e-02, 'w_ffn_in': 3.673326e-02, 'w_ffn_out': 5.988993e-02, 'final_norm_w': 3.195965e+01}


def _to_microbatches(a, axis):
    t = _jnp.moveaxis(a, axis, 0)
    t = t.reshape((N_MICROBATCH, t.shape[0] // N_MICROBATCH) + t.shape[1:])
    return _jnp.moveaxis(t, 1, axis + 1)


def setup_inputs(seed: int = 0) -> dict:
    inp = _fwd_setup_inputs(seed)
    key = _jax.random.fold_in(_jax.random.key(seed), 7919)
    shape, _ = _output_shape()
    out = dict(inp)
    out["loss_target"] = _jax.random.normal(_jax.random.fold_in(key, 0), shape, _jnp.float32)
    for i, name in enumerate(TWIN_WEIGHTS):
        w = inp[name].astype(_jnp.float32)
        if MOMENT_SCALE is None:
            s = _jnp.sqrt(_jnp.mean(_jnp.square(w)) + 1e-30)
        else:
            s = MOMENT_SCALE[name]
        km, kv = _jax.random.split(_jax.random.fold_in(key, i + 1))
        out[name] = w
        out["m_" + name] = s * _jax.random.normal(km, w.shape, _jnp.float32)
        out["v_" + name] = (s * s) * _jax.random.uniform(kv, w.shape, _jnp.float32, 0.5, 1.5)
    if N_MICROBATCH > 1:
        for name, axis in PER_EXAMPLE_BATCH_AXIS.items():
            out[name] = _to_microbatches(out[name], axis)
    return {'x': out['x'], 'meta_tokens': out['meta_tokens'], 'lb_logits': out['lb_logits'], 'mix_norm_w': out['mix_norm_w'], 'w_in': out['w_in'], 'hg_norm_w': out['hg_norm_w'], 'gd_conv_w': out['gd_conv_w'], 'gd_a_log': out['gd_a_log'], 'gd_dt_bias': out['gd_dt_bias'], 'gd_norm_w': out['gd_norm_w'], 'w_branch_a': out['w_branch_a'], 'w_branch_b': out['w_branch_b'], 'w_out': out['w_out'], 'ffn_norm_w': out['ffn_norm_w'], 'w_ffn_in': out['w_ffn_in'], 'w_ffn_out': out['w_ffn_out'], 'final_norm_w': out['final_norm_w'], 'loss_target': out['loss_target'], 'm_meta_tokens': out['m_meta_tokens'], 'm_lb_logits': out['m_lb_logits'], 'm_mix_norm_w': out['m_mix_norm_w'], 'm_w_in': out['m_w_in'], 'm_hg_norm_w': out['m_hg_norm_w'], 'm_gd_conv_w': out['m_gd_conv_w'], 'm_gd_a_log': out['m_gd_a_log'], 'm_gd_dt_bias': out['m_gd_dt_bias'], 'm_gd_norm_w': out['m_gd_norm_w'], 'm_w_branch_a': out['m_w_branch_a'], 'm_w_branch_b': out['m_w_branch_b'], 'm_w_out': out['m_w_out'], 'm_ffn_norm_w': out['m_ffn_norm_w'], 'm_w_ffn_in': out['m_w_ffn_in'], 'm_w_ffn_out': out['m_w_ffn_out'], 'm_final_norm_w': out['m_final_norm_w'], 'v_meta_tokens': out['v_meta_tokens'], 'v_lb_logits': out['v_lb_logits'], 'v_mix_norm_w': out['v_mix_norm_w'], 'v_w_in': out['v_w_in'], 'v_hg_norm_w': out['v_hg_norm_w'], 'v_gd_conv_w': out['v_gd_conv_w'], 'v_gd_a_log': out['v_gd_a_log'], 'v_gd_dt_bias': out['v_gd_dt_bias'], 'v_gd_norm_w': out['v_gd_norm_w'], 'v_w_branch_a': out['v_w_branch_a'], 'v_w_branch_b': out['v_w_branch_b'], 'v_w_out': out['v_w_out'], 'v_ffn_norm_w': out['v_ffn_norm_w'], 'v_w_ffn_in': out['v_w_ffn_in'], 'v_w_ffn_out': out['v_w_ffn_out'], 'v_final_norm_w': out['v_final_norm_w']}


def _loss(weights, diff, rest, loss_target):
    with _jax.named_scope("forward"):
        args = {**rest, TWIN_DIFF_INPUT: diff, **{k: w.astype(_WEIGHT_DTYPES[k]) for k, w in weights.items()}}
        y = _forward(args)
    with _jax.named_scope("loss_head"):
        err = _jnp.square(y.astype(_jnp.float32) - loss_target)
        return 0.5 * _jnp.sum(_jnp.mean(err, axis=-1)) if err.ndim else 0.5 * err


def _adamw(w, g, m, v):
    m = ADAM_B1 * m + (1.0 - ADAM_B1) * g
    v = ADAM_B2 * v + (1.0 - ADAM_B2) * _jnp.square(g)
    m_hat = m / (1.0 - ADAM_B1 ** ADAM_STEP)
    v_hat = v / (1.0 - ADAM_B2 ** ADAM_STEP)
    delta = -ADAM_LR * (m_hat / (_jnp.sqrt(v_hat) + ADAM_EPS) + ADAM_WD * w)
    return delta, m, v


def reference(x, meta_tokens, lb_logits, mix_norm_w, w_in, hg_norm_w, gd_conv_w, gd_a_log, gd_dt_bias, gd_norm_w, w_branch_a, w_branch_b, w_out, ffn_norm_w, w_ffn_in, w_ffn_out, final_norm_w, loss_target, m_meta_tokens, m_lb_logits, m_mix_norm_w, m_w_in, m_hg_norm_w, m_gd_conv_w, m_gd_a_log, m_gd_dt_bias, m_gd_norm_w, m_w_branch_a, m_w_branch_b, m_w_out, m_ffn_norm_w, m_w_ffn_in, m_w_ffn_out, m_final_norm_w, v_meta_tokens, v_lb_logits, v_mix_norm_w, v_w_in, v_hg_norm_w, v_gd_conv_w, v_gd_a_log, v_gd_dt_bias, v_gd_norm_w, v_w_branch_a, v_w_branch_b, v_w_out, v_ffn_norm_w, v_w_ffn_in, v_w_ffn_out, v_final_norm_w):
    given = dict(x=x, meta_tokens=meta_tokens, lb_logits=lb_logits, mix_norm_w=mix_norm_w, w_in=w_in, hg_norm_w=hg_norm_w, gd_conv_w=gd_conv_w, gd_a_log=gd_a_log, gd_dt_bias=gd_dt_bias, gd_norm_w=gd_norm_w, w_branch_a=w_branch_a, w_branch_b=w_branch_b, w_out=w_out, ffn_norm_w=ffn_norm_w, w_ffn_in=w_ffn_in, w_ffn_out=w_ffn_out, final_norm_w=final_norm_w, loss_target=loss_target, m_meta_tokens=m_meta_tokens, m_lb_logits=m_lb_logits, m_mix_norm_w=m_mix_norm_w, m_w_in=m_w_in, m_hg_norm_w=m_hg_norm_w, m_gd_conv_w=m_gd_conv_w, m_gd_a_log=m_gd_a_log, m_gd_dt_bias=m_gd_dt_bias, m_gd_norm_w=m_gd_norm_w, m_w_branch_a=m_w_branch_a, m_w_branch_b=m_w_branch_b, m_w_out=m_w_out, m_ffn_norm_w=m_ffn_norm_w, m_w_ffn_in=m_w_ffn_in, m_w_ffn_out=m_w_ffn_out, m_final_norm_w=m_final_norm_w, v_meta_tokens=v_meta_tokens, v_lb_logits=v_lb_logits, v_mix_norm_w=v_mix_norm_w, v_w_in=v_w_in, v_hg_norm_w=v_hg_norm_w, v_gd_conv_w=v_gd_conv_w, v_gd_a_log=v_gd_a_log, v_gd_dt_bias=v_gd_dt_bias, v_gd_norm_w=v_gd_norm_w, v_w_branch_a=v_w_branch_a, v_w_branch_b=v_w_branch_b, v_w_out=v_w_out, v_ffn_norm_w=v_ffn_norm_w, v_w_ffn_in=v_w_ffn_in, v_w_ffn_out=v_w_ffn_out, v_final_norm_w=v_final_norm_w)
    weights = {n: given[n] for n in TWIN_WEIGHTS}
    shared = {n: given[n] for n in SHARED_INPUTS}
    per_example = {n: given[n] for n in ['x']}
    grad_fn = _jax.value_and_grad(_loss, argnums=(0, 1))

    def one_microbatch(ex, loss_target):
        ex = dict(ex)
        diff = ex.pop(TWIN_DIFF_INPUT)
        return grad_fn(weights, diff, {**shared, **ex}, loss_target)

    if N_MICROBATCH == 1:
        loss, (grad_w, grad_x) = one_microbatch(per_example, given["loss_target"])
    else:
        def body(carry, xs):
            loss_sum, grad_sum = carry
            l_k, (gw_k, gx_k) = one_microbatch(xs[0], xs[1])
            with _jax.named_scope("update"):
                return (loss_sum + l_k, _jax.tree.map(_jnp.add, grad_sum, gw_k)), gx_k

        init = (_jnp.zeros((), _jnp.float32), _jax.tree.map(_jnp.zeros_like, weights))
        (loss, grad_w), grad_x = _jax.lax.scan(body, init, (per_example, given["loss_target"]))
    with _jax.named_scope("update"):
        delta_w, new_m, new_v = {}, {}, {}
        for n in TWIN_WEIGHTS:
            delta_w[n], new_m[n], new_v[n] = _adamw(weights[n], grad_w[n], given["m_" + n], given["v_" + n])
    return (loss, grad_x, *[grad_w[n] for n in TWIN_WEIGHTS], *[delta_w[n] for n in TWIN_WEIGHTS],
            *[new_m[n] for n in TWIN_WEIGHTS], *[new_v[n] for n in TWIN_WEIGHTS])
```

```python
import functools

import jax
import jax.numpy as jnp
from jax import lax
from jax.experimental import pallas as pl
from jax.experimental.pallas import tpu as pltpu

F32 = jnp.float32
BF16 = jnp.bfloat16
HIGHEST = lax.Precision.HIGHEST
MESH = pl.DeviceIdType.MESH

EPS = 1e-6
D = 2048
N_META = 16
CHUNK = 64
SUB = 16
HEADS = 8
DH = 128
CONV_K = 4
TAIL = 8
D_FF = 5632
PRE = 512
N_DEV = 8
LANES = 128
FF_BLK = 512

ADAM_LR = 0.001
ADAM_B1 = 0.9
ADAM_B2 = 0.999
ADAM_EPS = 1e-08
ADAM_WD = 0.01
ADAM_STEP = 10

VMEM_LIMIT = 52 * 1024 * 1024

_NN = ((1,), (0,))
_NT = ((1,), (1,))
_TN = ((0,), (0,))


def _params(*sem):
    return pltpu.CompilerParams(dimension_semantics=sem, vmem_limit_bytes=VMEM_LIMIT)


def _raw_dot(a, b, dims, precise):
    if precise:
        return lax.dot_general(a, b, (dims, ((), ())), precision=HIGHEST, preferred_element_type=F32)
    return lax.dot_general(a.astype(BF16), b.astype(BF16), (dims, ((), ())), preferred_element_type=F32)


@functools.partial(jax.custom_vjp, nondiff_argnums=(2,))
def mm_nn(a, b, precise=False):
    return _raw_dot(a, b, _NN, precise)


@functools.partial(jax.custom_vjp, nondiff_argnums=(2,))
def mm_nt(a, b, precise=False):
    return _raw_dot(a, b, _NT, precise)


@functools.partial(jax.custom_vjp, nondiff_argnums=(2,))
def mm_tn(a, b, precise=False):
    return _raw_dot(a, b, _TN, precise)


mm_nn.defvjp(lambda a, b, p: (_raw_dot(a, b, _NN, p), (a, b)),
             lambda p, res, g: (mm_nt(g, res[1], p), mm_tn(res[0], g, p)))
mm_nt.defvjp(lambda a, b, p: (_raw_dot(a, b, _NT, p), (a, b)),
             lambda p, res, g: (mm_nn(g, res[1], p), mm_tn(g, res[0], p)))
mm_tn.defvjp(lambda a, b, p: (_raw_dot(a, b, _TN, p), (a, b)),
             lambda p, res, g: (mm_nt(res[1], g, p), mm_nn(res[0], g, p)))


def _sigmoid(x):
    return 1.0 / (1.0 + jnp.exp(-x))


def _silu(x):
    return x * _sigmoid(x)


def _softplus(x):
    return jnp.maximum(x, 0.0) + jnp.log(1.0 + jnp.exp(-jnp.abs(x)))


def _l2n(t):
    return t * lax.rsqrt(jnp.sum(t * t, axis=-1, keepdims=True) + EPS)


def _rms_norm(x, w):
    return x * lax.rsqrt(jnp.mean(x * x, axis=-1, keepdims=True) + EPS) * w


def _gated_norm(o, gate, nw):
    o = o * lax.rsqrt(jnp.mean(o * o, axis=-1, keepdims=True) + EPS) * nw
    return o * _silu(gate)


def hg_chunk(hq, hf, hi, hg, l0, l1, nw, st):
    m = jnp.maximum(l0, l1)
    e0 = jnp.exp(l0 - m)
    e1 = jnp.exp(l1 - m)
    lb = e0 / (e0 + e1)
    sig = _sigmoid(hf)
    q = _silu(hq)
    log_f = jnp.log(lb + (1.0 - lb) * sig)
    k = (1.0 - lb) * _sigmoid(-hf)
    v = hi
    rows = lax.broadcasted_iota(jnp.int32, (CHUNK, CHUNK), 0)
    cols = lax.broadcasted_iota(jnp.int32, (CHUNK, CHUNK), 1)
    b = mm_nn((rows >= cols).astype(F32), log_f, True)
    o_inter = mm_nt(q * jnp.exp(b), st)
    tri_s = (lax.broadcasted_iota(jnp.int32, (SUB, SUB, DH), 0)
             >= lax.broadcasted_iota(jnp.int32, (SUB, SUB, DH), 1))
    outs = []
    for i in range(CHUNK // SUB):
        lo = i * SUB
        b_i, q_i, k_i, v_i = b[lo:lo + SUB], q[lo:lo + SUB], k[lo:lo + SUB], v[lo:lo + SUB]
        diff = b_i[:, None, :] - b_i[None, :, :]
        dec = jnp.where(tri_s, jnp.exp(jnp.minimum(diff, 0.0)), 0.0)
        s_ii = jnp.sum(q_i[:, None, :] * k_i[None, :, :] * dec, axis=-1)
        o_i = mm_nn(s_ii, v_i)
        if i > 0:
            b_ref = jnp.sum(log_f[:lo], axis=0, keepdims=True)
            q_t = q_i * jnp.exp(b_i - b_ref)
            k_t = k[:lo] * jnp.exp(b_ref - b[:lo])
            o_i = o_i + mm_nn(mm_nt(q_t, k_t), v[:lo])
        outs.append(o_i)
    o = o_inter + jnp.concatenate(outs, axis=0)
    b_last = jnp.sum(log_f, axis=0, keepdims=True)
    st_new = st * jnp.exp(b_last) + mm_tn(v, k * jnp.exp(b_last - b))
    return _gated_norm(o, hg, nw), st_new


def hgrn2_fwd(proj, l0, l1, nw):
    lp = proj.shape[0]
    nc = lp // CHUNK

    def body(x_ref, l0_ref, l1_ref, nw_ref, o_ref, st_out_ref, st_ref):
        c, h = pl.program_id(0), pl.program_id(1)

        @pl.when(c == 0)
        def _():
            st_ref[h] = jnp.zeros((DH, DH), F32)

        st = st_ref[h]
        st_out_ref[0, 0] = st
        x = x_ref[...]
        o, st_new = hg_chunk(x[:, 0:DH], x[:, DH:2 * DH], x[:, 2 * DH:3 * DH], x[:, 3 * DH:4 * DH],
                             l0_ref[...], l1_ref[...], nw_ref[...], st)
        o_ref[...] = o.astype(o_ref.dtype)
        st_ref[h] = st_new

    vec = pl.BlockSpec((1, DH), lambda c, h: (0, h))
    return pl.pallas_call(
        body, name="hgrn2_fwd", grid=(nc, HEADS),
        in_specs=[pl.BlockSpec((CHUNK, 4 * DH), lambda c, h: (c, h)), vec, vec,
                  pl.BlockSpec((1, DH), lambda c, h: (0, 0))],
        out_specs=[pl.BlockSpec((CHUNK, DH), lambda c, h: (c, h)),
                   pl.BlockSpec((1, 1, DH, DH), lambda c, h: (c, h, 0, 0))],
        out_shape=[jax.ShapeDtypeStruct((lp, HEADS * DH), BF16),
                   jax.ShapeDtypeStruct((nc, HEADS, DH, DH), F32)],
        scratch_shapes=[pltpu.VMEM((HEADS, DH, DH), F32)],
        compiler_params=_params("arbitrary", "arbitrary"),
    )(proj, l0, l1, nw)


def hgrn2_bwd(proj, l0, l1, nw, states, do):
    lp = proj.shape[0]
    nc = lp // CHUNK

    def body(x_ref, l0_ref, l1_ref, nw_ref, st_in_ref, do_ref, dx_ref, dl0_ref, dl1_ref, dnw_ref, dst_ref):
        c, h = pl.program_id(0), pl.program_id(1)

        @pl.when(c == 0)
        def _():
            dst_ref[h] = jnp.zeros((DH, DH), F32)
            dl0_ref[h] = jnp.zeros((1, DH), F32)
            dl1_ref[h] = jnp.zeros((1, DH), F32)
            dnw_ref[h] = jnp.zeros((1, DH), F32)

        x = x_ref[...]
        args = (x[:, 0:DH], x[:, DH:2 * DH], x[:, 2 * DH:3 * DH], x[:, 3 * DH:4 * DH],
                l0_ref[...], l1_ref[...], nw_ref[...], st_in_ref[0, 0])
        _, vjp = jax.vjp(hg_chunk, *args)
        dq, df, di, dg, dl0, dl1, dnw, dst = vjp((do_ref[...], dst_ref[h]))
        dx_ref[:, 0:DH] = dq.astype(dx_ref.dtype)
        dx_ref[:, DH:2 * DH] = df.astype(dx_ref.dtype)
        dx_ref[:, 2 * DH:3 * DH] = di.astype(dx_ref.dtype)
        dx_ref[:, 3 * DH:4 * DH] = dg.astype(dx_ref.dtype)
        dl0_ref[h] += dl0
        dl1_ref[h] += dl1
        dnw_ref[h] += dnw
        dst_ref[h] = dst

    rc = lambda c: nc - 1 - c
    vec = pl.BlockSpec((1, DH), lambda c, h: (0, h))
    acc = pl.BlockSpec((HEADS, 1, DH), lambda c, h: (0, 0, 0))
    acc_shape = jax.ShapeDtypeStruct((HEADS, 1, DH), F32)
    return pl.pallas_call(
        body, name="hgrn2_bwd", grid=(nc, HEADS),
        in_specs=[pl.BlockSpec((CHUNK, 4 * DH), lambda c, h: (rc(c), h)), vec, vec,
                  pl.BlockSpec((1, DH), lambda c, h: (0, 0)),
                  pl.BlockSpec((1, 1, DH, DH), lambda c, h: (rc(c), h, 0, 0)),
                  pl.BlockSpec((CHUNK, DH), lambda c, h: (rc(c), h))],
        out_specs=[pl.BlockSpec((CHUNK, 4 * DH), lambda c, h: (rc(c), h)), acc, acc, acc],
        out_shape=[jax.ShapeDtypeStruct((lp, 4 * HEADS * DH), BF16), acc_shape, acc_shape, acc_shape],
        scratch_shapes=[pltpu.VMEM((HEADS, DH, DH), F32)],
        compiler_params=_params("arbitrary", "arbitrary"),
    )(proj, l0, l1, nw, states, do)


def _unit_lower_inverse(a):
    n = a.shape[0]
    eye = (lax.broadcasted_iota(jnp.int32, (n, n), 0) == lax.broadcasted_iota(jnp.int32, (n, n), 1)).astype(F32)
    x = eye - a
    p = mm_nn(a, a, True)
    x = x + mm_nn(x, p, True)
    power = 4
    while power < n:
        p = mm_nn(p, p, True)
        x = x + mm_nn(x, p, True)
        power *= 2
    return x


def gd_chunk(cq, ck, cv, z, ab, alog, dtb, nw, s, h):
    lane = lax.broadcasted_iota(jnp.int32, (1, DH), 1)
    pick = lambda t, l: jnp.sum(jnp.where(lane == l, t, 0.0), axis=1, keepdims=True)
    a_raw, b_raw = pick(ab, h), pick(ab, h + HEADS)
    beta = _sigmoid(b_raw)
    g = -jnp.exp(pick(alog, h)) * _softplus(a_raw + pick(dtb, h))
    q = _l2n(_silu(cq)) * (DH ** -0.5)
    k = _l2n(_silu(ck))
    v = _silu(cv)
    c = cq.shape[0]
    rows = lax.broadcasted_iota(jnp.int32, (c, c), 0)
    cols = lax.broadcasted_iota(jnp.int32, (c, c), 1)
    tril = (rows >= cols).astype(F32)
    gc_w = mm_nn(tril, jnp.broadcast_to(g, (c, DH)), True)
    g_sq = jnp.broadcast_to(g, (c, c))
    gc_col = mm_nn(tril, g_sq, True)
    gc_row = mm_nn(jnp.ones((c, c), F32), g_sq * (rows <= cols).astype(F32), True)
    rel = jnp.exp(jnp.minimum(gc_col - gc_row, 0.0))
    a = jnp.where(rows > cols, beta * mm_nt(k, k) * rel, 0.0)
    t_inv = _unit_lower_inverse(a)
    e_gc = jnp.exp(gc_w)
    w = mm_nn(t_inv, beta * e_gc * k, True)
    u = mm_nn(t_inv, beta * v, True)
    v_new = u - mm_nn(w, s)
    attn = jnp.where(rows >= cols, mm_nt(q, k) * rel, 0.0)
    o = mm_nn(q * e_gc, s) + mm_nn(attn, v_new)
    g_last = jnp.sum(g, axis=0, keepdims=True)
    s_new = jnp.exp(g_last) * s + mm_tn(k * jnp.exp(g_last - gc_w), v_new)
    return _gated_norm(o, z, nw), s_new


def _gd_conv(x_ref, tail_ref, cw_ref, xe_ref, first):
    tail = tail_ref[...][:, :3 * DH]
    xe_ref[0:TAIL, :] = jnp.where(first, 0.0, tail)
    xe_ref[TAIL:TAIL + CHUNK, :] = x_ref[:, 0:3 * DH]
    y = cw_ref[pl.ds(0, 1), :] * xe_ref[pl.ds(TAIL - CONV_K + 1, CHUNK), :]
    for k in range(1, CONV_K):
        y = y + cw_ref[pl.ds(k, 1), :] * xe_ref[pl.ds(TAIL - CONV_K + 1 + k, CHUNK), :]
    return y


def gdn_fwd(proj, proj_ab, cw, alog, dtb, nw):
    lp = proj.shape[0]
    nc = lp // CHUNK

    def body(x_ref, tail_ref, ab_ref, cw_ref, alog_ref, dtb_ref, nw_ref, o_ref, st_out_ref, st_ref, xe_ref):
        c, h = pl.program_id(0), pl.program_id(1)

        @pl.when(c == 0)
        def _():
            st_ref[h] = jnp.zeros((DH, DH), F32)

        st = st_ref[h]
        st_out_ref[0, 0] = st
        y = _gd_conv(x_ref, tail_ref, cw_ref, xe_ref, c == 0)
        o, st_new = gd_chunk(y[:, 0:DH], y[:, DH:2 * DH], y[:, 2 * DH:3 * DH], x_ref[:, 3 * DH:4 * DH],
                             ab_ref[...], alog_ref[...], dtb_ref[...], nw_ref[...], st, h)
        o_ref[...] = o.astype(o_ref.dtype)
        st_ref[h] = st_new

    one = pl.BlockSpec((1, DH), lambda c, h: (0, 0))
    return pl.pallas_call(
        body, name="gdn_fwd", grid=(nc, HEADS),
        in_specs=[pl.BlockSpec((CHUNK, 4 * DH), lambda c, h: (c, HEADS + h)),
                  pl.BlockSpec((TAIL, 4 * DH), lambda c, h: (jnp.maximum(c * (CHUNK // TAIL) - 1, 0), HEADS + h)),
                  pl.BlockSpec((CHUNK, DH), lambda c, h: (c, 0)),
                  pl.BlockSpec((CONV_K, 3 * DH), lambda c, h: (0, h)), one, one, one],
        out_specs=[pl.BlockSpec((CHUNK, DH), lambda c, h: (c, h)),
                   pl.BlockSpec((1, 1, DH, DH), lambda c, h: (c, h, 0, 0))],
        out_shape=[jax.ShapeDtypeStruct((lp, HEADS * DH), BF16),
                   jax.ShapeDtypeStruct((nc, HEADS, DH, DH), F32)],
        scratch_shapes=[pltpu.VMEM((HEADS, DH, DH), F32), pltpu.VMEM((TAIL + CHUNK, 3 * DH), F32)],
        compiler_params=_params("arbitrary", "arbitrary"),
    )(proj, proj, proj_ab, cw, alog, dtb, nw)


def gdn_bwd(proj, proj_ab, cw, alog, dtb, nw, states, do):
    lp = proj.shape[0]
    nc = lp // CHUNK

    def body(x_ref, tail_ref, ab_ref, cw_ref, alog_ref, dtb_ref, nw_ref, st_in_ref, do_ref,
             dx_ref, dab_ref, dcw_ref, dalog_ref, ddtb_ref, dnw_ref, dst_ref, xe_ref, dye_ref, head_ref):
        c, h = pl.program_id(0), pl.program_id(1)
        cc = nc - 1 - c

        @pl.when(jnp.logical_and(c == 0, h == 0))
        def _():
            dcw_ref[...] = jnp.zeros(dcw_ref.shape, F32)
            dalog_ref[...] = jnp.zeros(dalog_ref.shape, F32)
            ddtb_ref[...] = jnp.zeros(ddtb_ref.shape, F32)
            dnw_ref[...] = jnp.zeros(dnw_ref.shape, F32)

        @pl.when(c == 0)
        def _():
            dst_ref[h] = jnp.zeros((DH, DH), F32)
            head_ref[h] = jnp.zeros((TAIL, 3 * DH), F32)

        y = _gd_conv(x_ref, tail_ref, cw_ref, xe_ref, cc == 0)
        fn = functools.partial(gd_chunk, h=h)
        _, vjp = jax.vjp(fn, y[:, 0:DH], y[:, DH:2 * DH], y[:, 2 * DH:3 * DH], x_ref[:, 3 * DH:4 * DH],
                         ab_ref[...], alog_ref[...], dtb_ref[...], nw_ref[...], st_in_ref[0, 0])
        dcq, dck, dcv, dz, dab, dalog, ddtb, dnw, dst = vjp((do_ref[...], dst_ref[h]))
        dst_ref[h] = dst
        dalog_ref[...] += dalog
        ddtb_ref[...] += ddtb
        dnw_ref[...] += dnw

        @pl.when(h == 0)
        def _():
            dab_ref[...] = dab

        @pl.when(h > 0)
        def _():
            dab_ref[...] += dab

        dye_ref[0:CHUNK, 0:DH] = dcq
        dye_ref[0:CHUNK, DH:2 * DH] = dck
        dye_ref[0:CHUNK, 2 * DH:3 * DH] = dcv
        dye_ref[CHUNK:CHUNK + TAIL, :] = head_ref[h]
        head_ref[h] = dye_ref[0:TAIL, :]
        dy = dye_ref[0:CHUNK, :]
        dx = cw_ref[pl.ds(0, 1), :] * dye_ref[pl.ds(CONV_K - 1, CHUNK), :]
        dcw_ref[h, pl.ds(0, 1), :] += jnp.sum(dy * xe_ref[pl.ds(TAIL - CONV_K + 1, CHUNK), :], axis=0, keepdims=True)
        for k in range(1, CONV_K):
            dx = dx + cw_ref[pl.ds(k, 1), :] * dye_ref[pl.ds(CONV_K - 1 - k, CHUNK), :]
            dcw_ref[h, pl.ds(k, 1), :] += jnp.sum(dy * xe_ref[pl.ds(TAIL - CONV_K + 1 + k, CHUNK), :],
                                                  axis=0, keepdims=True)
        dx_ref[:, 0:3 * DH] = dx.astype(dx_ref.dtype)
        dx_ref[:, 3 * DH:4 * DH] = dz.astype(dx_ref.dtype)

    rc = lambda c: nc - 1 - c
    one = pl.BlockSpec((1, DH), lambda c, h: (0, 0))
    one_shape = jax.ShapeDtypeStruct((1, DH), F32)
    return pl.pallas_call(
        body, name="gdn_bwd", grid=(nc, HEADS),
        in_specs=[pl.BlockSpec((CHUNK, 4 * DH), lambda c, h: (rc(c), HEADS + h)),
                  pl.BlockSpec((TAIL, 4 * DH),
                               lambda c, h: (jnp.maximum(rc(c) * (CHUNK // TAIL) - 1, 0), HEADS + h)),
                  pl.BlockSpec((CHUNK, DH), lambda c, h: (rc(c), 0)),
                  pl.BlockSpec((CONV_K, 3 * DH), lambda c, h: (0, h)), one, one, one,
                  pl.BlockSpec((1, 1, DH, DH), lambda c, h: (rc(c), h, 0, 0)),
                  pl.BlockSpec((CHUNK, DH), lambda c, h: (rc(c), h))],
        out_specs=[pl.BlockSpec((CHUNK, 4 * DH), lambda c, h: (rc(c), h)),
                   pl.BlockSpec((CHUNK, DH), lambda c, h: (rc(c), 0)),
                   pl.BlockSpec((HEADS, CONV_K, 3 * DH), lambda c, h: (0, 0, 0)), one, one, one],
        out_shape=[jax.ShapeDtypeStruct((lp, 4 * HEADS * DH), BF16), jax.ShapeDtypeStruct((lp, DH), F32),
                   jax.ShapeDtypeStruct((HEADS, CONV_K, 3 * DH), F32), one_shape, one_shape, one_shape],
        scratch_shapes=[pltpu.VMEM((HEADS, DH, DH), F32), pltpu.VMEM((TAIL + CHUNK, 3 * DH), F32),
                        pltpu.VMEM((CHUNK + TAIL, 3 * DH), F32), pltpu.VMEM((HEADS, TAIL, 3 * DH), F32)],
        compiler_params=_params("arbitrary", "arbitrary"),
    )(proj, proj, proj_ab, cw, alog, dtb, nw, states, do)


def matmul(name, a, b, mode, m, n, k, out_dtype, tm=512, tn=1024, tk=None, a_off=(0, 0), b_off=(0, 0)):
    tm, tn = min(tm, m), min(tn, n)
    tk = k if tk is None else min(tk, k)
    assert m % tm == 0 and n % tn == 0 and k % tk == 0, (name, m, n, k, tm, tn, tk)
    gi, gj, gk = m // tm, n // tn, k // tk
    if mode == "nn":
        a_spec = pl.BlockSpec((tm, tk), lambda j, i, kk: (i + a_off[0], kk + a_off[1]))
        b_spec = pl.BlockSpec((tk, tn), lambda j, i, kk: (kk + b_off[0], j + b_off[1]))
        dims = _NN
    elif mode == "nt":
        a_spec = pl.BlockSpec((tm, tk), lambda j, i, kk: (i + a_off[0], kk + a_off[1]))
        b_spec = pl.BlockSpec((tn, tk), lambda j, i, kk: (j + b_off[0], kk + b_off[1]))
        dims = _NT
    else:
        a_spec = pl.BlockSpec((tk, tm), lambda j, i, kk: (kk + a_off[0], i + a_off[1]))
        b_spec = pl.BlockSpec((tk, tn), lambda j, i, kk: (kk + b_off[0], j + b_off[1]))
        dims = _TN

    def body(a_ref, b_ref, o_ref, *acc):
        d = lax.dot_general(a_ref[...].astype(BF16), b_ref[...].astype(BF16), (dims, ((), ())),
                            preferred_element_type=F32)
        if gk == 1:
            o_ref[...] = d.astype(o_ref.dtype)
        else:
            acc_ref, = acc
            kk = pl.program_id(2)

            @pl.when(kk == 0)
            def _():
                acc_ref[...] = d

            @pl.when(kk > 0)
            def _():
                acc_ref[...] += d

            @pl.when(kk == gk - 1)
            def _():
                o_ref[...] = acc_ref[...].astype(o_ref.dtype)

    return pl.pallas_call(
        body, name=name, grid=(gj, gi, gk),
        in_specs=[a_spec, b_spec],
        out_specs=pl.BlockSpec((tm, tn), lambda j, i, kk: (i, j)),
        out_shape=jax.ShapeDtypeStruct((m, n), out_dtype),
        scratch_shapes=[] if gk == 1 else [pltpu.VMEM((tm, tn), F32)],
        compiler_params=_params("parallel", "parallel", "arbitrary"),
    )(a, b)


TR = 256
N_PRE = PRE // TR


def _row(width, off=0, col=0):
    return pl.BlockSpec((TR, width), lambda i: (i + off, col))


def _pre_spec():
    return pl.BlockSpec((TR, D), lambda i: (jnp.minimum(i, N_PRE - 1), 0))


def _seq_spec(width=D, col=0):
    return pl.BlockSpec((TR, width), lambda i: (jnp.maximum(i - N_PRE, 0), col))


def _vec_spec(width=D):
    return pl.BlockSpec((1, width), lambda i: (0, 0))


def norm1_fwd(pre, x, w):
    lp = PRE + x.shape[0]

    def body(pre_ref, x_ref, w_ref, o_ref):
        h = jnp.where(pl.program_id(0) < N_PRE, pre_ref[...], x_ref[...])
        o_ref[...] = _rms_norm(h, w_ref[...]).astype(o_ref.dtype)

    return pl.pallas_call(
        body, name="norm1_fwd", grid=(lp // TR,),
        in_specs=[_pre_spec(), _seq_spec(), _vec_spec()],
        out_specs=_row(D), out_shape=jax.ShapeDtypeStruct((lp, D), BF16),
        compiler_params=_params("parallel"),
    )(pre, x, w)


def norm1_bwd(pre, x, w, dxn, dxn_ab, dh1):
    l = x.shape[0]
    lp = PRE + l

    def body(pre_ref, x_ref, w_ref, dxn_ref, dxn_ab_ref, dh1_ref, dx_ref, dpre_ref, dw_ref):
        i = pl.program_id(0)
        h = jnp.where(i < N_PRE, pre_ref[...], x_ref[...])
        _, vjp = jax.vjp(_rms_norm, h, w_ref[...])
        dh, dw = vjp(dxn_ref[...] + dxn_ab_ref[...])

        @pl.when(i == 0)
        def _():
            dw_ref[...] = dw

        @pl.when(i > 0)
        def _():
            dw_ref[...] += dw

        @pl.when(i < N_PRE)
        def _():
            dpre_ref[...] = dh

        @pl.when(i >= N_PRE)
        def _():
            dx_ref[...] = dh + dh1_ref[...]

    return pl.pallas_call(
        body, name="norm1_bwd", grid=(lp // TR,),
        in_specs=[_pre_spec(), _seq_spec(), _vec_spec(), _row(D), _row(D), _seq_spec()],
        out_specs=[_seq_spec(), _pre_spec(), _vec_spec()],
        out_shape=[jax.ShapeDtypeStruct((l, D), F32), jax.ShapeDtypeStruct((PRE, D), F32),
                   jax.ShapeDtypeStruct((1, D), F32)],
        compiler_params=_params("arbitrary"),
    )(pre, x, w, dxn, dxn_ab, dh1)


def _merge(gates, ya, yb):
    return _sigmoid(gates[:, :D]) * ya + _sigmoid(gates[:, D:]) * yb


def merge_fwd(proj, ya, yb):
    l = ya.shape[0]

    def body(g_ref, ya_ref, yb_ref, o_ref):
        o_ref[...] = _merge(g_ref[...], ya_ref[...], yb_ref[...]).astype(o_ref.dtype)

    return pl.pallas_call(
        body, name="merge_fwd", grid=(l // TR,),
        in_specs=[_row(2 * D, N_PRE, 2), _row(D), _row(D)],
        out_specs=_row(D), out_shape=jax.ShapeDtypeStruct((l, D), BF16),
        compiler_params=_params("parallel"),
    )(proj, ya, yb)


def merge_bwd(proj, ya, yb, dmerged):
    l = ya.shape[0]
    lp = PRE + l

    def body(g_ref, ya_ref, yb_ref, dm_ref, dg_ref, dya_ref, dyb_ref):
        i = pl.program_id(0)

        @pl.when(i < N_PRE)
        def _():
            dg_ref[...] = jnp.zeros(dg_ref.shape, dg_ref.dtype)
            dya_ref[...] = jnp.zeros(dya_ref.shape, dya_ref.dtype)
            dyb_ref[...] = jnp.zeros(dyb_ref.shape, dyb_ref.dtype)

        @pl.when(i >= N_PRE)
        def _():
            _, vjp = jax.vjp(_merge, g_ref[...], ya_ref[...], yb_ref[...])
            dg, dya, dyb = vjp(dm_ref[...])
            dg_ref[...] = dg.astype(dg_ref.dtype)
            dya_ref[...] = dya.astype(dya_ref.dtype)
            dyb_ref[...] = dyb.astype(dyb_ref.dtype)

    return pl.pallas_call(
        body, name="merge_bwd", grid=(lp // TR,),
        in_specs=[pl.BlockSpec((TR, 2 * D), lambda i: (jnp.maximum(i, N_PRE), 2)), _seq_spec(), _seq_spec(),
                  _seq_spec()],
        out_specs=[_row(2 * D), _row(D), _row(D)],
        out_shape=[jax.ShapeDtypeStruct((lp, 2 * D), BF16), jax.ShapeDtypeStruct((lp, D), BF16),
                   jax.ShapeDtypeStruct((lp, D), BF16)],
        compiler_params=_params("parallel"),
    )(proj, ya, yb, dmerged)


def norm2_fwd(x, mo, w):
    l = x.shape[0]

    def body(x_ref, mo_ref, w_ref, h_ref, o_ref):
        h = x_ref[...] + mo_ref[...]
        h_ref[...] = h
        o_ref[...] = _rms_norm(h, w_ref[...]).astype(o_ref.dtype)

    return pl.pallas_call(
        body, name="norm2_fwd", grid=(l // TR,),
        in_specs=[_row(D), _row(D), _vec_spec()],
        out_specs=[_row(D), _row(D)],
        out_shape=[jax.ShapeDtypeStruct((l, D), F32), jax.ShapeDtypeStruct((l, D), BF16)],
        compiler_params=_params("parallel"),
    )(x, mo, w)


def norm2_bwd(h1, w, dxn2, dh2):
    l = h1.shape[0]

    def body(h_ref, w_ref, dxn_ref, dh2_ref, dh1_ref, dw_ref):
        i = pl.program_id(0)
        _, vjp = jax.vjp(_rms_norm, h_ref[...], w_ref[...])
        dh, dw = vjp(dxn_ref[...])
        dh1_ref[...] = dh + dh2_ref[...]

        @pl.when(i == 0)
        def _():
            dw_ref[...] = dw

        @pl.when(i > 0)
        def _():
            dw_ref[...] += dw

    return pl.pallas_call(
        body, name="norm2_bwd", grid=(l // TR,),
        in_specs=[_row(D), _vec_spec(), _row(D), _row(D)],
        out_specs=[_row(D), _vec_spec()],
        out_shape=[jax.ShapeDtypeStruct((l, D), F32), jax.ShapeDtypeStruct((1, D), F32)],
        compiler_params=_params("arbitrary"),
    )(h1, w, dxn2, dh2)


def _swiglu(gu):
    return _silu(gu[:, :FF_BLK]) * gu[:, FF_BLK:]


def swiglu_fwd(gu):
    l = gu.shape[0]

    def body(gu_ref, o_ref):
        o_ref[...] = _swiglu(gu_ref[...]).astype(o_ref.dtype)

    return pl.pallas_call(
        body, name="swiglu_fwd", grid=(l // TR, D_FF // FF_BLK),
        in_specs=[pl.BlockSpec((TR, 2 * FF_BLK), lambda i, j: (i, j))],
        out_specs=pl.BlockSpec((TR, FF_BLK), lambda i, j: (i, j)),
        out_shape=jax.ShapeDtypeStruct((l, D_FF), BF16),
        compiler_params=_params("parallel", "parallel"),
    )(gu)


def swiglu_bwd(gu, dact):
    l = gu.shape[0]

    def body(gu_ref, da_ref, o_ref):
        _, vjp = jax.vjp(_swiglu, gu_ref[...])
        dgu, = vjp(da_ref[...])
        o_ref[...] = dgu.astype(o_ref.dtype)

    return pl.pallas_call(
        body, name="swiglu_bwd", grid=(l // TR, D_FF // FF_BLK),
        in_specs=[pl.BlockSpec((TR, 2 * FF_BLK), lambda i, j: (i, j)),
                  pl.BlockSpec((TR, FF_BLK), lambda i, j: (i, j))],
        out_specs=pl.BlockSpec((TR, 2 * FF_BLK), lambda i, j: (i, j)),
        out_shape=jax.ShapeDtypeStruct((l, 2 * D_FF), BF16),
        compiler_params=_params("parallel", "parallel"),
    )(gu, dact)


def _loss_rows(h1, f, w, tgt):
    y = _rms_norm(h1 + f, w)
    err = (y - tgt) * (y - tgt)
    return 0.5 * jnp.sum(jnp.mean(err, axis=-1, keepdims=True), axis=0, keepdims=True)


def loss_head(h1, f, w, tgt):
    l = h1.shape[0]

    def body(h_ref, f_ref, w_ref, t_ref, loss_ref, dh_ref, dw_ref):
        i = pl.program_id(0)
        loss, vjp = jax.vjp(_loss_rows, h_ref[...], f_ref[...], w_ref[...], t_ref[...])
        dh, _, dw, _ = vjp(jnp.ones((1, 1), F32))
        dh_ref[...] = dh
        loss = jnp.broadcast_to(loss, (1, LANES))

        @pl.when(i == 0)
        def _():
            dw_ref[...] = dw
            loss_ref[...] = loss

        @pl.when(i > 0)
        def _():
            dw_ref[...] += dw
            loss_ref[...] += loss

    return pl.pallas_call(
        body, name="loss_head", grid=(l // TR,),
        in_specs=[_row(D), _row(D), _vec_spec(), _row(D)],
        out_specs=[_vec_spec(LANES), _row(D), _vec_spec()],
        out_shape=[jax.ShapeDtypeStruct((1, LANES), F32), jax.ShapeDtypeStruct((l, D), F32),
                   jax.ShapeDtypeStruct((1, D), F32)],
        compiler_params=_params("arbitrary"),
    )(h1, f, w, tgt)


def _adamw(w, g, m, v):
    m = ADAM_B1 * m + (1.0 - ADAM_B1) * g
    v = ADAM_B2 * v + (1.0 - ADAM_B2) * (g * g)
    m_hat = m / (1.0 - ADAM_B1 ** ADAM_STEP)
    v_hat = v / (1.0 - ADAM_B2 ** ADAM_STEP)
    delta = -ADAM_LR * (m_hat / (jnp.sqrt(v_hat) + ADAM_EPS) + ADAM_WD * w)
    return delta, m, v


def adamw_shard(name, w, m, v, own, recv, tr):
    r, c = w.shape
    assert r % tr == 0

    def body(w_ref, m_ref, v_ref, own_ref, r0_ref, r1_ref, r2_ref, g_ref, d_ref, nm_ref, nv_ref):
        g = own_ref[...].astype(F32) + r0_ref[...].astype(F32)
        g = g + r1_ref[...].astype(F32)
        g = g + r2_ref[...].astype(F32)
        d, nm, nv = _adamw(w_ref[...], g, m_ref[...], v_ref[...])
        g_ref[...] = g
        d_ref[...] = d
        nm_ref[...] = nm
        nv_ref[...] = nv

    blk = pl.BlockSpec((tr, c), lambda i: (i, 0))
    part = lambda s: pl.BlockSpec((None, tr, c), lambda i: (s, i, 0))
    shape = jax.ShapeDtypeStruct((r, c), F32)
    return pl.pallas_call(
        body, name=name, grid=(r // tr,),
        in_specs=[blk, blk, blk, blk, part(0), part(1), part(2)],
        out_specs=[blk, blk, blk, blk], out_shape=[shape, shape, shape, shape],
        compiler_params=_params("parallel"),
    )(w, m, v, own, recv, recv, recv)


def adamw_small(w, g, m, v):
    def body(w_ref, g_ref, m_ref, v_ref, d_ref, nm_ref, nv_ref):
        d, nm, nv = _adamw(w_ref[...], g_ref[...], m_ref[...], v_ref[...])
        d_ref[...] = d
        nm_ref[...] = nm
        nv_ref[...] = nv

    shape = jax.ShapeDtypeStruct(w.shape, F32)
    return pl.pallas_call(body, name="adamw_small", out_shape=[shape, shape, shape])(w, g, m, v)


def pair_add(name, a, b, tr):
    q, r, c = a.shape
    assert r % tr == 0

    def body(a_ref, b_ref, o_ref):
        o_ref[...] = (a_ref[...].astype(F32) + b_ref[...].astype(F32)).astype(o_ref.dtype)

    blk = pl.BlockSpec((None, tr, c), lambda s, i: (s, i, 0))
    return pl.pallas_call(
        body, name=name, grid=(q, r // tr), in_specs=[blk, blk], out_specs=blk,
        out_shape=jax.ShapeDtypeStruct((q, r, c), BF16), compiler_params=_params("parallel", "parallel"),
    )(a, b)


def _place():
    return lax.axis_index("x"), lax.axis_index("y"), lax.axis_index("c")


def _dev(px, py, pc):
    return 4 * px + 2 * py + pc


def gather_blocks(arrays):
    n = len(arrays)
    any_spec = pl.BlockSpec(memory_space=pl.ANY)

    def body(*refs):
        ins, outs = refs[:n], refs[n:2 * n]
        send_sems, recv_sems, local_sems = refs[2 * n:]
        x, y, c = _place()
        me, sibling = (x, y, c), (x, y, 1 - c)
        chips = [(1 - x, y), (x, 1 - y), (1 - x, 1 - y)]

        def copy(a, k, block, to, src=None):
            dst = outs[a].at[_dev(*block)]
            return pltpu.make_async_remote_copy(
                src_ref=dst if src is None else src, dst_ref=dst, send_sem=send_sems.at[a, k],
                recv_sem=recv_sems.at[a, k], device_id=to, device_id_type=MESH)

        mine = [pltpu.make_async_copy(ins[a], outs[a].at[_dev(*me)], local_sems.at[a]) for a in range(n)]
        for cp in mine:
            cp.start()
        first = []
        for a in range(n):
            first.append(copy(a, 0, me, sibling, src=ins[a]))
            first += [copy(a, 1 + j, me, (*chip, c), src=ins[a]) for j, chip in enumerate(chips)]
        for cp in first:
            cp.start()
        passed = []
        for j, chip in enumerate(chips):
            for a in range(n):
                copy(a, 1 + j, (*chip, c), me).wait_recv()
                fwd = copy(a, 4 + j, (*chip, c), sibling)
                fwd.start()
                passed.append(fwd)
        for a in range(n):
            copy(a, 0, sibling, me).wait_recv()
        for j, chip in enumerate(chips):
            for a in range(n):
                copy(a, 4 + j, (*chip, 1 - c), me).wait_recv()
        for cp in first + passed:
            cp.wait_send()
        for cp in mine:
            cp.wait()

    return pl.pallas_call(
        body, name="gather_blocks",
        in_specs=[any_spec] * n, out_specs=[any_spec] * n,
        out_shape=[jax.ShapeDtypeStruct((N_DEV,) + t.shape, t.dtype) for t in arrays],
        scratch_shapes=[pltpu.SemaphoreType.DMA((n, 7)), pltpu.SemaphoreType.DMA((n, 7)),
                        pltpu.SemaphoreType.DMA((n,))],
    )(*arrays)


def exchange_core(arrays):
    n = len(arrays)
    any_spec = pl.BlockSpec(memory_space=pl.ANY)

    def body(*refs):
        ins, outs = refs[:n], refs[n:2 * n]
        send_sems, recv_sems = refs[2 * n:]
        x, y, c = _place()
        copies = []
        for a in range(n):
            for q in range(4):
                copies.append(pltpu.make_async_remote_copy(
                    src_ref=ins[a].at[q, 1 - c], dst_ref=outs[a].at[q], send_sem=send_sems.at[a, q],
                    recv_sem=recv_sems.at[a, q], device_id=(x, y, 1 - c), device_id_type=MESH))
        for cp in copies:
            cp.start()
        for cp in copies:
            cp.wait()

    return pl.pallas_call(
        body, name="exchange_core",
        in_specs=[any_spec] * n, out_specs=[any_spec] * n,
        out_shape=[jax.ShapeDtypeStruct((4,) + t.shape[2:], t.dtype) for t in arrays],
        scratch_shapes=[pltpu.SemaphoreType.DMA((n, 4)), pltpu.SemaphoreType.DMA((n, 4))],
    )(*arrays)


def exchange_chips(arrays):
    n = len(arrays)
    any_spec = pl.BlockSpec(memory_space=pl.ANY)

    def body(*refs):
        ins, outs = refs[:n], refs[n:2 * n]
        send_sems, recv_sems = refs[2 * n:]
        x, y, c = _place()
        chips = [(1 - x, y), (x, 1 - y), (1 - x, 1 - y)]
        copies = []
        for a in range(n):
            for j, (qx, qy) in enumerate(chips):
                copies.append(pltpu.make_async_remote_copy(
                    src_ref=ins[a].at[2 * qx + qy], dst_ref=outs[a].at[j], send_sem=send_sems.at[a, j],
                    recv_sem=recv_sems.at[a, j], device_id=(qx, qy, c), device_id_type=MESH))
        for cp in copies:
            cp.start()
        for cp in copies:
            cp.wait()

    return pl.pallas_call(
        body, name="exchange_chips",
        in_specs=[any_spec] * n, out_specs=[any_spec] * n,
        out_shape=[jax.ShapeDtypeStruct((3,) + t.shape[1:], t.dtype) for t in arrays],
        scratch_shapes=[pltpu.SemaphoreType.DMA((n, 3)), pltpu.SemaphoreType.DMA((n, 3))],
    )(*arrays)


def allreduce_small(buf):
    r = buf.shape[0]
    vmem = pl.BlockSpec(memory_space=pltpu.VMEM)

    def body(buf_ref, out_ref, all_ref, send_sems, recv_sems):
        x, y, c = _place()
        me = _dev(x, y, c)
        copies = []
        for k in range(1, N_DEV):
            peer = (x ^ (k >> 2), y ^ ((k >> 1) & 1), c ^ (k & 1))
            copies.append(pltpu.make_async_remote_copy(
                src_ref=buf_ref, dst_ref=all_ref.at[me], send_sem=send_sems.at[k - 1], recv_sem=recv_sems.at[k - 1],
                device_id=peer, device_id_type=MESH))
        for cp in copies:
            cp.start()
        all_ref[me] = buf_ref[...]
        for cp in copies:
            cp.wait()
        total = all_ref[0]
        for d in range(1, N_DEV):
            total = total + all_ref[d]
        out_ref[...] = total

    return pl.pallas_call(
        body, name="allreduce_small", in_specs=[vmem], out_specs=vmem,
        out_shape=jax.ShapeDtypeStruct((r, LANES), F32),
        scratch_shapes=[pltpu.VMEM((N_DEV, r, LANES), F32), pltpu.SemaphoreType.DMA((N_DEV - 1,)),
                        pltpu.SemaphoreType.DMA((N_DEV - 1,))],
    )(buf)


def _head_major(t, groups):
    r = t.shape[0]
    return jnp.transpose(t.reshape(r, groups, HEADS, DH), (0, 2, 1, 3)).reshape(r, groups * HEADS * DH)


def _group_major(t, groups):
    r = t.shape[0]
    return jnp.transpose(t.reshape(r, HEADS, groups, DH), (0, 2, 1, 3)).reshape(r, groups * HEADS * DH)


def _cols_from_shards(g):
    return jnp.transpose(g, (1, 0, 2)).reshape(g.shape[1], N_DEV * g.shape[2])


def _cols_to_shards(t):
    r, c = t.shape
    return jnp.transpose(t.reshape(r, N_DEV, c // N_DEV), (1, 0, 2))


def _interleave_ff(t):
    r = t.shape[0]
    nb = D_FF // FF_BLK
    return jnp.transpose(t.reshape(r, 2, nb, FF_BLK), (0, 2, 1, 3)).reshape(r, 2 * D_FF)


def _deinterleave_ff(t):
    r = t.shape[0]
    nb = D_FF // FF_BLK
    return jnp.transpose(t.reshape(r, nb, 2, FF_BLK), (0, 2, 1, 3)).reshape(r, 2 * D_FF)


def _rows128(t):
    return t.reshape(-1, LANES)


def _pad_lanes(t):
    t = t.reshape(1, -1)
    return jnp.pad(t, ((0, 0), (0, LANES - t.shape[1])))


def kernel(x, meta_tokens, lb_logits, mix_norm_w, w_in, hg_norm_w, gd_conv_w, gd_a_log, gd_dt_bias, gd_norm_w, w_branch_a, w_branch_b, w_out, ffn_norm_w, w_ffn_in, w_ffn_out, final_norm_w, loss_target, m_meta_tokens, m_lb_logits, m_mix_norm_w, m_w_in, m_hg_norm_w, m_gd_conv_w, m_gd_a_log, m_gd_dt_bias, m_gd_norm_w, m_w_branch_a, m_w_branch_b, m_w_out, m_ffn_norm_w, m_w_ffn_in, m_w_ffn_out, m_final_norm_w, v_meta_tokens, v_lb_logits, v_mix_norm_w, v_w_in, v_hg_norm_w, v_gd_conv_w, v_gd_a_log, v_gd_dt_bias, v_gd_norm_w, v_w_branch_a, v_w_branch_b, v_w_out, v_ffn_norm_w, v_w_ffn_in, v_w_ffn_out, v_final_norm_w):
    xs = x[0]
    tgt = loss_target[0]
    l = xs.shape[0]
    lp = PRE + l
    cx, cy, cc = _place()
    me = _dev(cx, cy, cc)
    my_chip = 2 * cx + cy

    big = [w_in[0], w_branch_a[0], w_branch_b[0], w_out[0], w_ffn_in[0], w_ffn_out[0]]
    gathered = gather_blocks([t.astype(BF16) for t in big] + [meta_tokens, gd_conv_w[0]])
    g_in, g_ba, g_bb, g_out, g_ffi, g_ffo, g_meta, g_conv = gathered
    w_in_full = _cols_from_shards(g_in)
    w_main = jnp.concatenate([_head_major(w_in_full[:, 0:4096], 4), _head_major(w_in_full[:, 4096:8192], 4),
                              w_in_full[:, 8208:]], axis=1)
    w_ab = jnp.pad(w_in_full[:, 8192:8208], ((0, 0), (0, LANES - 2 * HEADS)))
    wa = _cols_from_shards(g_ba)
    wb = _cols_from_shards(g_bb)
    wo = g_out.reshape(D, D)
    w_gu = _interleave_ff(_cols_from_shards(g_ffi))
    w_fo = g_ffo.reshape(D_FF, D)
    meta_full = _cols_from_shards(g_meta)
    conv_full = _cols_from_shards(g_conv)
    cw = _head_major(conv_full, 3)
    pre = jnp.concatenate([jnp.zeros((PRE - N_META, D), F32), meta_full], axis=0)
    l0, l1 = lb_logits[0:1], lb_logits[1:2]
    alog, dtb = _pad_lanes(gd_a_log), _pad_lanes(gd_dt_bias)

    xn = norm1_fwd(pre, xs, mix_norm_w)
    proj = matmul("proj_main", xn, w_main, "nn", lp, 12288, D, F32)
    proj_ab = matmul("proj_ab", xn, w_ab, "nn", lp, LANES, D, F32)
    o_a, st_a = hgrn2_fwd(proj, l0, l1, hg_norm_w)
    o_b, st_b = gdn_fwd(proj, proj_ab, cw, alog, dtb, gd_norm_w)
    ya = matmul("branch_a", o_a, wa, "nn", l, D, HEADS * DH, F32, a_off=(PRE // 512, 0))
    yb = matmul("branch_b", o_b, wb, "nn", l, D, HEADS * DH, F32, a_off=(PRE // 512, 0))
    merged = merge_fwd(proj, ya, yb)
    mo = matmul("mix_out", merged, wo, "nn", l, D, D, F32)
    h1, xn2 = norm2_fwd(xs, mo, ffn_norm_w)
    gu = matmul("ffn_in", xn2, w_gu, "nn", l, 2 * D_FF, D, F32)
    act = swiglu_fwd(gu)
    f = matmul("ffn_out", act, w_fo, "nn", l, D, D_FF, F32, tk=D_FF // 2)
    loss_part, dh2, d_final_w = loss_head(h1, f, final_norm_w.reshape(1, D), tgt)

    dact = matmul("d_act", dh2, w_fo, "nt", l, D_FF, D, F32, tn=512)
    dw_fo = matmul("dw_ffn_out", act, dh2, "tn", D_FF, D, l, BF16, tm=512, tn=1024, tk=512)
    dgu = swiglu_bwd(gu, dact)
    dxn2 = matmul("d_xn2", dgu, w_gu, "nt", l, D, 2 * D_FF, F32, tk=1024)
    dw_gu = matmul("dw_ffn_in", xn2, dgu, "tn", D, 2 * D_FF, l, BF16, tm=1024, tn=1024, tk=512)
    dh1, d_ffn_norm_w = norm2_bwd(h1, ffn_norm_w, dxn2, dh2)
    dmerged = matmul("d_merged", dh1, wo, "nt", l, D, D, F32)
    dw_o = matmul("dw_out", merged, dh1, "tn", D, D, l, BF16, tm=1024, tn=1024, tk=512)
    dgates, dya, dyb = merge_bwd(proj, ya, yb, dmerged)
    do_a = matmul("d_o_a", dya, wa, "nt", lp, HEADS * DH, D, F32)
    do_b = matmul("d_o_b", dyb, wb, "nt", lp, HEADS * DH, D, F32)
    dw_a = matmul("dw_branch_a", o_a, dya, "tn", HEADS * DH, D, lp, BF16, tm=1024, tn=1024, tk=512)
    dw_b = matmul("dw_branch_b", o_b, dyb, "tn", HEADS * DH, D, lp, BF16, tm=1024, tn=1024, tk=512)
    dproj_hg, dl0, dl1, d_hg_nw = hgrn2_bwd(proj, l0, l1, hg_norm_w, st_a, do_a)
    dproj_gd, dproj_ab, dcw, d_alog, d_dtb, d_gd_nw = gdn_bwd(proj, proj_ab, cw, alog, dtb, gd_norm_w, st_b, do_b)
    dproj = jnp.concatenate([dproj_hg, dproj_gd, dgates], axis=1)
    dxn = matmul("d_xn", dproj, w_main, "nt", lp, D, 12288, F32, tk=1024)
    dxn_ab = matmul("d_xn_ab", dproj_ab, w_ab, "nt", lp, D, LANES, F32)
    dw_main = matmul("dw_in_main", xn, dproj, "tn", D, 12288, lp, BF16, tm=1024, tn=1024, tk=512)
    dw_ab = matmul("dw_in_ab", xn, dproj_ab, "tn", D, LANES, lp, BF16, tm=1024, tk=512)
    grad_x, dpre, d_mix_norm_w = norm1_bwd(pre, xs, mix_norm_w, dxn, dxn_ab, dh1)

    dw_in_full = jnp.concatenate([_group_major(dw_main[:, 0:4096], 4), _group_major(dw_main[:, 4096:8192], 4),
                                  dw_ab[:, :2 * HEADS], dw_main[:, 8192:]], axis=1)
    parts = [_cols_to_shards(dw_in_full), _cols_to_shards(dw_a), _cols_to_shards(dw_b),
             dw_o.reshape(N_DEV, D // N_DEV, D), _cols_to_shards(_deinterleave_ff(dw_gu)),
             dw_fo.reshape(N_DEV, D_FF // N_DEV, D)]
    parts = [p.reshape((4, 2) + p.shape[1:]) for p in parts]
    from_core = exchange_core(parts)
    tiles = [256, 256, 256, 64, 256, 176]
    chip_sums = [pair_add(f"pair_add_{a}", lax.dynamic_index_in_dim(p, cc, axis=1, keepdims=False), r, t)
                 for a, (p, r, t) in enumerate(zip(parts, from_core, tiles))]
    from_chips = exchange_chips(chip_sums)
    owns = [lax.dynamic_index_in_dim(s, my_chip, axis=0, keepdims=False) for s in chip_sums]
    names = ["w_in", "w_branch_a", "w_branch_b", "w_out", "w_ffn_in", "w_ffn_out"]
    moms = [(m_w_in, v_w_in), (m_w_branch_a, v_w_branch_a), (m_w_branch_b, v_w_branch_b), (m_w_out, v_w_out),
            (m_w_ffn_in, v_w_ffn_in), (m_w_ffn_out, v_w_ffn_out)]
    upd = {}
    for nm, w, (m, v), own, recv, t in zip(names, big, moms, owns, from_chips, tiles):
        g, d, nm_, nv_ = adamw_shard("adamw_" + nm, w, m[0], v[0], own, recv, t)
        upd[nm] = tuple(t_[None] for t_ in (g, d, nm_, nv_))

    d_lb = jnp.concatenate([dl0.reshape(1, HEADS * DH), dl1.reshape(1, HEADS * DH)], axis=0)
    d_hg_nw = jnp.sum(d_hg_nw, axis=0)
    d_conv = _group_major(dcw.reshape(HEADS, CONV_K, 3, DH).transpose(1, 0, 2, 3).reshape(CONV_K, 3 * HEADS * DH), 3)
    rep = [d_lb, d_mix_norm_w, d_hg_nw, d_alog, d_dtb, d_gd_nw, d_ffn_norm_w, d_final_w]
    buf = jnp.concatenate([_rows128(t) for t in rep] + [loss_part, _rows128(dpre[PRE - N_META:]), _rows128(d_conv)],
                          axis=0)
    n_rep = sum(_rows128(t).shape[0] for t in rep)
    n_rows = buf.shape[0]
    buf = jnp.pad(buf, ((0, -n_rows % 8), (0, 0)))
    tot = allreduce_small(buf)
    loss = tot[n_rep, 0]
    g_meta_full = tot[n_rep + 1:n_rep + 1 + N_META * D // LANES].reshape(N_META, D)
    g_conv_full = tot[n_rep + 1 + N_META * D // LANES:n_rows].reshape(CONV_K, 3 * HEADS * DH)
    g_meta_mine = lax.dynamic_slice_in_dim(g_meta_full, me * (D // N_DEV), D // N_DEV, axis=1)
    g_conv_mine = lax.dynamic_slice_in_dim(g_conv_full, me * (3 * DH), 3 * DH, axis=1)

    small = [("lb_logits", lb_logits, m_lb_logits, v_lb_logits), ("mix_norm_w", mix_norm_w, m_mix_norm_w, v_mix_norm_w),
             ("hg_norm_w", hg_norm_w, m_hg_norm_w, v_hg_norm_w), ("gd_a_log", gd_a_log, m_gd_a_log, v_gd_a_log),
             ("gd_dt_bias", gd_dt_bias, m_gd_dt_bias, v_gd_dt_bias), ("gd_norm_w", gd_norm_w, m_gd_norm_w, v_gd_norm_w),
             ("ffn_norm_w", ffn_norm_w, m_ffn_norm_w, v_ffn_norm_w),
             ("final_norm_w", final_norm_w, m_final_norm_w, v_final_norm_w),
             ("meta_tokens", meta_tokens, m_meta_tokens, v_meta_tokens), ("gd_conv_w", gd_conv_w, m_gd_conv_w, v_gd_conv_w)]

    def pack(t):
        return _pad_lanes(t) if t.size < LANES else _rows128(t)

    sizes = [pack(w).shape[0] for _, w, _, _ in small]
    g_small = jnp.concatenate([tot[:n_rep], _rows128(g_meta_mine), _rows128(g_conv_mine)], axis=0)
    n_small = g_small.shape[0]
    assert n_small == sum(sizes)
    padr = lambda t: jnp.pad(t, ((0, -n_small % 8), (0, 0)))
    w_small = padr(jnp.concatenate([pack(w) for _, w, _, _ in small], axis=0))
    m_small = padr(jnp.concatenate([pack(m) for _, _, m, _ in small], axis=0))
    v_small = padr(jnp.concatenate([pack(v) for _, _, _, v in small], axis=0))
    g_small = padr(g_small)
    d_small, nm_small, nv_small = adamw_small(w_small, g_small, m_small, v_small)
    row = 0
    for (nm, w, _, _), sz in zip(small, sizes):
        def unpack(t):
            return t[row:row + sz].reshape(-1)[:w.size].reshape(w.shape)
        upd[nm] = (unpack(g_small), unpack(d_small), unpack(nm_small), unpack(nv_small))
        row += sz

    order = ["meta_tokens", "lb_logits", "mix_norm_w", "w_in", "hg_norm_w", "gd_conv_w", "gd_a_log", "gd_dt_bias",
             "gd_norm_w", "w_branch_a", "w_branch_b", "w_out", "ffn_norm_w", "w_ffn_in", "w_ffn_out", "final_norm_w"]
    outs = [loss, grad_x[None]]
    for j in range(4):
        outs += [upd[nm][j] for nm in order]
    return tuple(outs)
```

```python
import functools

import jax
import jax.numpy as jnp
from jax import lax
from jax.experimental import pallas as pl
from jax.experimental.pallas import tpu as pltpu

F32 = jnp.float32
BF16 = jnp.bfloat16
MESH = pl.DeviceIdType.MESH

EPS = 1e-6
D = 2048
N_META = 16
CHUNK = 64
SUB = 16
HEADS = 8
HB = 8
DH = 128
CONV_K = 4
TAIL = 8
D_FF = 5632
PRE = 512
N_DEV = 8
LANES = 128
FF_BLK = 512

ADAM_LR = 0.001
ADAM_B1 = 0.9
ADAM_B2 = 0.999
ADAM_EPS = 1e-08
ADAM_WD = 0.01
ADAM_STEP = 10

VMEM_LIMIT = 52 * 1024 * 1024

_NN = ((1,), (0,))
_NT = ((1,), (1,))
_TN = ((0,), (0,))


def _params(*sem):
    return pltpu.CompilerParams(dimension_semantics=sem, vmem_limit_bytes=VMEM_LIMIT)


BF16_ONCE, SPLIT, EXACT_LHS = 0, 1, 2


def _dot_bf16(a, b, dims):
    if a.ndim == 3:
        dn = (((dims[0][0] + 1,), (dims[1][0] + 1,)), ((0,), (0,)))
    else:
        dn = (dims, ((), ()))
    return lax.dot_general(a, b, dn, preferred_element_type=F32)


def _split(t):
    hi = t.astype(BF16)
    return hi, (t - hi.astype(F32)).astype(BF16)


def _raw_dot(a, b, dims, mode):
    if mode == BF16_ONCE:
        return _dot_bf16(a.astype(BF16), b.astype(BF16), dims)
    if mode == SPLIT:
        a_hi, a_lo = _split(a)
        b_hi, b_lo = _split(b)
        return _dot_bf16(a_hi, b_hi, dims) + (_dot_bf16(a_hi, b_lo, dims) + _dot_bf16(a_lo, b_hi, dims))
    a = a.astype(BF16)
    b_hi = b.astype(BF16)
    rest = b - b_hi.astype(F32)
    b_mid = rest.astype(BF16)
    b_lo = (rest - b_mid.astype(F32)).astype(BF16)
    return _dot_bf16(a, b_hi, dims) + (_dot_bf16(a, b_mid, dims) + _dot_bf16(a, b_lo, dims))


@functools.partial(jax.custom_vjp, nondiff_argnums=(2,))
def mm_nn(a, b, mode=BF16_ONCE):
    return _raw_dot(a, b, _NN, mode)


@functools.partial(jax.custom_vjp, nondiff_argnums=(2,))
def mm_nt(a, b, mode=BF16_ONCE):
    return _raw_dot(a, b, _NT, mode)


@functools.partial(jax.custom_vjp, nondiff_argnums=(2,))
def mm_tn(a, b, mode=BF16_ONCE):
    return _raw_dot(a, b, _TN, mode)


def _general(mode):
    return SPLIT if mode == EXACT_LHS else mode


mm_nn.defvjp(lambda a, b, p: (_raw_dot(a, b, _NN, p), (a, b)),
             lambda p, res, g: (mm_nt(g, res[1], _general(p)), mm_tn(res[0], g, p)))
mm_nt.defvjp(lambda a, b, p: (_raw_dot(a, b, _NT, p), (a, b)),
             lambda p, res, g: (mm_nn(g, res[1], p), mm_tn(g, res[0], p)))
mm_tn.defvjp(lambda a, b, p: (_raw_dot(a, b, _TN, p), (a, b)),
             lambda p, res, g: (mm_nt(res[1], g, _general(p)), mm_nn(res[0], g, _general(p))))


def _sigmoid(x):
    return 1.0 / (1.0 + jnp.exp(-x))


def _silu(x):
    return x * _sigmoid(x)


def _softplus(x):
    return jnp.maximum(x, 0.0) + jnp.log(1.0 + jnp.exp(-jnp.abs(x)))


def _l2n(t):
    return t * lax.rsqrt(jnp.sum(t * t, axis=-1, keepdims=True) + EPS)


def _rms_norm(x, w):
    return x * lax.rsqrt(jnp.mean(x * x, axis=-1, keepdims=True) + EPS) * w


def _gated_norm(o, gate, nw):
    o = o * lax.rsqrt(jnp.mean(o * o, axis=-1, keepdims=True) + EPS) * nw
    return o * _silu(gate)


def hg_chunk(hq, hf, hi, hg, l0, l1, nw, st):
    nb = hq.shape[0]
    m = jnp.maximum(l0, l1)
    e0 = jnp.exp(l0 - m)
    e1 = jnp.exp(l1 - m)
    lb = e0 / (e0 + e1)
    sig = _sigmoid(hf)
    q = _silu(hq)
    log_f = jnp.log(lb + (1.0 - lb) * sig)
    k = (1.0 - lb) * _sigmoid(-hf)
    v = hi
    rows = lax.broadcasted_iota(jnp.int32, (nb, CHUNK, CHUNK), 1)
    cols = lax.broadcasted_iota(jnp.int32, (nb, CHUNK, CHUNK), 2)
    b = mm_nn((rows >= cols).astype(F32), log_f, EXACT_LHS)
    o_inter = mm_nt(q * jnp.exp(b), st)
    tri_s = (lax.broadcasted_iota(jnp.int32, (nb, SUB, SUB, DH), 1)
             >= lax.broadcasted_iota(jnp.int32, (nb, SUB, SUB, DH), 2))
    outs = []
    for i in range(CHUNK // SUB):
        lo = i * SUB
        b_i, q_i, k_i, v_i = b[:, lo:lo + SUB], q[:, lo:lo + SUB], k[:, lo:lo + SUB], v[:, lo:lo + SUB]
        diff = b_i[:, :, None, :] - b_i[:, None, :, :]
        dec = jnp.where(tri_s, jnp.exp(jnp.minimum(diff, 0.0)), 0.0)
        s_ii = jnp.sum(q_i[:, :, None, :] * k_i[:, None, :, :] * dec, axis=-1)
        o_i = mm_nn(s_ii, v_i)
        if i > 0:
            b_ref = jnp.sum(log_f[:, :lo], axis=1, keepdims=True)
            q_t = q_i * jnp.exp(b_i - b_ref)
            k_t = k[:, :lo] * jnp.exp(b_ref - b[:, :lo])
            o_i = o_i + mm_nn(mm_nt(q_t, k_t), v[:, :lo])
        outs.append(o_i)
    o = o_inter + jnp.concatenate(outs, axis=1)
    b_last = jnp.sum(log_f, axis=1, keepdims=True)
    st_new = st * jnp.exp(b_last) + mm_tn(v, k * jnp.exp(b_last - b))
    return _gated_norm(o, hg, nw), st_new


def hgrn2_fwd(proj, l0, l1, nw):
    lp = proj.shape[0]
    nc = lp // CHUNK

    def body(x_ref, l0_ref, l1_ref, nw_ref, o_ref, st_out_ref, st_ref):
        c, g = pl.program_id(0), pl.program_id(1)

        @pl.when(c == 0)
        def _():
            for j in range(HB):
                st_ref[g * HB + j] = jnp.zeros((DH, DH), F32)

        st = jnp.stack([st_ref[g * HB + j] for j in range(HB)])
        st_out_ref[0] = st
        part = lambda i: jnp.stack([x_ref[:, 4 * DH * j + DH * i:4 * DH * j + DH * (i + 1)] for j in range(HB)])
        heads = lambda ref: jnp.stack([ref[:, DH * j:DH * (j + 1)] for j in range(HB)])
        o, st_new = hg_chunk(part(0), part(1), part(2), part(3), heads(l0_ref), heads(l1_ref), nw_ref[...], st)
        for j in range(HB):
            o_ref[:, DH * j:DH * (j + 1)] = o[j].astype(o_ref.dtype)
            st_ref[g * HB + j] = st_new[j]

    vec = pl.BlockSpec((1, HB * DH), lambda c, g: (0, g))
    return pl.pallas_call(
        body, name="hgrn2_fwd", grid=(nc, HEADS // HB),
        in_specs=[pl.BlockSpec((CHUNK, HB * 4 * DH), lambda c, g: (c, g)), vec, vec,
                  pl.BlockSpec((1, DH), lambda c, g: (0, 0))],
        out_specs=[pl.BlockSpec((CHUNK, HB * DH), lambda c, g: (c, g)),
                   pl.BlockSpec((1, HB, DH, DH), lambda c, g: (c, g, 0, 0))],
        out_shape=[jax.ShapeDtypeStruct((lp, HEADS * DH), BF16),
                   jax.ShapeDtypeStruct((nc, HEADS, DH, DH), F32)],
        scratch_shapes=[pltpu.VMEM((HEADS, DH, DH), F32)],
        compiler_params=_params("arbitrary", "arbitrary"),
    )(proj, l0, l1, nw)


def hgrn2_bwd(proj, l0, l1, nw, states, do):
    lp = proj.shape[0]
    nc = lp // CHUNK

    def body(x_ref, l0_ref, l1_ref, nw_ref, st_in_ref, do_ref, dx_ref, dl0_ref, dl1_ref, dnw_ref, dst_ref):
        c, g = pl.program_id(0), pl.program_id(1)

        @pl.when(c == 0)
        def _():
            for j in range(HB):
                h = g * HB + j
                dst_ref[h] = jnp.zeros((DH, DH), F32)
                dl0_ref[h] = jnp.zeros((1, DH), F32)
                dl1_ref[h] = jnp.zeros((1, DH), F32)
                dnw_ref[h] = jnp.zeros((1, DH), F32)

        carried = [(dst_ref[g * HB + j], dl0_ref[g * HB + j], dl1_ref[g * HB + j], dnw_ref[g * HB + j])
                   for j in range(HB)]
        part = lambda i: jnp.stack([x_ref[:, 4 * DH * j + DH * i:4 * DH * j + DH * (i + 1)] for j in range(HB)])
        heads = lambda ref: jnp.stack([ref[:, DH * j:DH * (j + 1)] for j in range(HB)])
        _, vjp = jax.vjp(hg_chunk, part(0), part(1), part(2), part(3), heads(l0_ref), heads(l1_ref), nw_ref[...],
                         st_in_ref[0])
        grads = vjp((heads(do_ref), jnp.stack([t[0] for t in carried])))
        dl0, dl1, dnw, dst = grads[4:]
        for j in range(HB):
            h = g * HB + j
            for i in range(4):
                dx_ref[:, 4 * DH * j + DH * i:4 * DH * j + DH * (i + 1)] = grads[i][j].astype(dx_ref.dtype)
            dst_ref[h], dl0_ref[h], dl1_ref[h] = dst[j], carried[j][1] + dl0[j], carried[j][2] + dl1[j]
        dnw_ref[g * HB] = carried[0][3] + dnw

    rc = lambda c: nc - 1 - c
    vec = pl.BlockSpec((1, HB * DH), lambda c, g: (0, g))
    acc = pl.BlockSpec((HEADS, 1, DH), lambda c, g: (0, 0, 0))
    acc_shape = jax.ShapeDtypeStruct((HEADS, 1, DH), F32)
    return pl.pallas_call(
        body, name="hgrn2_bwd", grid=(nc, HEADS // HB),
        in_specs=[pl.BlockSpec((CHUNK, HB * 4 * DH), lambda c, g: (rc(c), g)), vec, vec,
                  pl.BlockSpec((1, DH), lambda c, g: (0, 0)),
                  pl.BlockSpec((1, HB, DH, DH), lambda c, g: (rc(c), g, 0, 0)),
                  pl.BlockSpec((CHUNK, HB * DH), lambda c, g: (rc(c), g))],
        out_specs=[pl.BlockSpec((CHUNK, HB * 4 * DH), lambda c, g: (rc(c), g)), acc, acc, acc],
        out_shape=[jax.ShapeDtypeStruct((lp, 4 * HEADS * DH), BF16), acc_shape, acc_shape, acc_shape],
        scratch_shapes=[pltpu.VMEM((HEADS, DH, DH), F32)],
        compiler_params=_params("arbitrary", "arbitrary"),
    )(proj, l0, l1, nw, states, do)


def _unit_lower_inverse(a):
    n = a.shape[-1]
    eye = (lax.broadcasted_iota(jnp.int32, a.shape, 1) == lax.broadcasted_iota(jnp.int32, a.shape, 2)).astype(F32)
    x = eye - a
    p = mm_nn(a, a, SPLIT)
    x = x + mm_nn(x, p, SPLIT)
    power = 4
    while power < n:
        p = mm_nn(p, p, SPLIT)
        x = x + mm_nn(x, p, SPLIT)
        power *= 2
    return x


def gd_chunk(cq, ck, cv, z, ab, alog, dtb, nw, s, h0):
    nb, c, _ = cq.shape
    lane = lax.broadcasted_iota(jnp.int32, (nb, 1, DH), 2)
    head = h0 + lax.broadcasted_iota(jnp.int32, (nb, 1, DH), 0)
    pick = lambda t, off: jnp.sum(jnp.where(lane == head + off, t[None], 0.0), axis=2, keepdims=True)
    a_raw, b_raw = pick(ab, 0), pick(ab, HEADS)
    beta = _sigmoid(b_raw)
    g = -jnp.exp(pick(alog, 0)) * _softplus(a_raw + pick(dtb, 0))
    q = _l2n(_silu(cq)) * (DH ** -0.5)
    k = _l2n(_silu(ck))
    v = _silu(cv)
    rows = lax.broadcasted_iota(jnp.int32, (nb, c, c), 1)
    cols = lax.broadcasted_iota(jnp.int32, (nb, c, c), 2)
    tril = (rows >= cols).astype(F32)
    gc_w = mm_nn(tril, jnp.broadcast_to(g, (nb, c, DH)), EXACT_LHS)
    g_sq = jnp.broadcast_to(g, (nb, c, c))
    gc_col = mm_nn(tril, g_sq, EXACT_LHS)
    gc_row = mm_nn(jnp.ones((nb, c, c), F32), g_sq * (rows <= cols).astype(F32), EXACT_LHS)
    rel = jnp.exp(jnp.minimum(gc_col - gc_row, 0.0))
    a = jnp.where(rows > cols, beta * mm_nt(k, k) * rel, 0.0)
    t_inv = _unit_lower_inverse(a)
    e_gc = jnp.exp(gc_w)
    w = mm_nn(t_inv, beta * e_gc * k, SPLIT)
    u = mm_nn(t_inv, beta * v, SPLIT)
    v_new = u - mm_nn(w, s)
    attn = jnp.where(rows >= cols, mm_nt(q, k) * rel, 0.0)
    o = mm_nn(q * e_gc, s) + mm_nn(attn, v_new)
    g_last = jnp.sum(g, axis=1, keepdims=True)
    s_new = jnp.exp(g_last) * s + mm_tn(k * jnp.exp(g_last - gc_w), v_new)
    return _gated_norm(o, z, nw), s_new


def _gd_conv(x_ref, tail_ref, cw_ref, xe_ref, first, j):
    xe_ref[j, 0:TAIL, :] = jnp.where(first, 0.0, tail_ref[:, 4 * DH * j:4 * DH * j + 3 * DH])
    xe_ref[j, TAIL:TAIL + CHUNK, :] = x_ref[:, 4 * DH * j:4 * DH * j + 3 * DH]
    cols = slice(3 * DH * j, 3 * DH * (j + 1))
    y = cw_ref[pl.ds(0, 1), cols] * xe_ref[j, pl.ds(TAIL - CONV_K + 1, CHUNK), :]
    for k in range(1, CONV_K):
        y = y + cw_ref[pl.ds(k, 1), cols] * xe_ref[j, pl.ds(TAIL - CONV_K + 1 + k, CHUNK), :]
    return y


def gdn_fwd(proj, proj_ab, cw, alog, dtb, nw):
    lp = proj.shape[0]
    nc = lp // CHUNK

    def body(x_ref, tail_ref, ab_ref, cw_ref, alog_ref, dtb_ref, nw_ref, o_ref, st_out_ref, st_ref, xe_ref):
        c, g = pl.program_id(0), pl.program_id(1)

        @pl.when(c == 0)
        def _():
            for j in range(HB):
                st_ref[g * HB + j] = jnp.zeros((DH, DH), F32)

        st = jnp.stack([st_ref[g * HB + j] for j in range(HB)])
        st_out_ref[0] = st
        ys = [_gd_conv(x_ref, tail_ref, cw_ref, xe_ref, c == 0, j) for j in range(HB)]
        part = lambda i: jnp.stack([y[:, DH * i:DH * (i + 1)] for y in ys])
        z = jnp.stack([x_ref[:, 4 * DH * j + 3 * DH:4 * DH * (j + 1)] for j in range(HB)])
        o, st_new = gd_chunk(part(0), part(1), part(2), z, ab_ref[...], alog_ref[...], dtb_ref[...], nw_ref[...],
                             st, g * HB)
        for j in range(HB):
            o_ref[:, DH * j:DH * (j + 1)] = o[j].astype(o_ref.dtype)
            st_ref[g * HB + j] = st_new[j]

    one = pl.BlockSpec((1, DH), lambda c, g: (0, 0))
    ng = HEADS // HB
    return pl.pallas_call(
        body, name="gdn_fwd", grid=(nc, ng),
        in_specs=[pl.BlockSpec((CHUNK, HB * 4 * DH), lambda c, g: (c, ng + g)),
                  pl.BlockSpec((TAIL, HB * 4 * DH),
                               lambda c, g: (jnp.maximum(c * (CHUNK // TAIL) - 1, 0), ng + g)),
                  pl.BlockSpec((CHUNK, DH), lambda c, g: (c, 0)),
                  pl.BlockSpec((CONV_K, HB * 3 * DH), lambda c, g: (0, g)), one, one, one],
        out_specs=[pl.BlockSpec((CHUNK, HB * DH), lambda c, g: (c, g)),
                   pl.BlockSpec((1, HB, DH, DH), lambda c, g: (c, g, 0, 0))],
        out_shape=[jax.ShapeDtypeStruct((lp, HEADS * DH), BF16),
                   jax.ShapeDtypeStruct((nc, HEADS, DH, DH), F32)],
        scratch_shapes=[pltpu.VMEM((HEADS, DH, DH), F32), pltpu.VMEM((HB, TAIL + CHUNK, 3 * DH), F32)],
        compiler_params=_params("arbitrary", "arbitrary"),
    )(proj, proj, proj_ab, cw, alog, dtb, nw)


def gdn_bwd(proj, proj_ab, cw, alog, dtb, nw, states, do):
    lp = proj.shape[0]
    nc = lp // CHUNK

    def body(x_ref, tail_ref, ab_ref, cw_ref, alog_ref, dtb_ref, nw_ref, st_in_ref, do_ref,
             dx_ref, dab_ref, dcw_ref, dalog_ref, ddtb_ref, dnw_ref, dst_ref, xe_ref, dye_ref, head_ref):
        c, g = pl.program_id(0), pl.program_id(1)
        cc = nc - 1 - c

        @pl.when(jnp.logical_and(c == 0, g == 0))
        def _():
            dcw_ref[...] = jnp.zeros(dcw_ref.shape, F32)
            dalog_ref[...] = jnp.zeros(dalog_ref.shape, F32)
            ddtb_ref[...] = jnp.zeros(ddtb_ref.shape, F32)
            dnw_ref[...] = jnp.zeros(dnw_ref.shape, F32)

        @pl.when(c == 0)
        def _():
            for j in range(HB):
                dst_ref[g * HB + j] = jnp.zeros((DH, DH), F32)
                head_ref[g * HB + j] = jnp.zeros((TAIL, 3 * DH), F32)

        carried = [(dst_ref[g * HB + j], head_ref[g * HB + j], dcw_ref[g * HB + j]) for j in range(HB)]
        ys = [_gd_conv(x_ref, tail_ref, cw_ref, xe_ref, cc == 0, j) for j in range(HB)]
        part = lambda i: jnp.stack([y[:, DH * i:DH * (i + 1)] for y in ys])
        z = jnp.stack([x_ref[:, 4 * DH * j + 3 * DH:4 * DH * (j + 1)] for j in range(HB)])
        do = jnp.stack([do_ref[:, DH * j:DH * (j + 1)] for j in range(HB)])
        fn = functools.partial(gd_chunk, h0=g * HB)
        _, vjp = jax.vjp(fn, part(0), part(1), part(2), z, ab_ref[...], alog_ref[...], dtb_ref[...], nw_ref[...],
                         st_in_ref[0])
        dcq, dck, dcv, dz_all, dab_sum, dalog, ddtb, dnw, dst_all = vjp((do, jnp.stack([t[0] for t in carried])))
        dalog_sum, ddtb_sum, dnw_sum = dalog_ref[...] + dalog, ddtb_ref[...] + ddtb, dnw_ref[...] + dnw
        results = []
        for j in range(HB):
            dz, dst = dz_all[j], dst_all[j]
            dye_ref[j, 0:CHUNK, 0:DH] = dcq[j]
            dye_ref[j, 0:CHUNK, DH:2 * DH] = dck[j]
            dye_ref[j, 0:CHUNK, 2 * DH:3 * DH] = dcv[j]
            dye_ref[j, CHUNK:CHUNK + TAIL, :] = carried[j][1]
            dy = dye_ref[j, 0:CHUNK, :]
            cols = slice(3 * DH * j, 3 * DH * (j + 1))
            dx, dcw_rows = None, []
            for k in range(CONV_K):
                term = cw_ref[pl.ds(k, 1), cols] * dye_ref[j, pl.ds(CONV_K - 1 - k, CHUNK), :]
                dx = term if dx is None else dx + term
                dcw_rows.append(jnp.sum(dy * xe_ref[j, pl.ds(TAIL - CONV_K + 1 + k, CHUNK), :],
                                        axis=0, keepdims=True))
            base = 4 * DH * j
            dx_ref[:, base:base + 3 * DH] = dx.astype(dx_ref.dtype)
            dx_ref[:, base + 3 * DH:base + 4 * DH] = dz.astype(dx_ref.dtype)
            results.append((dst, dye_ref[j, 0:TAIL, :], dcw_rows))

        for j in range(HB):
            h = g * HB + j
            dst_ref[h], head_ref[h] = results[j][0], results[j][1]
            tap = lax.broadcasted_iota(jnp.int32, (CONV_K, 3 * DH), 0)
            dcw = carried[j][2]
            for k in range(CONV_K):
                dcw = dcw + jnp.where(tap == k, results[j][2][k], 0.0)
            dcw_ref[h] = dcw
        dalog_ref[...] = dalog_sum
        ddtb_ref[...] = ddtb_sum
        dnw_ref[...] = dnw_sum

        @pl.when(g == 0)
        def _():
            dab_ref[...] = dab_sum

        @pl.when(g > 0)
        def _():
            dab_ref[...] += dab_sum

    rc = lambda c: nc - 1 - c
    ng = HEADS // HB
    one = pl.BlockSpec((1, DH), lambda c, g: (0, 0))
    one_shape = jax.ShapeDtypeStruct((1, DH), F32)
    return pl.pallas_call(
        body, name="gdn_bwd", grid=(nc, ng),
        in_specs=[pl.BlockSpec((CHUNK, HB * 4 * DH), lambda c, g: (rc(c), ng + g)),
                  pl.BlockSpec((TAIL, HB * 4 * DH),
                               lambda c, g: (jnp.maximum(rc(c) * (CHUNK // TAIL) - 1, 0), ng + g)),
                  pl.BlockSpec((CHUNK, DH), lambda c, g: (rc(c), 0)),
                  pl.BlockSpec((CONV_K, HB * 3 * DH), lambda c, g: (0, g)), one, one, one,
                  pl.BlockSpec((1, HB, DH, DH), lambda c, g: (rc(c), g, 0, 0)),
                  pl.BlockSpec((CHUNK, HB * DH), lambda c, g: (rc(c), g))],
        out_specs=[pl.BlockSpec((CHUNK, HB * 4 * DH), lambda c, g: (rc(c), g)),
                   pl.BlockSpec((CHUNK, DH), lambda c, g: (rc(c), 0)),
                   pl.BlockSpec((HEADS, CONV_K, 3 * DH), lambda c, g: (0, 0, 0)), one, one, one],
        out_shape=[jax.ShapeDtypeStruct((lp, 4 * HEADS * DH), BF16), jax.ShapeDtypeStruct((lp, DH), F32),
                   jax.ShapeDtypeStruct((HEADS, CONV_K, 3 * DH), F32), one_shape, one_shape, one_shape],
        scratch_shapes=[pltpu.VMEM((HEADS, DH, DH), F32), pltpu.VMEM((HB, TAIL + CHUNK, 3 * DH), F32),
                        pltpu.VMEM((HB, CHUNK + TAIL, 3 * DH), F32), pltpu.VMEM((HEADS, TAIL, 3 * DH), F32)],
        compiler_params=_params("arbitrary", "arbitrary"),
    )(proj, proj, proj_ab, cw, alog, dtb, nw, states, do)


def matmul(name, a, b, mode, m, n, k, out_dtype, tm=512, tn=1024, tk=None, a_off=(0, 0), b_off=(0, 0)):
    tm, tn = min(tm, m), min(tn, n)
    tk = k if tk is None else min(tk, k)
    assert m % tm == 0 and n % tn == 0 and k % tk == 0, (name, m, n, k, tm, tn, tk)
    gi, gj, gk = m // tm, n // tn, k // tk
    if mode == "nn":
        a_spec = pl.BlockSpec((tm, tk), lambda j, i, kk: (i + a_off[0], kk + a_off[1]))
        b_spec = pl.BlockSpec((tk, tn), lambda j, i, kk: (kk + b_off[0], j + b_off[1]))
        dims = _NN
    elif mode == "nt":
        a_spec = pl.BlockSpec((tm, tk), lambda j, i, kk: (i + a_off[0], kk + a_off[1]))
        b_spec = pl.BlockSpec((tn, tk), lambda j, i, kk: (j + b_off[0], kk + b_off[1]))
        dims = _NT
    else:
        a_spec = pl.BlockSpec((tk, tm), lambda j, i, kk: (kk + a_off[0], i + a_off[1]))
        b_spec = pl.BlockSpec((tk, tn), lambda j, i, kk: (kk + b_off[0], j + b_off[1]))
        dims = _TN

    def body(a_ref, b_ref, o_ref, *acc):
        d = lax.dot_general(a_ref[...].astype(BF16), b_ref[...].astype(BF16), (dims, ((), ())),
                            preferred_element_type=F32)
        if gk == 1:
            o_ref[...] = d.astype(o_ref.dtype)
        else:
            acc_ref, = acc
            kk = pl.program_id(2)

            @pl.when(kk == 0)
            def _():
                acc_ref[...] = d

            @pl.when(kk > 0)
            def _():
                acc_ref[...] += d

            @pl.when(kk == gk - 1)
            def _():
                o_ref[...] = acc_ref[...].astype(o_ref.dtype)

    return pl.pallas_call(
        body, name=name, grid=(gj, gi, gk),
        in_specs=[a_spec, b_spec],
        out_specs=pl.BlockSpec((tm, tn), lambda j, i, kk: (i, j)),
        out_shape=jax.ShapeDtypeStruct((m, n), out_dtype),
        scratch_shapes=[] if gk == 1 else [pltpu.VMEM((tm, tn), F32)],
        compiler_params=_params("parallel", "parallel", "arbitrary"),
    )(a, b)


TR = 256
N_PRE = PRE // TR


def _row(width, off=0, col=0):
    return pl.BlockSpec((TR, width), lambda i: (i + off, col))


def _pre_spec():
    return pl.BlockSpec((TR, D), lambda i: (jnp.minimum(i, N_PRE - 1), 0))


def _seq_spec(width=D, col=0):
    return pl.BlockSpec((TR, width), lambda i: (jnp.maximum(i - N_PRE, 0), col))


def _vec_spec(width=D):
    return pl.BlockSpec((1, width), lambda i: (0, 0))


def norm1_fwd(pre, x, w):
    lp = PRE + x.shape[0]

    def body(pre_ref, x_ref, w_ref, o_ref):
        h = jnp.where(pl.program_id(0) < N_PRE, pre_ref[...], x_ref[...])
        o_ref[...] = _rms_norm(h, w_ref[...]).astype(o_ref.dtype)

    return pl.pallas_call(
        body, name="norm1_fwd", grid=(lp // TR,),
        in_specs=[_pre_spec(), _seq_spec(), _vec_spec()],
        out_specs=_row(D), out_shape=jax.ShapeDtypeStruct((lp, D), BF16),
        compiler_params=_params("parallel"),
    )(pre, x, w)


def norm1_bwd(pre, x, w, dxn, dxn_ab, dh1):
    l = x.shape[0]
    lp = PRE + l

    def body(pre_ref, x_ref, w_ref, dxn_ref, dxn_ab_ref, dh1_ref, dx_ref, dpre_ref, dw_ref):
        i = pl.program_id(0)
        h = jnp.where(i < N_PRE, pre_ref[...], x_ref[...])
        _, vjp = jax.vjp(_rms_norm, h, w_ref[...])
        dh, dw = vjp(dxn_ref[...] + dxn_ab_ref[...])

        @pl.when(i == 0)
        def _():
            dw_ref[...] = dw

        @pl.when(i > 0)
        def _():
            dw_ref[...] += dw

        @pl.when(i < N_PRE)
        def _():
            dpre_ref[...] = dh

        @pl.when(i >= N_PRE)
        def _():
            dx_ref[...] = dh + dh1_ref[...]

    return pl.pallas_call(
        body, name="norm1_bwd", grid=(lp // TR,),
        in_specs=[_pre_spec(), _seq_spec(), _vec_spec(), _row(D), _row(D), _seq_spec()],
        out_specs=[_seq_spec(), _pre_spec(), _vec_spec()],
        out_shape=[jax.ShapeDtypeStruct((l, D), F32), jax.ShapeDtypeStruct((PRE, D), F32),
                   jax.ShapeDtypeStruct((1, D), F32)],
        compiler_params=_params("arbitrary"),
    )(pre, x, w, dxn, dxn_ab, dh1)


def _merge(gates, ya, yb):
    return _sigmoid(gates[:, :D]) * ya + _sigmoid(gates[:, D:]) * yb


def merge_fwd(proj, ya, yb):
    l = ya.shape[0]

    def body(g_ref, ya_ref, yb_ref, o_ref):
        o_ref[...] = _merge(g_ref[...], ya_ref[...], yb_ref[...]).astype(o_ref.dtype)

    return pl.pallas_call(
        body, name="merge_fwd", grid=(l // TR,),
        in_specs=[_row(2 * D, N_PRE, 2), _row(D), _row(D)],
        out_specs=_row(D), out_shape=jax.ShapeDtypeStruct((l, D), BF16),
        compiler_params=_params("parallel"),
    )(proj, ya, yb)


def merge_bwd(proj, ya, yb, dmerged):
    l = ya.shape[0]
    lp = PRE + l

    def body(g_ref, ya_ref, yb_ref, dm_ref, dg_ref, dya_ref, dyb_ref):
        i = pl.program_id(0)

        @pl.when(i < N_PRE)
        def _():
            dg_ref[...] = jnp.zeros(dg_ref.shape, dg_ref.dtype)
            dya_ref[...] = jnp.zeros(dya_ref.shape, dya_ref.dtype)
            dyb_ref[...] = jnp.zeros(dyb_ref.shape, dyb_ref.dtype)

        @pl.when(i >= N_PRE)
        def _():
            _, vjp = jax.vjp(_merge, g_ref[...], ya_ref[...], yb_ref[...])
            dg, dya, dyb = vjp(dm_ref[...])
            dg_ref[...] = dg.astype(dg_ref.dtype)
            dya_ref[...] = dya.astype(dya_ref.dtype)
            dyb_ref[...] = dyb.astype(dyb_ref.dtype)

    return pl.pallas_call(
        body, name="merge_bwd", grid=(lp // TR,),
        in_specs=[pl.BlockSpec((TR, 2 * D), lambda i: (jnp.maximum(i, N_PRE), 2)), _seq_spec(), _seq_spec(),
                  _seq_spec()],
        out_specs=[_row(2 * D), _row(D), _row(D)],
        out_shape=[jax.ShapeDtypeStruct((lp, 2 * D), BF16), jax.ShapeDtypeStruct((lp, D), BF16),
                   jax.ShapeDtypeStruct((lp, D), BF16)],
        compiler_params=_params("parallel"),
    )(proj, ya, yb, dmerged)


def norm2_fwd(x, mo, w):
    l = x.shape[0]

    def body(x_ref, mo_ref, w_ref, h_ref, o_ref):
        h = x_ref[...] + mo_ref[...]
        h_ref[...] = h
        o_ref[...] = _rms_norm(h, w_ref[...]).astype(o_ref.dtype)

    return pl.pallas_call(
        body, name="norm2_fwd", grid=(l // TR,),
        in_specs=[_row(D), _row(D), _vec_spec()],
        out_specs=[_row(D), _row(D)],
        out_shape=[jax.ShapeDtypeStruct((l, D), F32), jax.ShapeDtypeStruct((l, D), BF16)],
        compiler_params=_params("parallel"),
    )(x, mo, w)


def norm2_bwd(h1, w, dxn2, dh2):
    l = h1.shape[0]

    def body(h_ref, w_ref, dxn_ref, dh2_ref, dh1_ref, dw_ref):
        i = pl.program_id(0)
        _, vjp = jax.vjp(_rms_norm, h_ref[...], w_ref[...])
        dh, dw = vjp(dxn_ref[...])
        dh1_ref[...] = dh + dh2_ref[...]

        @pl.when(i == 0)
        def _():
            dw_ref[...] = dw

        @pl.when(i > 0)
        def _():
            dw_ref[...] += dw

    return pl.pallas_call(
        body, name="norm2_bwd", grid=(l // TR,),
        in_specs=[_row(D), _vec_spec(), _row(D), _row(D)],
        out_specs=[_row(D), _vec_spec()],
        out_shape=[jax.ShapeDtypeStruct((l, D), F32), jax.ShapeDtypeStruct((1, D), F32)],
        compiler_params=_params("arbitrary"),
    )(h1, w, dxn2, dh2)


def _swiglu(gu):
    return _silu(gu[:, :FF_BLK]) * gu[:, FF_BLK:]


def swiglu_fwd(gu):
    l = gu.shape[0]

    def body(gu_ref, o_ref):
        o_ref[...] = _swiglu(gu_ref[...]).astype(o_ref.dtype)

    return pl.pallas_call(
        body, name="swiglu_fwd", grid=(l // TR, D_FF // FF_BLK),
        in_specs=[pl.BlockSpec((TR, 2 * FF_BLK), lambda i, j: (i, j))],
        out_specs=pl.BlockSpec((TR, FF_BLK), lambda i, j: (i, j)),
        out_shape=jax.ShapeDtypeStruct((l, D_FF), BF16),
        compiler_params=_params("parallel", "parallel"),
    )(gu)


def swiglu_bwd(gu, dact):
    l = gu.shape[0]

    def body(gu_ref, da_ref, o_ref):
        _, vjp = jax.vjp(_swiglu, gu_ref[...])
        dgu, = vjp(da_ref[...])
        o_ref[...] = dgu.astype(o_ref.dtype)

    return pl.pallas_call(
        body, name="swiglu_bwd", grid=(l // TR, D_FF // FF_BLK),
        in_specs=[pl.BlockSpec((TR, 2 * FF_BLK), lambda i, j: (i, j)),
                  pl.BlockSpec((TR, FF_BLK), lambda i, j: (i, j))],
        out_specs=pl.BlockSpec((TR, 2 * FF_BLK), lambda i, j: (i, j)),
        out_shape=jax.ShapeDtypeStruct((l, 2 * D_FF), BF16),
        compiler_params=_params("parallel", "parallel"),
    )(gu, dact)


def _loss_rows(h1, f, w, tgt):
    y = _rms_norm(h1 + f, w)
    err = (y - tgt) * (y - tgt)
    return 0.5 * jnp.sum(jnp.mean(err, axis=-1, keepdims=True), axis=0, keepdims=True)


def loss_head(h1, f, w, tgt):
    l = h1.shape[0]

    def body(h_ref, f_ref, w_ref, t_ref, loss_ref, dh_ref, dw_ref):
        i = pl.program_id(0)
        loss, vjp = jax.vjp(_loss_rows, h_ref[...], f_ref[...], w_ref[...], t_ref[...])
        dh, _, dw, _ = vjp(jnp.ones((1, 1), F32))
        dh_ref[...] = dh
        loss = jnp.broadcast_to(loss, (1, LANES))

        @pl.when(i == 0)
        def _():
            dw_ref[...] = dw
            loss_ref[...] = loss

        @pl.when(i > 0)
        def _():
            dw_ref[...] += dw
            loss_ref[...] += loss

    return pl.pallas_call(
        body, name="loss_head", grid=(l // TR,),
        in_specs=[_row(D), _row(D), _vec_spec(), _row(D)],
        out_specs=[_vec_spec(LANES), _row(D), _vec_spec()],
        out_shape=[jax.ShapeDtypeStruct((1, LANES), F32), jax.ShapeDtypeStruct((l, D), F32),
                   jax.ShapeDtypeStruct((1, D), F32)],
        compiler_params=_params("arbitrary"),
    )(h1, f, w, tgt)


def _adamw(w, g, m, v):
    m = ADAM_B1 * m + (1.0 - ADAM_B1) * g
    v = ADAM_B2 * v + (1.0 - ADAM_B2) * (g * g)
    m_hat = m / (1.0 - ADAM_B1 ** ADAM_STEP)
    v_hat = v / (1.0 - ADAM_B2 ** ADAM_STEP)
    delta = -ADAM_LR * (m_hat / (jnp.sqrt(v_hat) + ADAM_EPS) + ADAM_WD * w)
    return delta, m, v


def adamw_shard(name, w, m, v, own, recv, tr):
    r, c = w.shape
    assert r % tr == 0

    def body(w_ref, m_ref, v_ref, own_ref, r0_ref, r1_ref, r2_ref, g_ref, d_ref, nm_ref, nv_ref):
        g = own_ref[...].astype(F32) + r0_ref[...].astype(F32)
        g = g + r1_ref[...].astype(F32)
        g = g + r2_ref[...].astype(F32)
        d, nm, nv = _adamw(w_ref[...], g, m_ref[...], v_ref[...])
        g_ref[...] = g
        d_ref[...] = d
        nm_ref[...] = nm
        nv_ref[...] = nv

    blk = pl.BlockSpec((tr, c), lambda i: (i, 0))
    part = lambda s: pl.BlockSpec((None, tr, c), lambda i: (s, i, 0))
    shape = jax.ShapeDtypeStruct((r, c), F32)
    return pl.pallas_call(
        body, name=name, grid=(r // tr,),
        in_specs=[blk, blk, blk, blk, part(0), part(1), part(2)],
        out_specs=[blk, blk, blk, blk], out_shape=[shape, shape, shape, shape],
        compiler_params=_params("parallel"),
    )(w, m, v, own, recv, recv, recv)


def adamw_small(w, g, m, v):
    def body(w_ref, g_ref, m_ref, v_ref, d_ref, nm_ref, nv_ref):
        d, nm, nv = _adamw(w_ref[...], g_ref[...], m_ref[...], v_ref[...])
        d_ref[...] = d
        nm_ref[...] = nm
        nv_ref[...] = nv

    shape = jax.ShapeDtypeStruct(w.shape, F32)
    return pl.pallas_call(body, name="adamw_small", out_shape=[shape, shape, shape])(w, g, m, v)


def pair_add(name, a, b, tr):
    q, r, c = a.shape
    assert r % tr == 0

    def body(a_ref, b_ref, o_ref):
        o_ref[...] = (a_ref[...].astype(F32) + b_ref[...].astype(F32)).astype(o_ref.dtype)

    blk = pl.BlockSpec((None, tr, c), lambda s, i: (s, i, 0))
    return pl.pallas_call(
        body, name=name, grid=(q, r // tr), in_specs=[blk, blk], out_specs=blk,
        out_shape=jax.ShapeDtypeStruct((q, r, c), BF16), compiler_params=_params("parallel", "parallel"),
    )(a, b)


def _place():
    return lax.axis_index("x"), lax.axis_index("y"), lax.axis_index("c")


def _dev(px, py, pc):
    return 4 * px + 2 * py + pc


def gather_blocks(arrays):
    n = len(arrays)
    any_spec = pl.BlockSpec(memory_space=pl.ANY)

    def body(*refs):
        ins, outs = refs[:n], refs[n:2 * n]
        send_sems, recv_sems, local_sems = refs[2 * n:]
        x, y, c = _place()
        me, sibling = (x, y, c), (x, y, 1 - c)
        chips = [(1 - x, y), (x, 1 - y), (1 - x, 1 - y)]

        def copy(a, k, block, to, src=None):
            dst = outs[a].at[_dev(*block)]
            return pltpu.make_async_remote_copy(
                src_ref=dst if src is None else src, dst_ref=dst, send_sem=send_sems.at[a, k],
                recv_sem=recv_sems.at[a, k], device_id=to, device_id_type=MESH)

        mine = [pltpu.make_async_copy(ins[a], outs[a].at[_dev(*me)], local_sems.at[a]) for a in range(n)]
        for cp in mine:
            cp.start()
        first = []
        for a in range(n):
            first.append(copy(a, 0, me, sibling, src=ins[a]))
            first += [copy(a, 1 + j, me, (*chip, c), src=ins[a]) for j, chip in enumerate(chips)]
        for cp in first:
            cp.start()
        passed = []
        for j, chip in enumerate(chips):
            for a in range(n):
                copy(a, 1 + j, (*chip, c), me).wait_recv()
                fwd = copy(a, 4 + j, (*chip, c), sibling)
                fwd.start()
                passed.append(fwd)
        for a in range(n):
            copy(a, 0, sibling, me).wait_recv()
        for j, chip in enumerate(chips):
            for a in range(n):
                copy(a, 4 + j, (*chip, 1 - c), me).wait_recv()
        for cp in first + passed:
            cp.wait_send()
        for cp in mine:
            cp.wait()

    return pl.pallas_call(
        body, name="gather_blocks",
        in_specs=[any_spec] * n, out_specs=[any_spec] * n,
        out_shape=[jax.ShapeDtypeStruct((N_DEV,) + t.shape, t.dtype) for t in arrays],
        scratch_shapes=[pltpu.SemaphoreType.DMA((n, 7)), pltpu.SemaphoreType.DMA((n, 7)),
                        pltpu.SemaphoreType.DMA((n,))],
    )(*arrays)


def exchange_core(arrays):
    n = len(arrays)
    any_spec = pl.BlockSpec(memory_space=pl.ANY)

    def body(*refs):
        ins, outs = refs[:n], refs[n:2 * n]
        send_sems, recv_sems = refs[2 * n:]
        x, y, c = _place()
        copies = []
        for a in range(n):
            for q in range(4):
                copies.append(pltpu.make_async_remote_copy(
                    src_ref=ins[a].at[q, 1 - c], dst_ref=outs[a].at[q], send_sem=send_sems.at[a, q],
                    recv_sem=recv_sems.at[a, q], device_id=(x, y, 1 - c), device_id_type=MESH))
        for cp in copies:
            cp.start()
        for cp in copies:
            cp.wait()

    return pl.pallas_call(
        body, name="exchange_core",
        in_specs=[any_spec] * n, out_specs=[any_spec] * n,
        out_shape=[jax.ShapeDtypeStruct((4,) + t.shape[2:], t.dtype) for t in arrays],
        scratch_shapes=[pltpu.SemaphoreType.DMA((n, 4)), pltpu.SemaphoreType.DMA((n, 4))],
    )(*arrays)


def exchange_chips(arrays):
    n = len(arrays)
    any_spec = pl.BlockSpec(memory_space=pl.ANY)

    def body(*refs):
        ins, outs = refs[:n], refs[n:2 * n]
        send_sems, recv_sems = refs[2 * n:]
        x, y, c = _place()
        chips = [(1 - x, y), (x, 1 - y), (1 - x, 1 - y)]
        copies = []
        for a in range(n):
            for j, (qx, qy) in enumerate(chips):
                copies.append(pltpu.make_async_remote_copy(
                    src_ref=ins[a].at[2 * qx + qy], dst_ref=outs[a].at[j], send_sem=send_sems.at[a, j],
                    recv_sem=recv_sems.at[a, j], device_id=(qx, qy, c), device_id_type=MESH))
        for cp in copies:
            cp.start()
        for cp in copies:
            cp.wait()

    return pl.pallas_call(
        body, name="exchange_chips",
        in_specs=[any_spec] * n, out_specs=[any_spec] * n,
        out_shape=[jax.ShapeDtypeStruct((3,) + t.shape[1:], t.dtype) for t in arrays],
        scratch_shapes=[pltpu.SemaphoreType.DMA((n, 3)), pltpu.SemaphoreType.DMA((n, 3))],
    )(*arrays)


def allreduce_small(buf):
    r = buf.shape[0]
    vmem = pl.BlockSpec(memory_space=pltpu.VMEM)

    def body(buf_ref, out_ref, all_ref, send_sems, recv_sems):
        x, y, c = _place()
        me = _dev(x, y, c)
        copies = []
        for k in range(1, N_DEV):
            peer = (x ^ (k >> 2), y ^ ((k >> 1) & 1), c ^ (k & 1))
            copies.append(pltpu.make_async_remote_copy(
                src_ref=buf_ref, dst_ref=all_ref.at[me], send_sem=send_sems.at[k - 1], recv_sem=recv_sems.at[k - 1],
                device_id=peer, device_id_type=MESH))
        for cp in copies:
            cp.start()
        all_ref[me] = buf_ref[...]
        for cp in copies:
            cp.wait()
        total = all_ref[0]
        for d in range(1, N_DEV):
            total = total + all_ref[d]
        out_ref[...] = total

    return pl.pallas_call(
        body, name="allreduce_small", in_specs=[vmem], out_specs=vmem,
        out_shape=jax.ShapeDtypeStruct((r, LANES), F32),
        scratch_shapes=[pltpu.VMEM((N_DEV, r, LANES), F32), pltpu.SemaphoreType.DMA((N_DEV - 1,)),
                        pltpu.SemaphoreType.DMA((N_DEV - 1,))],
    )(buf)


def _head_major(t, groups):
    r = t.shape[0]
    return jnp.transpose(t.reshape(r, groups, HEADS, DH), (0, 2, 1, 3)).reshape(r, groups * HEADS * DH)


def _group_major(t, groups):
    r = t.shape[0]
    return jnp.transpose(t.reshape(r, HEADS, groups, DH), (0, 2, 1, 3)).reshape(r, groups * HEADS * DH)


def _cols_from_shards(g):
    return jnp.transpose(g, (1, 0, 2)).reshape(g.shape[1], N_DEV * g.shape[2])


def _cols_to_shards(t):
    r, c = t.shape
    return jnp.transpose(t.reshape(r, N_DEV, c // N_DEV), (1, 0, 2))


def _interleave_ff(t):
    r = t.shape[0]
    nb = D_FF // FF_BLK
    return jnp.transpose(t.reshape(r, 2, nb, FF_BLK), (0, 2, 1, 3)).reshape(r, 2 * D_FF)


def _deinterleave_ff(t):
    r = t.shape[0]
    nb = D_FF // FF_BLK
    return jnp.transpose(t.reshape(r, nb, 2, FF_BLK), (0, 2, 1, 3)).reshape(r, 2 * D_FF)


def _rows128(t):
    return t.reshape(-1, LANES)


def _pad_lanes(t):
    t = t.reshape(1, -1)
    return jnp.pad(t, ((0, 0), (0, LANES - t.shape[1])))


def kernel(x, meta_tokens, lb_logits, mix_norm_w, w_in, hg_norm_w, gd_conv_w, gd_a_log, gd_dt_bias, gd_norm_w, w_branch_a, w_branch_b, w_out, ffn_norm_w, w_ffn_in, w_ffn_out, final_norm_w, loss_target, m_meta_tokens, m_lb_logits, m_mix_norm_w, m_w_in, m_hg_norm_w, m_gd_conv_w, m_gd_a_log, m_gd_dt_bias, m_gd_norm_w, m_w_branch_a, m_w_branch_b, m_w_out, m_ffn_norm_w, m_w_ffn_in, m_w_ffn_out, m_final_norm_w, v_meta_tokens, v_lb_logits, v_mix_norm_w, v_w_in, v_hg_norm_w, v_gd_conv_w, v_gd_a_log, v_gd_dt_bias, v_gd_norm_w, v_w_branch_a, v_w_branch_b, v_w_out, v_ffn_norm_w, v_w_ffn_in, v_w_ffn_out, v_final_norm_w):
    xs = x[0]
    tgt = loss_target[0]
    l = xs.shape[0]
    lp = PRE + l
    cx, cy, cc = _place()
    me = _dev(cx, cy, cc)
    my_chip = 2 * cx + cy

    big = [w_in[0], w_branch_a[0], w_branch_b[0], w_out[0], w_ffn_in[0], w_ffn_out[0]]
    gathered = gather_blocks([t.astype(BF16) for t in big] + [meta_tokens, gd_conv_w[0]])
    g_in, g_ba, g_bb, g_out, g_ffi, g_ffo, g_meta, g_conv = gathered
    w_in_full = _cols_from_shards(g_in)
    w_main = jnp.concatenate([_head_major(w_in_full[:, 0:4096], 4), _head_major(w_in_full[:, 4096:8192], 4),
                              w_in_full[:, 8208:]], axis=1)
    w_ab = jnp.pad(w_in_full[:, 8192:8208], ((0, 0), (0, LANES - 2 * HEADS)))
    wa = _cols_from_shards(g_ba)
    wb = _cols_from_shards(g_bb)
    wo = g_out.reshape(D, D)
    w_gu = _interleave_ff(_cols_from_shards(g_ffi))
    w_fo = g_ffo.reshape(D_FF, D)
    meta_full = _cols_from_shards(g_meta)
    conv_full = _cols_from_shards(g_conv)
    cw = _head_major(conv_full, 3)
    pre = jnp.concatenate([jnp.zeros((PRE - N_META, D), F32), meta_full], axis=0)
    l0, l1 = lb_logits[0:1], lb_logits[1:2]
    alog, dtb = _pad_lanes(gd_a_log), _pad_lanes(gd_dt_bias)

    xn = norm1_fwd(pre, xs, mix_norm_w)
    proj = matmul("proj_main", xn, w_main, "nn", lp, 12288, D, F32)
    proj_ab = matmul("proj_ab", xn, w_ab, "nn", lp, LANES, D, F32)
    o_a, st_a = hgrn2_fwd(proj, l0, l1, hg_norm_w)
    o_b, st_b = gdn_fwd(proj, proj_ab, cw, alog, dtb, gd_norm_w)
    ya = matmul("branch_a", o_a, wa, "nn", l, D, HEADS * DH, F32, a_off=(PRE // 512, 0))
    yb = matmul("branch_b", o_b, wb, "nn", l, D, HEADS * DH, F32, a_off=(PRE // 512, 0))
    merged = merge_fwd(proj, ya, yb)
    mo = matmul("mix_out", merged, wo, "nn", l, D, D, F32)
    h1, xn2 = norm2_fwd(xs, mo, ffn_norm_w)
    gu = matmul("ffn_in", xn2, w_gu, "nn", l, 2 * D_FF, D, F32)
    act = swiglu_fwd(gu)
    f = matmul("ffn_out", act, w_fo, "nn", l, D, D_FF, F32, tn=512)
    loss_part, dh2, d_final_w = loss_head(h1, f, final_norm_w.reshape(1, D), tgt)

    dact = matmul("d_act", dh2, w_fo, "nt", l, D_FF, D, F32, tn=512)
    dw_fo = matmul("dw_ffn_out", act, dh2, "tn", D_FF, D, l, BF16, tm=512, tn=1024, tk=2048)
    dgu = swiglu_bwd(gu, dact)
    dxn2 = matmul("d_xn2", dgu, w_gu, "nt", l, D, 2 * D_FF, F32, tn=512, tk=D_FF)
    dw_gu = matmul("dw_ffn_in", xn2, dgu, "tn", D, 2 * D_FF, l, BF16, tm=1024, tn=1024, tk=2048)
    dh1, d_ffn_norm_w = norm2_bwd(h1, ffn_norm_w, dxn2, dh2)
    dmerged = matmul("d_merged", dh1, wo, "nt", l, D, D, F32)
    dw_o = matmul("dw_out", merged, dh1, "tn", D, D, l, BF16, tm=1024, tn=1024, tk=2048)
    dgates, dya, dyb = merge_bwd(proj, ya, yb, dmerged)
    do_a = matmul("d_o_a", dya, wa, "nt", lp, HEADS * DH, D, F32)
    do_b = matmul("d_o_b", dyb, wb, "nt", lp, HEADS * DH, D, F32)
    dw_a = matmul("dw_branch_a", o_a, dya, "tn", HEADS * DH, D, lp, BF16, tm=1024, tn=1024, tk=lp // 4)
    dw_b = matmul("dw_branch_b", o_b, dyb, "tn", HEADS * DH, D, lp, BF16, tm=1024, tn=1024, tk=lp // 4)
    dproj_hg, dl0, dl1, d_hg_nw = hgrn2_bwd(proj, l0, l1, hg_norm_w, st_a, do_a)
    dproj_gd, dproj_ab, dcw, d_alog, d_dtb, d_gd_nw = gdn_bwd(proj, proj_ab, cw, alog, dtb, gd_norm_w, st_b, do_b)
    dproj = jnp.concatenate([dproj_hg, dproj_gd, dgates], axis=1)
    dxn = matmul("d_xn", dproj, w_main, "nt", lp, D, 12288, F32, tn=512, tk=4096)
    dxn_ab = matmul("d_xn_ab", dproj_ab, w_ab, "nt", lp, D, LANES, F32)
    dw_main = matmul("dw_in_main", xn, dproj, "tn", D, 12288, lp, BF16, tm=1024, tn=1024, tk=lp // 4)
    dw_ab = matmul("dw_in_ab", xn, dproj_ab, "tn", D, LANES, lp, BF16, tm=1024, tk=lp // 4)
    grad_x, dpre, d_mix_norm_w = norm1_bwd(pre, xs, mix_norm_w, dxn, dxn_ab, dh1)

    dw_in_full = jnp.concatenate([_group_major(dw_main[:, 0:4096], 4), _group_major(dw_main[:, 4096:8192], 4),
                                  dw_ab[:, :2 * HEADS], dw_main[:, 8192:]], axis=1)
    parts = [_cols_to_shards(dw_in_full), _cols_to_shards(dw_a), _cols_to_shards(dw_b),
             dw_o.reshape(N_DEV, D // N_DEV, D), _cols_to_shards(_deinterleave_ff(dw_gu)),
             dw_fo.reshape(N_DEV, D_FF // N_DEV, D)]
    parts = [p.reshape((4, 2) + p.shape[1:]) for p in parts]
    from_core = exchange_core(parts)
    tiles = [256, 256, 256, 64, 256, 176]
    chip_sums = [pair_add(f"pair_add_{a}", lax.dynamic_index_in_dim(p, cc, axis=1, keepdims=False), r, t)
                 for a, (p, r, t) in enumerate(zip(parts, from_core, tiles))]
    from_chips = exchange_chips(chip_sums)
    owns = [lax.dynamic_index_in_dim(s, my_chip, axis=0, keepdims=False) for s in chip_sums]
    names = ["w_in", "w_branch_a", "w_branch_b", "w_out", "w_ffn_in", "w_ffn_out"]
    moms = [(m_w_in, v_w_in), (m_w_branch_a, v_w_branch_a), (m_w_branch_b, v_w_branch_b), (m_w_out, v_w_out),
            (m_w_ffn_in, v_w_ffn_in), (m_w_ffn_out, v_w_ffn_out)]
    upd = {}
    for nm, w, (m, v), own, recv, t in zip(names, big, moms, owns, from_chips, tiles):
        g, d, nm_, nv_ = adamw_shard("adamw_" + nm, w, m[0], v[0], own, recv, t)
        upd[nm] = tuple(t_[None] for t_ in (g, d, nm_, nv_))

    d_lb = jnp.concatenate([dl0.reshape(1, HEADS * DH), dl1.reshape(1, HEADS * DH)], axis=0)
    d_hg_nw = jnp.sum(d_hg_nw, axis=0)
    d_conv = _group_major(dcw.reshape(HEADS, CONV_K, 3, DH).transpose(1, 0, 2, 3).reshape(CONV_K, 3 * HEADS * DH), 3)
    rep = [d_lb, d_mix_norm_w, d_hg_nw, d_alog, d_dtb, d_gd_nw, d_ffn_norm_w, d_final_w]
    buf = jnp.concatenate([_rows128(t) for t in rep] + [loss_part, _rows128(dpre[PRE - N_META:]), _rows128(d_conv)],
                          axis=0)
    n_rep = sum(_rows128(t).shape[0] for t in rep)
    n_rows = buf.shape[0]
    buf = jnp.pad(buf, ((0, -n_rows % 8), (0, 0)))
    tot = allreduce_small(buf)
    loss = tot[n_rep, 0]
    g_meta_full = tot[n_rep + 1:n_rep + 1 + N_META * D // LANES].reshape(N_META, D)
    g_conv_full = tot[n_rep + 1 + N_META * D // LANES:n_rows].reshape(CONV_K, 3 * HEADS * DH)
    g_meta_mine = lax.dynamic_slice_in_dim(g_meta_full, me * (D // N_DEV), D // N_DEV, axis=1)
    g_conv_mine = lax.dynamic_slice_in_dim(g_conv_full, me * (3 * DH), 3 * DH, axis=1)

    small = [("lb_logits", lb_logits, m_lb_logits, v_lb_logits), ("mix_norm_w", mix_norm_w, m_mix_norm_w, v_mix_norm_w),
             ("hg_norm_w", hg_norm_w, m_hg_norm_w, v_hg_norm_w), ("gd_a_log", gd_a_log, m_gd_a_log, v_gd_a_log),
             ("gd_dt_bias", gd_dt_bias, m_gd_dt_bias, v_gd_dt_bias), ("gd_norm_w", gd_norm_w, m_gd_norm_w, v_gd_norm_w),
             ("ffn_norm_w", ffn_norm_w, m_ffn_norm_w, v_ffn_norm_w),
             ("final_norm_w", final_norm_w, m_final_norm_w, v_final_norm_w),
             ("meta_tokens", meta_tokens, m_meta_tokens, v_meta_tokens), ("gd_conv_w", gd_conv_w, m_gd_conv_w, v_gd_conv_w)]

    def pack(t):
        return _pad_lanes(t) if t.size < LANES else _rows128(t)

    sizes = [pack(w).shape[0] for _, w, _, _ in small]
    g_small = jnp.concatenate([tot[:n_rep], _rows128(g_meta_mine), _rows128(g_conv_mine)], axis=0)
    n_small = g_small.shape[0]
    assert n_small == sum(sizes)
    padr = lambda t: jnp.pad(t, ((0, -n_small % 8), (0, 0)))
    w_small = padr(jnp.concatenate([pack(w) for _, w, _, _ in small], axis=0))
    m_small = padr(jnp.concatenate([pack(m) for _, _, m, _ in small], axis=0))
    v_small = padr(jnp.concatenate([pack(v) for _, _, _, v in small], axis=0))
    g_small = padr(g_small)
    d_small, nm_small, nv_small = adamw_small(w_small, g_small, m_small, v_small)
    row = 0
    for (nm, w, _, _), sz in zip(small, sizes):
        def unpack(t):
            return t[row:row + sz].reshape(-1)[:w.size].reshape(w.shape)
        upd[nm] = (unpack(g_small), unpack(d_small), unpack(nm_small), unpack(nv_small))
        row += sz

    order = ["meta_tokens", "lb_logits", "mix_norm_w", "w_in", "hg_norm_w", "gd_conv_w", "gd_a_log", "gd_dt_bias",
             "gd_norm_w", "w_branch_a", "w_branch_b", "w_out", "ffn_norm_w", "w_ffn_in", "w_ffn_out", "final_norm_w"]
    outs = [loss, grad_x[None]]
    for j in range(4):
        outs += [upd[nm][j] for nm in order]
    return tuple(outs)
```

```python
import functools

import jax
import jax.numpy as jnp
from jax import lax
from jax.experimental import pallas as pl
from jax.experimental.pallas import tpu as pltpu

F32 = jnp.float32
BF16 = jnp.bfloat16
MESH = pl.DeviceIdType.MESH

EPS = 1e-6
D = 2048
N_META = 16
CHUNK = 64
SUB = 16
HEADS = 8
DH = 128
HW = HEADS * DH
CONV_K = 4
TAIL = 8
D_FF = 5632
PRE = 512
N_SKIP = PRE // CHUNK - 1
N_DEV = 8
LANES = 128
FF_BLK = 512
W_IN = 4 * HW + 4 * HW + 2 * HEADS + 2 * D
W_MAIN = 4 * HW + 4 * HW + 2 * D

ADAM_LR = 0.001
ADAM_B1 = 0.9
ADAM_B2 = 0.999
ADAM_EPS = 1e-08
ADAM_WD = 0.01
ADAM_STEP = 10

VMEM_LIMIT = 52 * 1024 * 1024

_NN = ((1,), (0,))
_NT = ((1,), (1,))
_TN = ((0,), (0,))


def _params(*sem):
    return pltpu.CompilerParams(dimension_semantics=sem, vmem_limit_bytes=VMEM_LIMIT)


BF16_ONCE, SPLIT, EXACT_LHS = 0, 1, 2


def _dot_bf16(a, b, dims):
    if a.ndim == 3:
        dn = (((dims[0][0] + 1,), (dims[1][0] + 1,)), ((0,), (0,)))
    else:
        dn = (dims, ((), ()))
    return lax.dot_general(a, b, dn, preferred_element_type=F32)


def _split(t):
    hi = t.astype(BF16)
    return hi, (t - hi.astype(F32)).astype(BF16)


def _raw_dot(a, b, dims, mode):
    if mode == BF16_ONCE:
        return _dot_bf16(a.astype(BF16), b.astype(BF16), dims)
    if mode == SPLIT:
        a_hi, a_lo = _split(a)
        b_hi, b_lo = _split(b)
        return _dot_bf16(a_hi, b_hi, dims) + (_dot_bf16(a_hi, b_lo, dims) + _dot_bf16(a_lo, b_hi, dims))
    a = a.astype(BF16)
    b_hi = b.astype(BF16)
    rest = b - b_hi.astype(F32)
    b_mid = rest.astype(BF16)
    b_lo = (rest - b_mid.astype(F32)).astype(BF16)
    return _dot_bf16(a, b_hi, dims) + (_dot_bf16(a, b_mid, dims) + _dot_bf16(a, b_lo, dims))


@functools.partial(jax.custom_vjp, nondiff_argnums=(2,))
def mm_nn(a, b, mode=BF16_ONCE):
    return _raw_dot(a, b, _NN, mode)


@functools.partial(jax.custom_vjp, nondiff_argnums=(2,))
def mm_nt(a, b, mode=BF16_ONCE):
    return _raw_dot(a, b, _NT, mode)


@functools.partial(jax.custom_vjp, nondiff_argnums=(2,))
def mm_tn(a, b, mode=BF16_ONCE):
    return _raw_dot(a, b, _TN, mode)


def _general(mode):
    return SPLIT if mode == EXACT_LHS else mode


mm_nn.defvjp(lambda a, b, p: (_raw_dot(a, b, _NN, p), (a, b)),
             lambda p, res, g: (mm_nt(g, res[1], _general(p)), mm_tn(res[0], g, p)))
mm_nt.defvjp(lambda a, b, p: (_raw_dot(a, b, _NT, p), (a, b)),
             lambda p, res, g: (mm_nn(g, res[1], p), mm_tn(g, res[0], p)))
mm_tn.defvjp(lambda a, b, p: (_raw_dot(a, b, _TN, p), (a, b)),
             lambda p, res, g: (mm_nt(res[1], g, _general(p)), mm_nn(res[0], g, _general(p))))


def _sigmoid(x):
    return 1.0 / (1.0 + jnp.exp(-x))


def _silu(x):
    return x * _sigmoid(x)


def _softplus(x):
    return jnp.maximum(x, 0.0) + jnp.log(1.0 + jnp.exp(-jnp.abs(x)))


def _l2n(t):
    return t * lax.rsqrt(jnp.sum(t * t, axis=-1, keepdims=True) + EPS)


def _rms_norm(x, w):
    return x * lax.rsqrt(jnp.mean(x * x, axis=-1, keepdims=True) + EPS) * w


def _gated_norm(o, gate, nw):
    o = o * lax.rsqrt(jnp.mean(o * o, axis=-1, keepdims=True) + EPS) * nw
    return o * _silu(gate)


def _heads(ref, piece):
    return jnp.stack([ref[:, piece * HW + DH * j:piece * HW + DH * (j + 1)] for j in range(HEADS)])


def _put_heads(ref, piece, value, accumulate=False):
    for j in range(HEADS):
        cols = slice(piece * HW + DH * j, piece * HW + DH * (j + 1))
        if accumulate:
            ref[:, cols] += value[j]
        else:
            ref[:, cols] = value[j].astype(ref.dtype)


def hg_chunk(hq, hf, hi, hg, l0, l1, nw, st):
    nb = hq.shape[0]
    m = jnp.maximum(l0, l1)
    e0 = jnp.exp(l0 - m)
    e1 = jnp.exp(l1 - m)
    lb = e0 / (e0 + e1)
    sig = _sigmoid(hf)
    q = _silu(hq)
    log_f = jnp.log(lb + (1.0 - lb) * sig)
    k = (1.0 - lb) * _sigmoid(-hf)
    v = hi
    rows = lax.broadcasted_iota(jnp.int32, (nb, CHUNK, CHUNK), 1)
    cols = lax.broadcasted_iota(jnp.int32, (nb, CHUNK, CHUNK), 2)
    b = mm_nn((rows >= cols).astype(F32), log_f, EXACT_LHS)
    o_inter = mm_nt(q * jnp.exp(b), st)
    tri_s = (lax.broadcasted_iota(jnp.int32, (nb, SUB, SUB, DH), 1)
             >= lax.broadcasted_iota(jnp.int32, (nb, SUB, SUB, DH), 2))
    outs = []
    for i in range(CHUNK // SUB):
        lo = i * SUB
        b_i, q_i, k_i, v_i = b[:, lo:lo + SUB], q[:, lo:lo + SUB], k[:, lo:lo + SUB], v[:, lo:lo + SUB]
        diff = b_i[:, :, None, :] - b_i[:, None, :, :]
        dec = jnp.where(tri_s, jnp.exp(jnp.minimum(diff, 0.0)), 0.0)
        s_ii = jnp.sum(q_i[:, :, None, :] * k_i[:, None, :, :] * dec, axis=-1)
        o_i = mm_nn(s_ii, v_i)
        if i > 0:
            b_ref = jnp.sum(log_f[:, :lo], axis=1, keepdims=True)
            q_t = q_i * jnp.exp(b_i - b_ref)
            k_t = k[:, :lo] * jnp.exp(b_ref - b[:, :lo])
            o_i = o_i + mm_nn(mm_nt(q_t, k_t), v[:, :lo])
        outs.append(o_i)
    o = o_inter + jnp.concatenate(outs, axis=1)
    b_last = jnp.sum(log_f, axis=1, keepdims=True)
    st_new = st * jnp.exp(b_last) + mm_tn(v, k * jnp.exp(b_last - b))
    return _gated_norm(o, hg, nw), st_new


def hgrn2_fwd(proj, l0, l1, nw):
    lp = proj.shape[0]
    nc = lp // CHUNK

    def body(x_ref, l0_ref, l1_ref, nw_ref, o_ref, st_out_ref, st_ref):
        c = pl.program_id(0)

        @pl.when(c == 0)
        def _():
            st_ref[...] = jnp.zeros(st_ref.shape, F32)

        @pl.when(c < N_SKIP)
        def _():
            o_ref[...] = jnp.zeros(o_ref.shape, o_ref.dtype)
            st_out_ref[...] = jnp.zeros(st_out_ref.shape, F32)

        @pl.when(c >= N_SKIP)
        def _():
            st = st_ref[...]
            st_out_ref[0] = st
            o, st_new = hg_chunk(_heads(x_ref, 0), _heads(x_ref, 1), _heads(x_ref, 2), _heads(x_ref, 3),
                                 _heads(l0_ref, 0), _heads(l1_ref, 0), nw_ref[...], st)
            _put_heads(o_ref, 0, o)
            st_ref[...] = st_new

    vec = pl.BlockSpec((1, HW), lambda c: (0, 0))
    return pl.pallas_call(
        body, name="hgrn2_fwd", grid=(nc,),
        in_specs=[pl.BlockSpec((CHUNK, 4 * HW), lambda c: (c, 0)), vec, vec, pl.BlockSpec((1, DH), lambda c: (0, 0))],
        out_specs=[pl.BlockSpec((CHUNK, HW), lambda c: (c, 0)),
                   pl.BlockSpec((1, HEADS, DH, DH), lambda c: (c, 0, 0, 0))],
        out_shape=[jax.ShapeDtypeStruct((lp, HW), BF16), jax.ShapeDtypeStruct((nc, HEADS, DH, DH), F32)],
        scratch_shapes=[pltpu.VMEM((HEADS, DH, DH), F32)],
        compiler_params=_params("arbitrary"),
    )(proj, l0, l1, nw)


def hgrn2_bwd(proj, l0, l1, nw, states, do, dproj):
    lp = proj.shape[0]
    nc = lp // CHUNK

    def body(x_ref, l0_ref, l1_ref, nw_ref, st_in_ref, do_ref, _, dx_ref, dl0_ref, dl1_ref, dnw_ref, dst_ref):
        c = pl.program_id(0)
        cc = nc - 1 - c

        @pl.when(c == 0)
        def _():
            dst_ref[...] = jnp.zeros(dst_ref.shape, F32)
            dl0_ref[...] = jnp.zeros(dl0_ref.shape, F32)
            dl1_ref[...] = jnp.zeros(dl1_ref.shape, F32)
            dnw_ref[...] = jnp.zeros(dnw_ref.shape, F32)

        @pl.when(cc < N_SKIP)
        def _():
            dx_ref[...] = jnp.zeros(dx_ref.shape, dx_ref.dtype)

        @pl.when(cc >= N_SKIP)
        def _():
            _, vjp = jax.vjp(hg_chunk, _heads(x_ref, 0), _heads(x_ref, 1), _heads(x_ref, 2), _heads(x_ref, 3),
                             _heads(l0_ref, 0), _heads(l1_ref, 0), nw_ref[...], st_in_ref[0])
            grads = vjp((_heads(do_ref, 0), dst_ref[...]))
            for i in range(4):
                _put_heads(dx_ref, i, grads[i])
            _put_heads(dl0_ref, 0, grads[4], accumulate=True)
            _put_heads(dl1_ref, 0, grads[5], accumulate=True)
            dnw_ref[...] += grads[6]
            dst_ref[...] = grads[7]

    rc = lambda c: nc - 1 - c
    vec = pl.BlockSpec((1, HW), lambda c: (0, 0))
    one = pl.BlockSpec((1, DH), lambda c: (0, 0))
    return pl.pallas_call(
        body, name="hgrn2_bwd", grid=(nc,),
        in_specs=[pl.BlockSpec((CHUNK, 4 * HW), lambda c: (rc(c), 0)), vec, vec, one,
                  pl.BlockSpec((1, HEADS, DH, DH), lambda c: (rc(c), 0, 0, 0)),
                  pl.BlockSpec((CHUNK, HW), lambda c: (rc(c), 0)), pl.BlockSpec(memory_space=pl.ANY)],
        out_specs=[pl.BlockSpec((CHUNK, 4 * HW), lambda c: (rc(c), 0)), vec, vec, one],
        out_shape=[jax.ShapeDtypeStruct((lp, W_MAIN), BF16), jax.ShapeDtypeStruct((1, HW), F32),
                   jax.ShapeDtypeStruct((1, HW), F32), jax.ShapeDtypeStruct((1, DH), F32)],
        input_output_aliases={6: 0},
        scratch_shapes=[pltpu.VMEM((HEADS, DH, DH), F32)],
        compiler_params=_params("arbitrary"),
    )(proj, l0, l1, nw, states, do, dproj)


def _unit_lower_inverse(a):
    n = a.shape[-1]
    eye = (lax.broadcasted_iota(jnp.int32, a.shape, 1) == lax.broadcasted_iota(jnp.int32, a.shape, 2)).astype(F32)
    x = eye - a
    p = mm_nn(a, a, SPLIT)
    x = x + mm_nn(x, p, SPLIT)
    power = 4
    while power < n:
        p = mm_nn(p, p, SPLIT)
        x = x + mm_nn(x, p, SPLIT)
        power *= 2
    return x


def gd_chunk(cq, ck, cv, z, ab, alog, dtb, nw, s):
    nb, c, _ = cq.shape
    lane = lax.broadcasted_iota(jnp.int32, (nb, 1, DH), 2)
    head = lax.broadcasted_iota(jnp.int32, (nb, 1, DH), 0)
    pick = lambda t, off: jnp.sum(jnp.where(lane == head + off, t[None], 0.0), axis=2, keepdims=True)
    a_raw, b_raw = pick(ab, 0), pick(ab, HEADS)
    beta = _sigmoid(b_raw)
    g = -jnp.exp(pick(alog, 0)) * _softplus(a_raw + pick(dtb, 0))
    q = _l2n(_silu(cq)) * (DH ** -0.5)
    k = _l2n(_silu(ck))
    v = _silu(cv)
    rows = lax.broadcasted_iota(jnp.int32, (nb, c, c), 1)
    cols = lax.broadcasted_iota(jnp.int32, (nb, c, c), 2)
    tril = (rows >= cols).astype(F32)
    gc_w = mm_nn(tril, jnp.broadcast_to(g, (nb, c, DH)), EXACT_LHS)
    g_sq = jnp.broadcast_to(g, (nb, c, c))
    gc_col = mm_nn(tril, g_sq, EXACT_LHS)
    gc_row = mm_nn(jnp.ones((nb, c, c), F32), g_sq * (rows <= cols).astype(F32), EXACT_LHS)
    rel = jnp.exp(jnp.minimum(gc_col - gc_row, 0.0))
    a = jnp.where(rows > cols, beta * mm_nt(k, k) * rel, 0.0)
    t_inv = _unit_lower_inverse(a)
    e_gc = jnp.exp(gc_w)
    w = mm_nn(t_inv, beta * e_gc * k, SPLIT)
    u = mm_nn(t_inv, beta * v, SPLIT)
    v_new = u - mm_nn(w, s)
    attn = jnp.where(rows >= cols, mm_nt(q, k) * rel, 0.0)
    o = mm_nn(q * e_gc, s) + mm_nn(attn, v_new)
    g_last = jnp.sum(g, axis=1, keepdims=True)
    s_new = jnp.exp(g_last) * s + mm_tn(k * jnp.exp(g_last - gc_w), v_new)
    return _gated_norm(o, z, nw), s_new


def _gd_conv(x_ref, tail_ref, cw_ref, xe_ref, first):
    xe_ref[0:TAIL, :] = jnp.where(first, 0.0, tail_ref[:, 0:3 * HW])
    xe_ref[TAIL:TAIL + CHUNK, :] = x_ref[:, 0:3 * HW]
    y = cw_ref[pl.ds(0, 1), :] * xe_ref[pl.ds(TAIL - CONV_K + 1, CHUNK), :]
    for k in range(1, CONV_K):
        y = y + cw_ref[pl.ds(k, 1), :] * xe_ref[pl.ds(TAIL - CONV_K + 1 + k, CHUNK), :]
    return y


def _conv_heads(y, piece):
    return jnp.stack([y[:, piece * HW + DH * j:piece * HW + DH * (j + 1)] for j in range(HEADS)])


def gdn_fwd(proj, proj_ab, cw, alog, dtb, nw):
    lp = proj.shape[0]
    nc = lp // CHUNK

    def body(x_ref, tail_ref, ab_ref, cw_ref, alog_ref, dtb_ref, nw_ref, o_ref, st_out_ref, st_ref, xe_ref):
        c = pl.program_id(0)

        @pl.when(c == 0)
        def _():
            st_ref[...] = jnp.zeros(st_ref.shape, F32)

        @pl.when(c < N_SKIP)
        def _():
            o_ref[...] = jnp.zeros(o_ref.shape, o_ref.dtype)
            st_out_ref[...] = jnp.zeros(st_out_ref.shape, F32)

        @pl.when(c >= N_SKIP)
        def _():
            st = st_ref[...]
            st_out_ref[0] = st
            y = _gd_conv(x_ref, tail_ref, cw_ref, xe_ref, c == 0)
            o, st_new = gd_chunk(_conv_heads(y, 0), _conv_heads(y, 1), _conv_heads(y, 2), _heads(x_ref, 3),
                                 ab_ref[...], alog_ref[...], dtb_ref[...], nw_ref[...], st)
            _put_heads(o_ref, 0, o)
            st_ref[...] = st_new

    one = pl.BlockSpec((1, DH), lambda c: (0, 0))
    return pl.pallas_call(
        body, name="gdn_fwd", grid=(nc,),
        in_specs=[pl.BlockSpec((CHUNK, 4 * HW), lambda c: (c, 1)),
                  pl.BlockSpec((TAIL, 4 * HW), lambda c: (jnp.maximum(c * (CHUNK // TAIL) - 1, 0), 1)),
                  pl.BlockSpec((CHUNK, DH), lambda c: (c, 0)),
                  pl.BlockSpec((CONV_K, 3 * HW), lambda c: (0, 0)), one, one, one],
        out_specs=[pl.BlockSpec((CHUNK, HW), lambda c: (c, 0)),
                   pl.BlockSpec((1, HEADS, DH, DH), lambda c: (c, 0, 0, 0))],
        out_shape=[jax.ShapeDtypeStruct((lp, HW), BF16), jax.ShapeDtypeStruct((nc, HEADS, DH, DH), F32)],
        scratch_shapes=[pltpu.VMEM((HEADS, DH, DH), F32), pltpu.VMEM((TAIL + CHUNK, 3 * HW), F32)],
        compiler_params=_params("arbitrary"),
    )(proj, proj, proj_ab, cw, alog, dtb, nw)


def gdn_bwd(proj, proj_ab, cw, alog, dtb, nw, states, do, dproj):
    lp = proj.shape[0]
    nc = lp // CHUNK

    def body(x_ref, tail_ref, ab_ref, cw_ref, alog_ref, dtb_ref, nw_ref, st_in_ref, do_ref, _,
             dx_ref, dab_ref, dcw_ref, dalog_ref, ddtb_ref, dnw_ref, dst_ref, xe_ref, dye_ref, head_ref):
        c = pl.program_id(0)
        cc = nc - 1 - c

        @pl.when(c == 0)
        def _():
            dcw_ref[...] = jnp.zeros(dcw_ref.shape, F32)
            dalog_ref[...] = jnp.zeros(dalog_ref.shape, F32)
            ddtb_ref[...] = jnp.zeros(ddtb_ref.shape, F32)
            dnw_ref[...] = jnp.zeros(dnw_ref.shape, F32)
            dst_ref[...] = jnp.zeros(dst_ref.shape, F32)
            head_ref[...] = jnp.zeros(head_ref.shape, F32)

        @pl.when(cc < N_SKIP)
        def _():
            dx_ref[...] = jnp.zeros(dx_ref.shape, dx_ref.dtype)
            dab_ref[...] = jnp.zeros(dab_ref.shape, F32)

        @pl.when(cc >= N_SKIP)
        def _():
            y = _gd_conv(x_ref, tail_ref, cw_ref, xe_ref, cc == 0)
            _, vjp = jax.vjp(gd_chunk, _conv_heads(y, 0), _conv_heads(y, 1), _conv_heads(y, 2), _heads(x_ref, 3),
                             ab_ref[...], alog_ref[...], dtb_ref[...], nw_ref[...], st_in_ref[0])
            dcq, dck, dcv, dz, dab, dalog, ddtb, dnw, dst = vjp((_heads(do_ref, 0), dst_ref[...]))
            dst_ref[...] = dst
            dab_ref[...] = dab
            dalog_ref[...] += dalog
            ddtb_ref[...] += ddtb
            dnw_ref[...] += dnw
            _put_heads(dx_ref, 3, dz)

            for i, t in enumerate((dcq, dck, dcv)):
                for j in range(HEADS):
                    dye_ref[0:CHUNK, i * HW + DH * j:i * HW + DH * (j + 1)] = t[j]
            dye_ref[CHUNK:CHUNK + TAIL, :] = head_ref[...]
            head_ref[...] = dye_ref[0:TAIL, :]
            dy = dye_ref[0:CHUNK, :]
            dx = None
            for k in range(CONV_K):
                term = cw_ref[pl.ds(k, 1), :] * dye_ref[pl.ds(CONV_K - 1 - k, CHUNK), :]
                dx = term if dx is None else dx + term
                dcw_ref[pl.ds(k, 1), :] += jnp.sum(dy * xe_ref[pl.ds(TAIL - CONV_K + 1 + k, CHUNK), :],
                                                   axis=0, keepdims=True)
            dx_ref[:, 0:3 * HW] = dx.astype(dx_ref.dtype)

    rc = lambda c: nc - 1 - c
    one = pl.BlockSpec((1, DH), lambda c: (0, 0))
    one_shape = jax.ShapeDtypeStruct((1, DH), F32)
    return pl.pallas_call(
        body, name="gdn_bwd", grid=(nc,),
        in_specs=[pl.BlockSpec((CHUNK, 4 * HW), lambda c: (rc(c), 1)),
                  pl.BlockSpec((TAIL, 4 * HW), lambda c: (jnp.maximum(rc(c) * (CHUNK // TAIL) - 1, 0), 1)),
                  pl.BlockSpec((CHUNK, DH), lambda c: (rc(c), 0)),
                  pl.BlockSpec((CONV_K, 3 * HW), lambda c: (0, 0)), one, one, one,
                  pl.BlockSpec((1, HEADS, DH, DH), lambda c: (rc(c), 0, 0, 0)),
                  pl.BlockSpec((CHUNK, HW), lambda c: (rc(c), 0)),
                  pl.BlockSpec(memory_space=pl.ANY)],
        out_specs=[pl.BlockSpec((CHUNK, 4 * HW), lambda c: (rc(c), 1)),
                   pl.BlockSpec((CHUNK, DH), lambda c: (rc(c), 0)),
                   pl.BlockSpec((CONV_K, 3 * HW), lambda c: (0, 0)), one, one, one],
        out_shape=[jax.ShapeDtypeStruct((lp, W_MAIN), BF16), jax.ShapeDtypeStruct((lp, DH), F32),
                   jax.ShapeDtypeStruct((CONV_K, 3 * HW), F32), one_shape, one_shape, one_shape],
        input_output_aliases={9: 0},
        scratch_shapes=[pltpu.VMEM((HEADS, DH, DH), F32), pltpu.VMEM((TAIL + CHUNK, 3 * HW), F32),
                        pltpu.VMEM((CHUNK + TAIL, 3 * HW), F32), pltpu.VMEM((TAIL, 3 * HW), F32)],
        compiler_params=_params("arbitrary"),
    )(proj, proj, proj_ab, cw, alog, dtb, nw, states, do, dproj)


def matmul(name, a, b, mode, m, n, k, out_dtype, tm=512, tn=1024, tk=None, a_off=(0, 0), b_off=(0, 0)):
    tm, tn = min(tm, m), min(tn, n)
    tk = k if tk is None else min(tk, k)
    assert m % tm == 0 and n % tn == 0 and k % tk == 0, (name, m, n, k, tm, tn, tk)
    gi, gj, gk = m // tm, n // tn, k // tk
    if mode == "nn":
        a_spec = pl.BlockSpec((tm, tk), lambda j, i, kk: (i + a_off[0], kk + a_off[1]))
        b_spec = pl.BlockSpec((tk, tn), lambda j, i, kk: (kk + b_off[0], j + b_off[1]))
        dims = _NN
    elif mode == "nt":
        a_spec = pl.BlockSpec((tm, tk), lambda j, i, kk: (i + a_off[0], kk + a_off[1]))
        b_spec = pl.BlockSpec((tn, tk), lambda j, i, kk: (j + b_off[0], kk + b_off[1]))
        dims = _NT
    else:
        a_spec = pl.BlockSpec((tk, tm), lambda j, i, kk: (kk + a_off[0], i + a_off[1]))
        b_spec = pl.BlockSpec((tk, tn), lambda j, i, kk: (kk + b_off[0], j + b_off[1]))
        dims = _TN

    def body(a_ref, b_ref, o_ref, *acc):
        d = lax.dot_general(a_ref[...].astype(BF16), b_ref[...].astype(BF16), (dims, ((), ())),
                            preferred_element_type=F32)
        if gk == 1:
            o_ref[...] = d.astype(o_ref.dtype)
        else:
            acc_ref, = acc
            kk = pl.program_id(2)

            @pl.when(kk == 0)
            def _():
                acc_ref[...] = d

            @pl.when(kk > 0)
            def _():
                acc_ref[...] += d

            @pl.when(kk == gk - 1)
            def _():
                o_ref[...] = acc_ref[...].astype(o_ref.dtype)

    return pl.pallas_call(
        body, name=name, grid=(gj, gi, gk),
        in_specs=[a_spec, b_spec],
        out_specs=pl.BlockSpec((tm, tn), lambda j, i, kk: (i, j)),
        out_shape=jax.ShapeDtypeStruct((m, n), out_dtype),
        scratch_shapes=[] if gk == 1 else [pltpu.VMEM((tm, tn), F32)],
        compiler_params=_params("parallel", "parallel", "arbitrary"),
    )(a, b)


def _swiglu(gu):
    return _silu(gu[:, :FF_BLK]) * gu[:, FF_BLK:]


def ffn_in_fused(xn2, w_gu, tm=512):
    l = xn2.shape[0]
    tn = 2 * FF_BLK

    def body(a_ref, b_ref, gu_ref, act_ref):
        gu = lax.dot_general(a_ref[...], b_ref[...], (_NN, ((), ())), preferred_element_type=F32)
        gu_ref[...] = gu
        act_ref[...] = _swiglu(gu).astype(act_ref.dtype)

    return pl.pallas_call(
        body, name="ffn_in", grid=(2 * D_FF // tn, l // tm),
        in_specs=[pl.BlockSpec((tm, D), lambda j, i: (i, 0)), pl.BlockSpec((D, tn), lambda j, i: (0, j))],
        out_specs=[pl.BlockSpec((tm, tn), lambda j, i: (i, j)), pl.BlockSpec((tm, FF_BLK), lambda j, i: (i, j))],
        out_shape=[jax.ShapeDtypeStruct((l, 2 * D_FF), F32), jax.ShapeDtypeStruct((l, D_FF), BF16)],
        compiler_params=_params("parallel", "parallel"),
    )(xn2, w_gu)


def d_act_fused(dh2, w_fo, gu, tm=512):
    l = dh2.shape[0]

    def body(a_ref, b_ref, gu_ref, o_ref):
        da = lax.dot_general(a_ref[...], b_ref[...], (_NT, ((), ())), preferred_element_type=F32)
        _, vjp = jax.vjp(_swiglu, gu_ref[...])
        dgu, = vjp(da)
        o_ref[...] = dgu.astype(o_ref.dtype)

    return pl.pallas_call(
        body, name="d_act", grid=(D_FF // FF_BLK, l // tm),
        in_specs=[pl.BlockSpec((tm, D), lambda j, i: (i, 0)), pl.BlockSpec((FF_BLK, D), lambda j, i: (j, 0)),
                  pl.BlockSpec((tm, 2 * FF_BLK), lambda j, i: (i, j))],
        out_specs=pl.BlockSpec((tm, 2 * FF_BLK), lambda j, i: (i, j)),
        out_shape=jax.ShapeDtypeStruct((l, 2 * D_FF), BF16),
        compiler_params=_params("parallel", "parallel"),
    )(dh2, w_fo, gu)


TR = 256
N_PRE = PRE // TR


def _row(width, off=0, col=0):
    return pl.BlockSpec((TR, width), lambda i: (i + off, col))


def _pre_spec():
    return pl.BlockSpec((TR, D), lambda i: (jnp.minimum(i, N_PRE - 1), 0))


def _seq_spec(width=D, col=0):
    return pl.BlockSpec((TR, width), lambda i: (jnp.maximum(i - N_PRE, 0), col))


def _vec_spec(width=D):
    return pl.BlockSpec((1, width), lambda i: (0, 0))


def norm1_fwd(pre, x, w):
    lp = PRE + x.shape[0]

    def body(pre_ref, x_ref, w_ref, o_ref):
        h = jnp.where(pl.program_id(0) < N_PRE, pre_ref[...], x_ref[...])
        o_ref[...] = _rms_norm(h, w_ref[...]).astype(o_ref.dtype)

    return pl.pallas_call(
        body, name="norm1_fwd", grid=(lp // TR,),
        in_specs=[_pre_spec(), _seq_spec(), _vec_spec()],
        out_specs=_row(D), out_shape=jax.ShapeDtypeStruct((lp, D), BF16),
        compiler_params=_params("parallel"),
    )(pre, x, w)


def norm1_bwd(pre, x, w, dxn, dxn_ab, dh1):
    l = x.shape[0]
    lp = PRE + l

    def body(pre_ref, x_ref, w_ref, dxn_ref, dxn_ab_ref, dh1_ref, dx_ref, dpre_ref, dw_ref):
        i = pl.program_id(0)
        h = jnp.where(i < N_PRE, pre_ref[...], x_ref[...])
        _, vjp = jax.vjp(_rms_norm, h, w_ref[...])
        dh, dw = vjp(dxn_ref[...] + dxn_ab_ref[...])

        @pl.when(i == 0)
        def _():
            dw_ref[...] = dw

        @pl.when(i > 0)
        def _():
            dw_ref[...] += dw

        @pl.when(i < N_PRE)
        def _():
            dpre_ref[...] = dh

        @pl.when(i >= N_PRE)
        def _():
            dx_ref[...] = dh + dh1_ref[...]

    return pl.pallas_call(
        body, name="norm1_bwd", grid=(lp // TR,),
        in_specs=[_pre_spec(), _seq_spec(), _vec_spec(), _row(D), _row(D), _seq_spec()],
        out_specs=[_seq_spec(), _pre_spec(), _vec_spec()],
        out_shape=[jax.ShapeDtypeStruct((l, D), F32), jax.ShapeDtypeStruct((PRE, D), F32),
                   jax.ShapeDtypeStruct((1, D), F32)],
        compiler_params=_params("arbitrary"),
    )(pre, x, w, dxn, dxn_ab, dh1)


def _merge(gates, ya, yb):
    return _sigmoid(gates[:, :D]) * ya + _sigmoid(gates[:, D:]) * yb


def merge_fwd(proj, ya, yb):
    l = ya.shape[0]

    def body(g_ref, ya_ref, yb_ref, o_ref):
        o_ref[...] = _merge(g_ref[...], ya_ref[...], yb_ref[...]).astype(o_ref.dtype)

    return pl.pallas_call(
        body, name="merge_fwd", grid=(l // TR,),
        in_specs=[_row(2 * D, N_PRE, 2), _row(D), _row(D)],
        out_specs=_row(D), out_shape=jax.ShapeDtypeStruct((l, D), BF16),
        compiler_params=_params("parallel"),
    )(proj, ya, yb)


def merge_bwd(proj, ya, yb, dmerged):
    l = ya.shape[0]
    lp = PRE + l

    def body(g_ref, ya_ref, yb_ref, dm_ref, dg_ref, dya_ref, dyb_ref):
        i = pl.program_id(0)

        @pl.when(i < N_PRE)
        def _():
            dg_ref[...] = jnp.zeros(dg_ref.shape, dg_ref.dtype)
            dya_ref[...] = jnp.zeros(dya_ref.shape, dya_ref.dtype)
            dyb_ref[...] = jnp.zeros(dyb_ref.shape, dyb_ref.dtype)

        @pl.when(i >= N_PRE)
        def _():
            _, vjp = jax.vjp(_merge, g_ref[...], ya_ref[...], yb_ref[...])
            dg, dya, dyb = vjp(dm_ref[...])
            dg_ref[...] = dg.astype(dg_ref.dtype)
            dya_ref[...] = dya.astype(dya_ref.dtype)
            dyb_ref[...] = dyb.astype(dyb_ref.dtype)

    return pl.pallas_call(
        body, name="merge_bwd", grid=(lp // TR,),
        in_specs=[pl.BlockSpec((TR, 2 * D), lambda i: (jnp.maximum(i, N_PRE), 2)), _seq_spec(), _seq_spec(),
                  _seq_spec()],
        out_specs=[_row(2 * D, 0, 2), _row(D), _row(D)],
        out_shape=[jax.ShapeDtypeStruct((lp, W_MAIN), BF16), jax.ShapeDtypeStruct((lp, D), BF16),
                   jax.ShapeDtypeStruct((lp, D), BF16)],
        compiler_params=_params("parallel"),
    )(proj, ya, yb, dmerged)


def norm2_fwd(x, mo, w):
    l = x.shape[0]

    def body(x_ref, mo_ref, w_ref, h_ref, o_ref):
        h = x_ref[...] + mo_ref[...]
        h_ref[...] = h
        o_ref[...] = _rms_norm(h, w_ref[...]).astype(o_ref.dtype)

    return pl.pallas_call(
        body, name="norm2_fwd", grid=(l // TR,),
        in_specs=[_row(D), _row(D), _vec_spec()],
        out_specs=[_row(D), _row(D)],
        out_shape=[jax.ShapeDtypeStruct((l, D), F32), jax.ShapeDtypeStruct((l, D), BF16)],
        compiler_params=_params("parallel"),
    )(x, mo, w)


def norm2_bwd(h1, w, dxn2, dh2):
    l = h1.shape[0]

    def body(h_ref, w_ref, dxn_ref, dh2_ref, dh1_ref, dh1b_ref, dw_ref):
        i = pl.program_id(0)
        _, vjp = jax.vjp(_rms_norm, h_ref[...], w_ref[...])
        dh, dw = vjp(dxn_ref[...])
        dh1 = dh + dh2_ref[...]
        dh1_ref[...] = dh1
        dh1b_ref[...] = dh1.astype(BF16)

        @pl.when(i == 0)
        def _():
            dw_ref[...] = dw

        @pl.when(i > 0)
        def _():
            dw_ref[...] += dw

    return pl.pallas_call(
        body, name="norm2_bwd", grid=(l // TR,),
        in_specs=[_row(D), _vec_spec(), _row(D), _row(D)],
        out_specs=[_row(D), _row(D), _vec_spec()],
        out_shape=[jax.ShapeDtypeStruct((l, D), F32), jax.ShapeDtypeStruct((l, D), BF16),
                   jax.ShapeDtypeStruct((1, D), F32)],
        compiler_params=_params("arbitrary"),
    )(h1, w, dxn2, dh2)


def _loss_rows(h1, f, w, tgt):
    y = _rms_norm(h1 + f, w)
    err = (y - tgt) * (y - tgt)
    return 0.5 * jnp.sum(jnp.mean(err, axis=-1, keepdims=True), axis=0, keepdims=True)


def loss_head(h1, f, w, tgt):
    l = h1.shape[0]

    def body(h_ref, f_ref, w_ref, t_ref, loss_ref, dh_ref, dhb_ref, dw_ref):
        i = pl.program_id(0)
        loss, vjp = jax.vjp(_loss_rows, h_ref[...], f_ref[...], w_ref[...], t_ref[...])
        dh, _, dw, _ = vjp(jnp.ones((1, 1), F32))
        dh_ref[...] = dh
        dhb_ref[...] = dh.astype(BF16)
        loss = jnp.broadcast_to(loss, (1, LANES))

        @pl.when(i == 0)
        def _():
            dw_ref[...] = dw
            loss_ref[...] = loss

        @pl.when(i > 0)
        def _():
            dw_ref[...] += dw
            loss_ref[...] += loss

    return pl.pallas_call(
        body, name="loss_head", grid=(l // TR,),
        in_specs=[_row(D), _row(D), _vec_spec(), _row(D)],
        out_specs=[_vec_spec(LANES), _row(D), _row(D), _vec_spec()],
        out_shape=[jax.ShapeDtypeStruct((1, LANES), F32), jax.ShapeDtypeStruct((l, D), F32),
                   jax.ShapeDtypeStruct((l, D), BF16), jax.ShapeDtypeStruct((1, D), F32)],
        compiler_params=_params("arbitrary"),
    )(h1, f, w, tgt)


def _adamw(w, g, m, v):
    m = ADAM_B1 * m + (1.0 - ADAM_B1) * g
    v = ADAM_B2 * v + (1.0 - ADAM_B2) * (g * g)
    m_hat = m / (1.0 - ADAM_B1 ** ADAM_STEP)
    v_hat = v / (1.0 - ADAM_B2 ** ADAM_STEP)
    delta = -ADAM_LR * (m_hat / (jnp.sqrt(v_hat) + ADAM_EPS) + ADAM_WD * w)
    return delta, m, v


def adamw_shard(name, w, m, v, own, recv, tr):
    r, c = w.shape
    assert r % tr == 0

    def body(w_ref, m_ref, v_ref, own_ref, r0_ref, r1_ref, r2_ref, g_ref, d_ref, nm_ref, nv_ref):
        g = own_ref[...].astype(F32) + r0_ref[...].astype(F32)
        g = g + r1_ref[...].astype(F32)
        g = g + r2_ref[...].astype(F32)
        d, nm, nv = _adamw(w_ref[...], g, m_ref[...], v_ref[...])
        g_ref[...] = g
        d_ref[...] = d
        nm_ref[...] = nm
        nv_ref[...] = nv

    blk = pl.BlockSpec((tr, c), lambda i: (i, 0))
    part = lambda s: pl.BlockSpec((None, tr, c), lambda i: (s, i, 0))
    shape = jax.ShapeDtypeStruct((r, c), F32)
    return pl.pallas_call(
        body, name=name, grid=(r // tr,),
        in_specs=[blk, blk, blk, blk, part(0), part(1), part(2)],
        out_specs=[blk, blk, blk, blk], out_shape=[shape, shape, shape, shape],
        compiler_params=_params("parallel"),
    )(w, m, v, own, recv, recv, recv)


def adamw_small(w, g, m, v):
    def body(w_ref, g_ref, m_ref, v_ref, d_ref, nm_ref, nv_ref):
        d, nm, nv = _adamw(w_ref[...], g_ref[...], m_ref[...], v_ref[...])
        d_ref[...] = d
        nm_ref[...] = nm
        nv_ref[...] = nv

    shape = jax.ShapeDtypeStruct(w.shape, F32)
    return pl.pallas_call(body, name="adamw_small", out_shape=[shape, shape, shape])(w, g, m, v)


def pair_add(name, a, b, tr):
    q, r, c = a.shape
    assert r % tr == 0

    def body(a_ref, b_ref, o_ref):
        o_ref[...] = (a_ref[...].astype(F32) + b_ref[...].astype(F32)).astype(o_ref.dtype)

    blk = pl.BlockSpec((None, tr, c), lambda s, i: (s, i, 0))
    return pl.pallas_call(
        body, name=name, grid=(q, r // tr), in_specs=[blk, blk], out_specs=blk,
        out_shape=jax.ShapeDtypeStruct((q, r, c), BF16), compiler_params=_params("parallel", "parallel"),
    )(a, b)


def _w_in_pieces():
    c = W_IN // N_DEV
    ranges = ((0, 8 * HW, 0, 0), (8 * HW, 8 * HW + 2 * HEADS, 1, 8 * HW), (8 * HW + 2 * HEADS, W_IN, 0, 2 * HEADS))
    out = []
    for d in range(N_DEV):
        for lo, hi, target, shift in ranges:
            s, e = max(lo, c * d), min(hi, c * (d + 1))
            if s < e:
                out.append((d, s - c * d, e - s, target, s - shift))
    return out


def _ffn_in_pieces():
    c = 2 * D_FF // N_DEV
    out = []
    for d in range(N_DEV):
        s = c * d
        while s < c * (d + 1):
            e = min((s // FF_BLK + 1) * FF_BLK, c * (d + 1))
            half, blk = divmod(s // FF_BLK, D_FF // FF_BLK)
            out.append((d, s - c * d, e - s, 0, (2 * blk + half) * FF_BLK + s % FF_BLK))
            s = e
    return out


def _plain_pieces(c):
    return [(d, 0, c, 0, c * d) for d in range(N_DEV)]


def shards_to_cols(name, g, widths, pieces, tr):
    _, r, c = g.shape
    covered = [sum(p[2] for p in pieces if p[3] == t) for t in range(len(widths))]

    def body(g_ref, *outs):
        for t, o in enumerate(outs):
            if covered[t] < widths[t]:
                o[...] = jnp.zeros(o.shape, o.dtype)
        for d, s, w, t, ts in pieces:
            outs[t][:, ts:ts + w] = g_ref[d, :, s:s + w]

    return pl.pallas_call(
        body, name=name, grid=(r // tr,),
        in_specs=[pl.BlockSpec((N_DEV, tr, c), lambda i: (0, i, 0))],
        out_specs=[pl.BlockSpec((tr, w), lambda i: (i, 0)) for w in widths],
        out_shape=[jax.ShapeDtypeStruct((r, w), g.dtype) for w in widths],
        compiler_params=_params("parallel"),
    )(g)


def cols_to_shards(name, srcs, c, pieces, tr):
    r = srcs[0].shape[0]

    def body(*refs):
        o = refs[-1]
        for d, s, w, t, ts in pieces:
            o[d, :, s:s + w] = refs[t][:, ts:ts + w]

    return pl.pallas_call(
        body, name=name, grid=(r // tr,),
        in_specs=[pl.BlockSpec((tr, t.shape[1]), lambda i: (i, 0)) for t in srcs],
        out_specs=pl.BlockSpec((N_DEV, tr, c), lambda i: (0, i, 0)),
        out_shape=jax.ShapeDtypeStruct((N_DEV, r, c), srcs[0].dtype),
        compiler_params=_params("parallel"),
    )(*srcs)


def _place():
    return lax.axis_index("x"), lax.axis_index("y"), lax.axis_index("c")


def _dev(px, py, pc):
    return 4 * px + 2 * py + pc


def gather_blocks(arrays):
    n = len(arrays)
    any_spec = pl.BlockSpec(memory_space=pl.ANY)

    def body(*refs):
        ins, outs = refs[:n], refs[n:2 * n]
        send_sems, recv_sems, local_sems = refs[2 * n:]
        x, y, c = _place()
        me, sibling = (x, y, c), (x, y, 1 - c)
        chips = [(1 - x, y), (x, 1 - y), (1 - x, 1 - y)]

        def copy(a, k, block, to, src=None):
            dst = outs[a].at[_dev(*block)]
            return pltpu.make_async_remote_copy(
                src_ref=dst if src is None else src, dst_ref=dst, send_sem=send_sems.at[a, k],
                recv_sem=recv_sems.at[a, k], device_id=to, device_id_type=MESH)

        mine = [pltpu.make_async_copy(ins[a], outs[a].at[_dev(*me)], local_sems.at[a]) for a in range(n)]
        for cp in mine:
            cp.start()
        first = []
        for a in range(n):
            first.append(copy(a, 0, me, sibling, src=ins[a]))
            first += [copy(a, 1 + j, me, (*chip, c), src=ins[a]) for j, chip in enumerate(chips)]
        for cp in first:
            cp.start()
        passed = []
        for j, chip in enumerate(chips):
            for a in range(n):
                copy(a, 1 + j, (*chip, c), me).wait_recv()
                fwd = copy(a, 4 + j, (*chip, c), sibling)
                fwd.start()
                passed.append(fwd)
        for a in range(n):
            copy(a, 0, sibling, me).wait_recv()
        for j, chip in enumerate(chips):
            for a in range(n):
                copy(a, 4 + j, (*chip, 1 - c), me).wait_recv()
        for cp in first + passed:
            cp.wait_send()
        for cp in mine:
            cp.wait()

    return pl.pallas_call(
        body, name="gather_blocks",
        in_specs=[any_spec] * n, out_specs=[any_spec] * n,
        out_shape=[jax.ShapeDtypeStruct((N_DEV,) + t.shape, t.dtype) for t in arrays],
        scratch_shapes=[pltpu.SemaphoreType.DMA((n, 7)), pltpu.SemaphoreType.DMA((n, 7)),
                        pltpu.SemaphoreType.DMA((n,))],
    )(*arrays)


def exchange_core(arrays):
    n = len(arrays)
    any_spec = pl.BlockSpec(memory_space=pl.ANY)

    def body(*refs):
        ins, outs = refs[:n], refs[n:2 * n]
        send_sems, recv_sems = refs[2 * n:]
        x, y, c = _place()
        copies = []
        for a in range(n):
            for q in range(4):
                copies.append(pltpu.make_async_remote_copy(
                    src_ref=ins[a].at[q, 1 - c], dst_ref=outs[a].at[q], send_sem=send_sems.at[a, q],
                    recv_sem=recv_sems.at[a, q], device_id=(x, y, 1 - c), device_id_type=MESH))
        for cp in copies:
            cp.start()
        for cp in copies:
            cp.wait()

    return pl.pallas_call(
        body, name="exchange_core",
        in_specs=[any_spec] * n, out_specs=[any_spec] * n,
        out_shape=[jax.ShapeDtypeStruct((4,) + t.shape[2:], t.dtype) for t in arrays],
        scratch_shapes=[pltpu.SemaphoreType.DMA((n, 4)), pltpu.SemaphoreType.DMA((n, 4))],
    )(*arrays)


def exchange_chips(arrays):
    n = len(arrays)
    any_spec = pl.BlockSpec(memory_space=pl.ANY)

    def body(*refs):
        ins, outs = refs[:n], refs[n:2 * n]
        send_sems, recv_sems = refs[2 * n:]
        x, y, c = _place()
        chips = [(1 - x, y), (x, 1 - y), (1 - x, 1 - y)]
        copies = []
        for a in range(n):
            for j, (qx, qy) in enumerate(chips):
                copies.append(pltpu.make_async_remote_copy(
                    src_ref=ins[a].at[2 * qx + qy], dst_ref=outs[a].at[j], send_sem=send_sems.at[a, j],
                    recv_sem=recv_sems.at[a, j], device_id=(qx, qy, c), device_id_type=MESH))
        for cp in copies:
            cp.start()
        for cp in copies:
            cp.wait()

    return pl.pallas_call(
        body, name="exchange_chips",
        in_specs=[any_spec] * n, out_specs=[any_spec] * n,
        out_shape=[jax.ShapeDtypeStruct((3,) + t.shape[1:], t.dtype) for t in arrays],
        scratch_shapes=[pltpu.SemaphoreType.DMA((n, 3)), pltpu.SemaphoreType.DMA((n, 3))],
    )(*arrays)


def allreduce_small(buf):
    r = buf.shape[0]
    vmem = pl.BlockSpec(memory_space=pltpu.VMEM)

    def body(buf_ref, out_ref, all_ref, send_sems, recv_sems):
        x, y, c = _place()
        me = _dev(x, y, c)
        copies = []
        for k in range(1, N_DEV):
            peer = (x ^ (k >> 2), y ^ ((k >> 1) & 1), c ^ (k & 1))
            copies.append(pltpu.make_async_remote_copy(
                src_ref=buf_ref, dst_ref=all_ref.at[me], send_sem=send_sems.at[k - 1], recv_sem=recv_sems.at[k - 1],
                device_id=peer, device_id_type=MESH))
        for cp in copies:
            cp.start()
        all_ref[me] = buf_ref[...]
        for cp in copies:
            cp.wait()
        total = all_ref[0]
        for d in range(1, N_DEV):
            total = total + all_ref[d]
        out_ref[...] = total

    return pl.pallas_call(
        body, name="allreduce_small", in_specs=[vmem], out_specs=vmem,
        out_shape=jax.ShapeDtypeStruct((r, LANES), F32),
        scratch_shapes=[pltpu.VMEM((N_DEV, r, LANES), F32), pltpu.SemaphoreType.DMA((N_DEV - 1,)),
                        pltpu.SemaphoreType.DMA((N_DEV - 1,))],
    )(buf)


def _cols_from_shards(g):
    return jnp.transpose(g, (1, 0, 2)).reshape(g.shape[1], N_DEV * g.shape[2])


def _rows128(t):
    return t.reshape(-1, LANES)


def _pad_lanes(t):
    t = t.reshape(1, -1)
    return jnp.pad(t, ((0, 0), (0, LANES - t.shape[1])))


def kernel(x, meta_tokens, lb_logits, mix_norm_w, w_in, hg_norm_w, gd_conv_w, gd_a_log, gd_dt_bias, gd_norm_w, w_branch_a, w_branch_b, w_out, ffn_norm_w, w_ffn_in, w_ffn_out, final_norm_w, loss_target, m_meta_tokens, m_lb_logits, m_mix_norm_w, m_w_in, m_hg_norm_w, m_gd_conv_w, m_gd_a_log, m_gd_dt_bias, m_gd_norm_w, m_w_branch_a, m_w_branch_b, m_w_out, m_ffn_norm_w, m_w_ffn_in, m_w_ffn_out, m_final_norm_w, v_meta_tokens, v_lb_logits, v_mix_norm_w, v_w_in, v_hg_norm_w, v_gd_conv_w, v_gd_a_log, v_gd_dt_bias, v_gd_norm_w, v_w_branch_a, v_w_branch_b, v_w_out, v_ffn_norm_w, v_w_ffn_in, v_w_ffn_out, v_final_norm_w):
    xs = x[0]
    tgt = loss_target[0]
    l = xs.shape[0]
    lp = PRE + l
    cx, cy, cc = _place()
    me = _dev(cx, cy, cc)
    my_chip = 2 * cx + cy

    big = [w_in[0], w_branch_a[0], w_branch_b[0], w_out[0], w_ffn_in[0], w_ffn_out[0]]
    gathered = gather_blocks([t.astype(BF16) for t in big] + [meta_tokens, gd_conv_w[0]])
    g_in, g_ba, g_bb, g_out, g_ffi, g_ffo, g_meta, g_conv = gathered
    w_main, w_ab = shards_to_cols("w_in_cols", g_in, [W_MAIN, LANES], _w_in_pieces(), 256)
    wa, = shards_to_cols("w_branch_a_cols", g_ba, [D], _plain_pieces(D // N_DEV), 256)
    wb, = shards_to_cols("w_branch_b_cols", g_bb, [D], _plain_pieces(D // N_DEV), 256)
    wo = g_out.reshape(D, D)
    w_gu, = shards_to_cols("w_ffn_in_cols", g_ffi, [2 * D_FF], _ffn_in_pieces(), 256)
    w_fo = g_ffo.reshape(D_FF, D)
    meta_full = _cols_from_shards(g_meta)
    cw = _cols_from_shards(g_conv)
    pre = jnp.concatenate([jnp.zeros((PRE - N_META, D), F32), meta_full], axis=0)
    l0, l1 = lb_logits[0:1], lb_logits[1:2]
    alog, dtb = _pad_lanes(gd_a_log), _pad_lanes(gd_dt_bias)

    xn = norm1_fwd(pre, xs, mix_norm_w)
    proj = matmul("proj_main", xn, w_main, "nn", lp, W_MAIN, D, F32)
    proj_ab = matmul("proj_ab", xn, w_ab, "nn", lp, LANES, D, F32)
    o_a, st_a = hgrn2_fwd(proj, l0, l1, hg_norm_w)
    o_b, st_b = gdn_fwd(proj, proj_ab, cw, alog, dtb, gd_norm_w)
    ya = matmul("branch_a", o_a, wa, "nn", l, D, HW, F32, a_off=(PRE // 512, 0))
    yb = matmul("branch_b", o_b, wb, "nn", l, D, HW, F32, a_off=(PRE // 512, 0))
    merged = merge_fwd(proj, ya, yb)
    mo = matmul("mix_out", merged, wo, "nn", l, D, D, F32)
    h1, xn2 = norm2_fwd(xs, mo, ffn_norm_w)
    gu, act = ffn_in_fused(xn2, w_gu)
    f = matmul("ffn_out", act, w_fo, "nn", l, D, D_FF, F32, tn=512)
    loss_part, dh2, dh2_b, d_final_w = loss_head(h1, f, final_norm_w.reshape(1, D), tgt)

    dgu = d_act_fused(dh2_b, w_fo, gu)
    dw_fo = matmul("dw_ffn_out", act, dh2_b, "tn", D_FF, D, l, BF16, tm=512, tn=1024, tk=2048)
    dxn2 = matmul("d_xn2", dgu, w_gu, "nt", l, D, 2 * D_FF, F32, tn=512, tk=D_FF)
    dw_gu = matmul("dw_ffn_in", xn2, dgu, "tn", D, 2 * D_FF, l, BF16, tm=1024, tn=1024, tk=2048)
    dh1, dh1_b, d_ffn_norm_w = norm2_bwd(h1, ffn_norm_w, dxn2, dh2)
    dmerged = matmul("d_merged", dh1_b, wo, "nt", l, D, D, F32)
    dw_o = matmul("dw_out", merged, dh1_b, "tn", D, D, l, BF16, tm=1024, tn=1024, tk=2048)
    dproj, dya, dyb = merge_bwd(proj, ya, yb, dmerged)
    do_a = matmul("d_o_a", dya, wa, "nt", lp, HW, D, F32)
    do_b = matmul("d_o_b", dyb, wb, "nt", lp, HW, D, F32)
    dw_a = matmul("dw_branch_a", o_a, dya, "tn", HW, D, lp, BF16, tm=1024, tn=1024, tk=lp // 4)
    dw_b = matmul("dw_branch_b", o_b, dyb, "tn", HW, D, lp, BF16, tm=1024, tn=1024, tk=lp // 4)
    dproj, dl0, dl1, d_hg_nw = hgrn2_bwd(proj, l0, l1, hg_norm_w, st_a, do_a, dproj)
    dproj, dproj_ab, d_conv, d_alog, d_dtb, d_gd_nw = gdn_bwd(proj, proj_ab, cw, alog, dtb, gd_norm_w, st_b, do_b,
                                                              dproj)
    dxn = matmul("d_xn", dproj, w_main, "nt", lp, D, W_MAIN, F32, tn=512, tk=4096)
    dxn_ab = matmul("d_xn_ab", dproj_ab, w_ab, "nt", lp, D, LANES, F32)
    dw_main = matmul("dw_in_main", xn, dproj, "tn", D, W_MAIN, lp, BF16, tm=1024, tn=1024, tk=lp // 4)
    dw_ab = matmul("dw_in_ab", xn, dproj_ab, "tn", D, LANES, lp, BF16, tm=1024, tk=lp // 4)
    grad_x, dpre, d_mix_norm_w = norm1_bwd(pre, xs, mix_norm_w, dxn, dxn_ab, dh1)

    parts = [cols_to_shards("dw_in_shards", [dw_main, dw_ab], W_IN // N_DEV, _w_in_pieces(), 256),
             cols_to_shards("dw_branch_a_shards", [dw_a], D // N_DEV, _plain_pieces(D // N_DEV), 256),
             cols_to_shards("dw_branch_b_shards", [dw_b], D // N_DEV, _plain_pieces(D // N_DEV), 256),
             dw_o.reshape(N_DEV, D // N_DEV, D),
             cols_to_shards("dw_ffn_in_shards", [dw_gu], 2 * D_FF // N_DEV, _ffn_in_pieces(), 256),
             dw_fo.reshape(N_DEV, D_FF // N_DEV, D)]
    parts = [p.reshape((4, 2) + p.shape[1:]) for p in parts]
    from_core = exchange_core(parts)
    tiles = [256, 256, 256, 64, 256, 176]
    chip_sums = [pair_add(f"pair_add_{a}", lax.dynamic_index_in_dim(p, cc, axis=1, keepdims=False), r, t)
                 for a, (p, r, t) in enumerate(zip(parts, from_core, tiles))]
    from_chips = exchange_chips(chip_sums)
    owns = [lax.dynamic_index_in_dim(s, my_chip, axis=0, keepdims=False) for s in chip_sums]
    names = ["w_in", "w_branch_a", "w_branch_b", "w_out", "w_ffn_in", "w_ffn_out"]
    moms = [(m_w_in, v_w_in), (m_w_branch_a, v_w_branch_a), (m_w_branch_b, v_w_branch_b), (m_w_out, v_w_out),
            (m_w_ffn_in, v_w_ffn_in), (m_w_ffn_out, v_w_ffn_out)]
    upd = {}
    for nm, w, (m, v), own, recv, t in zip(names, big, moms, owns, from_chips, tiles):
        g, d, nm_, nv_ = adamw_shard("adamw_" + nm, w, m[0], v[0], own, recv, t)
        upd[nm] = tuple(t_[None] for t_ in (g, d, nm_, nv_))

    d_lb = jnp.concatenate([dl0, dl1], axis=0)
    rep = [d_lb, d_mix_norm_w, d_hg_nw, d_alog, d_dtb, d_gd_nw, d_ffn_norm_w, d_final_w]
    buf = jnp.concatenate([_rows128(t) for t in rep] + [loss_part, _rows128(dpre[PRE - N_META:]), _rows128(d_conv)],
                          axis=0)
    n_rep = sum(_rows128(t).shape[0] for t in rep)
    n_rows = buf.shape[0]
    buf = jnp.pad(buf, ((0, -n_rows % 8), (0, 0)))
    tot = allreduce_small(buf)
    loss = tot[n_rep, 0]
    g_meta_full = tot[n_rep + 1:n_rep + 1 + N_META * D // LANES].reshape(N_META, D)
    g_conv_full = tot[n_rep + 1 + N_META * D // LANES:n_rows].reshape(CONV_K, 3 * HW)
    g_meta_mine = lax.dynamic_slice_in_dim(g_meta_full, me * (D // N_DEV), D // N_DEV, axis=1)
    g_conv_mine = lax.dynamic_slice_in_dim(g_conv_full, me * (3 * DH), 3 * DH, axis=1)

    small = [("lb_logits", lb_logits, m_lb_logits, v_lb_logits), ("mix_norm_w", mix_norm_w, m_mix_norm_w, v_mix_norm_w),
             ("hg_norm_w", hg_norm_w, m_hg_norm_w, v_hg_norm_w), ("gd_a_log", gd_a_log, m_gd_a_log, v_gd_a_log),
             ("gd_dt_bias", gd_dt_bias, m_gd_dt_bias, v_gd_dt_bias), ("gd_norm_w", gd_norm_w, m_gd_norm_w, v_gd_norm_w),
             ("ffn_norm_w", ffn_norm_w, m_ffn_norm_w, v_ffn_norm_w),
             ("final_norm_w", final_norm_w, m_final_norm_w, v_final_norm_w),
             ("meta_tokens", meta_tokens, m_meta_tokens, v_meta_tokens), ("gd_conv_w", gd_conv_w, m_gd_conv_w, v_gd_conv_w)]

    def pack(t):
        return _pad_lanes(t) if t.size < LANES else _rows128(t)

    sizes = [pack(w).shape[0] for _, w, _, _ in small]
    g_small = jnp.concatenate([tot[:n_rep], _rows128(g_meta_mine), _rows128(g_conv_mine)], axis=0)
    n_small = g_small.shape[0]
    assert n_small == sum(sizes)
    padr = lambda t: jnp.pad(t, ((0, -n_small % 8), (0, 0)))
    w_small = padr(jnp.concatenate([pack(w) for _, w, _, _ in small], axis=0))
    m_small = padr(jnp.concatenate([pack(m) for _, _, m, _ in small], axis=0))
    v_small = padr(jnp.concatenate([pack(v) for _, _, _, v in small], axis=0))
    g_small = padr(g_small)
    d_small, nm_small, nv_small = adamw_small(w_small, g_small, m_small, v_small)
    row = 0
    for (nm, w, _, _), sz in zip(small, sizes):
        def unpack(t):
            return t[row:row + sz].reshape(-1)[:w.size].reshape(w.shape)
        upd[nm] = (unpack(g_small), unpack(d_small), unpack(nm_small), unpack(nv_small))
        row += sz

    order = ["meta_tokens", "lb_logits", "mix_norm_w", "w_in", "hg_norm_w", "gd_conv_w", "gd_a_log", "gd_dt_bias",
             "gd_norm_w", "w_branch_a", "w_branch_b", "w_out", "ffn_norm_w", "w_ffn_in", "w_ffn_out", "final_norm_w"]
    outs = [loss, grad_x[None]]
    for j in range(4):
        outs += [upd[nm][j] for nm in order]
    return tuple(outs)
```

```python
import functools

import jax
import jax.numpy as jnp
from jax import lax
from jax.experimental import pallas as pl
from jax.experimental.pallas import tpu as pltpu

F32 = jnp.float32
BF16 = jnp.bfloat16
MESH = pl.DeviceIdType.MESH

EPS = 1e-6
D = 2048
N_META = 16
CHUNK = 64
SUB = 16
HEADS = 8
DH = 128
HW = HEADS * DH
CONV_K = 4
TAIL = 8
D_FF = 5632
PRE = 512
N_SKIP = PRE // CHUNK - 1
N_DEV = 8
LANES = 128
FF_BLK = 512
W_IN = 4 * HW + 4 * HW + 2 * HEADS + 2 * D
W_MAIN = 4 * HW + 4 * HW + 2 * D

ADAM_LR = 0.001
ADAM_B1 = 0.9
ADAM_B2 = 0.999
ADAM_EPS = 1e-08
ADAM_WD = 0.01
ADAM_STEP = 10

VMEM_LIMIT = 52 * 1024 * 1024

_NN = ((1,), (0,))
_NT = ((1,), (1,))
_TN = ((0,), (0,))


def _params(*sem):
    return pltpu.CompilerParams(dimension_semantics=sem, vmem_limit_bytes=VMEM_LIMIT)


BF16_ONCE, SPLIT, EXACT_LHS = 0, 1, 2


def _dot_bf16(a, b, dims):
    if a.ndim == 3:
        dn = (((dims[0][0] + 1,), (dims[1][0] + 1,)), ((0,), (0,)))
    else:
        dn = (dims, ((), ()))
    return lax.dot_general(a, b, dn, preferred_element_type=F32)


def _split(t):
    hi = t.astype(BF16)
    return hi, (t - hi.astype(F32)).astype(BF16)


def _raw_dot(a, b, dims, mode):
    if mode == BF16_ONCE:
        return _dot_bf16(a.astype(BF16), b.astype(BF16), dims)
    if mode == SPLIT:
        a_hi, a_lo = _split(a)
        b_hi, b_lo = _split(b)
        return _dot_bf16(a_hi, b_hi, dims) + (_dot_bf16(a_hi, b_lo, dims) + _dot_bf16(a_lo, b_hi, dims))
    a = a.astype(BF16)
    b_hi = b.astype(BF16)
    rest = b - b_hi.astype(F32)
    b_mid = rest.astype(BF16)
    b_lo = (rest - b_mid.astype(F32)).astype(BF16)
    return _dot_bf16(a, b_hi, dims) + (_dot_bf16(a, b_mid, dims) + _dot_bf16(a, b_lo, dims))


@functools.partial(jax.custom_vjp, nondiff_argnums=(2,))
def mm_nn(a, b, mode=BF16_ONCE):
    return _raw_dot(a, b, _NN, mode)


@functools.partial(jax.custom_vjp, nondiff_argnums=(2,))
def mm_nt(a, b, mode=BF16_ONCE):
    return _raw_dot(a, b, _NT, mode)


@functools.partial(jax.custom_vjp, nondiff_argnums=(2,))
def mm_tn(a, b, mode=BF16_ONCE):
    return _raw_dot(a, b, _TN, mode)


def _general(mode):
    return SPLIT if mode == EXACT_LHS else mode


mm_nn.defvjp(lambda a, b, p: (_raw_dot(a, b, _NN, p), (a, b)),
             lambda p, res, g: (mm_nt(g, res[1], _general(p)), mm_tn(res[0], g, p)))
mm_nt.defvjp(lambda a, b, p: (_raw_dot(a, b, _NT, p), (a, b)),
             lambda p, res, g: (mm_nn(g, res[1], p), mm_tn(g, res[0], p)))
mm_tn.defvjp(lambda a, b, p: (_raw_dot(a, b, _TN, p), (a, b)),
             lambda p, res, g: (mm_nt(res[1], g, _general(p)), mm_nn(res[0], g, _general(p))))


def _sigmoid(x):
    return 1.0 / (1.0 + jnp.exp(-x))


def _silu(x):
    return x * _sigmoid(x)


def _softplus(x):
    return jnp.maximum(x, 0.0) + jnp.log(1.0 + jnp.exp(-jnp.abs(x)))


def _l2n(t):
    return t * lax.rsqrt(jnp.sum(t * t, axis=-1, keepdims=True) + EPS)


def _rms_norm(x, w):
    return x * lax.rsqrt(jnp.mean(x * x, axis=-1, keepdims=True) + EPS) * w


def _gated_norm(o, gate, nw):
    o = o * lax.rsqrt(jnp.mean(o * o, axis=-1, keepdims=True) + EPS) * nw
    return o * _silu(gate)


def _heads(ref, piece):
    return jnp.stack([ref[:, piece * HW + DH * j:piece * HW + DH * (j + 1)] for j in range(HEADS)])


def _put_heads(ref, piece, value, accumulate=False):
    for j in range(HEADS):
        cols = slice(piece * HW + DH * j, piece * HW + DH * (j + 1))
        if accumulate:
            ref[:, cols] += value[j]
        else:
            ref[:, cols] = value[j].astype(ref.dtype)


def hg_chunk(hq, hf, hi, hg, l0, l1, nw, st):
    nb = hq.shape[0]
    m = jnp.maximum(l0, l1)
    e0 = jnp.exp(l0 - m)
    e1 = jnp.exp(l1 - m)
    lb = e0 / (e0 + e1)
    sig = _sigmoid(hf)
    q = _silu(hq)
    log_f = jnp.log(lb + (1.0 - lb) * sig)
    k = (1.0 - lb) * _sigmoid(-hf)
    v = hi
    rows = lax.broadcasted_iota(jnp.int32, (nb, CHUNK, CHUNK), 1)
    cols = lax.broadcasted_iota(jnp.int32, (nb, CHUNK, CHUNK), 2)
    b = mm_nn((rows >= cols).astype(F32), log_f, EXACT_LHS)
    o_inter = mm_nt(q * jnp.exp(b), st)
    tri_s = (lax.broadcasted_iota(jnp.int32, (nb, SUB, SUB, DH), 1)
             >= lax.broadcasted_iota(jnp.int32, (nb, SUB, SUB, DH), 2))
    outs = []
    for i in range(CHUNK // SUB):
        lo = i * SUB
        b_i, q_i, k_i, v_i = b[:, lo:lo + SUB], q[:, lo:lo + SUB], k[:, lo:lo + SUB], v[:, lo:lo + SUB]
        diff = b_i[:, :, None, :] - b_i[:, None, :, :]
        dec = jnp.where(tri_s, jnp.exp(jnp.minimum(diff, 0.0)), 0.0)
        s_ii = jnp.sum(q_i[:, :, None, :] * k_i[:, None, :, :] * dec, axis=-1)
        o_i = mm_nn(s_ii, v_i)
        if i > 0:
            b_ref = jnp.sum(log_f[:, :lo], axis=1, keepdims=True)
            q_t = q_i * jnp.exp(b_i - b_ref)
            k_t = k[:, :lo] * jnp.exp(b_ref - b[:, :lo])
            o_i = o_i + mm_nn(mm_nt(q_t, k_t), v[:, :lo])
        outs.append(o_i)
    o = o_inter + jnp.concatenate(outs, axis=1)
    b_last = jnp.sum(log_f, axis=1, keepdims=True)
    st_new = st * jnp.exp(b_last) + mm_tn(v, k * jnp.exp(b_last - b))
    return _gated_norm(o, hg, nw), st_new


def hgrn2_fwd(proj, l0, l1, nw):
    lp = proj.shape[0]
    nc = lp // CHUNK

    def body(x_ref, l0_ref, l1_ref, nw_ref, o_ref, st_out_ref, st_ref):
        c = pl.program_id(0)

        @pl.when(c == 0)
        def _():
            st_ref[...] = jnp.zeros(st_ref.shape, F32)

        @pl.when(c < N_SKIP)
        def _():
            o_ref[...] = jnp.zeros(o_ref.shape, o_ref.dtype)
            st_out_ref[...] = jnp.zeros(st_out_ref.shape, F32)

        @pl.when(c >= N_SKIP)
        def _():
            st = st_ref[...]
            st_out_ref[0] = st
            o, st_new = hg_chunk(_heads(x_ref, 0), _heads(x_ref, 1), _heads(x_ref, 2), _heads(x_ref, 3),
                                 _heads(l0_ref, 0), _heads(l1_ref, 0), nw_ref[...], st)
            _put_heads(o_ref, 0, o)
            st_ref[...] = st_new

    vec = pl.BlockSpec((1, HW), lambda c: (0, 0))
    return pl.pallas_call(
        body, name="hgrn2_fwd", grid=(nc,),
        in_specs=[pl.BlockSpec((CHUNK, 4 * HW), lambda c: (c, 0)), vec, vec, pl.BlockSpec((1, DH), lambda c: (0, 0))],
        out_specs=[pl.BlockSpec((CHUNK, HW), lambda c: (c, 0)),
                   pl.BlockSpec((1, HEADS, DH, DH), lambda c: (c, 0, 0, 0))],
        out_shape=[jax.ShapeDtypeStruct((lp, HW), BF16), jax.ShapeDtypeStruct((nc, HEADS, DH, DH), F32)],
        scratch_shapes=[pltpu.VMEM((HEADS, DH, DH), F32)],
        compiler_params=_params("arbitrary"),
    )(proj, l0, l1, nw)


def hgrn2_bwd(proj, l0, l1, nw, states, do, dproj, exch):
    lp = proj.shape[0]
    nc = lp // CHUNK

    def body(x_ref, l0_ref, l1_ref, nw_ref, st_in_ref, do_ref, _, dx_ref, dl0_ref, dl1_ref, dnw_ref, dst_ref):
        c = pl.program_id(0)
        cc = nc - 1 - c

        @pl.when(c == 0)
        def _():
            dst_ref[...] = jnp.zeros(dst_ref.shape, F32)
            dl0_ref[...] = jnp.zeros(dl0_ref.shape, F32)
            dl1_ref[...] = jnp.zeros(dl1_ref.shape, F32)
            dnw_ref[...] = jnp.zeros(dnw_ref.shape, F32)

        @pl.when(cc < N_SKIP)
        def _():
            dx_ref[...] = jnp.zeros(dx_ref.shape, dx_ref.dtype)

        @pl.when(cc >= N_SKIP)
        def _():
            _, vjp = jax.vjp(hg_chunk, _heads(x_ref, 0), _heads(x_ref, 1), _heads(x_ref, 2), _heads(x_ref, 3),
                             _heads(l0_ref, 0), _heads(l1_ref, 0), nw_ref[...], st_in_ref[0])
            grads = vjp((_heads(do_ref, 0), dst_ref[...]))
            for i in range(4):
                _put_heads(dx_ref, i, grads[i])
            _put_heads(dl0_ref, 0, grads[4], accumulate=True)
            _put_heads(dl1_ref, 0, grads[5], accumulate=True)
            dnw_ref[...] += grads[6]
            dst_ref[...] = grads[7]

    rc = lambda c: nc - 1 - c
    vec = pl.BlockSpec((1, HW), lambda c: (0, 0))
    one = pl.BlockSpec((1, DH), lambda c: (0, 0))
    outs = pl.pallas_call(
        host_exchange(body, 7, 4, exch, lambda: pl.program_id(0) == 0, lambda: pl.program_id(0) == nc - 1),
        name="hgrn2_bwd", grid=(nc,), input_output_aliases={6: 0}, compiler_params=_params("arbitrary"),
        **_hosted(exch,
                  [pl.BlockSpec((CHUNK, 4 * HW), lambda c: (rc(c), 0)), vec, vec, one,
                   pl.BlockSpec((1, HEADS, DH, DH), lambda c: (rc(c), 0, 0, 0)),
                   pl.BlockSpec((CHUNK, HW), lambda c: (rc(c), 0)), pl.BlockSpec(memory_space=pl.ANY)],
                  [pl.BlockSpec((CHUNK, 4 * HW), lambda c: (rc(c), 0)), vec, vec, one],
                  [jax.ShapeDtypeStruct((lp, W_MAIN), BF16), jax.ShapeDtypeStruct((1, HW), F32),
                   jax.ShapeDtypeStruct((1, HW), F32), jax.ShapeDtypeStruct((1, DH), F32)],
                  [pltpu.VMEM((HEADS, DH, DH), F32)]),
    )(proj, l0, l1, nw, states, do, dproj, *exch.arrays)
    return outs[:4], outs[4:]


def _unit_lower_inverse(a):
    n = a.shape[-1]
    eye = (lax.broadcasted_iota(jnp.int32, a.shape, 1) == lax.broadcasted_iota(jnp.int32, a.shape, 2)).astype(F32)
    x = eye - a
    p = mm_nn(a, a, SPLIT)
    x = x + mm_nn(x, p, SPLIT)
    power = 4
    while power < n:
        p = mm_nn(p, p, SPLIT)
        x = x + mm_nn(x, p, SPLIT)
        power *= 2
    return x


def gd_chunk(cq, ck, cv, z, ab, alog, dtb, nw, s):
    nb, c, _ = cq.shape
    lane = lax.broadcasted_iota(jnp.int32, (nb, 1, DH), 2)
    head = lax.broadcasted_iota(jnp.int32, (nb, 1, DH), 0)
    pick = lambda t, off: jnp.sum(jnp.where(lane == head + off, t[None], 0.0), axis=2, keepdims=True)
    a_raw, b_raw = pick(ab, 0), pick(ab, HEADS)
    beta = _sigmoid(b_raw)
    g = -jnp.exp(pick(alog, 0)) * _softplus(a_raw + pick(dtb, 0))
    q = _l2n(_silu(cq)) * (DH ** -0.5)
    k = _l2n(_silu(ck))
    v = _silu(cv)
    rows = lax.broadcasted_iota(jnp.int32, (nb, c, c), 1)
    cols = lax.broadcasted_iota(jnp.int32, (nb, c, c), 2)
    tril = (rows >= cols).astype(F32)
    gc_w = mm_nn(tril, jnp.broadcast_to(g, (nb, c, DH)), EXACT_LHS)
    g_sq = jnp.broadcast_to(g, (nb, c, c))
    gc_col = mm_nn(tril, g_sq, EXACT_LHS)
    gc_row = mm_nn(jnp.ones((nb, c, c), F32), g_sq * (rows <= cols).astype(F32), EXACT_LHS)
    rel = jnp.exp(jnp.minimum(gc_col - gc_row, 0.0))
    a = jnp.where(rows > cols, beta * mm_nt(k, k) * rel, 0.0)
    t_inv = _unit_lower_inverse(a)
    e_gc = jnp.exp(gc_w)
    w = mm_nn(t_inv, beta * e_gc * k, SPLIT)
    u = mm_nn(t_inv, beta * v, SPLIT)
    v_new = u - mm_nn(w, s)
    attn = jnp.where(rows >= cols, mm_nt(q, k) * rel, 0.0)
    o = mm_nn(q * e_gc, s) + mm_nn(attn, v_new)
    g_last = jnp.sum(g, axis=1, keepdims=True)
    s_new = jnp.exp(g_last) * s + mm_tn(k * jnp.exp(g_last - gc_w), v_new)
    return _gated_norm(o, z, nw), s_new


def _gd_conv(x_ref, tail_ref, cw_ref, xe_ref, first):
    xe_ref[0:TAIL, :] = jnp.where(first, 0.0, tail_ref[:, 0:3 * HW])
    xe_ref[TAIL:TAIL + CHUNK, :] = x_ref[:, 0:3 * HW]
    y = cw_ref[pl.ds(0, 1), :] * xe_ref[pl.ds(TAIL - CONV_K + 1, CHUNK), :]
    for k in range(1, CONV_K):
        y = y + cw_ref[pl.ds(k, 1), :] * xe_ref[pl.ds(TAIL - CONV_K + 1 + k, CHUNK), :]
    return y


def _conv_heads(y, piece):
    return jnp.stack([y[:, piece * HW + DH * j:piece * HW + DH * (j + 1)] for j in range(HEADS)])


def gdn_fwd(proj, proj_ab, cw, alog, dtb, nw):
    lp = proj.shape[0]
    nc = lp // CHUNK

    def body(x_ref, tail_ref, ab_ref, cw_ref, alog_ref, dtb_ref, nw_ref, o_ref, st_out_ref, st_ref, xe_ref):
        c = pl.program_id(0)

        @pl.when(c == 0)
        def _():
            st_ref[...] = jnp.zeros(st_ref.shape, F32)

        @pl.when(c < N_SKIP)
        def _():
            o_ref[...] = jnp.zeros(o_ref.shape, o_ref.dtype)
            st_out_ref[...] = jnp.zeros(st_out_ref.shape, F32)

        @pl.when(c >= N_SKIP)
        def _():
            st = st_ref[...]
            st_out_ref[0] = st
            y = _gd_conv(x_ref, tail_ref, cw_ref, xe_ref, c == 0)
            o, st_new = gd_chunk(_conv_heads(y, 0), _conv_heads(y, 1), _conv_heads(y, 2), _heads(x_ref, 3),
                                 ab_ref[...], alog_ref[...], dtb_ref[...], nw_ref[...], st)
            _put_heads(o_ref, 0, o)
            st_ref[...] = st_new

    one = pl.BlockSpec((1, DH), lambda c: (0, 0))
    return pl.pallas_call(
        body, name="gdn_fwd", grid=(nc,),
        in_specs=[pl.BlockSpec((CHUNK, 4 * HW), lambda c: (c, 1)),
                  pl.BlockSpec((TAIL, 4 * HW), lambda c: (jnp.maximum(c * (CHUNK // TAIL) - 1, 0), 1)),
                  pl.BlockSpec((CHUNK, DH), lambda c: (c, 0)),
                  pl.BlockSpec((CONV_K, 3 * HW), lambda c: (0, 0)), one, one, one],
        out_specs=[pl.BlockSpec((CHUNK, HW), lambda c: (c, 0)),
                   pl.BlockSpec((1, HEADS, DH, DH), lambda c: (c, 0, 0, 0))],
        out_shape=[jax.ShapeDtypeStruct((lp, HW), BF16), jax.ShapeDtypeStruct((nc, HEADS, DH, DH), F32)],
        scratch_shapes=[pltpu.VMEM((HEADS, DH, DH), F32), pltpu.VMEM((TAIL + CHUNK, 3 * HW), F32)],
        compiler_params=_params("arbitrary"),
    )(proj, proj, proj_ab, cw, alog, dtb, nw)


def gdn_bwd(proj, proj_ab, cw, alog, dtb, nw, states, do, dproj, exch):
    lp = proj.shape[0]
    nc = lp // CHUNK

    def body(x_ref, tail_ref, ab_ref, cw_ref, alog_ref, dtb_ref, nw_ref, st_in_ref, do_ref, _,
             dx_ref, dab_ref, dcw_ref, dalog_ref, ddtb_ref, dnw_ref, dst_ref, xe_ref, dye_ref, head_ref):
        c = pl.program_id(0)
        cc = nc - 1 - c

        @pl.when(c == 0)
        def _():
            dcw_ref[...] = jnp.zeros(dcw_ref.shape, F32)
            dalog_ref[...] = jnp.zeros(dalog_ref.shape, F32)
            ddtb_ref[...] = jnp.zeros(ddtb_ref.shape, F32)
            dnw_ref[...] = jnp.zeros(dnw_ref.shape, F32)
            dst_ref[...] = jnp.zeros(dst_ref.shape, F32)
            head_ref[...] = jnp.zeros(head_ref.shape, F32)

        @pl.when(cc < N_SKIP)
        def _():
            dx_ref[...] = jnp.zeros(dx_ref.shape, dx_ref.dtype)
            dab_ref[...] = jnp.zeros(dab_ref.shape, F32)

        @pl.when(cc >= N_SKIP)
        def _():
            y = _gd_conv(x_ref, tail_ref, cw_ref, xe_ref, cc == 0)
            _, vjp = jax.vjp(gd_chunk, _conv_heads(y, 0), _conv_heads(y, 1), _conv_heads(y, 2), _heads(x_ref, 3),
                             ab_ref[...], alog_ref[...], dtb_ref[...], nw_ref[...], st_in_ref[0])
            dcq, dck, dcv, dz, dab, dalog, ddtb, dnw, dst = vjp((_heads(do_ref, 0), dst_ref[...]))
            dst_ref[...] = dst
            dab_ref[...] = dab
            dalog_ref[...] += dalog
            ddtb_ref[...] += ddtb
            dnw_ref[...] += dnw
            _put_heads(dx_ref, 3, dz)

            for i, t in enumerate((dcq, dck, dcv)):
                for j in range(HEADS):
                    dye_ref[0:CHUNK, i * HW + DH * j:i * HW + DH * (j + 1)] = t[j]
            dye_ref[CHUNK:CHUNK + TAIL, :] = head_ref[...]
            head_ref[...] = dye_ref[0:TAIL, :]
            dy = dye_ref[0:CHUNK, :]
            dx = None
            for k in range(CONV_K):
                term = cw_ref[pl.ds(k, 1), :] * dye_ref[pl.ds(CONV_K - 1 - k, CHUNK), :]
                dx = term if dx is None else dx + term
                dcw_ref[pl.ds(k, 1), :] += jnp.sum(dy * xe_ref[pl.ds(TAIL - CONV_K + 1 + k, CHUNK), :],
                                                   axis=0, keepdims=True)
            dx_ref[:, 0:3 * HW] = dx.astype(dx_ref.dtype)

    rc = lambda c: nc - 1 - c
    one = pl.BlockSpec((1, DH), lambda c: (0, 0))
    one_shape = jax.ShapeDtypeStruct((1, DH), F32)
    outs = pl.pallas_call(
        host_exchange(body, 10, 6, exch, lambda: pl.program_id(0) == 0, lambda: pl.program_id(0) == nc - 1),
        name="gdn_bwd", grid=(nc,), input_output_aliases={9: 0}, compiler_params=_params("arbitrary"),
        **_hosted(exch,
                  [pl.BlockSpec((CHUNK, 4 * HW), lambda c: (rc(c), 1)),
                   pl.BlockSpec((TAIL, 4 * HW), lambda c: (jnp.maximum(rc(c) * (CHUNK // TAIL) - 1, 0), 1)),
                   pl.BlockSpec((CHUNK, DH), lambda c: (rc(c), 0)),
                   pl.BlockSpec((CONV_K, 3 * HW), lambda c: (0, 0)), one, one, one,
                   pl.BlockSpec((1, HEADS, DH, DH), lambda c: (rc(c), 0, 0, 0)),
                   pl.BlockSpec((CHUNK, HW), lambda c: (rc(c), 0)),
                   pl.BlockSpec(memory_space=pl.ANY)],
                  [pl.BlockSpec((CHUNK, 4 * HW), lambda c: (rc(c), 1)),
                   pl.BlockSpec((CHUNK, DH), lambda c: (rc(c), 0)),
                   pl.BlockSpec((CONV_K, 3 * HW), lambda c: (0, 0)), one, one, one],
                  [jax.ShapeDtypeStruct((lp, W_MAIN), BF16), jax.ShapeDtypeStruct((lp, DH), F32),
                   jax.ShapeDtypeStruct((CONV_K, 3 * HW), F32), one_shape, one_shape, one_shape],
                  [pltpu.VMEM((HEADS, DH, DH), F32), pltpu.VMEM((TAIL + CHUNK, 3 * HW), F32),
                   pltpu.VMEM((CHUNK + TAIL, 3 * HW), F32), pltpu.VMEM((TAIL, 3 * HW), F32)]),
    )(proj, proj, proj_ab, cw, alog, dtb, nw, states, do, dproj, *exch.arrays)
    return outs[:6], outs[6:]


def matmul(name, a, b, mode, m, n, k, out_dtype, tm=512, tn=1024, tk=None, a_off=(0, 0), b_off=(0, 0), exch=None):
    tm, tn = min(tm, m), min(tn, n)
    tk = k if tk is None else min(tk, k)
    assert m % tm == 0 and n % tn == 0 and k % tk == 0, (name, m, n, k, tm, tn, tk)
    gi, gj, gk = m // tm, n // tn, k // tk
    if mode == "nn":
        a_spec = pl.BlockSpec((tm, tk), lambda j, i, kk: (i + a_off[0], kk + a_off[1]))
        b_spec = pl.BlockSpec((tk, tn), lambda j, i, kk: (kk + b_off[0], j + b_off[1]))
        dims = _NN
    elif mode == "nt":
        a_spec = pl.BlockSpec((tm, tk), lambda j, i, kk: (i + a_off[0], kk + a_off[1]))
        b_spec = pl.BlockSpec((tn, tk), lambda j, i, kk: (j + b_off[0], kk + b_off[1]))
        dims = _NT
    else:
        a_spec = pl.BlockSpec((tk, tm), lambda j, i, kk: (kk + a_off[0], i + a_off[1]))
        b_spec = pl.BlockSpec((tk, tn), lambda j, i, kk: (kk + b_off[0], j + b_off[1]))
        dims = _TN

    def body(a_ref, b_ref, o_ref, *acc):
        d = lax.dot_general(a_ref[...].astype(BF16), b_ref[...].astype(BF16), (dims, ((), ())),
                            preferred_element_type=F32)
        if gk == 1:
            o_ref[...] = d.astype(o_ref.dtype)
        else:
            acc_ref, = acc
            kk = pl.program_id(2)

            @pl.when(kk == 0)
            def _():
                acc_ref[...] = d

            @pl.when(kk > 0)
            def _():
                acc_ref[...] += d

            @pl.when(kk == gk - 1)
            def _():
                o_ref[...] = acc_ref[...].astype(o_ref.dtype)

    o_spec = pl.BlockSpec((tm, tn), lambda j, i, kk: (i, j))
    o_shape = jax.ShapeDtypeStruct((m, n), out_dtype)
    scratch = [] if gk == 1 else [pltpu.VMEM((tm, tn), F32)]
    if exch is None:
        return pl.pallas_call(
            body, name=name, grid=(gj, gi, gk), in_specs=[a_spec, b_spec], out_specs=o_spec, out_shape=o_shape,
            scratch_shapes=scratch, compiler_params=_params("parallel", "parallel", "arbitrary"),
        )(a, b)
    step = lambda: (pl.program_id(0), pl.program_id(1), pl.program_id(2))
    first = lambda: jnp.logical_and(jnp.logical_and(step()[0] == 0, step()[1] == 0), step()[2] == 0)
    last = lambda: jnp.logical_and(jnp.logical_and(step()[0] == gj - 1, step()[1] == gi - 1), step()[2] == gk - 1)
    outs = pl.pallas_call(
        host_exchange(body, 2, 1, exch, first, last), name=name, grid=(gj, gi, gk),
        compiler_params=_params("arbitrary", "arbitrary", "arbitrary"),
        **_hosted(exch, [a_spec, b_spec], [o_spec], [o_shape], scratch),
    )(a, b, *exch.arrays)
    return outs[0], outs[1:]


def _swiglu(gu):
    return _silu(gu[:, :FF_BLK]) * gu[:, FF_BLK:]


def ffn_in_fused(xn2, w_gu, tm=512):
    l = xn2.shape[0]
    tn = 2 * FF_BLK

    def body(a_ref, b_ref, gu_ref, act_ref):
        gu = lax.dot_general(a_ref[...], b_ref[...], (_NN, ((), ())), preferred_element_type=F32)
        gu_ref[...] = gu
        act_ref[...] = _swiglu(gu).astype(act_ref.dtype)

    return pl.pallas_call(
        body, name="ffn_in", grid=(2 * D_FF // tn, l // tm),
        in_specs=[pl.BlockSpec((tm, D), lambda j, i: (i, 0)), pl.BlockSpec((D, tn), lambda j, i: (0, j))],
        out_specs=[pl.BlockSpec((tm, tn), lambda j, i: (i, j)), pl.BlockSpec((tm, FF_BLK), lambda j, i: (i, j))],
        out_shape=[jax.ShapeDtypeStruct((l, 2 * D_FF), F32), jax.ShapeDtypeStruct((l, D_FF), BF16)],
        compiler_params=_params("parallel", "parallel"),
    )(xn2, w_gu)


def d_act_fused(dh2, w_fo, gu, tm=512):
    l = dh2.shape[0]

    def body(a_ref, b_ref, gu_ref, o_ref):
        da = lax.dot_general(a_ref[...], b_ref[...], (_NT, ((), ())), preferred_element_type=F32)
        _, vjp = jax.vjp(_swiglu, gu_ref[...])
        dgu, = vjp(da)
        o_ref[...] = dgu.astype(o_ref.dtype)

    return pl.pallas_call(
        body, name="d_act", grid=(D_FF // FF_BLK, l // tm),
        in_specs=[pl.BlockSpec((tm, D), lambda j, i: (i, 0)), pl.BlockSpec((FF_BLK, D), lambda j, i: (j, 0)),
                  pl.BlockSpec((tm, 2 * FF_BLK), lambda j, i: (i, j))],
        out_specs=pl.BlockSpec((tm, 2 * FF_BLK), lambda j, i: (i, j)),
        out_shape=jax.ShapeDtypeStruct((l, 2 * D_FF), BF16),
        compiler_params=_params("parallel", "parallel"),
    )(dh2, w_fo, gu)


TR = 256
N_PRE = PRE // TR


def _row(width, off=0, col=0):
    return pl.BlockSpec((TR, width), lambda i: (i + off, col))


def _pre_spec():
    return pl.BlockSpec((TR, D), lambda i: (jnp.minimum(i, N_PRE - 1), 0))


def _seq_spec(width=D, col=0):
    return pl.BlockSpec((TR, width), lambda i: (jnp.maximum(i - N_PRE, 0), col))


def _vec_spec(width=D):
    return pl.BlockSpec((1, width), lambda i: (0, 0))


def norm1_fwd(pre, x, w):
    lp = PRE + x.shape[0]

    def body(pre_ref, x_ref, w_ref, o_ref):
        h = jnp.where(pl.program_id(0) < N_PRE, pre_ref[...], x_ref[...])
        o_ref[...] = _rms_norm(h, w_ref[...]).astype(o_ref.dtype)

    return pl.pallas_call(
        body, name="norm1_fwd", grid=(lp // TR,),
        in_specs=[_pre_spec(), _seq_spec(), _vec_spec()],
        out_specs=_row(D), out_shape=jax.ShapeDtypeStruct((lp, D), BF16),
        compiler_params=_params("parallel"),
    )(pre, x, w)


def norm1_bwd(pre, x, w, dxn, dxn_ab, dh1):
    l = x.shape[0]
    lp = PRE + l

    def body(pre_ref, x_ref, w_ref, dxn_ref, dxn_ab_ref, dh1_ref, dx_ref, dpre_ref, dw_ref):
        i = pl.program_id(0)
        h = jnp.where(i < N_PRE, pre_ref[...], x_ref[...])
        _, vjp = jax.vjp(_rms_norm, h, w_ref[...])
        dh, dw = vjp(dxn_ref[...] + dxn_ab_ref[...])

        @pl.when(i == 0)
        def _():
            dw_ref[...] = dw

        @pl.when(i > 0)
        def _():
            dw_ref[...] += dw

        @pl.when(i < N_PRE)
        def _():
            dpre_ref[...] = dh

        @pl.when(i >= N_PRE)
        def _():
            dx_ref[...] = dh + dh1_ref[...]

    return pl.pallas_call(
        body, name="norm1_bwd", grid=(lp // TR,),
        in_specs=[_pre_spec(), _seq_spec(), _vec_spec(), _row(D), _row(D), _seq_spec()],
        out_specs=[_seq_spec(), _pre_spec(), _vec_spec()],
        out_shape=[jax.ShapeDtypeStruct((l, D), F32), jax.ShapeDtypeStruct((PRE, D), F32),
                   jax.ShapeDtypeStruct((1, D), F32)],
        compiler_params=_params("arbitrary"),
    )(pre, x, w, dxn, dxn_ab, dh1)


def _merge(gates, ya, yb):
    return _sigmoid(gates[:, :D]) * ya + _sigmoid(gates[:, D:]) * yb


def merge_fwd(proj, ya, yb):
    l = ya.shape[0]

    def body(g_ref, ya_ref, yb_ref, o_ref):
        o_ref[...] = _merge(g_ref[...], ya_ref[...], yb_ref[...]).astype(o_ref.dtype)

    return pl.pallas_call(
        body, name="merge_fwd", grid=(l // TR,),
        in_specs=[_row(2 * D, N_PRE, 2), _row(D), _row(D)],
        out_specs=_row(D), out_shape=jax.ShapeDtypeStruct((l, D), BF16),
        compiler_params=_params("parallel"),
    )(proj, ya, yb)


def merge_bwd(proj, ya, yb, dmerged):
    l = ya.shape[0]
    lp = PRE + l

    def body(g_ref, ya_ref, yb_ref, dm_ref, dg_ref, dya_ref, dyb_ref):
        i = pl.program_id(0)

        @pl.when(i < N_PRE)
        def _():
            dg_ref[...] = jnp.zeros(dg_ref.shape, dg_ref.dtype)
            dya_ref[...] = jnp.zeros(dya_ref.shape, dya_ref.dtype)
            dyb_ref[...] = jnp.zeros(dyb_ref.shape, dyb_ref.dtype)

        @pl.when(i >= N_PRE)
        def _():
            _, vjp = jax.vjp(_merge, g_ref[...], ya_ref[...], yb_ref[...])
            dg, dya, dyb = vjp(dm_ref[...])
            dg_ref[...] = dg.astype(dg_ref.dtype)
            dya_ref[...] = dya.astype(dya_ref.dtype)
            dyb_ref[...] = dyb.astype(dyb_ref.dtype)

    return pl.pallas_call(
        body, name="merge_bwd", grid=(lp // TR,),
        in_specs=[pl.BlockSpec((TR, 2 * D), lambda i: (jnp.maximum(i, N_PRE), 2)), _seq_spec(), _seq_spec(),
                  _seq_spec()],
        out_specs=[_row(2 * D, 0, 2), _row(D), _row(D)],
        out_shape=[jax.ShapeDtypeStruct((lp, W_MAIN), BF16), jax.ShapeDtypeStruct((lp, D), BF16),
                   jax.ShapeDtypeStruct((lp, D), BF16)],
        compiler_params=_params("parallel"),
    )(proj, ya, yb, dmerged)


def norm2_fwd(x, mo, w):
    l = x.shape[0]

    def body(x_ref, mo_ref, w_ref, h_ref, o_ref):
        h = x_ref[...] + mo_ref[...]
        h_ref[...] = h
        o_ref[...] = _rms_norm(h, w_ref[...]).astype(o_ref.dtype)

    return pl.pallas_call(
        body, name="norm2_fwd", grid=(l // TR,),
        in_specs=[_row(D), _row(D), _vec_spec()],
        out_specs=[_row(D), _row(D)],
        out_shape=[jax.ShapeDtypeStruct((l, D), F32), jax.ShapeDtypeStruct((l, D), BF16)],
        compiler_params=_params("parallel"),
    )(x, mo, w)


def norm2_bwd(h1, w, dxn2, dh2):
    l = h1.shape[0]

    def body(h_ref, w_ref, dxn_ref, dh2_ref, dh1_ref, dh1b_ref, dw_ref):
        i = pl.program_id(0)
        _, vjp = jax.vjp(_rms_norm, h_ref[...], w_ref[...])
        dh, dw = vjp(dxn_ref[...])
        dh1 = dh + dh2_ref[...]
        dh1_ref[...] = dh1
        dh1b_ref[...] = dh1.astype(BF16)

        @pl.when(i == 0)
        def _():
            dw_ref[...] = dw

        @pl.when(i > 0)
        def _():
            dw_ref[...] += dw

    return pl.pallas_call(
        body, name="norm2_bwd", grid=(l // TR,),
        in_specs=[_row(D), _vec_spec(), _row(D), _row(D)],
        out_specs=[_row(D), _row(D), _vec_spec()],
        out_shape=[jax.ShapeDtypeStruct((l, D), F32), jax.ShapeDtypeStruct((l, D), BF16),
                   jax.ShapeDtypeStruct((1, D), F32)],
        compiler_params=_params("arbitrary"),
    )(h1, w, dxn2, dh2)


def _loss_rows(h1, f, w, tgt):
    y = _rms_norm(h1 + f, w)
    err = (y - tgt) * (y - tgt)
    return 0.5 * jnp.sum(jnp.mean(err, axis=-1, keepdims=True), axis=0, keepdims=True)


def loss_head(h1, f, w, tgt):
    l = h1.shape[0]

    def body(h_ref, f_ref, w_ref, t_ref, loss_ref, dh_ref, dhb_ref, dw_ref):
        i = pl.program_id(0)
        loss, vjp = jax.vjp(_loss_rows, h_ref[...], f_ref[...], w_ref[...], t_ref[...])
        dh, _, dw, _ = vjp(jnp.ones((1, 1), F32))
        dh_ref[...] = dh
        dhb_ref[...] = dh.astype(BF16)
        loss = jnp.broadcast_to(loss, (1, LANES))

        @pl.when(i == 0)
        def _():
            dw_ref[...] = dw
            loss_ref[...] = loss

        @pl.when(i > 0)
        def _():
            dw_ref[...] += dw
            loss_ref[...] += loss

    return pl.pallas_call(
        body, name="loss_head", grid=(l // TR,),
        in_specs=[_row(D), _row(D), _vec_spec(), _row(D)],
        out_specs=[_vec_spec(LANES), _row(D), _row(D), _vec_spec()],
        out_shape=[jax.ShapeDtypeStruct((1, LANES), F32), jax.ShapeDtypeStruct((l, D), F32),
                   jax.ShapeDtypeStruct((l, D), BF16), jax.ShapeDtypeStruct((1, D), F32)],
        compiler_params=_params("arbitrary"),
    )(h1, f, w, tgt)


def _adamw(w, g, m, v):
    m = ADAM_B1 * m + (1.0 - ADAM_B1) * g
    v = ADAM_B2 * v + (1.0 - ADAM_B2) * (g * g)
    m_hat = m / (1.0 - ADAM_B1 ** ADAM_STEP)
    v_hat = v / (1.0 - ADAM_B2 ** ADAM_STEP)
    delta = -ADAM_LR * (m_hat / (jnp.sqrt(v_hat) + ADAM_EPS) + ADAM_WD * w)
    return delta, m, v


def adamw_shard(name, w, m, v, own, recv, tr):
    r, c = w.shape
    assert r % tr == 0

    def body(w_ref, m_ref, v_ref, own_ref, r0_ref, r1_ref, r2_ref, g_ref, d_ref, nm_ref, nv_ref):
        g = own_ref[...].astype(F32) + r0_ref[...].astype(F32)
        g = g + r1_ref[...].astype(F32)
        g = g + r2_ref[...].astype(F32)
        d, nm, nv = _adamw(w_ref[...], g, m_ref[...], v_ref[...])
        g_ref[...] = g
        d_ref[...] = d
        nm_ref[...] = nm
        nv_ref[...] = nv

    blk = pl.BlockSpec((tr, c), lambda i: (i, 0))
    part = lambda s: pl.BlockSpec((None, tr, c), lambda i: (s, i, 0))
    shape = jax.ShapeDtypeStruct((r, c), F32)
    return pl.pallas_call(
        body, name=name, grid=(r // tr,),
        in_specs=[blk, blk, blk, blk, part(0), part(1), part(2)],
        out_specs=[blk, blk, blk, blk], out_shape=[shape, shape, shape, shape],
        compiler_params=_params("parallel"),
    )(w, m, v, own, recv, recv, recv)


def adamw_small(w, g, m, v):
    def body(w_ref, g_ref, m_ref, v_ref, d_ref, nm_ref, nv_ref):
        d, nm, nv = _adamw(w_ref[...], g_ref[...], m_ref[...], v_ref[...])
        d_ref[...] = d
        nm_ref[...] = nm
        nv_ref[...] = nv

    shape = jax.ShapeDtypeStruct(w.shape, F32)
    return pl.pallas_call(body, name="adamw_small", out_shape=[shape, shape, shape])(w, g, m, v)


def pair_add(name, a, b, tr):
    q, r, c = a.shape
    assert r % tr == 0

    def body(a_ref, b_ref, o_ref):
        o_ref[...] = (a_ref[...].astype(F32) + b_ref[...].astype(F32)).astype(o_ref.dtype)

    blk = pl.BlockSpec((None, tr, c), lambda s, i: (s, i, 0))
    return pl.pallas_call(
        body, name=name, grid=(q, r // tr), in_specs=[blk, blk], out_specs=blk,
        out_shape=jax.ShapeDtypeStruct((q, r, c), BF16), compiler_params=_params("parallel", "parallel"),
    )(a, b)


def _w_in_pieces():
    c = W_IN // N_DEV
    ranges = ((0, 8 * HW, 0, 0), (8 * HW, 8 * HW + 2 * HEADS, 1, 8 * HW), (8 * HW + 2 * HEADS, W_IN, 0, 2 * HEADS))
    out = []
    for d in range(N_DEV):
        for lo, hi, target, shift in ranges:
            s, e = max(lo, c * d), min(hi, c * (d + 1))
            if s < e:
                out.append((d, s - c * d, e - s, target, s - shift))
    return out


def _ffn_in_pieces():
    c = 2 * D_FF // N_DEV
    out = []
    for d in range(N_DEV):
        s = c * d
        while s < c * (d + 1):
            e = min((s // FF_BLK + 1) * FF_BLK, c * (d + 1))
            half, blk = divmod(s // FF_BLK, D_FF // FF_BLK)
            out.append((d, s - c * d, e - s, 0, (2 * blk + half) * FF_BLK + s % FF_BLK))
            s = e
    return out


def _plain_pieces(c):
    return [(d, 0, c, 0, c * d) for d in range(N_DEV)]


def shards_to_cols(name, g, widths, pieces, tr):
    _, r, c = g.shape
    covered = [sum(p[2] for p in pieces if p[3] == t) for t in range(len(widths))]

    def body(g_ref, *outs):
        for t, o in enumerate(outs):
            if covered[t] < widths[t]:
                o[...] = jnp.zeros(o.shape, o.dtype)
        for d, s, w, t, ts in pieces:
            outs[t][:, ts:ts + w] = g_ref[d, :, s:s + w]

    return pl.pallas_call(
        body, name=name, grid=(r // tr,),
        in_specs=[pl.BlockSpec((N_DEV, tr, c), lambda i: (0, i, 0))],
        out_specs=[pl.BlockSpec((tr, w), lambda i: (i, 0)) for w in widths],
        out_shape=[jax.ShapeDtypeStruct((r, w), g.dtype) for w in widths],
        compiler_params=_params("parallel"),
    )(g)


def cols_to_shards(name, srcs, c, pieces, tr):
    r = srcs[0].shape[0]

    def body(*refs):
        o = refs[-1]
        for d, s, w, t, ts in pieces:
            o[d, :, s:s + w] = refs[t][:, ts:ts + w]

    return pl.pallas_call(
        body, name=name, grid=(r // tr,),
        in_specs=[pl.BlockSpec((tr, t.shape[1]), lambda i: (i, 0)) for t in srcs],
        out_specs=pl.BlockSpec((N_DEV, tr, c), lambda i: (0, i, 0)),
        out_shape=jax.ShapeDtypeStruct((N_DEV, r, c), srcs[0].dtype),
        compiler_params=_params("parallel"),
    )(*srcs)


def _place():
    return lax.axis_index("x"), lax.axis_index("y"), lax.axis_index("c")


def _dev(px, py, pc):
    return 4 * px + 2 * py + pc


class Exchange:
    def __init__(self, arrays, out_shapes, sems, start, finish):
        self.arrays, self.out_shapes, self.sems, self.start, self.finish = arrays, out_shapes, sems, start, finish


def gather_exchange(arrays):
    n = len(arrays)

    def plan(ins, outs, sems):
        send_sems, recv_sems, local_sems = sems
        x, y, c = _place()
        me, sibling = (x, y, c), (x, y, 1 - c)
        chips = [(1 - x, y), (x, 1 - y), (1 - x, 1 - y)]

        def copy(a, k, block, to, src=None):
            dst = outs[a].at[_dev(*block)]
            return pltpu.make_async_remote_copy(
                src_ref=dst if src is None else src, dst_ref=dst, send_sem=send_sems.at[a, k],
                recv_sem=recv_sems.at[a, k], device_id=to, device_id_type=MESH)

        mine = [pltpu.make_async_copy(ins[a], outs[a].at[_dev(*me)], local_sems.at[a]) for a in range(n)]
        first = []
        for a in range(n):
            first.append(copy(a, 0, me, sibling, src=ins[a]))
            first += [copy(a, 1 + j, me, (*chip, c), src=ins[a]) for j, chip in enumerate(chips)]
        return me, sibling, chips, c, copy, mine, first

    def start(ins, outs, sems):
        *_, mine, first = plan(ins, outs, sems)
        for cp in mine + first:
            cp.start()

    def finish(ins, outs, sems):
        me, sibling, chips, c, copy, mine, first = plan(ins, outs, sems)
        passed = []
        for j, chip in enumerate(chips):
            for a in range(n):
                copy(a, 1 + j, (*chip, c), me).wait_recv()
                fwd = copy(a, 4 + j, (*chip, c), sibling)
                fwd.start()
                passed.append(fwd)
        for a in range(n):
            copy(a, 0, sibling, me).wait_recv()
        for j, chip in enumerate(chips):
            for a in range(n):
                copy(a, 4 + j, (*chip, 1 - c), me).wait_recv()
        for cp in first + passed:
            cp.wait_send()
        for cp in mine:
            cp.wait()

    return Exchange(arrays, [jax.ShapeDtypeStruct((N_DEV,) + t.shape, t.dtype) for t in arrays],
                    [pltpu.SemaphoreType.DMA((n, 7)), pltpu.SemaphoreType.DMA((n, 7)), pltpu.SemaphoreType.DMA((n,))],
                    start, finish)


def core_exchange(arrays):
    n = len(arrays)

    def copies(ins, outs, sems):
        send_sems, recv_sems = sems
        x, y, c = _place()
        return [pltpu.make_async_remote_copy(
            src_ref=ins[a].at[q, 1 - c], dst_ref=outs[a].at[q], send_sem=send_sems.at[a, q],
            recv_sem=recv_sems.at[a, q], device_id=(x, y, 1 - c), device_id_type=MESH)
            for a in range(n) for q in range(4)]

    def start(ins, outs, sems):
        for cp in copies(ins, outs, sems):
            cp.start()

    def finish(ins, outs, sems):
        for cp in copies(ins, outs, sems):
            cp.wait()

    return Exchange(arrays, [jax.ShapeDtypeStruct((4,) + t.shape[2:], t.dtype) for t in arrays],
                    [pltpu.SemaphoreType.DMA((n, 4)), pltpu.SemaphoreType.DMA((n, 4))], start, finish)


def chips_exchange(arrays):
    n = len(arrays)

    def copies(ins, outs, sems):
        send_sems, recv_sems = sems
        x, y, c = _place()
        chips = [(1 - x, y), (x, 1 - y), (1 - x, 1 - y)]
        return [pltpu.make_async_remote_copy(
            src_ref=ins[a].at[2 * qx + qy], dst_ref=outs[a].at[j], send_sem=send_sems.at[a, j],
            recv_sem=recv_sems.at[a, j], device_id=(qx, qy, c), device_id_type=MESH)
            for a in range(n) for j, (qx, qy) in enumerate(chips)]

    def start(ins, outs, sems):
        for cp in copies(ins, outs, sems):
            cp.start()

    def finish(ins, outs, sems):
        for cp in copies(ins, outs, sems):
            cp.wait()

    return Exchange(arrays, [jax.ShapeDtypeStruct((3,) + t.shape[1:], t.dtype) for t in arrays],
                    [pltpu.SemaphoreType.DMA((n, 3)), pltpu.SemaphoreType.DMA((n, 3))], start, finish)


def run_exchange(name, exch):
    n_in, n_out = len(exch.arrays), len(exch.out_shapes)
    any_spec = pl.BlockSpec(memory_space=pl.ANY)

    def body(*refs):
        ins, outs, sems = refs[:n_in], refs[n_in:n_in + n_out], refs[n_in + n_out:]
        exch.start(ins, outs, sems)
        exch.finish(ins, outs, sems)

    return pl.pallas_call(
        body, name=name, in_specs=[any_spec] * n_in, out_specs=[any_spec] * n_out, out_shape=exch.out_shapes,
        scratch_shapes=exch.sems,
    )(*exch.arrays)


def host_exchange(body, n_in, n_out, exch, first, last):
    ne_in, ne_out, ne_sem = len(exch.arrays), len(exch.out_shapes), len(exch.sems)

    def wrapped(*refs):
        ins, refs = refs[:n_in], refs[n_in:]
        e_ins, refs = refs[:ne_in], refs[ne_in:]
        outs, refs = refs[:n_out], refs[n_out:]
        e_outs, refs = refs[:ne_out], refs[ne_out:]
        scratch, e_sems = refs[:len(refs) - ne_sem], refs[len(refs) - ne_sem:]

        @pl.when(first())
        def _():
            exch.start(e_ins, e_outs, e_sems)

        body(*ins, *outs, *scratch)

        @pl.when(last())
        def _():
            exch.finish(e_ins, e_outs, e_sems)

    return wrapped


def _hosted(exch, in_specs, out_specs, out_shape, scratch_shapes):
    any_spec = pl.BlockSpec(memory_space=pl.ANY)
    return dict(in_specs=list(in_specs) + [any_spec] * len(exch.arrays),
                out_specs=list(out_specs) + [any_spec] * len(exch.out_shapes),
                out_shape=list(out_shape) + list(exch.out_shapes),
                scratch_shapes=list(scratch_shapes) + list(exch.sems))


def allreduce_small(buf):
    r = buf.shape[0]
    vmem = pl.BlockSpec(memory_space=pltpu.VMEM)

    def body(buf_ref, out_ref, all_ref, send_sems, recv_sems):
        x, y, c = _place()
        me = _dev(x, y, c)
        copies = []
        for k in range(1, N_DEV):
            peer = (x ^ (k >> 2), y ^ ((k >> 1) & 1), c ^ (k & 1))
            copies.append(pltpu.make_async_remote_copy(
                src_ref=buf_ref, dst_ref=all_ref.at[me], send_sem=send_sems.at[k - 1], recv_sem=recv_sems.at[k - 1],
                device_id=peer, device_id_type=MESH))
        for cp in copies:
            cp.start()
        all_ref[me] = buf_ref[...]
        for cp in copies:
            cp.wait()
        total = all_ref[0]
        for d in range(1, N_DEV):
            total = total + all_ref[d]
        out_ref[...] = total

    return pl.pallas_call(
        body, name="allreduce_small", in_specs=[vmem], out_specs=vmem,
        out_shape=jax.ShapeDtypeStruct((r, LANES), F32),
        scratch_shapes=[pltpu.VMEM((N_DEV, r, LANES), F32), pltpu.SemaphoreType.DMA((N_DEV - 1,)),
                        pltpu.SemaphoreType.DMA((N_DEV - 1,))],
    )(buf)


def _cols_from_shards(g):
    return jnp.transpose(g, (1, 0, 2)).reshape(g.shape[1], N_DEV * g.shape[2])


def _rows128(t):
    return t.reshape(-1, LANES)


def _pad_lanes(t):
    t = t.reshape(1, -1)
    return jnp.pad(t, ((0, 0), (0, LANES - t.shape[1])))


def kernel(x, meta_tokens, lb_logits, mix_norm_w, w_in, hg_norm_w, gd_conv_w, gd_a_log, gd_dt_bias, gd_norm_w, w_branch_a, w_branch_b, w_out, ffn_norm_w, w_ffn_in, w_ffn_out, final_norm_w, loss_target, m_meta_tokens, m_lb_logits, m_mix_norm_w, m_w_in, m_hg_norm_w, m_gd_conv_w, m_gd_a_log, m_gd_dt_bias, m_gd_norm_w, m_w_branch_a, m_w_branch_b, m_w_out, m_ffn_norm_w, m_w_ffn_in, m_w_ffn_out, m_final_norm_w, v_meta_tokens, v_lb_logits, v_mix_norm_w, v_w_in, v_hg_norm_w, v_gd_conv_w, v_gd_a_log, v_gd_dt_bias, v_gd_norm_w, v_w_branch_a, v_w_branch_b, v_w_out, v_ffn_norm_w, v_w_ffn_in, v_w_ffn_out, v_final_norm_w):
    xs = x[0]
    tgt = loss_target[0]
    l = xs.shape[0]
    lp = PRE + l
    cx, cy, cc = _place()
    me = _dev(cx, cy, cc)
    my_chip = 2 * cx + cy

    big = [w_in[0], w_branch_a[0], w_branch_b[0], w_out[0], w_ffn_in[0], w_ffn_out[0]]
    big16 = [t.astype(BF16) for t in big]
    g_in, g_meta, g_conv = run_exchange("gather_first", gather_exchange([big16[0], meta_tokens, gd_conv_w[0]]))
    w_main, w_ab = shards_to_cols("w_in_cols", g_in, [W_MAIN, LANES], _w_in_pieces(), 256)
    meta_full = _cols_from_shards(g_meta)
    cw = _cols_from_shards(g_conv)
    pre = jnp.concatenate([jnp.zeros((PRE - N_META, D), F32), meta_full], axis=0)
    l0, l1 = lb_logits[0:1], lb_logits[1:2]
    alog, dtb = _pad_lanes(gd_a_log), _pad_lanes(gd_dt_bias)

    xn = norm1_fwd(pre, xs, mix_norm_w)
    proj, (g_ba, g_bb, g_out, g_ffi, g_ffo) = matmul("proj_main", xn, w_main, "nn", lp, W_MAIN, D, F32,
                                                      exch=gather_exchange(big16[1:]))
    wa, = shards_to_cols("w_branch_a_cols", g_ba, [D], _plain_pieces(D // N_DEV), 256)
    wb, = shards_to_cols("w_branch_b_cols", g_bb, [D], _plain_pieces(D // N_DEV), 256)
    wo = g_out.reshape(D, D)
    w_gu, = shards_to_cols("w_ffn_in_cols", g_ffi, [2 * D_FF], _ffn_in_pieces(), 256)
    w_fo = g_ffo.reshape(D_FF, D)
    proj_ab = matmul("proj_ab", xn, w_ab, "nn", lp, LANES, D, F32)
    o_a, st_a = hgrn2_fwd(proj, l0, l1, hg_norm_w)
    o_b, st_b = gdn_fwd(proj, proj_ab, cw, alog, dtb, gd_norm_w)
    ya = matmul("branch_a", o_a, wa, "nn", l, D, HW, F32, a_off=(PRE // 512, 0))
    yb = matmul("branch_b", o_b, wb, "nn", l, D, HW, F32, a_off=(PRE // 512, 0))
    merged = merge_fwd(proj, ya, yb)
    mo = matmul("mix_out", merged, wo, "nn", l, D, D, F32)
    h1, xn2 = norm2_fwd(xs, mo, ffn_norm_w)
    gu, act = ffn_in_fused(xn2, w_gu)
    f = matmul("ffn_out", act, w_fo, "nn", l, D, D_FF, F32, tn=512)
    loss_part, dh2, dh2_b, d_final_w = loss_head(h1, f, final_norm_w.reshape(1, D), tgt)

    tiles = {"w_in": 256, "w_branch_a": 256, "w_branch_b": 256, "w_out": 64, "w_ffn_in": 256, "w_ffn_out": 176}

    def chip_sums(tag, parts):
        blocks = [p.reshape((4, 2) + p.shape[1:]) for p in parts.values()]
        from_core = run_exchange("exchange_core_" + tag, core_exchange(blocks))
        return {nm: pair_add("pair_add_" + nm, lax.dynamic_index_in_dim(p, cc, axis=1, keepdims=False), r, tiles[nm])
                for (nm, p, r) in zip(parts, blocks, from_core)}

    dgu = d_act_fused(dh2_b, w_fo, gu)
    dw_fo = matmul("dw_ffn_out", act, dh2_b, "tn", D_FF, D, l, BF16, tm=512, tn=1024, tk=2048)
    dxn2 = matmul("d_xn2", dgu, w_gu, "nt", l, D, 2 * D_FF, F32, tn=512, tk=D_FF)
    dw_gu = matmul("dw_ffn_in", xn2, dgu, "tn", D, 2 * D_FF, l, BF16, tm=1024, tn=1024, tk=2048)
    sums = chip_sums("ffn", {
        "w_ffn_in": cols_to_shards("dw_ffn_in_shards", [dw_gu], 2 * D_FF // N_DEV, _ffn_in_pieces(), 256),
        "w_ffn_out": dw_fo.reshape(N_DEV, D_FF // N_DEV, D)})
    dh1, dh1_b, d_ffn_norm_w = norm2_bwd(h1, ffn_norm_w, dxn2, dh2)
    dmerged = matmul("d_merged", dh1_b, wo, "nt", l, D, D, F32)
    dw_o = matmul("dw_out", merged, dh1_b, "tn", D, D, l, BF16, tm=1024, tn=1024, tk=2048)
    dproj, dya, dyb = merge_bwd(proj, ya, yb, dmerged)
    do_a = matmul("d_o_a", dya, wa, "nt", lp, HW, D, F32)
    do_b = matmul("d_o_b", dyb, wb, "nt", lp, HW, D, F32)
    dw_a = matmul("dw_branch_a", o_a, dya, "tn", HW, D, lp, BF16, tm=1024, tn=1024, tk=lp // 4)
    dw_b = matmul("dw_branch_b", o_b, dyb, "tn", HW, D, lp, BF16, tm=1024, tn=1024, tk=lp // 4)
    sums.update(chip_sums("mix", {
        "w_branch_a": cols_to_shards("dw_branch_a_shards", [dw_a], D // N_DEV, _plain_pieces(D // N_DEV), 256),
        "w_branch_b": cols_to_shards("dw_branch_b_shards", [dw_b], D // N_DEV, _plain_pieces(D // N_DEV), 256),
        "w_out": dw_o.reshape(N_DEV, D // N_DEV, D)}))
    ffn_names, mix_names = ["w_ffn_in", "w_ffn_out"], ["w_branch_a", "w_branch_b", "w_out"]
    (dproj, dl0, dl1, d_hg_nw), from_chips_ffn = hgrn2_bwd(
        proj, l0, l1, hg_norm_w, st_a, do_a, dproj, chips_exchange([sums[nm] for nm in ffn_names]))
    (dproj, dproj_ab, d_conv, d_alog, d_dtb, d_gd_nw), from_chips_mix = gdn_bwd(
        proj, proj_ab, cw, alog, dtb, gd_norm_w, st_b, do_b, dproj, chips_exchange([sums[nm] for nm in mix_names]))
    dxn = matmul("d_xn", dproj, w_main, "nt", lp, D, W_MAIN, F32, tn=512, tk=4096)
    dxn_ab = matmul("d_xn_ab", dproj_ab, w_ab, "nt", lp, D, LANES, F32)
    dw_main = matmul("dw_in_main", xn, dproj, "tn", D, W_MAIN, lp, BF16, tm=1024, tn=1024, tk=lp // 4)
    dw_ab = matmul("dw_in_ab", xn, dproj_ab, "tn", D, LANES, lp, BF16, tm=1024, tk=lp // 4)
    grad_x, dpre, d_mix_norm_w = norm1_bwd(pre, xs, mix_norm_w, dxn, dxn_ab, dh1)
    sums.update(chip_sums("w_in", {
        "w_in": cols_to_shards("dw_in_shards", [dw_main, dw_ab], W_IN // N_DEV, _w_in_pieces(), 256)}))
    from_chips = dict(zip(ffn_names + mix_names, list(from_chips_ffn) + list(from_chips_mix)))
    from_chips["w_in"], = run_exchange("exchange_chips_w_in", chips_exchange([sums["w_in"]]))
    names = ["w_in", "w_branch_a", "w_branch_b", "w_out", "w_ffn_in", "w_ffn_out"]
    moms = [(m_w_in, v_w_in), (m_w_branch_a, v_w_branch_a), (m_w_branch_b, v_w_branch_b), (m_w_out, v_w_out),
            (m_w_ffn_in, v_w_ffn_in), (m_w_ffn_out, v_w_ffn_out)]
    upd = {}
    for nm, w, (m, v) in zip(names, big, moms):
        own = lax.dynamic_index_in_dim(sums[nm], my_chip, axis=0, keepdims=False)
        g, d, nm_, nv_ = adamw_shard("adamw_" + nm, w, m[0], v[0], own, from_chips[nm], tiles[nm])
        upd[nm] = tuple(t_[None] for t_ in (g, d, nm_, nv_))

    d_lb = jnp.concatenate([dl0, dl1], axis=0)
    rep = [d_lb, d_mix_norm_w, d_hg_nw, d_alog, d_dtb, d_gd_nw, d_ffn_norm_w, d_final_w]
    buf = jnp.concatenate([_rows128(t) for t in rep] + [loss_part, _rows128(dpre[PRE - N_META:]), _rows128(d_conv)],
                          axis=0)
    n_rep = sum(_rows128(t).shape[0] for t in rep)
    n_rows = buf.shape[0]
    buf = jnp.pad(buf, ((0, -n_rows % 8), (0, 0)))
    tot = allreduce_small(buf)
    loss = tot[n_rep, 0]
    g_meta_full = tot[n_rep + 1:n_rep + 1 + N_META * D // LANES].reshape(N_META, D)
    g_conv_full = tot[n_rep + 1 + N_META * D // LANES:n_rows].reshape(CONV_K, 3 * HW)
    g_meta_mine = lax.dynamic_slice_in_dim(g_meta_full, me * (D // N_DEV), D // N_DEV, axis=1)
    g_conv_mine = lax.dynamic_slice_in_dim(g_conv_full, me * (3 * DH), 3 * DH, axis=1)

    small = [("lb_logits", lb_logits, m_lb_logits, v_lb_logits), ("mix_norm_w", mix_norm_w, m_mix_norm_w, v_mix_norm_w),
             ("hg_norm_w", hg_norm_w, m_hg_norm_w, v_hg_norm_w), ("gd_a_log", gd_a_log, m_gd_a_log, v_gd_a_log),
             ("gd_dt_bias", gd_dt_bias, m_gd_dt_bias, v_gd_dt_bias), ("gd_norm_w", gd_norm_w, m_gd_norm_w, v_gd_norm_w),
             ("ffn_norm_w", ffn_norm_w, m_ffn_norm_w, v_ffn_norm_w),
             ("final_norm_w", final_norm_w, m_final_norm_w, v_final_norm_w),
             ("meta_tokens", meta_tokens, m_meta_tokens, v_meta_tokens), ("gd_conv_w", gd_conv_w, m_gd_conv_w, v_gd_conv_w)]

    def pack(t):
        return _pad_lanes(t) if t.size < LANES else _rows128(t)

    sizes = [pack(w).shape[0] for _, w, _, _ in small]
    g_small = jnp.concatenate([tot[:n_rep], _rows128(g_meta_mine), _rows128(g_conv_mine)], axis=0)
    n_small = g_small.shape[0]
    assert n_small == sum(sizes)
    padr = lambda t: jnp.pad(t, ((0, -n_small % 8), (0, 0)))
    w_small = padr(jnp.concatenate([pack(w) for _, w, _, _ in small], axis=0))
    m_small = padr(jnp.concatenate([pack(m) for _, _, m, _ in small], axis=0))
    v_small = padr(jnp.concatenate([pack(v) for _, _, _, v in small], axis=0))
    g_small = padr(g_small)
    d_small, nm_small, nv_small = adamw_small(w_small, g_small, m_small, v_small)
    row = 0
    for (nm, w, _, _), sz in zip(small, sizes):
        def unpack(t):
            return t[row:row + sz].reshape(-1)[:w.size].reshape(w.shape)
        upd[nm] = (unpack(g_small), unpack(d_small), unpack(nm_small), unpack(nv_small))
        row += sz

    order = ["meta_tokens", "lb_logits", "mix_norm_w", "w_in", "hg_norm_w", "gd_conv_w", "gd_a_log", "gd_dt_bias",
             "gd_norm_w", "w_branch_a", "w_branch_b", "w_out", "ffn_norm_w", "w_ffn_in", "w_ffn_out", "final_norm_w"]
    outs = [loss, grad_x[None]]
    for j in range(4):
        outs += [upd[nm][j] for nm in order]
    return tuple(outs)
```

```python
import functools

import jax
import jax.numpy as jnp
from jax import lax
from jax.experimental import pallas as pl
from jax.experimental.pallas import tpu as pltpu

F32 = jnp.float32
BF16 = jnp.bfloat16
MESH = pl.DeviceIdType.MESH

EPS = 1e-6
D = 2048
N_META = 16
CHUNK = 64
SUB = 16
HEADS = 8
DH = 128
HW = HEADS * DH
CONV_K = 4
TAIL = 8
D_FF = 5632
PRE = 512
N_SKIP = PRE // CHUNK - 1
N_DEV = 8
LANES = 128
FF_BLK = 512
W_IN = 4 * HW + 4 * HW + 2 * HEADS + 2 * D
W_MAIN = 4 * HW + 4 * HW + 2 * D

ADAM_LR = 0.001
ADAM_B1 = 0.9
ADAM_B2 = 0.999
ADAM_EPS = 1e-08
ADAM_WD = 0.01
ADAM_STEP = 10

VMEM_LIMIT = 52 * 1024 * 1024

_NN = ((1,), (0,))
_NT = ((1,), (1,))
_TN = ((0,), (0,))


def _params(*sem):
    return pltpu.CompilerParams(dimension_semantics=sem, vmem_limit_bytes=VMEM_LIMIT)


BF16_ONCE, SPLIT, EXACT_LHS = 0, 1, 2


def _dot_bf16(a, b, dims):
    if a.ndim == 3:
        dn = (((dims[0][0] + 1,), (dims[1][0] + 1,)), ((0,), (0,)))
    else:
        dn = (dims, ((), ()))
    return lax.dot_general(a, b, dn, preferred_element_type=F32)


def _split(t):
    hi = t.astype(BF16)
    return hi, (t - hi.astype(F32)).astype(BF16)


def _raw_dot(a, b, dims, mode):
    if mode == BF16_ONCE:
        return _dot_bf16(a.astype(BF16), b.astype(BF16), dims)
    if mode == SPLIT:
        a_hi, a_lo = _split(a)
        b_hi, b_lo = _split(b)
        return _dot_bf16(a_hi, b_hi, dims) + (_dot_bf16(a_hi, b_lo, dims) + _dot_bf16(a_lo, b_hi, dims))
    a = a.astype(BF16)
    b_hi = b.astype(BF16)
    rest = b - b_hi.astype(F32)
    b_mid = rest.astype(BF16)
    b_lo = (rest - b_mid.astype(F32)).astype(BF16)
    return _dot_bf16(a, b_hi, dims) + (_dot_bf16(a, b_mid, dims) + _dot_bf16(a, b_lo, dims))


@functools.partial(jax.custom_vjp, nondiff_argnums=(2,))
def mm_nn(a, b, mode=BF16_ONCE):
    return _raw_dot(a, b, _NN, mode)


@functools.partial(jax.custom_vjp, nondiff_argnums=(2,))
def mm_nt(a, b, mode=BF16_ONCE):
    return _raw_dot(a, b, _NT, mode)


@functools.partial(jax.custom_vjp, nondiff_argnums=(2,))
def mm_tn(a, b, mode=BF16_ONCE):
    return _raw_dot(a, b, _TN, mode)


def _general(mode):
    return SPLIT if mode == EXACT_LHS else mode


mm_nn.defvjp(lambda a, b, p: (_raw_dot(a, b, _NN, p), (a, b)),
             lambda p, res, g: (mm_nt(g, res[1], _general(p)), mm_tn(res[0], g, p)))
mm_nt.defvjp(lambda a, b, p: (_raw_dot(a, b, _NT, p), (a, b)),
             lambda p, res, g: (mm_nn(g, res[1], p), mm_tn(g, res[0], p)))
mm_tn.defvjp(lambda a, b, p: (_raw_dot(a, b, _TN, p), (a, b)),
             lambda p, res, g: (mm_nt(res[1], g, _general(p)), mm_nn(res[0], g, _general(p))))


def _sigmoid(x):
    return 1.0 / (1.0 + jnp.exp(-x))


def _silu(x):
    return x * _sigmoid(x)


def _softplus(x):
    return jnp.maximum(x, 0.0) + jnp.log(1.0 + jnp.exp(-jnp.abs(x)))


def _l2n(t):
    return t * lax.rsqrt(jnp.sum(t * t, axis=-1, keepdims=True) + EPS)


def _rms_norm(x, w):
    return x * lax.rsqrt(jnp.mean(x * x, axis=-1, keepdims=True) + EPS) * w


def _gated_norm(o, gate, nw):
    o = o * lax.rsqrt(jnp.mean(o * o, axis=-1, keepdims=True) + EPS) * nw
    return o * _silu(gate)


def _heads(ref, piece):
    return jnp.stack([ref[:, piece * HW + DH * j:piece * HW + DH * (j + 1)] for j in range(HEADS)])


def _put_heads(ref, piece, value, accumulate=False):
    for j in range(HEADS):
        cols = slice(piece * HW + DH * j, piece * HW + DH * (j + 1))
        if accumulate:
            ref[:, cols] += value[j]
        else:
            ref[:, cols] = value[j].astype(ref.dtype)


def hg_chunk(hq, hf, hi, hg, l0, l1, nw, st):
    nb = hq.shape[0]
    m = jnp.maximum(l0, l1)
    e0 = jnp.exp(l0 - m)
    e1 = jnp.exp(l1 - m)
    lb = e0 / (e0 + e1)
    sig = _sigmoid(hf)
    q = _silu(hq)
    log_f = jnp.log(lb + (1.0 - lb) * sig)
    k = (1.0 - lb) * _sigmoid(-hf)
    v = hi
    rows = lax.broadcasted_iota(jnp.int32, (nb, CHUNK, CHUNK), 1)
    cols = lax.broadcasted_iota(jnp.int32, (nb, CHUNK, CHUNK), 2)
    b = mm_nn((rows >= cols).astype(F32), log_f, EXACT_LHS)
    o_inter = mm_nt(q * jnp.exp(b), st)
    tri_s = (lax.broadcasted_iota(jnp.int32, (nb, SUB, SUB, DH), 1)
             >= lax.broadcasted_iota(jnp.int32, (nb, SUB, SUB, DH), 2))
    outs = []
    for i in range(CHUNK // SUB):
        lo = i * SUB
        b_i, q_i, k_i, v_i = b[:, lo:lo + SUB], q[:, lo:lo + SUB], k[:, lo:lo + SUB], v[:, lo:lo + SUB]
        diff = b_i[:, :, None, :] - b_i[:, None, :, :]
        dec = jnp.where(tri_s, jnp.exp(jnp.minimum(diff, 0.0)), 0.0)
        s_ii = jnp.sum(q_i[:, :, None, :] * k_i[:, None, :, :] * dec, axis=-1)
        o_i = mm_nn(s_ii, v_i)
        if i > 0:
            b_ref = jnp.sum(log_f[:, :lo], axis=1, keepdims=True)
            q_t = q_i * jnp.exp(b_i - b_ref)
            k_t = k[:, :lo] * jnp.exp(b_ref - b[:, :lo])
            o_i = o_i + mm_nn(mm_nt(q_t, k_t), v[:, :lo])
        outs.append(o_i)
    o = o_inter + jnp.concatenate(outs, axis=1)
    b_last = jnp.sum(log_f, axis=1, keepdims=True)
    st_new = st * jnp.exp(b_last) + mm_tn(v, k * jnp.exp(b_last - b))
    return _gated_norm(o, hg, nw), st_new


def hgrn2_fwd(proj, l0, l1, nw):
    lp = proj.shape[0]
    nc = lp // CHUNK

    def body(x_ref, l0_ref, l1_ref, nw_ref, o_ref, st_out_ref, st_ref):
        c = pl.program_id(0)

        @pl.when(c == 0)
        def _():
            st_ref[...] = jnp.zeros(st_ref.shape, F32)

        @pl.when(c < N_SKIP)
        def _():
            o_ref[...] = jnp.zeros(o_ref.shape, o_ref.dtype)
            st_out_ref[...] = jnp.zeros(st_out_ref.shape, F32)

        @pl.when(c >= N_SKIP)
        def _():
            st = st_ref[...]
            st_out_ref[0] = st
            o, st_new = hg_chunk(_heads(x_ref, 0), _heads(x_ref, 1), _heads(x_ref, 2), _heads(x_ref, 3),
                                 _heads(l0_ref, 0), _heads(l1_ref, 0), nw_ref[...], st)
            _put_heads(o_ref, 0, o)
            st_ref[...] = st_new

    vec = pl.BlockSpec((1, HW), lambda c: (0, 0))
    return pl.pallas_call(
        body, name="hgrn2_fwd", grid=(nc,),
        in_specs=[pl.BlockSpec((CHUNK, 4 * HW), lambda c: (c, 0)), vec, vec, pl.BlockSpec((1, DH), lambda c: (0, 0))],
        out_specs=[pl.BlockSpec((CHUNK, HW), lambda c: (c, 0)),
                   pl.BlockSpec((1, HEADS, DH, DH), lambda c: (c, 0, 0, 0))],
        out_shape=[jax.ShapeDtypeStruct((lp, HW), BF16), jax.ShapeDtypeStruct((nc, HEADS, DH, DH), F32)],
        scratch_shapes=[pltpu.VMEM((HEADS, DH, DH), F32)],
        compiler_params=_params("arbitrary"),
    )(proj, l0, l1, nw)


def hgrn2_bwd(proj, l0, l1, nw, states, do, dproj, exch):
    lp = proj.shape[0]
    nc = lp // CHUNK

    def body(x_ref, l0_ref, l1_ref, nw_ref, st_in_ref, do_ref, _, dx_ref, dl0_ref, dl1_ref, dnw_ref, dst_ref):
        c = pl.program_id(0)
        cc = nc - 1 - c

        @pl.when(c == 0)
        def _():
            dst_ref[...] = jnp.zeros(dst_ref.shape, F32)
            dl0_ref[...] = jnp.zeros(dl0_ref.shape, F32)
            dl1_ref[...] = jnp.zeros(dl1_ref.shape, F32)
            dnw_ref[...] = jnp.zeros(dnw_ref.shape, F32)

        @pl.when(cc < N_SKIP)
        def _():
            dx_ref[...] = jnp.zeros(dx_ref.shape, dx_ref.dtype)

        @pl.when(cc >= N_SKIP)
        def _():
            _, vjp = jax.vjp(hg_chunk, _heads(x_ref, 0), _heads(x_ref, 1), _heads(x_ref, 2), _heads(x_ref, 3),
                             _heads(l0_ref, 0), _heads(l1_ref, 0), nw_ref[...], st_in_ref[0])
            grads = vjp((_heads(do_ref, 0), dst_ref[...]))
            for i in range(4):
                _put_heads(dx_ref, i, grads[i])
            _put_heads(dl0_ref, 0, grads[4], accumulate=True)
            _put_heads(dl1_ref, 0, grads[5], accumulate=True)
            dnw_ref[...] += grads[6]
            dst_ref[...] = grads[7]

    rc = lambda c: nc - 1 - c
    vec = pl.BlockSpec((1, HW), lambda c: (0, 0))
    one = pl.BlockSpec((1, DH), lambda c: (0, 0))
    outs = pl.pallas_call(
        host_exchange(body, 7, 4, exch, lambda: pl.program_id(0) == 0, lambda: pl.program_id(0) == nc - 1),
        name="hgrn2_bwd", grid=(nc,), input_output_aliases={6: 0}, compiler_params=_params("arbitrary"),
        **_hosted(exch,
                  [pl.BlockSpec((CHUNK, 4 * HW), lambda c: (rc(c), 0)), vec, vec, one,
                   pl.BlockSpec((1, HEADS, DH, DH), lambda c: (rc(c), 0, 0, 0)),
                   pl.BlockSpec((CHUNK, HW), lambda c: (rc(c), 0)), pl.BlockSpec(memory_space=pl.ANY)],
                  [pl.BlockSpec((CHUNK, 4 * HW), lambda c: (rc(c), 0)), vec, vec, one],
                  [jax.ShapeDtypeStruct((lp, W_MAIN), BF16), jax.ShapeDtypeStruct((1, HW), F32),
                   jax.ShapeDtypeStruct((1, HW), F32), jax.ShapeDtypeStruct((1, DH), F32)],
                  [pltpu.VMEM((HEADS, DH, DH), F32)]),
    )(proj, l0, l1, nw, states, do, dproj, *exch.arrays)
    return outs[:4], outs[4:]


def _unit_lower_inverse(a):
    n = a.shape[-1]
    eye = (lax.broadcasted_iota(jnp.int32, a.shape, 1) == lax.broadcasted_iota(jnp.int32, a.shape, 2)).astype(F32)
    x = eye - a
    p = mm_nn(a, a, SPLIT)
    x = x + mm_nn(x, p, SPLIT)
    power = 4
    while power < n:
        p = mm_nn(p, p, SPLIT)
        x = x + mm_nn(x, p, SPLIT)
        power *= 2
    return x


def gd_chunk(cq, ck, cv, z, ab, alog, dtb, nw, s):
    nb, c, _ = cq.shape
    lane = lax.broadcasted_iota(jnp.int32, (nb, 1, DH), 2)
    head = lax.broadcasted_iota(jnp.int32, (nb, 1, DH), 0)
    pick = lambda t, off: jnp.sum(jnp.where(lane == head + off, t[None], 0.0), axis=2, keepdims=True)
    a_raw, b_raw = pick(ab, 0), pick(ab, HEADS)
    beta = _sigmoid(b_raw)
    g = -jnp.exp(pick(alog, 0)) * _softplus(a_raw + pick(dtb, 0))
    q = _l2n(_silu(cq)) * (DH ** -0.5)
    k = _l2n(_silu(ck))
    v = _silu(cv)
    rows = lax.broadcasted_iota(jnp.int32, (nb, c, c), 1)
    cols = lax.broadcasted_iota(jnp.int32, (nb, c, c), 2)
    tril = (rows >= cols).astype(F32)
    gc_w = mm_nn(tril, jnp.broadcast_to(g, (nb, c, DH)), EXACT_LHS)
    g_sq = jnp.broadcast_to(g, (nb, c, c))
    gc_col = mm_nn(tril, g_sq, EXACT_LHS)
    gc_row = mm_nn(jnp.ones((nb, c, c), F32), g_sq * (rows <= cols).astype(F32), EXACT_LHS)
    rel = jnp.exp(jnp.minimum(gc_col - gc_row, 0.0))
    a = jnp.where(rows > cols, beta * mm_nt(k, k) * rel, 0.0)
    t_inv = _unit_lower_inverse(a)
    e_gc = jnp.exp(gc_w)
    w = mm_nn(t_inv, beta * e_gc * k, SPLIT)
    u = mm_nn(t_inv, beta * v, SPLIT)
    v_new = u - mm_nn(w, s)
    attn = jnp.where(rows >= cols, mm_nt(q, k) * rel, 0.0)
    o = mm_nn(q * e_gc, s) + mm_nn(attn, v_new)
    g_last = jnp.sum(g, axis=1, keepdims=True)
    s_new = jnp.exp(g_last) * s + mm_tn(k * jnp.exp(g_last - gc_w), v_new)
    return _gated_norm(o, z, nw), s_new


def _gd_conv(x_ref, tail_ref, cw_ref, xe_ref, first):
    xe_ref[0:TAIL, :] = jnp.where(first, 0.0, tail_ref[:, 0:3 * HW])
    xe_ref[TAIL:TAIL + CHUNK, :] = x_ref[:, 0:3 * HW]
    y = cw_ref[pl.ds(0, 1), :] * xe_ref[pl.ds(TAIL - CONV_K + 1, CHUNK), :]
    for k in range(1, CONV_K):
        y = y + cw_ref[pl.ds(k, 1), :] * xe_ref[pl.ds(TAIL - CONV_K + 1 + k, CHUNK), :]
    return y


def _conv_heads(y, piece):
    return jnp.stack([y[:, piece * HW + DH * j:piece * HW + DH * (j + 1)] for j in range(HEADS)])


def gdn_fwd(proj, proj_ab, cw, alog, dtb, nw):
    lp = proj.shape[0]
    nc = lp // CHUNK

    def body(x_ref, tail_ref, ab_ref, cw_ref, alog_ref, dtb_ref, nw_ref, o_ref, st_out_ref, st_ref, xe_ref):
        c = pl.program_id(0)

        @pl.when(c == 0)
        def _():
            st_ref[...] = jnp.zeros(st_ref.shape, F32)

        @pl.when(c < N_SKIP)
        def _():
            o_ref[...] = jnp.zeros(o_ref.shape, o_ref.dtype)
            st_out_ref[...] = jnp.zeros(st_out_ref.shape, F32)

        @pl.when(c >= N_SKIP)
        def _():
            st = st_ref[...]
            st_out_ref[0] = st
            y = _gd_conv(x_ref, tail_ref, cw_ref, xe_ref, c == 0)
            o, st_new = gd_chunk(_conv_heads(y, 0), _conv_heads(y, 1), _conv_heads(y, 2), _heads(x_ref, 3),
                                 ab_ref[...], alog_ref[...], dtb_ref[...], nw_ref[...], st)
            _put_heads(o_ref, 0, o)
            st_ref[...] = st_new

    one = pl.BlockSpec((1, DH), lambda c: (0, 0))
    return pl.pallas_call(
        body, name="gdn_fwd", grid=(nc,),
        in_specs=[pl.BlockSpec((CHUNK, 4 * HW), lambda c: (c, 1)),
                  pl.BlockSpec((TAIL, 4 * HW), lambda c: (jnp.maximum(c * (CHUNK // TAIL) - 1, 0), 1)),
                  pl.BlockSpec((CHUNK, DH), lambda c: (c, 0)),
                  pl.BlockSpec((CONV_K, 3 * HW), lambda c: (0, 0)), one, one, one],
        out_specs=[pl.BlockSpec((CHUNK, HW), lambda c: (c, 0)),
                   pl.BlockSpec((1, HEADS, DH, DH), lambda c: (c, 0, 0, 0))],
        out_shape=[jax.ShapeDtypeStruct((lp, HW), BF16), jax.ShapeDtypeStruct((nc, HEADS, DH, DH), F32)],
        scratch_shapes=[pltpu.VMEM((HEADS, DH, DH), F32), pltpu.VMEM((TAIL + CHUNK, 3 * HW), F32)],
        compiler_params=_params("arbitrary"),
    )(proj, proj, proj_ab, cw, alog, dtb, nw)


def gdn_bwd(proj, proj_ab, cw, alog, dtb, nw, states, do, dproj, exch):
    lp = proj.shape[0]
    nc = lp // CHUNK

    def body(x_ref, tail_ref, ab_ref, cw_ref, alog_ref, dtb_ref, nw_ref, st_in_ref, do_ref, _,
             dx_ref, dab_ref, dcw_ref, dalog_ref, ddtb_ref, dnw_ref, dst_ref, xe_ref, dye_ref, head_ref):
        c = pl.program_id(0)
        cc = nc - 1 - c

        @pl.when(c == 0)
        def _():
            dcw_ref[...] = jnp.zeros(dcw_ref.shape, F32)
            dalog_ref[...] = jnp.zeros(dalog_ref.shape, F32)
            ddtb_ref[...] = jnp.zeros(ddtb_ref.shape, F32)
            dnw_ref[...] = jnp.zeros(dnw_ref.shape, F32)
            dst_ref[...] = jnp.zeros(dst_ref.shape, F32)
            head_ref[...] = jnp.zeros(head_ref.shape, F32)

        @pl.when(cc < N_SKIP)
        def _():
            dx_ref[...] = jnp.zeros(dx_ref.shape, dx_ref.dtype)
            dab_ref[...] = jnp.zeros(dab_ref.shape, F32)

        @pl.when(cc >= N_SKIP)
        def _():
            y = _gd_conv(x_ref, tail_ref, cw_ref, xe_ref, cc == 0)
            _, vjp = jax.vjp(gd_chunk, _conv_heads(y, 0), _conv_heads(y, 1), _conv_heads(y, 2), _heads(x_ref, 3),
                             ab_ref[...], alog_ref[...], dtb_ref[...], nw_ref[...], st_in_ref[0])
            dcq, dck, dcv, dz, dab, dalog, ddtb, dnw, dst = vjp((_heads(do_ref, 0), dst_ref[...]))
            dst_ref[...] = dst
            dab_ref[...] = dab
            dalog_ref[...] += dalog
            ddtb_ref[...] += ddtb
            dnw_ref[...] += dnw
            _put_heads(dx_ref, 3, dz)

            for i, t in enumerate((dcq, dck, dcv)):
                for j in range(HEADS):
                    dye_ref[0:CHUNK, i * HW + DH * j:i * HW + DH * (j + 1)] = t[j]
            dye_ref[CHUNK:CHUNK + TAIL, :] = head_ref[...]
            head_ref[...] = dye_ref[0:TAIL, :]
            dy = dye_ref[0:CHUNK, :]
            dx = None
            for k in range(CONV_K):
                term = cw_ref[pl.ds(k, 1), :] * dye_ref[pl.ds(CONV_K - 1 - k, CHUNK), :]
                dx = term if dx is None else dx + term
                dcw_ref[pl.ds(k, 1), :] += jnp.sum(dy * xe_ref[pl.ds(TAIL - CONV_K + 1 + k, CHUNK), :],
                                                   axis=0, keepdims=True)
            dx_ref[:, 0:3 * HW] = dx.astype(dx_ref.dtype)

    rc = lambda c: nc - 1 - c
    one = pl.BlockSpec((1, DH), lambda c: (0, 0))
    one_shape = jax.ShapeDtypeStruct((1, DH), F32)
    outs = pl.pallas_call(
        host_exchange(body, 10, 6, exch, lambda: pl.program_id(0) == 0, lambda: pl.program_id(0) == nc - 1),
        name="gdn_bwd", grid=(nc,), input_output_aliases={9: 0}, compiler_params=_params("arbitrary"),
        **_hosted(exch,
                  [pl.BlockSpec((CHUNK, 4 * HW), lambda c: (rc(c), 1)),
                   pl.BlockSpec((TAIL, 4 * HW), lambda c: (jnp.maximum(rc(c) * (CHUNK // TAIL) - 1, 0), 1)),
                   pl.BlockSpec((CHUNK, DH), lambda c: (rc(c), 0)),
                   pl.BlockSpec((CONV_K, 3 * HW), lambda c: (0, 0)), one, one, one,
                   pl.BlockSpec((1, HEADS, DH, DH), lambda c: (rc(c), 0, 0, 0)),
                   pl.BlockSpec((CHUNK, HW), lambda c: (rc(c), 0)),
                   pl.BlockSpec(memory_space=pl.ANY)],
                  [pl.BlockSpec((CHUNK, 4 * HW), lambda c: (rc(c), 1)),
                   pl.BlockSpec((CHUNK, DH), lambda c: (rc(c), 0)),
                   pl.BlockSpec((CONV_K, 3 * HW), lambda c: (0, 0)), one, one, one],
                  [jax.ShapeDtypeStruct((lp, W_MAIN), BF16), jax.ShapeDtypeStruct((lp, DH), F32),
                   jax.ShapeDtypeStruct((CONV_K, 3 * HW), F32), one_shape, one_shape, one_shape],
                  [pltpu.VMEM((HEADS, DH, DH), F32), pltpu.VMEM((TAIL + CHUNK, 3 * HW), F32),
                   pltpu.VMEM((CHUNK + TAIL, 3 * HW), F32), pltpu.VMEM((TAIL, 3 * HW), F32)]),
    )(proj, proj, proj_ab, cw, alog, dtb, nw, states, do, dproj, *exch.arrays)
    return outs[:6], outs[6:]


def matmul(name, a, b, mode, m, n, k, out_dtype, tm=512, tn=1024, tk=None, a_off=(0, 0), b_off=(0, 0), exch=None):
    tm, tn = min(tm, m), min(tn, n)
    tk = k if tk is None else min(tk, k)
    assert m % tm == 0 and n % tn == 0 and k % tk == 0, (name, m, n, k, tm, tn, tk)
    gi, gj, gk = m // tm, n // tn, k // tk
    if mode == "nn":
        a_spec = pl.BlockSpec((tm, tk), lambda j, i, kk: (i + a_off[0], kk + a_off[1]))
        b_spec = pl.BlockSpec((tk, tn), lambda j, i, kk: (kk + b_off[0], j + b_off[1]))
        dims = _NN
    elif mode == "nt":
        a_spec = pl.BlockSpec((tm, tk), lambda j, i, kk: (i + a_off[0], kk + a_off[1]))
        b_spec = pl.BlockSpec((tn, tk), lambda j, i, kk: (j + b_off[0], kk + b_off[1]))
        dims = _NT
    else:
        a_spec = pl.BlockSpec((tk, tm), lambda j, i, kk: (kk + a_off[0], i + a_off[1]))
        b_spec = pl.BlockSpec((tk, tn), lambda j, i, kk: (kk + b_off[0], j + b_off[1]))
        dims = _TN

    def body(a_ref, b_ref, o_ref, *acc):
        d = lax.dot_general(a_ref[...].astype(BF16), b_ref[...].astype(BF16), (dims, ((), ())),
                            preferred_element_type=F32)
        if gk == 1:
            o_ref[...] = d.astype(o_ref.dtype)
        else:
            acc_ref, = acc
            kk = pl.program_id(2)

            @pl.when(kk == 0)
            def _():
                acc_ref[...] = d

            @pl.when(kk > 0)
            def _():
                acc_ref[...] += d

            @pl.when(kk == gk - 1)
            def _():
                o_ref[...] = acc_ref[...].astype(o_ref.dtype)

    o_spec = pl.BlockSpec((tm, tn), lambda j, i, kk: (i, j))
    o_shape = jax.ShapeDtypeStruct((m, n), out_dtype)
    scratch = [] if gk == 1 else [pltpu.VMEM((tm, tn), F32)]
    if exch is None:
        return pl.pallas_call(
            body, name=name, grid=(gj, gi, gk), in_specs=[a_spec, b_spec], out_specs=o_spec, out_shape=o_shape,
            scratch_shapes=scratch, compiler_params=_params("parallel", "parallel", "arbitrary"),
        )(a, b)
    step = lambda: (pl.program_id(0), pl.program_id(1), pl.program_id(2))
    first = lambda: jnp.logical_and(jnp.logical_and(step()[0] == 0, step()[1] == 0), step()[2] == 0)
    last = lambda: jnp.logical_and(jnp.logical_and(step()[0] == gj - 1, step()[1] == gi - 1), step()[2] == gk - 1)
    outs = pl.pallas_call(
        host_exchange(body, 2, 1, exch, first, last), name=name, grid=(gj, gi, gk),
        compiler_params=_params("arbitrary", "arbitrary", "arbitrary"),
        **_hosted(exch, [a_spec, b_spec], [o_spec], [o_shape], scratch),
    )(a, b, *exch.arrays)
    return outs[0], outs[1:]


def _swiglu(gu):
    return _silu(gu[:, :FF_BLK]) * gu[:, FF_BLK:]


def ffn_in_fused(xn2, w_gu, tm=512):
    l = xn2.shape[0]
    tn = 2 * FF_BLK

    def body(a_ref, b_ref, gu_ref, act_ref):
        gu = lax.dot_general(a_ref[...], b_ref[...], (_NN, ((), ())), preferred_element_type=F32)
        gu_ref[...] = gu
        act_ref[...] = _swiglu(gu).astype(act_ref.dtype)

    return pl.pallas_call(
        body, name="ffn_in", grid=(2 * D_FF // tn, l // tm),
        in_specs=[pl.BlockSpec((tm, D), lambda j, i: (i, 0)), pl.BlockSpec((D, tn), lambda j, i: (0, j))],
        out_specs=[pl.BlockSpec((tm, tn), lambda j, i: (i, j)), pl.BlockSpec((tm, FF_BLK), lambda j, i: (i, j))],
        out_shape=[jax.ShapeDtypeStruct((l, 2 * D_FF), F32), jax.ShapeDtypeStruct((l, D_FF), BF16)],
        compiler_params=_params("parallel", "parallel"),
    )(xn2, w_gu)


def d_act_fused(dh2, w_fo, gu, tm=512):
    l = dh2.shape[0]

    def body(a_ref, b_ref, gu_ref, o_ref):
        da = lax.dot_general(a_ref[...], b_ref[...], (_NT, ((), ())), preferred_element_type=F32)
        _, vjp = jax.vjp(_swiglu, gu_ref[...])
        dgu, = vjp(da)
        o_ref[...] = dgu.astype(o_ref.dtype)

    return pl.pallas_call(
        body, name="d_act", grid=(D_FF // FF_BLK, l // tm),
        in_specs=[pl.BlockSpec((tm, D), lambda j, i: (i, 0)), pl.BlockSpec((FF_BLK, D), lambda j, i: (j, 0)),
                  pl.BlockSpec((tm, 2 * FF_BLK), lambda j, i: (i, j))],
        out_specs=pl.BlockSpec((tm, 2 * FF_BLK), lambda j, i: (i, j)),
        out_shape=jax.ShapeDtypeStruct((l, 2 * D_FF), BF16),
        compiler_params=_params("parallel", "parallel"),
    )(dh2, w_fo, gu)


TR = 256
N_PRE = PRE // TR


def _row(width, off=0, col=0):
    return pl.BlockSpec((TR, width), lambda i: (i + off, col))


def _pre_spec():
    return pl.BlockSpec((TR, D), lambda i: (jnp.minimum(i, N_PRE - 1), 0))


def _seq_spec(width=D, col=0):
    return pl.BlockSpec((TR, width), lambda i: (jnp.maximum(i - N_PRE, 0), col))


def _vec_spec(width=D):
    return pl.BlockSpec((1, width), lambda i: (0, 0))


def norm1_fwd(pre, x, w):
    lp = PRE + x.shape[0]

    def body(pre_ref, x_ref, w_ref, o_ref):
        h = jnp.where(pl.program_id(0) < N_PRE, pre_ref[...], x_ref[...])
        o_ref[...] = _rms_norm(h, w_ref[...]).astype(o_ref.dtype)

    return pl.pallas_call(
        body, name="norm1_fwd", grid=(lp // TR,),
        in_specs=[_pre_spec(), _seq_spec(), _vec_spec()],
        out_specs=_row(D), out_shape=jax.ShapeDtypeStruct((lp, D), BF16),
        compiler_params=_params("parallel"),
    )(pre, x, w)


def norm1_bwd(pre, x, w, dxn, dxn_ab, dh1):
    l = x.shape[0]
    lp = PRE + l

    def body(pre_ref, x_ref, w_ref, dxn_ref, dxn_ab_ref, dh1_ref, dx_ref, dpre_ref, dw_ref):
        i = pl.program_id(0)
        h = jnp.where(i < N_PRE, pre_ref[...], x_ref[...])
        _, vjp = jax.vjp(_rms_norm, h, w_ref[...])
        dh, dw = vjp(dxn_ref[...] + dxn_ab_ref[...])

        @pl.when(i == 0)
        def _():
            dw_ref[...] = dw

        @pl.when(i > 0)
        def _():
            dw_ref[...] += dw

        @pl.when(i < N_PRE)
        def _():
            dpre_ref[...] = dh

        @pl.when(i >= N_PRE)
        def _():
            dx_ref[...] = dh + dh1_ref[...]

    return pl.pallas_call(
        body, name="norm1_bwd", grid=(lp // TR,),
        in_specs=[_pre_spec(), _seq_spec(), _vec_spec(), _row(D), _row(D), _seq_spec()],
        out_specs=[_seq_spec(), _pre_spec(), _vec_spec()],
        out_shape=[jax.ShapeDtypeStruct((l, D), F32), jax.ShapeDtypeStruct((PRE, D), F32),
                   jax.ShapeDtypeStruct((1, D), F32)],
        compiler_params=_params("arbitrary"),
    )(pre, x, w, dxn, dxn_ab, dh1)


def _merge(gates, ya, yb):
    return _sigmoid(gates[:, :D]) * ya + _sigmoid(gates[:, D:]) * yb


def merge_fwd(proj, ya, yb):
    l = ya.shape[0]

    def body(g_ref, ya_ref, yb_ref, o_ref):
        o_ref[...] = _merge(g_ref[...], ya_ref[...], yb_ref[...]).astype(o_ref.dtype)

    return pl.pallas_call(
        body, name="merge_fwd", grid=(l // TR,),
        in_specs=[_row(2 * D, N_PRE, 2), _row(D), _row(D)],
        out_specs=_row(D), out_shape=jax.ShapeDtypeStruct((l, D), BF16),
        compiler_params=_params("parallel"),
    )(proj, ya, yb)


def merge_bwd(proj, ya, yb, dmerged):
    l = ya.shape[0]
    lp = PRE + l

    def body(g_ref, ya_ref, yb_ref, dm_ref, dg_ref, dya_ref, dyb_ref):
        i = pl.program_id(0)

        @pl.when(i < N_PRE)
        def _():
            dg_ref[...] = jnp.zeros(dg_ref.shape, dg_ref.dtype)
            dya_ref[...] = jnp.zeros(dya_ref.shape, dya_ref.dtype)
            dyb_ref[...] = jnp.zeros(dyb_ref.shape, dyb_ref.dtype)

        @pl.when(i >= N_PRE)
        def _():
            _, vjp = jax.vjp(_merge, g_ref[...], ya_ref[...], yb_ref[...])
            dg, dya, dyb = vjp(dm_ref[...])
            dg_ref[...] = dg.astype(dg_ref.dtype)
            dya_ref[...] = dya.astype(dya_ref.dtype)
            dyb_ref[...] = dyb.astype(dyb_ref.dtype)

    return pl.pallas_call(
        body, name="merge_bwd", grid=(lp // TR,),
        in_specs=[pl.BlockSpec((TR, 2 * D), lambda i: (jnp.maximum(i, N_PRE), 2)), _seq_spec(), _seq_spec(),
                  _seq_spec()],
        out_specs=[_row(2 * D, 0, 2), _row(D), _row(D)],
        out_shape=[jax.ShapeDtypeStruct((lp, W_MAIN), BF16), jax.ShapeDtypeStruct((lp, D), BF16),
                   jax.ShapeDtypeStruct((lp, D), BF16)],
        compiler_params=_params("parallel"),
    )(proj, ya, yb, dmerged)


def norm2_fwd(x, mo, w):
    l = x.shape[0]

    def body(x_ref, mo_ref, w_ref, h_ref, o_ref):
        h = x_ref[...] + mo_ref[...]
        h_ref[...] = h
        o_ref[...] = _rms_norm(h, w_ref[...]).astype(o_ref.dtype)

    return pl.pallas_call(
        body, name="norm2_fwd", grid=(l // TR,),
        in_specs=[_row(D), _row(D), _vec_spec()],
        out_specs=[_row(D), _row(D)],
        out_shape=[jax.ShapeDtypeStruct((l, D), F32), jax.ShapeDtypeStruct((l, D), BF16)],
        compiler_params=_params("parallel"),
    )(x, mo, w)


def norm2_bwd(h1, w, dxn2, dh2):
    l = h1.shape[0]

    def body(h_ref, w_ref, dxn_ref, dh2_ref, dh1_ref, dh1b_ref, dw_ref):
        i = pl.program_id(0)
        _, vjp = jax.vjp(_rms_norm, h_ref[...], w_ref[...])
        dh, dw = vjp(dxn_ref[...])
        dh1 = dh + dh2_ref[...]
        dh1_ref[...] = dh1
        dh1b_ref[...] = dh1.astype(BF16)

        @pl.when(i == 0)
        def _():
            dw_ref[...] = dw

        @pl.when(i > 0)
        def _():
            dw_ref[...] += dw

    return pl.pallas_call(
        body, name="norm2_bwd", grid=(l // TR,),
        in_specs=[_row(D), _vec_spec(), _row(D), _row(D)],
        out_specs=[_row(D), _row(D), _vec_spec()],
        out_shape=[jax.ShapeDtypeStruct((l, D), F32), jax.ShapeDtypeStruct((l, D), BF16),
                   jax.ShapeDtypeStruct((1, D), F32)],
        compiler_params=_params("arbitrary"),
    )(h1, w, dxn2, dh2)


def _loss_rows(h1, f, w, tgt):
    y = _rms_norm(h1 + f, w)
    err = (y - tgt) * (y - tgt)
    return 0.5 * jnp.sum(jnp.mean(err, axis=-1, keepdims=True), axis=0, keepdims=True)


def loss_head(h1, f, w, tgt):
    l = h1.shape[0]

    def body(h_ref, f_ref, w_ref, t_ref, loss_ref, dh_ref, dhb_ref, dw_ref):
        i = pl.program_id(0)
        loss, vjp = jax.vjp(_loss_rows, h_ref[...], f_ref[...], w_ref[...], t_ref[...])
        dh, _, dw, _ = vjp(jnp.ones((1, 1), F32))
        dh_ref[...] = dh
        dhb_ref[...] = dh.astype(BF16)
        loss = jnp.broadcast_to(loss, (1, LANES))

        @pl.when(i == 0)
        def _():
            dw_ref[...] = dw
            loss_ref[...] = loss

        @pl.when(i > 0)
        def _():
            dw_ref[...] += dw
            loss_ref[...] += loss

    return pl.pallas_call(
        body, name="loss_head", grid=(l // TR,),
        in_specs=[_row(D), _row(D), _vec_spec(), _row(D)],
        out_specs=[_vec_spec(LANES), _row(D), _row(D), _vec_spec()],
        out_shape=[jax.ShapeDtypeStruct((1, LANES), F32), jax.ShapeDtypeStruct((l, D), F32),
                   jax.ShapeDtypeStruct((l, D), BF16), jax.ShapeDtypeStruct((1, D), F32)],
        compiler_params=_params("arbitrary"),
    )(h1, f, w, tgt)


def _adamw(w, g, m, v):
    m = ADAM_B1 * m + (1.0 - ADAM_B1) * g
    v = ADAM_B2 * v + (1.0 - ADAM_B2) * (g * g)
    m_hat = m / (1.0 - ADAM_B1 ** ADAM_STEP)
    v_hat = v / (1.0 - ADAM_B2 ** ADAM_STEP)
    delta = -ADAM_LR * (m_hat / (jnp.sqrt(v_hat) + ADAM_EPS) + ADAM_WD * w)
    return delta, m, v


def adamw_shard(name, w, m, v, own, recv, tile):
    r, c = w.shape
    tr, tc = tile
    assert r % tr == 0 and c % tc == 0

    def body(w_ref, m_ref, v_ref, own_ref, r0_ref, r1_ref, r2_ref, g_ref, d_ref, nm_ref, nv_ref):
        g = own_ref[...].astype(F32) + r0_ref[...].astype(F32)
        g = g + r1_ref[...].astype(F32)
        g = g + r2_ref[...].astype(F32)
        d, nm, nv = _adamw(w_ref[...], g, m_ref[...], v_ref[...])
        g_ref[...] = g
        d_ref[...] = d
        nm_ref[...] = nm
        nv_ref[...] = nv

    blk = pl.BlockSpec((tr, tc), lambda i, j: (i, j))
    part = lambda s: pl.BlockSpec((None, tr, tc), lambda i, j: (s, i, j))
    shape = jax.ShapeDtypeStruct((r, c), F32)
    return pl.pallas_call(
        body, name=name, grid=(r // tr, c // tc),
        in_specs=[blk, blk, blk, blk, part(0), part(1), part(2)],
        out_specs=[blk, blk, blk, blk], out_shape=[shape, shape, shape, shape],
        compiler_params=_params("parallel", "parallel"),
    )(w, m, v, own, recv, recv, recv)


def adamw_small(w, g, m, v):
    def body(w_ref, g_ref, m_ref, v_ref, d_ref, nm_ref, nv_ref):
        d, nm, nv = _adamw(w_ref[...], g_ref[...], m_ref[...], v_ref[...])
        d_ref[...] = d
        nm_ref[...] = nm
        nv_ref[...] = nv

    shape = jax.ShapeDtypeStruct(w.shape, F32)
    return pl.pallas_call(body, name="adamw_small", out_shape=[shape, shape, shape])(w, g, m, v)


def pair_add(name, a, b, tile):
    q, r, c = a.shape
    tr, tc = tile
    assert r % tr == 0 and c % tc == 0

    def body(a_ref, b_ref, o_ref):
        o_ref[...] = (a_ref[...].astype(F32) + b_ref[...].astype(F32)).astype(o_ref.dtype)

    blk = pl.BlockSpec((None, tr, tc), lambda s, i, j: (s, i, j))
    return pl.pallas_call(
        body, name=name, grid=(q, r // tr, c // tc), in_specs=[blk, blk], out_specs=blk,
        out_shape=jax.ShapeDtypeStruct((q, r, c), BF16),
        compiler_params=_params("parallel", "parallel", "parallel"),
    )(a, b)


def _w_in_pieces():
    c = W_IN // N_DEV
    ranges = ((0, 8 * HW, 0, 0), (8 * HW, 8 * HW + 2 * HEADS, 1, 8 * HW), (8 * HW + 2 * HEADS, W_IN, 0, 2 * HEADS))
    out = []
    for d in range(N_DEV):
        for lo, hi, target, shift in ranges:
            s, e = max(lo, c * d), min(hi, c * (d + 1))
            if s < e:
                out.append((d, s - c * d, e - s, target, s - shift))
    return out


def _ffn_in_pieces():
    c = 2 * D_FF // N_DEV
    out = []
    for d in range(N_DEV):
        s = c * d
        while s < c * (d + 1):
            e = min((s // FF_BLK + 1) * FF_BLK, c * (d + 1))
            half, blk = divmod(s // FF_BLK, D_FF // FF_BLK)
            out.append((d, s - c * d, e - s, 0, (2 * blk + half) * FF_BLK + s % FF_BLK))
            s = e
    return out


def _plain_pieces(c):
    return [(d, 0, c, 0, c * d) for d in range(N_DEV)]


def shards_to_cols(name, g, widths, pieces, tr):
    _, r, c = g.shape
    covered = [sum(p[2] for p in pieces if p[3] == t) for t in range(len(widths))]

    def body(g_ref, *outs):
        for t, o in enumerate(outs):
            if covered[t] < widths[t]:
                o[...] = jnp.zeros(o.shape, o.dtype)
        for d, s, w, t, ts in pieces:
            outs[t][:, ts:ts + w] = g_ref[d, :, s:s + w]

    return pl.pallas_call(
        body, name=name, grid=(r // tr,),
        in_specs=[pl.BlockSpec((N_DEV, tr, c), lambda i: (0, i, 0))],
        out_specs=[pl.BlockSpec((tr, w), lambda i: (i, 0)) for w in widths],
        out_shape=[jax.ShapeDtypeStruct((r, w), g.dtype) for w in widths],
        compiler_params=_params("parallel"),
    )(g)


def cols_to_shards(name, srcs, c, pieces, tr):
    r = srcs[0].shape[0]

    def body(*refs):
        o = refs[-1]
        for d, s, w, t, ts in pieces:
            o[d, :, s:s + w] = refs[t][:, ts:ts + w]

    return pl.pallas_call(
        body, name=name, grid=(r // tr,),
        in_specs=[pl.BlockSpec((tr, t.shape[1]), lambda i: (i, 0)) for t in srcs],
        out_specs=pl.BlockSpec((N_DEV, tr, c), lambda i: (0, i, 0)),
        out_shape=jax.ShapeDtypeStruct((N_DEV, r, c), srcs[0].dtype),
        compiler_params=_params("parallel"),
    )(*srcs)


def rows_to_shards(name, srcs, c, pieces, tc):
    r = srcs[0].shape[1]

    def body(*refs):
        o = refs[-1]
        for d, s, w, t, ts in pieces:
            o[d, s:s + w, :] = refs[t][ts:ts + w, :]

    return pl.pallas_call(
        body, name=name, grid=(r // tc,),
        in_specs=[pl.BlockSpec((t.shape[0], tc), lambda i: (0, i)) for t in srcs],
        out_specs=pl.BlockSpec((N_DEV, c, tc), lambda i: (0, 0, i)),
        out_shape=jax.ShapeDtypeStruct((N_DEV, c, r), srcs[0].dtype),
        compiler_params=_params("parallel"),
    )(*srcs)


def _place():
    return lax.axis_index("x"), lax.axis_index("y"), lax.axis_index("c")


def _dev(px, py, pc):
    return 4 * px + 2 * py + pc


class Exchange:
    def __init__(self, arrays, out_shapes, sems, start, finish):
        self.arrays, self.out_shapes, self.sems, self.start, self.finish = arrays, out_shapes, sems, start, finish


def gather_exchange(arrays):
    n = len(arrays)

    def plan(ins, outs, sems):
        send_sems, recv_sems, local_sems = sems
        x, y, c = _place()
        me, sibling = (x, y, c), (x, y, 1 - c)
        chips = [(1 - x, y), (x, 1 - y), (1 - x, 1 - y)]

        def copy(a, k, block, to, src=None):
            dst = outs[a].at[_dev(*block)]
            return pltpu.make_async_remote_copy(
                src_ref=dst if src is None else src, dst_ref=dst, send_sem=send_sems.at[a, k],
                recv_sem=recv_sems.at[a, k], device_id=to, device_id_type=MESH)

        mine = [pltpu.make_async_copy(ins[a], outs[a].at[_dev(*me)], local_sems.at[a]) for a in range(n)]
        first = []
        for a in range(n):
            first.append(copy(a, 0, me, sibling, src=ins[a]))
            first += [copy(a, 1 + j, me, (*chip, c), src=ins[a]) for j, chip in enumerate(chips)]
        return me, sibling, chips, c, copy, mine, first

    def start(ins, outs, sems):
        *_, mine, first = plan(ins, outs, sems)
        for cp in mine + first:
            cp.start()

    def finish(ins, outs, sems):
        me, sibling, chips, c, copy, mine, first = plan(ins, outs, sems)
        passed = []
        for j, chip in enumerate(chips):
            for a in range(n):
                copy(a, 1 + j, (*chip, c), me).wait_recv()
                fwd = copy(a, 4 + j, (*chip, c), sibling)
                fwd.start()
                passed.append(fwd)
        for a in range(n):
            copy(a, 0, sibling, me).wait_recv()
        for j, chip in enumerate(chips):
            for a in range(n):
                copy(a, 4 + j, (*chip, 1 - c), me).wait_recv()
        for cp in first + passed:
            cp.wait_send()
        for cp in mine:
            cp.wait()

    return Exchange(arrays, [jax.ShapeDtypeStruct((N_DEV,) + t.shape, t.dtype) for t in arrays],
                    [pltpu.SemaphoreType.DMA((n, 7)), pltpu.SemaphoreType.DMA((n, 7)), pltpu.SemaphoreType.DMA((n,))],
                    start, finish)


def core_exchange(arrays):
    n = len(arrays)

    def copies(ins, outs, sems):
        send_sems, recv_sems = sems
        x, y, c = _place()
        return [pltpu.make_async_remote_copy(
            src_ref=ins[a].at[q, 1 - c], dst_ref=outs[a].at[q], send_sem=send_sems.at[a, q],
            recv_sem=recv_sems.at[a, q], device_id=(x, y, 1 - c), device_id_type=MESH)
            for a in range(n) for q in range(4)]

    def start(ins, outs, sems):
        for cp in copies(ins, outs, sems):
            cp.start()

    def finish(ins, outs, sems):
        for cp in copies(ins, outs, sems):
            cp.wait()

    return Exchange(arrays, [jax.ShapeDtypeStruct((4,) + t.shape[2:], t.dtype) for t in arrays],
                    [pltpu.SemaphoreType.DMA((n, 4)), pltpu.SemaphoreType.DMA((n, 4))], start, finish)


def chips_exchange(arrays):
    n = len(arrays)

    def copies(ins, outs, sems):
        send_sems, recv_sems = sems
        x, y, c = _place()
        chips = [(1 - x, y), (x, 1 - y), (1 - x, 1 - y)]
        return [pltpu.make_async_remote_copy(
            src_ref=ins[a].at[2 * qx + qy], dst_ref=outs[a].at[j], send_sem=send_sems.at[a, j],
            recv_sem=recv_sems.at[a, j], device_id=(qx, qy, c), device_id_type=MESH)
            for a in range(n) for j, (qx, qy) in enumerate(chips)]

    def start(ins, outs, sems):
        for cp in copies(ins, outs, sems):
            cp.start()

    def finish(ins, outs, sems):
        for cp in copies(ins, outs, sems):
            cp.wait()

    return Exchange(arrays, [jax.ShapeDtypeStruct((3,) + t.shape[1:], t.dtype) for t in arrays],
                    [pltpu.SemaphoreType.DMA((n, 3)), pltpu.SemaphoreType.DMA((n, 3))], start, finish)


def run_exchange(name, exch):
    n_in, n_out = len(exch.arrays), len(exch.out_shapes)
    any_spec = pl.BlockSpec(memory_space=pl.ANY)

    def body(*refs):
        ins, outs, sems = refs[:n_in], refs[n_in:n_in + n_out], refs[n_in + n_out:]
        exch.start(ins, outs, sems)
        exch.finish(ins, outs, sems)

    return pl.pallas_call(
        body, name=name, in_specs=[any_spec] * n_in, out_specs=[any_spec] * n_out, out_shape=exch.out_shapes,
        scratch_shapes=exch.sems,
    )(*exch.arrays)


def host_exchange(body, n_in, n_out, exch, first, last):
    ne_in, ne_out, ne_sem = len(exch.arrays), len(exch.out_shapes), len(exch.sems)

    def wrapped(*refs):
        ins, refs = refs[:n_in], refs[n_in:]
        e_ins, refs = refs[:ne_in], refs[ne_in:]
        outs, refs = refs[:n_out], refs[n_out:]
        e_outs, refs = refs[:ne_out], refs[ne_out:]
        scratch, e_sems = refs[:len(refs) - ne_sem], refs[len(refs) - ne_sem:]

        @pl.when(first())
        def _():
            exch.start(e_ins, e_outs, e_sems)

        body(*ins, *outs, *scratch)

        @pl.when(last())
        def _():
            exch.finish(e_ins, e_outs, e_sems)

    return wrapped


def _hosted(exch, in_specs, out_specs, out_shape, scratch_shapes):
    any_spec = pl.BlockSpec(memory_space=pl.ANY)
    return dict(in_specs=list(in_specs) + [any_spec] * len(exch.arrays),
                out_specs=list(out_specs) + [any_spec] * len(exch.out_shapes),
                out_shape=list(out_shape) + list(exch.out_shapes),
                scratch_shapes=list(scratch_shapes) + list(exch.sems))


def allreduce_small(buf):
    r = buf.shape[0]
    vmem = pl.BlockSpec(memory_space=pltpu.VMEM)

    def body(buf_ref, out_ref, all_ref, send_sems, recv_sems):
        x, y, c = _place()
        me = _dev(x, y, c)
        copies = []
        for k in range(1, N_DEV):
            peer = (x ^ (k >> 2), y ^ ((k >> 1) & 1), c ^ (k & 1))
            copies.append(pltpu.make_async_remote_copy(
                src_ref=buf_ref, dst_ref=all_ref.at[me], send_sem=send_sems.at[k - 1], recv_sem=recv_sems.at[k - 1],
                device_id=peer, device_id_type=MESH))
        for cp in copies:
            cp.start()
        all_ref[me] = buf_ref[...]
        for cp in copies:
            cp.wait()
        total = all_ref[0]
        for d in range(1, N_DEV):
            total = total + all_ref[d]
        out_ref[...] = total

    return pl.pallas_call(
        body, name="allreduce_small", in_specs=[vmem], out_specs=vmem,
        out_shape=jax.ShapeDtypeStruct((r, LANES), F32),
        scratch_shapes=[pltpu.VMEM((N_DEV, r, LANES), F32), pltpu.SemaphoreType.DMA((N_DEV - 1,)),
                        pltpu.SemaphoreType.DMA((N_DEV - 1,))],
    )(buf)


def _cols_from_shards(g):
    return jnp.transpose(g, (1, 0, 2)).reshape(g.shape[1], N_DEV * g.shape[2])


def _rows128(t):
    return t.reshape(-1, LANES)


def _pad_lanes(t):
    t = t.reshape(1, -1)
    return jnp.pad(t, ((0, 0), (0, LANES - t.shape[1])))


def kernel(x, meta_tokens, lb_logits, mix_norm_w, w_in, hg_norm_w, gd_conv_w, gd_a_log, gd_dt_bias, gd_norm_w, w_branch_a, w_branch_b, w_out, ffn_norm_w, w_ffn_in, w_ffn_out, final_norm_w, loss_target, m_meta_tokens, m_lb_logits, m_mix_norm_w, m_w_in, m_hg_norm_w, m_gd_conv_w, m_gd_a_log, m_gd_dt_bias, m_gd_norm_w, m_w_branch_a, m_w_branch_b, m_w_out, m_ffn_norm_w, m_w_ffn_in, m_w_ffn_out, m_final_norm_w, v_meta_tokens, v_lb_logits, v_mix_norm_w, v_w_in, v_hg_norm_w, v_gd_conv_w, v_gd_a_log, v_gd_dt_bias, v_gd_norm_w, v_w_branch_a, v_w_branch_b, v_w_out, v_ffn_norm_w, v_w_ffn_in, v_w_ffn_out, v_final_norm_w):
    xs = x[0]
    tgt = loss_target[0]
    l = xs.shape[0]
    lp = PRE + l
    cx, cy, cc = _place()
    me = _dev(cx, cy, cc)
    my_chip = 2 * cx + cy

    big = [w_in[0], w_branch_a[0], w_branch_b[0], w_out[0], w_ffn_in[0], w_ffn_out[0]]
    big16 = [t.astype(BF16) for t in big]
    g_in, g_meta, g_conv = run_exchange("gather_first", gather_exchange([big16[0], meta_tokens, gd_conv_w[0]]))
    w_main, w_ab = shards_to_cols("w_in_cols", g_in, [W_MAIN, LANES], _w_in_pieces(), 256)
    meta_full = _cols_from_shards(g_meta)
    cw = _cols_from_shards(g_conv)
    pre = jnp.concatenate([jnp.zeros((PRE - N_META, D), F32), meta_full], axis=0)
    l0, l1 = lb_logits[0:1], lb_logits[1:2]
    alog, dtb = _pad_lanes(gd_a_log), _pad_lanes(gd_dt_bias)

    xn = norm1_fwd(pre, xs, mix_norm_w)
    proj, (g_ba, g_bb, g_out, g_ffi, g_ffo) = matmul("proj_main", xn, w_main, "nn", lp, W_MAIN, D, F32,
                                                      exch=gather_exchange(big16[1:]))
    wa, = shards_to_cols("w_branch_a_cols", g_ba, [D], _plain_pieces(D // N_DEV), 256)
    wb, = shards_to_cols("w_branch_b_cols", g_bb, [D], _plain_pieces(D // N_DEV), 256)
    wo = g_out.reshape(D, D)
    w_gu, = shards_to_cols("w_ffn_in_cols", g_ffi, [2 * D_FF], _ffn_in_pieces(), 256)
    w_fo = g_ffo.reshape(D_FF, D)
    proj_ab = matmul("proj_ab", xn, w_ab, "nn", lp, LANES, D, F32)
    o_a, st_a = hgrn2_fwd(proj, l0, l1, hg_norm_w)
    o_b, st_b = gdn_fwd(proj, proj_ab, cw, alog, dtb, gd_norm_w)
    ya = matmul("branch_a", o_a, wa, "nn", l, D, HW, F32, a_off=(PRE // 512, 0))
    yb = matmul("branch_b", o_b, wb, "nn", l, D, HW, F32, a_off=(PRE // 512, 0))
    merged = merge_fwd(proj, ya, yb)
    mo = matmul("mix_out", merged, wo, "nn", l, D, D, F32)
    h1, xn2 = norm2_fwd(xs, mo, ffn_norm_w)
    gu, act = ffn_in_fused(xn2, w_gu)
    f = matmul("ffn_out", act, w_fo, "nn", l, D, D_FF, F32, tn=512)
    loss_part, dh2, dh2_b, d_final_w = loss_head(h1, f, final_norm_w.reshape(1, D), tgt)

    tiles = {"w_in": (W_IN // N_DEV, 256), "w_branch_a": (256, D // N_DEV), "w_branch_b": (256, D // N_DEV),
             "w_out": (64, D), "w_ffn_in": (256, 2 * D_FF // N_DEV), "w_ffn_out": (176, D)}

    def chip_sums(tag, parts, host=None):
        blocks = [p.reshape((4, 2) + p.shape[1:]) for p in parts.values()]
        exch = core_exchange(blocks)
        hosted, from_core = (None, run_exchange("exchange_core_" + tag, exch)) if host is None else host(exch)
        return hosted, {
            nm: pair_add("pair_add_" + nm, lax.dynamic_index_in_dim(p, cc, axis=1, keepdims=False), r, tiles[nm])
            for (nm, p, r) in zip(parts, blocks, from_core)}

    dgu = d_act_fused(dh2_b, w_fo, gu)
    dw_fo = matmul("dw_ffn_out", act, dh2_b, "tn", D_FF, D, l, BF16, tm=512, tn=1024, tk=2048)
    dxn2 = matmul("d_xn2", dgu, w_gu, "nt", l, D, 2 * D_FF, F32, tn=512, tk=D_FF)
    dw_gu = matmul("dw_ffn_in", xn2, dgu, "tn", D, 2 * D_FF, l, BF16, tm=1024, tn=1024, tk=2048)
    dh1, dh1_b, d_ffn_norm_w = norm2_bwd(h1, ffn_norm_w, dxn2, dh2)
    dmerged, sums = chip_sums("ffn", {
        "w_ffn_in": cols_to_shards("dw_ffn_in_shards", [dw_gu], 2 * D_FF // N_DEV, _ffn_in_pieces(), 256),
        "w_ffn_out": dw_fo.reshape(N_DEV, D_FF // N_DEV, D)},
        host=lambda exch: matmul("d_merged", dh1_b, wo, "nt", l, D, D, F32, exch=exch))
    dw_o = matmul("dw_out", merged, dh1_b, "tn", D, D, l, BF16, tm=1024, tn=1024, tk=2048)
    dproj, dya, dyb = merge_bwd(proj, ya, yb, dmerged)
    do_a = matmul("d_o_a", dya, wa, "nt", lp, HW, D, F32)
    do_b = matmul("d_o_b", dyb, wb, "nt", lp, HW, D, F32)
    dw_a = matmul("dw_branch_a", o_a, dya, "tn", HW, D, lp, BF16, tm=1024, tn=1024, tk=lp // 4)
    dw_b = matmul("dw_branch_b", o_b, dyb, "tn", HW, D, lp, BF16, tm=1024, tn=1024, tk=lp // 4)
    sums.update(chip_sums("mix", {
        "w_branch_a": cols_to_shards("dw_branch_a_shards", [dw_a], D // N_DEV, _plain_pieces(D // N_DEV), 256),
        "w_branch_b": cols_to_shards("dw_branch_b_shards", [dw_b], D // N_DEV, _plain_pieces(D // N_DEV), 256),
        "w_out": dw_o.reshape(N_DEV, D // N_DEV, D)})[1])
    ffn_names, mix_names = ["w_ffn_in", "w_ffn_out"], ["w_branch_a", "w_branch_b", "w_out"]
    (dproj, dl0, dl1, d_hg_nw), from_chips_ffn = hgrn2_bwd(
        proj, l0, l1, hg_norm_w, st_a, do_a, dproj, chips_exchange([sums[nm] for nm in ffn_names]))
    (dproj, dproj_ab, d_conv, d_alog, d_dtb, d_gd_nw), from_chips_mix = gdn_bwd(
        proj, proj_ab, cw, alog, dtb, gd_norm_w, st_b, do_b, dproj, chips_exchange([sums[nm] for nm in mix_names]))
    dwt_main = matmul("dw_in_main", dproj, xn, "tn", W_MAIN, D, lp, BF16, tm=1024, tn=1024, tk=lp // 4)
    dwt_ab = matmul("dw_in_ab", dproj_ab, xn, "tn", LANES, D, lp, BF16, tn=1024, tk=lp // 4)
    dxn_ab, sums_in = chip_sums("w_in", {
        "w_in": rows_to_shards("dw_in_shards", [dwt_main, dwt_ab], W_IN // N_DEV, _w_in_pieces(), 256)},
        host=lambda exch: matmul("d_xn_ab", dproj_ab, w_ab, "nt", lp, D, LANES, F32, exch=exch))
    sums.update(sums_in)
    from_chips = dict(zip(ffn_names + mix_names, list(from_chips_ffn) + list(from_chips_mix)))
    dxn, (from_chips["w_in"],) = matmul("d_xn", dproj, w_main, "nt", lp, D, W_MAIN, F32, tn=512, tk=4096,
                                        exch=chips_exchange([sums["w_in"]]))
    grad_x, dpre, d_mix_norm_w = norm1_bwd(pre, xs, mix_norm_w, dxn, dxn_ab, dh1)
    names = ["w_in", "w_branch_a", "w_branch_b", "w_out", "w_ffn_in", "w_ffn_out"]
    moms = [(m_w_in, v_w_in), (m_w_branch_a, v_w_branch_a), (m_w_branch_b, v_w_branch_b), (m_w_out, v_w_out),
            (m_w_ffn_in, v_w_ffn_in), (m_w_ffn_out, v_w_ffn_out)]
    upd = {}
    for nm, w, (m, v) in zip(names, big, moms):
        own = lax.dynamic_index_in_dim(sums[nm], my_chip, axis=0, keepdims=False)
        flip = (lambda t: t.T) if nm == "w_in" else (lambda t: t)
        res = adamw_shard("adamw_" + nm, flip(w), flip(m[0]), flip(v[0]), own, from_chips[nm], tiles[nm])
        upd[nm] = tuple(flip(t_)[None] for t_ in res)

    d_lb = jnp.concatenate([dl0, dl1], axis=0)
    rep = [d_lb, d_mix_norm_w, d_hg_nw, d_alog, d_dtb, d_gd_nw, d_ffn_norm_w, d_final_w]
    buf = jnp.concatenate([_rows128(t) for t in rep] + [loss_part, _rows128(dpre[PRE - N_META:]), _rows128(d_conv)],
                          axis=0)
    n_rep = sum(_rows128(t).shape[0] for t in rep)
    n_rows = buf.shape[0]
    buf = jnp.pad(buf, ((0, -n_rows % 8), (0, 0)))
    tot = allreduce_small(buf)
    loss = tot[n_rep, 0]
    g_meta_full = tot[n_rep + 1:n_rep + 1 + N_META * D // LANES].reshape(N_META, D)
    g_conv_full = tot[n_rep + 1 + N_META * D // LANES:n_rows].reshape(CONV_K, 3 * HW)
    g_meta_mine = lax.dynamic_slice_in_dim(g_meta_full, me * (D // N_DEV), D // N_DEV, axis=1)
    g_conv_mine = lax.dynamic_slice_in_dim(g_conv_full, me * (3 * DH), 3 * DH, axis=1)

    small = [("lb_logits", lb_logits, m_lb_logits, v_lb_logits), ("mix_norm_w", mix_norm_w, m_mix_norm_w, v_mix_norm_w),
             ("hg_norm_w", hg_norm_w, m_hg_norm_w, v_hg_norm_w), ("gd_a_log", gd_a_log, m_gd_a_log, v_gd_a_log),
             ("gd_dt_bias", gd_dt_bias, m_gd_dt_bias, v_gd_dt_bias), ("gd_norm_w", gd_norm_w, m_gd_norm_w, v_gd_norm_w),
             ("ffn_norm_w", ffn_norm_w, m_ffn_norm_w, v_ffn_norm_w),
             ("final_norm_w", final_norm_w, m_final_norm_w, v_final_norm_w),
             ("meta_tokens", meta_tokens, m_meta_tokens, v_meta_tokens), ("gd_conv_w", gd_conv_w, m_gd_conv_w, v_gd_conv_w)]

    def pack(t):
        return _pad_lanes(t) if t.size < LANES else _rows128(t)

    sizes = [pack(w).shape[0] for _, w, _, _ in small]
    g_small = jnp.concatenate([tot[:n_rep], _rows128(g_meta_mine), _rows128(g_conv_mine)], axis=0)
    n_small = g_small.shape[0]
    assert n_small == sum(sizes)
    padr = lambda t: jnp.pad(t, ((0, -n_small % 8), (0, 0)))
    w_small = padr(jnp.concatenate([pack(w) for _, w, _, _ in small], axis=0))
    m_small = padr(jnp.concatenate([pack(m) for _, _, m, _ in small], axis=0))
    v_small = padr(jnp.concatenate([pack(v) for _, _, _, v in small], axis=0))
    g_small = padr(g_small)
    d_small, nm_small, nv_small = adamw_small(w_small, g_small, m_small, v_small)
    row = 0
    for (nm, w, _, _), sz in zip(small, sizes):
        def unpack(t):
            return t[row:row + sz].reshape(-1)[:w.size].reshape(w.shape)
        upd[nm] = (unpack(g_small), unpack(d_small), unpack(nm_small), unpack(nv_small))
        row += sz

    order = ["meta_tokens", "lb_logits", "mix_norm_w", "w_in", "hg_norm_w", "gd_conv_w", "gd_a_log", "gd_dt_bias",
             "gd_norm_w", "w_branch_a", "w_branch_b", "w_out", "ffn_norm_w", "w_ffn_in", "w_ffn_out", "final_norm_w"]
    outs = [loss, grad_x[None]]
    for j in range(4):
        outs += [upd[nm][j] for nm in order]
    return tuple(outs)
```

```python
import functools

import jax
import jax.numpy as jnp
from jax import lax
from jax.experimental import pallas as pl
from jax.experimental.pallas import tpu as pltpu

F32 = jnp.float32
BF16 = jnp.bfloat16
MESH = pl.DeviceIdType.MESH

EPS = 1e-6
D = 2048
N_META = 16
CHUNK = 64
SUB = 16
HEADS = 8
DH = 128
HW = HEADS * DH
CONV_K = 4
TAIL = 8
D_FF = 5632
PRE = 512
N_SKIP = PRE // CHUNK - 1
N_DEV = 8
LANES = 128
FF_BLK = 512
W_IN = 4 * HW + 4 * HW + 2 * HEADS + 2 * D
W_MAIN = 4 * HW + 4 * HW + 2 * D

ADAM_LR = 0.001
ADAM_B1 = 0.9
ADAM_B2 = 0.999
ADAM_EPS = 1e-08
ADAM_WD = 0.01
ADAM_STEP = 10

VMEM_LIMIT = 52 * 1024 * 1024

_NN = ((1,), (0,))
_NT = ((1,), (1,))
_TN = ((0,), (0,))


def _params(*sem):
    return pltpu.CompilerParams(dimension_semantics=sem, vmem_limit_bytes=VMEM_LIMIT)


BF16_ONCE, SPLIT, EXACT_LHS = 0, 1, 2


def _dot_bf16(a, b, dims):
    if a.ndim == 3:
        dn = (((dims[0][0] + 1,), (dims[1][0] + 1,)), ((0,), (0,)))
    else:
        dn = (dims, ((), ()))
    return lax.dot_general(a, b, dn, preferred_element_type=F32)


def _split(t):
    hi = t.astype(BF16)
    return hi, (t - hi.astype(F32)).astype(BF16)


def _raw_dot(a, b, dims, mode):
    if mode == BF16_ONCE:
        return _dot_bf16(a.astype(BF16), b.astype(BF16), dims)
    if mode == SPLIT:
        a_hi, a_lo = _split(a)
        b_hi, b_lo = _split(b)
        return _dot_bf16(a_hi, b_hi, dims) + (_dot_bf16(a_hi, b_lo, dims) + _dot_bf16(a_lo, b_hi, dims))
    a = a.astype(BF16)
    b_hi = b.astype(BF16)
    rest = b - b_hi.astype(F32)
    b_mid = rest.astype(BF16)
    b_lo = (rest - b_mid.astype(F32)).astype(BF16)
    return _dot_bf16(a, b_hi, dims) + (_dot_bf16(a, b_mid, dims) + _dot_bf16(a, b_lo, dims))


@functools.partial(jax.custom_vjp, nondiff_argnums=(2,))
def mm_nn(a, b, mode=BF16_ONCE):
    return _raw_dot(a, b, _NN, mode)


@functools.partial(jax.custom_vjp, nondiff_argnums=(2,))
def mm_nt(a, b, mode=BF16_ONCE):
    return _raw_dot(a, b, _NT, mode)


@functools.partial(jax.custom_vjp, nondiff_argnums=(2,))
def mm_tn(a, b, mode=BF16_ONCE):
    return _raw_dot(a, b, _TN, mode)


def _general(mode):
    return SPLIT if mode == EXACT_LHS else mode


mm_nn.defvjp(lambda a, b, p: (_raw_dot(a, b, _NN, p), (a, b)),
             lambda p, res, g: (mm_nt(g, res[1], _general(p)), mm_tn(res[0], g, p)))
mm_nt.defvjp(lambda a, b, p: (_raw_dot(a, b, _NT, p), (a, b)),
             lambda p, res, g: (mm_nn(g, res[1], p), mm_tn(g, res[0], p)))
mm_tn.defvjp(lambda a, b, p: (_raw_dot(a, b, _TN, p), (a, b)),
             lambda p, res, g: (mm_nt(res[1], g, _general(p)), mm_nn(res[0], g, _general(p))))


def _sigmoid(x):
    return 1.0 / (1.0 + jnp.exp(-x))


def _silu(x):
    return x * _sigmoid(x)


def _softplus(x):
    return jnp.maximum(x, 0.0) + jnp.log(1.0 + jnp.exp(-jnp.abs(x)))


def _l2n(t):
    return t * lax.rsqrt(jnp.sum(t * t, axis=-1, keepdims=True) + EPS)


def _rms_norm(x, w):
    return x * lax.rsqrt(jnp.mean(x * x, axis=-1, keepdims=True) + EPS) * w


def _gated_norm(o, gate, nw):
    o = o * lax.rsqrt(jnp.mean(o * o, axis=-1, keepdims=True) + EPS) * nw
    return o * _silu(gate)


def _heads(ref, piece):
    return jnp.stack([ref[:, piece * HW + DH * j:piece * HW + DH * (j + 1)] for j in range(HEADS)])


def _put_heads(ref, piece, value, accumulate=False):
    for j in range(HEADS):
        cols = slice(piece * HW + DH * j, piece * HW + DH * (j + 1))
        if accumulate:
            ref[:, cols] += value[j]
        else:
            ref[:, cols] = value[j].astype(ref.dtype)


def hg_chunk(hq, hf, hi, hg, l0, l1, nw, st):
    nb = hq.shape[0]
    m = jnp.maximum(l0, l1)
    e0 = jnp.exp(l0 - m)
    e1 = jnp.exp(l1 - m)
    lb = e0 / (e0 + e1)
    sig = _sigmoid(hf)
    q = _silu(hq)
    log_f = jnp.log(lb + (1.0 - lb) * sig)
    k = (1.0 - lb) * _sigmoid(-hf)
    v = hi
    rows = lax.broadcasted_iota(jnp.int32, (nb, CHUNK, CHUNK), 1)
    cols = lax.broadcasted_iota(jnp.int32, (nb, CHUNK, CHUNK), 2)
    b = mm_nn((rows >= cols).astype(F32), log_f, EXACT_LHS)
    o_inter = mm_nt(q * jnp.exp(b), st)
    tri_s = (lax.broadcasted_iota(jnp.int32, (nb, SUB, SUB, DH), 1)
             >= lax.broadcasted_iota(jnp.int32, (nb, SUB, SUB, DH), 2))
    outs = []
    for i in range(CHUNK // SUB):
        lo = i * SUB
        b_i, q_i, k_i, v_i = b[:, lo:lo + SUB], q[:, lo:lo + SUB], k[:, lo:lo + SUB], v[:, lo:lo + SUB]
        diff = b_i[:, :, None, :] - b_i[:, None, :, :]
        dec = jnp.where(tri_s, jnp.exp(jnp.minimum(diff, 0.0)), 0.0)
        s_ii = jnp.sum(q_i[:, :, None, :] * k_i[:, None, :, :] * dec, axis=-1)
        o_i = mm_nn(s_ii, v_i)
        if i > 0:
            b_ref = jnp.sum(log_f[:, :lo], axis=1, keepdims=True)
            q_t = q_i * jnp.exp(b_i - b_ref)
            k_t = k[:, :lo] * jnp.exp(b_ref - b[:, :lo])
            o_i = o_i + mm_nn(mm_nt(q_t, k_t), v[:, :lo])
        outs.append(o_i)
    o = o_inter + jnp.concatenate(outs, axis=1)
    b_last = jnp.sum(log_f, axis=1, keepdims=True)
    st_new = st * jnp.exp(b_last) + mm_tn(v, k * jnp.exp(b_last - b))
    return _gated_norm(o, hg, nw), st_new


def _unit_lower_inverse(a):
    n = a.shape[-1]
    eye = (lax.broadcasted_iota(jnp.int32, a.shape, 1) == lax.broadcasted_iota(jnp.int32, a.shape, 2)).astype(F32)
    x = eye - a
    p = mm_nn(a, a, SPLIT)
    x = x + mm_nn(x, p, SPLIT)
    power = 4
    while power < n:
        p = mm_nn(p, p, SPLIT)
        x = x + mm_nn(x, p, SPLIT)
        power *= 2
    return x


def gd_chunk(cq, ck, cv, z, ab, alog, dtb, nw, s):
    nb, c, _ = cq.shape
    lane = lax.broadcasted_iota(jnp.int32, (nb, 1, DH), 2)
    head = lax.broadcasted_iota(jnp.int32, (nb, 1, DH), 0)
    pick = lambda t, off: jnp.sum(jnp.where(lane == head + off, t[None], 0.0), axis=2, keepdims=True)
    a_raw, b_raw = pick(ab, 0), pick(ab, HEADS)
    beta = _sigmoid(b_raw)
    g = -jnp.exp(pick(alog, 0)) * _softplus(a_raw + pick(dtb, 0))
    q = _l2n(_silu(cq)) * (DH ** -0.5)
    k = _l2n(_silu(ck))
    v = _silu(cv)
    rows = lax.broadcasted_iota(jnp.int32, (nb, c, c), 1)
    cols = lax.broadcasted_iota(jnp.int32, (nb, c, c), 2)
    tril = (rows >= cols).astype(F32)
    gc_w = mm_nn(tril, jnp.broadcast_to(g, (nb, c, DH)), EXACT_LHS)
    g_sq = jnp.broadcast_to(g, (nb, c, c))
    gc_col = mm_nn(tril, g_sq, EXACT_LHS)
    gc_row = mm_nn(jnp.ones((nb, c, c), F32), g_sq * (rows <= cols).astype(F32), EXACT_LHS)
    rel = jnp.exp(jnp.minimum(gc_col - gc_row, 0.0))
    a = jnp.where(rows > cols, beta * mm_nt(k, k) * rel, 0.0)
    t_inv = _unit_lower_inverse(a)
    e_gc = jnp.exp(gc_w)
    w = mm_nn(t_inv, beta * e_gc * k, SPLIT)
    u = mm_nn(t_inv, beta * v, SPLIT)
    v_new = u - mm_nn(w, s)
    attn = jnp.where(rows >= cols, mm_nt(q, k) * rel, 0.0)
    o = mm_nn(q * e_gc, s) + mm_nn(attn, v_new)
    g_last = jnp.sum(g, axis=1, keepdims=True)
    s_new = jnp.exp(g_last) * s + mm_tn(k * jnp.exp(g_last - gc_w), v_new)
    return _gated_norm(o, z, nw), s_new


def _gd_conv(x_ref, tail_ref, cw_ref, xe_ref, first):
    xe_ref[0:TAIL, :] = jnp.where(first, 0.0, tail_ref[:, 0:3 * HW])
    xe_ref[TAIL:TAIL + CHUNK, :] = x_ref[:, 0:3 * HW]
    y = cw_ref[pl.ds(0, 1), :] * xe_ref[pl.ds(TAIL - CONV_K + 1, CHUNK), :]
    for k in range(1, CONV_K):
        y = y + cw_ref[pl.ds(k, 1), :] * xe_ref[pl.ds(TAIL - CONV_K + 1 + k, CHUNK), :]
    return y


def _conv_heads(y, piece):
    return jnp.stack([y[:, piece * HW + DH * j:piece * HW + DH * (j + 1)] for j in range(HEADS)])


def mixer_fwd(proj, proj_ab, l0, l1, hg_nw, cw, alog, dtb, gd_nw):
    lp = proj.shape[0]
    nc = lp // CHUNK

    def body(xh_ref, l0_ref, l1_ref, hnw_ref, xg_ref, tail_ref, ab_ref, cw_ref, alog_ref, dtb_ref, gnw_ref,
             oa_ref, sta_out_ref, ob_ref, stb_out_ref, sta_ref, stb_ref, xe_ref):
        c = pl.program_id(0)

        @pl.when(c == 0)
        def _():
            sta_ref[...] = jnp.zeros(sta_ref.shape, F32)
            stb_ref[...] = jnp.zeros(stb_ref.shape, F32)

        @pl.when(c < N_SKIP)
        def _():
            for ref in (oa_ref, sta_out_ref, ob_ref, stb_out_ref):
                ref[...] = jnp.zeros(ref.shape, ref.dtype)

        @pl.when(c >= N_SKIP)
        def _():
            stb = stb_ref[...]
            stb_out_ref[0] = stb
            y = _gd_conv(xg_ref, tail_ref, cw_ref, xe_ref, c == 0)
            ob, stb_new = gd_chunk(_conv_heads(y, 0), _conv_heads(y, 1), _conv_heads(y, 2), _heads(xg_ref, 3),
                                   ab_ref[...], alog_ref[...], dtb_ref[...], gnw_ref[...], stb)
            _put_heads(ob_ref, 0, ob)
            stb_ref[...] = stb_new
            sta = sta_ref[...]
            sta_out_ref[0] = sta
            oa, sta_new = hg_chunk(_heads(xh_ref, 0), _heads(xh_ref, 1), _heads(xh_ref, 2), _heads(xh_ref, 3),
                                   _heads(l0_ref, 0), _heads(l1_ref, 0), hnw_ref[...], sta)
            _put_heads(oa_ref, 0, oa)
            sta_ref[...] = sta_new

    vec = pl.BlockSpec((1, HW), lambda c: (0, 0))
    one = pl.BlockSpec((1, DH), lambda c: (0, 0))
    st_spec = pl.BlockSpec((1, HEADS, DH, DH), lambda c: (c, 0, 0, 0))
    o_spec = pl.BlockSpec((CHUNK, HW), lambda c: (c, 0))
    o_shape = jax.ShapeDtypeStruct((lp, HW), BF16)
    st_shape = jax.ShapeDtypeStruct((nc, HEADS, DH, DH), F32)
    return pl.pallas_call(
        body, name="mixer_fwd", grid=(nc,),
        in_specs=[pl.BlockSpec((CHUNK, 4 * HW), lambda c: (c, 0)), vec, vec, one,
                  pl.BlockSpec((CHUNK, 4 * HW), lambda c: (c, 1)),
                  pl.BlockSpec((TAIL, 4 * HW), lambda c: (jnp.maximum(c * (CHUNK // TAIL) - 1, 0), 1)),
                  pl.BlockSpec((CHUNK, DH), lambda c: (c, 0)),
                  pl.BlockSpec((CONV_K, 3 * HW), lambda c: (0, 0)), one, one, one],
        out_specs=[o_spec, st_spec, o_spec, st_spec],
        out_shape=[o_shape, st_shape, o_shape, st_shape],
        scratch_shapes=[pltpu.VMEM((HEADS, DH, DH), F32), pltpu.VMEM((HEADS, DH, DH), F32),
                        pltpu.VMEM((TAIL + CHUNK, 3 * HW), F32)],
        compiler_params=_params("arbitrary"),
    )(proj, l0, l1, hg_nw, proj, proj, proj_ab, cw, alog, dtb, gd_nw)


def mixer_bwd(proj, proj_ab, l0, l1, hg_nw, cw, alog, dtb, gd_nw, st_a, st_b, do_a, do_b, dproj, exch):
    lp = proj.shape[0]
    nc = lp // CHUNK

    def body(xh_ref, l0_ref, l1_ref, hnw_ref, xg_ref, tail_ref, ab_ref, cw_ref, alog_ref, dtb_ref, gnw_ref,
             sta_in_ref, stb_in_ref, doa_ref, dob_ref, _,
             dx_ref, dl0_ref, dl1_ref, dhnw_ref, dab_ref, dcw_ref, dalog_ref, ddtb_ref, dgnw_ref,
             dsta_ref, dstb_ref, xe_ref, dye_ref, head_ref):
        c = pl.program_id(0)
        cc = nc - 1 - c

        @pl.when(c == 0)
        def _():
            for ref in (dl0_ref, dl1_ref, dhnw_ref, dcw_ref, dalog_ref, ddtb_ref, dgnw_ref, dsta_ref, dstb_ref,
                        head_ref):
                ref[...] = jnp.zeros(ref.shape, F32)

        @pl.when(cc < N_SKIP)
        def _():
            dx_ref[...] = jnp.zeros(dx_ref.shape, dx_ref.dtype)
            dab_ref[...] = jnp.zeros(dab_ref.shape, F32)

        @pl.when(cc >= N_SKIP)
        def _():
            y = _gd_conv(xg_ref, tail_ref, cw_ref, xe_ref, cc == 0)
            _, gd_vjp = jax.vjp(gd_chunk, _conv_heads(y, 0), _conv_heads(y, 1), _conv_heads(y, 2), _heads(xg_ref, 3),
                                ab_ref[...], alog_ref[...], dtb_ref[...], gnw_ref[...], stb_in_ref[0])
            dcq, dck, dcv, dz, dab, dalog, ddtb, dgnw, dstb = gd_vjp((_heads(dob_ref, 0), dstb_ref[...]))
            _, hg_vjp = jax.vjp(hg_chunk, _heads(xh_ref, 0), _heads(xh_ref, 1), _heads(xh_ref, 2), _heads(xh_ref, 3),
                                _heads(l0_ref, 0), _heads(l1_ref, 0), hnw_ref[...], sta_in_ref[0])
            hg_grads = hg_vjp((_heads(doa_ref, 0), dsta_ref[...]))
            dstb_ref[...] = dstb
            dab_ref[...] = dab
            dalog_ref[...] += dalog
            ddtb_ref[...] += ddtb
            dgnw_ref[...] += dgnw
            _put_heads(dx_ref, 4 + 3, dz)
            for i, t in enumerate((dcq, dck, dcv)):
                for j in range(HEADS):
                    dye_ref[0:CHUNK, i * HW + DH * j:i * HW + DH * (j + 1)] = t[j]
            dye_ref[CHUNK:CHUNK + TAIL, :] = head_ref[...]
            head_ref[...] = dye_ref[0:TAIL, :]
            dy = dye_ref[0:CHUNK, :]
            dx = None
            for k in range(CONV_K):
                term = cw_ref[pl.ds(k, 1), :] * dye_ref[pl.ds(CONV_K - 1 - k, CHUNK), :]
                dx = term if dx is None else dx + term
                dcw_ref[pl.ds(k, 1), :] += jnp.sum(dy * xe_ref[pl.ds(TAIL - CONV_K + 1 + k, CHUNK), :],
                                                   axis=0, keepdims=True)
            dx_ref[:, 4 * HW:7 * HW] = dx.astype(dx_ref.dtype)
            for i in range(4):
                _put_heads(dx_ref, i, hg_grads[i])
            _put_heads(dl0_ref, 0, hg_grads[4], accumulate=True)
            _put_heads(dl1_ref, 0, hg_grads[5], accumulate=True)
            dhnw_ref[...] += hg_grads[6]
            dsta_ref[...] = hg_grads[7]

    rc = lambda c: nc - 1 - c
    vec = pl.BlockSpec((1, HW), lambda c: (0, 0))
    one = pl.BlockSpec((1, DH), lambda c: (0, 0))
    one_shape = jax.ShapeDtypeStruct((1, DH), F32)
    vec_shape = jax.ShapeDtypeStruct((1, HW), F32)
    st_spec = pl.BlockSpec((1, HEADS, DH, DH), lambda c: (rc(c), 0, 0, 0))
    do_spec = pl.BlockSpec((CHUNK, HW), lambda c: (rc(c), 0))
    n_in, n_out = 16, 9
    outs = pl.pallas_call(
        host_exchange(body, n_in, n_out, exch, lambda: pl.program_id(0) == 0, lambda: pl.program_id(0) == nc - 1),
        name="mixer_bwd", grid=(nc,), input_output_aliases={15: 0}, compiler_params=_params("arbitrary"),
        **_hosted(exch,
                  [pl.BlockSpec((CHUNK, 4 * HW), lambda c: (rc(c), 0)), vec, vec, one,
                   pl.BlockSpec((CHUNK, 4 * HW), lambda c: (rc(c), 1)),
                   pl.BlockSpec((TAIL, 4 * HW), lambda c: (jnp.maximum(rc(c) * (CHUNK // TAIL) - 1, 0), 1)),
                   pl.BlockSpec((CHUNK, DH), lambda c: (rc(c), 0)),
                   pl.BlockSpec((CONV_K, 3 * HW), lambda c: (0, 0)), one, one, one,
                   st_spec, st_spec, do_spec, do_spec, pl.BlockSpec(memory_space=pl.ANY)],
                  [pl.BlockSpec((CHUNK, 8 * HW), lambda c: (rc(c), 0)), vec, vec, one,
                   pl.BlockSpec((CHUNK, DH), lambda c: (rc(c), 0)),
                   pl.BlockSpec((CONV_K, 3 * HW), lambda c: (0, 0)), one, one, one],
                  [jax.ShapeDtypeStruct((lp, W_MAIN), BF16), vec_shape, vec_shape, one_shape,
                   jax.ShapeDtypeStruct((lp, DH), F32), jax.ShapeDtypeStruct((CONV_K, 3 * HW), F32),
                   one_shape, one_shape, one_shape],
                  [pltpu.VMEM((HEADS, DH, DH), F32), pltpu.VMEM((HEADS, DH, DH), F32),
                   pltpu.VMEM((TAIL + CHUNK, 3 * HW), F32), pltpu.VMEM((CHUNK + TAIL, 3 * HW), F32),
                   pltpu.VMEM((TAIL, 3 * HW), F32)]),
    )(proj, l0, l1, hg_nw, proj, proj, proj_ab, cw, alog, dtb, gd_nw, st_a, st_b, do_a, do_b, dproj, *exch.arrays)
    return outs[:n_out], outs[n_out:]


def matmul(name, a, b, mode, m, n, k, out_dtype, tm=512, tn=1024, tk=None, a_off=(0, 0), b_off=(0, 0), exch=None):
    tm, tn = min(tm, m), min(tn, n)
    tk = k if tk is None else min(tk, k)
    assert m % tm == 0 and n % tn == 0 and k % tk == 0, (name, m, n, k, tm, tn, tk)
    gi, gj, gk = m // tm, n // tn, k // tk
    if mode == "nn":
        a_spec = pl.BlockSpec((tm, tk), lambda j, i, kk: (i + a_off[0], kk + a_off[1]))
        b_spec = pl.BlockSpec((tk, tn), lambda j, i, kk: (kk + b_off[0], j + b_off[1]))
        dims = _NN
    elif mode == "nt":
        a_spec = pl.BlockSpec((tm, tk), lambda j, i, kk: (i + a_off[0], kk + a_off[1]))
        b_spec = pl.BlockSpec((tn, tk), lambda j, i, kk: (j + b_off[0], kk + b_off[1]))
        dims = _NT
    else:
        a_spec = pl.BlockSpec((tk, tm), lambda j, i, kk: (kk + a_off[0], i + a_off[1]))
        b_spec = pl.BlockSpec((tk, tn), lambda j, i, kk: (kk + b_off[0], j + b_off[1]))
        dims = _TN

    def body(a_ref, b_ref, o_ref, *acc):
        d = lax.dot_general(a_ref[...].astype(BF16), b_ref[...].astype(BF16), (dims, ((), ())),
                            preferred_element_type=F32)
        if gk == 1:
            o_ref[...] = d.astype(o_ref.dtype)
        else:
            acc_ref, = acc
            kk = pl.program_id(2)

            @pl.when(kk == 0)
            def _():
                acc_ref[...] = d

            @pl.when(kk > 0)
            def _():
                acc_ref[...] += d

            @pl.when(kk == gk - 1)
            def _():
                o_ref[...] = acc_ref[...].astype(o_ref.dtype)

    o_spec = pl.BlockSpec((tm, tn), lambda j, i, kk: (i, j))
    o_shape = jax.ShapeDtypeStruct((m, n), out_dtype)
    scratch = [] if gk == 1 else [pltpu.VMEM((tm, tn), F32)]
    if exch is None:
        return pl.pallas_call(
            body, name=name, grid=(gj, gi, gk), in_specs=[a_spec, b_spec], out_specs=o_spec, out_shape=o_shape,
            scratch_shapes=scratch, compiler_params=_params("parallel", "parallel", "arbitrary"),
        )(a, b)
    step = lambda: (pl.program_id(0), pl.program_id(1), pl.program_id(2))
    first = lambda: jnp.logical_and(jnp.logical_and(step()[0] == 0, step()[1] == 0), step()[2] == 0)
    last = lambda: jnp.logical_and(jnp.logical_and(step()[0] == gj - 1, step()[1] == gi - 1), step()[2] == gk - 1)
    outs = pl.pallas_call(
        host_exchange(body, 2, 1, exch, first, last), name=name, grid=(gj, gi, gk),
        compiler_params=_params("arbitrary", "arbitrary", "arbitrary"),
        **_hosted(exch, [a_spec, b_spec], [o_spec], [o_shape], scratch),
    )(a, b, *exch.arrays)
    return outs[0], outs[1:]


def _swiglu(gu):
    return _silu(gu[:, :FF_BLK]) * gu[:, FF_BLK:]


def ffn_in_fused(xn2, w_gu, tm=512):
    l = xn2.shape[0]
    tn = 2 * FF_BLK

    def body(a_ref, b_ref, gu_ref, act_ref):
        gu = lax.dot_general(a_ref[...], b_ref[...], (_NN, ((), ())), preferred_element_type=F32)
        gu_ref[...] = gu
        act_ref[...] = _swiglu(gu).astype(act_ref.dtype)

    return pl.pallas_call(
        body, name="ffn_in", grid=(2 * D_FF // tn, l // tm),
        in_specs=[pl.BlockSpec((tm, D), lambda j, i: (i, 0)), pl.BlockSpec((D, tn), lambda j, i: (0, j))],
        out_specs=[pl.BlockSpec((tm, tn), lambda j, i: (i, j)), pl.BlockSpec((tm, FF_BLK), lambda j, i: (i, j))],
        out_shape=[jax.ShapeDtypeStruct((l, 2 * D_FF), F32), jax.ShapeDtypeStruct((l, D_FF), BF16)],
        compiler_params=_params("parallel", "parallel"),
    )(xn2, w_gu)


def d_act_fused(dh2, w_fo, gu, tm=512):
    l = dh2.shape[0]

    def body(a_ref, b_ref, gu_ref, o_ref):
        da = lax.dot_general(a_ref[...], b_ref[...], (_NT, ((), ())), preferred_element_type=F32)
        _, vjp = jax.vjp(_swiglu, gu_ref[...])
        dgu, = vjp(da)
        o_ref[...] = dgu.astype(o_ref.dtype)

    return pl.pallas_call(
        body, name="d_act", grid=(D_FF // FF_BLK, l // tm),
        in_specs=[pl.BlockSpec((tm, D), lambda j, i: (i, 0)), pl.BlockSpec((FF_BLK, D), lambda j, i: (j, 0)),
                  pl.BlockSpec((tm, 2 * FF_BLK), lambda j, i: (i, j))],
        out_specs=pl.BlockSpec((tm, 2 * FF_BLK), lambda j, i: (i, j)),
        out_shape=jax.ShapeDtypeStruct((l, 2 * D_FF), BF16),
        compiler_params=_params("parallel", "parallel"),
    )(dh2, w_fo, gu)


TR = 256
N_PRE = PRE // TR


def _row(width, off=0, col=0):
    return pl.BlockSpec((TR, width), lambda i: (i + off, col))


def _pre_spec():
    return pl.BlockSpec((TR, D), lambda i: (jnp.minimum(i, N_PRE - 1), 0))


def _seq_spec(width=D, col=0):
    return pl.BlockSpec((TR, width), lambda i: (jnp.maximum(i - N_PRE, 0), col))


def _vec_spec(width=D):
    return pl.BlockSpec((1, width), lambda i: (0, 0))


def norm1_fwd(pre, x, w):
    lp = PRE + x.shape[0]

    def body(pre_ref, x_ref, w_ref, o_ref):
        h = jnp.where(pl.program_id(0) < N_PRE, pre_ref[...], x_ref[...])
        o_ref[...] = _rms_norm(h, w_ref[...]).astype(o_ref.dtype)

    return pl.pallas_call(
        body, name="norm1_fwd", grid=(lp // TR,),
        in_specs=[_pre_spec(), _seq_spec(), _vec_spec()],
        out_specs=_row(D), out_shape=jax.ShapeDtypeStruct((lp, D), BF16),
        compiler_params=_params("parallel"),
    )(pre, x, w)


def norm1_bwd(pre, x, w, dxn, dxn_ab, dh1):
    l = x.shape[0]
    lp = PRE + l

    def body(pre_ref, x_ref, w_ref, dxn_ref, dxn_ab_ref, dh1_ref, dx_ref, dpre_ref, dw_ref):
        i = pl.program_id(0)
        h = jnp.where(i < N_PRE, pre_ref[...], x_ref[...])
        _, vjp = jax.vjp(_rms_norm, h, w_ref[...])
        dh, dw = vjp(dxn_ref[...] + dxn_ab_ref[...])

        @pl.when(i == 0)
        def _():
            dw_ref[...] = dw

        @pl.when(i > 0)
        def _():
            dw_ref[...] += dw

        @pl.when(i < N_PRE)
        def _():
            dpre_ref[...] = dh

        @pl.when(i >= N_PRE)
        def _():
            dx_ref[...] = dh + dh1_ref[...]

    return pl.pallas_call(
        body, name="norm1_bwd", grid=(lp // TR,),
        in_specs=[_pre_spec(), _seq_spec(), _vec_spec(), _row(D), _row(D), _seq_spec()],
        out_specs=[_seq_spec(), _pre_spec(), _vec_spec()],
        out_shape=[jax.ShapeDtypeStruct((l, D), F32), jax.ShapeDtypeStruct((PRE, D), F32),
                   jax.ShapeDtypeStruct((1, D), F32)],
        compiler_params=_params("arbitrary"),
    )(pre, x, w, dxn, dxn_ab, dh1)


def _merge(gates, ya, yb):
    return _sigmoid(gates[:, :D]) * ya + _sigmoid(gates[:, D:]) * yb


def merge_fwd(proj, ya, yb):
    l = ya.shape[0]

    def body(g_ref, ya_ref, yb_ref, o_ref):
        o_ref[...] = _merge(g_ref[...], ya_ref[...], yb_ref[...]).astype(o_ref.dtype)

    return pl.pallas_call(
        body, name="merge_fwd", grid=(l // TR,),
        in_specs=[_row(2 * D, N_PRE, 2), _row(D), _row(D)],
        out_specs=_row(D), out_shape=jax.ShapeDtypeStruct((l, D), BF16),
        compiler_params=_params("parallel"),
    )(proj, ya, yb)


def merge_bwd(proj, ya, yb, dmerged):
    l = ya.shape[0]
    lp = PRE + l

    def body(g_ref, ya_ref, yb_ref, dm_ref, dg_ref, dya_ref, dyb_ref):
        i = pl.program_id(0)

        @pl.when(i < N_PRE)
        def _():
            dg_ref[...] = jnp.zeros(dg_ref.shape, dg_ref.dtype)
            dya_ref[...] = jnp.zeros(dya_ref.shape, dya_ref.dtype)
            dyb_ref[...] = jnp.zeros(dyb_ref.shape, dyb_ref.dtype)

        @pl.when(i >= N_PRE)
        def _():
            _, vjp = jax.vjp(_merge, g_ref[...], ya_ref[...], yb_ref[...])
            dg, dya, dyb = vjp(dm_ref[...])
            dg_ref[...] = dg.astype(dg_ref.dtype)
            dya_ref[...] = dya.astype(dya_ref.dtype)
            dyb_ref[...] = dyb.astype(dyb_ref.dtype)

    return pl.pallas_call(
        body, name="merge_bwd", grid=(lp // TR,),
        in_specs=[pl.BlockSpec((TR, 2 * D), lambda i: (jnp.maximum(i, N_PRE), 2)), _seq_spec(), _seq_spec(),
                  _seq_spec()],
        out_specs=[_row(2 * D, 0, 2), _row(D), _row(D)],
        out_shape=[jax.ShapeDtypeStruct((lp, W_MAIN), BF16), jax.ShapeDtypeStruct((lp, D), BF16),
                   jax.ShapeDtypeStruct((lp, D), BF16)],
        compiler_params=_params("parallel"),
    )(proj, ya, yb, dmerged)


def mix_out_norm2(merged, wo, x, w, tm=512):
    l = x.shape[0]

    def body(a_ref, b_ref, x_ref, w_ref, h_ref, o_ref):
        h = x_ref[...] + lax.dot_general(a_ref[...], b_ref[...], (_NN, ((), ())), preferred_element_type=F32)
        h_ref[...] = h
        o_ref[...] = _rms_norm(h, w_ref[...]).astype(o_ref.dtype)

    row = pl.BlockSpec((tm, D), lambda i: (i, 0))
    return pl.pallas_call(
        body, name="mix_out", grid=(l // tm,),
        in_specs=[row, pl.BlockSpec((D, D), lambda i: (0, 0)), row, _vec_spec()],
        out_specs=[row, row],
        out_shape=[jax.ShapeDtypeStruct((l, D), F32), jax.ShapeDtypeStruct((l, D), BF16)],
        compiler_params=_params("parallel"),
    )(merged, wo, x, w)


def norm2_bwd(h1, w, dxn2, dh2):
    l = h1.shape[0]

    def body(h_ref, w_ref, dxn_ref, dh2_ref, dh1_ref, dh1b_ref, dw_ref):
        i = pl.program_id(0)
        _, vjp = jax.vjp(_rms_norm, h_ref[...], w_ref[...])
        dh, dw = vjp(dxn_ref[...])
        dh1 = dh + dh2_ref[...]
        dh1_ref[...] = dh1
        dh1b_ref[...] = dh1.astype(BF16)

        @pl.when(i == 0)
        def _():
            dw_ref[...] = dw

        @pl.when(i > 0)
        def _():
            dw_ref[...] += dw

    return pl.pallas_call(
        body, name="norm2_bwd", grid=(l // TR,),
        in_specs=[_row(D), _vec_spec(), _row(D), _row(D)],
        out_specs=[_row(D), _row(D), _vec_spec()],
        out_shape=[jax.ShapeDtypeStruct((l, D), F32), jax.ShapeDtypeStruct((l, D), BF16),
                   jax.ShapeDtypeStruct((1, D), F32)],
        compiler_params=_params("arbitrary"),
    )(h1, w, dxn2, dh2)


def _loss_rows(h1, f, w, tgt):
    y = _rms_norm(h1 + f, w)
    err = (y - tgt) * (y - tgt)
    return 0.5 * jnp.sum(jnp.mean(err, axis=-1, keepdims=True), axis=0, keepdims=True)


def loss_head(h1, f, w, tgt):
    l = h1.shape[0]

    def body(h_ref, f_ref, w_ref, t_ref, loss_ref, dh_ref, dhb_ref, dw_ref):
        i = pl.program_id(0)
        loss, vjp = jax.vjp(_loss_rows, h_ref[...], f_ref[...], w_ref[...], t_ref[...])
        dh, _, dw, _ = vjp(jnp.ones((1, 1), F32))
        dh_ref[...] = dh
        dhb_ref[...] = dh.astype(BF16)
        loss = jnp.broadcast_to(loss, (1, LANES))

        @pl.when(i == 0)
        def _():
            dw_ref[...] = dw
            loss_ref[...] = loss

        @pl.when(i > 0)
        def _():
            dw_ref[...] += dw
            loss_ref[...] += loss

    return pl.pallas_call(
        body, name="loss_head", grid=(l // TR,),
        in_specs=[_row(D), _row(D), _vec_spec(), _row(D)],
        out_specs=[_vec_spec(LANES), _row(D), _row(D), _vec_spec()],
        out_shape=[jax.ShapeDtypeStruct((1, LANES), F32), jax.ShapeDtypeStruct((l, D), F32),
                   jax.ShapeDtypeStruct((l, D), BF16), jax.ShapeDtypeStruct((1, D), F32)],
        compiler_params=_params("arbitrary"),
    )(h1, f, w, tgt)


def _adamw(w, g, m, v):
    m = ADAM_B1 * m + (1.0 - ADAM_B1) * g
    v = ADAM_B2 * v + (1.0 - ADAM_B2) * (g * g)
    m_hat = m / (1.0 - ADAM_B1 ** ADAM_STEP)
    v_hat = v / (1.0 - ADAM_B2 ** ADAM_STEP)
    delta = -ADAM_LR * (m_hat / (jnp.sqrt(v_hat) + ADAM_EPS) + ADAM_WD * w)
    return delta, m, v


def adamw_shard(name, w, m, v, sums, recv, tile):
    r, c = w.shape
    tr, tc = tile
    assert r % tr == 0 and c % tc == 0

    def body(chip_ref, w_ref, m_ref, v_ref, own_ref, r0_ref, r1_ref, r2_ref, g_ref, d_ref, nm_ref, nv_ref):
        g = own_ref[...].astype(F32) + r0_ref[...].astype(F32)
        g = g + r1_ref[...].astype(F32)
        g = g + r2_ref[...].astype(F32)
        d, nm, nv = _adamw(w_ref[...], g, m_ref[...], v_ref[...])
        g_ref[...] = g
        d_ref[...] = d
        nm_ref[...] = nm
        nv_ref[...] = nv

    blk = pl.BlockSpec((tr, tc), lambda i, j, chip: (i, j))
    part = lambda s: pl.BlockSpec((None, tr, tc), lambda i, j, chip: (s, i, j))
    own = pl.BlockSpec((None, tr, tc), lambda i, j, chip: (chip[0], i, j))
    shape = jax.ShapeDtypeStruct((r, c), F32)
    my_chip = (2 * lax.axis_index("x") + lax.axis_index("y")).astype(jnp.int32).reshape(1)
    return pl.pallas_call(
        body, name=name, out_shape=[shape, shape, shape, shape],
        grid_spec=pltpu.PrefetchScalarGridSpec(num_scalar_prefetch=1, grid=(r // tr, c // tc),
                                               in_specs=[blk, blk, blk, own, part(0), part(1), part(2)],
                                               out_specs=[blk, blk, blk, blk]),
        compiler_params=_params("parallel", "parallel"),
    )(my_chip, w, m, v, sums, recv, recv, recv)


def adamw_small(w, g, m, v):
    def body(w_ref, g_ref, m_ref, v_ref, d_ref, nm_ref, nv_ref):
        d, nm, nv = _adamw(w_ref[...], g_ref[...], m_ref[...], v_ref[...])
        d_ref[...] = d
        nm_ref[...] = nm
        nv_ref[...] = nv

    shape = jax.ShapeDtypeStruct(w.shape, F32)
    return pl.pallas_call(body, name="adamw_small", out_shape=[shape, shape, shape])(w, g, m, v)


def pair_add(name, a, b, tile):
    q, r, c = b.shape
    tr, tc = tile
    assert r % tr == 0 and c % tc == 0

    def body(core_ref, a_ref, b_ref, o_ref):
        o_ref[...] = (a_ref[...].astype(F32) + b_ref[...].astype(F32)).astype(o_ref.dtype)

    blk = pl.BlockSpec((None, tr, tc), lambda s, i, j, core: (s, i, j))
    mine = pl.BlockSpec((None, None, tr, tc), lambda s, i, j, core: (s, core[0], i, j))
    return pl.pallas_call(
        body, name=name, out_shape=jax.ShapeDtypeStruct((q, r, c), BF16),
        grid_spec=pltpu.PrefetchScalarGridSpec(num_scalar_prefetch=1, grid=(q, r // tr, c // tc),
                                               in_specs=[mine, blk], out_specs=blk),
        compiler_params=_params("parallel", "parallel", "parallel"),
    )(lax.axis_index("c").astype(jnp.int32).reshape(1), a, b)


def _w_in_pieces():
    c = W_IN // N_DEV
    ranges = ((0, 8 * HW, 0, 0), (8 * HW, 8 * HW + 2 * HEADS, 1, 8 * HW), (8 * HW + 2 * HEADS, W_IN, 0, 2 * HEADS))
    out = []
    for d in range(N_DEV):
        for lo, hi, target, shift in ranges:
            s, e = max(lo, c * d), min(hi, c * (d + 1))
            if s < e:
                out.append((d, s - c * d, e - s, target, s - shift))
    return out


def _ffn_in_pieces():
    c = 2 * D_FF // N_DEV
    out = []
    for d in range(N_DEV):
        s = c * d
        while s < c * (d + 1):
            e = min((s // FF_BLK + 1) * FF_BLK, c * (d + 1))
            half, blk = divmod(s // FF_BLK, D_FF // FF_BLK)
            out.append((d, s - c * d, e - s, 0, (2 * blk + half) * FF_BLK + s % FF_BLK))
            s = e
    return out


def _plain_pieces(c):
    return [(d, 0, c, 0, c * d) for d in range(N_DEV)]


def shards_to_cols(name, g, widths, pieces, tr):
    _, r, c = g.shape
    covered = [sum(p[2] for p in pieces if p[3] == t) for t in range(len(widths))]

    def body(g_ref, *outs):
        for t, o in enumerate(outs):
            if covered[t] < widths[t]:
                o[...] = jnp.zeros(o.shape, o.dtype)
        for d, s, w, t, ts in pieces:
            outs[t][:, ts:ts + w] = g_ref[d, :, s:s + w]

    return pl.pallas_call(
        body, name=name, grid=(r // tr,),
        in_specs=[pl.BlockSpec((N_DEV, tr, c), lambda i: (0, i, 0))],
        out_specs=[pl.BlockSpec((tr, w), lambda i: (i, 0)) for w in widths],
        out_shape=[jax.ShapeDtypeStruct((r, w), g.dtype) for w in widths],
        compiler_params=_params("parallel"),
    )(g)


def cols_to_shards(name, srcs, c, pieces, tr):
    r = srcs[0].shape[0]

    def body(*refs):
        o = refs[-1]
        for d, s, w, t, ts in pieces:
            o[d, :, s:s + w] = refs[t][:, ts:ts + w]

    return pl.pallas_call(
        body, name=name, grid=(r // tr,),
        in_specs=[pl.BlockSpec((tr, t.shape[1]), lambda i: (i, 0)) for t in srcs],
        out_specs=pl.BlockSpec((N_DEV, tr, c), lambda i: (0, i, 0)),
        out_shape=jax.ShapeDtypeStruct((N_DEV, r, c), srcs[0].dtype),
        compiler_params=_params("parallel"),
    )(*srcs)


def rows_to_shards(name, srcs, c, pieces, tc):
    r = srcs[0].shape[1]

    def body(*refs):
        o = refs[-1]
        for d, s, w, t, ts in pieces:
            o[d, s:s + w, :] = refs[t][ts:ts + w, :]

    return pl.pallas_call(
        body, name=name, grid=(r // tc,),
        in_specs=[pl.BlockSpec((t.shape[0], tc), lambda i: (0, i)) for t in srcs],
        out_specs=pl.BlockSpec((N_DEV, c, tc), lambda i: (0, 0, i)),
        out_shape=jax.ShapeDtypeStruct((N_DEV, c, r), srcs[0].dtype),
        compiler_params=_params("parallel"),
    )(*srcs)


def _place():
    return lax.axis_index("x"), lax.axis_index("y"), lax.axis_index("c")


def _dev(px, py, pc):
    return 4 * px + 2 * py + pc


class Exchange:
    def __init__(self, arrays, out_shapes, sems, start, finish):
        self.arrays, self.out_shapes, self.sems, self.start, self.finish = arrays, out_shapes, sems, start, finish


def gather_exchange(arrays):
    n = len(arrays)

    def plan(ins, outs, sems):
        send_sems, recv_sems, local_sems = sems
        x, y, c = _place()
        me, sibling = (x, y, c), (x, y, 1 - c)
        chips = [(1 - x, y), (x, 1 - y), (1 - x, 1 - y)]

        def copy(a, k, block, to, src=None):
            dst = outs[a].at[_dev(*block)]
            return pltpu.make_async_remote_copy(
                src_ref=dst if src is None else src, dst_ref=dst, send_sem=send_sems.at[a, k],
                recv_sem=recv_sems.at[a, k], device_id=to, device_id_type=MESH)

        mine = [pltpu.make_async_copy(ins[a], outs[a].at[_dev(*me)], local_sems.at[a]) for a in range(n)]
        first = []
        for a in range(n):
            first.append(copy(a, 0, me, sibling, src=ins[a]))
            first += [copy(a, 1 + j, me, (*chip, c), src=ins[a]) for j, chip in enumerate(chips)]
        return me, sibling, chips, c, copy, mine, first

    def start(ins, outs, sems):
        *_, mine, first = plan(ins, outs, sems)
        for cp in mine + first:
            cp.start()

    def finish(ins, outs, sems):
        me, sibling, chips, c, copy, mine, first = plan(ins, outs, sems)
        passed = []
        for j, chip in enumerate(chips):
            for a in range(n):
                copy(a, 1 + j, (*chip, c), me).wait_recv()
                fwd = copy(a, 4 + j, (*chip, c), sibling)
                fwd.start()
                passed.append(fwd)
        for a in range(n):
            copy(a, 0, sibling, me).wait_recv()
        for j, chip in enumerate(chips):
            for a in range(n):
                copy(a, 4 + j, (*chip, 1 - c), me).wait_recv()
        for cp in first + passed:
            cp.wait_send()
        for cp in mine:
            cp.wait()

    return Exchange(arrays, [jax.ShapeDtypeStruct((N_DEV,) + t.shape, t.dtype) for t in arrays],
                    [pltpu.SemaphoreType.DMA((n, 7)), pltpu.SemaphoreType.DMA((n, 7)), pltpu.SemaphoreType.DMA((n,))],
                    start, finish)


def core_exchange(arrays):
    n = len(arrays)

    def copies(ins, outs, sems):
        send_sems, recv_sems = sems
        x, y, c = _place()
        return [pltpu.make_async_remote_copy(
            src_ref=ins[a].at[q, 1 - c], dst_ref=outs[a].at[q], send_sem=send_sems.at[a, q],
            recv_sem=recv_sems.at[a, q], device_id=(x, y, 1 - c), device_id_type=MESH)
            for a in range(n) for q in range(4)]

    def start(ins, outs, sems):
        for cp in copies(ins, outs, sems):
            cp.start()

    def finish(ins, outs, sems):
        for cp in copies(ins, outs, sems):
            cp.wait()

    return Exchange(arrays, [jax.ShapeDtypeStruct((4,) + t.shape[2:], t.dtype) for t in arrays],
                    [pltpu.SemaphoreType.DMA((n, 4)), pltpu.SemaphoreType.DMA((n, 4))], start, finish)


def chips_exchange(arrays):
    n = len(arrays)

    def copies(ins, outs, sems):
        send_sems, recv_sems = sems
        x, y, c = _place()
        chips = [(1 - x, y), (x, 1 - y), (1 - x, 1 - y)]
        return [pltpu.make_async_remote_copy(
            src_ref=ins[a].at[2 * qx + qy], dst_ref=outs[a].at[j], send_sem=send_sems.at[a, j],
            recv_sem=recv_sems.at[a, j], device_id=(qx, qy, c), device_id_type=MESH)
            for a in range(n) for j, (qx, qy) in enumerate(chips)]

    def start(ins, outs, sems):
        for cp in copies(ins, outs, sems):
            cp.start()

    def finish(ins, outs, sems):
        for cp in copies(ins, outs, sems):
            cp.wait()

    return Exchange(arrays, [jax.ShapeDtypeStruct((3,) + t.shape[1:], t.dtype) for t in arrays],
                    [pltpu.SemaphoreType.DMA((n, 3)), pltpu.SemaphoreType.DMA((n, 3))], start, finish)


def run_exchange(name, exch):
    n_in, n_out = len(exch.arrays), len(exch.out_shapes)
    any_spec = pl.BlockSpec(memory_space=pl.ANY)

    def body(*refs):
        ins, outs, sems = refs[:n_in], refs[n_in:n_in + n_out], refs[n_in + n_out:]
        exch.start(ins, outs, sems)
        exch.finish(ins, outs, sems)

    return pl.pallas_call(
        body, name=name, in_specs=[any_spec] * n_in, out_specs=[any_spec] * n_out, out_shape=exch.out_shapes,
        scratch_shapes=exch.sems,
    )(*exch.arrays)


def host_exchange(body, n_in, n_out, exch, first, last):
    ne_in, ne_out, ne_sem = len(exch.arrays), len(exch.out_shapes), len(exch.sems)

    def wrapped(*refs):
        ins, refs = refs[:n_in], refs[n_in:]
        e_ins, refs = refs[:ne_in], refs[ne_in:]
        outs, refs = refs[:n_out], refs[n_out:]
        e_outs, refs = refs[:ne_out], refs[ne_out:]
        scratch, e_sems = refs[:len(refs) - ne_sem], refs[len(refs) - ne_sem:]

        @pl.when(first())
        def _():
            exch.start(e_ins, e_outs, e_sems)

        body(*ins, *outs, *scratch)

        @pl.when(last())
        def _():
            exch.finish(e_ins, e_outs, e_sems)

    return wrapped


def _hosted(exch, in_specs, out_specs, out_shape, scratch_shapes):
    any_spec = pl.BlockSpec(memory_space=pl.ANY)
    return dict(in_specs=list(in_specs) + [any_spec] * len(exch.arrays),
                out_specs=list(out_specs) + [any_spec] * len(exch.out_shapes),
                out_shape=list(out_shape) + list(exch.out_shapes),
                scratch_shapes=list(scratch_shapes) + list(exch.sems))


def allreduce_small(buf):
    r = buf.shape[0]
    vmem = pl.BlockSpec(memory_space=pltpu.VMEM)

    def body(buf_ref, out_ref, all_ref, send_sems, recv_sems):
        x, y, c = _place()
        me = _dev(x, y, c)
        copies = []
        for k in range(1, N_DEV):
            peer = (x ^ (k >> 2), y ^ ((k >> 1) & 1), c ^ (k & 1))
            copies.append(pltpu.make_async_remote_copy(
                src_ref=buf_ref, dst_ref=all_ref.at[me], send_sem=send_sems.at[k - 1], recv_sem=recv_sems.at[k - 1],
                device_id=peer, device_id_type=MESH))
        for cp in copies:
            cp.start()
        all_ref[me] = buf_ref[...]
        for cp in copies:
            cp.wait()
        total = all_ref[0]
        for d in range(1, N_DEV):
            total = total + all_ref[d]
        out_ref[...] = total

    return pl.pallas_call(
        body, name="allreduce_small", in_specs=[vmem], out_specs=vmem,
        out_shape=jax.ShapeDtypeStruct((r, LANES), F32),
        scratch_shapes=[pltpu.VMEM((N_DEV, r, LANES), F32), pltpu.SemaphoreType.DMA((N_DEV - 1,)),
                        pltpu.SemaphoreType.DMA((N_DEV - 1,))],
    )(buf)


def _cols_from_shards(g):
    return jnp.transpose(g, (1, 0, 2)).reshape(g.shape[1], N_DEV * g.shape[2])


def _pack(t):
    flat = t.reshape(-1)
    return jnp.pad(flat, (0, -flat.size % (8 * LANES))).reshape(-1, LANES)


def _pad_lanes(t):
    t = t.reshape(1, -1)
    return jnp.pad(t, ((0, 0), (0, LANES - t.shape[1])))


def kernel(x, meta_tokens, lb_logits, mix_norm_w, w_in, hg_norm_w, gd_conv_w, gd_a_log, gd_dt_bias, gd_norm_w, w_branch_a, w_branch_b, w_out, ffn_norm_w, w_ffn_in, w_ffn_out, final_norm_w, loss_target, m_meta_tokens, m_lb_logits, m_mix_norm_w, m_w_in, m_hg_norm_w, m_gd_conv_w, m_gd_a_log, m_gd_dt_bias, m_gd_norm_w, m_w_branch_a, m_w_branch_b, m_w_out, m_ffn_norm_w, m_w_ffn_in, m_w_ffn_out, m_final_norm_w, v_meta_tokens, v_lb_logits, v_mix_norm_w, v_w_in, v_hg_norm_w, v_gd_conv_w, v_gd_a_log, v_gd_dt_bias, v_gd_norm_w, v_w_branch_a, v_w_branch_b, v_w_out, v_ffn_norm_w, v_w_ffn_in, v_w_ffn_out, v_final_norm_w):
    xs = x[0]
    tgt = loss_target[0]
    l = xs.shape[0]
    lp = PRE + l
    me = _dev(*_place())

    big = [w_in[0], w_branch_a[0], w_branch_b[0], w_out[0], w_ffn_in[0], w_ffn_out[0]]
    big16 = [t.astype(BF16) for t in big]
    g_in, g_meta, g_conv = run_exchange("gather_first", gather_exchange([big16[0], meta_tokens, gd_conv_w[0]]))
    w_main, w_ab = shards_to_cols("w_in_cols", g_in, [W_MAIN, LANES], _w_in_pieces(), 256)
    meta_full = _cols_from_shards(g_meta)
    cw = _cols_from_shards(g_conv)
    pre = jnp.concatenate([jnp.zeros((PRE - N_META, D), F32), meta_full], axis=0)
    l0, l1 = lb_logits[0:1], lb_logits[1:2]
    alog, dtb = _pad_lanes(gd_a_log), _pad_lanes(gd_dt_bias)

    xn = norm1_fwd(pre, xs, mix_norm_w)
    proj, (g_ba, g_bb, g_out, g_ffi, g_ffo) = matmul("proj_main", xn, w_main, "nn", lp, W_MAIN, D, F32,
                                                      exch=gather_exchange(big16[1:]))
    wa, = shards_to_cols("w_branch_a_cols", g_ba, [D], _plain_pieces(D // N_DEV), 256)
    wb, = shards_to_cols("w_branch_b_cols", g_bb, [D], _plain_pieces(D // N_DEV), 256)
    wo = g_out.reshape(D, D)
    w_gu, = shards_to_cols("w_ffn_in_cols", g_ffi, [2 * D_FF], _ffn_in_pieces(), 256)
    w_fo = g_ffo.reshape(D_FF, D)
    proj_ab = matmul("proj_ab", xn, w_ab, "nn", lp, LANES, D, F32)
    o_a, st_a, o_b, st_b = mixer_fwd(proj, proj_ab, l0, l1, hg_norm_w, cw, alog, dtb, gd_norm_w)
    ya = matmul("branch_a", o_a, wa, "nn", l, D, HW, F32, a_off=(PRE // 512, 0))
    yb = matmul("branch_b", o_b, wb, "nn", l, D, HW, F32, a_off=(PRE // 512, 0))
    merged = merge_fwd(proj, ya, yb)
    h1, xn2 = mix_out_norm2(merged, wo, xs, ffn_norm_w)
    gu, act = ffn_in_fused(xn2, w_gu)
    f = matmul("ffn_out", act, w_fo, "nn", l, D, D_FF, F32, tn=512)
    loss_part, dh2, dh2_b, d_final_w = loss_head(h1, f, final_norm_w.reshape(1, D), tgt)

    tiles = {"w_in": (W_IN // N_DEV, 256), "w_branch_a": (256, D // N_DEV), "w_branch_b": (256, D // N_DEV),
             "w_out": (64, D), "w_ffn_in": (256, 2 * D_FF // N_DEV), "w_ffn_out": (176, D)}

    def chip_sums(tag, parts, host=None):
        blocks = [p.reshape((4, 2) + p.shape[1:]) for p in parts.values()]
        exch = core_exchange(blocks)
        hosted, from_core = (None, run_exchange("exchange_core_" + tag, exch)) if host is None else host(exch)
        return hosted, {nm: pair_add("pair_add_" + nm, p, r, tiles[nm]) for (nm, p, r) in zip(parts, blocks, from_core)}

    dgu = d_act_fused(dh2_b, w_fo, gu)
    dw_fo = matmul("dw_ffn_out", act, dh2_b, "tn", D_FF, D, l, BF16, tm=512, tn=1024, tk=2048)
    dxn2 = matmul("d_xn2", dgu, w_gu, "nt", l, D, 2 * D_FF, F32, tn=512, tk=D_FF)
    dw_gu = matmul("dw_ffn_in", xn2, dgu, "tn", D, 2 * D_FF, l, BF16, tm=1024, tn=1024, tk=2048)
    dh1, dh1_b, d_ffn_norm_w = norm2_bwd(h1, ffn_norm_w, dxn2, dh2)
    dmerged, sums = chip_sums("ffn", {
        "w_ffn_in": cols_to_shards("dw_ffn_in_shards", [dw_gu], 2 * D_FF // N_DEV, _ffn_in_pieces(), 256),
        "w_ffn_out": dw_fo.reshape(N_DEV, D_FF // N_DEV, D)},
        host=lambda exch: matmul("d_merged", dh1_b, wo, "nt", l, D, D, F32, exch=exch))
    dw_o = matmul("dw_out", merged, dh1_b, "tn", D, D, l, BF16, tm=1024, tn=1024, tk=2048)
    dproj, dya, dyb = merge_bwd(proj, ya, yb, dmerged)
    do_a = matmul("d_o_a", dya, wa, "nt", lp, HW, D, F32)
    do_b = matmul("d_o_b", dyb, wb, "nt", lp, HW, D, F32)
    dw_a = matmul("dw_branch_a", o_a, dya, "tn", HW, D, lp, BF16, tm=1024, tn=1024, tk=lp // 4)
    dw_b = matmul("dw_branch_b", o_b, dyb, "tn", HW, D, lp, BF16, tm=1024, tn=1024, tk=lp // 4)
    sums.update(chip_sums("mix", {
        "w_branch_a": cols_to_shards("dw_branch_a_shards", [dw_a], D // N_DEV, _plain_pieces(D // N_DEV), 256),
        "w_branch_b": cols_to_shards("dw_branch_b_shards", [dw_b], D // N_DEV, _plain_pieces(D // N_DEV), 256),
        "w_out": dw_o.reshape(N_DEV, D // N_DEV, D)})[1])
    ffn_names, mix_names = ["w_ffn_in", "w_ffn_out"], ["w_branch_a", "w_branch_b", "w_out"]
    (dproj, dl0, dl1, d_hg_nw, dproj_ab, d_conv, d_alog, d_dtb, d_gd_nw), from_chips_early = mixer_bwd(
        proj, proj_ab, l0, l1, hg_norm_w, cw, alog, dtb, gd_norm_w, st_a, st_b, do_a, do_b, dproj,
        chips_exchange([sums[nm] for nm in ffn_names + mix_names]))
    dwt_main = matmul("dw_in_main", dproj, xn, "tn", W_MAIN, D, lp, BF16, tm=1024, tn=1024, tk=lp // 4)
    dwt_ab = matmul("dw_in_ab", dproj_ab, xn, "tn", LANES, D, lp, BF16, tn=1024, tk=lp // 4)
    dxn_ab, sums_in = chip_sums("w_in", {
        "w_in": rows_to_shards("dw_in_shards", [dwt_main, dwt_ab], W_IN // N_DEV, _w_in_pieces(), 256)},
        host=lambda exch: matmul("d_xn_ab", dproj_ab, w_ab, "nt", lp, D, LANES, F32, exch=exch))
    sums.update(sums_in)
    from_chips = dict(zip(ffn_names + mix_names, from_chips_early))
    dxn, (from_chips["w_in"],) = matmul("d_xn", dproj, w_main, "nt", lp, D, W_MAIN, F32, tn=512, tk=4096,
                                        exch=chips_exchange([sums["w_in"]]))
    grad_x, dpre, d_mix_norm_w = norm1_bwd(pre, xs, mix_norm_w, dxn, dxn_ab, dh1)
    names = ["w_in", "w_branch_a", "w_branch_b", "w_out", "w_ffn_in", "w_ffn_out"]
    moms = [(m_w_in, v_w_in), (m_w_branch_a, v_w_branch_a), (m_w_branch_b, v_w_branch_b), (m_w_out, v_w_out),
            (m_w_ffn_in, v_w_ffn_in), (m_w_ffn_out, v_w_ffn_out)]
    upd = {}
    for nm, w, (m, v) in zip(names, big, moms):
        flip = (lambda t: t.T) if nm == "w_in" else (lambda t: t)
        res = adamw_shard("adamw_" + nm, flip(w), flip(m[0]), flip(v[0]), sums[nm], from_chips[nm], tiles[nm])
        upd[nm] = tuple(flip(t_)[None] for t_ in res)

    d_lb = jnp.concatenate([dl0, dl1], axis=0)
    rep = [_pack(t) for t in (d_lb, d_mix_norm_w, d_hg_nw, d_alog, d_dtb, d_gd_nw, d_ffn_norm_w, d_final_w)]
    n_rep = sum(t.shape[0] for t in rep)
    n_meta = N_META * D // LANES
    tot = allreduce_small(jnp.concatenate(rep + [_pack(loss_part), _pack(dpre[PRE - N_META:]), _pack(d_conv)], axis=0))
    loss = tot[n_rep, 0]
    g_meta_full = tot[n_rep + 8:n_rep + 8 + n_meta].reshape(N_META, D)
    g_conv_full = tot[n_rep + 8 + n_meta:].reshape(CONV_K, 3 * HW)
    g_meta_mine = lax.dynamic_slice_in_dim(g_meta_full, me * (D // N_DEV), D // N_DEV, axis=1)
    g_conv_mine = lax.dynamic_slice_in_dim(g_conv_full, me * (3 * DH), 3 * DH, axis=1)

    small = [("lb_logits", lb_logits, m_lb_logits, v_lb_logits), ("mix_norm_w", mix_norm_w, m_mix_norm_w, v_mix_norm_w),
             ("hg_norm_w", hg_norm_w, m_hg_norm_w, v_hg_norm_w), ("gd_a_log", gd_a_log, m_gd_a_log, v_gd_a_log),
             ("gd_dt_bias", gd_dt_bias, m_gd_dt_bias, v_gd_dt_bias), ("gd_norm_w", gd_norm_w, m_gd_norm_w, v_gd_norm_w),
             ("ffn_norm_w", ffn_norm_w, m_ffn_norm_w, v_ffn_norm_w),
             ("final_norm_w", final_norm_w, m_final_norm_w, v_final_norm_w),
             ("meta_tokens", meta_tokens, m_meta_tokens, v_meta_tokens), ("gd_conv_w", gd_conv_w, m_gd_conv_w, v_gd_conv_w)]

    sizes = [_pack(w).shape[0] for _, w, _, _ in small]
    g_small = jnp.concatenate([tot[:n_rep], _pack(g_meta_mine), _pack(g_conv_mine)], axis=0)
    assert g_small.shape[0] == sum(sizes)
    w_small = jnp.concatenate([_pack(w) for _, w, _, _ in small], axis=0)
    m_small = jnp.concatenate([_pack(m) for _, _, m, _ in small], axis=0)
    v_small = jnp.concatenate([_pack(v) for _, _, _, v in small], axis=0)
    d_small, nm_small, nv_small = adamw_small(w_small, g_small, m_small, v_small)
    row = 0
    for (nm, w, _, _), sz in zip(small, sizes):
        def unpack(t):
            return t[row:row + sz].reshape(-1)[:w.size].reshape(w.shape)
        upd[nm] = (unpack(g_small), unpack(d_small), unpack(nm_small), unpack(nv_small))
        row += sz

    order = ["meta_tokens", "lb_logits", "mix_norm_w", "w_in", "hg_norm_w", "gd_conv_w", "gd_a_log", "gd_dt_bias",
             "gd_norm_w", "w_branch_a", "w_branch_b", "w_out", "ffn_norm_w", "w_ffn_in", "w_ffn_out", "final_norm_w"]
    outs = [loss, grad_x[None]]
    for j in range(4):
        outs += [upd[nm][j] for nm in order]
    return tuple(outs)
```

```python
import functools

import jax
import jax.numpy as jnp
from jax import lax
from jax.experimental import pallas as pl
from jax.experimental.pallas import tpu as pltpu

F32 = jnp.float32
BF16 = jnp.bfloat16
MESH = pl.DeviceIdType.MESH

EPS = 1e-6
D = 2048
N_META = 16
CHUNK = 64
SUB = 16
HEADS = 8
DH = 128
HW = HEADS * DH
CONV_K = 4
TAIL = 8
D_FF = 5632
PRE = 512
N_SKIP = PRE // CHUNK - 1
N_DEV = 8
LANES = 128
FF_BLK = 512
W_IN = 4 * HW + 4 * HW + 2 * HEADS + 2 * D
W_MAIN = 4 * HW + 4 * HW + 2 * D

ADAM_LR = 0.001
ADAM_B1 = 0.9
ADAM_B2 = 0.999
ADAM_EPS = 1e-08
ADAM_WD = 0.01
ADAM_STEP = 10

VMEM_LIMIT = 52 * 1024 * 1024

_NN = ((1,), (0,))
_NT = ((1,), (1,))
_TN = ((0,), (0,))


def _params(*sem):
    return pltpu.CompilerParams(dimension_semantics=sem, vmem_limit_bytes=VMEM_LIMIT)


BF16_ONCE, SPLIT, EXACT_LHS = 0, 1, 2


def _dot_bf16(a, b, dims):
    if a.ndim == 3:
        dn = (((dims[0][0] + 1,), (dims[1][0] + 1,)), ((0,), (0,)))
    else:
        dn = (dims, ((), ()))
    return lax.dot_general(a, b, dn, preferred_element_type=F32)


def _split(t):
    hi = t.astype(BF16)
    return hi, (t - hi.astype(F32)).astype(BF16)


def _raw_dot(a, b, dims, mode):
    if mode == BF16_ONCE:
        return _dot_bf16(a.astype(BF16), b.astype(BF16), dims)
    if mode == SPLIT:
        a_hi, a_lo = _split(a)
        b_hi, b_lo = _split(b)
        return _dot_bf16(a_hi, b_hi, dims) + (_dot_bf16(a_hi, b_lo, dims) + _dot_bf16(a_lo, b_hi, dims))
    a = a.astype(BF16)
    b_hi = b.astype(BF16)
    rest = b - b_hi.astype(F32)
    b_mid = rest.astype(BF16)
    b_lo = (rest - b_mid.astype(F32)).astype(BF16)
    return _dot_bf16(a, b_hi, dims) + (_dot_bf16(a, b_mid, dims) + _dot_bf16(a, b_lo, dims))


@functools.partial(jax.custom_vjp, nondiff_argnums=(2,))
def mm_nn(a, b, mode=BF16_ONCE):
    return _raw_dot(a, b, _NN, mode)


@functools.partial(jax.custom_vjp, nondiff_argnums=(2,))
def mm_nt(a, b, mode=BF16_ONCE):
    return _raw_dot(a, b, _NT, mode)


@functools.partial(jax.custom_vjp, nondiff_argnums=(2,))
def mm_tn(a, b, mode=BF16_ONCE):
    return _raw_dot(a, b, _TN, mode)


def _general(mode):
    return SPLIT if mode == EXACT_LHS else mode


mm_nn.defvjp(lambda a, b, p: (_raw_dot(a, b, _NN, p), (a, b)),
             lambda p, res, g: (mm_nt(g, res[1], _general(p)), mm_tn(res[0], g, p)))
mm_nt.defvjp(lambda a, b, p: (_raw_dot(a, b, _NT, p), (a, b)),
             lambda p, res, g: (mm_nn(g, res[1], p), mm_tn(g, res[0], p)))
mm_tn.defvjp(lambda a, b, p: (_raw_dot(a, b, _TN, p), (a, b)),
             lambda p, res, g: (mm_nt(res[1], g, _general(p)), mm_nn(res[0], g, _general(p))))


def _sigmoid(x):
    return 1.0 / (1.0 + jnp.exp(-x))


def _silu(x):
    return x * _sigmoid(x)


def _softplus(x):
    return jnp.maximum(x, 0.0) + jnp.log(1.0 + jnp.exp(-jnp.abs(x)))


def _l2n(t):
    return t * lax.rsqrt(jnp.sum(t * t, axis=-1, keepdims=True) + EPS)


def _rms_norm(x, w):
    return x * lax.rsqrt(jnp.mean(x * x, axis=-1, keepdims=True) + EPS) * w


def _gated_norm(o, gate, nw):
    o = o * lax.rsqrt(jnp.mean(o * o, axis=-1, keepdims=True) + EPS) * nw
    return o * _silu(gate)


def _heads(ref, piece):
    return jnp.stack([ref[:, piece * HW + DH * j:piece * HW + DH * (j + 1)] for j in range(HEADS)])


def _put_heads(ref, piece, value, accumulate=False):
    for j in range(HEADS):
        cols = slice(piece * HW + DH * j, piece * HW + DH * (j + 1))
        if accumulate:
            ref[:, cols] += value[j]
        else:
            ref[:, cols] = value[j].astype(ref.dtype)


def hg_chunk(hq, hf, hi, hg, l0, l1, nw, st):
    nb = hq.shape[0]
    m = jnp.maximum(l0, l1)
    e0 = jnp.exp(l0 - m)
    e1 = jnp.exp(l1 - m)
    lb = e0 / (e0 + e1)
    sig = _sigmoid(hf)
    q = _silu(hq)
    log_f = jnp.log(lb + (1.0 - lb) * sig)
    k = (1.0 - lb) * _sigmoid(-hf)
    v = hi
    rows = lax.broadcasted_iota(jnp.int32, (nb, CHUNK, CHUNK), 1)
    cols = lax.broadcasted_iota(jnp.int32, (nb, CHUNK, CHUNK), 2)
    b = mm_nn((rows >= cols).astype(F32), log_f, EXACT_LHS)
    o_inter = mm_nt(q * jnp.exp(b), st)
    tri_s = (lax.broadcasted_iota(jnp.int32, (nb, SUB, SUB, DH), 1)
             >= lax.broadcasted_iota(jnp.int32, (nb, SUB, SUB, DH), 2))
    outs = []
    for i in range(CHUNK // SUB):
        lo = i * SUB
        b_i, q_i, k_i, v_i = b[:, lo:lo + SUB], q[:, lo:lo + SUB], k[:, lo:lo + SUB], v[:, lo:lo + SUB]
        diff = b_i[:, :, None, :] - b_i[:, None, :, :]
        dec = jnp.where(tri_s, jnp.exp(jnp.minimum(diff, 0.0)), 0.0)
        s_ii = jnp.sum(q_i[:, :, None, :] * k_i[:, None, :, :] * dec, axis=-1)
        o_i = mm_nn(s_ii, v_i)
        if i > 0:
            b_ref = jnp.sum(log_f[:, :lo], axis=1, keepdims=True)
            q_t = q_i * jnp.exp(b_i - b_ref)
            k_t = k[:, :lo] * jnp.exp(b_ref - b[:, :lo])
            o_i = o_i + mm_nn(mm_nt(q_t, k_t), v[:, :lo])
        outs.append(o_i)
    o = o_inter + jnp.concatenate(outs, axis=1)
    b_last = jnp.sum(log_f, axis=1, keepdims=True)
    st_new = st * jnp.exp(b_last) + mm_tn(v, k * jnp.exp(b_last - b))
    return _gated_norm(o, hg, nw), st_new


@jax.custom_vjp
def _unit_lower_inverse(a):
    n = a.shape[-1]
    eye = (lax.broadcasted_iota(jnp.int32, a.shape, 1) == lax.broadcasted_iota(jnp.int32, a.shape, 2)).astype(F32)
    x = eye - a
    p = mm_nn(a, a, SPLIT)
    x = x + mm_nn(x, p, SPLIT)
    power = 4
    while power < n:
        p = mm_nn(p, p, SPLIT)
        x = x + mm_nn(x, p, SPLIT)
        power *= 2
    return x


def _unit_lower_inverse_fwd(a):
    t = _unit_lower_inverse(a)
    return t, t


def _unit_lower_inverse_bwd(t, dt):
    return (-mm_tn(t, mm_nt(dt, t, SPLIT), SPLIT),)


_unit_lower_inverse.defvjp(_unit_lower_inverse_fwd, _unit_lower_inverse_bwd)


@jax.custom_vjp
def _kept_inverse(a, t):
    return t


_kept_inverse.defvjp(lambda a, t: (t, t),
                     lambda t, dt: (-mm_tn(t, mm_nt(dt, t, SPLIT), SPLIT), jnp.zeros_like(t)))


def gd_chunk(cq, ck, cv, z, ab, alog, dtb, nw, s, kept_t=None, keep_t=False):
    nb, c, _ = cq.shape
    lane = lax.broadcasted_iota(jnp.int32, (nb, 1, DH), 2)
    head = lax.broadcasted_iota(jnp.int32, (nb, 1, DH), 0)
    pick = lambda t, off: jnp.sum(jnp.where(lane == head + off, t[None], 0.0), axis=2, keepdims=True)
    a_raw, b_raw = pick(ab, 0), pick(ab, HEADS)
    beta = _sigmoid(b_raw)
    g = -jnp.exp(pick(alog, 0)) * _softplus(a_raw + pick(dtb, 0))
    q = _l2n(_silu(cq)) * (DH ** -0.5)
    k = _l2n(_silu(ck))
    v = _silu(cv)
    rows = lax.broadcasted_iota(jnp.int32, (nb, c, c), 1)
    cols = lax.broadcasted_iota(jnp.int32, (nb, c, c), 2)
    tril = (rows >= cols).astype(F32)
    gc_w = mm_nn(tril, jnp.broadcast_to(g, (nb, c, DH)), EXACT_LHS)
    g_sq = jnp.broadcast_to(g, (nb, c, c))
    gc_col = mm_nn(tril, g_sq, EXACT_LHS)
    gc_row = mm_nn(jnp.ones((nb, c, c), F32), g_sq * (rows <= cols).astype(F32), EXACT_LHS)
    rel = jnp.exp(jnp.minimum(gc_col - gc_row, 0.0))
    a = jnp.where(rows > cols, beta * mm_nt(k, k) * rel, 0.0)
    t_inv = _unit_lower_inverse(a) if kept_t is None else _kept_inverse(a, kept_t)
    e_gc = jnp.exp(gc_w)
    w = mm_nn(t_inv, beta * e_gc * k, SPLIT)
    u = mm_nn(t_inv, beta * v, SPLIT)
    v_new = u - mm_nn(w, s)
    attn = jnp.where(rows >= cols, mm_nt(q, k) * rel, 0.0)
    o = mm_nn(q * e_gc, s) + mm_nn(attn, v_new)
    g_last = jnp.sum(g, axis=1, keepdims=True)
    s_new = jnp.exp(g_last) * s + mm_tn(k * jnp.exp(g_last - gc_w), v_new)
    out = _gated_norm(o, z, nw)
    return (out, s_new, t_inv) if keep_t else (out, s_new)


def _gd_conv(x_ref, tail_ref, cw_ref, xe_ref, first):
    xe_ref[0:TAIL, :] = jnp.where(first, 0.0, tail_ref[:, 0:3 * HW])
    xe_ref[TAIL:TAIL + CHUNK, :] = x_ref[:, 0:3 * HW]
    y = cw_ref[pl.ds(0, 1), :] * xe_ref[pl.ds(TAIL - CONV_K + 1, CHUNK), :]
    for k in range(1, CONV_K):
        y = y + cw_ref[pl.ds(k, 1), :] * xe_ref[pl.ds(TAIL - CONV_K + 1 + k, CHUNK), :]
    return y


def _conv_heads(y, piece):
    return jnp.stack([y[:, piece * HW + DH * j:piece * HW + DH * (j + 1)] for j in range(HEADS)])


def mixer_fwd(proj, proj_ab, l0, l1, hg_nw, cw, alog, dtb, gd_nw):
    lp = proj.shape[0]
    nc = lp // CHUNK

    def body(xh_ref, l0_ref, l1_ref, hnw_ref, xg_ref, tail_ref, ab_ref, cw_ref, alog_ref, dtb_ref, gnw_ref,
             oa_ref, sta_out_ref, ob_ref, stb_out_ref, tinv_ref, sta_ref, stb_ref, xe_ref):
        c = pl.program_id(0)

        @pl.when(c == 0)
        def _():
            sta_ref[...] = jnp.zeros(sta_ref.shape, F32)
            stb_ref[...] = jnp.zeros(stb_ref.shape, F32)

        @pl.when(c < N_SKIP)
        def _():
            for ref in (oa_ref, sta_out_ref, ob_ref, stb_out_ref, tinv_ref):
                ref[...] = jnp.zeros(ref.shape, ref.dtype)

        @pl.when(c >= N_SKIP)
        def _():
            stb = stb_ref[...]
            stb_out_ref[0] = stb
            y = _gd_conv(xg_ref, tail_ref, cw_ref, xe_ref, c == 0)
            ob, stb_new, tinv_ref[0] = gd_chunk(
                _conv_heads(y, 0), _conv_heads(y, 1), _conv_heads(y, 2), _heads(xg_ref, 3),
                ab_ref[...], alog_ref[...], dtb_ref[...], gnw_ref[...], stb, keep_t=True)
            _put_heads(ob_ref, 0, ob)
            stb_ref[...] = stb_new
            sta = sta_ref[...]
            sta_out_ref[0] = sta
            oa, sta_new = hg_chunk(_heads(xh_ref, 0), _heads(xh_ref, 1), _heads(xh_ref, 2), _heads(xh_ref, 3),
                                   _heads(l0_ref, 0), _heads(l1_ref, 0), hnw_ref[...], sta)
            _put_heads(oa_ref, 0, oa)
            sta_ref[...] = sta_new

    vec = pl.BlockSpec((1, HW), lambda c: (0, 0))
    one = pl.BlockSpec((1, DH), lambda c: (0, 0))
    st_spec = pl.BlockSpec((1, HEADS, DH, DH), lambda c: (c, 0, 0, 0))
    o_spec = pl.BlockSpec((CHUNK, HW), lambda c: (c, 0))
    o_shape = jax.ShapeDtypeStruct((lp, HW), BF16)
    st_shape = jax.ShapeDtypeStruct((nc, HEADS, DH, DH), F32)
    return pl.pallas_call(
        body, name="mixer_fwd", grid=(nc,),
        in_specs=[pl.BlockSpec((CHUNK, 4 * HW), lambda c: (c, 0)), vec, vec, one,
                  pl.BlockSpec((CHUNK, 4 * HW), lambda c: (c, 1)),
                  pl.BlockSpec((TAIL, 4 * HW), lambda c: (jnp.maximum(c * (CHUNK // TAIL) - 1, 0), 1)),
                  pl.BlockSpec((CHUNK, DH), lambda c: (c, 0)),
                  pl.BlockSpec((CONV_K, 3 * HW), lambda c: (0, 0)), one, one, one],
        out_specs=[o_spec, st_spec, o_spec, st_spec,
                   pl.BlockSpec((1, HEADS, CHUNK, CHUNK), lambda c: (c, 0, 0, 0))],
        out_shape=[o_shape, st_shape, o_shape, st_shape, jax.ShapeDtypeStruct((nc, HEADS, CHUNK, CHUNK), F32)],
        scratch_shapes=[pltpu.VMEM((HEADS, DH, DH), F32), pltpu.VMEM((HEADS, DH, DH), F32),
                        pltpu.VMEM((TAIL + CHUNK, 3 * HW), F32)],
        compiler_params=_params("arbitrary"),
    )(proj, l0, l1, hg_nw, proj, proj, proj_ab, cw, alog, dtb, gd_nw)


def mixer_bwd(proj, proj_ab, l0, l1, hg_nw, cw, alog, dtb, gd_nw, st_a, st_b, t_inv, do_a, do_b, dproj, exch):
    lp = proj.shape[0]
    nc = lp // CHUNK

    def body(xh_ref, l0_ref, l1_ref, hnw_ref, xg_ref, tail_ref, ab_ref, cw_ref, alog_ref, dtb_ref, gnw_ref,
             sta_in_ref, stb_in_ref, tinv_ref, doa_ref, dob_ref, _,
             dx_ref, dl0_ref, dl1_ref, dhnw_ref, dab_ref, dcw_ref, dalog_ref, ddtb_ref, dgnw_ref,
             dsta_ref, dstb_ref, xe_ref, dye_ref, head_ref):
        c = pl.program_id(0)
        cc = nc - 1 - c

        @pl.when(c == 0)
        def _():
            for ref in (dl0_ref, dl1_ref, dhnw_ref, dcw_ref, dalog_ref, ddtb_ref, dgnw_ref, dsta_ref, dstb_ref,
                        head_ref):
                ref[...] = jnp.zeros(ref.shape, F32)

        @pl.when(cc < N_SKIP)
        def _():
            dx_ref[...] = jnp.zeros(dx_ref.shape, dx_ref.dtype)
            dab_ref[...] = jnp.zeros(dab_ref.shape, F32)

        @pl.when(cc >= N_SKIP)
        def _():
            y = _gd_conv(xg_ref, tail_ref, cw_ref, xe_ref, cc == 0)
            _, gd_vjp = jax.vjp(functools.partial(gd_chunk, kept_t=tinv_ref[0]),
                                _conv_heads(y, 0), _conv_heads(y, 1), _conv_heads(y, 2), _heads(xg_ref, 3),
                                ab_ref[...], alog_ref[...], dtb_ref[...], gnw_ref[...], stb_in_ref[0])
            dcq, dck, dcv, dz, dab, dalog, ddtb, dgnw, dstb = gd_vjp((_heads(dob_ref, 0), dstb_ref[...]))
            _, hg_vjp = jax.vjp(hg_chunk, _heads(xh_ref, 0), _heads(xh_ref, 1), _heads(xh_ref, 2), _heads(xh_ref, 3),
                                _heads(l0_ref, 0), _heads(l1_ref, 0), hnw_ref[...], sta_in_ref[0])
            hg_grads = hg_vjp((_heads(doa_ref, 0), dsta_ref[...]))
            dstb_ref[...] = dstb
            dab_ref[...] = dab
            dalog_ref[...] += dalog
            ddtb_ref[...] += ddtb
            dgnw_ref[...] += dgnw
            _put_heads(dx_ref, 4 + 3, dz)
            for i, t in enumerate((dcq, dck, dcv)):
                for j in range(HEADS):
                    dye_ref[0:CHUNK, i * HW + DH * j:i * HW + DH * (j + 1)] = t[j]
            dye_ref[CHUNK:CHUNK + TAIL, :] = head_ref[...]
            head_ref[...] = dye_ref[0:TAIL, :]
            dy = dye_ref[0:CHUNK, :]
            dx = None
            for k in range(CONV_K):
                term = cw_ref[pl.ds(k, 1), :] * dye_ref[pl.ds(CONV_K - 1 - k, CHUNK), :]
                dx = term if dx is None else dx + term
                dcw_ref[pl.ds(k, 1), :] += jnp.sum(dy * xe_ref[pl.ds(TAIL - CONV_K + 1 + k, CHUNK), :],
                                                   axis=0, keepdims=True)
            dx_ref[:, 4 * HW:7 * HW] = dx.astype(dx_ref.dtype)
            for i in range(4):
                _put_heads(dx_ref, i, hg_grads[i])
            _put_heads(dl0_ref, 0, hg_grads[4], accumulate=True)
            _put_heads(dl1_ref, 0, hg_grads[5], accumulate=True)
            dhnw_ref[...] += hg_grads[6]
            dsta_ref[...] = hg_grads[7]

    rc = lambda c: nc - 1 - c
    vec = pl.BlockSpec((1, HW), lambda c: (0, 0))
    one = pl.BlockSpec((1, DH), lambda c: (0, 0))
    one_shape = jax.ShapeDtypeStruct((1, DH), F32)
    vec_shape = jax.ShapeDtypeStruct((1, HW), F32)
    st_spec = pl.BlockSpec((1, HEADS, DH, DH), lambda c: (rc(c), 0, 0, 0))
    do_spec = pl.BlockSpec((CHUNK, HW), lambda c: (rc(c), 0))
    n_in, n_out = 17, 9
    outs = pl.pallas_call(
        host_exchange(body, n_in, n_out, exch, lambda: pl.program_id(0) == 0, lambda: pl.program_id(0) == nc - 1),
        name="mixer_bwd", grid=(nc,), input_output_aliases={n_in - 1: 0}, compiler_params=_params("arbitrary"),
        **_hosted(exch,
                  [pl.BlockSpec((CHUNK, 4 * HW), lambda c: (rc(c), 0)), vec, vec, one,
                   pl.BlockSpec((CHUNK, 4 * HW), lambda c: (rc(c), 1)),
                   pl.BlockSpec((TAIL, 4 * HW), lambda c: (jnp.maximum(rc(c) * (CHUNK // TAIL) - 1, 0), 1)),
                   pl.BlockSpec((CHUNK, DH), lambda c: (rc(c), 0)),
                   pl.BlockSpec((CONV_K, 3 * HW), lambda c: (0, 0)), one, one, one,
                   st_spec, st_spec, pl.BlockSpec((1, HEADS, CHUNK, CHUNK), lambda c: (rc(c), 0, 0, 0)),
                   do_spec, do_spec, pl.BlockSpec(memory_space=pl.ANY)],
                  [pl.BlockSpec((CHUNK, 8 * HW), lambda c: (rc(c), 0)), vec, vec, one,
                   pl.BlockSpec((CHUNK, DH), lambda c: (rc(c), 0)),
                   pl.BlockSpec((CONV_K, 3 * HW), lambda c: (0, 0)), one, one, one],
                  [jax.ShapeDtypeStruct((lp, W_MAIN), BF16), vec_shape, vec_shape, one_shape,
                   jax.ShapeDtypeStruct((lp, DH), F32), jax.ShapeDtypeStruct((CONV_K, 3 * HW), F32),
                   one_shape, one_shape, one_shape],
                  [pltpu.VMEM((HEADS, DH, DH), F32), pltpu.VMEM((HEADS, DH, DH), F32),
                   pltpu.VMEM((TAIL + CHUNK, 3 * HW), F32), pltpu.VMEM((CHUNK + TAIL, 3 * HW), F32),
                   pltpu.VMEM((TAIL, 3 * HW), F32)]),
    )(proj, l0, l1, hg_nw, proj, proj, proj_ab, cw, alog, dtb, gd_nw, st_a, st_b, t_inv, do_a, do_b, dproj,
      *exch.arrays)
    return outs[:n_out], outs[n_out:]


def matmul(name, a, b, mode, m, n, k, out_dtype, tm=512, tn=1024, tk=None, a_off=(0, 0), b_off=(0, 0), exch=None):
    tm, tn = min(tm, m), min(tn, n)
    tk = k if tk is None else min(tk, k)
    assert m % tm == 0 and n % tn == 0 and k % tk == 0, (name, m, n, k, tm, tn, tk)
    gi, gj, gk = m // tm, n // tn, k // tk
    if mode == "nn":
        a_spec = pl.BlockSpec((tm, tk), lambda j, i, kk: (i + a_off[0], kk + a_off[1]))
        b_spec = pl.BlockSpec((tk, tn), lambda j, i, kk: (kk + b_off[0], j + b_off[1]))
        dims = _NN
    elif mode == "nt":
        a_spec = pl.BlockSpec((tm, tk), lambda j, i, kk: (i + a_off[0], kk + a_off[1]))
        b_spec = pl.BlockSpec((tn, tk), lambda j, i, kk: (j + b_off[0], kk + b_off[1]))
        dims = _NT
    else:
        a_spec = pl.BlockSpec((tk, tm), lambda j, i, kk: (kk + a_off[0], i + a_off[1]))
        b_spec = pl.BlockSpec((tk, tn), lambda j, i, kk: (kk + b_off[0], j + b_off[1]))
        dims = _TN

    def body(a_ref, b_ref, o_ref, *acc):
        d = lax.dot_general(a_ref[...].astype(BF16), b_ref[...].astype(BF16), (dims, ((), ())),
                            preferred_element_type=F32)
        if gk == 1:
            o_ref[...] = d.astype(o_ref.dtype)
        else:
            acc_ref, = acc
            kk = pl.program_id(2)

            @pl.when(kk == 0)
            def _():
                acc_ref[...] = d

            @pl.when(kk > 0)
            def _():
                acc_ref[...] += d

            @pl.when(kk == gk - 1)
            def _():
                o_ref[...] = acc_ref[...].astype(o_ref.dtype)

    o_spec = pl.BlockSpec((tm, tn), lambda j, i, kk: (i, j))
    o_shape = jax.ShapeDtypeStruct((m, n), out_dtype)
    scratch = [] if gk == 1 else [pltpu.VMEM((tm, tn), F32)]
    if exch is None:
        return pl.pallas_call(
            body, name=name, grid=(gj, gi, gk), in_specs=[a_spec, b_spec], out_specs=o_spec, out_shape=o_shape,
            scratch_shapes=scratch, compiler_params=_params("parallel", "parallel", "arbitrary"),
        )(a, b)
    step = lambda: (pl.program_id(0), pl.program_id(1), pl.program_id(2))
    first = lambda: jnp.logical_and(jnp.logical_and(step()[0] == 0, step()[1] == 0), step()[2] == 0)
    last = lambda: jnp.logical_and(jnp.logical_and(step()[0] == gj - 1, step()[1] == gi - 1), step()[2] == gk - 1)
    outs = pl.pallas_call(
        host_exchange(body, 2, 1, exch, first, last), name=name, grid=(gj, gi, gk),
        compiler_params=_params("arbitrary", "arbitrary", "arbitrary"),
        **_hosted(exch, [a_spec, b_spec], [o_spec], [o_shape], scratch),
    )(a, b, *exch.arrays)
    return outs[0], outs[1:]


def _swiglu(gu):
    return _silu(gu[:, :FF_BLK]) * gu[:, FF_BLK:]


def ffn_in_fused(xn2, w_gu, tm=512):
    l = xn2.shape[0]
    tn = 2 * FF_BLK

    def body(a_ref, b_ref, gu_ref, act_ref):
        gu = lax.dot_general(a_ref[...], b_ref[...], (_NN, ((), ())), preferred_element_type=F32)
        gu_ref[...] = gu
        act_ref[...] = _swiglu(gu).astype(act_ref.dtype)

    return pl.pallas_call(
        body, name="ffn_in", grid=(2 * D_FF // tn, l // tm),
        in_specs=[pl.BlockSpec((tm, D), lambda j, i: (i, 0)), pl.BlockSpec((D, tn), lambda j, i: (0, j))],
        out_specs=[pl.BlockSpec((tm, tn), lambda j, i: (i, j)), pl.BlockSpec((tm, FF_BLK), lambda j, i: (i, j))],
        out_shape=[jax.ShapeDtypeStruct((l, 2 * D_FF), F32), jax.ShapeDtypeStruct((l, D_FF), BF16)],
        compiler_params=_params("parallel", "parallel"),
    )(xn2, w_gu)


def d_act_fused(dh2, w_fo, gu, tm=512):
    l = dh2.shape[0]

    def body(a_ref, b_ref, gu_ref, o_ref):
        da = lax.dot_general(a_ref[...], b_ref[...], (_NT, ((), ())), preferred_element_type=F32)
        _, vjp = jax.vjp(_swiglu, gu_ref[...])
        dgu, = vjp(da)
        o_ref[...] = dgu.astype(o_ref.dtype)

    return pl.pallas_call(
        body, name="d_act", grid=(D_FF // FF_BLK, l // tm),
        in_specs=[pl.BlockSpec((tm, D), lambda j, i: (i, 0)), pl.BlockSpec((FF_BLK, D), lambda j, i: (j, 0)),
                  pl.BlockSpec((tm, 2 * FF_BLK), lambda j, i: (i, j))],
        out_specs=pl.BlockSpec((tm, 2 * FF_BLK), lambda j, i: (i, j)),
        out_shape=jax.ShapeDtypeStruct((l, 2 * D_FF), BF16),
        compiler_params=_params("parallel", "parallel"),
    )(dh2, w_fo, gu)


TR = 256
N_PRE = PRE // TR


def _row(width, off=0, col=0):
    return pl.BlockSpec((TR, width), lambda i: (i + off, col))


def _pre_spec():
    return pl.BlockSpec((TR, D), lambda i: (jnp.minimum(i, N_PRE - 1), 0))


def _seq_spec(width=D, col=0):
    return pl.BlockSpec((TR, width), lambda i: (jnp.maximum(i - N_PRE, 0), col))


def _vec_spec(width=D):
    return pl.BlockSpec((1, width), lambda i: (0, 0))


def norm1_fwd(pre, x, w):
    lp = PRE + x.shape[0]

    def body(pre_ref, x_ref, w_ref, o_ref):
        h = jnp.where(pl.program_id(0) < N_PRE, pre_ref[...], x_ref[...])
        o_ref[...] = _rms_norm(h, w_ref[...]).astype(o_ref.dtype)

    return pl.pallas_call(
        body, name="norm1_fwd", grid=(lp // TR,),
        in_specs=[_pre_spec(), _seq_spec(), _vec_spec()],
        out_specs=_row(D), out_shape=jax.ShapeDtypeStruct((lp, D), BF16),
        compiler_params=_params("parallel"),
    )(pre, x, w)


def norm1_bwd(pre, x, w, dxn, dxn_ab, dh1):
    l = x.shape[0]
    lp = PRE + l

    def body(pre_ref, x_ref, w_ref, dxn_ref, dxn_ab_ref, dh1_ref, dx_ref, dpre_ref, dw_ref):
        i = pl.program_id(0)
        h = jnp.where(i < N_PRE, pre_ref[...], x_ref[...])
        _, vjp = jax.vjp(_rms_norm, h, w_ref[...])
        dh, dw = vjp(dxn_ref[...] + dxn_ab_ref[...])

        @pl.when(i == 0)
        def _():
            dw_ref[...] = dw

        @pl.when(i > 0)
        def _():
            dw_ref[...] += dw

        @pl.when(i < N_PRE)
        def _():
            dpre_ref[...] = dh

        @pl.when(i >= N_PRE)
        def _():
            dx_ref[...] = dh + dh1_ref[...]

    return pl.pallas_call(
        body, name="norm1_bwd", grid=(lp // TR,),
        in_specs=[_pre_spec(), _seq_spec(), _vec_spec(), _row(D), _row(D), _seq_spec()],
        out_specs=[_seq_spec(), _pre_spec(), _vec_spec()],
        out_shape=[jax.ShapeDtypeStruct((l, D), F32), jax.ShapeDtypeStruct((PRE, D), F32),
                   jax.ShapeDtypeStruct((1, D), F32)],
        compiler_params=_params("arbitrary"),
    )(pre, x, w, dxn, dxn_ab, dh1)


def _merge(gates, ya, yb):
    return _sigmoid(gates[:, :D]) * ya + _sigmoid(gates[:, D:]) * yb


def merge_fwd(proj, ya, yb):
    l = ya.shape[0]

    def body(g_ref, ya_ref, yb_ref, o_ref):
        o_ref[...] = _merge(g_ref[...], ya_ref[...], yb_ref[...]).astype(o_ref.dtype)

    return pl.pallas_call(
        body, name="merge_fwd", grid=(l // TR,),
        in_specs=[_row(2 * D, N_PRE, 2), _row(D), _row(D)],
        out_specs=_row(D), out_shape=jax.ShapeDtypeStruct((l, D), BF16),
        compiler_params=_params("parallel"),
    )(proj, ya, yb)


def merge_bwd(proj, ya, yb, dmerged):
    l = ya.shape[0]
    lp = PRE + l

    def body(g_ref, ya_ref, yb_ref, dm_ref, dg_ref, dya_ref, dyb_ref):
        i = pl.program_id(0)

        @pl.when(i < N_PRE)
        def _():
            dg_ref[...] = jnp.zeros(dg_ref.shape, dg_ref.dtype)
            dya_ref[...] = jnp.zeros(dya_ref.shape, dya_ref.dtype)
            dyb_ref[...] = jnp.zeros(dyb_ref.shape, dyb_ref.dtype)

        @pl.when(i >= N_PRE)
        def _():
            _, vjp = jax.vjp(_merge, g_ref[...], ya_ref[...], yb_ref[...])
            dg, dya, dyb = vjp(dm_ref[...])
            dg_ref[...] = dg.astype(dg_ref.dtype)
            dya_ref[...] = dya.astype(dya_ref.dtype)
            dyb_ref[...] = dyb.astype(dyb_ref.dtype)

    return pl.pallas_call(
        body, name="merge_bwd", grid=(lp // TR,),
        in_specs=[pl.BlockSpec((TR, 2 * D), lambda i: (jnp.maximum(i, N_PRE), 2)), _seq_spec(), _seq_spec(),
                  _seq_spec()],
        out_specs=[_row(2 * D, 0, 2), _row(D), _row(D)],
        out_shape=[jax.ShapeDtypeStruct((lp, W_MAIN), BF16), jax.ShapeDtypeStruct((lp, D), BF16),
                   jax.ShapeDtypeStruct((lp, D), BF16)],
        compiler_params=_params("parallel"),
    )(proj, ya, yb, dmerged)


def mix_out_norm2(merged, wo, x, w, tm=512):
    l = x.shape[0]

    def body(a_ref, b_ref, x_ref, w_ref, h_ref, o_ref):
        h = x_ref[...] + lax.dot_general(a_ref[...], b_ref[...], (_NN, ((), ())), preferred_element_type=F32)
        h_ref[...] = h
        o_ref[...] = _rms_norm(h, w_ref[...]).astype(o_ref.dtype)

    row = pl.BlockSpec((tm, D), lambda i: (i, 0))
    return pl.pallas_call(
        body, name="mix_out", grid=(l // tm,),
        in_specs=[row, pl.BlockSpec((D, D), lambda i: (0, 0)), row, _vec_spec()],
        out_specs=[row, row],
        out_shape=[jax.ShapeDtypeStruct((l, D), F32), jax.ShapeDtypeStruct((l, D), BF16)],
        compiler_params=_params("parallel"),
    )(merged, wo, x, w)


def norm2_bwd(h1, w, dxn2, dh2):
    l = h1.shape[0]

    def body(h_ref, w_ref, dxn_ref, dh2_ref, dh1_ref, dh1b_ref, dw_ref):
        i = pl.program_id(0)
        _, vjp = jax.vjp(_rms_norm, h_ref[...], w_ref[...])
        dh, dw = vjp(dxn_ref[...])
        dh1 = dh + dh2_ref[...]
        dh1_ref[...] = dh1
        dh1b_ref[...] = dh1.astype(BF16)

        @pl.when(i == 0)
        def _():
            dw_ref[...] = dw

        @pl.when(i > 0)
        def _():
            dw_ref[...] += dw

    return pl.pallas_call(
        body, name="norm2_bwd", grid=(l // TR,),
        in_specs=[_row(D), _vec_spec(), _row(D), _row(D)],
        out_specs=[_row(D), _row(D), _vec_spec()],
        out_shape=[jax.ShapeDtypeStruct((l, D), F32), jax.ShapeDtypeStruct((l, D), BF16),
                   jax.ShapeDtypeStruct((1, D), F32)],
        compiler_params=_params("arbitrary"),
    )(h1, w, dxn2, dh2)


def _loss_rows(h1, f, w, tgt):
    y = _rms_norm(h1 + f, w)
    err = (y - tgt) * (y - tgt)
    return 0.5 * jnp.sum(jnp.mean(err, axis=-1, keepdims=True), axis=0, keepdims=True)


def loss_head(h1, f, w, tgt):
    l = h1.shape[0]

    def body(h_ref, f_ref, w_ref, t_ref, loss_ref, dh_ref, dhb_ref, dw_ref):
        i = pl.program_id(0)
        loss, vjp = jax.vjp(_loss_rows, h_ref[...], f_ref[...], w_ref[...], t_ref[...])
        dh, _, dw, _ = vjp(jnp.ones((1, 1), F32))
        dh_ref[...] = dh
        dhb_ref[...] = dh.astype(BF16)
        loss = jnp.broadcast_to(loss, (1, LANES))

        @pl.when(i == 0)
        def _():
            dw_ref[...] = dw
            loss_ref[...] = loss

        @pl.when(i > 0)
        def _():
            dw_ref[...] += dw
            loss_ref[...] += loss

    return pl.pallas_call(
        body, name="loss_head", grid=(l // TR,),
        in_specs=[_row(D), _row(D), _vec_spec(), _row(D)],
        out_specs=[_vec_spec(LANES), _row(D), _row(D), _vec_spec()],
        out_shape=[jax.ShapeDtypeStruct((1, LANES), F32), jax.ShapeDtypeStruct((l, D), F32),
                   jax.ShapeDtypeStruct((l, D), BF16), jax.ShapeDtypeStruct((1, D), F32)],
        compiler_params=_params("arbitrary"),
    )(h1, f, w, tgt)


def _adamw(w, g, m, v):
    m = ADAM_B1 * m + (1.0 - ADAM_B1) * g
    v = ADAM_B2 * v + (1.0 - ADAM_B2) * (g * g)
    m_hat = m / (1.0 - ADAM_B1 ** ADAM_STEP)
    v_hat = v / (1.0 - ADAM_B2 ** ADAM_STEP)
    delta = -ADAM_LR * (m_hat / (jnp.sqrt(v_hat) + ADAM_EPS) + ADAM_WD * w)
    return delta, m, v


def adamw_shard(name, w, m, v, sums, recv, tile):
    r, c = w.shape
    tr, tc = tile
    assert r % tr == 0 and c % tc == 0

    def body(chip_ref, w_ref, m_ref, v_ref, own_ref, r0_ref, r1_ref, r2_ref, g_ref, d_ref, nm_ref, nv_ref):
        g = own_ref[...].astype(F32) + r0_ref[...].astype(F32)
        g = g + r1_ref[...].astype(F32)
        g = g + r2_ref[...].astype(F32)
        d, nm, nv = _adamw(w_ref[...], g, m_ref[...], v_ref[...])
        g_ref[...] = g
        d_ref[...] = d
        nm_ref[...] = nm
        nv_ref[...] = nv

    blk = pl.BlockSpec((tr, tc), lambda i, j, chip: (i, j))
    part = lambda s: pl.BlockSpec((None, tr, tc), lambda i, j, chip: (s, i, j))
    own = pl.BlockSpec((None, tr, tc), lambda i, j, chip: (chip[0], i, j))
    shape = jax.ShapeDtypeStruct((r, c), F32)
    my_chip = (2 * lax.axis_index("x") + lax.axis_index("y")).astype(jnp.int32).reshape(1)
    return pl.pallas_call(
        body, name=name, out_shape=[shape, shape, shape, shape],
        grid_spec=pltpu.PrefetchScalarGridSpec(num_scalar_prefetch=1, grid=(r // tr, c // tc),
                                               in_specs=[blk, blk, blk, own, part(0), part(1), part(2)],
                                               out_specs=[blk, blk, blk, blk]),
        compiler_params=_params("parallel", "parallel"),
    )(my_chip, w, m, v, sums, recv, recv, recv)


def adamw_small(w, g, m, v):
    def body(w_ref, g_ref, m_ref, v_ref, d_ref, nm_ref, nv_ref):
        d, nm, nv = _adamw(w_ref[...], g_ref[...], m_ref[...], v_ref[...])
        d_ref[...] = d
        nm_ref[...] = nm
        nv_ref[...] = nv

    shape = jax.ShapeDtypeStruct(w.shape, F32)
    return pl.pallas_call(body, name="adamw_small", out_shape=[shape, shape, shape])(w, g, m, v)


def pair_add(name, a, b, tile):
    q, r, c = b.shape
    tr, tc = tile
    assert r % tr == 0 and c % tc == 0

    def body(core_ref, a_ref, b_ref, o_ref):
        o_ref[...] = (a_ref[...].astype(F32) + b_ref[...].astype(F32)).astype(o_ref.dtype)

    blk = pl.BlockSpec((None, tr, tc), lambda s, i, j, core: (s, i, j))
    mine = pl.BlockSpec((None, None, tr, tc), lambda s, i, j, core: (s, core[0], i, j))
    return pl.pallas_call(
        body, name=name, out_shape=jax.ShapeDtypeStruct((q, r, c), BF16),
        grid_spec=pltpu.PrefetchScalarGridSpec(num_scalar_prefetch=1, grid=(q, r // tr, c // tc),
                                               in_specs=[mine, blk], out_specs=blk),
        compiler_params=_params("parallel", "parallel", "parallel"),
    )(lax.axis_index("c").astype(jnp.int32).reshape(1), a, b)


def _w_in_pieces():
    c = W_IN // N_DEV
    ranges = ((0, 8 * HW, 0, 0), (8 * HW, 8 * HW + 2 * HEADS, 1, 8 * HW), (8 * HW + 2 * HEADS, W_IN, 0, 2 * HEADS))
    out = []
    for d in range(N_DEV):
        for lo, hi, target, shift in ranges:
            s, e = max(lo, c * d), min(hi, c * (d + 1))
            if s < e:
                out.append((d, s - c * d, e - s, target, s - shift))
    return out


def _ffn_in_pieces():
    c = 2 * D_FF // N_DEV
    out = []
    for d in range(N_DEV):
        s = c * d
        while s < c * (d + 1):
            e = min((s // FF_BLK + 1) * FF_BLK, c * (d + 1))
            half, blk = divmod(s // FF_BLK, D_FF // FF_BLK)
            out.append((d, s - c * d, e - s, 0, (2 * blk + half) * FF_BLK + s % FF_BLK))
            s = e
    return out


def _plain_pieces(c):
    return [(d, 0, c, 0, c * d) for d in range(N_DEV)]


def shards_to_cols(name, g, widths, pieces, tr):
    _, r, c = g.shape
    covered = [sum(p[2] for p in pieces if p[3] == t) for t in range(len(widths))]

    def body(g_ref, *outs):
        for t, o in enumerate(outs):
            if covered[t] < widths[t]:
                o[...] = jnp.zeros(o.shape, o.dtype)
        for d, s, w, t, ts in pieces:
            outs[t][:, ts:ts + w] = g_ref[d, :, s:s + w]

    return pl.pallas_call(
        body, name=name, grid=(r // tr,),
        in_specs=[pl.BlockSpec((N_DEV, tr, c), lambda i: (0, i, 0))],
        out_specs=[pl.BlockSpec((tr, w), lambda i: (i, 0)) for w in widths],
        out_shape=[jax.ShapeDtypeStruct((r, w), g.dtype) for w in widths],
        compiler_params=_params("parallel"),
    )(g)


def cols_to_shards(name, srcs, c, pieces, tr):
    r = srcs[0].shape[0]

    def body(*refs):
        o = refs[-1]
        for d, s, w, t, ts in pieces:
            o[d, :, s:s + w] = refs[t][:, ts:ts + w]

    return pl.pallas_call(
        body, name=name, grid=(r // tr,),
        in_specs=[pl.BlockSpec((tr, t.shape[1]), lambda i: (i, 0)) for t in srcs],
        out_specs=pl.BlockSpec((N_DEV, tr, c), lambda i: (0, i, 0)),
        out_shape=jax.ShapeDtypeStruct((N_DEV, r, c), srcs[0].dtype),
        compiler_params=_params("parallel"),
    )(*srcs)


def rows_to_shards(name, srcs, c, pieces, tc):
    r = srcs[0].shape[1]

    def body(*refs):
        o = refs[-1]
        for d, s, w, t, ts in pieces:
            o[d, s:s + w, :] = refs[t][ts:ts + w, :]

    return pl.pallas_call(
        body, name=name, grid=(r // tc,),
        in_specs=[pl.BlockSpec((t.shape[0], tc), lambda i: (0, i)) for t in srcs],
        out_specs=pl.BlockSpec((N_DEV, c, tc), lambda i: (0, 0, i)),
        out_shape=jax.ShapeDtypeStruct((N_DEV, c, r), srcs[0].dtype),
        compiler_params=_params("parallel"),
    )(*srcs)


def _place():
    return lax.axis_index("x"), lax.axis_index("y"), lax.axis_index("c")


def _dev(px, py, pc):
    return 4 * px + 2 * py + pc


class Exchange:
    def __init__(self, arrays, out_shapes, sems, start, finish):
        self.arrays, self.out_shapes, self.sems, self.start, self.finish = arrays, out_shapes, sems, start, finish


def gather_exchange(arrays):
    n = len(arrays)

    def plan(ins, outs, sems):
        send_sems, recv_sems, local_sems = sems
        x, y, c = _place()
        me, sibling = (x, y, c), (x, y, 1 - c)
        chips = [(1 - x, y), (x, 1 - y), (1 - x, 1 - y)]

        def copy(a, k, block, to, src=None):
            dst = outs[a].at[_dev(*block)]
            return pltpu.make_async_remote_copy(
                src_ref=dst if src is None else src, dst_ref=dst, send_sem=send_sems.at[a, k],
                recv_sem=recv_sems.at[a, k], device_id=to, device_id_type=MESH)

        mine = [pltpu.make_async_copy(ins[a], outs[a].at[_dev(*me)], local_sems.at[a]) for a in range(n)]
        first = []
        for a in range(n):
            first.append(copy(a, 0, me, sibling, src=ins[a]))
            first += [copy(a, 1 + j, me, (*chip, c), src=ins[a]) for j, chip in enumerate(chips)]
        return me, sibling, chips, c, copy, mine, first

    def start(ins, outs, sems):
        *_, mine, first = plan(ins, outs, sems)
        for cp in mine + first:
            cp.start()

    def finish(ins, outs, sems):
        me, sibling, chips, c, copy, mine, first = plan(ins, outs, sems)
        passed = []
        for j, chip in enumerate(chips):
            for a in range(n):
                copy(a, 1 + j, (*chip, c), me).wait_recv()
                fwd = copy(a, 4 + j, (*chip, c), sibling)
                fwd.start()
                passed.append(fwd)
        for a in range(n):
            copy(a, 0, sibling, me).wait_recv()
        for j, chip in enumerate(chips):
            for a in range(n):
                copy(a, 4 + j, (*chip, 1 - c), me).wait_recv()
        for cp in first + passed:
            cp.wait_send()
        for cp in mine:
            cp.wait()

    return Exchange(arrays, [jax.ShapeDtypeStruct((N_DEV,) + t.shape, t.dtype) for t in arrays],
                    [pltpu.SemaphoreType.DMA((n, 7)), pltpu.SemaphoreType.DMA((n, 7)), pltpu.SemaphoreType.DMA((n,))],
                    start, finish)


def core_exchange(arrays):
    n = len(arrays)

    def copies(ins, outs, sems):
        send_sems, recv_sems = sems
        x, y, c = _place()
        return [pltpu.make_async_remote_copy(
            src_ref=ins[a].at[q, 1 - c], dst_ref=outs[a].at[q], send_sem=send_sems.at[a, q],
            recv_sem=recv_sems.at[a, q], device_id=(x, y, 1 - c), device_id_type=MESH)
            for a in range(n) for q in range(4)]

    def start(ins, outs, sems):
        for cp in copies(ins, outs, sems):
            cp.start()

    def finish(ins, outs, sems):
        for cp in copies(ins, outs, sems):
            cp.wait()

    return Exchange(arrays, [jax.ShapeDtypeStruct((4,) + t.shape[2:], t.dtype) for t in arrays],
                    [pltpu.SemaphoreType.DMA((n, 4)), pltpu.SemaphoreType.DMA((n, 4))], start, finish)


def chips_exchange(arrays):
    n = len(arrays)

    def copies(ins, outs, sems):
        send_sems, recv_sems = sems
        x, y, c = _place()
        chips = [(1 - x, y), (x, 1 - y), (1 - x, 1 - y)]
        return [pltpu.make_async_remote_copy(
            src_ref=ins[a].at[2 * qx + qy], dst_ref=outs[a].at[j], send_sem=send_sems.at[a, j],
            recv_sem=recv_sems.at[a, j], device_id=(qx, qy, c), device_id_type=MESH)
            for a in range(n) for j, (qx, qy) in enumerate(chips)]

    def start(ins, outs, sems):
        for cp in copies(ins, outs, sems):
            cp.start()

    def finish(ins, outs, sems):
        for cp in copies(ins, outs, sems):
            cp.wait()

    return Exchange(arrays, [jax.ShapeDtypeStruct((3,) + t.shape[1:], t.dtype) for t in arrays],
                    [pltpu.SemaphoreType.DMA((n, 3)), pltpu.SemaphoreType.DMA((n, 3))], start, finish)


def run_exchange(name, exch):
    n_in, n_out = len(exch.arrays), len(exch.out_shapes)
    any_spec = pl.BlockSpec(memory_space=pl.ANY)

    def body(*refs):
        ins, outs, sems = refs[:n_in], refs[n_in:n_in + n_out], refs[n_in + n_out:]
        exch.start(ins, outs, sems)
        exch.finish(ins, outs, sems)

    return pl.pallas_call(
        body, name=name, in_specs=[any_spec] * n_in, out_specs=[any_spec] * n_out, out_shape=exch.out_shapes,
        scratch_shapes=exch.sems,
    )(*exch.arrays)


def host_exchange(body, n_in, n_out, exch, first, last):
    ne_in, ne_out, ne_sem = len(exch.arrays), len(exch.out_shapes), len(exch.sems)

    def wrapped(*refs):
        ins, refs = refs[:n_in], refs[n_in:]
        e_ins, refs = refs[:ne_in], refs[ne_in:]
        outs, refs = refs[:n_out], refs[n_out:]
        e_outs, refs = refs[:ne_out], refs[ne_out:]
        scratch, e_sems = refs[:len(refs) - ne_sem], refs[len(refs) - ne_sem:]

        @pl.when(first())
        def _():
            exch.start(e_ins, e_outs, e_sems)

        body(*ins, *outs, *scratch)

        @pl.when(last())
        def _():
            exch.finish(e_ins, e_outs, e_sems)

    return wrapped


def _hosted(exch, in_specs, out_specs, out_shape, scratch_shapes):
    any_spec = pl.BlockSpec(memory_space=pl.ANY)
    return dict(in_specs=list(in_specs) + [any_spec] * len(exch.arrays),
                out_specs=list(out_specs) + [any_spec] * len(exch.out_shapes),
                out_shape=list(out_shape) + list(exch.out_shapes),
                scratch_shapes=list(scratch_shapes) + list(exch.sems))


def allreduce_small(buf):
    r = buf.shape[0]
    vmem = pl.BlockSpec(memory_space=pltpu.VMEM)

    def body(buf_ref, out_ref, all_ref, send_sems, recv_sems):
        x, y, c = _place()
        me = _dev(x, y, c)
        copies = []
        for k in range(1, N_DEV):
            peer = (x ^ (k >> 2), y ^ ((k >> 1) & 1), c ^ (k & 1))
            copies.append(pltpu.make_async_remote_copy(
                src_ref=buf_ref, dst_ref=all_ref.at[me], send_sem=send_sems.at[k - 1], recv_sem=recv_sems.at[k - 1],
                device_id=peer, device_id_type=MESH))
        for cp in copies:
            cp.start()
        all_ref[me] = buf_ref[...]
        for cp in copies:
            cp.wait()
        total = all_ref[0]
        for d in range(1, N_DEV):
            total = total + all_ref[d]
        out_ref[...] = total

    return pl.pallas_call(
        body, name="allreduce_small", in_specs=[vmem], out_specs=vmem,
        out_shape=jax.ShapeDtypeStruct((r, LANES), F32),
        scratch_shapes=[pltpu.VMEM((N_DEV, r, LANES), F32), pltpu.SemaphoreType.DMA((N_DEV - 1,)),
                        pltpu.SemaphoreType.DMA((N_DEV - 1,))],
    )(buf)


def _cols_from_shards(g):
    return jnp.transpose(g, (1, 0, 2)).reshape(g.shape[1], N_DEV * g.shape[2])


def _pack(t):
    flat = t.reshape(-1)
    return jnp.pad(flat, (0, -flat.size % (8 * LANES))).reshape(-1, LANES)


def _pad_lanes(t):
    t = t.reshape(1, -1)
    return jnp.pad(t, ((0, 0), (0, LANES - t.shape[1])))


def kernel(x, meta_tokens, lb_logits, mix_norm_w, w_in, hg_norm_w, gd_conv_w, gd_a_log, gd_dt_bias, gd_norm_w, w_branch_a, w_branch_b, w_out, ffn_norm_w, w_ffn_in, w_ffn_out, final_norm_w, loss_target, m_meta_tokens, m_lb_logits, m_mix_norm_w, m_w_in, m_hg_norm_w, m_gd_conv_w, m_gd_a_log, m_gd_dt_bias, m_gd_norm_w, m_w_branch_a, m_w_branch_b, m_w_out, m_ffn_norm_w, m_w_ffn_in, m_w_ffn_out, m_final_norm_w, v_meta_tokens, v_lb_logits, v_mix_norm_w, v_w_in, v_hg_norm_w, v_gd_conv_w, v_gd_a_log, v_gd_dt_bias, v_gd_norm_w, v_w_branch_a, v_w_branch_b, v_w_out, v_ffn_norm_w, v_w_ffn_in, v_w_ffn_out, v_final_norm_w):
    xs = x[0]
    tgt = loss_target[0]
    l = xs.shape[0]
    lp = PRE + l
    me = _dev(*_place())

    big = [w_in[0], w_branch_a[0], w_branch_b[0], w_out[0], w_ffn_in[0], w_ffn_out[0]]
    big16 = [t.astype(BF16) for t in big]
    g_in, g_meta, g_conv = run_exchange("gather_first", gather_exchange([big16[0], meta_tokens, gd_conv_w[0]]))
    w_main, w_ab = shards_to_cols("w_in_cols", g_in, [W_MAIN, LANES], _w_in_pieces(), 256)
    meta_full = _cols_from_shards(g_meta)
    cw = _cols_from_shards(g_conv)
    pre = jnp.concatenate([jnp.zeros((PRE - N_META, D), F32), meta_full], axis=0)
    l0, l1 = lb_logits[0:1], lb_logits[1:2]
    alog, dtb = _pad_lanes(gd_a_log), _pad_lanes(gd_dt_bias)

    xn = norm1_fwd(pre, xs, mix_norm_w)
    proj, (g_ba, g_bb, g_out, g_ffi, g_ffo) = matmul("proj_main", xn, w_main, "nn", lp, W_MAIN, D, F32,
                                                      exch=gather_exchange(big16[1:]))
    wa, = shards_to_cols("w_branch_a_cols", g_ba, [D], _plain_pieces(D // N_DEV), 256)
    wb, = shards_to_cols("w_branch_b_cols", g_bb, [D], _plain_pieces(D // N_DEV), 256)
    wo = g_out.reshape(D, D)
    w_gu, = shards_to_cols("w_ffn_in_cols", g_ffi, [2 * D_FF], _ffn_in_pieces(), 256)
    w_fo = g_ffo.reshape(D_FF, D)
    proj_ab = matmul("proj_ab", xn, w_ab, "nn", lp, LANES, D, F32)
    o_a, st_a, o_b, st_b, t_inv = mixer_fwd(proj, proj_ab, l0, l1, hg_norm_w, cw, alog, dtb, gd_norm_w)
    ya = matmul("branch_a", o_a, wa, "nn", l, D, HW, F32, a_off=(PRE // 512, 0))
    yb = matmul("branch_b", o_b, wb, "nn", l, D, HW, F32, a_off=(PRE // 512, 0))
    merged = merge_fwd(proj, ya, yb)
    h1, xn2 = mix_out_norm2(merged, wo, xs, ffn_norm_w)
    gu, act = ffn_in_fused(xn2, w_gu)
    f = matmul("ffn_out", act, w_fo, "nn", l, D, D_FF, F32, tn=512)
    loss_part, dh2, dh2_b, d_final_w = loss_head(h1, f, final_norm_w.reshape(1, D), tgt)

    tiles = {"w_in": (W_IN // N_DEV, 256), "w_branch_a": (256, D // N_DEV), "w_branch_b": (256, D // N_DEV),
             "w_out": (64, D), "w_ffn_in": (256, 2 * D_FF // N_DEV), "w_ffn_out": (176, D)}

    def chip_sums(tag, parts, host=None):
        blocks = [p.reshape((4, 2) + p.shape[1:]) for p in parts.values()]
        exch = core_exchange(blocks)
        hosted, from_core = (None, run_exchange("exchange_core_" + tag, exch)) if host is None else host(exch)
        return hosted, {nm: pair_add("pair_add_" + nm, p, r, tiles[nm]) for (nm, p, r) in zip(parts, blocks, from_core)}

    dgu = d_act_fused(dh2_b, w_fo, gu)
    dw_fo = matmul("dw_ffn_out", act, dh2_b, "tn", D_FF, D, l, BF16, tm=512, tn=1024, tk=2048)
    dxn2 = matmul("d_xn2", dgu, w_gu, "nt", l, D, 2 * D_FF, F32, tn=512, tk=D_FF)
    dw_gu = matmul("dw_ffn_in", xn2, dgu, "tn", D, 2 * D_FF, l, BF16, tm=1024, tn=1024, tk=2048)
    dh1, dh1_b, d_ffn_norm_w = norm2_bwd(h1, ffn_norm_w, dxn2, dh2)
    dmerged, sums = chip_sums("ffn", {
        "w_ffn_in": cols_to_shards("dw_ffn_in_shards", [dw_gu], 2 * D_FF // N_DEV, _ffn_in_pieces(), 256),
        "w_ffn_out": dw_fo.reshape(N_DEV, D_FF // N_DEV, D)},
        host=lambda exch: matmul("d_merged", dh1_b, wo, "nt", l, D, D, F32, exch=exch))
    dw_o = matmul("dw_out", merged, dh1_b, "tn", D, D, l, BF16, tm=1024, tn=1024, tk=2048)
    dproj, dya, dyb = merge_bwd(proj, ya, yb, dmerged)
    do_a = matmul("d_o_a", dya, wa, "nt", lp, HW, D, F32)
    do_b = matmul("d_o_b", dyb, wb, "nt", lp, HW, D, F32)
    dw_a = matmul("dw_branch_a", o_a, dya, "tn", HW, D, lp, BF16, tm=1024, tn=1024, tk=lp // 4)
    dw_b = matmul("dw_branch_b", o_b, dyb, "tn", HW, D, lp, BF16, tm=1024, tn=1024, tk=lp // 4)
    sums.update(chip_sums("mix", {
        "w_branch_a": cols_to_shards("dw_branch_a_shards", [dw_a], D // N_DEV, _plain_pieces(D // N_DEV), 256),
        "w_branch_b": cols_to_shards("dw_branch_b_shards", [dw_b], D // N_DEV, _plain_pieces(D // N_DEV), 256),
        "w_out": dw_o.reshape(N_DEV, D // N_DEV, D)})[1])
    ffn_names, mix_names = ["w_ffn_in", "w_ffn_out"], ["w_branch_a", "w_branch_b", "w_out"]
    (dproj, dl0, dl1, d_hg_nw, dproj_ab, d_conv, d_alog, d_dtb, d_gd_nw), from_chips_early = mixer_bwd(
        proj, proj_ab, l0, l1, hg_norm_w, cw, alog, dtb, gd_norm_w, st_a, st_b, t_inv, do_a, do_b, dproj,
        chips_exchange([sums[nm] for nm in ffn_names + mix_names]))
    dwt_main = matmul("dw_in_main", dproj, xn, "tn", W_MAIN, D, lp, BF16, tm=1024, tn=1024, tk=lp // 4)
    dwt_ab = matmul("dw_in_ab", dproj_ab, xn, "tn", LANES, D, lp, BF16, tn=1024, tk=lp // 4)
    dxn_ab, sums_in = chip_sums("w_in", {
        "w_in": rows_to_shards("dw_in_shards", [dwt_main, dwt_ab], W_IN // N_DEV, _w_in_pieces(), 256)},
        host=lambda exch: matmul("d_xn_ab", dproj_ab, w_ab, "nt", lp, D, LANES, F32, exch=exch))
    sums.update(sums_in)
    from_chips = dict(zip(ffn_names + mix_names, from_chips_early))
    dxn, (from_chips["w_in"],) = matmul("d_xn", dproj, w_main, "nt", lp, D, W_MAIN, F32, tn=512, tk=4096,
                                        exch=chips_exchange([sums["w_in"]]))
    grad_x, dpre, d_mix_norm_w = norm1_bwd(pre, xs, mix_norm_w, dxn, dxn_ab, dh1)
    names = ["w_in", "w_branch_a", "w_branch_b", "w_out", "w_ffn_in", "w_ffn_out"]
    moms = [(m_w_in, v_w_in), (m_w_branch_a, v_w_branch_a), (m_w_branch_b, v_w_branch_b), (m_w_out, v_w_out),
            (m_w_ffn_in, v_w_ffn_in), (m_w_ffn_out, v_w_ffn_out)]
    upd = {}
    for nm, w, (m, v) in zip(names, big, moms):
        flip = (lambda t: t.T) if nm == "w_in" else (lambda t: t)
        res = adamw_shard("adamw_" + nm, flip(w), flip(m[0]), flip(v[0]), sums[nm], from_chips[nm], tiles[nm])
        upd[nm] = tuple(flip(t_)[None] for t_ in res)

    d_lb = jnp.concatenate([dl0, dl1], axis=0)
    rep = [_pack(t) for t in (d_lb, d_mix_norm_w, d_hg_nw, d_alog, d_dtb, d_gd_nw, d_ffn_norm_w, d_final_w)]
    n_rep = sum(t.shape[0] for t in rep)
    n_meta = N_META * D // LANES
    tot = allreduce_small(jnp.concatenate(rep + [_pack(loss_part), _pack(dpre[PRE - N_META:]), _pack(d_conv)], axis=0))
    loss = tot[n_rep, 0]
    g_meta_full = tot[n_rep + 8:n_rep + 8 + n_meta].reshape(N_META, D)
    g_conv_full = tot[n_rep + 8 + n_meta:].reshape(CONV_K, 3 * HW)
    g_meta_mine = lax.dynamic_slice_in_dim(g_meta_full, me * (D // N_DEV), D // N_DEV, axis=1)
    g_conv_mine = lax.dynamic_slice_in_dim(g_conv_full, me * (3 * DH), 3 * DH, axis=1)

    small = [("lb_logits", lb_logits, m_lb_logits, v_lb_logits), ("mix_norm_w", mix_norm_w, m_mix_norm_w, v_mix_norm_w),
             ("hg_norm_w", hg_norm_w, m_hg_norm_w, v_hg_norm_w), ("gd_a_log", gd_a_log, m_gd_a_log, v_gd_a_log),
             ("gd_dt_bias", gd_dt_bias, m_gd_dt_bias, v_gd_dt_bias), ("gd_norm_w", gd_norm_w, m_gd_norm_w, v_gd_norm_w),
             ("ffn_norm_w", ffn_norm_w, m_ffn_norm_w, v_ffn_norm_w),
             ("final_norm_w", final_norm_w, m_final_norm_w, v_final_norm_w),
             ("meta_tokens", meta_tokens, m_meta_tokens, v_meta_tokens), ("gd_conv_w", gd_conv_w, m_gd_conv_w, v_gd_conv_w)]

    sizes = [_pack(w).shape[0] for _, w, _, _ in small]
    g_small = jnp.concatenate([tot[:n_rep], _pack(g_meta_mine), _pack(g_conv_mine)], axis=0)
    assert g_small.shape[0] == sum(sizes)
    w_small = jnp.concatenate([_pack(w) for _, w, _, _ in small], axis=0)
    m_small = jnp.concatenate([_pack(m) for _, _, m, _ in small], axis=0)
    v_small = jnp.concatenate([_pack(v) for _, _, _, v in small], axis=0)
    d_small, nm_small, nv_small = adamw_small(w_small, g_small, m_small, v_small)
    row = 0
    for (nm, w, _, _), sz in zip(small, sizes):
        def unpack(t):
            return t[row:row + sz].reshape(-1)[:w.size].reshape(w.shape)
        upd[nm] = (unpack(g_small), unpack(d_small), unpack(nm_small), unpack(nv_small))
        row += sz

    order = ["meta_tokens", "lb_logits", "mix_norm_w", "w_in", "hg_norm_w", "gd_conv_w", "gd_a_log", "gd_dt_bias",
             "gd_norm_w", "w_branch_a", "w_branch_b", "w_out", "ffn_norm_w", "w_ffn_in", "w_ffn_out", "final_norm_w"]
    outs = [loss, grad_x[None]]
    for j in range(4):
        outs += [upd[nm][j] for nm in order]
    return tuple(outs)
```

```python
import functools

import jax
import jax.numpy as jnp
from jax import lax
from jax.experimental import pallas as pl
from jax.experimental.pallas import tpu as pltpu

F32 = jnp.float32
BF16 = jnp.bfloat16
MESH = pl.DeviceIdType.MESH

EPS = 1e-6
D = 2048
N_META = 16
CHUNK = 64
SUB = 16
HEADS = 8
DH = 128
HW = HEADS * DH
CONV_K = 4
TAIL = 8
D_FF = 5632
PRE = 512
N_SKIP = PRE // CHUNK - 1
N_DEV = 8
LANES = 128
FF_BLK = 512
W_IN = 4 * HW + 4 * HW + 2 * HEADS + 2 * D
W_MAIN = 4 * HW + 4 * HW + 2 * D

ADAM_LR = 0.001
ADAM_B1 = 0.9
ADAM_B2 = 0.999
ADAM_EPS = 1e-08
ADAM_WD = 0.01
ADAM_STEP = 10

VMEM_LIMIT = 52 * 1024 * 1024

_NN = ((1,), (0,))
_NT = ((1,), (1,))
_TN = ((0,), (0,))


def _params(*sem):
    return pltpu.CompilerParams(dimension_semantics=sem, vmem_limit_bytes=VMEM_LIMIT)


BF16_ONCE, SPLIT, EXACT_LHS = 0, 1, 2


def _dot_bf16(a, b, dims):
    if a.ndim == 3:
        dn = (((dims[0][0] + 1,), (dims[1][0] + 1,)), ((0,), (0,)))
    else:
        dn = (dims, ((), ()))
    return lax.dot_general(a, b, dn, preferred_element_type=F32)


def _split(t):
    hi = t.astype(BF16)
    return hi, (t - hi.astype(F32)).astype(BF16)


def _raw_dot(a, b, dims, mode):
    if mode == BF16_ONCE:
        return _dot_bf16(a.astype(BF16), b.astype(BF16), dims)
    if mode == SPLIT:
        a_hi, a_lo = _split(a)
        b_hi, b_lo = _split(b)
        return _dot_bf16(a_hi, b_hi, dims) + (_dot_bf16(a_hi, b_lo, dims) + _dot_bf16(a_lo, b_hi, dims))
    a = a.astype(BF16)
    b_hi = b.astype(BF16)
    rest = b - b_hi.astype(F32)
    b_mid = rest.astype(BF16)
    b_lo = (rest - b_mid.astype(F32)).astype(BF16)
    return _dot_bf16(a, b_hi, dims) + (_dot_bf16(a, b_mid, dims) + _dot_bf16(a, b_lo, dims))


@functools.partial(jax.custom_vjp, nondiff_argnums=(2,))
def mm_nn(a, b, mode=BF16_ONCE):
    return _raw_dot(a, b, _NN, mode)


@functools.partial(jax.custom_vjp, nondiff_argnums=(2,))
def mm_nt(a, b, mode=BF16_ONCE):
    return _raw_dot(a, b, _NT, mode)


@functools.partial(jax.custom_vjp, nondiff_argnums=(2,))
def mm_tn(a, b, mode=BF16_ONCE):
    return _raw_dot(a, b, _TN, mode)


def _general(mode):
    return SPLIT if mode == EXACT_LHS else mode


mm_nn.defvjp(lambda a, b, p: (_raw_dot(a, b, _NN, p), (a, b)),
             lambda p, res, g: (mm_nt(g, res[1], _general(p)), mm_tn(res[0], g, p)))
mm_nt.defvjp(lambda a, b, p: (_raw_dot(a, b, _NT, p), (a, b)),
             lambda p, res, g: (mm_nn(g, res[1], p), mm_tn(g, res[0], p)))
mm_tn.defvjp(lambda a, b, p: (_raw_dot(a, b, _TN, p), (a, b)),
             lambda p, res, g: (mm_nt(res[1], g, _general(p)), mm_nn(res[0], g, _general(p))))


def _sigmoid(x):
    return jax.nn.sigmoid(x)


def _silu(x):
    return x * _sigmoid(x)


def _softplus(x):
    return jnp.maximum(x, 0.0) + jnp.log(1.0 + jnp.exp(-jnp.abs(x)))


def _l2n(t):
    return t * lax.rsqrt(jnp.sum(t * t, axis=-1, keepdims=True) + EPS)


def _rms_norm(x, w):
    return x * lax.rsqrt(jnp.mean(x * x, axis=-1, keepdims=True) + EPS) * w


def _gated_norm(o, gate, nw):
    o = o * lax.rsqrt(jnp.mean(o * o, axis=-1, keepdims=True) + EPS) * nw
    return o * _silu(gate)


def _heads(ref, piece):
    return jnp.stack([ref[:, piece * HW + DH * j:piece * HW + DH * (j + 1)] for j in range(HEADS)])


def _put_heads(ref, piece, value, accumulate=False):
    for j in range(HEADS):
        cols = slice(piece * HW + DH * j, piece * HW + DH * (j + 1))
        if accumulate:
            ref[:, cols] += value[j]
        else:
            ref[:, cols] = value[j].astype(ref.dtype)


def _diag_scores(q_i, k_i, f_i):
    nb, s, d = q_i.shape
    row = lax.broadcasted_iota(jnp.int32, (nb, s, d), 1)
    q4, f4 = q_i[:, :, None, :], f_i[:, :, None, :]
    kk = k_i
    rows_out = []
    for t in range(s):
        if t > 0:
            kk = jnp.where(row < t, kk * f4[:, t], kk)
        rows_out.append(jnp.sum(kk * q4[:, t], axis=-1)[:, None, :])
    s_ii = jnp.concatenate(rows_out, axis=1)
    tri = lax.broadcasted_iota(jnp.int32, (nb, s, s), 1) >= lax.broadcasted_iota(jnp.int32, (nb, s, s), 2)
    return jnp.where(tri, s_ii, 0.0)


def hg_chunk(hq, hf, hi, hg, l0, l1, nw, st):
    nb = hq.shape[0]
    m = jnp.maximum(l0, l1)
    e0 = jnp.exp(l0 - m)
    e1 = jnp.exp(l1 - m)
    lb = e0 / (e0 + e1)
    sig = _sigmoid(hf)
    q = _silu(hq)
    f = lb + (1.0 - lb) * sig
    log_f = jnp.log(f)
    k = (1.0 - lb) * _sigmoid(-hf)
    v = hi
    rows = lax.broadcasted_iota(jnp.int32, (nb, CHUNK, CHUNK), 1)
    cols = lax.broadcasted_iota(jnp.int32, (nb, CHUNK, CHUNK), 2)
    b = mm_nn((rows >= cols).astype(F32), log_f, EXACT_LHS)
    o_inter = mm_nt(q * jnp.exp(b), st)
    outs = []
    for i in range(CHUNK // SUB):
        lo = i * SUB
        b_i, q_i, k_i, v_i = b[:, lo:lo + SUB], q[:, lo:lo + SUB], k[:, lo:lo + SUB], v[:, lo:lo + SUB]
        o_i = mm_nn(_diag_scores(q_i, k_i, f[:, lo:lo + SUB]), v_i)
        if i > 0:
            b_ref = b[:, :, None, :][:, lo - 1]
            q_t = q_i * jnp.exp(b_i - b_ref)
            k_t = k[:, :lo] * jnp.exp(b_ref - b[:, :lo])
            o_i = o_i + mm_nn(mm_nt(q_t, k_t), v[:, :lo])
        outs.append(o_i)
    o = o_inter + jnp.concatenate(outs, axis=1)
    b_last = b[:, :, None, :][:, CHUNK - 1]
    st_new = st * jnp.exp(b_last) + mm_tn(v, k * jnp.exp(b_last - b))
    return _gated_norm(o, hg, nw), st_new


@jax.custom_vjp
def _unit_lower_inverse(a):
    n = a.shape[-1]
    eye = (lax.broadcasted_iota(jnp.int32, a.shape, 1) == lax.broadcasted_iota(jnp.int32, a.shape, 2)).astype(F32)
    x = eye - a
    p = mm_nn(a, a, SPLIT)
    x = x + mm_nn(x, p, SPLIT)
    power = 4
    while power < n:
        p = mm_nn(p, p, SPLIT)
        x = x + mm_nn(x, p, SPLIT)
        power *= 2
    return x


def _unit_lower_inverse_fwd(a):
    t = _unit_lower_inverse(a)
    return t, t


def _unit_lower_inverse_bwd(t, dt):
    return (-mm_tn(t, mm_nt(dt, t, SPLIT), SPLIT),)


_unit_lower_inverse.defvjp(_unit_lower_inverse_fwd, _unit_lower_inverse_bwd)


@jax.custom_vjp
def _kept_inverse(a, t):
    return t


_kept_inverse.defvjp(lambda a, t: (t, t),
                     lambda t, dt: (-mm_tn(t, mm_nt(dt, t, SPLIT), SPLIT), jnp.zeros_like(t)))


def gd_chunk(cq, ck, cv, z, ab, alog, dtb, nw, s, kept_t=None, keep_t=False):
    nb, c, _ = cq.shape
    lane = lax.broadcasted_iota(jnp.int32, (nb, 1, DH), 2)
    head = lax.broadcasted_iota(jnp.int32, (nb, 1, DH), 0)
    pick = lambda t, off: jnp.sum(jnp.where(lane == head + off, t[None], 0.0), axis=2, keepdims=True)
    a_raw, b_raw = pick(ab, 0), pick(ab, HEADS)
    beta = _sigmoid(b_raw)
    g = -jnp.exp(pick(alog, 0)) * _softplus(a_raw + pick(dtb, 0))
    q = _l2n(_silu(cq)) * (DH ** -0.5)
    k = _l2n(_silu(ck))
    v = _silu(cv)
    rows = lax.broadcasted_iota(jnp.int32, (nb, c, c), 1)
    cols = lax.broadcasted_iota(jnp.int32, (nb, c, c), 2)
    tril = (rows >= cols).astype(F32)
    gc_w = mm_nn(tril, jnp.broadcast_to(g, (nb, c, DH)), EXACT_LHS)
    g_sq = jnp.broadcast_to(g, (nb, c, c))
    gc_col = mm_nn(tril, g_sq, EXACT_LHS)
    gc_row = mm_nn(jnp.ones((nb, c, c), F32), g_sq * (rows <= cols).astype(F32), EXACT_LHS)
    rel = jnp.exp(jnp.minimum(gc_col - gc_row, 0.0))
    a = jnp.where(rows > cols, beta * mm_nt(k, k) * rel, 0.0)
    t_inv = _unit_lower_inverse(a) if kept_t is None else _kept_inverse(a, kept_t)
    e_gc = jnp.exp(gc_w)
    w = mm_nn(t_inv, beta * e_gc * k, SPLIT)
    u = mm_nn(t_inv, beta * v, SPLIT)
    v_new = u - mm_nn(w, s)
    attn = jnp.where(rows >= cols, mm_nt(q, k) * rel, 0.0)
    o = mm_nn(q * e_gc, s) + mm_nn(attn, v_new)
    g_last = jnp.sum(g, axis=1, keepdims=True)
    s_new = jnp.exp(g_last) * s + mm_tn(k * jnp.exp(g_last - gc_w), v_new)
    out = _gated_norm(o, z, nw)
    return (out, s_new, t_inv) if keep_t else (out, s_new)


def _gd_conv(x_ref, tail_ref, cw_ref, xe_ref, first):
    xe_ref[0:TAIL, :] = jnp.where(first, 0.0, tail_ref[:, 0:3 * HW])
    xe_ref[TAIL:TAIL + CHUNK, :] = x_ref[:, 0:3 * HW]
    y = cw_ref[pl.ds(0, 1), :] * xe_ref[pl.ds(TAIL - CONV_K + 1, CHUNK), :]
    for k in range(1, CONV_K):
        y = y + cw_ref[pl.ds(k, 1), :] * xe_ref[pl.ds(TAIL - CONV_K + 1 + k, CHUNK), :]
    return y


def _conv_heads(y, piece):
    return jnp.stack([y[:, piece * HW + DH * j:piece * HW + DH * (j + 1)] for j in range(HEADS)])


def mixer_fwd(proj, proj_ab, l0, l1, hg_nw, cw, alog, dtb, gd_nw):
    lp = proj.shape[0]
    nc = lp // CHUNK

    def body(xh_ref, l0_ref, l1_ref, hnw_ref, xg_ref, tail_ref, ab_ref, cw_ref, alog_ref, dtb_ref, gnw_ref,
             oa_ref, sta_out_ref, ob_ref, stb_out_ref, tinv_ref, sta_ref, stb_ref, xe_ref):
        c = pl.program_id(0)

        @pl.when(c == 0)
        def _():
            sta_ref[...] = jnp.zeros(sta_ref.shape, F32)
            stb_ref[...] = jnp.zeros(stb_ref.shape, F32)

        @pl.when(c < N_SKIP)
        def _():
            for ref in (oa_ref, sta_out_ref, ob_ref, stb_out_ref, tinv_ref):
                ref[...] = jnp.zeros(ref.shape, ref.dtype)

        @pl.when(c >= N_SKIP)
        def _():
            stb = stb_ref[...]
            stb_out_ref[0] = stb
            y = _gd_conv(xg_ref, tail_ref, cw_ref, xe_ref, c == 0)
            ob, stb_new, tinv_ref[0] = gd_chunk(
                _conv_heads(y, 0), _conv_heads(y, 1), _conv_heads(y, 2), _heads(xg_ref, 3),
                ab_ref[...], alog_ref[...], dtb_ref[...], gnw_ref[...], stb, keep_t=True)
            _put_heads(ob_ref, 0, ob)
            stb_ref[...] = stb_new
            sta = sta_ref[...]
            sta_out_ref[0] = sta
            oa, sta_new = hg_chunk(_heads(xh_ref, 0), _heads(xh_ref, 1), _heads(xh_ref, 2), _heads(xh_ref, 3),
                                   _heads(l0_ref, 0), _heads(l1_ref, 0), hnw_ref[...], sta)
            _put_heads(oa_ref, 0, oa)
            sta_ref[...] = sta_new

    vec = pl.BlockSpec((1, HW), lambda c: (0, 0))
    one = pl.BlockSpec((1, DH), lambda c: (0, 0))
    st_spec = pl.BlockSpec((1, HEADS, DH, DH), lambda c: (c, 0, 0, 0))
    o_spec = pl.BlockSpec((CHUNK, HW), lambda c: (c, 0))
    o_shape = jax.ShapeDtypeStruct((lp, HW), BF16)
    st_shape = jax.ShapeDtypeStruct((nc, HEADS, DH, DH), F32)
    return pl.pallas_call(
        body, name="mixer_fwd", grid=(nc,),
        in_specs=[pl.BlockSpec((CHUNK, 4 * HW), lambda c: (c, 0)), vec, vec, one,
                  pl.BlockSpec((CHUNK, 4 * HW), lambda c: (c, 1)),
                  pl.BlockSpec((TAIL, 4 * HW), lambda c: (jnp.maximum(c * (CHUNK // TAIL) - 1, 0), 1)),
                  pl.BlockSpec((CHUNK, DH), lambda c: (c, 0)),
                  pl.BlockSpec((CONV_K, 3 * HW), lambda c: (0, 0)), one, one, one],
        out_specs=[o_spec, st_spec, o_spec, st_spec,
                   pl.BlockSpec((1, HEADS, CHUNK, CHUNK), lambda c: (c, 0, 0, 0))],
        out_shape=[o_shape, st_shape, o_shape, st_shape, jax.ShapeDtypeStruct((nc, HEADS, CHUNK, CHUNK), F32)],
        scratch_shapes=[pltpu.VMEM((HEADS, DH, DH), F32), pltpu.VMEM((HEADS, DH, DH), F32),
                        pltpu.VMEM((TAIL + CHUNK, 3 * HW), F32)],
        compiler_params=_params("arbitrary"),
    )(proj, l0, l1, hg_nw, proj, proj, proj_ab, cw, alog, dtb, gd_nw)


def mixer_bwd(proj, proj_ab, l0, l1, hg_nw, cw, alog, dtb, gd_nw, st_a, st_b, t_inv, do_a, do_b, dproj, exch):
    lp = proj.shape[0]
    nc = lp // CHUNK

    def body(xh_ref, l0_ref, l1_ref, hnw_ref, xg_ref, tail_ref, ab_ref, cw_ref, alog_ref, dtb_ref, gnw_ref,
             sta_in_ref, stb_in_ref, tinv_ref, doa_ref, dob_ref, _,
             dx_ref, dl0_ref, dl1_ref, dhnw_ref, dab_ref, dcw_ref, dalog_ref, ddtb_ref, dgnw_ref,
             dsta_ref, dstb_ref, xe_ref, dye_ref, head_ref):
        c = pl.program_id(0)
        cc = nc - 1 - c

        @pl.when(c == 0)
        def _():
            for ref in (dl0_ref, dl1_ref, dhnw_ref, dcw_ref, dalog_ref, ddtb_ref, dgnw_ref, dsta_ref, dstb_ref,
                        head_ref):
                ref[...] = jnp.zeros(ref.shape, F32)

        @pl.when(cc < N_SKIP)
        def _():
            dx_ref[...] = jnp.zeros(dx_ref.shape, dx_ref.dtype)
            dab_ref[...] = jnp.zeros(dab_ref.shape, F32)

        @pl.when(cc >= N_SKIP)
        def _():
            y = _gd_conv(xg_ref, tail_ref, cw_ref, xe_ref, cc == 0)
            _, gd_vjp = jax.vjp(functools.partial(gd_chunk, kept_t=tinv_ref[0]),
                                _conv_heads(y, 0), _conv_heads(y, 1), _conv_heads(y, 2), _heads(xg_ref, 3),
                                ab_ref[...], alog_ref[...], dtb_ref[...], gnw_ref[...], stb_in_ref[0])
            dcq, dck, dcv, dz, dab, dalog, ddtb, dgnw, dstb = gd_vjp((_heads(dob_ref, 0), dstb_ref[...]))
            _, hg_vjp = jax.vjp(hg_chunk, _heads(xh_ref, 0), _heads(xh_ref, 1), _heads(xh_ref, 2), _heads(xh_ref, 3),
                                _heads(l0_ref, 0), _heads(l1_ref, 0), hnw_ref[...], sta_in_ref[0])
            hg_grads = hg_vjp((_heads(doa_ref, 0), dsta_ref[...]))
            dstb_ref[...] = dstb
            dab_ref[...] = dab
            dalog_ref[...] += dalog
            ddtb_ref[...] += ddtb
            dgnw_ref[...] += dgnw
            _put_heads(dx_ref, 4 + 3, dz)
            for i, t in enumerate((dcq, dck, dcv)):
                for j in range(HEADS):
                    dye_ref[0:CHUNK, i * HW + DH * j:i * HW + DH * (j + 1)] = t[j]
            dye_ref[CHUNK:CHUNK + TAIL, :] = head_ref[...]
            head_ref[...] = dye_ref[0:TAIL, :]
            dy = dye_ref[0:CHUNK, :]
            dx = None
            for k in range(CONV_K):
                term = cw_ref[pl.ds(k, 1), :] * dye_ref[pl.ds(CONV_K - 1 - k, CHUNK), :]
                dx = term if dx is None else dx + term
                dcw_ref[pl.ds(k, 1), :] += jnp.sum(dy * xe_ref[pl.ds(TAIL - CONV_K + 1 + k, CHUNK), :],
                                                   axis=0, keepdims=True)
            dx_ref[:, 4 * HW:7 * HW] = dx.astype(dx_ref.dtype)
            for i in range(4):
                _put_heads(dx_ref, i, hg_grads[i])
            _put_heads(dl0_ref, 0, hg_grads[4], accumulate=True)
            _put_heads(dl1_ref, 0, hg_grads[5], accumulate=True)
            dhnw_ref[...] += hg_grads[6]
            dsta_ref[...] = hg_grads[7]

    rc = lambda c: nc - 1 - c
    vec = pl.BlockSpec((1, HW), lambda c: (0, 0))
    one = pl.BlockSpec((1, DH), lambda c: (0, 0))
    one_shape = jax.ShapeDtypeStruct((1, DH), F32)
    vec_shape = jax.ShapeDtypeStruct((1, HW), F32)
    st_spec = pl.BlockSpec((1, HEADS, DH, DH), lambda c: (rc(c), 0, 0, 0))
    do_spec = pl.BlockSpec((CHUNK, HW), lambda c: (rc(c), 0))
    n_in, n_out = 17, 9
    outs = pl.pallas_call(
        host_exchange(body, n_in, n_out, exch, lambda: pl.program_id(0) == 0, lambda: pl.program_id(0) == nc - 1),
        name="mixer_bwd", grid=(nc,), input_output_aliases={n_in - 1: 0}, compiler_params=_params("arbitrary"),
        **_hosted(exch,
                  [pl.BlockSpec((CHUNK, 4 * HW), lambda c: (rc(c), 0)), vec, vec, one,
                   pl.BlockSpec((CHUNK, 4 * HW), lambda c: (rc(c), 1)),
                   pl.BlockSpec((TAIL, 4 * HW), lambda c: (jnp.maximum(rc(c) * (CHUNK // TAIL) - 1, 0), 1)),
                   pl.BlockSpec((CHUNK, DH), lambda c: (rc(c), 0)),
                   pl.BlockSpec((CONV_K, 3 * HW), lambda c: (0, 0)), one, one, one,
                   st_spec, st_spec, pl.BlockSpec((1, HEADS, CHUNK, CHUNK), lambda c: (rc(c), 0, 0, 0)),
                   do_spec, do_spec, pl.BlockSpec(memory_space=pl.ANY)],
                  [pl.BlockSpec((CHUNK, 8 * HW), lambda c: (rc(c), 0)), vec, vec, one,
                   pl.BlockSpec((CHUNK, DH), lambda c: (rc(c), 0)),
                   pl.BlockSpec((CONV_K, 3 * HW), lambda c: (0, 0)), one, one, one],
                  [jax.ShapeDtypeStruct((lp, W_MAIN), BF16), vec_shape, vec_shape, one_shape,
                   jax.ShapeDtypeStruct((lp, DH), F32), jax.ShapeDtypeStruct((CONV_K, 3 * HW), F32),
                   one_shape, one_shape, one_shape],
                  [pltpu.VMEM((HEADS, DH, DH), F32), pltpu.VMEM((HEADS, DH, DH), F32),
                   pltpu.VMEM((TAIL + CHUNK, 3 * HW), F32), pltpu.VMEM((CHUNK + TAIL, 3 * HW), F32),
                   pltpu.VMEM((TAIL, 3 * HW), F32)]),
    )(proj, l0, l1, hg_nw, proj, proj, proj_ab, cw, alog, dtb, gd_nw, st_a, st_b, t_inv, do_a, do_b, dproj,
      *exch.arrays)
    return outs[:n_out], outs[n_out:]


def matmul(name, a, b, mode, m, n, k, out_dtype, tm=512, tn=1024, tk=None, a_off=(0, 0), b_off=(0, 0), exch=None):
    tm, tn = min(tm, m), min(tn, n)
    tk = k if tk is None else min(tk, k)
    assert m % tm == 0 and n % tn == 0 and k % tk == 0, (name, m, n, k, tm, tn, tk)
    gi, gj, gk = m // tm, n // tn, k // tk
    if mode == "nn":
        a_spec = pl.BlockSpec((tm, tk), lambda j, i, kk: (i + a_off[0], kk + a_off[1]))
        b_spec = pl.BlockSpec((tk, tn), lambda j, i, kk: (kk + b_off[0], j + b_off[1]))
        dims = _NN
    elif mode == "nt":
        a_spec = pl.BlockSpec((tm, tk), lambda j, i, kk: (i + a_off[0], kk + a_off[1]))
        b_spec = pl.BlockSpec((tn, tk), lambda j, i, kk: (j + b_off[0], kk + b_off[1]))
        dims = _NT
    else:
        a_spec = pl.BlockSpec((tk, tm), lambda j, i, kk: (kk + a_off[0], i + a_off[1]))
        b_spec = pl.BlockSpec((tk, tn), lambda j, i, kk: (kk + b_off[0], j + b_off[1]))
        dims = _TN

    def body(a_ref, b_ref, o_ref, *acc):
        d = lax.dot_general(a_ref[...].astype(BF16), b_ref[...].astype(BF16), (dims, ((), ())),
                            preferred_element_type=F32)
        if gk == 1:
            o_ref[...] = d.astype(o_ref.dtype)
        else:
            acc_ref, = acc
            kk = pl.program_id(2)

            @pl.when(kk == 0)
            def _():
                acc_ref[...] = d

            @pl.when(kk > 0)
            def _():
                acc_ref[...] += d

            @pl.when(kk == gk - 1)
            def _():
                o_ref[...] = acc_ref[...].astype(o_ref.dtype)

    o_spec = pl.BlockSpec((tm, tn), lambda j, i, kk: (i, j))
    o_shape = jax.ShapeDtypeStruct((m, n), out_dtype)
    scratch = [] if gk == 1 else [pltpu.VMEM((tm, tn), F32)]
    if exch is None:
        return pl.pallas_call(
            body, name=name, grid=(gj, gi, gk), in_specs=[a_spec, b_spec], out_specs=o_spec, out_shape=o_shape,
            scratch_shapes=scratch, compiler_params=_params("parallel", "parallel", "arbitrary"),
        )(a, b)
    step = lambda: (pl.program_id(0), pl.program_id(1), pl.program_id(2))
    first = lambda: jnp.logical_and(jnp.logical_and(step()[0] == 0, step()[1] == 0), step()[2] == 0)
    last = lambda: jnp.logical_and(jnp.logical_and(step()[0] == gj - 1, step()[1] == gi - 1), step()[2] == gk - 1)
    outs = pl.pallas_call(
        host_exchange(body, 2, 1, exch, first, last), name=name, grid=(gj, gi, gk),
        compiler_params=_params("arbitrary", "arbitrary", "arbitrary"),
        **_hosted(exch, [a_spec, b_spec], [o_spec], [o_shape], scratch),
    )(a, b, *exch.arrays)
    return outs[0], outs[1:]


def _swiglu(gu):
    return _silu(gu[:, :FF_BLK]) * gu[:, FF_BLK:]


def ffn_in_fused(xn2, w_gu, tm=512):
    l = xn2.shape[0]
    tn = 2 * FF_BLK

    def body(a_ref, b_ref, gu_ref, act_ref):
        gu = lax.dot_general(a_ref[...], b_ref[...], (_NN, ((), ())), preferred_element_type=F32)
        gu_ref[...] = gu
        act_ref[...] = _swiglu(gu).astype(act_ref.dtype)

    return pl.pallas_call(
        body, name="ffn_in", grid=(2 * D_FF // tn, l // tm),
        in_specs=[pl.BlockSpec((tm, D), lambda j, i: (i, 0)), pl.BlockSpec((D, tn), lambda j, i: (0, j))],
        out_specs=[pl.BlockSpec((tm, tn), lambda j, i: (i, j)), pl.BlockSpec((tm, FF_BLK), lambda j, i: (i, j))],
        out_shape=[jax.ShapeDtypeStruct((l, 2 * D_FF), F32), jax.ShapeDtypeStruct((l, D_FF), BF16)],
        compiler_params=_params("parallel", "parallel"),
    )(xn2, w_gu)


def d_act_fused(dh2, w_fo, gu, tm=512):
    l = dh2.shape[0]

    def body(a_ref, b_ref, gu_ref, o_ref):
        da = lax.dot_general(a_ref[...], b_ref[...], (_NT, ((), ())), preferred_element_type=F32)
        _, vjp = jax.vjp(_swiglu, gu_ref[...])
        dgu, = vjp(da)
        o_ref[...] = dgu.astype(o_ref.dtype)

    return pl.pallas_call(
        body, name="d_act", grid=(D_FF // FF_BLK, l // tm),
        in_specs=[pl.BlockSpec((tm, D), lambda j, i: (i, 0)), pl.BlockSpec((FF_BLK, D), lambda j, i: (j, 0)),
                  pl.BlockSpec((tm, 2 * FF_BLK), lambda j, i: (i, j))],
        out_specs=pl.BlockSpec((tm, 2 * FF_BLK), lambda j, i: (i, j)),
        out_shape=jax.ShapeDtypeStruct((l, 2 * D_FF), BF16),
        compiler_params=_params("parallel", "parallel"),
    )(dh2, w_fo, gu)


TR = 256
N_PRE = PRE // TR


def _row(width, off=0, col=0):
    return pl.BlockSpec((TR, width), lambda i: (i + off, col))


def _pre_spec():
    return pl.BlockSpec((TR, D), lambda i: (jnp.minimum(i, N_PRE - 1), 0))


def _seq_spec(width=D, col=0):
    return pl.BlockSpec((TR, width), lambda i: (jnp.maximum(i - N_PRE, 0), col))


def _vec_spec(width=D):
    return pl.BlockSpec((1, width), lambda i: (0, 0))


def norm1_fwd(pre, x, w):
    lp = PRE + x.shape[0]

    def body(pre_ref, x_ref, w_ref, o_ref):
        h = jnp.where(pl.program_id(0) < N_PRE, pre_ref[...], x_ref[...])
        o_ref[...] = _rms_norm(h, w_ref[...]).astype(o_ref.dtype)

    return pl.pallas_call(
        body, name="norm1_fwd", grid=(lp // TR,),
        in_specs=[_pre_spec(), _seq_spec(), _vec_spec()],
        out_specs=_row(D), out_shape=jax.ShapeDtypeStruct((lp, D), BF16),
        compiler_params=_params("parallel"),
    )(pre, x, w)


def norm1_bwd(pre, x, w, dxn, dxn_ab, dh1):
    l = x.shape[0]
    lp = PRE + l

    def body(pre_ref, x_ref, w_ref, dxn_ref, dxn_ab_ref, dh1_ref, dx_ref, dpre_ref, dw_ref):
        i = pl.program_id(0)
        h = jnp.where(i < N_PRE, pre_ref[...], x_ref[...])
        _, vjp = jax.vjp(_rms_norm, h, w_ref[...])
        dh, dw = vjp(dxn_ref[...] + dxn_ab_ref[...])

        @pl.when(i == 0)
        def _():
            dw_ref[...] = dw

        @pl.when(i > 0)
        def _():
            dw_ref[...] += dw

        @pl.when(i < N_PRE)
        def _():
            dpre_ref[...] = dh

        @pl.when(i >= N_PRE)
        def _():
            dx_ref[...] = dh + dh1_ref[...]

    return pl.pallas_call(
        body, name="norm1_bwd", grid=(lp // TR,),
        in_specs=[_pre_spec(), _seq_spec(), _vec_spec(), _row(D), _row(D), _seq_spec()],
        out_specs=[_seq_spec(), _pre_spec(), _vec_spec()],
        out_shape=[jax.ShapeDtypeStruct((l, D), F32), jax.ShapeDtypeStruct((PRE, D), F32),
                   jax.ShapeDtypeStruct((1, D), F32)],
        compiler_params=_params("arbitrary"),
    )(pre, x, w, dxn, dxn_ab, dh1)


def _merge(gates, ya, yb):
    return _sigmoid(gates[:, :D]) * ya + _sigmoid(gates[:, D:]) * yb


def merge_fwd(proj, ya, yb):
    l = ya.shape[0]

    def body(g_ref, ya_ref, yb_ref, o_ref):
        o_ref[...] = _merge(g_ref[...], ya_ref[...], yb_ref[...]).astype(o_ref.dtype)

    return pl.pallas_call(
        body, name="merge_fwd", grid=(l // TR,),
        in_specs=[_row(2 * D, N_PRE, 2), _row(D), _row(D)],
        out_specs=_row(D), out_shape=jax.ShapeDtypeStruct((l, D), BF16),
        compiler_params=_params("parallel"),
    )(proj, ya, yb)


def merge_bwd(proj, ya, yb, dmerged):
    l = ya.shape[0]
    lp = PRE + l

    def body(g_ref, ya_ref, yb_ref, dm_ref, dg_ref, dya_ref, dyb_ref):
        i = pl.program_id(0)

        @pl.when(i < N_PRE)
        def _():
            dg_ref[...] = jnp.zeros(dg_ref.shape, dg_ref.dtype)
            dya_ref[...] = jnp.zeros(dya_ref.shape, dya_ref.dtype)
            dyb_ref[...] = jnp.zeros(dyb_ref.shape, dyb_ref.dtype)

        @pl.when(i >= N_PRE)
        def _():
            _, vjp = jax.vjp(_merge, g_ref[...], ya_ref[...], yb_ref[...])
            dg, dya, dyb = vjp(dm_ref[...])
            dg_ref[...] = dg.astype(dg_ref.dtype)
            dya_ref[...] = dya.astype(dya_ref.dtype)
            dyb_ref[...] = dyb.astype(dyb_ref.dtype)

    return pl.pallas_call(
        body, name="merge_bwd", grid=(lp // TR,),
        in_specs=[pl.BlockSpec((TR, 2 * D), lambda i: (jnp.maximum(i, N_PRE), 2)), _seq_spec(), _seq_spec(),
                  _seq_spec()],
        out_specs=[_row(2 * D, 0, 2), _row(D), _row(D)],
        out_shape=[jax.ShapeDtypeStruct((lp, W_MAIN), BF16), jax.ShapeDtypeStruct((lp, D), BF16),
                   jax.ShapeDtypeStruct((lp, D), BF16)],
        compiler_params=_params("parallel"),
    )(proj, ya, yb, dmerged)


def mix_out_norm2(merged, wo, x, w, tm=512):
    l = x.shape[0]

    def body(a_ref, b_ref, x_ref, w_ref, h_ref, o_ref):
        h = x_ref[...] + lax.dot_general(a_ref[...], b_ref[...], (_NN, ((), ())), preferred_element_type=F32)
        h_ref[...] = h
        o_ref[...] = _rms_norm(h, w_ref[...]).astype(o_ref.dtype)

    row = pl.BlockSpec((tm, D), lambda i: (i, 0))
    return pl.pallas_call(
        body, name="mix_out", grid=(l // tm,),
        in_specs=[row, pl.BlockSpec((D, D), lambda i: (0, 0)), row, _vec_spec()],
        out_specs=[row, row],
        out_shape=[jax.ShapeDtypeStruct((l, D), F32), jax.ShapeDtypeStruct((l, D), BF16)],
        compiler_params=_params("parallel"),
    )(merged, wo, x, w)


def norm2_bwd(h1, w, dxn2, dh2):
    l = h1.shape[0]

    def body(h_ref, w_ref, dxn_ref, dh2_ref, dh1_ref, dh1b_ref, dw_ref):
        i = pl.program_id(0)
        _, vjp = jax.vjp(_rms_norm, h_ref[...], w_ref[...])
        dh, dw = vjp(dxn_ref[...])
        dh1 = dh + dh2_ref[...]
        dh1_ref[...] = dh1
        dh1b_ref[...] = dh1.astype(BF16)

        @pl.when(i == 0)
        def _():
            dw_ref[...] = dw

        @pl.when(i > 0)
        def _():
            dw_ref[...] += dw

    return pl.pallas_call(
        body, name="norm2_bwd", grid=(l // TR,),
        in_specs=[_row(D), _vec_spec(), _row(D), _row(D)],
        out_specs=[_row(D), _row(D), _vec_spec()],
        out_shape=[jax.ShapeDtypeStruct((l, D), F32), jax.ShapeDtypeStruct((l, D), BF16),
                   jax.ShapeDtypeStruct((1, D), F32)],
        compiler_params=_params("arbitrary"),
    )(h1, w, dxn2, dh2)


def _loss_rows(h1, f, w, tgt):
    y = _rms_norm(h1 + f, w)
    err = (y - tgt) * (y - tgt)
    return 0.5 * jnp.sum(jnp.mean(err, axis=-1, keepdims=True), axis=0, keepdims=True)


def loss_head(h1, f, w, tgt):
    l = h1.shape[0]

    def body(h_ref, f_ref, w_ref, t_ref, loss_ref, dh_ref, dhb_ref, dw_ref):
        i = pl.program_id(0)
        loss, vjp = jax.vjp(_loss_rows, h_ref[...], f_ref[...], w_ref[...], t_ref[...])
        dh, _, dw, _ = vjp(jnp.ones((1, 1), F32))
        dh_ref[...] = dh
        dhb_ref[...] = dh.astype(BF16)
        loss = jnp.broadcast_to(loss, (1, LANES))

        @pl.when(i == 0)
        def _():
            dw_ref[...] = dw
            loss_ref[...] = loss

        @pl.when(i > 0)
        def _():
            dw_ref[...] += dw
            loss_ref[...] += loss

    return pl.pallas_call(
        body, name="loss_head", grid=(l // TR,),
        in_specs=[_row(D), _row(D), _vec_spec(), _row(D)],
        out_specs=[_vec_spec(LANES), _row(D), _row(D), _vec_spec()],
        out_shape=[jax.ShapeDtypeStruct((1, LANES), F32), jax.ShapeDtypeStruct((l, D), F32),
                   jax.ShapeDtypeStruct((l, D), BF16), jax.ShapeDtypeStruct((1, D), F32)],
        compiler_params=_params("arbitrary"),
    )(h1, f, w, tgt)


def _adamw(w, g, m, v):
    m = ADAM_B1 * m + (1.0 - ADAM_B1) * g
    v = ADAM_B2 * v + (1.0 - ADAM_B2) * (g * g)
    m_hat = m / (1.0 - ADAM_B1 ** ADAM_STEP)
    v_hat = v / (1.0 - ADAM_B2 ** ADAM_STEP)
    delta = -ADAM_LR * (m_hat / (jnp.sqrt(v_hat) + ADAM_EPS) + ADAM_WD * w)
    return delta, m, v


def adamw_shard(name, w, m, v, sums, recv, tile):
    r, c = w.shape
    tr, tc = tile
    assert r % tr == 0 and c % tc == 0

    def body(chip_ref, w_ref, m_ref, v_ref, own_ref, r0_ref, r1_ref, r2_ref, g_ref, d_ref, nm_ref, nv_ref):
        g = own_ref[...].astype(F32) + r0_ref[...].astype(F32)
        g = g + r1_ref[...].astype(F32)
        g = g + r2_ref[...].astype(F32)
        d, nm, nv = _adamw(w_ref[...], g, m_ref[...], v_ref[...])
        g_ref[...] = g
        d_ref[...] = d
        nm_ref[...] = nm
        nv_ref[...] = nv

    blk = pl.BlockSpec((tr, tc), lambda i, j, chip: (i, j))
    part = lambda s: pl.BlockSpec((None, tr, tc), lambda i, j, chip: (s, i, j))
    own = pl.BlockSpec((None, tr, tc), lambda i, j, chip: (chip[0], i, j))
    shape = jax.ShapeDtypeStruct((r, c), F32)
    my_chip = (2 * lax.axis_index("x") + lax.axis_index("y")).astype(jnp.int32).reshape(1)
    return pl.pallas_call(
        body, name=name, out_shape=[shape, shape, shape, shape],
        grid_spec=pltpu.PrefetchScalarGridSpec(num_scalar_prefetch=1, grid=(r // tr, c // tc),
                                               in_specs=[blk, blk, blk, own, part(0), part(1), part(2)],
                                               out_specs=[blk, blk, blk, blk]),
        compiler_params=_params("parallel", "parallel"),
    )(my_chip, w, m, v, sums, recv, recv, recv)


def adamw_small(w, g, m, v):
    def body(w_ref, g_ref, m_ref, v_ref, d_ref, nm_ref, nv_ref):
        d, nm, nv = _adamw(w_ref[...], g_ref[...], m_ref[...], v_ref[...])
        d_ref[...] = d
        nm_ref[...] = nm
        nv_ref[...] = nv

    shape = jax.ShapeDtypeStruct(w.shape, F32)
    return pl.pallas_call(body, name="adamw_small", out_shape=[shape, shape, shape])(w, g, m, v)


def pair_add(name, a, b, tile):
    q, r, c = b.shape
    tr, tc = tile
    assert r % tr == 0 and c % tc == 0

    def body(core_ref, a_ref, b_ref, o_ref):
        o_ref[...] = (a_ref[...].astype(F32) + b_ref[...].astype(F32)).astype(o_ref.dtype)

    blk = pl.BlockSpec((None, tr, tc), lambda s, i, j, core: (s, i, j))
    mine = pl.BlockSpec((None, None, tr, tc), lambda s, i, j, core: (s, core[0], i, j))
    return pl.pallas_call(
        body, name=name, out_shape=jax.ShapeDtypeStruct((q, r, c), BF16),
        grid_spec=pltpu.PrefetchScalarGridSpec(num_scalar_prefetch=1, grid=(q, r // tr, c // tc),
                                               in_specs=[mine, blk], out_specs=blk),
        compiler_params=_params("parallel", "parallel", "parallel"),
    )(lax.axis_index("c").astype(jnp.int32).reshape(1), a, b)


def _w_in_pieces():
    c = W_IN // N_DEV
    ranges = ((0, 8 * HW, 0, 0), (8 * HW, 8 * HW + 2 * HEADS, 1, 8 * HW), (8 * HW + 2 * HEADS, W_IN, 0, 2 * HEADS))
    out = []
    for d in range(N_DEV):
        for lo, hi, target, shift in ranges:
            s, e = max(lo, c * d), min(hi, c * (d + 1))
            if s < e:
                out.append((d, s - c * d, e - s, target, s - shift))
    return out


def _ffn_in_pieces():
    c = 2 * D_FF // N_DEV
    out = []
    for d in range(N_DEV):
        s = c * d
        while s < c * (d + 1):
            e = min((s // FF_BLK + 1) * FF_BLK, c * (d + 1))
            half, blk = divmod(s // FF_BLK, D_FF // FF_BLK)
            out.append((d, s - c * d, e - s, 0, (2 * blk + half) * FF_BLK + s % FF_BLK))
            s = e
    return out


def _plain_pieces(c):
    return [(d, 0, c, 0, c * d) for d in range(N_DEV)]


def shards_to_cols(name, g, widths, pieces, tr):
    _, r, c = g.shape
    covered = [sum(p[2] for p in pieces if p[3] == t) for t in range(len(widths))]

    def body(g_ref, *outs):
        for t, o in enumerate(outs):
            if covered[t] < widths[t]:
                o[...] = jnp.zeros(o.shape, o.dtype)
        for d, s, w, t, ts in pieces:
            outs[t][:, ts:ts + w] = g_ref[d, :, s:s + w]

    return pl.pallas_call(
        body, name=name, grid=(r // tr,),
        in_specs=[pl.BlockSpec((N_DEV, tr, c), lambda i: (0, i, 0))],
        out_specs=[pl.BlockSpec((tr, w), lambda i: (i, 0)) for w in widths],
        out_shape=[jax.ShapeDtypeStruct((r, w), g.dtype) for w in widths],
        compiler_params=_params("parallel"),
    )(g)


def cols_to_shards(name, srcs, c, pieces, tr):
    r = srcs[0].shape[0]

    def body(*refs):
        o = refs[-1]
        for d, s, w, t, ts in pieces:
            o[d, :, s:s + w] = refs[t][:, ts:ts + w]

    return pl.pallas_call(
        body, name=name, grid=(r // tr,),
        in_specs=[pl.BlockSpec((tr, t.shape[1]), lambda i: (i, 0)) for t in srcs],
        out_specs=pl.BlockSpec((N_DEV, tr, c), lambda i: (0, i, 0)),
        out_shape=jax.ShapeDtypeStruct((N_DEV, r, c), srcs[0].dtype),
        compiler_params=_params("parallel"),
    )(*srcs)


def rows_to_shards(name, srcs, c, pieces, tc):
    r = srcs[0].shape[1]

    def body(*refs):
        o = refs[-1]
        for d, s, w, t, ts in pieces:
            o[d, s:s + w, :] = refs[t][ts:ts + w, :]

    return pl.pallas_call(
        body, name=name, grid=(r // tc,),
        in_specs=[pl.BlockSpec((t.shape[0], tc), lambda i: (0, i)) for t in srcs],
        out_specs=pl.BlockSpec((N_DEV, c, tc), lambda i: (0, 0, i)),
        out_shape=jax.ShapeDtypeStruct((N_DEV, c, r), srcs[0].dtype),
        compiler_params=_params("parallel"),
    )(*srcs)


def _place():
    return lax.axis_index("x"), lax.axis_index("y"), lax.axis_index("c")


def _dev(px, py, pc):
    return 4 * px + 2 * py + pc


class Exchange:
    def __init__(self, arrays, out_shapes, sems, start, finish):
        self.arrays, self.out_shapes, self.sems, self.start, self.finish = arrays, out_shapes, sems, start, finish


def gather_exchange(arrays):
    n = len(arrays)

    def plan(ins, outs, sems):
        send_sems, recv_sems, local_sems = sems
        x, y, c = _place()
        me, sibling = (x, y, c), (x, y, 1 - c)
        chips = [(1 - x, y), (x, 1 - y), (1 - x, 1 - y)]

        def copy(a, k, block, to, src=None):
            dst = outs[a].at[_dev(*block)]
            return pltpu.make_async_remote_copy(
                src_ref=dst if src is None else src, dst_ref=dst, send_sem=send_sems.at[a, k],
                recv_sem=recv_sems.at[a, k], device_id=to, device_id_type=MESH)

        mine = [pltpu.make_async_copy(ins[a], outs[a].at[_dev(*me)], local_sems.at[a]) for a in range(n)]
        first = []
        for a in range(n):
            first.append(copy(a, 0, me, sibling, src=ins[a]))
            first += [copy(a, 1 + j, me, (*chip, c), src=ins[a]) for j, chip in enumerate(chips)]
        return me, sibling, chips, c, copy, mine, first

    def start(ins, outs, sems):
        *_, mine, first = plan(ins, outs, sems)
        for cp in mine + first:
            cp.start()

    def finish(ins, outs, sems):
        me, sibling, chips, c, copy, mine, first = plan(ins, outs, sems)
        passed = []
        for j, chip in enumerate(chips):
            for a in range(n):
                copy(a, 1 + j, (*chip, c), me).wait_recv()
                fwd = copy(a, 4 + j, (*chip, c), sibling)
                fwd.start()
                passed.append(fwd)
        for a in range(n):
            copy(a, 0, sibling, me).wait_recv()
        for j, chip in enumerate(chips):
            for a in range(n):
                copy(a, 4 + j, (*chip, 1 - c), me).wait_recv()
        for cp in first + passed:
            cp.wait_send()
        for cp in mine:
            cp.wait()

    return Exchange(arrays, [jax.ShapeDtypeStruct((N_DEV,) + t.shape, t.dtype) for t in arrays],
                    [pltpu.SemaphoreType.DMA((n, 7)), pltpu.SemaphoreType.DMA((n, 7)), pltpu.SemaphoreType.DMA((n,))],
                    start, finish)


def core_exchange(arrays):
    n = len(arrays)

    def copies(ins, outs, sems):
        send_sems, recv_sems = sems
        x, y, c = _place()
        return [pltpu.make_async_remote_copy(
            src_ref=ins[a].at[q, 1 - c], dst_ref=outs[a].at[q], send_sem=send_sems.at[a, q],
            recv_sem=recv_sems.at[a, q], device_id=(x, y, 1 - c), device_id_type=MESH)
            for a in range(n) for q in range(4)]

    def start(ins, outs, sems):
        for cp in copies(ins, outs, sems):
            cp.start()

    def finish(ins, outs, sems):
        for cp in copies(ins, outs, sems):
            cp.wait()

    return Exchange(arrays, [jax.ShapeDtypeStruct((4,) + t.shape[2:], t.dtype) for t in arrays],
                    [pltpu.SemaphoreType.DMA((n, 4)), pltpu.SemaphoreType.DMA((n, 4))], start, finish)


def chips_exchange(arrays):
    n = len(arrays)

    def copies(ins, outs, sems):
        send_sems, recv_sems = sems
        x, y, c = _place()
        chips = [(1 - x, y), (x, 1 - y), (1 - x, 1 - y)]
        return [pltpu.make_async_remote_copy(
            src_ref=ins[a].at[2 * qx + qy], dst_ref=outs[a].at[j], send_sem=send_sems.at[a, j],
            recv_sem=recv_sems.at[a, j], device_id=(qx, qy, c), device_id_type=MESH)
            for a in range(n) for j, (qx, qy) in enumerate(chips)]

    def start(ins, outs, sems):
        for cp in copies(ins, outs, sems):
            cp.start()

    def finish(ins, outs, sems):
        for cp in copies(ins, outs, sems):
            cp.wait()

    return Exchange(arrays, [jax.ShapeDtypeStruct((3,) + t.shape[1:], t.dtype) for t in arrays],
                    [pltpu.SemaphoreType.DMA((n, 3)), pltpu.SemaphoreType.DMA((n, 3))], start, finish)


def run_exchange(name, exch):
    n_in, n_out = len(exch.arrays), len(exch.out_shapes)
    any_spec = pl.BlockSpec(memory_space=pl.ANY)

    def body(*refs):
        ins, outs, sems = refs[:n_in], refs[n_in:n_in + n_out], refs[n_in + n_out:]
        exch.start(ins, outs, sems)
        exch.finish(ins, outs, sems)

    return pl.pallas_call(
        body, name=name, in_specs=[any_spec] * n_in, out_specs=[any_spec] * n_out, out_shape=exch.out_shapes,
        scratch_shapes=exch.sems,
    )(*exch.arrays)


def host_exchange(body, n_in, n_out, exch, first, last):
    ne_in, ne_out, ne_sem = len(exch.arrays), len(exch.out_shapes), len(exch.sems)

    def wrapped(*refs):
        ins, refs = refs[:n_in], refs[n_in:]
        e_ins, refs = refs[:ne_in], refs[ne_in:]
        outs, refs = refs[:n_out], refs[n_out:]
        e_outs, refs = refs[:ne_out], refs[ne_out:]
        scratch, e_sems = refs[:len(refs) - ne_sem], refs[len(refs) - ne_sem:]

        @pl.when(first())
        def _():
            exch.start(e_ins, e_outs, e_sems)

        body(*ins, *outs, *scratch)

        @pl.when(last())
        def _():
            exch.finish(e_ins, e_outs, e_sems)

    return wrapped


def _hosted(exch, in_specs, out_specs, out_shape, scratch_shapes):
    any_spec = pl.BlockSpec(memory_space=pl.ANY)
    return dict(in_specs=list(in_specs) + [any_spec] * len(exch.arrays),
                out_specs=list(out_specs) + [any_spec] * len(exch.out_shapes),
                out_shape=list(out_shape) + list(exch.out_shapes),
                scratch_shapes=list(scratch_shapes) + list(exch.sems))


def allreduce_small(buf):
    r = buf.shape[0]
    vmem = pl.BlockSpec(memory_space=pltpu.VMEM)

    def body(buf_ref, out_ref, all_ref, send_sems, recv_sems):
        x, y, c = _place()
        me = _dev(x, y, c)
        copies = []
        for k in range(1, N_DEV):
            peer = (x ^ (k >> 2), y ^ ((k >> 1) & 1), c ^ (k & 1))
            copies.append(pltpu.make_async_remote_copy(
                src_ref=buf_ref, dst_ref=all_ref.at[me], send_sem=send_sems.at[k - 1], recv_sem=recv_sems.at[k - 1],
                device_id=peer, device_id_type=MESH))
        for cp in copies:
            cp.start()
        all_ref[me] = buf_ref[...]
        for cp in copies:
            cp.wait()
        total = all_ref[0]
        for d in range(1, N_DEV):
            total = total + all_ref[d]
        out_ref[...] = total

    return pl.pallas_call(
        body, name="allreduce_small", in_specs=[vmem], out_specs=vmem,
        out_shape=jax.ShapeDtypeStruct((r, LANES), F32),
        scratch_shapes=[pltpu.VMEM((N_DEV, r, LANES), F32), pltpu.SemaphoreType.DMA((N_DEV - 1,)),
                        pltpu.SemaphoreType.DMA((N_DEV - 1,))],
    )(buf)


def _cols_from_shards(g):
    return jnp.transpose(g, (1, 0, 2)).reshape(g.shape[1], N_DEV * g.shape[2])


def _pack(t):
    flat = t.reshape(-1)
    return jnp.pad(flat, (0, -flat.size % (8 * LANES))).reshape(-1, LANES)


def _pad_lanes(t):
    t = t.reshape(1, -1)
    return jnp.pad(t, ((0, 0), (0, LANES - t.shape[1])))


def kernel(x, meta_tokens, lb_logits, mix_norm_w, w_in, hg_norm_w, gd_conv_w, gd_a_log, gd_dt_bias, gd_norm_w, w_branch_a, w_branch_b, w_out, ffn_norm_w, w_ffn_in, w_ffn_out, final_norm_w, loss_target, m_meta_tokens, m_lb_logits, m_mix_norm_w, m_w_in, m_hg_norm_w, m_gd_conv_w, m_gd_a_log, m_gd_dt_bias, m_gd_norm_w, m_w_branch_a, m_w_branch_b, m_w_out, m_ffn_norm_w, m_w_ffn_in, m_w_ffn_out, m_final_norm_w, v_meta_tokens, v_lb_logits, v_mix_norm_w, v_w_in, v_hg_norm_w, v_gd_conv_w, v_gd_a_log, v_gd_dt_bias, v_gd_norm_w, v_w_branch_a, v_w_branch_b, v_w_out, v_ffn_norm_w, v_w_ffn_in, v_w_ffn_out, v_final_norm_w):
    xs = x[0]
    tgt = loss_target[0]
    l = xs.shape[0]
    lp = PRE + l
    me = _dev(*_place())

    big = [w_in[0], w_branch_a[0], w_branch_b[0], w_out[0], w_ffn_in[0], w_ffn_out[0]]
    big16 = [t.astype(BF16) for t in big]
    g_in, g_meta, g_conv = run_exchange("gather_first", gather_exchange([big16[0], meta_tokens, gd_conv_w[0]]))
    w_main, w_ab = shards_to_cols("w_in_cols", g_in, [W_MAIN, LANES], _w_in_pieces(), 256)
    meta_full = _cols_from_shards(g_meta)
    cw = _cols_from_shards(g_conv)
    pre = jnp.concatenate([jnp.zeros((PRE - N_META, D), F32), meta_full], axis=0)
    l0, l1 = lb_logits[0:1], lb_logits[1:2]
    alog, dtb = _pad_lanes(gd_a_log), _pad_lanes(gd_dt_bias)

    xn = norm1_fwd(pre, xs, mix_norm_w)
    proj, (g_ba, g_bb, g_out, g_ffi, g_ffo) = matmul("proj_main", xn, w_main, "nn", lp, W_MAIN, D, F32,
                                                      exch=gather_exchange(big16[1:]))
    wa, = shards_to_cols("w_branch_a_cols", g_ba, [D], _plain_pieces(D // N_DEV), 256)
    wb, = shards_to_cols("w_branch_b_cols", g_bb, [D], _plain_pieces(D // N_DEV), 256)
    wo = g_out.reshape(D, D)
    w_gu, = shards_to_cols("w_ffn_in_cols", g_ffi, [2 * D_FF], _ffn_in_pieces(), 256)
    w_fo = g_ffo.reshape(D_FF, D)
    proj_ab = matmul("proj_ab", xn, w_ab, "nn", lp, LANES, D, F32)
    o_a, st_a, o_b, st_b, t_inv = mixer_fwd(proj, proj_ab, l0, l1, hg_norm_w, cw, alog, dtb, gd_norm_w)
    ya = matmul("branch_a", o_a, wa, "nn", l, D, HW, F32, a_off=(PRE // 512, 0))
    yb = matmul("branch_b", o_b, wb, "nn", l, D, HW, F32, a_off=(PRE // 512, 0))
    merged = merge_fwd(proj, ya, yb)
    h1, xn2 = mix_out_norm2(merged, wo, xs, ffn_norm_w)
    gu, act = ffn_in_fused(xn2, w_gu)
    f = matmul("ffn_out", act, w_fo, "nn", l, D, D_FF, F32, tn=512)
    loss_part, dh2, dh2_b, d_final_w = loss_head(h1, f, final_norm_w.reshape(1, D), tgt)

    tiles = {"w_in": (W_IN // N_DEV, 256), "w_branch_a": (256, D // N_DEV), "w_branch_b": (256, D // N_DEV),
             "w_out": (64, D), "w_ffn_in": (256, 2 * D_FF // N_DEV), "w_ffn_out": (176, D)}

    def chip_sums(tag, parts, host=None):
        blocks = [p.reshape((4, 2) + p.shape[1:]) for p in parts.values()]
        exch = core_exchange(blocks)
        hosted, from_core = (None, run_exchange("exchange_core_" + tag, exch)) if host is None else host(exch)
        return hosted, {nm: pair_add("pair_add_" + nm, p, r, tiles[nm]) for (nm, p, r) in zip(parts, blocks, from_core)}

    dgu = d_act_fused(dh2_b, w_fo, gu)
    dw_fo = matmul("dw_ffn_out", act, dh2_b, "tn", D_FF, D, l, BF16, tm=512, tn=1024, tk=2048)
    dxn2 = matmul("d_xn2", dgu, w_gu, "nt", l, D, 2 * D_FF, F32, tn=512, tk=D_FF)
    dw_gu = matmul("dw_ffn_in", xn2, dgu, "tn", D, 2 * D_FF, l, BF16, tm=1024, tn=1024, tk=2048)
    dh1, dh1_b, d_ffn_norm_w = norm2_bwd(h1, ffn_norm_w, dxn2, dh2)
    dmerged, sums = chip_sums("ffn", {
        "w_ffn_in": cols_to_shards("dw_ffn_in_shards", [dw_gu], 2 * D_FF // N_DEV, _ffn_in_pieces(), 256),
        "w_ffn_out": dw_fo.reshape(N_DEV, D_FF // N_DEV, D)},
        host=lambda exch: matmul("d_merged", dh1_b, wo, "nt", l, D, D, F32, exch=exch))
    dw_o = matmul("dw_out", merged, dh1_b, "tn", D, D, l, BF16, tm=1024, tn=1024, tk=2048)
    dproj, dya, dyb = merge_bwd(proj, ya, yb, dmerged)
    do_a = matmul("d_o_a", dya, wa, "nt", lp, HW, D, F32)
    do_b = matmul("d_o_b", dyb, wb, "nt", lp, HW, D, F32)
    dw_a = matmul("dw_branch_a", o_a, dya, "tn", HW, D, lp, BF16, tm=1024, tn=1024, tk=lp // 4)
    dw_b = matmul("dw_branch_b", o_b, dyb, "tn", HW, D, lp, BF16, tm=1024, tn=1024, tk=lp // 4)
    sums.update(chip_sums("mix", {
        "w_branch_a": cols_to_shards("dw_branch_a_shards", [dw_a], D // N_DEV, _plain_pieces(D // N_DEV), 256),
        "w_branch_b": cols_to_shards("dw_branch_b_shards", [dw_b], D // N_DEV, _plain_pieces(D // N_DEV), 256),
        "w_out": dw_o.reshape(N_DEV, D // N_DEV, D)})[1])
    ffn_names, mix_names = ["w_ffn_in", "w_ffn_out"], ["w_branch_a", "w_branch_b", "w_out"]
    (dproj, dl0, dl1, d_hg_nw, dproj_ab, d_conv, d_alog, d_dtb, d_gd_nw), from_chips_early = mixer_bwd(
        proj, proj_ab, l0, l1, hg_norm_w, cw, alog, dtb, gd_norm_w, st_a, st_b, t_inv, do_a, do_b, dproj,
        chips_exchange([sums[nm] for nm in ffn_names + mix_names]))
    dwt_main = matmul("dw_in_main", dproj, xn, "tn", W_MAIN, D, lp, BF16, tm=1024, tn=1024, tk=lp // 4)
    dwt_ab = matmul("dw_in_ab", dproj_ab, xn, "tn", LANES, D, lp, BF16, tn=1024, tk=lp // 4)
    dxn_ab, sums_in = chip_sums("w_in", {
        "w_in": rows_to_shards("dw_in_shards", [dwt_main, dwt_ab], W_IN // N_DEV, _w_in_pieces(), 256)},
        host=lambda exch: matmul("d_xn_ab", dproj_ab, w_ab, "nt", lp, D, LANES, F32, exch=exch))
    sums.update(sums_in)
    from_chips = dict(zip(ffn_names + mix_names, from_chips_early))
    dxn, (from_chips["w_in"],) = matmul("d_xn", dproj, w_main, "nt", lp, D, W_MAIN, F32, tn=512, tk=4096,
                                        exch=chips_exchange([sums["w_in"]]))
    grad_x, dpre, d_mix_norm_w = norm1_bwd(pre, xs, mix_norm_w, dxn, dxn_ab, dh1)
    names = ["w_in", "w_branch_a", "w_branch_b", "w_out", "w_ffn_in", "w_ffn_out"]
    moms = [(m_w_in, v_w_in), (m_w_branch_a, v_w_branch_a), (m_w_branch_b, v_w_branch_b), (m_w_out, v_w_out),
            (m_w_ffn_in, v_w_ffn_in), (m_w_ffn_out, v_w_ffn_out)]
    upd = {}
    for nm, w, (m, v) in zip(names, big, moms):
        flip = (lambda t: t.T) if nm == "w_in" else (lambda t: t)
        res = adamw_shard("adamw_" + nm, flip(w), flip(m[0]), flip(v[0]), sums[nm], from_chips[nm], tiles[nm])
        upd[nm] = tuple(flip(t_)[None] for t_ in res)

    d_lb = jnp.concatenate([dl0, dl1], axis=0)
    rep = [_pack(t) for t in (d_lb, d_mix_norm_w, d_hg_nw, d_alog, d_dtb, d_gd_nw, d_ffn_norm_w, d_final_w)]
    n_rep = sum(t.shape[0] for t in rep)
    n_meta = N_META * D // LANES
    tot = allreduce_small(jnp.concatenate(rep + [_pack(loss_part), _pack(dpre[PRE - N_META:]), _pack(d_conv)], axis=0))
    loss = tot[n_rep, 0]
    g_meta_full = tot[n_rep + 8:n_rep + 8 + n_meta].reshape(N_META, D)
    g_conv_full = tot[n_rep + 8 + n_meta:].reshape(CONV_K, 3 * HW)
    g_meta_mine = lax.dynamic_slice_in_dim(g_meta_full, me * (D // N_DEV), D // N_DEV, axis=1)
    g_conv_mine = lax.dynamic_slice_in_dim(g_conv_full, me * (3 * DH), 3 * DH, axis=1)

    small = [("lb_logits", lb_logits, m_lb_logits, v_lb_logits), ("mix_norm_w", mix_norm_w, m_mix_norm_w, v_mix_norm_w),
             ("hg_norm_w", hg_norm_w, m_hg_norm_w, v_hg_norm_w), ("gd_a_log", gd_a_log, m_gd_a_log, v_gd_a_log),
             ("gd_dt_bias", gd_dt_bias, m_gd_dt_bias, v_gd_dt_bias), ("gd_norm_w", gd_norm_w, m_gd_norm_w, v_gd_norm_w),
             ("ffn_norm_w", ffn_norm_w, m_ffn_norm_w, v_ffn_norm_w),
             ("final_norm_w", final_norm_w, m_final_norm_w, v_final_norm_w),
             ("meta_tokens", meta_tokens, m_meta_tokens, v_meta_tokens), ("gd_conv_w", gd_conv_w, m_gd_conv_w, v_gd_conv_w)]

    sizes = [_pack(w).shape[0] for _, w, _, _ in small]
    g_small = jnp.concatenate([tot[:n_rep], _pack(g_meta_mine), _pack(g_conv_mine)], axis=0)
    assert g_small.shape[0] == sum(sizes)
    w_small = jnp.concatenate([_pack(w) for _, w, _, _ in small], axis=0)
    m_small = jnp.concatenate([_pack(m) for _, _, m, _ in small], axis=0)
    v_small = jnp.concatenate([_pack(v) for _, _, _, v in small], axis=0)
    d_small, nm_small, nv_small = adamw_small(w_small, g_small, m_small, v_small)
    row = 0
    for (nm, w, _, _), sz in zip(small, sizes):
        def unpack(t):
            return t[row:row + sz].reshape(-1)[:w.size].reshape(w.shape)
        upd[nm] = (unpack(g_small), unpack(d_small), unpack(nm_small), unpack(nv_small))
        row += sz

    order = ["meta_tokens", "lb_logits", "mix_norm_w", "w_in", "hg_norm_w", "gd_conv_w", "gd_a_log", "gd_dt_bias",
             "gd_norm_w", "w_branch_a", "w_branch_b", "w_out", "ffn_norm_w", "w_ffn_in", "w_ffn_out", "final_norm_w"]
    outs = [loss, grad_x[None]]
    for j in range(4):
        outs += [upd[nm][j] for nm in order]
    return tuple(outs)
```

```python
import functools

import jax
import jax.numpy as jnp
from jax import lax
from jax.experimental import pallas as pl
from jax.experimental.pallas import tpu as pltpu

F32 = jnp.float32
BF16 = jnp.bfloat16
MESH = pl.DeviceIdType.MESH

EPS = 1e-6
D = 2048
N_META = 16
CHUNK = 64
SUB = 16
HEADS = 8
DH = 128
HW = HEADS * DH
CONV_K = 4
TAIL = 8
D_FF = 5632
PRE = 512
N_SKIP = PRE // CHUNK - 1
N_DEV = 8
LANES = 128
FF_BLK = 512
W_IN = 4 * HW + 4 * HW + 2 * HEADS + 2 * D
W_MAIN = 4 * HW + 4 * HW + 2 * D

ADAM_LR = 0.001
ADAM_B1 = 0.9
ADAM_B2 = 0.999
ADAM_EPS = 1e-08
ADAM_WD = 0.01
ADAM_STEP = 10

VMEM_LIMIT = 52 * 1024 * 1024

_NN = ((1,), (0,))
_NT = ((1,), (1,))
_TN = ((0,), (0,))


def _params(*sem):
    return pltpu.CompilerParams(dimension_semantics=sem, vmem_limit_bytes=VMEM_LIMIT)


BF16_ONCE, SPLIT, EXACT_LHS = 0, 1, 2


def _dot_bf16(a, b, dims):
    if a.ndim == 3:
        dn = (((dims[0][0] + 1,), (dims[1][0] + 1,)), ((0,), (0,)))
    else:
        dn = (dims, ((), ()))
    return lax.dot_general(a, b, dn, preferred_element_type=F32)


def _split(t):
    hi = t.astype(BF16)
    return hi, (t - hi.astype(F32)).astype(BF16)


def _raw_dot(a, b, dims, mode):
    if mode == BF16_ONCE:
        return _dot_bf16(a.astype(BF16), b.astype(BF16), dims)
    if mode == SPLIT:
        a_hi, a_lo = _split(a)
        b_hi, b_lo = _split(b)
        return _dot_bf16(a_hi, b_hi, dims) + (_dot_bf16(a_hi, b_lo, dims) + _dot_bf16(a_lo, b_hi, dims))
    a = a.astype(BF16)
    b_hi = b.astype(BF16)
    rest = b - b_hi.astype(F32)
    b_mid = rest.astype(BF16)
    b_lo = (rest - b_mid.astype(F32)).astype(BF16)
    return _dot_bf16(a, b_hi, dims) + (_dot_bf16(a, b_mid, dims) + _dot_bf16(a, b_lo, dims))


@functools.partial(jax.custom_vjp, nondiff_argnums=(2,))
def mm_nn(a, b, mode=BF16_ONCE):
    return _raw_dot(a, b, _NN, mode)


@functools.partial(jax.custom_vjp, nondiff_argnums=(2,))
def mm_nt(a, b, mode=BF16_ONCE):
    return _raw_dot(a, b, _NT, mode)


@functools.partial(jax.custom_vjp, nondiff_argnums=(2,))
def mm_tn(a, b, mode=BF16_ONCE):
    return _raw_dot(a, b, _TN, mode)


def _general(mode):
    return SPLIT if mode == EXACT_LHS else mode


mm_nn.defvjp(lambda a, b, p: (_raw_dot(a, b, _NN, p), (a, b)),
             lambda p, res, g: (mm_nt(g, res[1], _general(p)), mm_tn(res[0], g, p)))
mm_nt.defvjp(lambda a, b, p: (_raw_dot(a, b, _NT, p), (a, b)),
             lambda p, res, g: (mm_nn(g, res[1], p), mm_tn(g, res[0], p)))
mm_tn.defvjp(lambda a, b, p: (_raw_dot(a, b, _TN, p), (a, b)),
             lambda p, res, g: (mm_nt(res[1], g, _general(p)), mm_nn(res[0], g, _general(p))))


def _sigmoid(x):
    return jax.nn.sigmoid(x)


def _silu(x):
    return x * _sigmoid(x)


def _softplus(x):
    return jnp.maximum(x, 0.0) + jnp.log(1.0 + jnp.exp(-jnp.abs(x)))


def _l2n(t):
    return t * lax.rsqrt(jnp.sum(t * t, axis=-1, keepdims=True) + EPS)


def _rms_norm(x, w):
    return x * lax.rsqrt(jnp.mean(x * x, axis=-1, keepdims=True) + EPS) * w


def _gated_norm(o, gate, nw):
    o = o * lax.rsqrt(jnp.mean(o * o, axis=-1, keepdims=True) + EPS) * nw
    return o * _silu(gate)


def _heads(ref, piece):
    return jnp.stack([ref[:, piece * HW + DH * j:piece * HW + DH * (j + 1)] for j in range(HEADS)])


def _put_heads(ref, piece, value, accumulate=False):
    for j in range(HEADS):
        cols = slice(piece * HW + DH * j, piece * HW + DH * (j + 1))
        if accumulate:
            ref[:, cols] += value[j]
        else:
            ref[:, cols] = value[j].astype(ref.dtype)


def _diag_scores(q_i, k_i, f_i):
    nb, s, d = q_i.shape
    row = lax.broadcasted_iota(jnp.int32, (nb, s, d), 1)
    q4, f4 = q_i[:, :, None, :], f_i[:, :, None, :]
    kk = k_i
    rows_out = []
    for t in range(s):
        if t > 0:
            kk = jnp.where(row < t, kk * f4[:, t], kk)
        rows_out.append(jnp.sum(kk * q4[:, t], axis=-1)[:, None, :])
    s_ii = jnp.concatenate(rows_out, axis=1)
    tri = lax.broadcasted_iota(jnp.int32, (nb, s, s), 1) >= lax.broadcasted_iota(jnp.int32, (nb, s, s), 2)
    return jnp.where(tri, s_ii, 0.0)


def hg_chunk(hq, hf, hi, hg, l0, l1, nw, st):
    nb = hq.shape[0]
    m = jnp.maximum(l0, l1)
    e0 = jnp.exp(l0 - m)
    e1 = jnp.exp(l1 - m)
    lb = e0 / (e0 + e1)
    sig = _sigmoid(hf)
    q = _silu(hq)
    f = lb + (1.0 - lb) * sig
    log_f = jnp.log(f)
    k = (1.0 - lb) * _sigmoid(-hf)
    v = hi
    rows = lax.broadcasted_iota(jnp.int32, (nb, CHUNK, CHUNK), 1)
    cols = lax.broadcasted_iota(jnp.int32, (nb, CHUNK, CHUNK), 2)
    b = mm_nn((rows >= cols).astype(F32), log_f, EXACT_LHS)
    o_inter = mm_nt(q * jnp.exp(b), st)
    outs = []
    for i in range(CHUNK // SUB):
        lo = i * SUB
        b_i, q_i, k_i, v_i = b[:, lo:lo + SUB], q[:, lo:lo + SUB], k[:, lo:lo + SUB], v[:, lo:lo + SUB]
        o_i = mm_nn(_diag_scores(q_i, k_i, f[:, lo:lo + SUB]), v_i)
        if i > 0:
            b_ref = b[:, :, None, :][:, lo - 1]
            q_t = q_i * jnp.exp(b_i - b_ref)
            k_t = k[:, :lo] * jnp.exp(b_ref - b[:, :lo])
            o_i = o_i + mm_nn(mm_nt(q_t, k_t), v[:, :lo])
        outs.append(o_i)
    o = o_inter + jnp.concatenate(outs, axis=1)
    b_last = b[:, :, None, :][:, CHUNK - 1]
    st_new = st * jnp.exp(b_last) + mm_tn(v, k * jnp.exp(b_last - b))
    return _gated_norm(o, hg, nw), st_new


@jax.custom_vjp
def _unit_lower_inverse(a):
    n = a.shape[-1]
    eye = (lax.broadcasted_iota(jnp.int32, a.shape, 1) == lax.broadcasted_iota(jnp.int32, a.shape, 2)).astype(F32)
    x = eye - a
    p = mm_nn(a, a, SPLIT)
    x = x + mm_nn(x, p, SPLIT)
    power = 4
    while power < n:
        p = mm_nn(p, p, SPLIT)
        x = x + mm_nn(x, p, SPLIT)
        power *= 2
    return x


def _unit_lower_inverse_fwd(a):
    t = _unit_lower_inverse(a)
    return t, t


def _unit_lower_inverse_bwd(t, dt):
    return (-mm_tn(t, mm_nt(dt, t, SPLIT), SPLIT),)


_unit_lower_inverse.defvjp(_unit_lower_inverse_fwd, _unit_lower_inverse_bwd)


@jax.custom_vjp
def _kept_inverse(a, t):
    return t


_kept_inverse.defvjp(lambda a, t: (t, t),
                     lambda t, dt: (-mm_tn(t, mm_nt(dt, t, SPLIT), SPLIT), jnp.zeros_like(t)))


def gd_chunk(cq, ck, cv, z, ab, alog, dtb, nw, s, kept_t=None, keep_t=False):
    nb, c, _ = cq.shape
    lane = lax.broadcasted_iota(jnp.int32, (nb, 1, DH), 2)
    head = lax.broadcasted_iota(jnp.int32, (nb, 1, DH), 0)
    pick = lambda t, off: jnp.sum(jnp.where(lane == head + off, t[None], 0.0), axis=2, keepdims=True)
    a_raw, b_raw = pick(ab, 0), pick(ab, HEADS)
    beta = _sigmoid(b_raw)
    g = -jnp.exp(pick(alog, 0)) * _softplus(a_raw + pick(dtb, 0))
    q = _l2n(_silu(cq)) * (DH ** -0.5)
    k = _l2n(_silu(ck))
    v = _silu(cv)
    rows = lax.broadcasted_iota(jnp.int32, (nb, c, c), 1)
    cols = lax.broadcasted_iota(jnp.int32, (nb, c, c), 2)
    tril = (rows >= cols).astype(F32)
    gc_w = mm_nn(tril, jnp.broadcast_to(g, (nb, c, DH)), EXACT_LHS)
    g_sq = jnp.broadcast_to(g, (nb, c, c))
    gc_col = mm_nn(tril, g_sq, EXACT_LHS)
    gc_row = mm_nn(jnp.ones((nb, c, c), F32), g_sq * (rows <= cols).astype(F32), EXACT_LHS)
    rel = jnp.exp(jnp.minimum(gc_col - gc_row, 0.0))
    a = jnp.where(rows > cols, beta * mm_nt(k, k) * rel, 0.0)
    t_inv = _unit_lower_inverse(a) if kept_t is None else _kept_inverse(a, kept_t)
    e_gc = jnp.exp(gc_w)
    w = mm_nn(t_inv, beta * e_gc * k, SPLIT)
    u = mm_nn(t_inv, beta * v, SPLIT)
    v_new = u - mm_nn(w, s)
    attn = jnp.where(rows >= cols, mm_nt(q, k) * rel, 0.0)
    o = mm_nn(q * e_gc, s) + mm_nn(attn, v_new)
    g_last = jnp.sum(g, axis=1, keepdims=True)
    s_new = jnp.exp(g_last) * s + mm_tn(k * jnp.exp(g_last - gc_w), v_new)
    out = _gated_norm(o, z, nw)
    return (out, s_new, t_inv) if keep_t else (out, s_new)


def _gd_conv(x_ref, tail_ref, cw_ref, xe_ref, first):
    xe_ref[0:TAIL, :] = jnp.where(first, 0.0, tail_ref[:, 0:3 * HW])
    xe_ref[TAIL:TAIL + CHUNK, :] = x_ref[:, 0:3 * HW]
    y = cw_ref[pl.ds(0, 1), :] * xe_ref[pl.ds(TAIL - CONV_K + 1, CHUNK), :]
    for k in range(1, CONV_K):
        y = y + cw_ref[pl.ds(k, 1), :] * xe_ref[pl.ds(TAIL - CONV_K + 1 + k, CHUNK), :]
    return y


def _conv_heads(y, piece):
    return jnp.stack([y[:, piece * HW + DH * j:piece * HW + DH * (j + 1)] for j in range(HEADS)])


def mixer_fwd(proj, proj_ab, l0, l1, hg_nw, cw, alog, dtb, gd_nw):
    lp = proj.shape[0]
    nc = lp // CHUNK

    def body(xh_ref, l0_ref, l1_ref, hnw_ref, xg_ref, tail_ref, ab_ref, cw_ref, alog_ref, dtb_ref, gnw_ref,
             oa_ref, sta_out_ref, ob_ref, stb_out_ref, tinv_ref, sta_ref, stb_ref, xe_ref):
        c = pl.program_id(0)

        @pl.when(c == 0)
        def _():
            sta_ref[...] = jnp.zeros(sta_ref.shape, F32)
            stb_ref[...] = jnp.zeros(stb_ref.shape, F32)

        @pl.when(c < N_SKIP)
        def _():
            for ref in (oa_ref, sta_out_ref, ob_ref, stb_out_ref, tinv_ref):
                ref[...] = jnp.zeros(ref.shape, ref.dtype)

        @pl.when(c >= N_SKIP)
        def _():
            stb = stb_ref[...]
            stb_out_ref[0] = stb
            y = _gd_conv(xg_ref, tail_ref, cw_ref, xe_ref, c == 0)
            ob, stb_new, tinv_ref[0] = gd_chunk(
                _conv_heads(y, 0), _conv_heads(y, 1), _conv_heads(y, 2), _heads(xg_ref, 3),
                ab_ref[...], alog_ref[...], dtb_ref[...], gnw_ref[...], stb, keep_t=True)
            _put_heads(ob_ref, 0, ob)
            stb_ref[...] = stb_new
            sta = sta_ref[...]
            sta_out_ref[0] = sta
            oa, sta_new = hg_chunk(_heads(xh_ref, 0), _heads(xh_ref, 1), _heads(xh_ref, 2), _heads(xh_ref, 3),
                                   _heads(l0_ref, 0), _heads(l1_ref, 0), hnw_ref[...], sta)
            _put_heads(oa_ref, 0, oa)
            sta_ref[...] = sta_new

    vec = pl.BlockSpec((1, HW), lambda c: (0, 0))
    one = pl.BlockSpec((1, DH), lambda c: (0, 0))
    st_spec = pl.BlockSpec((1, HEADS, DH, DH), lambda c: (c, 0, 0, 0))
    o_spec = pl.BlockSpec((CHUNK, HW), lambda c: (c, 0))
    o_shape = jax.ShapeDtypeStruct((lp, HW), BF16)
    st_shape = jax.ShapeDtypeStruct((nc, HEADS, DH, DH), F32)
    return pl.pallas_call(
        body, name="mixer_fwd", grid=(nc,),
        in_specs=[pl.BlockSpec((CHUNK, 4 * HW), lambda c: (c, 0)), vec, vec, one,
                  pl.BlockSpec((CHUNK, 4 * HW), lambda c: (c, 1)),
                  pl.BlockSpec((TAIL, 4 * HW), lambda c: (jnp.maximum(c * (CHUNK // TAIL) - 1, 0), 1)),
                  pl.BlockSpec((CHUNK, DH), lambda c: (c, 0)),
                  pl.BlockSpec((CONV_K, 3 * HW), lambda c: (0, 0)), one, one, one],
        out_specs=[o_spec, st_spec, o_spec, st_spec,
                   pl.BlockSpec((1, HEADS, CHUNK, CHUNK), lambda c: (c, 0, 0, 0))],
        out_shape=[o_shape, st_shape, o_shape, st_shape, jax.ShapeDtypeStruct((nc, HEADS, CHUNK, CHUNK), F32)],
        scratch_shapes=[pltpu.VMEM((HEADS, DH, DH), F32), pltpu.VMEM((HEADS, DH, DH), F32),
                        pltpu.VMEM((TAIL + CHUNK, 3 * HW), F32)],
        compiler_params=_params("arbitrary"),
    )(proj, l0, l1, hg_nw, proj, proj, proj_ab, cw, alog, dtb, gd_nw)


def mixer_bwd(proj, proj_ab, l0, l1, hg_nw, cw, alog, dtb, gd_nw, st_a, st_b, t_inv, do_a, do_b, dproj, exch):
    lp = proj.shape[0]
    nc = lp // CHUNK

    def body(xh_ref, l0_ref, l1_ref, hnw_ref, xg_ref, tail_ref, ab_ref, cw_ref, alog_ref, dtb_ref, gnw_ref,
             sta_in_ref, stb_in_ref, tinv_ref, doa_ref, dob_ref, _,
             dx_ref, dl0_ref, dl1_ref, dhnw_ref, dab_ref, dcw_ref, dalog_ref, ddtb_ref, dgnw_ref,
             dsta_ref, dstb_ref, xe_ref, dye_ref, head_ref):
        c = pl.program_id(0)
        cc = nc - 1 - c

        @pl.when(c == 0)
        def _():
            for ref in (dl0_ref, dl1_ref, dhnw_ref, dcw_ref, dalog_ref, ddtb_ref, dgnw_ref, dsta_ref, dstb_ref,
                        head_ref):
                ref[...] = jnp.zeros(ref.shape, F32)

        @pl.when(cc < N_SKIP)
        def _():
            dx_ref[...] = jnp.zeros(dx_ref.shape, dx_ref.dtype)
            dab_ref[...] = jnp.zeros(dab_ref.shape, F32)

        @pl.when(cc >= N_SKIP)
        def _():
            y = _gd_conv(xg_ref, tail_ref, cw_ref, xe_ref, cc == 0)
            _, gd_vjp = jax.vjp(functools.partial(gd_chunk, kept_t=tinv_ref[0]),
                                _conv_heads(y, 0), _conv_heads(y, 1), _conv_heads(y, 2), _heads(xg_ref, 3),
                                ab_ref[...], alog_ref[...], dtb_ref[...], gnw_ref[...], stb_in_ref[0])
            dcq, dck, dcv, dz, dab, dalog, ddtb, dgnw, dstb = gd_vjp((_heads(dob_ref, 0), dstb_ref[...]))
            _, hg_vjp = jax.vjp(hg_chunk, _heads(xh_ref, 0), _heads(xh_ref, 1), _heads(xh_ref, 2), _heads(xh_ref, 3),
                                _heads(l0_ref, 0), _heads(l1_ref, 0), hnw_ref[...], sta_in_ref[0])
            hg_grads = hg_vjp((_heads(doa_ref, 0), dsta_ref[...]))
            dstb_ref[...] = dstb
            dab_ref[...] = dab
            dalog_ref[...] += dalog
            ddtb_ref[...] += ddtb
            dgnw_ref[...] += dgnw
            _put_heads(dx_ref, 4 + 3, dz)
            for i, t in enumerate((dcq, dck, dcv)):
                for j in range(HEADS):
                    dye_ref[0:CHUNK, i * HW + DH * j:i * HW + DH * (j + 1)] = t[j]
            dye_ref[CHUNK:CHUNK + TAIL, :] = head_ref[...]
            head_ref[...] = dye_ref[0:TAIL, :]
            dy = dye_ref[0:CHUNK, :]
            dx = None
            for k in range(CONV_K):
                term = cw_ref[pl.ds(k, 1), :] * dye_ref[pl.ds(CONV_K - 1 - k, CHUNK), :]
                dx = term if dx is None else dx + term
                dcw_ref[pl.ds(k, 1), :] += jnp.sum(dy * xe_ref[pl.ds(TAIL - CONV_K + 1 + k, CHUNK), :],
                                                   axis=0, keepdims=True)
            dx_ref[:, 4 * HW:7 * HW] = dx.astype(dx_ref.dtype)
            for i in range(4):
                _put_heads(dx_ref, i, hg_grads[i])
            _put_heads(dl0_ref, 0, hg_grads[4], accumulate=True)
            _put_heads(dl1_ref, 0, hg_grads[5], accumulate=True)
            dhnw_ref[...] += hg_grads[6]
            dsta_ref[...] = hg_grads[7]

    rc = lambda c: nc - 1 - c
    vec = pl.BlockSpec((1, HW), lambda c: (0, 0))
    one = pl.BlockSpec((1, DH), lambda c: (0, 0))
    one_shape = jax.ShapeDtypeStruct((1, DH), F32)
    vec_shape = jax.ShapeDtypeStruct((1, HW), F32)
    st_spec = pl.BlockSpec((1, HEADS, DH, DH), lambda c: (rc(c), 0, 0, 0))
    do_spec = pl.BlockSpec((CHUNK, HW), lambda c: (rc(c), 0))
    n_in, n_out = 17, 9
    outs = pl.pallas_call(
        host_exchange(body, n_in, n_out, exch, lambda: pl.program_id(0) == 0, lambda: pl.program_id(0) == nc - 1),
        name="mixer_bwd", grid=(nc,), input_output_aliases={n_in - 1: 0}, compiler_params=_params("arbitrary"),
        **_hosted(exch,
                  [pl.BlockSpec((CHUNK, 4 * HW), lambda c: (rc(c), 0)), vec, vec, one,
                   pl.BlockSpec((CHUNK, 4 * HW), lambda c: (rc(c), 1)),
                   pl.BlockSpec((TAIL, 4 * HW), lambda c: (jnp.maximum(rc(c) * (CHUNK // TAIL) - 1, 0), 1)),
                   pl.BlockSpec((CHUNK, DH), lambda c: (rc(c), 0)),
                   pl.BlockSpec((CONV_K, 3 * HW), lambda c: (0, 0)), one, one, one,
                   st_spec, st_spec, pl.BlockSpec((1, HEADS, CHUNK, CHUNK), lambda c: (rc(c), 0, 0, 0)),
                   do_spec, do_spec, pl.BlockSpec(memory_space=pl.ANY)],
                  [pl.BlockSpec((CHUNK, 8 * HW), lambda c: (rc(c), 0)), vec, vec, one,
                   pl.BlockSpec((CHUNK, DH), lambda c: (rc(c), 0)),
                   pl.BlockSpec((CONV_K, 3 * HW), lambda c: (0, 0)), one, one, one],
                  [jax.ShapeDtypeStruct((lp, W_MAIN), BF16), vec_shape, vec_shape, one_shape,
                   jax.ShapeDtypeStruct((lp, DH), F32), jax.ShapeDtypeStruct((CONV_K, 3 * HW), F32),
                   one_shape, one_shape, one_shape],
                  [pltpu.VMEM((HEADS, DH, DH), F32), pltpu.VMEM((HEADS, DH, DH), F32),
                   pltpu.VMEM((TAIL + CHUNK, 3 * HW), F32), pltpu.VMEM((CHUNK + TAIL, 3 * HW), F32),
                   pltpu.VMEM((TAIL, 3 * HW), F32)]),
    )(proj, l0, l1, hg_nw, proj, proj, proj_ab, cw, alog, dtb, gd_nw, st_a, st_b, t_inv, do_a, do_b, dproj,
      *exch.arrays)
    return outs[:n_out], outs[n_out:]


def matmul(name, a, b, mode, m, n, k, out_dtype, tm=512, tn=1024, tk=None, a_off=(0, 0), b_off=(0, 0), exch=None):
    tm, tn = min(tm, m), min(tn, n)
    tk = k if tk is None else min(tk, k)
    assert m % tm == 0 and n % tn == 0 and k % tk == 0, (name, m, n, k, tm, tn, tk)
    gi, gj, gk = m // tm, n // tn, k // tk
    if mode == "nn":
        a_spec = pl.BlockSpec((tm, tk), lambda j, i, kk: (i + a_off[0], kk + a_off[1]))
        b_spec = pl.BlockSpec((tk, tn), lambda j, i, kk: (kk + b_off[0], j + b_off[1]))
        dims = _NN
    elif mode == "nt":
        a_spec = pl.BlockSpec((tm, tk), lambda j, i, kk: (i + a_off[0], kk + a_off[1]))
        b_spec = pl.BlockSpec((tn, tk), lambda j, i, kk: (j + b_off[0], kk + b_off[1]))
        dims = _NT
    else:
        a_spec = pl.BlockSpec((tk, tm), lambda j, i, kk: (kk + a_off[0], i + a_off[1]))
        b_spec = pl.BlockSpec((tk, tn), lambda j, i, kk: (kk + b_off[0], j + b_off[1]))
        dims = _TN

    def body(a_ref, b_ref, o_ref, *acc):
        d = lax.dot_general(a_ref[...].astype(BF16), b_ref[...].astype(BF16), (dims, ((), ())),
                            preferred_element_type=F32)
        if gk == 1:
            o_ref[...] = d.astype(o_ref.dtype)
        else:
            acc_ref, = acc
            kk = pl.program_id(2)

            @pl.when(kk == 0)
            def _():
                acc_ref[...] = d

            @pl.when(kk > 0)
            def _():
                acc_ref[...] += d

            @pl.when(kk == gk - 1)
            def _():
                o_ref[...] = acc_ref[...].astype(o_ref.dtype)

    o_spec = pl.BlockSpec((tm, tn), lambda j, i, kk: (i, j))
    o_shape = jax.ShapeDtypeStruct((m, n), out_dtype)
    scratch = [] if gk == 1 else [pltpu.VMEM((tm, tn), F32)]
    if exch is None:
        return pl.pallas_call(
            body, name=name, grid=(gj, gi, gk), in_specs=[a_spec, b_spec], out_specs=o_spec, out_shape=o_shape,
            scratch_shapes=scratch, compiler_params=_params("parallel", "parallel", "arbitrary"),
        )(a, b)
    step = lambda: (pl.program_id(0), pl.program_id(1), pl.program_id(2))
    first = lambda: jnp.logical_and(jnp.logical_and(step()[0] == 0, step()[1] == 0), step()[2] == 0)
    last = lambda: jnp.logical_and(jnp.logical_and(step()[0] == gj - 1, step()[1] == gi - 1), step()[2] == gk - 1)
    outs = pl.pallas_call(
        host_exchange(body, 2, 1, exch, first, last), name=name, grid=(gj, gi, gk),
        compiler_params=_params("arbitrary", "arbitrary", "arbitrary"),
        **_hosted(exch, [a_spec, b_spec], [o_spec], [o_shape], scratch),
    )(a, b, *exch.arrays)
    return outs[0], outs[1:]


def _swiglu(gu):
    return _silu(gu[:, :FF_BLK]) * gu[:, FF_BLK:]


def ffn_in_fused(xn2, w_gu, tm=512):
    l = xn2.shape[0]
    tn = 2 * FF_BLK

    def body(a_ref, b_ref, gu_ref, act_ref):
        gu = lax.dot_general(a_ref[...], b_ref[...], (_NN, ((), ())), preferred_element_type=F32)
        gu_ref[...] = gu
        act_ref[...] = _swiglu(gu).astype(act_ref.dtype)

    return pl.pallas_call(
        body, name="ffn_in", grid=(2 * D_FF // tn, l // tm),
        in_specs=[pl.BlockSpec((tm, D), lambda j, i: (i, 0)), pl.BlockSpec((D, tn), lambda j, i: (0, j))],
        out_specs=[pl.BlockSpec((tm, tn), lambda j, i: (i, j)), pl.BlockSpec((tm, FF_BLK), lambda j, i: (i, j))],
        out_shape=[jax.ShapeDtypeStruct((l, 2 * D_FF), F32), jax.ShapeDtypeStruct((l, D_FF), BF16)],
        compiler_params=_params("parallel", "parallel"),
    )(xn2, w_gu)


def d_act_fused(dh2, w_fo, gu, tm=512):
    l = dh2.shape[0]

    def body(a_ref, b_ref, gu_ref, o_ref):
        da = lax.dot_general(a_ref[...], b_ref[...], (_NT, ((), ())), preferred_element_type=F32)
        _, vjp = jax.vjp(_swiglu, gu_ref[...])
        dgu, = vjp(da)
        o_ref[...] = dgu.astype(o_ref.dtype)

    return pl.pallas_call(
        body, name="d_act", grid=(D_FF // FF_BLK, l // tm),
        in_specs=[pl.BlockSpec((tm, D), lambda j, i: (i, 0)), pl.BlockSpec((FF_BLK, D), lambda j, i: (j, 0)),
                  pl.BlockSpec((tm, 2 * FF_BLK), lambda j, i: (i, j))],
        out_specs=pl.BlockSpec((tm, 2 * FF_BLK), lambda j, i: (i, j)),
        out_shape=jax.ShapeDtypeStruct((l, 2 * D_FF), BF16),
        compiler_params=_params("parallel", "parallel"),
    )(dh2, w_fo, gu)


TR = 256
N_PRE = PRE // TR


def _row(width, off=0, col=0):
    return pl.BlockSpec((TR, width), lambda i: (i + off, col))


def _pre_spec():
    return pl.BlockSpec((TR, D), lambda i: (jnp.minimum(i, N_PRE - 1), 0))


def _seq_spec(width=D, col=0):
    return pl.BlockSpec((TR, width), lambda i: (jnp.maximum(i - N_PRE, 0), col))


def _vec_spec(width=D):
    return pl.BlockSpec((1, width), lambda i: (0, 0))


def norm1_fwd(pre, x, w):
    lp = PRE + x.shape[0]

    def body(pre_ref, x_ref, w_ref, o_ref):
        h = jnp.where(pl.program_id(0) < N_PRE, pre_ref[...], x_ref[...])
        o_ref[...] = _rms_norm(h, w_ref[...]).astype(o_ref.dtype)

    return pl.pallas_call(
        body, name="norm1_fwd", grid=(lp // TR,),
        in_specs=[_pre_spec(), _seq_spec(), _vec_spec()],
        out_specs=_row(D), out_shape=jax.ShapeDtypeStruct((lp, D), BF16),
        compiler_params=_params("parallel"),
    )(pre, x, w)


def norm1_bwd(pre, x, w, dxn, dxn_ab, dh1):
    l = x.shape[0]
    lp = PRE + l

    def body(pre_ref, x_ref, w_ref, dxn_ref, dxn_ab_ref, dh1_ref, dx_ref, dpre_ref, dw_ref):
        i = pl.program_id(0)
        h = jnp.where(i < N_PRE, pre_ref[...], x_ref[...])
        _, vjp = jax.vjp(_rms_norm, h, w_ref[...])
        dh, dw = vjp(dxn_ref[...] + dxn_ab_ref[...])

        @pl.when(i == 0)
        def _():
            dw_ref[...] = dw

        @pl.when(i > 0)
        def _():
            dw_ref[...] += dw

        @pl.when(i < N_PRE)
        def _():
            dpre_ref[...] = dh

        @pl.when(i >= N_PRE)
        def _():
            dx_ref[...] = dh + dh1_ref[...]

    return pl.pallas_call(
        body, name="norm1_bwd", grid=(lp // TR,),
        in_specs=[_pre_spec(), _seq_spec(), _vec_spec(), _row(D), _row(D), _seq_spec()],
        out_specs=[_seq_spec(), _pre_spec(), _vec_spec()],
        out_shape=[jax.ShapeDtypeStruct((l, D), F32), jax.ShapeDtypeStruct((PRE, D), F32),
                   jax.ShapeDtypeStruct((1, D), F32)],
        compiler_params=_params("arbitrary"),
    )(pre, x, w, dxn, dxn_ab, dh1)


def _merge(gates, ya, yb):
    return _sigmoid(gates[:, :D]) * ya + _sigmoid(gates[:, D:]) * yb


def merge_fwd(proj, ya, yb):
    l = ya.shape[0]

    def body(g_ref, ya_ref, yb_ref, o_ref):
        o_ref[...] = _merge(g_ref[...], ya_ref[...], yb_ref[...]).astype(o_ref.dtype)

    return pl.pallas_call(
        body, name="merge_fwd", grid=(l // TR,),
        in_specs=[_row(2 * D, N_PRE, 2), _row(D), _row(D)],
        out_specs=_row(D), out_shape=jax.ShapeDtypeStruct((l, D), BF16),
        compiler_params=_params("parallel"),
    )(proj, ya, yb)


def merge_bwd(proj, ya, yb, dmerged):
    l = ya.shape[0]
    lp = PRE + l

    def body(g_ref, ya_ref, yb_ref, dm_ref, dg_ref, dya_ref, dyb_ref):
        i = pl.program_id(0)

        @pl.when(i < N_PRE)
        def _():
            dg_ref[...] = jnp.zeros(dg_ref.shape, dg_ref.dtype)
            dya_ref[...] = jnp.zeros(dya_ref.shape, dya_ref.dtype)
            dyb_ref[...] = jnp.zeros(dyb_ref.shape, dyb_ref.dtype)

        @pl.when(i >= N_PRE)
        def _():
            _, vjp = jax.vjp(_merge, g_ref[...], ya_ref[...], yb_ref[...])
            dg, dya, dyb = vjp(dm_ref[...])
            dg_ref[...] = dg.astype(dg_ref.dtype)
            dya_ref[...] = dya.astype(dya_ref.dtype)
            dyb_ref[...] = dyb.astype(dyb_ref.dtype)

    return pl.pallas_call(
        body, name="merge_bwd", grid=(lp // TR,),
        in_specs=[pl.BlockSpec((TR, 2 * D), lambda i: (jnp.maximum(i, N_PRE), 2)), _seq_spec(), _seq_spec(),
                  _seq_spec()],
        out_specs=[_row(2 * D, 0, 2), _row(D), _row(D)],
        out_shape=[jax.ShapeDtypeStruct((lp, W_MAIN), BF16), jax.ShapeDtypeStruct((lp, D), BF16),
                   jax.ShapeDtypeStruct((lp, D), BF16)],
        compiler_params=_params("parallel"),
    )(proj, ya, yb, dmerged)


def mix_out_norm2(merged, wo, x, w, tm=512):
    l = x.shape[0]

    def body(a_ref, b_ref, x_ref, w_ref, h_ref, o_ref):
        h = x_ref[...] + lax.dot_general(a_ref[...], b_ref[...], (_NN, ((), ())), preferred_element_type=F32)
        h_ref[...] = h
        o_ref[...] = _rms_norm(h, w_ref[...]).astype(o_ref.dtype)

    row = pl.BlockSpec((tm, D), lambda i: (i, 0))
    return pl.pallas_call(
        body, name="mix_out", grid=(l // tm,),
        in_specs=[row, pl.BlockSpec((D, D), lambda i: (0, 0)), row, _vec_spec()],
        out_specs=[row, row],
        out_shape=[jax.ShapeDtypeStruct((l, D), F32), jax.ShapeDtypeStruct((l, D), BF16)],
        compiler_params=_params("parallel"),
    )(merged, wo, x, w)


def norm2_bwd(h1, w, dxn2, dh2):
    l = h1.shape[0]

    def body(h_ref, w_ref, dxn_ref, dh2_ref, dh1_ref, dh1b_ref, dw_ref):
        i = pl.program_id(0)
        _, vjp = jax.vjp(_rms_norm, h_ref[...], w_ref[...])
        dh, dw = vjp(dxn_ref[...])
        dh1 = dh + dh2_ref[...]
        dh1_ref[...] = dh1
        dh1b_ref[...] = dh1.astype(BF16)

        @pl.when(i == 0)
        def _():
            dw_ref[...] = dw

        @pl.when(i > 0)
        def _():
            dw_ref[...] += dw

    return pl.pallas_call(
        body, name="norm2_bwd", grid=(l // TR,),
        in_specs=[_row(D), _vec_spec(), _row(D), _row(D)],
        out_specs=[_row(D), _row(D), _vec_spec()],
        out_shape=[jax.ShapeDtypeStruct((l, D), F32), jax.ShapeDtypeStruct((l, D), BF16),
                   jax.ShapeDtypeStruct((1, D), F32)],
        compiler_params=_params("arbitrary"),
    )(h1, w, dxn2, dh2)


def _loss_rows(h1, f, w, tgt):
    y = _rms_norm(h1 + f, w)
    err = (y - tgt) * (y - tgt)
    return 0.5 * jnp.sum(jnp.mean(err, axis=-1, keepdims=True), axis=0, keepdims=True)


def loss_head(h1, f, w, tgt):
    l = h1.shape[0]

    def body(h_ref, f_ref, w_ref, t_ref, loss_ref, dh_ref, dhb_ref, dw_ref):
        i = pl.program_id(0)
        loss, vjp = jax.vjp(_loss_rows, h_ref[...], f_ref[...], w_ref[...], t_ref[...])
        dh, _, dw, _ = vjp(jnp.ones((1, 1), F32))
        dh_ref[...] = dh
        dhb_ref[...] = dh.astype(BF16)
        loss = jnp.broadcast_to(loss, (1, LANES))

        @pl.when(i == 0)
        def _():
            dw_ref[...] = dw
            loss_ref[...] = loss

        @pl.when(i > 0)
        def _():
            dw_ref[...] += dw
            loss_ref[...] += loss

    return pl.pallas_call(
        body, name="loss_head", grid=(l // TR,),
        in_specs=[_row(D), _row(D), _vec_spec(), _row(D)],
        out_specs=[_vec_spec(LANES), _row(D), _row(D), _vec_spec()],
        out_shape=[jax.ShapeDtypeStruct((1, LANES), F32), jax.ShapeDtypeStruct((l, D), F32),
                   jax.ShapeDtypeStruct((l, D), BF16), jax.ShapeDtypeStruct((1, D), F32)],
        compiler_params=_params("arbitrary"),
    )(h1, f, w, tgt)


def _adamw(w, g, m, v):
    m = ADAM_B1 * m + (1.0 - ADAM_B1) * g
    v = ADAM_B2 * v + (1.0 - ADAM_B2) * (g * g)
    m_hat = m / (1.0 - ADAM_B1 ** ADAM_STEP)
    v_hat = v / (1.0 - ADAM_B2 ** ADAM_STEP)
    delta = -ADAM_LR * (m_hat / (jnp.sqrt(v_hat) + ADAM_EPS) + ADAM_WD * w)
    return delta, m, v


def adamw_shard(name, w, m, v, sums, recv, tile):
    r, c = w.shape
    tr, tc = tile
    assert r % tr == 0 and c % tc == 0

    def body(chip_ref, w_ref, m_ref, v_ref, own_ref, r0_ref, r1_ref, r2_ref, g_ref, d_ref, nm_ref, nv_ref):
        g = own_ref[...].astype(F32) + r0_ref[...].astype(F32)
        g = g + r1_ref[...].astype(F32)
        g = g + r2_ref[...].astype(F32)
        d, nm, nv = _adamw(w_ref[...], g, m_ref[...], v_ref[...])
        g_ref[...] = g
        d_ref[...] = d
        nm_ref[...] = nm
        nv_ref[...] = nv

    blk = pl.BlockSpec((tr, tc), lambda i, j, chip: (i, j))
    part = lambda s: pl.BlockSpec((None, tr, tc), lambda i, j, chip: (s, i, j))
    own = pl.BlockSpec((None, tr, tc), lambda i, j, chip: (chip[0], i, j))
    shape = jax.ShapeDtypeStruct((r, c), F32)
    my_chip = (2 * lax.axis_index("x") + lax.axis_index("y")).astype(jnp.int32).reshape(1)
    return pl.pallas_call(
        body, name=name, out_shape=[shape, shape, shape, shape],
        grid_spec=pltpu.PrefetchScalarGridSpec(num_scalar_prefetch=1, grid=(r // tr, c // tc),
                                               in_specs=[blk, blk, blk, own, part(0), part(1), part(2)],
                                               out_specs=[blk, blk, blk, blk]),
        compiler_params=_params("parallel", "parallel"),
    )(my_chip, w, m, v, sums, recv, recv, recv)


def adamw_small(w, g, m, v):
    def body(w_ref, g_ref, m_ref, v_ref, d_ref, nm_ref, nv_ref):
        d, nm, nv = _adamw(w_ref[...], g_ref[...], m_ref[...], v_ref[...])
        d_ref[...] = d
        nm_ref[...] = nm
        nv_ref[...] = nv

    shape = jax.ShapeDtypeStruct(w.shape, F32)
    return pl.pallas_call(body, name="adamw_small", out_shape=[shape, shape, shape])(w, g, m, v)


def pair_add(name, a, b, tile):
    q, r, c = b.shape
    tr, tc = tile
    assert r % tr == 0 and c % tc == 0

    def body(core_ref, a_ref, b_ref, o_ref):
        o_ref[...] = (a_ref[...].astype(F32) + b_ref[...].astype(F32)).astype(o_ref.dtype)

    blk = pl.BlockSpec((None, tr, tc), lambda s, i, j, core: (s, i, j))
    mine = pl.BlockSpec((None, None, tr, tc), lambda s, i, j, core: (s, core[0], i, j))
    return pl.pallas_call(
        body, name=name, out_shape=jax.ShapeDtypeStruct((q, r, c), BF16),
        grid_spec=pltpu.PrefetchScalarGridSpec(num_scalar_prefetch=1, grid=(q, r // tr, c // tc),
                                               in_specs=[mine, blk], out_specs=blk),
        compiler_params=_params("parallel", "parallel", "parallel"),
    )(lax.axis_index("c").astype(jnp.int32).reshape(1), a, b)


def _w_in_pieces():
    c = W_IN // N_DEV
    ranges = ((0, 8 * HW, 0, 0), (8 * HW, 8 * HW + 2 * HEADS, 1, 8 * HW), (8 * HW + 2 * HEADS, W_IN, 0, 2 * HEADS))
    out = []
    for d in range(N_DEV):
        for lo, hi, target, shift in ranges:
            s, e = max(lo, c * d), min(hi, c * (d + 1))
            if s < e:
                out.append((d, s - c * d, e - s, target, s - shift))
    return out


def _ffn_in_pieces():
    c = 2 * D_FF // N_DEV
    out = []
    for d in range(N_DEV):
        s = c * d
        while s < c * (d + 1):
            e = min((s // FF_BLK + 1) * FF_BLK, c * (d + 1))
            half, blk = divmod(s // FF_BLK, D_FF // FF_BLK)
            out.append((d, s - c * d, e - s, 0, (2 * blk + half) * FF_BLK + s % FF_BLK))
            s = e
    return out


def _plain_pieces(c):
    return [(d, 0, c, 0, c * d) for d in range(N_DEV)]


def shards_to_cols(name, g, widths, pieces, tr):
    _, r, c = g.shape
    covered = [sum(p[2] for p in pieces if p[3] == t) for t in range(len(widths))]

    def body(g_ref, *outs):
        for t, o in enumerate(outs):
            if covered[t] < widths[t]:
                o[...] = jnp.zeros(o.shape, o.dtype)
        for d, s, w, t, ts in pieces:
            outs[t][:, ts:ts + w] = g_ref[d, :, s:s + w]

    return pl.pallas_call(
        body, name=name, grid=(r // tr,),
        in_specs=[pl.BlockSpec((N_DEV, tr, c), lambda i: (0, i, 0))],
        out_specs=[pl.BlockSpec((tr, w), lambda i: (i, 0)) for w in widths],
        out_shape=[jax.ShapeDtypeStruct((r, w), g.dtype) for w in widths],
        compiler_params=_params("parallel"),
    )(g)


def cols_to_shards(name, srcs, c, pieces, tr):
    r = srcs[0].shape[0]

    def body(*refs):
        o = refs[-1]
        for d, s, w, t, ts in pieces:
            o[d, :, s:s + w] = refs[t][:, ts:ts + w]

    return pl.pallas_call(
        body, name=name, grid=(r // tr,),
        in_specs=[pl.BlockSpec((tr, t.shape[1]), lambda i: (i, 0)) for t in srcs],
        out_specs=pl.BlockSpec((N_DEV, tr, c), lambda i: (0, i, 0)),
        out_shape=jax.ShapeDtypeStruct((N_DEV, r, c), srcs[0].dtype),
        compiler_params=_params("parallel"),
    )(*srcs)


def rows_to_shards(name, srcs, c, pieces, tc):
    r = srcs[0].shape[1]

    def body(*refs):
        o = refs[-1]
        for d, s, w, t, ts in pieces:
            o[d, s:s + w, :] = refs[t][ts:ts + w, :]

    return pl.pallas_call(
        body, name=name, grid=(r // tc,),
        in_specs=[pl.BlockSpec((t.shape[0], tc), lambda i: (0, i)) for t in srcs],
        out_specs=pl.BlockSpec((N_DEV, c, tc), lambda i: (0, 0, i)),
        out_shape=jax.ShapeDtypeStruct((N_DEV, c, r), srcs[0].dtype),
        compiler_params=_params("parallel"),
    )(*srcs)


def _place():
    return lax.axis_index("x"), lax.axis_index("y"), lax.axis_index("c")


def _dev(px, py, pc):
    return 4 * px + 2 * py + pc


class Exchange:
    def __init__(self, arrays, out_shapes, sems, start, finish):
        self.arrays, self.out_shapes, self.sems, self.start, self.finish = arrays, out_shapes, sems, start, finish


def gather_exchange(arrays):
    n = len(arrays)

    def plan(ins, outs, sems):
        send_sems, recv_sems, local_sems = sems
        x, y, c = _place()
        me, sibling = (x, y, c), (x, y, 1 - c)
        chips = [(1 - x, y), (x, 1 - y), (1 - x, 1 - y)]

        def copy(a, k, block, to, src=None):
            dst = outs[a].at[_dev(*block)]
            return pltpu.make_async_remote_copy(
                src_ref=dst if src is None else src, dst_ref=dst, send_sem=send_sems.at[a, k],
                recv_sem=recv_sems.at[a, k], device_id=to, device_id_type=MESH)

        mine = [pltpu.make_async_copy(ins[a], outs[a].at[_dev(*me)], local_sems.at[a]) for a in range(n)]
        first = []
        for a in range(n):
            first.append(copy(a, 0, me, sibling, src=ins[a]))
            first += [copy(a, 1 + j, me, (*chip, c), src=ins[a]) for j, chip in enumerate(chips)]
        return me, sibling, chips, c, copy, mine, first

    def start(ins, outs, sems):
        *_, mine, first = plan(ins, outs, sems)
        for cp in mine + first:
            cp.start()

    def finish(ins, outs, sems):
        me, sibling, chips, c, copy, mine, first = plan(ins, outs, sems)
        passed = []
        for j, chip in enumerate(chips):
            for a in range(n):
                copy(a, 1 + j, (*chip, c), me).wait_recv()
                fwd = copy(a, 4 + j, (*chip, c), sibling)
                fwd.start()
                passed.append(fwd)
        for a in range(n):
            copy(a, 0, sibling, me).wait_recv()
        for j, chip in enumerate(chips):
            for a in range(n):
                copy(a, 4 + j, (*chip, 1 - c), me).wait_recv()
        for cp in first + passed:
            cp.wait_send()
        for cp in mine:
            cp.wait()

    return Exchange(arrays, [jax.ShapeDtypeStruct((N_DEV,) + t.shape, t.dtype) for t in arrays],
                    [pltpu.SemaphoreType.DMA((n, 7)), pltpu.SemaphoreType.DMA((n, 7)), pltpu.SemaphoreType.DMA((n,))],
                    start, finish)


def core_exchange(arrays):
    n = len(arrays)

    def copies(ins, outs, sems):
        send_sems, recv_sems = sems
        x, y, c = _place()
        return [pltpu.make_async_remote_copy(
            src_ref=ins[a].at[q, 1 - c], dst_ref=outs[a].at[q], send_sem=send_sems.at[a, q],
            recv_sem=recv_sems.at[a, q], device_id=(x, y, 1 - c), device_id_type=MESH)
            for a in range(n) for q in range(4)]

    def start(ins, outs, sems):
        for cp in copies(ins, outs, sems):
            cp.start()

    def finish(ins, outs, sems):
        for cp in copies(ins, outs, sems):
            cp.wait()

    return Exchange(arrays, [jax.ShapeDtypeStruct((4,) + t.shape[2:], t.dtype) for t in arrays],
                    [pltpu.SemaphoreType.DMA((n, 4)), pltpu.SemaphoreType.DMA((n, 4))], start, finish)


def chips_exchange(arrays):
    n = len(arrays)

    def copies(ins, outs, sems):
        send_sems, recv_sems = sems
        x, y, c = _place()
        chips = [(1 - x, y), (x, 1 - y), (1 - x, 1 - y)]
        return [pltpu.make_async_remote_copy(
            src_ref=ins[a].at[2 * qx + qy], dst_ref=outs[a].at[j], send_sem=send_sems.at[a, j],
            recv_sem=recv_sems.at[a, j], device_id=(qx, qy, c), device_id_type=MESH)
            for a in range(n) for j, (qx, qy) in enumerate(chips)]

    def start(ins, outs, sems):
        for cp in copies(ins, outs, sems):
            cp.start()

    def finish(ins, outs, sems):
        for cp in copies(ins, outs, sems):
            cp.wait()

    return Exchange(arrays, [jax.ShapeDtypeStruct((3,) + t.shape[1:], t.dtype) for t in arrays],
                    [pltpu.SemaphoreType.DMA((n, 3)), pltpu.SemaphoreType.DMA((n, 3))], start, finish)


def run_exchange(name, exch):
    n_in, n_out = len(exch.arrays), len(exch.out_shapes)
    any_spec = pl.BlockSpec(memory_space=pl.ANY)

    def body(*refs):
        ins, outs, sems = refs[:n_in], refs[n_in:n_in + n_out], refs[n_in + n_out:]
        exch.start(ins, outs, sems)
        exch.finish(ins, outs, sems)

    return pl.pallas_call(
        body, name=name, in_specs=[any_spec] * n_in, out_specs=[any_spec] * n_out, out_shape=exch.out_shapes,
        scratch_shapes=exch.sems,
    )(*exch.arrays)


def host_exchange(body, n_in, n_out, exch, first, last):
    ne_in, ne_out, ne_sem = len(exch.arrays), len(exch.out_shapes), len(exch.sems)

    def wrapped(*refs):
        ins, refs = refs[:n_in], refs[n_in:]
        e_ins, refs = refs[:ne_in], refs[ne_in:]
        outs, refs = refs[:n_out], refs[n_out:]
        e_outs, refs = refs[:ne_out], refs[ne_out:]
        scratch, e_sems = refs[:len(refs) - ne_sem], refs[len(refs) - ne_sem:]

        @pl.when(first())
        def _():
            exch.start(e_ins, e_outs, e_sems)

        body(*ins, *outs, *scratch)

        @pl.when(last())
        def _():
            exch.finish(e_ins, e_outs, e_sems)

    return wrapped


def _hosted(exch, in_specs, out_specs, out_shape, scratch_shapes):
    any_spec = pl.BlockSpec(memory_space=pl.ANY)
    return dict(in_specs=list(in_specs) + [any_spec] * len(exch.arrays),
                out_specs=list(out_specs) + [any_spec] * len(exch.out_shapes),
                out_shape=list(out_shape) + list(exch.out_shapes),
                scratch_shapes=list(scratch_shapes) + list(exch.sems))


def allreduce_small(buf):
    r = buf.shape[0]
    vmem = pl.BlockSpec(memory_space=pltpu.VMEM)

    def body(buf_ref, out_ref, all_ref, send_sems, recv_sems):
        x, y, c = _place()
        me = _dev(x, y, c)
        copies = []
        for k in range(1, N_DEV):
            peer = (x ^ (k >> 2), y ^ ((k >> 1) & 1), c ^ (k & 1))
            copies.append(pltpu.make_async_remote_copy(
                src_ref=buf_ref, dst_ref=all_ref.at[me], send_sem=send_sems.at[k - 1], recv_sem=recv_sems.at[k - 1],
                device_id=peer, device_id_type=MESH))
        for cp in copies:
            cp.start()
        all_ref[me] = buf_ref[...]
        for cp in copies:
            cp.wait()
        total = all_ref[0]
        for d in range(1, N_DEV):
            total = total + all_ref[d]
        out_ref[...] = total

    return pl.pallas_call(
        body, name="allreduce_small", in_specs=[vmem], out_specs=vmem,
        out_shape=jax.ShapeDtypeStruct((r, LANES), F32),
        scratch_shapes=[pltpu.VMEM((N_DEV, r, LANES), F32), pltpu.SemaphoreType.DMA((N_DEV - 1,)),
                        pltpu.SemaphoreType.DMA((N_DEV - 1,))],
    )(buf)


def _cols_from_shards(g):
    return jnp.transpose(g, (1, 0, 2)).reshape(g.shape[1], N_DEV * g.shape[2])


def _pack(t):
    flat = t.reshape(-1)
    return jnp.pad(flat, (0, -flat.size % (8 * LANES))).reshape(-1, LANES)


def _pad_lanes(t):
    t = t.reshape(1, -1)
    return jnp.pad(t, ((0, 0), (0, LANES - t.shape[1])))


def kernel(x, meta_tokens, lb_logits, mix_norm_w, w_in, hg_norm_w, gd_conv_w, gd_a_log, gd_dt_bias, gd_norm_w, w_branch_a, w_branch_b, w_out, ffn_norm_w, w_ffn_in, w_ffn_out, final_norm_w, loss_target, m_meta_tokens, m_lb_logits, m_mix_norm_w, m_w_in, m_hg_norm_w, m_gd_conv_w, m_gd_a_log, m_gd_dt_bias, m_gd_norm_w, m_w_branch_a, m_w_branch_b, m_w_out, m_ffn_norm_w, m_w_ffn_in, m_w_ffn_out, m_final_norm_w, v_meta_tokens, v_lb_logits, v_mix_norm_w, v_w_in, v_hg_norm_w, v_gd_conv_w, v_gd_a_log, v_gd_dt_bias, v_gd_norm_w, v_w_branch_a, v_w_branch_b, v_w_out, v_ffn_norm_w, v_w_ffn_in, v_w_ffn_out, v_final_norm_w):
    xs = x[0]
    tgt = loss_target[0]
    l = xs.shape[0]
    lp = PRE + l
    me = _dev(*_place())

    big = [w_in[0], w_branch_a[0], w_branch_b[0], w_out[0], w_ffn_in[0], w_ffn_out[0]]
    big16 = [t.astype(BF16) for t in big]
    g_in, g_meta, g_conv = run_exchange("gather_first", gather_exchange([big16[0], meta_tokens, gd_conv_w[0]]))
    w_main, w_ab = shards_to_cols("w_in_cols", g_in, [W_MAIN, LANES], _w_in_pieces(), 256)
    meta_full = _cols_from_shards(g_meta)
    cw = _cols_from_shards(g_conv)
    pre = jnp.concatenate([jnp.zeros((PRE - N_META, D), F32), meta_full], axis=0)
    l0, l1 = lb_logits[0:1], lb_logits[1:2]
    alog, dtb = _pad_lanes(gd_a_log), _pad_lanes(gd_dt_bias)

    xn = norm1_fwd(pre, xs, mix_norm_w)
    proj, (g_ba, g_bb, g_out, g_ffi, g_ffo) = matmul("proj_main", xn, w_main, "nn", lp, W_MAIN, D, F32,
                                                      exch=gather_exchange(big16[1:]))
    wa, = shards_to_cols("w_branch_a_cols", g_ba, [D], _plain_pieces(D // N_DEV), 256)
    wb, = shards_to_cols("w_branch_b_cols", g_bb, [D], _plain_pieces(D // N_DEV), 256)
    wo = g_out.reshape(D, D)
    w_gu, = shards_to_cols("w_ffn_in_cols", g_ffi, [2 * D_FF], _ffn_in_pieces(), 256)
    w_fo = g_ffo.reshape(D_FF, D)
    proj_ab = matmul("proj_ab", xn, w_ab, "nn", lp, LANES, D, F32)
    o_a, st_a, o_b, st_b, t_inv = mixer_fwd(proj, proj_ab, l0, l1, hg_norm_w, cw, alog, dtb, gd_norm_w)
    ya = matmul("branch_a", o_a, wa, "nn", l, D, HW, F32, a_off=(PRE // 512, 0))
    yb = matmul("branch_b", o_b, wb, "nn", l, D, HW, F32, a_off=(PRE // 512, 0))
    merged = merge_fwd(proj, ya, yb)
    h1, xn2 = mix_out_norm2(merged, wo, xs, ffn_norm_w)
    gu, act = ffn_in_fused(xn2, w_gu)
    f = matmul("ffn_out", act, w_fo, "nn", l, D, D_FF, F32, tn=512)
    loss_part, dh2, dh2_b, d_final_w = loss_head(h1, f, final_norm_w.reshape(1, D), tgt)

    tiles = {"w_in": (W_IN // N_DEV, 256), "w_branch_a": (256, D // N_DEV), "w_branch_b": (256, D // N_DEV),
             "w_out": (64, D), "w_ffn_in": (256, 2 * D_FF // N_DEV), "w_ffn_out": (176, D)}

    def chip_sums(tag, parts, host=None):
        blocks = [p.reshape((4, 2) + p.shape[1:]) for p in parts.values()]
        exch = core_exchange(blocks)
        hosted, from_core = (None, run_exchange("exchange_core_" + tag, exch)) if host is None else host(exch)
        return hosted, {nm: pair_add("pair_add_" + nm, p, r, tiles[nm]) for (nm, p, r) in zip(parts, blocks, from_core)}

    dgu = d_act_fused(dh2_b, w_fo, gu)
    dw_fo = matmul("dw_ffn_out", act, dh2_b, "tn", D_FF, D, l, BF16, tm=512, tn=512)
    dxn2 = matmul("d_xn2", dgu, w_gu, "nt", l, D, 2 * D_FF, F32, tn=256)
    dw_gu = matmul("dw_ffn_in", xn2, dgu, "tn", D, 2 * D_FF, l, BF16, tm=512, tn=512)
    dh1, dh1_b, d_ffn_norm_w = norm2_bwd(h1, ffn_norm_w, dxn2, dh2)
    dmerged, sums = chip_sums("ffn", {
        "w_ffn_in": cols_to_shards("dw_ffn_in_shards", [dw_gu], 2 * D_FF // N_DEV, _ffn_in_pieces(), 256),
        "w_ffn_out": dw_fo.reshape(N_DEV, D_FF // N_DEV, D)},
        host=lambda exch: matmul("d_merged", dh1_b, wo, "nt", l, D, D, F32, exch=exch))
    dw_o = matmul("dw_out", merged, dh1_b, "tn", D, D, l, BF16, tm=512, tn=512)
    dproj, dya, dyb = merge_bwd(proj, ya, yb, dmerged)
    do_a = matmul("d_o_a", dya, wa, "nt", lp, HW, D, F32)
    do_b = matmul("d_o_b", dyb, wb, "nt", lp, HW, D, F32)
    dw_a = matmul("dw_branch_a", o_a, dya, "tn", HW, D, lp, BF16, tm=512, tn=512)
    dw_b = matmul("dw_branch_b", o_b, dyb, "tn", HW, D, lp, BF16, tm=512, tn=512)
    sums.update(chip_sums("mix", {
        "w_branch_a": cols_to_shards("dw_branch_a_shards", [dw_a], D // N_DEV, _plain_pieces(D // N_DEV), 256),
        "w_branch_b": cols_to_shards("dw_branch_b_shards", [dw_b], D // N_DEV, _plain_pieces(D // N_DEV), 256),
        "w_out": dw_o.reshape(N_DEV, D // N_DEV, D)})[1])
    ffn_names, mix_names = ["w_ffn_in", "w_ffn_out"], ["w_branch_a", "w_branch_b", "w_out"]
    (dproj, dl0, dl1, d_hg_nw, dproj_ab, d_conv, d_alog, d_dtb, d_gd_nw), from_chips_early = mixer_bwd(
        proj, proj_ab, l0, l1, hg_norm_w, cw, alog, dtb, gd_norm_w, st_a, st_b, t_inv, do_a, do_b, dproj,
        chips_exchange([sums[nm] for nm in ffn_names + mix_names]))
    dwt_main = matmul("dw_in_main", dproj, xn, "tn", W_MAIN, D, lp, BF16, tm=512, tn=512)
    dwt_ab = matmul("dw_in_ab", dproj_ab, xn, "tn", LANES, D, lp, BF16, tn=512)
    dxn_ab, sums_in = chip_sums("w_in", {
        "w_in": rows_to_shards("dw_in_shards", [dwt_main, dwt_ab], W_IN // N_DEV, _w_in_pieces(), 256)},
        host=lambda exch: matmul("d_xn_ab", dproj_ab, w_ab, "nt", lp, D, LANES, F32, exch=exch))
    sums.update(sums_in)
    from_chips = dict(zip(ffn_names + mix_names, from_chips_early))
    dxn, (from_chips["w_in"],) = matmul("d_xn", dproj, w_main, "nt", lp, D, W_MAIN, F32, tn=256,
                                        exch=chips_exchange([sums["w_in"]]))
    grad_x, dpre, d_mix_norm_w = norm1_bwd(pre, xs, mix_norm_w, dxn, dxn_ab, dh1)
    names = ["w_in", "w_branch_a", "w_branch_b", "w_out", "w_ffn_in", "w_ffn_out"]
    moms = [(m_w_in, v_w_in), (m_w_branch_a, v_w_branch_a), (m_w_branch_b, v_w_branch_b), (m_w_out, v_w_out),
            (m_w_ffn_in, v_w_ffn_in), (m_w_ffn_out, v_w_ffn_out)]
    upd = {}
    for nm, w, (m, v) in zip(names, big, moms):
        flip = (lambda t: t.T) if nm == "w_in" else (lambda t: t)
        res = adamw_shard("adamw_" + nm, flip(w), flip(m[0]), flip(v[0]), sums[nm], from_chips[nm], tiles[nm])
        upd[nm] = tuple(flip(t_)[None] for t_ in res)

    d_lb = jnp.concatenate([dl0, dl1], axis=0)
    rep = [_pack(t) for t in (d_lb, d_mix_norm_w, d_hg_nw, d_alog, d_dtb, d_gd_nw, d_ffn_norm_w, d_final_w)]
    n_rep = sum(t.shape[0] for t in rep)
    n_meta = N_META * D // LANES
    tot = allreduce_small(jnp.concatenate(rep + [_pack(loss_part), _pack(dpre[PRE - N_META:]), _pack(d_conv)], axis=0))
    loss = tot[n_rep, 0]
    g_meta_full = tot[n_rep + 8:n_rep + 8 + n_meta].reshape(N_META, D)
    g_conv_full = tot[n_rep + 8 + n_meta:].reshape(CONV_K, 3 * HW)
    g_meta_mine = lax.dynamic_slice_in_dim(g_meta_full, me * (D // N_DEV), D // N_DEV, axis=1)
    g_conv_mine = lax.dynamic_slice_in_dim(g_conv_full, me * (3 * DH), 3 * DH, axis=1)

    small = [("lb_logits", lb_logits, m_lb_logits, v_lb_logits), ("mix_norm_w", mix_norm_w, m_mix_norm_w, v_mix_norm_w),
             ("hg_norm_w", hg_norm_w, m_hg_norm_w, v_hg_norm_w), ("gd_a_log", gd_a_log, m_gd_a_log, v_gd_a_log),
             ("gd_dt_bias", gd_dt_bias, m_gd_dt_bias, v_gd_dt_bias), ("gd_norm_w", gd_norm_w, m_gd_norm_w, v_gd_norm_w),
             ("ffn_norm_w", ffn_norm_w, m_ffn_norm_w, v_ffn_norm_w),
             ("final_norm_w", final_norm_w, m_final_norm_w, v_final_norm_w),
             ("meta_tokens", meta_tokens, m_meta_tokens, v_meta_tokens), ("gd_conv_w", gd_conv_w, m_gd_conv_w, v_gd_conv_w)]

    sizes = [_pack(w).shape[0] for _, w, _, _ in small]
    g_small = jnp.concatenate([tot[:n_rep], _pack(g_meta_mine), _pack(g_conv_mine)], axis=0)
    assert g_small.shape[0] == sum(sizes)
    w_small = jnp.concatenate([_pack(w) for _, w, _, _ in small], axis=0)
    m_small = jnp.concatenate([_pack(m) for _, _, m, _ in small], axis=0)
    v_small = jnp.concatenate([_pack(v) for _, _, _, v in small], axis=0)
    d_small, nm_small, nv_small = adamw_small(w_small, g_small, m_small, v_small)
    row = 0
    for (nm, w, _, _), sz in zip(small, sizes):
        def unpack(t):
            return t[row:row + sz].reshape(-1)[:w.size].reshape(w.shape)
        upd[nm] = (unpack(g_small), unpack(d_small), unpack(nm_small), unpack(nv_small))
        row += sz

    order = ["meta_tokens", "lb_logits", "mix_norm_w", "w_in", "hg_norm_w", "gd_conv_w", "gd_a_log", "gd_dt_bias",
             "gd_norm_w", "w_branch_a", "w_branch_b", "w_out", "ffn_norm_w", "w_ffn_in", "w_ffn_out", "final_norm_w"]
    outs = [loss, grad_x[None]]
    for j in range(4):
        outs += [upd[nm][j] for nm in order]
    return tuple(outs)
```

```python
import functools

import jax
import jax.numpy as jnp
from jax import lax
from jax.experimental import pallas as pl
from jax.experimental.pallas import tpu as pltpu

F32 = jnp.float32
BF16 = jnp.bfloat16
MESH = pl.DeviceIdType.MESH

EPS = 1e-6
D = 2048
N_META = 16
CHUNK = 64
SUB = 16
HEADS = 8
DH = 128
HW = HEADS * DH
CONV_K = 4
TAIL = 8
D_FF = 5632
PRE = 512
N_SKIP = PRE // CHUNK - 1
N_DEV = 8
LANES = 128
FF_BLK = 512
W_IN = 4 * HW + 4 * HW + 2 * HEADS + 2 * D
W_MAIN = 4 * HW + 4 * HW + 2 * D

ADAM_LR = 0.001
ADAM_B1 = 0.9
ADAM_B2 = 0.999
ADAM_EPS = 1e-08
ADAM_WD = 0.01
ADAM_STEP = 10

VMEM_LIMIT = 52 * 1024 * 1024

_NN = ((1,), (0,))
_NT = ((1,), (1,))
_TN = ((0,), (0,))


def _params(*sem):
    return pltpu.CompilerParams(dimension_semantics=sem, vmem_limit_bytes=VMEM_LIMIT)


BF16_ONCE, SPLIT, EXACT_LHS = 0, 1, 2


def _dot_bf16(a, b, dims):
    if a.ndim == 3:
        dn = (((dims[0][0] + 1,), (dims[1][0] + 1,)), ((0,), (0,)))
    else:
        dn = (dims, ((), ()))
    return lax.dot_general(a, b, dn, preferred_element_type=F32)


def _split(t):
    hi = t.astype(BF16)
    return hi, (t - hi.astype(F32)).astype(BF16)


def _raw_dot(a, b, dims, mode):
    if mode == BF16_ONCE:
        return _dot_bf16(a.astype(BF16), b.astype(BF16), dims)
    if mode == SPLIT:
        a_hi, a_lo = _split(a)
        b_hi, b_lo = _split(b)
        return _dot_bf16(a_hi, b_hi, dims) + (_dot_bf16(a_hi, b_lo, dims) + _dot_bf16(a_lo, b_hi, dims))
    a = a.astype(BF16)
    b_hi = b.astype(BF16)
    rest = b - b_hi.astype(F32)
    b_mid = rest.astype(BF16)
    b_lo = (rest - b_mid.astype(F32)).astype(BF16)
    return _dot_bf16(a, b_hi, dims) + (_dot_bf16(a, b_mid, dims) + _dot_bf16(a, b_lo, dims))


@functools.partial(jax.custom_vjp, nondiff_argnums=(2,))
def mm_nn(a, b, mode=BF16_ONCE):
    return _raw_dot(a, b, _NN, mode)


@functools.partial(jax.custom_vjp, nondiff_argnums=(2,))
def mm_nt(a, b, mode=BF16_ONCE):
    return _raw_dot(a, b, _NT, mode)


@functools.partial(jax.custom_vjp, nondiff_argnums=(2,))
def mm_tn(a, b, mode=BF16_ONCE):
    return _raw_dot(a, b, _TN, mode)


def _general(mode):
    return SPLIT if mode == EXACT_LHS else mode


mm_nn.defvjp(lambda a, b, p: (_raw_dot(a, b, _NN, p), (a, b)),
             lambda p, res, g: (mm_nt(g, res[1], _general(p)), mm_tn(res[0], g, p)))
mm_nt.defvjp(lambda a, b, p: (_raw_dot(a, b, _NT, p), (a, b)),
             lambda p, res, g: (mm_nn(g, res[1], p), mm_tn(g, res[0], p)))
mm_tn.defvjp(lambda a, b, p: (_raw_dot(a, b, _TN, p), (a, b)),
             lambda p, res, g: (mm_nt(res[1], g, _general(p)), mm_nn(res[0], g, _general(p))))


def _sigmoid(x):
    return jax.nn.sigmoid(x)


def _silu(x):
    return x * _sigmoid(x)


def _softplus(x):
    return jnp.maximum(x, 0.0) + jnp.log(1.0 + jnp.exp(-jnp.abs(x)))


def _l2n(t):
    return t * lax.rsqrt(jnp.sum(t * t, axis=-1, keepdims=True) + EPS)


def _rms_norm(x, w):
    return x * lax.rsqrt(jnp.mean(x * x, axis=-1, keepdims=True) + EPS) * w


def _gated_norm(o, gate, nw):
    o = o * lax.rsqrt(jnp.mean(o * o, axis=-1, keepdims=True) + EPS) * nw
    return o * _silu(gate)


def _heads(ref, piece):
    return jnp.stack([ref[:, piece * HW + DH * j:piece * HW + DH * (j + 1)] for j in range(HEADS)])


def _put_heads(ref, piece, value, accumulate=False):
    for j in range(HEADS):
        cols = slice(piece * HW + DH * j, piece * HW + DH * (j + 1))
        if accumulate:
            ref[:, cols] += value[j]
        else:
            ref[:, cols] = value[j].astype(ref.dtype)


def _diag_scores(q_i, k_i, f_i):
    nb, s, d = q_i.shape
    row = lax.broadcasted_iota(jnp.int32, (nb, s, d), 1)
    q4, f4 = q_i[:, :, None, :], f_i[:, :, None, :]
    kk = k_i
    rows_out = []
    for t in range(s):
        if t > 0:
            kk = jnp.where(row < t, kk * f4[:, t], kk)
        rows_out.append(jnp.sum(kk * q4[:, t], axis=-1)[:, None, :])
    s_ii = jnp.concatenate(rows_out, axis=1)
    tri = lax.broadcasted_iota(jnp.int32, (nb, s, s), 1) >= lax.broadcasted_iota(jnp.int32, (nb, s, s), 2)
    return jnp.where(tri, s_ii, 0.0)


def hg_chunk(hq, hf, hi, hg, l0, l1, nw, st):
    nb = hq.shape[0]
    m = jnp.maximum(l0, l1)
    e0 = jnp.exp(l0 - m)
    e1 = jnp.exp(l1 - m)
    lb = e0 / (e0 + e1)
    sig = _sigmoid(hf)
    q = _silu(hq)
    f = lb + (1.0 - lb) * sig
    log_f = jnp.log(f)
    k = (1.0 - lb) * _sigmoid(-hf)
    v = hi
    rows = lax.broadcasted_iota(jnp.int32, (nb, CHUNK, CHUNK), 1)
    cols = lax.broadcasted_iota(jnp.int32, (nb, CHUNK, CHUNK), 2)
    b = mm_nn((rows >= cols).astype(F32), log_f, EXACT_LHS)
    o_inter = mm_nt(q * jnp.exp(b), st)
    outs = []
    for i in range(CHUNK // SUB):
        lo = i * SUB
        b_i, q_i, k_i, v_i = b[:, lo:lo + SUB], q[:, lo:lo + SUB], k[:, lo:lo + SUB], v[:, lo:lo + SUB]
        o_i = mm_nn(_diag_scores(q_i, k_i, f[:, lo:lo + SUB]), v_i)
        if i > 0:
            b_ref = b[:, :, None, :][:, lo - 1]
            q_t = q_i * jnp.exp(b_i - b_ref)
            k_t = k[:, :lo] * jnp.exp(b_ref - b[:, :lo])
            o_i = o_i + mm_nn(mm_nt(q_t, k_t), v[:, :lo])
        outs.append(o_i)
    o = o_inter + jnp.concatenate(outs, axis=1)
    b_last = b[:, :, None, :][:, CHUNK - 1]
    st_new = st * jnp.exp(b_last) + mm_tn(v, k * jnp.exp(b_last - b))
    return _gated_norm(o, hg, nw), st_new


@jax.custom_vjp
def _unit_lower_inverse(a):
    n = a.shape[-1]
    eye = (lax.broadcasted_iota(jnp.int32, a.shape, 1) == lax.broadcasted_iota(jnp.int32, a.shape, 2)).astype(F32)
    x = eye - a
    p = mm_nn(a, a, SPLIT)
    x = x + mm_nn(x, p, SPLIT)
    power = 4
    while power < n:
        p = mm_nn(p, p, SPLIT)
        x = x + mm_nn(x, p, SPLIT)
        power *= 2
    return x


def _unit_lower_inverse_fwd(a):
    t = _unit_lower_inverse(a)
    return t, t


def _unit_lower_inverse_bwd(t, dt):
    return (-mm_tn(t, mm_nt(dt, t, SPLIT), SPLIT),)


_unit_lower_inverse.defvjp(_unit_lower_inverse_fwd, _unit_lower_inverse_bwd)


@jax.custom_vjp
def _kept_inverse(a, t):
    return t


_kept_inverse.defvjp(lambda a, t: (t, t),
                     lambda t, dt: (-mm_tn(t, mm_nt(dt, t, SPLIT), SPLIT), jnp.zeros_like(t)))


def gd_chunk(cq, ck, cv, z, ab, alog, dtb, nw, s, kept_t=None, keep_t=False):
    nb, c, _ = cq.shape
    lane = lax.broadcasted_iota(jnp.int32, (nb, 1, DH), 2)
    head = lax.broadcasted_iota(jnp.int32, (nb, 1, DH), 0)
    pick = lambda t, off: jnp.sum(jnp.where(lane == head + off, t[None], 0.0), axis=2, keepdims=True)
    a_raw, b_raw = pick(ab, 0), pick(ab, HEADS)
    beta = _sigmoid(b_raw)
    g = -jnp.exp(pick(alog, 0)) * _softplus(a_raw + pick(dtb, 0))
    q = _l2n(_silu(cq)) * (DH ** -0.5)
    k = _l2n(_silu(ck))
    v = _silu(cv)
    rows = lax.broadcasted_iota(jnp.int32, (nb, c, c), 1)
    cols = lax.broadcasted_iota(jnp.int32, (nb, c, c), 2)
    tril = (rows >= cols).astype(F32)
    gc_w = mm_nn(tril, jnp.broadcast_to(g, (nb, c, DH)), EXACT_LHS)
    g_sq = jnp.broadcast_to(g, (nb, c, c))
    gc_col = mm_nn(tril, g_sq, EXACT_LHS)
    gc_row = mm_nn(jnp.ones((nb, c, c), F32), g_sq * (rows <= cols).astype(F32), EXACT_LHS)
    rel = jnp.exp(jnp.minimum(gc_col - gc_row, 0.0))
    a = jnp.where(rows > cols, beta * mm_nt(k, k) * rel, 0.0)
    t_inv = _unit_lower_inverse(a) if kept_t is None else _kept_inverse(a, kept_t)
    e_gc = jnp.exp(gc_w)
    w = mm_nn(t_inv, beta * e_gc * k, SPLIT)
    u = mm_nn(t_inv, beta * v, SPLIT)
    v_new = u - mm_nn(w, s)
    attn = jnp.where(rows >= cols, mm_nt(q, k) * rel, 0.0)
    o = mm_nn(q * e_gc, s) + mm_nn(attn, v_new)
    g_last = jnp.sum(g, axis=1, keepdims=True)
    s_new = jnp.exp(g_last) * s + mm_tn(k * jnp.exp(g_last - gc_w), v_new)
    out = _gated_norm(o, z, nw)
    return (out, s_new, t_inv) if keep_t else (out, s_new)


def _gd_conv(x_ref, tail_ref, cw_ref, xe_ref, first):
    xe_ref[0:TAIL, :] = jnp.where(first, 0.0, tail_ref[:, 0:3 * HW])
    xe_ref[TAIL:TAIL + CHUNK, :] = x_ref[:, 0:3 * HW]
    y = cw_ref[pl.ds(0, 1), :] * xe_ref[pl.ds(TAIL - CONV_K + 1, CHUNK), :]
    for k in range(1, CONV_K):
        y = y + cw_ref[pl.ds(k, 1), :] * xe_ref[pl.ds(TAIL - CONV_K + 1 + k, CHUNK), :]
    return y


def _conv_heads(y, piece):
    return jnp.stack([y[:, piece * HW + DH * j:piece * HW + DH * (j + 1)] for j in range(HEADS)])


def mixer_fwd(proj, proj_ab, l0, l1, hg_nw, cw, alog, dtb, gd_nw):
    lp = proj.shape[0]
    nc = lp // CHUNK

    def body(xh_ref, l0_ref, l1_ref, hnw_ref, xg_ref, tail_ref, ab_ref, cw_ref, alog_ref, dtb_ref, gnw_ref,
             oa_ref, sta_out_ref, ob_ref, stb_out_ref, tinv_ref, sta_ref, stb_ref, xe_ref):
        c = pl.program_id(0)

        @pl.when(c == 0)
        def _():
            sta_ref[...] = jnp.zeros(sta_ref.shape, F32)
            stb_ref[...] = jnp.zeros(stb_ref.shape, F32)

        @pl.when(c < N_SKIP)
        def _():
            for ref in (oa_ref, sta_out_ref, ob_ref, stb_out_ref, tinv_ref):
                ref[...] = jnp.zeros(ref.shape, ref.dtype)

        @pl.when(c >= N_SKIP)
        def _():
            stb = stb_ref[...]
            stb_out_ref[0] = stb
            y = _gd_conv(xg_ref, tail_ref, cw_ref, xe_ref, c == 0)
            ob, stb_new, tinv_ref[0] = gd_chunk(
                _conv_heads(y, 0), _conv_heads(y, 1), _conv_heads(y, 2), _heads(xg_ref, 3),
                ab_ref[...], alog_ref[...], dtb_ref[...], gnw_ref[...], stb, keep_t=True)
            _put_heads(ob_ref, 0, ob)
            stb_ref[...] = stb_new
            sta = sta_ref[...]
            sta_out_ref[0] = sta
            oa, sta_new = hg_chunk(_heads(xh_ref, 0), _heads(xh_ref, 1), _heads(xh_ref, 2), _heads(xh_ref, 3),
                                   _heads(l0_ref, 0), _heads(l1_ref, 0), hnw_ref[...], sta)
            _put_heads(oa_ref, 0, oa)
            sta_ref[...] = sta_new

    vec = pl.BlockSpec((1, HW), lambda c: (0, 0))
    one = pl.BlockSpec((1, DH), lambda c: (0, 0))
    st_spec = pl.BlockSpec((1, HEADS, DH, DH), lambda c: (c, 0, 0, 0))
    o_spec = pl.BlockSpec((CHUNK, HW), lambda c: (c, 0))
    o_shape = jax.ShapeDtypeStruct((lp, HW), BF16)
    st_shape = jax.ShapeDtypeStruct((nc, HEADS, DH, DH), F32)
    return pl.pallas_call(
        body, name="mixer_fwd", grid=(nc,),
        in_specs=[pl.BlockSpec((CHUNK, 4 * HW), lambda c: (c, 0)), vec, vec, one,
                  pl.BlockSpec((CHUNK, 4 * HW), lambda c: (c, 1)),
                  pl.BlockSpec((TAIL, 4 * HW), lambda c: (jnp.maximum(c * (CHUNK // TAIL) - 1, 0), 1)),
                  pl.BlockSpec((CHUNK, DH), lambda c: (c, 0)),
                  pl.BlockSpec((CONV_K, 3 * HW), lambda c: (0, 0)), one, one, one],
        out_specs=[o_spec, st_spec, o_spec, st_spec,
                   pl.BlockSpec((1, HEADS, CHUNK, CHUNK), lambda c: (c, 0, 0, 0))],
        out_shape=[o_shape, st_shape, o_shape, st_shape, jax.ShapeDtypeStruct((nc, HEADS, CHUNK, CHUNK), F32)],
        scratch_shapes=[pltpu.VMEM((HEADS, DH, DH), F32), pltpu.VMEM((HEADS, DH, DH), F32),
                        pltpu.VMEM((TAIL + CHUNK, 3 * HW), F32)],
        compiler_params=_params("arbitrary"),
    )(proj, l0, l1, hg_nw, proj, proj, proj_ab, cw, alog, dtb, gd_nw)


def mixer_bwd(proj, proj_ab, l0, l1, hg_nw, cw, alog, dtb, gd_nw, st_a, st_b, t_inv, do_a, do_b, dproj, exch):
    lp = proj.shape[0]
    nc = lp // CHUNK

    def body(xh_ref, l0_ref, l1_ref, hnw_ref, xg_ref, tail_ref, ab_ref, cw_ref, alog_ref, dtb_ref, gnw_ref,
             sta_in_ref, stb_in_ref, tinv_ref, doa_ref, dob_ref, _,
             dx_ref, dl0_ref, dl1_ref, dhnw_ref, dab_ref, dcw_ref, dalog_ref, ddtb_ref, dgnw_ref,
             dsta_ref, dstb_ref, xe_ref, dye_ref, head_ref):
        c = pl.program_id(0)
        cc = nc - 1 - c

        @pl.when(c == 0)
        def _():
            for ref in (dl0_ref, dl1_ref, dhnw_ref, dcw_ref, dalog_ref, ddtb_ref, dgnw_ref, dsta_ref, dstb_ref,
                        head_ref):
                ref[...] = jnp.zeros(ref.shape, F32)

        @pl.when(cc < N_SKIP)
        def _():
            dx_ref[...] = jnp.zeros(dx_ref.shape, dx_ref.dtype)
            dab_ref[...] = jnp.zeros(dab_ref.shape, F32)

        @pl.when(cc >= N_SKIP)
        def _():
            y = _gd_conv(xg_ref, tail_ref, cw_ref, xe_ref, cc == 0)
            _, gd_vjp = jax.vjp(functools.partial(gd_chunk, kept_t=tinv_ref[0]),
                                _conv_heads(y, 0), _conv_heads(y, 1), _conv_heads(y, 2), _heads(xg_ref, 3),
                                ab_ref[...], alog_ref[...], dtb_ref[...], gnw_ref[...], stb_in_ref[0])
            dcq, dck, dcv, dz, dab, dalog, ddtb, dgnw, dstb = gd_vjp((_heads(dob_ref, 0), dstb_ref[...]))
            _, hg_vjp = jax.vjp(hg_chunk, _heads(xh_ref, 0), _heads(xh_ref, 1), _heads(xh_ref, 2), _heads(xh_ref, 3),
                                _heads(l0_ref, 0), _heads(l1_ref, 0), hnw_ref[...], sta_in_ref[0])
            hg_grads = hg_vjp((_heads(doa_ref, 0), dsta_ref[...]))
            dstb_ref[...] = dstb
            dab_ref[...] = dab
            dalog_ref[...] += dalog
            ddtb_ref[...] += ddtb
            dgnw_ref[...] += dgnw
            _put_heads(dx_ref, 4 + 3, dz)
            for i, t in enumerate((dcq, dck, dcv)):
                for j in range(HEADS):
                    dye_ref[0:CHUNK, i * HW + DH * j:i * HW + DH * (j + 1)] = t[j]
            dye_ref[CHUNK:CHUNK + TAIL, :] = head_ref[...]
            head_ref[...] = dye_ref[0:TAIL, :]
            dy = dye_ref[0:CHUNK, :]
            dx = None
            for k in range(CONV_K):
                term = cw_ref[pl.ds(k, 1), :] * dye_ref[pl.ds(CONV_K - 1 - k, CHUNK), :]
                dx = term if dx is None else dx + term
                dcw_ref[pl.ds(k, 1), :] += jnp.sum(dy * xe_ref[pl.ds(TAIL - CONV_K + 1 + k, CHUNK), :],
                                                   axis=0, keepdims=True)
            dx_ref[:, 4 * HW:7 * HW] = dx.astype(dx_ref.dtype)
            for i in range(4):
                _put_heads(dx_ref, i, hg_grads[i])
            _put_heads(dl0_ref, 0, hg_grads[4], accumulate=True)
            _put_heads(dl1_ref, 0, hg_grads[5], accumulate=True)
            dhnw_ref[...] += hg_grads[6]
            dsta_ref[...] = hg_grads[7]

    rc = lambda c: nc - 1 - c
    vec = pl.BlockSpec((1, HW), lambda c: (0, 0))
    one = pl.BlockSpec((1, DH), lambda c: (0, 0))
    one_shape = jax.ShapeDtypeStruct((1, DH), F32)
    vec_shape = jax.ShapeDtypeStruct((1, HW), F32)
    st_spec = pl.BlockSpec((1, HEADS, DH, DH), lambda c: (rc(c), 0, 0, 0))
    do_spec = pl.BlockSpec((CHUNK, HW), lambda c: (rc(c), 0))
    n_in, n_out = 17, 9
    outs = pl.pallas_call(
        host_exchange(body, n_in, n_out, exch, lambda: pl.program_id(0) == 0, lambda: pl.program_id(0) == nc - 1),
        name="mixer_bwd", grid=(nc,), input_output_aliases={n_in - 1: 0}, compiler_params=_params("arbitrary"),
        **_hosted(exch,
                  [pl.BlockSpec((CHUNK, 4 * HW), lambda c: (rc(c), 0)), vec, vec, one,
                   pl.BlockSpec((CHUNK, 4 * HW), lambda c: (rc(c), 1)),
                   pl.BlockSpec((TAIL, 4 * HW), lambda c: (jnp.maximum(rc(c) * (CHUNK // TAIL) - 1, 0), 1)),
                   pl.BlockSpec((CHUNK, DH), lambda c: (rc(c), 0)),
                   pl.BlockSpec((CONV_K, 3 * HW), lambda c: (0, 0)), one, one, one,
                   st_spec, st_spec, pl.BlockSpec((1, HEADS, CHUNK, CHUNK), lambda c: (rc(c), 0, 0, 0)),
                   do_spec, do_spec, pl.BlockSpec(memory_space=pl.ANY)],
                  [pl.BlockSpec((CHUNK, 8 * HW), lambda c: (rc(c), 0)), vec, vec, one,
                   pl.BlockSpec((CHUNK, DH), lambda c: (rc(c), 0)),
                   pl.BlockSpec((CONV_K, 3 * HW), lambda c: (0, 0)), one, one, one],
                  [jax.ShapeDtypeStruct((lp, W_MAIN), BF16), vec_shape, vec_shape, one_shape,
                   jax.ShapeDtypeStruct((lp, DH), F32), jax.ShapeDtypeStruct((CONV_K, 3 * HW), F32),
                   one_shape, one_shape, one_shape],
                  [pltpu.VMEM((HEADS, DH, DH), F32), pltpu.VMEM((HEADS, DH, DH), F32),
                   pltpu.VMEM((TAIL + CHUNK, 3 * HW), F32), pltpu.VMEM((CHUNK + TAIL, 3 * HW), F32),
                   pltpu.VMEM((TAIL, 3 * HW), F32)]),
    )(proj, l0, l1, hg_nw, proj, proj, proj_ab, cw, alog, dtb, gd_nw, st_a, st_b, t_inv, do_a, do_b, dproj,
      *exch.arrays)
    return outs[:n_out], outs[n_out:]


def matmul(name, a, b, mode, m, n, k, out_dtype, tm=512, tn=1024, tk=None, a_off=(0, 0), b_off=(0, 0), exch=None,
           rows_outer=False):
    tm, tn = min(tm, m), min(tn, n)
    tk = k if tk is None else min(tk, k)
    assert m % tm == 0 and n % tn == 0 and k % tk == 0, (name, m, n, k, tm, tn, tk)
    gi, gj, gk = m // tm, n // tn, k // tk
    grid = (gi, gj, gk) if rows_outer else (gj, gi, gk)

    def at(index):
        return (lambda i, j, kk: index(j, i, kk)) if rows_outer else index

    if mode == "nn":
        a_spec = pl.BlockSpec((tm, tk), at(lambda j, i, kk: (i + a_off[0], kk + a_off[1])))
        b_spec = pl.BlockSpec((tk, tn), at(lambda j, i, kk: (kk + b_off[0], j + b_off[1])))
        dims = _NN
    elif mode == "nt":
        a_spec = pl.BlockSpec((tm, tk), at(lambda j, i, kk: (i + a_off[0], kk + a_off[1])))
        b_spec = pl.BlockSpec((tn, tk), at(lambda j, i, kk: (j + b_off[0], kk + b_off[1])))
        dims = _NT
    else:
        a_spec = pl.BlockSpec((tk, tm), at(lambda j, i, kk: (kk + a_off[0], i + a_off[1])))
        b_spec = pl.BlockSpec((tk, tn), at(lambda j, i, kk: (kk + b_off[0], j + b_off[1])))
        dims = _TN

    def body(a_ref, b_ref, o_ref, *acc):
        d = lax.dot_general(a_ref[...].astype(BF16), b_ref[...].astype(BF16), (dims, ((), ())),
                            preferred_element_type=F32)
        if gk == 1:
            o_ref[...] = d.astype(o_ref.dtype)
        else:
            acc_ref, = acc
            kk = pl.program_id(2)

            @pl.when(kk == 0)
            def _():
                acc_ref[...] = d

            @pl.when(kk > 0)
            def _():
                acc_ref[...] += d

            @pl.when(kk == gk - 1)
            def _():
                o_ref[...] = acc_ref[...].astype(o_ref.dtype)

    o_spec = pl.BlockSpec((tm, tn), at(lambda j, i, kk: (i, j)))
    o_shape = jax.ShapeDtypeStruct((m, n), out_dtype)
    scratch = [] if gk == 1 else [pltpu.VMEM((tm, tn), F32)]
    if exch is None:
        return pl.pallas_call(
            body, name=name, grid=grid, in_specs=[a_spec, b_spec], out_specs=o_spec, out_shape=o_shape,
            scratch_shapes=scratch, compiler_params=_params("parallel", "parallel", "arbitrary"),
        )(a, b)
    step = lambda: (pl.program_id(0), pl.program_id(1), pl.program_id(2))
    first = lambda: jnp.logical_and(jnp.logical_and(step()[0] == 0, step()[1] == 0), step()[2] == 0)
    last = lambda: jnp.logical_and(jnp.logical_and(step()[0] == grid[0] - 1, step()[1] == grid[1] - 1),
                                   step()[2] == gk - 1)
    outs = pl.pallas_call(
        host_exchange(body, 2, 1, exch, first, last), name=name, grid=grid,
        compiler_params=_params("arbitrary", "arbitrary", "arbitrary"),
        **_hosted(exch, [a_spec, b_spec], [o_spec], [o_shape], scratch),
    )(a, b, *exch.arrays)
    return outs[0], outs[1:]


def _swiglu(gu):
    return _silu(gu[:, :FF_BLK]) * gu[:, FF_BLK:]


def ffn_in_fused(xn2, w_gu, tm=512):
    l = xn2.shape[0]
    tn = 2 * FF_BLK

    def body(a_ref, b_ref, gu_ref, act_ref):
        gu = lax.dot_general(a_ref[...], b_ref[...], (_NN, ((), ())), preferred_element_type=F32)
        gu_ref[...] = gu
        act_ref[...] = _swiglu(gu).astype(act_ref.dtype)

    return pl.pallas_call(
        body, name="ffn_in", grid=(2 * D_FF // tn, l // tm),
        in_specs=[pl.BlockSpec((tm, D), lambda j, i: (i, 0)), pl.BlockSpec((D, tn), lambda j, i: (0, j))],
        out_specs=[pl.BlockSpec((tm, tn), lambda j, i: (i, j)), pl.BlockSpec((tm, FF_BLK), lambda j, i: (i, j))],
        out_shape=[jax.ShapeDtypeStruct((l, 2 * D_FF), F32), jax.ShapeDtypeStruct((l, D_FF), BF16)],
        compiler_params=_params("parallel", "parallel"),
    )(xn2, w_gu)


def d_act_fused(dh2, w_fo, gu, tm=512):
    l = dh2.shape[0]

    def body(a_ref, b_ref, gu_ref, o_ref):
        da = lax.dot_general(a_ref[...], b_ref[...], (_NT, ((), ())), preferred_element_type=F32)
        _, vjp = jax.vjp(_swiglu, gu_ref[...])
        dgu, = vjp(da)
        o_ref[...] = dgu.astype(o_ref.dtype)

    return pl.pallas_call(
        body, name="d_act", grid=(D_FF // FF_BLK, l // tm),
        in_specs=[pl.BlockSpec((tm, D), lambda j, i: (i, 0)), pl.BlockSpec((FF_BLK, D), lambda j, i: (j, 0)),
                  pl.BlockSpec((tm, 2 * FF_BLK), lambda j, i: (i, j))],
        out_specs=pl.BlockSpec((tm, 2 * FF_BLK), lambda j, i: (i, j)),
        out_shape=jax.ShapeDtypeStruct((l, 2 * D_FF), BF16),
        compiler_params=_params("parallel", "parallel"),
    )(dh2, w_fo, gu)


TR = 256
N_PRE = PRE // TR


def _row(width, off=0, col=0):
    return pl.BlockSpec((TR, width), lambda i: (i + off, col))


def _pre_spec():
    return pl.BlockSpec((TR, D), lambda i: (jnp.minimum(i, N_PRE - 1), 0))


def _seq_spec(width=D, col=0):
    return pl.BlockSpec((TR, width), lambda i: (jnp.maximum(i - N_PRE, 0), col))


def _vec_spec(width=D):
    return pl.BlockSpec((1, width), lambda i: (0, 0))


def norm1_fwd(pre, x, w):
    lp = PRE + x.shape[0]

    def body(pre_ref, x_ref, w_ref, o_ref):
        h = jnp.where(pl.program_id(0) < N_PRE, pre_ref[...], x_ref[...])
        o_ref[...] = _rms_norm(h, w_ref[...]).astype(o_ref.dtype)

    return pl.pallas_call(
        body, name="norm1_fwd", grid=(lp // TR,),
        in_specs=[_pre_spec(), _seq_spec(), _vec_spec()],
        out_specs=_row(D), out_shape=jax.ShapeDtypeStruct((lp, D), BF16),
        compiler_params=_params("parallel"),
    )(pre, x, w)


def norm1_bwd(pre, x, w, dxn, dxn_ab, dh1):
    l = x.shape[0]
    lp = PRE + l

    def body(pre_ref, x_ref, w_ref, dxn_ref, dxn_ab_ref, dh1_ref, dx_ref, dpre_ref, dw_ref):
        i = pl.program_id(0)
        h = jnp.where(i < N_PRE, pre_ref[...], x_ref[...])
        _, vjp = jax.vjp(_rms_norm, h, w_ref[...])
        dh, dw = vjp(dxn_ref[...] + dxn_ab_ref[...])

        @pl.when(i == 0)
        def _():
            dw_ref[...] = dw

        @pl.when(i > 0)
        def _():
            dw_ref[...] += dw

        @pl.when(i < N_PRE)
        def _():
            dpre_ref[...] = dh

        @pl.when(i >= N_PRE)
        def _():
            dx_ref[...] = dh + dh1_ref[...]

    return pl.pallas_call(
        body, name="norm1_bwd", grid=(lp // TR,),
        in_specs=[_pre_spec(), _seq_spec(), _vec_spec(), _row(D), _row(D), _seq_spec()],
        out_specs=[_seq_spec(), _pre_spec(), _vec_spec()],
        out_shape=[jax.ShapeDtypeStruct((l, D), F32), jax.ShapeDtypeStruct((PRE, D), F32),
                   jax.ShapeDtypeStruct((1, D), F32)],
        compiler_params=_params("arbitrary"),
    )(pre, x, w, dxn, dxn_ab, dh1)


def _merge(gates, ya, yb):
    return _sigmoid(gates[:, :D]) * ya + _sigmoid(gates[:, D:]) * yb


def merge_fwd(proj, ya, yb):
    l = ya.shape[0]

    def body(g_ref, ya_ref, yb_ref, o_ref):
        o_ref[...] = _merge(g_ref[...], ya_ref[...], yb_ref[...]).astype(o_ref.dtype)

    return pl.pallas_call(
        body, name="merge_fwd", grid=(l // TR,),
        in_specs=[_row(2 * D, N_PRE, 2), _row(D), _row(D)],
        out_specs=_row(D), out_shape=jax.ShapeDtypeStruct((l, D), BF16),
        compiler_params=_params("parallel"),
    )(proj, ya, yb)


def merge_bwd(proj, ya, yb, dmerged):
    l = ya.shape[0]
    lp = PRE + l

    def body(g_ref, ya_ref, yb_ref, dm_ref, dg_ref, dya_ref, dyb_ref):
        i = pl.program_id(0)

        @pl.when(i < N_PRE)
        def _():
            dg_ref[...] = jnp.zeros(dg_ref.shape, dg_ref.dtype)
            dya_ref[...] = jnp.zeros(dya_ref.shape, dya_ref.dtype)
            dyb_ref[...] = jnp.zeros(dyb_ref.shape, dyb_ref.dtype)

        @pl.when(i >= N_PRE)
        def _():
            _, vjp = jax.vjp(_merge, g_ref[...], ya_ref[...], yb_ref[...])
            dg, dya, dyb = vjp(dm_ref[...])
            dg_ref[...] = dg.astype(dg_ref.dtype)
            dya_ref[...] = dya.astype(dya_ref.dtype)
            dyb_ref[...] = dyb.astype(dyb_ref.dtype)

    return pl.pallas_call(
        body, name="merge_bwd", grid=(lp // TR,),
        in_specs=[pl.BlockSpec((TR, 2 * D), lambda i: (jnp.maximum(i, N_PRE), 2)), _seq_spec(), _seq_spec(),
                  _seq_spec()],
        out_specs=[_row(2 * D, 0, 2), _row(D), _row(D)],
        out_shape=[jax.ShapeDtypeStruct((lp, W_MAIN), BF16), jax.ShapeDtypeStruct((lp, D), BF16),
                   jax.ShapeDtypeStruct((lp, D), BF16)],
        compiler_params=_params("parallel"),
    )(proj, ya, yb, dmerged)


def mix_out_norm2(merged, wo, x, w, tm=512):
    l = x.shape[0]

    def body(a_ref, b_ref, x_ref, w_ref, h_ref, o_ref):
        h = x_ref[...] + lax.dot_general(a_ref[...], b_ref[...], (_NN, ((), ())), preferred_element_type=F32)
        h_ref[...] = h
        o_ref[...] = _rms_norm(h, w_ref[...]).astype(o_ref.dtype)

    row = pl.BlockSpec((tm, D), lambda i: (i, 0))
    return pl.pallas_call(
        body, name="mix_out", grid=(l // tm,),
        in_specs=[row, pl.BlockSpec((D, D), lambda i: (0, 0)), row, _vec_spec()],
        out_specs=[row, row],
        out_shape=[jax.ShapeDtypeStruct((l, D), F32), jax.ShapeDtypeStruct((l, D), BF16)],
        compiler_params=_params("parallel"),
    )(merged, wo, x, w)


def norm2_bwd(h1, w, dxn2, dh2):
    l = h1.shape[0]

    def body(h_ref, w_ref, dxn_ref, dh2_ref, dh1_ref, dh1b_ref, dw_ref):
        i = pl.program_id(0)
        _, vjp = jax.vjp(_rms_norm, h_ref[...], w_ref[...])
        dh, dw = vjp(dxn_ref[...])
        dh1 = dh + dh2_ref[...]
        dh1_ref[...] = dh1
        dh1b_ref[...] = dh1.astype(BF16)

        @pl.when(i == 0)
        def _():
            dw_ref[...] = dw

        @pl.when(i > 0)
        def _():
            dw_ref[...] += dw

    return pl.pallas_call(
        body, name="norm2_bwd", grid=(l // TR,),
        in_specs=[_row(D), _vec_spec(), _row(D), _row(D)],
        out_specs=[_row(D), _row(D), _vec_spec()],
        out_shape=[jax.ShapeDtypeStruct((l, D), F32), jax.ShapeDtypeStruct((l, D), BF16),
                   jax.ShapeDtypeStruct((1, D), F32)],
        compiler_params=_params("arbitrary"),
    )(h1, w, dxn2, dh2)


def _loss_rows(h1, f, w, tgt):
    y = _rms_norm(h1 + f, w)
    err = (y - tgt) * (y - tgt)
    return 0.5 * jnp.sum(jnp.mean(err, axis=-1, keepdims=True), axis=0, keepdims=True)


def loss_head(h1, f, w, tgt):
    l = h1.shape[0]

    def body(h_ref, f_ref, w_ref, t_ref, loss_ref, dh_ref, dhb_ref, dw_ref):
        i = pl.program_id(0)
        loss, vjp = jax.vjp(_loss_rows, h_ref[...], f_ref[...], w_ref[...], t_ref[...])
        dh, _, dw, _ = vjp(jnp.ones((1, 1), F32))
        dh_ref[...] = dh
        dhb_ref[...] = dh.astype(BF16)
        loss = jnp.broadcast_to(loss, (1, LANES))

        @pl.when(i == 0)
        def _():
            dw_ref[...] = dw
            loss_ref[...] = loss

        @pl.when(i > 0)
        def _():
            dw_ref[...] += dw
            loss_ref[...] += loss

    return pl.pallas_call(
        body, name="loss_head", grid=(l // TR,),
        in_specs=[_row(D), _row(D), _vec_spec(), _row(D)],
        out_specs=[_vec_spec(LANES), _row(D), _row(D), _vec_spec()],
        out_shape=[jax.ShapeDtypeStruct((1, LANES), F32), jax.ShapeDtypeStruct((l, D), F32),
                   jax.ShapeDtypeStruct((l, D), BF16), jax.ShapeDtypeStruct((1, D), F32)],
        compiler_params=_params("arbitrary"),
    )(h1, f, w, tgt)


def _adamw(w, g, m, v):
    m = ADAM_B1 * m + (1.0 - ADAM_B1) * g
    v = ADAM_B2 * v + (1.0 - ADAM_B2) * (g * g)
    m_hat = m / (1.0 - ADAM_B1 ** ADAM_STEP)
    v_hat = v / (1.0 - ADAM_B2 ** ADAM_STEP)
    delta = -ADAM_LR * (m_hat / (jnp.sqrt(v_hat) + ADAM_EPS) + ADAM_WD * w)
    return delta, m, v


def adamw_shard(name, w, m, v, sums, recv, tile):
    r, c = w.shape
    tr, tc = tile
    assert r % tr == 0 and c % tc == 0

    def body(chip_ref, w_ref, m_ref, v_ref, own_ref, r0_ref, r1_ref, r2_ref, g_ref, d_ref, nm_ref, nv_ref):
        g = own_ref[...].astype(F32) + r0_ref[...].astype(F32)
        g = g + r1_ref[...].astype(F32)
        g = g + r2_ref[...].astype(F32)
        d, nm, nv = _adamw(w_ref[...], g, m_ref[...], v_ref[...])
        g_ref[...] = g
        d_ref[...] = d
        nm_ref[...] = nm
        nv_ref[...] = nv

    blk = pl.BlockSpec((tr, tc), lambda i, j, chip: (i, j))
    part = lambda s: pl.BlockSpec((None, tr, tc), lambda i, j, chip: (s, i, j))
    own = pl.BlockSpec((None, tr, tc), lambda i, j, chip: (chip[0], i, j))
    shape = jax.ShapeDtypeStruct((r, c), F32)
    my_chip = (2 * lax.axis_index("x") + lax.axis_index("y")).astype(jnp.int32).reshape(1)
    return pl.pallas_call(
        body, name=name, out_shape=[shape, shape, shape, shape],
        grid_spec=pltpu.PrefetchScalarGridSpec(num_scalar_prefetch=1, grid=(r // tr, c // tc),
                                               in_specs=[blk, blk, blk, own, part(0), part(1), part(2)],
                                               out_specs=[blk, blk, blk, blk]),
        compiler_params=_params("parallel", "parallel"),
    )(my_chip, w, m, v, sums, recv, recv, recv)


def adamw_small(w, g, m, v):
    def body(w_ref, g_ref, m_ref, v_ref, d_ref, nm_ref, nv_ref):
        d, nm, nv = _adamw(w_ref[...], g_ref[...], m_ref[...], v_ref[...])
        d_ref[...] = d
        nm_ref[...] = nm
        nv_ref[...] = nv

    shape = jax.ShapeDtypeStruct(w.shape, F32)
    return pl.pallas_call(body, name="adamw_small", out_shape=[shape, shape, shape])(w, g, m, v)


def pair_add(name, a, b, tile):
    q, r, c = b.shape
    tr, tc = tile
    assert r % tr == 0 and c % tc == 0

    def body(core_ref, a_ref, b_ref, o_ref):
        o_ref[...] = (a_ref[...].astype(F32) + b_ref[...].astype(F32)).astype(o_ref.dtype)

    blk = pl.BlockSpec((None, tr, tc), lambda s, i, j, core: (s, i, j))
    mine = pl.BlockSpec((None, None, tr, tc), lambda s, i, j, core: (s, core[0], i, j))
    return pl.pallas_call(
        body, name=name, out_shape=jax.ShapeDtypeStruct((q, r, c), BF16),
        grid_spec=pltpu.PrefetchScalarGridSpec(num_scalar_prefetch=1, grid=(q, r // tr, c // tc),
                                               in_specs=[mine, blk], out_specs=blk),
        compiler_params=_params("parallel", "parallel", "parallel"),
    )(lax.axis_index("c").astype(jnp.int32).reshape(1), a, b)


def _w_in_pieces():
    c = W_IN // N_DEV
    ranges = ((0, 8 * HW, 0, 0), (8 * HW, 8 * HW + 2 * HEADS, 1, 8 * HW), (8 * HW + 2 * HEADS, W_IN, 0, 2 * HEADS))
    out = []
    for d in range(N_DEV):
        for lo, hi, target, shift in ranges:
            s, e = max(lo, c * d), min(hi, c * (d + 1))
            if s < e:
                out.append((d, s - c * d, e - s, target, s - shift))
    return out


def _ffn_in_pieces():
    c = 2 * D_FF // N_DEV
    out = []
    for d in range(N_DEV):
        s = c * d
        while s < c * (d + 1):
            e = min((s // FF_BLK + 1) * FF_BLK, c * (d + 1))
            half, blk = divmod(s // FF_BLK, D_FF // FF_BLK)
            out.append((d, s - c * d, e - s, 0, (2 * blk + half) * FF_BLK + s % FF_BLK))
            s = e
    return out


def _plain_pieces(c):
    return [(d, 0, c, 0, c * d) for d in range(N_DEV)]


def shards_to_cols(name, g, widths, pieces, tr):
    _, r, c = g.shape
    covered = [sum(p[2] for p in pieces if p[3] == t) for t in range(len(widths))]

    def body(g_ref, *outs):
        for t, o in enumerate(outs):
            if covered[t] < widths[t]:
                o[...] = jnp.zeros(o.shape, o.dtype)
        for d, s, w, t, ts in pieces:
            outs[t][:, ts:ts + w] = g_ref[d, :, s:s + w]

    return pl.pallas_call(
        body, name=name, grid=(r // tr,),
        in_specs=[pl.BlockSpec((N_DEV, tr, c), lambda i: (0, i, 0))],
        out_specs=[pl.BlockSpec((tr, w), lambda i: (i, 0)) for w in widths],
        out_shape=[jax.ShapeDtypeStruct((r, w), g.dtype) for w in widths],
        compiler_params=_params("parallel"),
    )(g)


def cols_to_shards(name, srcs, c, pieces, tr):
    r = srcs[0].shape[0]

    def body(*refs):
        o = refs[-1]
        for d, s, w, t, ts in pieces:
            o[d, :, s:s + w] = refs[t][:, ts:ts + w]

    return pl.pallas_call(
        body, name=name, grid=(r // tr,),
        in_specs=[pl.BlockSpec((tr, t.shape[1]), lambda i: (i, 0)) for t in srcs],
        out_specs=pl.BlockSpec((N_DEV, tr, c), lambda i: (0, i, 0)),
        out_shape=jax.ShapeDtypeStruct((N_DEV, r, c), srcs[0].dtype),
        compiler_params=_params("parallel"),
    )(*srcs)


def rows_to_shards(name, srcs, c, pieces, tc):
    r = srcs[0].shape[1]

    def body(*refs):
        o = refs[-1]
        for d, s, w, t, ts in pieces:
            o[d, s:s + w, :] = refs[t][ts:ts + w, :]

    return pl.pallas_call(
        body, name=name, grid=(r // tc,),
        in_specs=[pl.BlockSpec((t.shape[0], tc), lambda i: (0, i)) for t in srcs],
        out_specs=pl.BlockSpec((N_DEV, c, tc), lambda i: (0, 0, i)),
        out_shape=jax.ShapeDtypeStruct((N_DEV, c, r), srcs[0].dtype),
        compiler_params=_params("parallel"),
    )(*srcs)


def _place():
    return lax.axis_index("x"), lax.axis_index("y"), lax.axis_index("c")


def _dev(px, py, pc):
    return 4 * px + 2 * py + pc


class Exchange:
    def __init__(self, arrays, out_shapes, sems, start, finish):
        self.arrays, self.out_shapes, self.sems, self.start, self.finish = arrays, out_shapes, sems, start, finish


def gather_exchange(arrays):
    n = len(arrays)

    def plan(ins, outs, sems):
        send_sems, recv_sems, local_sems = sems
        x, y, c = _place()
        me, sibling = (x, y, c), (x, y, 1 - c)
        chips = [(1 - x, y), (x, 1 - y), (1 - x, 1 - y)]

        def copy(a, k, block, to, src=None):
            dst = outs[a].at[_dev(*block)]
            return pltpu.make_async_remote_copy(
                src_ref=dst if src is None else src, dst_ref=dst, send_sem=send_sems.at[a, k],
                recv_sem=recv_sems.at[a, k], device_id=to, device_id_type=MESH)

        mine = [pltpu.make_async_copy(ins[a], outs[a].at[_dev(*me)], local_sems.at[a]) for a in range(n)]
        first = []
        for a in range(n):
            first.append(copy(a, 0, me, sibling, src=ins[a]))
            first += [copy(a, 1 + j, me, (*chip, c), src=ins[a]) for j, chip in enumerate(chips)]
        return me, sibling, chips, c, copy, mine, first

    def start(ins, outs, sems):
        *_, mine, first = plan(ins, outs, sems)
        for cp in mine + first:
            cp.start()

    def finish(ins, outs, sems):
        me, sibling, chips, c, copy, mine, first = plan(ins, outs, sems)
        passed = []
        for j, chip in enumerate(chips):
            for a in range(n):
                copy(a, 1 + j, (*chip, c), me).wait_recv()
                fwd = copy(a, 4 + j, (*chip, c), sibling)
                fwd.start()
                passed.append(fwd)
        for a in range(n):
            copy(a, 0, sibling, me).wait_recv()
        for j, chip in enumerate(chips):
            for a in range(n):
                copy(a, 4 + j, (*chip, 1 - c), me).wait_recv()
        for cp in first + passed:
            cp.wait_send()
        for cp in mine:
            cp.wait()

    return Exchange(arrays, [jax.ShapeDtypeStruct((N_DEV,) + t.shape, t.dtype) for t in arrays],
                    [pltpu.SemaphoreType.DMA((n, 7)), pltpu.SemaphoreType.DMA((n, 7)), pltpu.SemaphoreType.DMA((n,))],
                    start, finish)


def core_exchange(arrays):
    n = len(arrays)

    def copies(ins, outs, sems):
        send_sems, recv_sems = sems
        x, y, c = _place()
        return [pltpu.make_async_remote_copy(
            src_ref=ins[a].at[q, 1 - c], dst_ref=outs[a].at[q], send_sem=send_sems.at[a, q],
            recv_sem=recv_sems.at[a, q], device_id=(x, y, 1 - c), device_id_type=MESH)
            for a in range(n) for q in range(4)]

    def start(ins, outs, sems):
        for cp in copies(ins, outs, sems):
            cp.start()

    def finish(ins, outs, sems):
        for cp in copies(ins, outs, sems):
            cp.wait()

    return Exchange(arrays, [jax.ShapeDtypeStruct((4,) + t.shape[2:], t.dtype) for t in arrays],
                    [pltpu.SemaphoreType.DMA((n, 4)), pltpu.SemaphoreType.DMA((n, 4))], start, finish)


def chips_exchange(arrays):
    n = len(arrays)

    def copies(ins, outs, sems):
        send_sems, recv_sems = sems
        x, y, c = _place()
        chips = [(1 - x, y), (x, 1 - y), (1 - x, 1 - y)]
        return [pltpu.make_async_remote_copy(
            src_ref=ins[a].at[2 * qx + qy], dst_ref=outs[a].at[j], send_sem=send_sems.at[a, j],
            recv_sem=recv_sems.at[a, j], device_id=(qx, qy, c), device_id_type=MESH)
            for a in range(n) for j, (qx, qy) in enumerate(chips)]

    def start(ins, outs, sems):
        for cp in copies(ins, outs, sems):
            cp.start()

    def finish(ins, outs, sems):
        for cp in copies(ins, outs, sems):
            cp.wait()

    return Exchange(arrays, [jax.ShapeDtypeStruct((3,) + t.shape[1:], t.dtype) for t in arrays],
                    [pltpu.SemaphoreType.DMA((n, 3)), pltpu.SemaphoreType.DMA((n, 3))], start, finish)


def run_exchange(name, exch):
    n_in, n_out = len(exch.arrays), len(exch.out_shapes)
    any_spec = pl.BlockSpec(memory_space=pl.ANY)

    def body(*refs):
        ins, outs, sems = refs[:n_in], refs[n_in:n_in + n_out], refs[n_in + n_out:]
        exch.start(ins, outs, sems)
        exch.finish(ins, outs, sems)

    return pl.pallas_call(
        body, name=name, in_specs=[any_spec] * n_in, out_specs=[any_spec] * n_out, out_shape=exch.out_shapes,
        scratch_shapes=exch.sems,
    )(*exch.arrays)


def host_exchange(body, n_in, n_out, exch, first, last):
    ne_in, ne_out, ne_sem = len(exch.arrays), len(exch.out_shapes), len(exch.sems)

    def wrapped(*refs):
        ins, refs = refs[:n_in], refs[n_in:]
        e_ins, refs = refs[:ne_in], refs[ne_in:]
        outs, refs = refs[:n_out], refs[n_out:]
        e_outs, refs = refs[:ne_out], refs[ne_out:]
        scratch, e_sems = refs[:len(refs) - ne_sem], refs[len(refs) - ne_sem:]

        @pl.when(first())
        def _():
            exch.start(e_ins, e_outs, e_sems)

        body(*ins, *outs, *scratch)

        @pl.when(last())
        def _():
            exch.finish(e_ins, e_outs, e_sems)

    return wrapped


def _hosted(exch, in_specs, out_specs, out_shape, scratch_shapes):
    any_spec = pl.BlockSpec(memory_space=pl.ANY)
    return dict(in_specs=list(in_specs) + [any_spec] * len(exch.arrays),
                out_specs=list(out_specs) + [any_spec] * len(exch.out_shapes),
                out_shape=list(out_shape) + list(exch.out_shapes),
                scratch_shapes=list(scratch_shapes) + list(exch.sems))


def allreduce_small(buf):
    r = buf.shape[0]
    vmem = pl.BlockSpec(memory_space=pltpu.VMEM)

    def body(buf_ref, out_ref, all_ref, send_sems, recv_sems):
        x, y, c = _place()
        me = _dev(x, y, c)
        copies = []
        for k in range(1, N_DEV):
            peer = (x ^ (k >> 2), y ^ ((k >> 1) & 1), c ^ (k & 1))
            copies.append(pltpu.make_async_remote_copy(
                src_ref=buf_ref, dst_ref=all_ref.at[me], send_sem=send_sems.at[k - 1], recv_sem=recv_sems.at[k - 1],
                device_id=peer, device_id_type=MESH))
        for cp in copies:
            cp.start()
        all_ref[me] = buf_ref[...]
        for cp in copies:
            cp.wait()
        total = all_ref[0]
        for d in range(1, N_DEV):
            total = total + all_ref[d]
        out_ref[...] = total

    return pl.pallas_call(
        body, name="allreduce_small", in_specs=[vmem], out_specs=vmem,
        out_shape=jax.ShapeDtypeStruct((r, LANES), F32),
        scratch_shapes=[pltpu.VMEM((N_DEV, r, LANES), F32), pltpu.SemaphoreType.DMA((N_DEV - 1,)),
                        pltpu.SemaphoreType.DMA((N_DEV - 1,))],
    )(buf)


def _cols_from_shards(g):
    return jnp.transpose(g, (1, 0, 2)).reshape(g.shape[1], N_DEV * g.shape[2])


def _pack(t):
    flat = t.reshape(-1)
    return jnp.pad(flat, (0, -flat.size % (8 * LANES))).reshape(-1, LANES)


def _pad_lanes(t):
    t = t.reshape(1, -1)
    return jnp.pad(t, ((0, 0), (0, LANES - t.shape[1])))


def kernel(x, meta_tokens, lb_logits, mix_norm_w, w_in, hg_norm_w, gd_conv_w, gd_a_log, gd_dt_bias, gd_norm_w, w_branch_a, w_branch_b, w_out, ffn_norm_w, w_ffn_in, w_ffn_out, final_norm_w, loss_target, m_meta_tokens, m_lb_logits, m_mix_norm_w, m_w_in, m_hg_norm_w, m_gd_conv_w, m_gd_a_log, m_gd_dt_bias, m_gd_norm_w, m_w_branch_a, m_w_branch_b, m_w_out, m_ffn_norm_w, m_w_ffn_in, m_w_ffn_out, m_final_norm_w, v_meta_tokens, v_lb_logits, v_mix_norm_w, v_w_in, v_hg_norm_w, v_gd_conv_w, v_gd_a_log, v_gd_dt_bias, v_gd_norm_w, v_w_branch_a, v_w_branch_b, v_w_out, v_ffn_norm_w, v_w_ffn_in, v_w_ffn_out, v_final_norm_w):
    xs = x[0]
    tgt = loss_target[0]
    l = xs.shape[0]
    lp = PRE + l
    me = _dev(*_place())

    big = [w_in[0], w_branch_a[0], w_branch_b[0], w_out[0], w_ffn_in[0], w_ffn_out[0]]
    big16 = [t.astype(BF16) for t in big]
    g_in, g_meta, g_conv = run_exchange("gather_first", gather_exchange([big16[0], meta_tokens, gd_conv_w[0]]))
    w_main, w_ab = shards_to_cols("w_in_cols", g_in, [W_MAIN, LANES], _w_in_pieces(), 256)
    meta_full = _cols_from_shards(g_meta)
    cw = _cols_from_shards(g_conv)
    pre = jnp.concatenate([jnp.zeros((PRE - N_META, D), F32), meta_full], axis=0)
    l0, l1 = lb_logits[0:1], lb_logits[1:2]
    alog, dtb = _pad_lanes(gd_a_log), _pad_lanes(gd_dt_bias)

    xn = norm1_fwd(pre, xs, mix_norm_w)
    proj, (g_ba, g_bb, g_out, g_ffi, g_ffo) = matmul("proj_main", xn, w_main, "nn", lp, W_MAIN, D, F32,
                                                      exch=gather_exchange(big16[1:]))
    wa, = shards_to_cols("w_branch_a_cols", g_ba, [D], _plain_pieces(D // N_DEV), 256)
    wb, = shards_to_cols("w_branch_b_cols", g_bb, [D], _plain_pieces(D // N_DEV), 256)
    wo = g_out.reshape(D, D)
    w_gu, = shards_to_cols("w_ffn_in_cols", g_ffi, [2 * D_FF], _ffn_in_pieces(), 256)
    w_fo = g_ffo.reshape(D_FF, D)
    proj_ab = matmul("proj_ab", xn, w_ab, "nn", lp, LANES, D, F32)
    o_a, st_a, o_b, st_b, t_inv = mixer_fwd(proj, proj_ab, l0, l1, hg_norm_w, cw, alog, dtb, gd_norm_w)
    ya = matmul("branch_a", o_a, wa, "nn", l, D, HW, F32, a_off=(PRE // 512, 0))
    yb = matmul("branch_b", o_b, wb, "nn", l, D, HW, F32, a_off=(PRE // 512, 0))
    merged = merge_fwd(proj, ya, yb)
    h1, xn2 = mix_out_norm2(merged, wo, xs, ffn_norm_w)
    gu, act = ffn_in_fused(xn2, w_gu)
    f = matmul("ffn_out", act, w_fo, "nn", l, D, D_FF, F32, tn=512)
    loss_part, dh2, dh2_b, d_final_w = loss_head(h1, f, final_norm_w.reshape(1, D), tgt)

    tiles = {"w_in": (W_IN // N_DEV, 256), "w_branch_a": (256, D // N_DEV), "w_branch_b": (256, D // N_DEV),
             "w_out": (64, D), "w_ffn_in": (256, 2 * D_FF // N_DEV), "w_ffn_out": (176, D)}

    def chip_sums(tag, parts, host=None):
        blocks = [p.reshape((4, 2) + p.shape[1:]) for p in parts.values()]
        exch = core_exchange(blocks)
        hosted, from_core = (None, run_exchange("exchange_core_" + tag, exch)) if host is None else host(exch)
        return hosted, {nm: pair_add("pair_add_" + nm, p, r, tiles[nm]) for (nm, p, r) in zip(parts, blocks, from_core)}

    dgu = d_act_fused(dh2_b, w_fo, gu)
    dw_fo = matmul("dw_ffn_out", act, dh2_b, "tn", D_FF, D, l, BF16, tm=512, tn=512)
    dxn2 = matmul("d_xn2", dgu, w_gu, "nt", l, D, 2 * D_FF, F32, tn=256, rows_outer=True)
    dw_gu = matmul("dw_ffn_in", xn2, dgu, "tn", D, 2 * D_FF, l, BF16, tm=512, tn=512)
    dh1, dh1_b, d_ffn_norm_w = norm2_bwd(h1, ffn_norm_w, dxn2, dh2)
    dmerged, sums = chip_sums("ffn", {
        "w_ffn_in": cols_to_shards("dw_ffn_in_shards", [dw_gu], 2 * D_FF // N_DEV, _ffn_in_pieces(), 256),
        "w_ffn_out": dw_fo.reshape(N_DEV, D_FF // N_DEV, D)},
        host=lambda exch: matmul("d_merged", dh1_b, wo, "nt", l, D, D, F32, exch=exch))
    dw_o = matmul("dw_out", merged, dh1_b, "tn", D, D, l, BF16, tm=512, tn=512)
    dproj, dya, dyb = merge_bwd(proj, ya, yb, dmerged)
    do_a = matmul("d_o_a", dya, wa, "nt", lp, HW, D, F32)
    do_b = matmul("d_o_b", dyb, wb, "nt", lp, HW, D, F32)
    dw_a = matmul("dw_branch_a", o_a, dya, "tn", HW, D, lp, BF16, tm=512, tn=512)
    dw_b = matmul("dw_branch_b", o_b, dyb, "tn", HW, D, lp, BF16, tm=512, tn=512)
    sums.update(chip_sums("mix", {
        "w_branch_a": cols_to_shards("dw_branch_a_shards", [dw_a], D // N_DEV, _plain_pieces(D // N_DEV), 256),
        "w_branch_b": cols_to_shards("dw_branch_b_shards", [dw_b], D // N_DEV, _plain_pieces(D // N_DEV), 256),
        "w_out": dw_o.reshape(N_DEV, D // N_DEV, D)})[1])
    ffn_names, mix_names = ["w_ffn_in", "w_ffn_out"], ["w_branch_a", "w_branch_b", "w_out"]
    (dproj, dl0, dl1, d_hg_nw, dproj_ab, d_conv, d_alog, d_dtb, d_gd_nw), from_chips_early = mixer_bwd(
        proj, proj_ab, l0, l1, hg_norm_w, cw, alog, dtb, gd_norm_w, st_a, st_b, t_inv, do_a, do_b, dproj,
        chips_exchange([sums[nm] for nm in ffn_names + mix_names]))
    dwt_main = matmul("dw_in_main", dproj, xn, "tn", W_MAIN, D, lp, BF16, tm=512, tn=512)
    dwt_ab = matmul("dw_in_ab", dproj_ab, xn, "tn", LANES, D, lp, BF16, tn=512)
    dxn_ab, sums_in = chip_sums("w_in", {
        "w_in": rows_to_shards("dw_in_shards", [dwt_main, dwt_ab], W_IN // N_DEV, _w_in_pieces(), 256)},
        host=lambda exch: matmul("d_xn_ab", dproj_ab, w_ab, "nt", lp, D, LANES, F32, exch=exch))
    sums.update(sums_in)
    from_chips = dict(zip(ffn_names + mix_names, from_chips_early))
    dxn, (from_chips["w_in"],) = matmul("d_xn", dproj, w_main, "nt", lp, D, W_MAIN, F32, tn=256, rows_outer=True,
                                        exch=chips_exchange([sums["w_in"]]))
    grad_x, dpre, d_mix_norm_w = norm1_bwd(pre, xs, mix_norm_w, dxn, dxn_ab, dh1)
    names = ["w_in", "w_branch_a", "w_branch_b", "w_out", "w_ffn_in", "w_ffn_out"]
    moms = [(m_w_in, v_w_in), (m_w_branch_a, v_w_branch_a), (m_w_branch_b, v_w_branch_b), (m_w_out, v_w_out),
            (m_w_ffn_in, v_w_ffn_in), (m_w_ffn_out, v_w_ffn_out)]
    upd = {}
    for nm, w, (m, v) in zip(names, big, moms):
        flip = (lambda t: t.T) if nm == "w_in" else (lambda t: t)
        res = adamw_shard("adamw_" + nm, flip(w), flip(m[0]), flip(v[0]), sums[nm], from_chips[nm], tiles[nm])
        upd[nm] = tuple(flip(t_)[None] for t_ in res)

    d_lb = jnp.concatenate([dl0, dl1], axis=0)
    rep = [_pack(t) for t in (d_lb, d_mix_norm_w, d_hg_nw, d_alog, d_dtb, d_gd_nw, d_ffn_norm_w, d_final_w)]
    n_rep = sum(t.shape[0] for t in rep)
    n_meta = N_META * D // LANES
    tot = allreduce_small(jnp.concatenate(rep + [_pack(loss_part), _pack(dpre[PRE - N_META:]), _pack(d_conv)], axis=0))
    loss = tot[n_rep, 0]
    g_meta_full = tot[n_rep + 8:n_rep + 8 + n_meta].reshape(N_META, D)
    g_conv_full = tot[n_rep + 8 + n_meta:].reshape(CONV_K, 3 * HW)
    g_meta_mine = lax.dynamic_slice_in_dim(g_meta_full, me * (D // N_DEV), D // N_DEV, axis=1)
    g_conv_mine = lax.dynamic_slice_in_dim(g_conv_full, me * (3 * DH), 3 * DH, axis=1)

    small = [("lb_logits", lb_logits, m_lb_logits, v_lb_logits), ("mix_norm_w", mix_norm_w, m_mix_norm_w, v_mix_norm_w),
             ("hg_norm_w", hg_norm_w, m_hg_norm_w, v_hg_norm_w), ("gd_a_log", gd_a_log, m_gd_a_log, v_gd_a_log),
             ("gd_dt_bias", gd_dt_bias, m_gd_dt_bias, v_gd_dt_bias), ("gd_norm_w", gd_norm_w, m_gd_norm_w, v_gd_norm_w),
             ("ffn_norm_w", ffn_norm_w, m_ffn_norm_w, v_ffn_norm_w),
             ("final_norm_w", final_norm_w, m_final_norm_w, v_final_norm_w),
             ("meta_tokens", meta_tokens, m_meta_tokens, v_meta_tokens), ("gd_conv_w", gd_conv_w, m_gd_conv_w, v_gd_conv_w)]

    sizes = [_pack(w).shape[0] for _, w, _, _ in small]
    g_small = jnp.concatenate([tot[:n_rep], _pack(g_meta_mine), _pack(g_conv_mine)], axis=0)
    assert g_small.shape[0] == sum(sizes)
    w_small = jnp.concatenate([_pack(w) for _, w, _, _ in small], axis=0)
    m_small = jnp.concatenate([_pack(m) for _, _, m, _ in small], axis=0)
    v_small = jnp.concatenate([_pack(v) for _, _, _, v in small], axis=0)
    d_small, nm_small, nv_small = adamw_small(w_small, g_small, m_small, v_small)
    row = 0
    for (nm, w, _, _), sz in zip(small, sizes):
        def unpack(t):
            return t[row:row + sz].reshape(-1)[:w.size].reshape(w.shape)
        upd[nm] = (unpack(g_small), unpack(d_small), unpack(nm_small), unpack(nv_small))
        row += sz

    order = ["meta_tokens", "lb_logits", "mix_norm_w", "w_in", "hg_norm_w", "gd_conv_w", "gd_a_log", "gd_dt_bias",
             "gd_norm_w", "w_branch_a", "w_branch_b", "w_out", "ffn_norm_w", "w_ffn_in", "w_ffn_out", "final_norm_w"]
    outs = [loss, grad_x[None]]
    for j in range(4):
        outs += [upd[nm][j] for nm in order]
    return tuple(outs)
```

```python
import functools

import jax
import jax.numpy as jnp
from jax import lax
from jax.experimental import pallas as pl
from jax.experimental.pallas import tpu as pltpu

F32 = jnp.float32
BF16 = jnp.bfloat16
MESH = pl.DeviceIdType.MESH

EPS = 1e-6
D = 2048
N_META = 16
CHUNK = 64
SUB = 16
HEADS = 8
DH = 128
HW = HEADS * DH
CONV_K = 4
TAIL = 8
D_FF = 5632
PRE = 512
N_SKIP = PRE // CHUNK - 1
N_DEV = 8
LANES = 128
FF_BLK = 512
W_IN = 4 * HW + 4 * HW + 2 * HEADS + 2 * D
W_MAIN = 4 * HW + 4 * HW + 2 * D

ADAM_LR = 0.001
ADAM_B1 = 0.9
ADAM_B2 = 0.999
ADAM_EPS = 1e-08
ADAM_WD = 0.01
ADAM_STEP = 10

VMEM_LIMIT = 52 * 1024 * 1024

_NN = ((1,), (0,))
_NT = ((1,), (1,))
_TN = ((0,), (0,))


def _params(*sem):
    return pltpu.CompilerParams(dimension_semantics=sem, vmem_limit_bytes=VMEM_LIMIT)


BF16_ONCE, SPLIT, EXACT_LHS = 0, 1, 2


def _dot_bf16(a, b, dims):
    if a.ndim == 3:
        dn = (((dims[0][0] + 1,), (dims[1][0] + 1,)), ((0,), (0,)))
    else:
        dn = (dims, ((), ()))
    return lax.dot_general(a, b, dn, preferred_element_type=F32)


def _split(t):
    hi = t.astype(BF16)
    return hi, (t - hi.astype(F32)).astype(BF16)


def _raw_dot(a, b, dims, mode):
    if mode == BF16_ONCE:
        return _dot_bf16(a.astype(BF16), b.astype(BF16), dims)
    if mode == SPLIT:
        a_hi, a_lo = _split(a)
        b_hi, b_lo = _split(b)
        return _dot_bf16(a_hi, b_hi, dims) + (_dot_bf16(a_hi, b_lo, dims) + _dot_bf16(a_lo, b_hi, dims))
    a = a.astype(BF16)
    b_hi = b.astype(BF16)
    rest = b - b_hi.astype(F32)
    b_mid = rest.astype(BF16)
    b_lo = (rest - b_mid.astype(F32)).astype(BF16)
    return _dot_bf16(a, b_hi, dims) + (_dot_bf16(a, b_mid, dims) + _dot_bf16(a, b_lo, dims))


@functools.partial(jax.custom_vjp, nondiff_argnums=(2,))
def mm_nn(a, b, mode=BF16_ONCE):
    return _raw_dot(a, b, _NN, mode)


@functools.partial(jax.custom_vjp, nondiff_argnums=(2,))
def mm_nt(a, b, mode=BF16_ONCE):
    return _raw_dot(a, b, _NT, mode)


@functools.partial(jax.custom_vjp, nondiff_argnums=(2,))
def mm_tn(a, b, mode=BF16_ONCE):
    return _raw_dot(a, b, _TN, mode)


def _general(mode):
    return SPLIT if mode == EXACT_LHS else mode


mm_nn.defvjp(lambda a, b, p: (_raw_dot(a, b, _NN, p), (a, b)),
             lambda p, res, g: (mm_nt(g, res[1], _general(p)), mm_tn(res[0], g, p)))
mm_nt.defvjp(lambda a, b, p: (_raw_dot(a, b, _NT, p), (a, b)),
             lambda p, res, g: (mm_nn(g, res[1], p), mm_tn(g, res[0], p)))
mm_tn.defvjp(lambda a, b, p: (_raw_dot(a, b, _TN, p), (a, b)),
             lambda p, res, g: (mm_nt(res[1], g, _general(p)), mm_nn(res[0], g, _general(p))))


def _sigmoid(x):
    return jax.nn.sigmoid(x)


def _silu(x):
    return x * _sigmoid(x)


def _softplus(x):
    return jnp.maximum(x, 0.0) + jnp.log(1.0 + jnp.exp(-jnp.abs(x)))


def _l2n(t):
    return t * lax.rsqrt(jnp.sum(t * t, axis=-1, keepdims=True) + EPS)


def _rms_norm(x, w):
    return x * lax.rsqrt(jnp.mean(x * x, axis=-1, keepdims=True) + EPS) * w


def _gated_norm(o, gate, nw):
    o = o * lax.rsqrt(jnp.mean(o * o, axis=-1, keepdims=True) + EPS) * nw
    return o * _silu(gate)


def _heads(ref, piece):
    return jnp.stack([ref[:, piece * HW + DH * j:piece * HW + DH * (j + 1)] for j in range(HEADS)])


def _put_heads(ref, piece, value, accumulate=False):
    for j in range(HEADS):
        cols = slice(piece * HW + DH * j, piece * HW + DH * (j + 1))
        if accumulate:
            ref[:, cols] += value[j]
        else:
            ref[:, cols] = value[j].astype(ref.dtype)


def _diag_scores(q_i, k_i, f_i):
    nb, s, d = q_i.shape
    row = lax.broadcasted_iota(jnp.int32, (nb, s, d), 1)
    q4, f4 = q_i[:, :, None, :], f_i[:, :, None, :]
    kk = k_i
    rows_out = []
    for t in range(s):
        if t > 0:
            kk = jnp.where(row < t, kk * f4[:, t], kk)
        rows_out.append(jnp.sum(kk * q4[:, t], axis=-1)[:, None, :])
    s_ii = jnp.concatenate(rows_out, axis=1)
    tri = lax.broadcasted_iota(jnp.int32, (nb, s, s), 1) >= lax.broadcasted_iota(jnp.int32, (nb, s, s), 2)
    return jnp.where(tri, s_ii, 0.0)


def hg_chunk(hq, hf, hi, hg, l0, l1, nw, st):
    nb = hq.shape[0]
    m = jnp.maximum(l0, l1)
    e0 = jnp.exp(l0 - m)
    e1 = jnp.exp(l1 - m)
    lb = e0 / (e0 + e1)
    sig = _sigmoid(hf)
    q = _silu(hq)
    f = lb + (1.0 - lb) * sig
    log_f = jnp.log(f)
    k = (1.0 - lb) * _sigmoid(-hf)
    v = hi
    rows = lax.broadcasted_iota(jnp.int32, (nb, CHUNK, CHUNK), 1)
    cols = lax.broadcasted_iota(jnp.int32, (nb, CHUNK, CHUNK), 2)
    b = mm_nn((rows >= cols).astype(F32), log_f, EXACT_LHS)
    o_inter = mm_nt(q * jnp.exp(b), st)
    outs = []
    for i in range(CHUNK // SUB):
        lo = i * SUB
        b_i, q_i, k_i, v_i = b[:, lo:lo + SUB], q[:, lo:lo + SUB], k[:, lo:lo + SUB], v[:, lo:lo + SUB]
        o_i = mm_nn(_diag_scores(q_i, k_i, f[:, lo:lo + SUB]), v_i)
        if i > 0:
            b_ref = b[:, :, None, :][:, lo - 1]
            q_t = q_i * jnp.exp(b_i - b_ref)
            k_t = k[:, :lo] * jnp.exp(b_ref - b[:, :lo])
            o_i = o_i + mm_nn(mm_nt(q_t, k_t), v[:, :lo])
        outs.append(o_i)
    o = o_inter + jnp.concatenate(outs, axis=1)
    b_last = b[:, :, None, :][:, CHUNK - 1]
    st_new = st * jnp.exp(b_last) + mm_tn(v, k * jnp.exp(b_last - b))
    return _gated_norm(o, hg, nw), st_new


@jax.custom_vjp
def _unit_lower_inverse(a):
    n = a.shape[-1]
    eye = (lax.broadcasted_iota(jnp.int32, a.shape, 1) == lax.broadcasted_iota(jnp.int32, a.shape, 2)).astype(F32)
    x = eye - a
    p = mm_nn(a, a, SPLIT)
    x = x + mm_nn(x, p, SPLIT)
    power = 4
    while power < n:
        p = mm_nn(p, p, SPLIT)
        x = x + mm_nn(x, p, SPLIT)
        power *= 2
    return x


def _unit_lower_inverse_fwd(a):
    t = _unit_lower_inverse(a)
    return t, t


def _unit_lower_inverse_bwd(t, dt):
    return (-mm_tn(t, mm_nt(dt, t, SPLIT), SPLIT),)


_unit_lower_inverse.defvjp(_unit_lower_inverse_fwd, _unit_lower_inverse_bwd)


@jax.custom_vjp
def _kept_inverse(a, t):
    return t


_kept_inverse.defvjp(lambda a, t: (t, t),
                     lambda t, dt: (-mm_tn(t, mm_nt(dt, t, SPLIT), SPLIT), jnp.zeros_like(t)))


def gd_chunk(cq, ck, cv, z, ab, alog, dtb, nw, s, kept_t=None, keep_t=False):
    nb, c, _ = cq.shape
    lane = lax.broadcasted_iota(jnp.int32, (nb, 1, DH), 2)
    head = lax.broadcasted_iota(jnp.int32, (nb, 1, DH), 0)
    pick = lambda t, off: jnp.sum(jnp.where(lane == head + off, t[None], 0.0), axis=2, keepdims=True)
    a_raw, b_raw = pick(ab, 0), pick(ab, HEADS)
    beta = _sigmoid(b_raw)
    g = -jnp.exp(pick(alog, 0)) * _softplus(a_raw + pick(dtb, 0))
    q = _l2n(_silu(cq)) * (DH ** -0.5)
    k = _l2n(_silu(ck))
    v = _silu(cv)
    rows = lax.broadcasted_iota(jnp.int32, (nb, c, c), 1)
    cols = lax.broadcasted_iota(jnp.int32, (nb, c, c), 2)
    tril = (rows >= cols).astype(F32)
    gc_w = mm_nn(tril, jnp.broadcast_to(g, (nb, c, DH)), EXACT_LHS)
    g_sq = jnp.broadcast_to(g, (nb, c, c))
    gc_col = mm_nn(tril, g_sq, EXACT_LHS)
    gc_row = mm_nn(jnp.ones((nb, c, c), F32), g_sq * (rows <= cols).astype(F32), EXACT_LHS)
    rel = jnp.exp(jnp.minimum(gc_col - gc_row, 0.0))
    a = jnp.where(rows > cols, beta * mm_nt(k, k) * rel, 0.0)
    t_inv = _unit_lower_inverse(a) if kept_t is None else _kept_inverse(a, kept_t)
    e_gc = jnp.exp(gc_w)
    w = mm_nn(t_inv, beta * e_gc * k, SPLIT)
    u = mm_nn(t_inv, beta * v, SPLIT)
    v_new = u - mm_nn(w, s)
    attn = jnp.where(rows >= cols, mm_nt(q, k) * rel, 0.0)
    o = mm_nn(q * e_gc, s) + mm_nn(attn, v_new)
    g_last = jnp.sum(g, axis=1, keepdims=True)
    s_new = jnp.exp(g_last) * s + mm_tn(k * jnp.exp(g_last - gc_w), v_new)
    out = _gated_norm(o, z, nw)
    return (out, s_new, t_inv) if keep_t else (out, s_new)


def _gd_conv(x_ref, tail_ref, cw_ref, xe_ref, first):
    xe_ref[0:TAIL, :] = jnp.where(first, 0.0, tail_ref[:, 0:3 * HW])
    xe_ref[TAIL:TAIL + CHUNK, :] = x_ref[:, 0:3 * HW]
    y = cw_ref[pl.ds(0, 1), :] * xe_ref[pl.ds(TAIL - CONV_K + 1, CHUNK), :]
    for k in range(1, CONV_K):
        y = y + cw_ref[pl.ds(k, 1), :] * xe_ref[pl.ds(TAIL - CONV_K + 1 + k, CHUNK), :]
    return y


def _conv_heads(y, piece):
    return jnp.stack([y[:, piece * HW + DH * j:piece * HW + DH * (j + 1)] for j in range(HEADS)])


def mixer_fwd(proj, proj_ab, l0, l1, hg_nw, cw, alog, dtb, gd_nw):
    lp = proj.shape[0]
    nc = lp // CHUNK

    def body(xh_ref, l0_ref, l1_ref, hnw_ref, xg_ref, tail_ref, ab_ref, cw_ref, alog_ref, dtb_ref, gnw_ref,
             oa_ref, sta_out_ref, ob_ref, stb_out_ref, tinv_ref, sta_ref, stb_ref, xe_ref):
        c = pl.program_id(0)

        @pl.when(c == 0)
        def _():
            sta_ref[...] = jnp.zeros(sta_ref.shape, F32)
            stb_ref[...] = jnp.zeros(stb_ref.shape, F32)

        @pl.when(c < N_SKIP)
        def _():
            for ref in (oa_ref, sta_out_ref, ob_ref, stb_out_ref, tinv_ref):
                ref[...] = jnp.zeros(ref.shape, ref.dtype)

        @pl.when(c >= N_SKIP)
        def _():
            stb = stb_ref[...]
            stb_out_ref[0] = stb
            y = _gd_conv(xg_ref, tail_ref, cw_ref, xe_ref, c == 0)
            ob, stb_new, tinv_ref[0] = gd_chunk(
                _conv_heads(y, 0), _conv_heads(y, 1), _conv_heads(y, 2), _heads(xg_ref, 3),
                ab_ref[...], alog_ref[...], dtb_ref[...], gnw_ref[...], stb, keep_t=True)
            _put_heads(ob_ref, 0, ob)
            stb_ref[...] = stb_new
            sta = sta_ref[...]
            sta_out_ref[0] = sta
            oa, sta_new = hg_chunk(_heads(xh_ref, 0), _heads(xh_ref, 1), _heads(xh_ref, 2), _heads(xh_ref, 3),
                                   _heads(l0_ref, 0), _heads(l1_ref, 0), hnw_ref[...], sta)
            _put_heads(oa_ref, 0, oa)
            sta_ref[...] = sta_new

    vec = pl.BlockSpec((1, HW), lambda c: (0, 0))
    one = pl.BlockSpec((1, DH), lambda c: (0, 0))
    st_spec = pl.BlockSpec((1, HEADS, DH, DH), lambda c: (c, 0, 0, 0))
    o_spec = pl.BlockSpec((CHUNK, HW), lambda c: (c, 0))
    o_shape = jax.ShapeDtypeStruct((lp, HW), BF16)
    st_shape = jax.ShapeDtypeStruct((nc, HEADS, DH, DH), F32)
    return pl.pallas_call(
        body, name="mixer_fwd", grid=(nc,),
        in_specs=[pl.BlockSpec((CHUNK, 4 * HW), lambda c: (c, 0)), vec, vec, one,
                  pl.BlockSpec((CHUNK, 4 * HW), lambda c: (c, 1)),
                  pl.BlockSpec((TAIL, 4 * HW), lambda c: (jnp.maximum(c * (CHUNK // TAIL) - 1, 0), 1)),
                  pl.BlockSpec((CHUNK, DH), lambda c: (c, 0)),
                  pl.BlockSpec((CONV_K, 3 * HW), lambda c: (0, 0)), one, one, one],
        out_specs=[o_spec, st_spec, o_spec, st_spec,
                   pl.BlockSpec((1, HEADS, CHUNK, CHUNK), lambda c: (c, 0, 0, 0))],
        out_shape=[o_shape, st_shape, o_shape, st_shape, jax.ShapeDtypeStruct((nc, HEADS, CHUNK, CHUNK), F32)],
        scratch_shapes=[pltpu.VMEM((HEADS, DH, DH), F32), pltpu.VMEM((HEADS, DH, DH), F32),
                        pltpu.VMEM((TAIL + CHUNK, 3 * HW), F32)],
        compiler_params=_params("arbitrary"),
    )(proj, l0, l1, hg_nw, proj, proj, proj_ab, cw, alog, dtb, gd_nw)


def mixer_bwd(proj, proj_ab, l0, l1, hg_nw, cw, alog, dtb, gd_nw, st_a, st_b, t_inv, do_a, do_b, dproj, exch):
    lp = proj.shape[0]
    nc = lp // CHUNK

    def body(xh_ref, l0_ref, l1_ref, hnw_ref, xg_ref, tail_ref, ab_ref, cw_ref, alog_ref, dtb_ref, gnw_ref,
             sta_in_ref, stb_in_ref, tinv_ref, doa_ref, dob_ref, _,
             dx_ref, dl0_ref, dl1_ref, dhnw_ref, dab_ref, dcw_ref, dalog_ref, ddtb_ref, dgnw_ref,
             dsta_ref, dstb_ref, xe_ref, dye_ref, head_ref):
        c = pl.program_id(0)
        cc = nc - 1 - c

        @pl.when(c == 0)
        def _():
            for ref in (dl0_ref, dl1_ref, dhnw_ref, dcw_ref, dalog_ref, ddtb_ref, dgnw_ref, dsta_ref, dstb_ref,
                        head_ref):
                ref[...] = jnp.zeros(ref.shape, F32)

        @pl.when(cc < N_SKIP)
        def _():
            dx_ref[...] = jnp.zeros(dx_ref.shape, dx_ref.dtype)
            dab_ref[...] = jnp.zeros(dab_ref.shape, F32)

        @pl.when(cc >= N_SKIP)
        def _():
            y = _gd_conv(xg_ref, tail_ref, cw_ref, xe_ref, cc == 0)
            _, gd_vjp = jax.vjp(functools.partial(gd_chunk, kept_t=tinv_ref[0]),
                                _conv_heads(y, 0), _conv_heads(y, 1), _conv_heads(y, 2), _heads(xg_ref, 3),
                                ab_ref[...], alog_ref[...], dtb_ref[...], gnw_ref[...], stb_in_ref[0])
            dcq, dck, dcv, dz, dab, dalog, ddtb, dgnw, dstb = gd_vjp((_heads(dob_ref, 0), dstb_ref[...]))
            _, hg_vjp = jax.vjp(hg_chunk, _heads(xh_ref, 0), _heads(xh_ref, 1), _heads(xh_ref, 2), _heads(xh_ref, 3),
                                _heads(l0_ref, 0), _heads(l1_ref, 0), hnw_ref[...], sta_in_ref[0])
            hg_grads = hg_vjp((_heads(doa_ref, 0), dsta_ref[...]))
            dstb_ref[...] = dstb
            dab_ref[...] = dab
            dalog_ref[...] += dalog
            ddtb_ref[...] += ddtb
            dgnw_ref[...] += dgnw
            _put_heads(dx_ref, 4 + 3, dz)
            for i, t in enumerate((dcq, dck, dcv)):
                for j in range(HEADS):
                    dye_ref[0:CHUNK, i * HW + DH * j:i * HW + DH * (j + 1)] = t[j]
            dye_ref[CHUNK:CHUNK + TAIL, :] = head_ref[...]
            head_ref[...] = dye_ref[0:TAIL, :]
            dy = dye_ref[0:CHUNK, :]
            dx = None
            for k in range(CONV_K):
                term = cw_ref[pl.ds(k, 1), :] * dye_ref[pl.ds(CONV_K - 1 - k, CHUNK), :]
                dx = term if dx is None else dx + term
                dcw_ref[pl.ds(k, 1), :] += jnp.sum(dy * xe_ref[pl.ds(TAIL - CONV_K + 1 + k, CHUNK), :],
                                                   axis=0, keepdims=True)
            dx_ref[:, 4 * HW:7 * HW] = dx.astype(dx_ref.dtype)
            for i in range(4):
                _put_heads(dx_ref, i, hg_grads[i])
            _put_heads(dl0_ref, 0, hg_grads[4], accumulate=True)
            _put_heads(dl1_ref, 0, hg_grads[5], accumulate=True)
            dhnw_ref[...] += hg_grads[6]
            dsta_ref[...] = hg_grads[7]

    rc = lambda c: nc - 1 - c
    vec = pl.BlockSpec((1, HW), lambda c: (0, 0))
    one = pl.BlockSpec((1, DH), lambda c: (0, 0))
    one_shape = jax.ShapeDtypeStruct((1, DH), F32)
    vec_shape = jax.ShapeDtypeStruct((1, HW), F32)
    st_spec = pl.BlockSpec((1, HEADS, DH, DH), lambda c: (rc(c), 0, 0, 0))
    do_spec = pl.BlockSpec((CHUNK, HW), lambda c: (rc(c), 0))
    n_in, n_out = 17, 9
    outs = pl.pallas_call(
        host_exchange(body, n_in, n_out, exch, lambda: pl.program_id(0) == 0, lambda: pl.program_id(0) == nc - 1),
        name="mixer_bwd", grid=(nc,), input_output_aliases={n_in - 1: 0}, compiler_params=_params("arbitrary"),
        **_hosted(exch,
                  [pl.BlockSpec((CHUNK, 4 * HW), lambda c: (rc(c), 0)), vec, vec, one,
                   pl.BlockSpec((CHUNK, 4 * HW), lambda c: (rc(c), 1)),
                   pl.BlockSpec((TAIL, 4 * HW), lambda c: (jnp.maximum(rc(c) * (CHUNK // TAIL) - 1, 0), 1)),
                   pl.BlockSpec((CHUNK, DH), lambda c: (rc(c), 0)),
                   pl.BlockSpec((CONV_K, 3 * HW), lambda c: (0, 0)), one, one, one,
                   st_spec, st_spec, pl.BlockSpec((1, HEADS, CHUNK, CHUNK), lambda c: (rc(c), 0, 0, 0)),
                   do_spec, do_spec, pl.BlockSpec(memory_space=pl.ANY)],
                  [pl.BlockSpec((CHUNK, 8 * HW), lambda c: (rc(c), 0)), vec, vec, one,
                   pl.BlockSpec((CHUNK, DH), lambda c: (rc(c), 0)),
                   pl.BlockSpec((CONV_K, 3 * HW), lambda c: (0, 0)), one, one, one],
                  [jax.ShapeDtypeStruct((lp, W_MAIN), BF16), vec_shape, vec_shape, one_shape,
                   jax.ShapeDtypeStruct((lp, DH), F32), jax.ShapeDtypeStruct((CONV_K, 3 * HW), F32),
                   one_shape, one_shape, one_shape],
                  [pltpu.VMEM((HEADS, DH, DH), F32), pltpu.VMEM((HEADS, DH, DH), F32),
                   pltpu.VMEM((TAIL + CHUNK, 3 * HW), F32), pltpu.VMEM((CHUNK + TAIL, 3 * HW), F32),
                   pltpu.VMEM((TAIL, 3 * HW), F32)]),
    )(proj, l0, l1, hg_nw, proj, proj, proj_ab, cw, alog, dtb, gd_nw, st_a, st_b, t_inv, do_a, do_b, dproj,
      *exch.arrays)
    return outs[:n_out], outs[n_out:]


def matmul(name, a, b, mode, m, n, k, out_dtype, tm=512, tn=1024, tk=None, a_off=(0, 0), b_off=(0, 0), exch=None,
           rows_outer=False):
    tm, tn = min(tm, m), min(tn, n)
    tk = k if tk is None else min(tk, k)
    assert m % tm == 0 and n % tn == 0 and k % tk == 0, (name, m, n, k, tm, tn, tk)
    gi, gj, gk = m // tm, n // tn, k // tk
    grid = (gi, gj, gk) if rows_outer else (gj, gi, gk)

    def at(index):
        return (lambda i, j, kk: index(j, i, kk)) if rows_outer else index

    if mode == "nn":
        a_spec = pl.BlockSpec((tm, tk), at(lambda j, i, kk: (i + a_off[0], kk + a_off[1])))
        b_spec = pl.BlockSpec((tk, tn), at(lambda j, i, kk: (kk + b_off[0], j + b_off[1])))
        dims = _NN
    elif mode == "nt":
        a_spec = pl.BlockSpec((tm, tk), at(lambda j, i, kk: (i + a_off[0], kk + a_off[1])))
        b_spec = pl.BlockSpec((tn, tk), at(lambda j, i, kk: (j + b_off[0], kk + b_off[1])))
        dims = _NT
    else:
        a_spec = pl.BlockSpec((tk, tm), at(lambda j, i, kk: (kk + a_off[0], i + a_off[1])))
        b_spec = pl.BlockSpec((tk, tn), at(lambda j, i, kk: (kk + b_off[0], j + b_off[1])))
        dims = _TN

    def body(a_ref, b_ref, o_ref, *acc):
        d = lax.dot_general(a_ref[...].astype(BF16), b_ref[...].astype(BF16), (dims, ((), ())),
                            preferred_element_type=F32)
        if gk == 1:
            o_ref[...] = d.astype(o_ref.dtype)
        else:
            acc_ref, = acc
            kk = pl.program_id(2)

            @pl.when(kk == 0)
            def _():
                acc_ref[...] = d

            @pl.when(kk > 0)
            def _():
                acc_ref[...] += d

            @pl.when(kk == gk - 1)
            def _():
                o_ref[...] = acc_ref[...].astype(o_ref.dtype)

    o_spec = pl.BlockSpec((tm, tn), at(lambda j, i, kk: (i, j)))
    o_shape = jax.ShapeDtypeStruct((m, n), out_dtype)
    scratch = [] if gk == 1 else [pltpu.VMEM((tm, tn), F32)]
    if exch is None:
        return pl.pallas_call(
            body, name=name, grid=grid, in_specs=[a_spec, b_spec], out_specs=o_spec, out_shape=o_shape,
            scratch_shapes=scratch, compiler_params=_params("parallel", "parallel", "arbitrary"),
        )(a, b)
    step = lambda: (pl.program_id(0), pl.program_id(1), pl.program_id(2))
    first = lambda: jnp.logical_and(jnp.logical_and(step()[0] == 0, step()[1] == 0), step()[2] == 0)
    last = lambda: jnp.logical_and(jnp.logical_and(step()[0] == grid[0] - 1, step()[1] == grid[1] - 1),
                                   step()[2] == gk - 1)
    outs = pl.pallas_call(
        host_exchange(body, 2, 1, exch, first, last), name=name, grid=grid,
        compiler_params=_params("arbitrary", "arbitrary", "arbitrary"),
        **_hosted(exch, [a_spec, b_spec], [o_spec], [o_shape], scratch),
    )(a, b, *exch.arrays)
    return outs[0], outs[1:]


def _swiglu(gu):
    return _silu(gu[:, :FF_BLK]) * gu[:, FF_BLK:]


def ffn_in_fused(xn2, w_gu, tm=512):
    l = xn2.shape[0]
    tn = 2 * FF_BLK

    def body(a_ref, b_ref, gu_ref, act_ref):
        gu = lax.dot_general(a_ref[...], b_ref[...], (_NN, ((), ())), preferred_element_type=F32)
        gu_ref[...] = gu
        act_ref[...] = _swiglu(gu).astype(act_ref.dtype)

    return pl.pallas_call(
        body, name="ffn_in", grid=(2 * D_FF // tn, l // tm),
        in_specs=[pl.BlockSpec((tm, D), lambda j, i: (i, 0)), pl.BlockSpec((D, tn), lambda j, i: (0, j))],
        out_specs=[pl.BlockSpec((tm, tn), lambda j, i: (i, j)), pl.BlockSpec((tm, FF_BLK), lambda j, i: (i, j))],
        out_shape=[jax.ShapeDtypeStruct((l, 2 * D_FF), F32), jax.ShapeDtypeStruct((l, D_FF), BF16)],
        compiler_params=_params("parallel", "parallel"),
    )(xn2, w_gu)


def d_act_fused(dh2, w_fo, gu, tm=512):
    l = dh2.shape[0]

    def body(a_ref, b_ref, gu_ref, o_ref):
        da = lax.dot_general(a_ref[...], b_ref[...], (_NT, ((), ())), preferred_element_type=F32)
        _, vjp = jax.vjp(_swiglu, gu_ref[...])
        dgu, = vjp(da)
        o_ref[...] = dgu.astype(o_ref.dtype)

    return pl.pallas_call(
        body, name="d_act", grid=(D_FF // FF_BLK, l // tm),
        in_specs=[pl.BlockSpec((tm, D), lambda j, i: (i, 0)), pl.BlockSpec((FF_BLK, D), lambda j, i: (j, 0)),
                  pl.BlockSpec((tm, 2 * FF_BLK), lambda j, i: (i, j))],
        out_specs=pl.BlockSpec((tm, 2 * FF_BLK), lambda j, i: (i, j)),
        out_shape=jax.ShapeDtypeStruct((l, 2 * D_FF), BF16),
        compiler_params=_params("parallel", "parallel"),
    )(dh2, w_fo, gu)


TR = 256
N_PRE = PRE // TR


def _row(width, off=0, col=0):
    return pl.BlockSpec((TR, width), lambda i: (i + off, col))


def _pre_spec():
    return pl.BlockSpec((TR, D), lambda i: (jnp.minimum(i, N_PRE - 1), 0))


def _seq_spec(width=D, col=0):
    return pl.BlockSpec((TR, width), lambda i: (jnp.maximum(i - N_PRE, 0), col))


def _vec_spec(width=D):
    return pl.BlockSpec((1, width), lambda i: (0, 0))


def norm1_fwd(pre, x, w):
    lp = PRE + x.shape[0]

    def body(pre_ref, x_ref, w_ref, o_ref):
        h = jnp.where(pl.program_id(0) < N_PRE, pre_ref[...], x_ref[...])
        o_ref[...] = _rms_norm(h, w_ref[...]).astype(o_ref.dtype)

    return pl.pallas_call(
        body, name="norm1_fwd", grid=(lp // TR,),
        in_specs=[_pre_spec(), _seq_spec(), _vec_spec()],
        out_specs=_row(D), out_shape=jax.ShapeDtypeStruct((lp, D), BF16),
        compiler_params=_params("parallel"),
    )(pre, x, w)


def norm1_bwd(pre, x, w, dxn, dxn_ab, dh1):
    l = x.shape[0]
    lp = PRE + l

    def body(pre_ref, x_ref, w_ref, dxn_ref, dxn_ab_ref, dh1_ref, dx_ref, dpre_ref, dw_ref):
        i = pl.program_id(0)
        h = jnp.where(i < N_PRE, pre_ref[...], x_ref[...])
        _, vjp = jax.vjp(_rms_norm, h, w_ref[...])
        dh, dw = vjp(dxn_ref[...] + dxn_ab_ref[...])

        @pl.when(i == 0)
        def _():
            dw_ref[...] = dw

        @pl.when(i > 0)
        def _():
            dw_ref[...] += dw

        @pl.when(i < N_PRE)
        def _():
            dpre_ref[...] = dh

        @pl.when(i >= N_PRE)
        def _():
            dx_ref[...] = dh + dh1_ref[...]

    return pl.pallas_call(
        body, name="norm1_bwd", grid=(lp // TR,),
        in_specs=[_pre_spec(), _seq_spec(), _vec_spec(), _row(D), _row(D), _seq_spec()],
        out_specs=[_seq_spec(), _pre_spec(), _vec_spec()],
        out_shape=[jax.ShapeDtypeStruct((l, D), F32), jax.ShapeDtypeStruct((PRE, D), F32),
                   jax.ShapeDtypeStruct((1, D), F32)],
        compiler_params=_params("arbitrary"),
    )(pre, x, w, dxn, dxn_ab, dh1)


def _merge(gates, ya, yb):
    return _sigmoid(gates[:, :D]) * ya + _sigmoid(gates[:, D:]) * yb


def merge_bwd(dh1, wo, proj, ya, yb, exch):
    l = ya.shape[0]
    lp = PRE + l

    def body(a_ref, b_ref, g_ref, ya_ref, yb_ref, dg_ref, dya_ref, dyb_ref):
        i = pl.program_id(0)

        @pl.when(i < N_PRE)
        def _():
            dg_ref[...] = jnp.zeros(dg_ref.shape, dg_ref.dtype)
            dya_ref[...] = jnp.zeros(dya_ref.shape, dya_ref.dtype)
            dyb_ref[...] = jnp.zeros(dyb_ref.shape, dyb_ref.dtype)

        @pl.when(i >= N_PRE)
        def _():
            dm = lax.dot_general(a_ref[...], b_ref[...], (_NT, ((), ())), preferred_element_type=F32)
            _, vjp = jax.vjp(_merge, g_ref[...], ya_ref[...], yb_ref[...])
            dg, dya, dyb = vjp(dm)
            dg_ref[...] = dg.astype(dg_ref.dtype)
            dya_ref[...] = dya.astype(dya_ref.dtype)
            dyb_ref[...] = dyb.astype(dyb_ref.dtype)

    last = lp // TR - 1
    outs = pl.pallas_call(
        host_exchange(body, 5, 3, exch, lambda: pl.program_id(0) == 0, lambda: pl.program_id(0) == last),
        name="merge_bwd", grid=(lp // TR,), compiler_params=_params("arbitrary"),
        **_hosted(exch,
                  [_seq_spec(), pl.BlockSpec((D, D), lambda i: (0, 0)),
                   pl.BlockSpec((TR, 2 * D), lambda i: (jnp.maximum(i, N_PRE), 2)), _seq_spec(), _seq_spec()],
                  [_row(2 * D, 0, 2), _row(D), _row(D)],
                  [jax.ShapeDtypeStruct((lp, W_MAIN), BF16), jax.ShapeDtypeStruct((lp, D), BF16),
                   jax.ShapeDtypeStruct((lp, D), BF16)], []),
    )(dh1, wo, proj, ya, yb, *exch.arrays)
    return outs[:3], outs[3:]


def merge_mix_out(proj, ya, yb, wo, x, w):
    l = x.shape[0]

    def body(g_ref, ya_ref, yb_ref, b_ref, x_ref, w_ref, m_ref, h_ref, o_ref):
        merged = _merge(g_ref[...], ya_ref[...], yb_ref[...]).astype(BF16)
        m_ref[...] = merged
        h = x_ref[...] + lax.dot_general(merged, b_ref[...], (_NN, ((), ())), preferred_element_type=F32)
        h_ref[...] = h
        o_ref[...] = _rms_norm(h, w_ref[...]).astype(o_ref.dtype)

    return pl.pallas_call(
        body, name="mix_out", grid=(l // TR,),
        in_specs=[_row(2 * D, N_PRE, 2), _row(D), _row(D), pl.BlockSpec((D, D), lambda i: (0, 0)), _row(D),
                  _vec_spec()],
        out_specs=[_row(D), _row(D), _row(D)],
        out_shape=[jax.ShapeDtypeStruct((l, D), BF16), jax.ShapeDtypeStruct((l, D), F32),
                   jax.ShapeDtypeStruct((l, D), BF16)],
        compiler_params=_params("parallel"),
    )(proj, ya, yb, wo, x, w)


def norm2_bwd(h1, w, dxn2, dh2):
    l = h1.shape[0]

    def body(h_ref, w_ref, dxn_ref, dh2_ref, dh1_ref, dh1b_ref, dw_ref):
        i = pl.program_id(0)
        _, vjp = jax.vjp(_rms_norm, h_ref[...], w_ref[...])
        dh, dw = vjp(dxn_ref[...])
        dh1 = dh + dh2_ref[...]
        dh1_ref[...] = dh1
        dh1b_ref[...] = dh1.astype(BF16)

        @pl.when(i == 0)
        def _():
            dw_ref[...] = dw

        @pl.when(i > 0)
        def _():
            dw_ref[...] += dw

    return pl.pallas_call(
        body, name="norm2_bwd", grid=(l // TR,),
        in_specs=[_row(D), _vec_spec(), _row(D), _row(D)],
        out_specs=[_row(D), _row(D), _vec_spec()],
        out_shape=[jax.ShapeDtypeStruct((l, D), F32), jax.ShapeDtypeStruct((l, D), BF16),
                   jax.ShapeDtypeStruct((1, D), F32)],
        compiler_params=_params("arbitrary"),
    )(h1, w, dxn2, dh2)


def _loss_rows(h1, f, w, tgt):
    y = _rms_norm(h1 + f, w)
    err = (y - tgt) * (y - tgt)
    return 0.5 * jnp.sum(jnp.mean(err, axis=-1, keepdims=True), axis=0, keepdims=True)


def loss_head(h1, f, w, tgt):
    l = h1.shape[0]

    def body(h_ref, f_ref, w_ref, t_ref, loss_ref, dh_ref, dhb_ref, dw_ref):
        i = pl.program_id(0)
        loss, vjp = jax.vjp(_loss_rows, h_ref[...], f_ref[...], w_ref[...], t_ref[...])
        dh, _, dw, _ = vjp(jnp.ones((1, 1), F32))
        dh_ref[...] = dh
        dhb_ref[...] = dh.astype(BF16)
        loss = jnp.broadcast_to(loss, (1, LANES))

        @pl.when(i == 0)
        def _():
            dw_ref[...] = dw
            loss_ref[...] = loss

        @pl.when(i > 0)
        def _():
            dw_ref[...] += dw
            loss_ref[...] += loss

    return pl.pallas_call(
        body, name="loss_head", grid=(l // TR,),
        in_specs=[_row(D), _row(D), _vec_spec(), _row(D)],
        out_specs=[_vec_spec(LANES), _row(D), _row(D), _vec_spec()],
        out_shape=[jax.ShapeDtypeStruct((1, LANES), F32), jax.ShapeDtypeStruct((l, D), F32),
                   jax.ShapeDtypeStruct((l, D), BF16), jax.ShapeDtypeStruct((1, D), F32)],
        compiler_params=_params("arbitrary"),
    )(h1, f, w, tgt)


def _adamw(w, g, m, v):
    m = ADAM_B1 * m + (1.0 - ADAM_B1) * g
    v = ADAM_B2 * v + (1.0 - ADAM_B2) * (g * g)
    m_hat = m / (1.0 - ADAM_B1 ** ADAM_STEP)
    v_hat = v / (1.0 - ADAM_B2 ** ADAM_STEP)
    delta = -ADAM_LR * (m_hat / (jnp.sqrt(v_hat) + ADAM_EPS) + ADAM_WD * w)
    return delta, m, v


def adamw_shard(name, w, m, v, sums, recv, tile):
    r, c = w.shape
    tr, tc = tile
    assert r % tr == 0 and c % tc == 0

    def body(chip_ref, w_ref, m_ref, v_ref, own_ref, r0_ref, r1_ref, r2_ref, g_ref, d_ref, nm_ref, nv_ref):
        g = own_ref[...].astype(F32) + r0_ref[...].astype(F32)
        g = g + r1_ref[...].astype(F32)
        g = g + r2_ref[...].astype(F32)
        d, nm, nv = _adamw(w_ref[...], g, m_ref[...], v_ref[...])
        g_ref[...] = g
        d_ref[...] = d
        nm_ref[...] = nm
        nv_ref[...] = nv

    blk = pl.BlockSpec((tr, tc), lambda i, j, chip: (i, j))
    part = lambda s: pl.BlockSpec((None, tr, tc), lambda i, j, chip: (s, i, j))
    own = pl.BlockSpec((None, tr, tc), lambda i, j, chip: (chip[0], i, j))
    shape = jax.ShapeDtypeStruct((r, c), F32)
    my_chip = (2 * lax.axis_index("x") + lax.axis_index("y")).astype(jnp.int32).reshape(1)
    return pl.pallas_call(
        body, name=name, out_shape=[shape, shape, shape, shape],
        grid_spec=pltpu.PrefetchScalarGridSpec(num_scalar_prefetch=1, grid=(r // tr, c // tc),
                                               in_specs=[blk, blk, blk, own, part(0), part(1), part(2)],
                                               out_specs=[blk, blk, blk, blk]),
        compiler_params=_params("parallel", "parallel"),
    )(my_chip, w, m, v, sums, recv, recv, recv)


def adamw_small(w, g, m, v):
    def body(w_ref, g_ref, m_ref, v_ref, d_ref, nm_ref, nv_ref):
        d, nm, nv = _adamw(w_ref[...], g_ref[...], m_ref[...], v_ref[...])
        d_ref[...] = d
        nm_ref[...] = nm
        nv_ref[...] = nv

    shape = jax.ShapeDtypeStruct(w.shape, F32)
    return pl.pallas_call(body, name="adamw_small", out_shape=[shape, shape, shape])(w, g, m, v)


def pair_add(name, a, b, tile):
    q, r, c = b.shape
    tr, tc = tile
    assert r % tr == 0 and c % tc == 0

    def body(core_ref, a_ref, b_ref, o_ref):
        o_ref[...] = (a_ref[...].astype(F32) + b_ref[...].astype(F32)).astype(o_ref.dtype)

    blk = pl.BlockSpec((None, tr, tc), lambda s, i, j, core: (s, i, j))
    mine = pl.BlockSpec((None, None, tr, tc), lambda s, i, j, core: (s, core[0], i, j))
    return pl.pallas_call(
        body, name=name, out_shape=jax.ShapeDtypeStruct((q, r, c), BF16),
        grid_spec=pltpu.PrefetchScalarGridSpec(num_scalar_prefetch=1, grid=(q, r // tr, c // tc),
                                               in_specs=[mine, blk], out_specs=blk),
        compiler_params=_params("parallel", "parallel", "parallel"),
    )(lax.axis_index("c").astype(jnp.int32).reshape(1), a, b)


def _w_in_pieces():
    c = W_IN // N_DEV
    ranges = ((0, 8 * HW, 0, 0), (8 * HW, 8 * HW + 2 * HEADS, 1, 8 * HW), (8 * HW + 2 * HEADS, W_IN, 0, 2 * HEADS))
    out = []
    for d in range(N_DEV):
        for lo, hi, target, shift in ranges:
            s, e = max(lo, c * d), min(hi, c * (d + 1))
            if s < e:
                out.append((d, s - c * d, e - s, target, s - shift))
    return out


def _ffn_in_pieces():
    c = 2 * D_FF // N_DEV
    out = []
    for d in range(N_DEV):
        s = c * d
        while s < c * (d + 1):
            e = min((s // FF_BLK + 1) * FF_BLK, c * (d + 1))
            half, blk = divmod(s // FF_BLK, D_FF // FF_BLK)
            out.append((d, s - c * d, e - s, 0, (2 * blk + half) * FF_BLK + s % FF_BLK))
            s = e
    return out


def _plain_pieces(c):
    return [(d, 0, c, 0, c * d) for d in range(N_DEV)]


def shards_to_cols(name, g, widths, pieces, tr):
    _, r, c = g.shape
    covered = [sum(p[2] for p in pieces if p[3] == t) for t in range(len(widths))]

    def body(g_ref, *outs):
        for t, o in enumerate(outs):
            if covered[t] < widths[t]:
                o[...] = jnp.zeros(o.shape, o.dtype)
        for d, s, w, t, ts in pieces:
            outs[t][:, ts:ts + w] = g_ref[d, :, s:s + w]

    return pl.pallas_call(
        body, name=name, grid=(r // tr,),
        in_specs=[pl.BlockSpec((N_DEV, tr, c), lambda i: (0, i, 0))],
        out_specs=[pl.BlockSpec((tr, w), lambda i: (i, 0)) for w in widths],
        out_shape=[jax.ShapeDtypeStruct((r, w), g.dtype) for w in widths],
        compiler_params=_params("parallel"),
    )(g)


def cols_to_shards(name, srcs, c, pieces, tr):
    r = srcs[0].shape[0]

    def body(*refs):
        o = refs[-1]
        for d, s, w, t, ts in pieces:
            o[d, :, s:s + w] = refs[t][:, ts:ts + w]

    return pl.pallas_call(
        body, name=name, grid=(r // tr,),
        in_specs=[pl.BlockSpec((tr, t.shape[1]), lambda i: (i, 0)) for t in srcs],
        out_specs=pl.BlockSpec((N_DEV, tr, c), lambda i: (0, i, 0)),
        out_shape=jax.ShapeDtypeStruct((N_DEV, r, c), srcs[0].dtype),
        compiler_params=_params("parallel"),
    )(*srcs)


def rows_to_shards(name, srcs, c, pieces, tc):
    r = srcs[0].shape[1]

    def body(*refs):
        o = refs[-1]
        for d, s, w, t, ts in pieces:
            o[d, s:s + w, :] = refs[t][ts:ts + w, :]

    return pl.pallas_call(
        body, name=name, grid=(r // tc,),
        in_specs=[pl.BlockSpec((t.shape[0], tc), lambda i: (0, i)) for t in srcs],
        out_specs=pl.BlockSpec((N_DEV, c, tc), lambda i: (0, 0, i)),
        out_shape=jax.ShapeDtypeStruct((N_DEV, c, r), srcs[0].dtype),
        compiler_params=_params("parallel"),
    )(*srcs)


def _place():
    return lax.axis_index("x"), lax.axis_index("y"), lax.axis_index("c")


def _dev(px, py, pc):
    return 4 * px + 2 * py + pc


class Exchange:
    def __init__(self, arrays, out_shapes, sems, start, finish):
        self.arrays, self.out_shapes, self.sems, self.start, self.finish = arrays, out_shapes, sems, start, finish


def gather_exchange(arrays):
    n = len(arrays)

    def plan(ins, outs, sems):
        send_sems, recv_sems, local_sems = sems
        x, y, c = _place()
        me, sibling = (x, y, c), (x, y, 1 - c)
        chips = [(1 - x, y), (x, 1 - y), (1 - x, 1 - y)]

        def copy(a, k, block, to, src=None):
            dst = outs[a].at[_dev(*block)]
            return pltpu.make_async_remote_copy(
                src_ref=dst if src is None else src, dst_ref=dst, send_sem=send_sems.at[a, k],
                recv_sem=recv_sems.at[a, k], device_id=to, device_id_type=MESH)

        mine = [pltpu.make_async_copy(ins[a], outs[a].at[_dev(*me)], local_sems.at[a]) for a in range(n)]
        first = []
        for a in range(n):
            first.append(copy(a, 0, me, sibling, src=ins[a]))
            first += [copy(a, 1 + j, me, (*chip, c), src=ins[a]) for j, chip in enumerate(chips)]
        return me, sibling, chips, c, copy, mine, first

    def start(ins, outs, sems):
        *_, mine, first = plan(ins, outs, sems)
        for cp in mine + first:
            cp.start()

    def finish(ins, outs, sems):
        me, sibling, chips, c, copy, mine, first = plan(ins, outs, sems)
        passed = []
        for j, chip in enumerate(chips):
            for a in range(n):
                copy(a, 1 + j, (*chip, c), me).wait_recv()
                fwd = copy(a, 4 + j, (*chip, c), sibling)
                fwd.start()
                passed.append(fwd)
        for a in range(n):
            copy(a, 0, sibling, me).wait_recv()
        for j, chip in enumerate(chips):
            for a in range(n):
                copy(a, 4 + j, (*chip, 1 - c), me).wait_recv()
        for cp in first + passed:
            cp.wait_send()
        for cp in mine:
            cp.wait()

    return Exchange(arrays, [jax.ShapeDtypeStruct((N_DEV,) + t.shape, t.dtype) for t in arrays],
                    [pltpu.SemaphoreType.DMA((n, 7)), pltpu.SemaphoreType.DMA((n, 7)), pltpu.SemaphoreType.DMA((n,))],
                    start, finish)


def core_exchange(arrays):
    n = len(arrays)

    def copies(ins, outs, sems):
        send_sems, recv_sems = sems
        x, y, c = _place()
        return [pltpu.make_async_remote_copy(
            src_ref=ins[a].at[q, 1 - c], dst_ref=outs[a].at[q], send_sem=send_sems.at[a, q],
            recv_sem=recv_sems.at[a, q], device_id=(x, y, 1 - c), device_id_type=MESH)
            for a in range(n) for q in range(4)]

    def start(ins, outs, sems):
        for cp in copies(ins, outs, sems):
            cp.start()

    def finish(ins, outs, sems):
        for cp in copies(ins, outs, sems):
            cp.wait()

    return Exchange(arrays, [jax.ShapeDtypeStruct((4,) + t.shape[2:], t.dtype) for t in arrays],
                    [pltpu.SemaphoreType.DMA((n, 4)), pltpu.SemaphoreType.DMA((n, 4))], start, finish)


def chips_exchange(arrays):
    n = len(arrays)

    def copies(ins, outs, sems):
        send_sems, recv_sems = sems
        x, y, c = _place()
        chips = [(1 - x, y), (x, 1 - y), (1 - x, 1 - y)]
        return [pltpu.make_async_remote_copy(
            src_ref=ins[a].at[2 * qx + qy], dst_ref=outs[a].at[j], send_sem=send_sems.at[a, j],
            recv_sem=recv_sems.at[a, j], device_id=(qx, qy, c), device_id_type=MESH)
            for a in range(n) for j, (qx, qy) in enumerate(chips)]

    def start(ins, outs, sems):
        for cp in copies(ins, outs, sems):
            cp.start()

    def finish(ins, outs, sems):
        for cp in copies(ins, outs, sems):
            cp.wait()

    return Exchange(arrays, [jax.ShapeDtypeStruct((3,) + t.shape[1:], t.dtype) for t in arrays],
                    [pltpu.SemaphoreType.DMA((n, 3)), pltpu.SemaphoreType.DMA((n, 3))], start, finish)


def run_exchange(name, exch):
    n_in, n_out = len(exch.arrays), len(exch.out_shapes)
    any_spec = pl.BlockSpec(memory_space=pl.ANY)

    def body(*refs):
        ins, outs, sems = refs[:n_in], refs[n_in:n_in + n_out], refs[n_in + n_out:]
        exch.start(ins, outs, sems)
        exch.finish(ins, outs, sems)

    return pl.pallas_call(
        body, name=name, in_specs=[any_spec] * n_in, out_specs=[any_spec] * n_out, out_shape=exch.out_shapes,
        scratch_shapes=exch.sems,
    )(*exch.arrays)


def host_exchange(body, n_in, n_out, exch, first, last):
    ne_in, ne_out, ne_sem = len(exch.arrays), len(exch.out_shapes), len(exch.sems)

    def wrapped(*refs):
        ins, refs = refs[:n_in], refs[n_in:]
        e_ins, refs = refs[:ne_in], refs[ne_in:]
        outs, refs = refs[:n_out], refs[n_out:]
        e_outs, refs = refs[:ne_out], refs[ne_out:]
        scratch, e_sems = refs[:len(refs) - ne_sem], refs[len(refs) - ne_sem:]

        @pl.when(first())
        def _():
            exch.start(e_ins, e_outs, e_sems)

        body(*ins, *outs, *scratch)

        @pl.when(last())
        def _():
            exch.finish(e_ins, e_outs, e_sems)

    return wrapped


def _hosted(exch, in_specs, out_specs, out_shape, scratch_shapes):
    any_spec = pl.BlockSpec(memory_space=pl.ANY)
    return dict(in_specs=list(in_specs) + [any_spec] * len(exch.arrays),
                out_specs=list(out_specs) + [any_spec] * len(exch.out_shapes),
                out_shape=list(out_shape) + list(exch.out_shapes),
                scratch_shapes=list(scratch_shapes) + list(exch.sems))


def allreduce_small(buf):
    r = buf.shape[0]
    vmem = pl.BlockSpec(memory_space=pltpu.VMEM)

    def body(buf_ref, out_ref, all_ref, send_sems, recv_sems):
        x, y, c = _place()
        me = _dev(x, y, c)
        copies = []
        for k in range(1, N_DEV):
            peer = (x ^ (k >> 2), y ^ ((k >> 1) & 1), c ^ (k & 1))
            copies.append(pltpu.make_async_remote_copy(
                src_ref=buf_ref, dst_ref=all_ref.at[me], send_sem=send_sems.at[k - 1], recv_sem=recv_sems.at[k - 1],
                device_id=peer, device_id_type=MESH))
        for cp in copies:
            cp.start()
        all_ref[me] = buf_ref[...]
        for cp in copies:
            cp.wait()
        total = all_ref[0]
        for d in range(1, N_DEV):
            total = total + all_ref[d]
        out_ref[...] = total

    return pl.pallas_call(
        body, name="allreduce_small", in_specs=[vmem], out_specs=vmem,
        out_shape=jax.ShapeDtypeStruct((r, LANES), F32),
        scratch_shapes=[pltpu.VMEM((N_DEV, r, LANES), F32), pltpu.SemaphoreType.DMA((N_DEV - 1,)),
                        pltpu.SemaphoreType.DMA((N_DEV - 1,))],
    )(buf)


def _cols_from_shards(g):
    return jnp.transpose(g, (1, 0, 2)).reshape(g.shape[1], N_DEV * g.shape[2])


def _pack(t):
    flat = t.reshape(-1)
    return jnp.pad(flat, (0, -flat.size % (8 * LANES))).reshape(-1, LANES)


def _pad_lanes(t):
    t = t.reshape(1, -1)
    return jnp.pad(t, ((0, 0), (0, LANES - t.shape[1])))


def kernel(x, meta_tokens, lb_logits, mix_norm_w, w_in, hg_norm_w, gd_conv_w, gd_a_log, gd_dt_bias, gd_norm_w, w_branch_a, w_branch_b, w_out, ffn_norm_w, w_ffn_in, w_ffn_out, final_norm_w, loss_target, m_meta_tokens, m_lb_logits, m_mix_norm_w, m_w_in, m_hg_norm_w, m_gd_conv_w, m_gd_a_log, m_gd_dt_bias, m_gd_norm_w, m_w_branch_a, m_w_branch_b, m_w_out, m_ffn_norm_w, m_w_ffn_in, m_w_ffn_out, m_final_norm_w, v_meta_tokens, v_lb_logits, v_mix_norm_w, v_w_in, v_hg_norm_w, v_gd_conv_w, v_gd_a_log, v_gd_dt_bias, v_gd_norm_w, v_w_branch_a, v_w_branch_b, v_w_out, v_ffn_norm_w, v_w_ffn_in, v_w_ffn_out, v_final_norm_w):
    xs = x[0]
    tgt = loss_target[0]
    l = xs.shape[0]
    lp = PRE + l
    me = _dev(*_place())

    big = [w_in[0], w_branch_a[0], w_branch_b[0], w_out[0], w_ffn_in[0], w_ffn_out[0]]
    big16 = [t.astype(BF16) for t in big]
    g_in, g_meta, g_conv = run_exchange("gather_first", gather_exchange([big16[0], meta_tokens, gd_conv_w[0]]))
    w_main, w_ab = shards_to_cols("w_in_cols", g_in, [W_MAIN, LANES], _w_in_pieces(), 256)
    meta_full = _cols_from_shards(g_meta)
    cw = _cols_from_shards(g_conv)
    pre = jnp.concatenate([jnp.zeros((PRE - N_META, D), F32), meta_full], axis=0)
    l0, l1 = lb_logits[0:1], lb_logits[1:2]
    alog, dtb = _pad_lanes(gd_a_log), _pad_lanes(gd_dt_bias)

    xn = norm1_fwd(pre, xs, mix_norm_w)
    proj, (g_ba, g_bb, g_out, g_ffi, g_ffo) = matmul("proj_main", xn, w_main, "nn", lp, W_MAIN, D, F32,
                                                      exch=gather_exchange(big16[1:]))
    wa, = shards_to_cols("w_branch_a_cols", g_ba, [D], _plain_pieces(D // N_DEV), 256)
    wb, = shards_to_cols("w_branch_b_cols", g_bb, [D], _plain_pieces(D // N_DEV), 256)
    wo = g_out.reshape(D, D)
    w_gu, = shards_to_cols("w_ffn_in_cols", g_ffi, [2 * D_FF], _ffn_in_pieces(), 256)
    w_fo = g_ffo.reshape(D_FF, D)
    proj_ab = matmul("proj_ab", xn, w_ab, "nn", lp, LANES, D, F32)
    o_a, st_a, o_b, st_b, t_inv = mixer_fwd(proj, proj_ab, l0, l1, hg_norm_w, cw, alog, dtb, gd_norm_w)
    ya = matmul("branch_a", o_a, wa, "nn", l, D, HW, F32, a_off=(PRE // 512, 0))
    yb = matmul("branch_b", o_b, wb, "nn", l, D, HW, F32, a_off=(PRE // 512, 0))
    merged, h1, xn2 = merge_mix_out(proj, ya, yb, wo, xs, ffn_norm_w)
    gu, act = ffn_in_fused(xn2, w_gu)
    f = matmul("ffn_out", act, w_fo, "nn", l, D, D_FF, F32, tn=512)
    loss_part, dh2, dh2_b, d_final_w = loss_head(h1, f, final_norm_w.reshape(1, D), tgt)

    tiles = {"w_in": (W_IN // N_DEV, 256), "w_branch_a": (256, D // N_DEV), "w_branch_b": (256, D // N_DEV),
             "w_out": (64, D), "w_ffn_in": (256, 2 * D_FF // N_DEV), "w_ffn_out": (176, D)}

    def chip_sums(tag, parts, host=None):
        blocks = [p.reshape((4, 2) + p.shape[1:]) for p in parts.values()]
        exch = core_exchange(blocks)
        hosted, from_core = (None, run_exchange("exchange_core_" + tag, exch)) if host is None else host(exch)
        return hosted, {nm: pair_add("pair_add_" + nm, p, r, tiles[nm]) for (nm, p, r) in zip(parts, blocks, from_core)}

    dgu = d_act_fused(dh2_b, w_fo, gu)
    dw_fo = matmul("dw_ffn_out", act, dh2_b, "tn", D_FF, D, l, BF16, tm=512, tn=512)
    dxn2 = matmul("d_xn2", dgu, w_gu, "nt", l, D, 2 * D_FF, F32, tn=256, rows_outer=True)
    dw_gu = matmul("dw_ffn_in", xn2, dgu, "tn", D, 2 * D_FF, l, BF16, tm=512, tn=512)
    dh1, dh1_b, d_ffn_norm_w = norm2_bwd(h1, ffn_norm_w, dxn2, dh2)
    (dproj, dya, dyb), sums = chip_sums("ffn", {
        "w_ffn_in": cols_to_shards("dw_ffn_in_shards", [dw_gu], 2 * D_FF // N_DEV, _ffn_in_pieces(), 256),
        "w_ffn_out": dw_fo.reshape(N_DEV, D_FF // N_DEV, D)},
        host=lambda exch: merge_bwd(dh1_b, wo, proj, ya, yb, exch))
    dw_o = matmul("dw_out", merged, dh1_b, "tn", D, D, l, BF16, tm=512, tn=512)
    do_a = matmul("d_o_a", dya, wa, "nt", lp, HW, D, F32)
    do_b = matmul("d_o_b", dyb, wb, "nt", lp, HW, D, F32)
    dw_a = matmul("dw_branch_a", o_a, dya, "tn", HW, D, lp, BF16, tm=512, tn=512)
    dw_b = matmul("dw_branch_b", o_b, dyb, "tn", HW, D, lp, BF16, tm=512, tn=512)
    sums.update(chip_sums("mix", {
        "w_branch_a": cols_to_shards("dw_branch_a_shards", [dw_a], D // N_DEV, _plain_pieces(D // N_DEV), 256),
        "w_branch_b": cols_to_shards("dw_branch_b_shards", [dw_b], D // N_DEV, _plain_pieces(D // N_DEV), 256),
        "w_out": dw_o.reshape(N_DEV, D // N_DEV, D)})[1])
    ffn_names, mix_names = ["w_ffn_in", "w_ffn_out"], ["w_branch_a", "w_branch_b", "w_out"]
    (dproj, dl0, dl1, d_hg_nw, dproj_ab, d_conv, d_alog, d_dtb, d_gd_nw), from_chips_early = mixer_bwd(
        proj, proj_ab, l0, l1, hg_norm_w, cw, alog, dtb, gd_norm_w, st_a, st_b, t_inv, do_a, do_b, dproj,
        chips_exchange([sums[nm] for nm in ffn_names + mix_names]))
    dwt_main = matmul("dw_in_main", dproj, xn, "tn", W_MAIN, D, lp, BF16, tm=512, tn=512)
    dwt_ab = matmul("dw_in_ab", dproj_ab, xn, "tn", LANES, D, lp, BF16, tn=512)
    dxn_ab, sums_in = chip_sums("w_in", {
        "w_in": rows_to_shards("dw_in_shards", [dwt_main, dwt_ab], W_IN // N_DEV, _w_in_pieces(), 256)},
        host=lambda exch: matmul("d_xn_ab", dproj_ab, w_ab, "nt", lp, D, LANES, F32, exch=exch))
    sums.update(sums_in)
    from_chips = dict(zip(ffn_names + mix_names, from_chips_early))
    dxn, (from_chips["w_in"],) = matmul("d_xn", dproj, w_main, "nt", lp, D, W_MAIN, F32, tn=256, rows_outer=True,
                                        exch=chips_exchange([sums["w_in"]]))
    grad_x, dpre, d_mix_norm_w = norm1_bwd(pre, xs, mix_norm_w, dxn, dxn_ab, dh1)
    names = ["w_in", "w_branch_a", "w_branch_b", "w_out", "w_ffn_in", "w_ffn_out"]
    moms = [(m_w_in, v_w_in), (m_w_branch_a, v_w_branch_a), (m_w_branch_b, v_w_branch_b), (m_w_out, v_w_out),
            (m_w_ffn_in, v_w_ffn_in), (m_w_ffn_out, v_w_ffn_out)]
    upd = {}
    for nm, w, (m, v) in zip(names, big, moms):
        flip = (lambda t: t.T) if nm == "w_in" else (lambda t: t)
        res = adamw_shard("adamw_" + nm, flip(w), flip(m[0]), flip(v[0]), sums[nm], from_chips[nm], tiles[nm])
        upd[nm] = tuple(flip(t_)[None] for t_ in res)

    d_lb = jnp.concatenate([dl0, dl1], axis=0)
    rep = [_pack(t) for t in (d_lb, d_mix_norm_w, d_hg_nw, d_alog, d_dtb, d_gd_nw, d_ffn_norm_w, d_final_w)]
    n_rep = sum(t.shape[0] for t in rep)
    n_meta = N_META * D // LANES
    tot = allreduce_small(jnp.concatenate(rep + [_pack(loss_part), _pack(dpre[PRE - N_META:]), _pack(d_conv)], axis=0))
    loss = tot[n_rep, 0]
    g_meta_full = tot[n_rep + 8:n_rep + 8 + n_meta].reshape(N_META, D)
    g_conv_full = tot[n_rep + 8 + n_meta:].reshape(CONV_K, 3 * HW)
    g_meta_mine = lax.dynamic_slice_in_dim(g_meta_full, me * (D // N_DEV), D // N_DEV, axis=1)
    g_conv_mine = lax.dynamic_slice_in_dim(g_conv_full, me * (3 * DH), 3 * DH, axis=1)

    small = [("lb_logits", lb_logits, m_lb_logits, v_lb_logits), ("mix_norm_w", mix_norm_w, m_mix_norm_w, v_mix_norm_w),
             ("hg_norm_w", hg_norm_w, m_hg_norm_w, v_hg_norm_w), ("gd_a_log", gd_a_log, m_gd_a_log, v_gd_a_log),
             ("gd_dt_bias", gd_dt_bias, m_gd_dt_bias, v_gd_dt_bias), ("gd_norm_w", gd_norm_w, m_gd_norm_w, v_gd_norm_w),
             ("ffn_norm_w", ffn_norm_w, m_ffn_norm_w, v_ffn_norm_w),
             ("final_norm_w", final_norm_w, m_final_norm_w, v_final_norm_w),
             ("meta_tokens", meta_tokens, m_meta_tokens, v_meta_tokens), ("gd_conv_w", gd_conv_w, m_gd_conv_w, v_gd_conv_w)]

    sizes = [_pack(w).shape[0] for _, w, _, _ in small]
    g_small = jnp.concatenate([tot[:n_rep], _pack(g_meta_mine), _pack(g_conv_mine)], axis=0)
    assert g_small.shape[0] == sum(sizes)
    w_small = jnp.concatenate([_pack(w) for _, w, _, _ in small], axis=0)
    m_small = jnp.concatenate([_pack(m) for _, _, m, _ in small], axis=0)
    v_small = jnp.concatenate([_pack(v) for _, _, _, v in small], axis=0)
    d_small, nm_small, nv_small = adamw_small(w_small, g_small, m_small, v_small)
    row = 0
    for (nm, w, _, _), sz in zip(small, sizes):
        def unpack(t):
            return t[row:row + sz].reshape(-1)[:w.size].reshape(w.shape)
        upd[nm] = (unpack(g_small), unpack(d_small), unpack(nm_small), unpack(nv_small))
        row += sz

    order = ["meta_tokens", "lb_logits", "mix_norm_w", "w_in", "hg_norm_w", "gd_conv_w", "gd_a_log", "gd_dt_bias",
             "gd_norm_w", "w_branch_a", "w_branch_b", "w_out", "ffn_norm_w", "w_ffn_in", "w_ffn_out", "final_norm_w"]
    outs = [loss, grad_x[None]]
    for j in range(4):
        outs += [upd[nm][j] for nm in order]
    return tuple(outs)
```

```python
import functools

import jax
import jax.numpy as jnp
from jax import lax
from jax.experimental import pallas as pl
from jax.experimental.pallas import tpu as pltpu

F32 = jnp.float32
BF16 = jnp.bfloat16
MESH = pl.DeviceIdType.MESH

EPS = 1e-6
D = 2048
N_META = 16
CHUNK = 64
SUB = 16
HEADS = 8
DH = 128
HW = HEADS * DH
CONV_K = 4
TAIL = 8
D_FF = 5632
PRE = 512
N_SKIP = PRE // CHUNK - 1
N_DEV = 8
LANES = 128
FF_BLK = 512
W_IN = 4 * HW + 4 * HW + 2 * HEADS + 2 * D
W_MAIN = 4 * HW + 4 * HW + 2 * D

ADAM_LR = 0.001
ADAM_B1 = 0.9
ADAM_B2 = 0.999
ADAM_EPS = 1e-08
ADAM_WD = 0.01
ADAM_STEP = 10

VMEM_LIMIT = 52 * 1024 * 1024

_NN = ((1,), (0,))
_NT = ((1,), (1,))
_TN = ((0,), (0,))


def _params(*sem):
    return pltpu.CompilerParams(dimension_semantics=sem, vmem_limit_bytes=VMEM_LIMIT)


BF16_ONCE, SPLIT, EXACT_LHS = 0, 1, 2


def _dot_bf16(a, b, dims):
    if a.ndim == 3:
        dn = (((dims[0][0] + 1,), (dims[1][0] + 1,)), ((0,), (0,)))
    else:
        dn = (dims, ((), ()))
    return lax.dot_general(a, b, dn, preferred_element_type=F32)


def _split(t):
    hi = t.astype(BF16)
    return hi, (t - hi.astype(F32)).astype(BF16)


def _raw_dot(a, b, dims, mode):
    if mode == BF16_ONCE:
        return _dot_bf16(a.astype(BF16), b.astype(BF16), dims)
    if mode == SPLIT:
        a_hi, a_lo = _split(a)
        b_hi, b_lo = _split(b)
        return _dot_bf16(a_hi, b_hi, dims) + (_dot_bf16(a_hi, b_lo, dims) + _dot_bf16(a_lo, b_hi, dims))
    a = a.astype(BF16)
    b_hi = b.astype(BF16)
    rest = b - b_hi.astype(F32)
    b_mid = rest.astype(BF16)
    b_lo = (rest - b_mid.astype(F32)).astype(BF16)
    return _dot_bf16(a, b_hi, dims) + (_dot_bf16(a, b_mid, dims) + _dot_bf16(a, b_lo, dims))


@functools.partial(jax.custom_vjp, nondiff_argnums=(2,))
def mm_nn(a, b, mode=BF16_ONCE):
    return _raw_dot(a, b, _NN, mode)


@functools.partial(jax.custom_vjp, nondiff_argnums=(2,))
def mm_nt(a, b, mode=BF16_ONCE):
    return _raw_dot(a, b, _NT, mode)


@functools.partial(jax.custom_vjp, nondiff_argnums=(2,))
def mm_tn(a, b, mode=BF16_ONCE):
    return _raw_dot(a, b, _TN, mode)


def _general(mode):
    return SPLIT if mode == EXACT_LHS else mode


mm_nn.defvjp(lambda a, b, p: (_raw_dot(a, b, _NN, p), (a, b)),
             lambda p, res, g: (mm_nt(g, res[1], _general(p)), mm_tn(res[0], g, p)))
mm_nt.defvjp(lambda a, b, p: (_raw_dot(a, b, _NT, p), (a, b)),
             lambda p, res, g: (mm_nn(g, res[1], p), mm_tn(g, res[0], p)))
mm_tn.defvjp(lambda a, b, p: (_raw_dot(a, b, _TN, p), (a, b)),
             lambda p, res, g: (mm_nt(res[1], g, _general(p)), mm_nn(res[0], g, _general(p))))


def _sigmoid(x):
    return jax.nn.sigmoid(x)


def _silu(x):
    return x * _sigmoid(x)


def _softplus(x):
    return jnp.maximum(x, 0.0) + jnp.log(1.0 + jnp.exp(-jnp.abs(x)))


def _l2n(t):
    return t * lax.rsqrt(jnp.sum(t * t, axis=-1, keepdims=True) + EPS)


def _rms_norm(x, w):
    return x * lax.rsqrt(jnp.mean(x * x, axis=-1, keepdims=True) + EPS) * w


def _gated_norm(o, gate, nw):
    o = o * lax.rsqrt(jnp.mean(o * o, axis=-1, keepdims=True) + EPS) * nw
    return o * _silu(gate)


def _heads(ref, piece):
    return jnp.stack([ref[:, piece * HW + DH * j:piece * HW + DH * (j + 1)] for j in range(HEADS)])


def _put_heads(ref, piece, value, accumulate=False):
    for j in range(HEADS):
        cols = slice(piece * HW + DH * j, piece * HW + DH * (j + 1))
        if accumulate:
            ref[:, cols] += value[j]
        else:
            ref[:, cols] = value[j].astype(ref.dtype)


def _diag_scores(q_i, k_i, f_i):
    nb, s, d = q_i.shape
    row = lax.broadcasted_iota(jnp.int32, (nb, s, d), 1)
    q4, f4 = q_i[:, :, None, :], f_i[:, :, None, :]
    kk = k_i
    rows_out = []
    for t in range(s):
        if t > 0:
            kk = jnp.where(row < t, kk * f4[:, t], kk)
        rows_out.append(jnp.sum(kk * q4[:, t], axis=-1)[:, None, :])
    s_ii = jnp.concatenate(rows_out, axis=1)
    tri = lax.broadcasted_iota(jnp.int32, (nb, s, s), 1) >= lax.broadcasted_iota(jnp.int32, (nb, s, s), 2)
    return jnp.where(tri, s_ii, 0.0)


def hg_chunk(hq, hf, hi, hg, l0, l1, nw, st):
    nb = hq.shape[0]
    m = jnp.maximum(l0, l1)
    e0 = jnp.exp(l0 - m)
    e1 = jnp.exp(l1 - m)
    lb = e0 / (e0 + e1)
    sig = _sigmoid(hf)
    q = _silu(hq)
    f = lb + (1.0 - lb) * sig
    log_f = jnp.log(f)
    k = (1.0 - lb) * _sigmoid(-hf)
    v = hi
    rows = lax.broadcasted_iota(jnp.int32, (nb, CHUNK, CHUNK), 1)
    cols = lax.broadcasted_iota(jnp.int32, (nb, CHUNK, CHUNK), 2)
    b = mm_nn((rows >= cols).astype(F32), log_f, EXACT_LHS)
    o_inter = mm_nt(q * jnp.exp(b), st)
    outs = []
    for i in range(CHUNK // SUB):
        lo = i * SUB
        b_i, q_i, k_i, v_i = b[:, lo:lo + SUB], q[:, lo:lo + SUB], k[:, lo:lo + SUB], v[:, lo:lo + SUB]
        o_i = mm_nn(_diag_scores(q_i, k_i, f[:, lo:lo + SUB]), v_i)
        if i > 0:
            b_ref = b[:, :, None, :][:, lo - 1]
            q_t = q_i * jnp.exp(b_i - b_ref)
            k_t = k[:, :lo] * jnp.exp(b_ref - b[:, :lo])
            o_i = o_i + mm_nn(mm_nt(q_t, k_t), v[:, :lo])
        outs.append(o_i)
    o = o_inter + jnp.concatenate(outs, axis=1)
    b_last = b[:, :, None, :][:, CHUNK - 1]
    st_new = st * jnp.exp(b_last) + mm_tn(v, k * jnp.exp(b_last - b))
    return _gated_norm(o, hg, nw), st_new


@jax.custom_vjp
def _unit_lower_inverse(a):
    n = a.shape[-1]
    eye = (lax.broadcasted_iota(jnp.int32, a.shape, 1) == lax.broadcasted_iota(jnp.int32, a.shape, 2)).astype(F32)
    x = eye - a
    p = mm_nn(a, a, SPLIT)
    x = x + mm_nn(x, p, SPLIT)
    power = 4
    while power < n:
        p = mm_nn(p, p, SPLIT)
        x = x + mm_nn(x, p, SPLIT)
        power *= 2
    return x


def _unit_lower_inverse_fwd(a):
    t = _unit_lower_inverse(a)
    return t, t


def _unit_lower_inverse_bwd(t, dt):
    return (-mm_tn(t, mm_nt(dt, t, SPLIT), SPLIT),)


_unit_lower_inverse.defvjp(_unit_lower_inverse_fwd, _unit_lower_inverse_bwd)


@jax.custom_vjp
def _kept_inverse(a, t):
    return t


_kept_inverse.defvjp(lambda a, t: (t, t),
                     lambda t, dt: (-mm_tn(t, mm_nt(dt, t, SPLIT), SPLIT), jnp.zeros_like(t)))


def gd_chunk(cq, ck, cv, z, ab, alog, dtb, nw, s, kept_t=None, keep_t=False):
    nb, c, _ = cq.shape
    lane = lax.broadcasted_iota(jnp.int32, (nb, 1, DH), 2)
    head = lax.broadcasted_iota(jnp.int32, (nb, 1, DH), 0)
    pick = lambda t, off: jnp.sum(jnp.where(lane == head + off, t[None], 0.0), axis=2, keepdims=True)
    a_raw, b_raw = pick(ab, 0), pick(ab, HEADS)
    beta = _sigmoid(b_raw)
    g = -jnp.exp(pick(alog, 0)) * _softplus(a_raw + pick(dtb, 0))
    q = _l2n(_silu(cq)) * (DH ** -0.5)
    k = _l2n(_silu(ck))
    v = _silu(cv)
    rows = lax.broadcasted_iota(jnp.int32, (nb, c, c), 1)
    cols = lax.broadcasted_iota(jnp.int32, (nb, c, c), 2)
    tril = (rows >= cols).astype(F32)
    gc_w = mm_nn(tril, jnp.broadcast_to(g, (nb, c, DH)), EXACT_LHS)
    g_sq = jnp.broadcast_to(g, (nb, c, c))
    gc_col = mm_nn(tril, g_sq, EXACT_LHS)
    gc_row = mm_nn(jnp.ones((nb, c, c), F32), g_sq * (rows <= cols).astype(F32), EXACT_LHS)
    rel = jnp.exp(jnp.minimum(gc_col - gc_row, 0.0))
    a = jnp.where(rows > cols, beta * mm_nt(k, k) * rel, 0.0)
    t_inv = _unit_lower_inverse(a) if kept_t is None else _kept_inverse(a, kept_t)
    e_gc = jnp.exp(gc_w)
    w = mm_nn(t_inv, beta * e_gc * k, SPLIT)
    u = mm_nn(t_inv, beta * v, SPLIT)
    v_new = u - mm_nn(w, s)
    attn = jnp.where(rows >= cols, mm_nt(q, k) * rel, 0.0)
    o = mm_nn(q * e_gc, s) + mm_nn(attn, v_new)
    g_last = jnp.sum(g, axis=1, keepdims=True)
    s_new = jnp.exp(g_last) * s + mm_tn(k * jnp.exp(g_last - gc_w), v_new)
    out = _gated_norm(o, z, nw)
    return (out, s_new, t_inv) if keep_t else (out, s_new)


def _gd_conv(x_ref, tail_ref, cw_ref, xe_ref, first):
    xe_ref[0:TAIL, :] = jnp.where(first, 0.0, tail_ref[:, 0:3 * HW])
    xe_ref[TAIL:TAIL + CHUNK, :] = x_ref[:, 0:3 * HW]
    y = cw_ref[pl.ds(0, 1), :] * xe_ref[pl.ds(TAIL - CONV_K + 1, CHUNK), :]
    for k in range(1, CONV_K):
        y = y + cw_ref[pl.ds(k, 1), :] * xe_ref[pl.ds(TAIL - CONV_K + 1 + k, CHUNK), :]
    return y


def _conv_heads(y, piece):
    return jnp.stack([y[:, piece * HW + DH * j:piece * HW + DH * (j + 1)] for j in range(HEADS)])


def mixer_fwd(proj, proj_ab, l0, l1, hg_nw, cw, alog, dtb, gd_nw):
    lp = proj.shape[0]
    nc = lp // CHUNK

    def body(xh_ref, l0_ref, l1_ref, hnw_ref, xg_ref, tail_ref, ab_ref, cw_ref, alog_ref, dtb_ref, gnw_ref,
             oa_ref, sta_out_ref, ob_ref, stb_out_ref, tinv_ref, sta_ref, stb_ref, xe_ref):
        c = pl.program_id(0)

        @pl.when(c == 0)
        def _():
            sta_ref[...] = jnp.zeros(sta_ref.shape, F32)
            stb_ref[...] = jnp.zeros(stb_ref.shape, F32)

        @pl.when(c < N_SKIP)
        def _():
            for ref in (oa_ref, sta_out_ref, ob_ref, stb_out_ref, tinv_ref):
                ref[...] = jnp.zeros(ref.shape, ref.dtype)

        @pl.when(c >= N_SKIP)
        def _():
            stb = stb_ref[...]
            stb_out_ref[0] = stb
            y = _gd_conv(xg_ref, tail_ref, cw_ref, xe_ref, c == 0)
            ob, stb_new, tinv_ref[0] = gd_chunk(
                _conv_heads(y, 0), _conv_heads(y, 1), _conv_heads(y, 2), _heads(xg_ref, 3),
                ab_ref[...], alog_ref[...], dtb_ref[...], gnw_ref[...], stb, keep_t=True)
            _put_heads(ob_ref, 0, ob)
            stb_ref[...] = stb_new
            sta = sta_ref[...]
            sta_out_ref[0] = sta
            oa, sta_new = hg_chunk(_heads(xh_ref, 0), _heads(xh_ref, 1), _heads(xh_ref, 2), _heads(xh_ref, 3),
                                   _heads(l0_ref, 0), _heads(l1_ref, 0), hnw_ref[...], sta)
            _put_heads(oa_ref, 0, oa)
            sta_ref[...] = sta_new

    vec = pl.BlockSpec((1, HW), lambda c: (0, 0))
    one = pl.BlockSpec((1, DH), lambda c: (0, 0))
    st_spec = pl.BlockSpec((1, HEADS, DH, DH), lambda c: (c, 0, 0, 0))
    o_spec = pl.BlockSpec((CHUNK, HW), lambda c: (c, 0))
    o_shape = jax.ShapeDtypeStruct((lp, HW), BF16)
    st_shape = jax.ShapeDtypeStruct((nc, HEADS, DH, DH), F32)
    return pl.pallas_call(
        body, name="mixer_fwd", grid=(nc,),
        in_specs=[pl.BlockSpec((CHUNK, 4 * HW), lambda c: (c, 0)), vec, vec, one,
                  pl.BlockSpec((CHUNK, 4 * HW), lambda c: (c, 1)),
                  pl.BlockSpec((TAIL, 4 * HW), lambda c: (jnp.maximum(c * (CHUNK // TAIL) - 1, 0), 1)),
                  pl.BlockSpec((CHUNK, DH), lambda c: (c, 0)),
                  pl.BlockSpec((CONV_K, 3 * HW), lambda c: (0, 0)), one, one, one],
        out_specs=[o_spec, st_spec, o_spec, st_spec,
                   pl.BlockSpec((1, HEADS, CHUNK, CHUNK), lambda c: (c, 0, 0, 0))],
        out_shape=[o_shape, st_shape, o_shape, st_shape, jax.ShapeDtypeStruct((nc, HEADS, CHUNK, CHUNK), F32)],
        scratch_shapes=[pltpu.VMEM((HEADS, DH, DH), F32), pltpu.VMEM((HEADS, DH, DH), F32),
                        pltpu.VMEM((TAIL + CHUNK, 3 * HW), F32)],
        compiler_params=_params("arbitrary"),
    )(proj, l0, l1, hg_nw, proj, proj, proj_ab, cw, alog, dtb, gd_nw)


def mixer_bwd(proj, proj_ab, l0, l1, hg_nw, cw, alog, dtb, gd_nw, st_a, st_b, t_inv, do_a, do_b, dproj, exch):
    lp = proj.shape[0]
    nc = lp // CHUNK

    def body(xh_ref, l0_ref, l1_ref, hnw_ref, xg_ref, tail_ref, ab_ref, cw_ref, alog_ref, dtb_ref, gnw_ref,
             sta_in_ref, stb_in_ref, tinv_ref, doa_ref, dob_ref, _,
             dx_ref, dl0_ref, dl1_ref, dhnw_ref, dab_ref, dcw_ref, dalog_ref, ddtb_ref, dgnw_ref,
             dsta_ref, dstb_ref, xe_ref, dye_ref, head_ref):
        c = pl.program_id(0)
        cc = nc - 1 - c

        @pl.when(c == 0)
        def _():
            for ref in (dl0_ref, dl1_ref, dhnw_ref, dcw_ref, dalog_ref, ddtb_ref, dgnw_ref, dsta_ref, dstb_ref,
                        head_ref):
                ref[...] = jnp.zeros(ref.shape, F32)

        @pl.when(cc < N_SKIP)
        def _():
            dx_ref[...] = jnp.zeros(dx_ref.shape, dx_ref.dtype)
            dab_ref[...] = jnp.zeros(dab_ref.shape, F32)

        @pl.when(cc >= N_SKIP)
        def _():
            y = _gd_conv(xg_ref, tail_ref, cw_ref, xe_ref, cc == 0)
            _, gd_vjp = jax.vjp(functools.partial(gd_chunk, kept_t=tinv_ref[0]),
                                _conv_heads(y, 0), _conv_heads(y, 1), _conv_heads(y, 2), _heads(xg_ref, 3),
                                ab_ref[...], alog_ref[...], dtb_ref[...], gnw_ref[...], stb_in_ref[0])
            dcq, dck, dcv, dz, dab, dalog, ddtb, dgnw, dstb = gd_vjp((_heads(dob_ref, 0), dstb_ref[...]))
            _, hg_vjp = jax.vjp(hg_chunk, _heads(xh_ref, 0), _heads(xh_ref, 1), _heads(xh_ref, 2), _heads(xh_ref, 3),
                                _heads(l0_ref, 0), _heads(l1_ref, 0), hnw_ref[...], sta_in_ref[0])
            hg_grads = hg_vjp((_heads(doa_ref, 0), dsta_ref[...]))
            dstb_ref[...] = dstb
            dab_ref[...] = dab
            dalog_ref[...] += dalog
            ddtb_ref[...] += ddtb
            dgnw_ref[...] += dgnw
            _put_heads(dx_ref, 4 + 3, dz)
            for i, t in enumerate((dcq, dck, dcv)):
                for j in range(HEADS):
                    dye_ref[0:CHUNK, i * HW + DH * j:i * HW + DH * (j + 1)] = t[j]
            dye_ref[CHUNK:CHUNK + TAIL, :] = head_ref[...]
            head_ref[...] = dye_ref[0:TAIL, :]
            dy = dye_ref[0:CHUNK, :]
            dx = None
            for k in range(CONV_K):
                term = cw_ref[pl.ds(k, 1), :] * dye_ref[pl.ds(CONV_K - 1 - k, CHUNK), :]
                dx = term if dx is None else dx + term
                dcw_ref[pl.ds(k, 1), :] += jnp.sum(dy * xe_ref[pl.ds(TAIL - CONV_K + 1 + k, CHUNK), :],
                                                   axis=0, keepdims=True)
            dx_ref[:, 4 * HW:7 * HW] = dx.astype(dx_ref.dtype)
            for i in range(4):
                _put_heads(dx_ref, i, hg_grads[i])
            _put_heads(dl0_ref, 0, hg_grads[4], accumulate=True)
            _put_heads(dl1_ref, 0, hg_grads[5], accumulate=True)
            dhnw_ref[...] += hg_grads[6]
            dsta_ref[...] = hg_grads[7]

    rc = lambda c: nc - 1 - c
    vec = pl.BlockSpec((1, HW), lambda c: (0, 0))
    one = pl.BlockSpec((1, DH), lambda c: (0, 0))
    one_shape = jax.ShapeDtypeStruct((1, DH), F32)
    vec_shape = jax.ShapeDtypeStruct((1, HW), F32)
    st_spec = pl.BlockSpec((1, HEADS, DH, DH), lambda c: (rc(c), 0, 0, 0))
    do_spec = pl.BlockSpec((CHUNK, HW), lambda c: (rc(c), 0))
    n_in, n_out = 17, 9
    outs = pl.pallas_call(
        host_exchange(body, n_in, n_out, exch, lambda: pl.program_id(0) == 0, lambda: pl.program_id(0) == nc - 1),
        name="mixer_bwd", grid=(nc,), input_output_aliases={n_in - 1: 0}, compiler_params=_params("arbitrary"),
        **_hosted(exch,
                  [pl.BlockSpec((CHUNK, 4 * HW), lambda c: (rc(c), 0)), vec, vec, one,
                   pl.BlockSpec((CHUNK, 4 * HW), lambda c: (rc(c), 1)),
                   pl.BlockSpec((TAIL, 4 * HW), lambda c: (jnp.maximum(rc(c) * (CHUNK // TAIL) - 1, 0), 1)),
                   pl.BlockSpec((CHUNK, DH), lambda c: (rc(c), 0)),
                   pl.BlockSpec((CONV_K, 3 * HW), lambda c: (0, 0)), one, one, one,
                   st_spec, st_spec, pl.BlockSpec((1, HEADS, CHUNK, CHUNK), lambda c: (rc(c), 0, 0, 0)),
                   do_spec, do_spec, pl.BlockSpec(memory_space=pl.ANY)],
                  [pl.BlockSpec((CHUNK, 8 * HW), lambda c: (rc(c), 0)), vec, vec, one,
                   pl.BlockSpec((CHUNK, DH), lambda c: (rc(c), 0)),
                   pl.BlockSpec((CONV_K, 3 * HW), lambda c: (0, 0)), one, one, one],
                  [jax.ShapeDtypeStruct((lp, W_MAIN), BF16), vec_shape, vec_shape, one_shape,
                   jax.ShapeDtypeStruct((lp, DH), F32), jax.ShapeDtypeStruct((CONV_K, 3 * HW), F32),
                   one_shape, one_shape, one_shape],
                  [pltpu.VMEM((HEADS, DH, DH), F32), pltpu.VMEM((HEADS, DH, DH), F32),
                   pltpu.VMEM((TAIL + CHUNK, 3 * HW), F32), pltpu.VMEM((CHUNK + TAIL, 3 * HW), F32),
                   pltpu.VMEM((TAIL, 3 * HW), F32)]),
    )(proj, l0, l1, hg_nw, proj, proj, proj_ab, cw, alog, dtb, gd_nw, st_a, st_b, t_inv, do_a, do_b, dproj,
      *exch.arrays)
    return outs[:n_out], outs[n_out:]


def matmul(name, a, b, mode, m, n, k, out_dtype, tm=512, tn=1024, tk=None, a_off=(0, 0), b_off=(0, 0), exch=None,
           rows_outer=False):
    tm, tn = min(tm, m), min(tn, n)
    tk = k if tk is None else min(tk, k)
    assert m % tm == 0 and n % tn == 0 and k % tk == 0, (name, m, n, k, tm, tn, tk)
    gi, gj, gk = m // tm, n // tn, k // tk
    grid = (gi, gj, gk) if rows_outer else (gj, gi, gk)

    def at(index):
        return (lambda i, j, kk: index(j, i, kk)) if rows_outer else index

    if mode == "nn":
        a_spec = pl.BlockSpec((tm, tk), at(lambda j, i, kk: (i + a_off[0], kk + a_off[1])))
        b_spec = pl.BlockSpec((tk, tn), at(lambda j, i, kk: (kk + b_off[0], j + b_off[1])))
        dims = _NN
    elif mode == "nt":
        a_spec = pl.BlockSpec((tm, tk), at(lambda j, i, kk: (i + a_off[0], kk + a_off[1])))
        b_spec = pl.BlockSpec((tn, tk), at(lambda j, i, kk: (j + b_off[0], kk + b_off[1])))
        dims = _NT
    else:
        a_spec = pl.BlockSpec((tk, tm), at(lambda j, i, kk: (kk + a_off[0], i + a_off[1])))
        b_spec = pl.BlockSpec((tk, tn), at(lambda j, i, kk: (kk + b_off[0], j + b_off[1])))
        dims = _TN

    def body(a_ref, b_ref, o_ref, *acc):
        d = lax.dot_general(a_ref[...].astype(BF16), b_ref[...].astype(BF16), (dims, ((), ())),
                            preferred_element_type=F32)
        if gk == 1:
            o_ref[...] = d.astype(o_ref.dtype)
        else:
            acc_ref, = acc
            kk = pl.program_id(2)

            @pl.when(kk == 0)
            def _():
                acc_ref[...] = d

            @pl.when(kk > 0)
            def _():
                acc_ref[...] += d

            @pl.when(kk == gk - 1)
            def _():
                o_ref[...] = acc_ref[...].astype(o_ref.dtype)

    o_spec = pl.BlockSpec((tm, tn), at(lambda j, i, kk: (i, j)))
    o_shape = jax.ShapeDtypeStruct((m, n), out_dtype)
    scratch = [] if gk == 1 else [pltpu.VMEM((tm, tn), F32)]
    if exch is None:
        return pl.pallas_call(
            body, name=name, grid=grid, in_specs=[a_spec, b_spec], out_specs=o_spec, out_shape=o_shape,
            scratch_shapes=scratch, compiler_params=_params("parallel", "parallel", "arbitrary"),
        )(a, b)
    step = lambda: (pl.program_id(0), pl.program_id(1), pl.program_id(2))
    first = lambda: jnp.logical_and(jnp.logical_and(step()[0] == 0, step()[1] == 0), step()[2] == 0)
    last = lambda: jnp.logical_and(jnp.logical_and(step()[0] == grid[0] - 1, step()[1] == grid[1] - 1),
                                   step()[2] == gk - 1)
    outs = pl.pallas_call(
        host_exchange(body, 2, 1, exch, first, last), name=name, grid=grid,
        compiler_params=_params("arbitrary", "arbitrary", "arbitrary"),
        **_hosted(exch, [a_spec, b_spec], [o_spec], [o_shape], scratch),
    )(a, b, *exch.arrays)
    return outs[0], outs[1:]


def _swiglu(gu):
    return _silu(gu[:, :FF_BLK]) * gu[:, FF_BLK:]


def ffn_in_fused(xn2, w_gu, tm=512):
    l = xn2.shape[0]
    tn = 2 * FF_BLK

    def body(a_ref, b_ref, gu_ref, act_ref):
        gu = lax.dot_general(a_ref[...], b_ref[...], (_NN, ((), ())), preferred_element_type=F32)
        gu_ref[...] = gu
        act_ref[...] = _swiglu(gu).astype(act_ref.dtype)

    return pl.pallas_call(
        body, name="ffn_in", grid=(2 * D_FF // tn, l // tm),
        in_specs=[pl.BlockSpec((tm, D), lambda j, i: (i, 0)), pl.BlockSpec((D, tn), lambda j, i: (0, j))],
        out_specs=[pl.BlockSpec((tm, tn), lambda j, i: (i, j)), pl.BlockSpec((tm, FF_BLK), lambda j, i: (i, j))],
        out_shape=[jax.ShapeDtypeStruct((l, 2 * D_FF), F32), jax.ShapeDtypeStruct((l, D_FF), BF16)],
        compiler_params=_params("parallel", "parallel"),
    )(xn2, w_gu)


def d_act_fused(dh2, w_fo, gu, tm=512):
    l = dh2.shape[0]

    def body(a_ref, b_ref, gu_ref, o_ref):
        da = lax.dot_general(a_ref[...], b_ref[...], (_NT, ((), ())), preferred_element_type=F32)
        _, vjp = jax.vjp(_swiglu, gu_ref[...])
        dgu, = vjp(da)
        o_ref[...] = dgu.astype(o_ref.dtype)

    return pl.pallas_call(
        body, name="d_act", grid=(D_FF // FF_BLK, l // tm),
        in_specs=[pl.BlockSpec((tm, D), lambda j, i: (i, 0)), pl.BlockSpec((FF_BLK, D), lambda j, i: (j, 0)),
                  pl.BlockSpec((tm, 2 * FF_BLK), lambda j, i: (i, j))],
        out_specs=pl.BlockSpec((tm, 2 * FF_BLK), lambda j, i: (i, j)),
        out_shape=jax.ShapeDtypeStruct((l, 2 * D_FF), BF16),
        compiler_params=_params("parallel", "parallel"),
    )(dh2, w_fo, gu)


TR = 256
N_PRE = PRE // TR


def _row(width, off=0, col=0):
    return pl.BlockSpec((TR, width), lambda i: (i + off, col))


def _pre_spec():
    return pl.BlockSpec((TR, D), lambda i: (jnp.minimum(i, N_PRE - 1), 0))


def _seq_spec(width=D, col=0):
    return pl.BlockSpec((TR, width), lambda i: (jnp.maximum(i - N_PRE, 0), col))


def _vec_spec(width=D):
    return pl.BlockSpec((1, width), lambda i: (0, 0))


def norm1_fwd(pre, x, w):
    lp = PRE + x.shape[0]

    def body(pre_ref, x_ref, w_ref, o_ref):
        h = jnp.where(pl.program_id(0) < N_PRE, pre_ref[...], x_ref[...])
        o_ref[...] = _rms_norm(h, w_ref[...]).astype(o_ref.dtype)

    return pl.pallas_call(
        body, name="norm1_fwd", grid=(lp // TR,),
        in_specs=[_pre_spec(), _seq_spec(), _vec_spec()],
        out_specs=_row(D), out_shape=jax.ShapeDtypeStruct((lp, D), BF16),
        compiler_params=_params("parallel"),
    )(pre, x, w)


def norm1_bwd(pre, x, w, dxn, dxn_ab, dh1):
    l = x.shape[0]
    lp = PRE + l

    def body(pre_ref, x_ref, w_ref, dxn_ref, dxn_ab_ref, dh1_ref, dx_ref, dpre_ref, dw_ref):
        i = pl.program_id(0)
        h = jnp.where(i < N_PRE, pre_ref[...], x_ref[...])
        _, vjp = jax.vjp(_rms_norm, h, w_ref[...])
        dh, dw = vjp(dxn_ref[...] + dxn_ab_ref[...])

        @pl.when(i == 0)
        def _():
            dw_ref[...] = dw

        @pl.when(i > 0)
        def _():
            dw_ref[...] += dw

        @pl.when(i < N_PRE)
        def _():
            dpre_ref[...] = dh

        @pl.when(i >= N_PRE)
        def _():
            dx_ref[...] = dh + dh1_ref[...]

    return pl.pallas_call(
        body, name="norm1_bwd", grid=(lp // TR,),
        in_specs=[_pre_spec(), _seq_spec(), _vec_spec(), _row(D), _row(D), _seq_spec()],
        out_specs=[_seq_spec(), _pre_spec(), _vec_spec()],
        out_shape=[jax.ShapeDtypeStruct((l, D), F32), jax.ShapeDtypeStruct((PRE, D), F32),
                   jax.ShapeDtypeStruct((1, D), F32)],
        compiler_params=_params("arbitrary"),
    )(pre, x, w, dxn, dxn_ab, dh1)


def _merge(gates, ya, yb):
    return _sigmoid(gates[:, :D]) * ya + _sigmoid(gates[:, D:]) * yb


def merge_bwd(dh1, wo, proj, ya, yb, exch):
    l = ya.shape[0]
    lp = PRE + l

    def body(a_ref, b_ref, g_ref, ya_ref, yb_ref, dg_ref, dya_ref, dyb_ref):
        i = pl.program_id(0)

        @pl.when(i < N_PRE)
        def _():
            dg_ref[...] = jnp.zeros(dg_ref.shape, dg_ref.dtype)
            dya_ref[...] = jnp.zeros(dya_ref.shape, dya_ref.dtype)
            dyb_ref[...] = jnp.zeros(dyb_ref.shape, dyb_ref.dtype)

        @pl.when(i >= N_PRE)
        def _():
            dm = lax.dot_general(a_ref[...], b_ref[...], (_NT, ((), ())), preferred_element_type=F32)
            _, vjp = jax.vjp(_merge, g_ref[...], ya_ref[...], yb_ref[...])
            dg, dya, dyb = vjp(dm)
            dg_ref[...] = dg.astype(dg_ref.dtype)
            dya_ref[...] = dya.astype(dya_ref.dtype)
            dyb_ref[...] = dyb.astype(dyb_ref.dtype)

    last = lp // TR - 1
    outs = pl.pallas_call(
        host_exchange(body, 5, 3, exch, lambda: pl.program_id(0) == 0, lambda: pl.program_id(0) == last),
        name="merge_bwd", grid=(lp // TR,), compiler_params=_params("arbitrary"),
        **_hosted(exch,
                  [_seq_spec(), pl.BlockSpec((D, D), lambda i: (0, 0)),
                   pl.BlockSpec((TR, 2 * D), lambda i: (jnp.maximum(i, N_PRE), 2)), _seq_spec(), _seq_spec()],
                  [_row(2 * D, 0, 2), _row(D), _row(D)],
                  [jax.ShapeDtypeStruct((lp, W_MAIN), BF16), jax.ShapeDtypeStruct((lp, D), BF16),
                   jax.ShapeDtypeStruct((lp, D), BF16)], []),
    )(dh1, wo, proj, ya, yb, *exch.arrays)
    return outs[:3], outs[3:]


def merge_mix_out(proj, ya, yb, wo, x, w):
    l = x.shape[0]

    def body(g_ref, ya_ref, yb_ref, b_ref, x_ref, w_ref, m_ref, h_ref, o_ref):
        merged = _merge(g_ref[...], ya_ref[...], yb_ref[...]).astype(BF16)
        m_ref[...] = merged
        h = x_ref[...] + lax.dot_general(merged, b_ref[...], (_NN, ((), ())), preferred_element_type=F32)
        h_ref[...] = h
        o_ref[...] = _rms_norm(h, w_ref[...]).astype(o_ref.dtype)

    return pl.pallas_call(
        body, name="mix_out", grid=(l // TR,),
        in_specs=[_row(2 * D, N_PRE, 2), _row(D), _row(D), pl.BlockSpec((D, D), lambda i: (0, 0)), _row(D),
                  _vec_spec()],
        out_specs=[_row(D), _row(D), _row(D)],
        out_shape=[jax.ShapeDtypeStruct((l, D), BF16), jax.ShapeDtypeStruct((l, D), F32),
                   jax.ShapeDtypeStruct((l, D), BF16)],
        compiler_params=_params("parallel"),
    )(proj, ya, yb, wo, x, w)


def norm2_bwd(h1, w, dxn2, dh2):
    l = h1.shape[0]

    def body(h_ref, w_ref, dxn_ref, dh2_ref, dh1_ref, dh1b_ref, dw_ref):
        i = pl.program_id(0)
        _, vjp = jax.vjp(_rms_norm, h_ref[...], w_ref[...])
        dh, dw = vjp(dxn_ref[...])
        dh1 = dh + dh2_ref[...]
        dh1_ref[...] = dh1
        dh1b_ref[...] = dh1.astype(BF16)

        @pl.when(i == 0)
        def _():
            dw_ref[...] = dw

        @pl.when(i > 0)
        def _():
            dw_ref[...] += dw

    return pl.pallas_call(
        body, name="norm2_bwd", grid=(l // TR,),
        in_specs=[_row(D), _vec_spec(), _row(D), _row(D)],
        out_specs=[_row(D), _row(D), _vec_spec()],
        out_shape=[jax.ShapeDtypeStruct((l, D), F32), jax.ShapeDtypeStruct((l, D), BF16),
                   jax.ShapeDtypeStruct((1, D), F32)],
        compiler_params=_params("arbitrary"),
    )(h1, w, dxn2, dh2)


def _loss_rows(h1, f, w, tgt):
    y = _rms_norm(h1 + f, w)
    err = (y - tgt) * (y - tgt)
    return 0.5 * jnp.sum(jnp.mean(err, axis=-1, keepdims=True), axis=0, keepdims=True)


def loss_head(h1, f, w, tgt):
    l = h1.shape[0]

    def body(h_ref, f_ref, w_ref, t_ref, loss_ref, dh_ref, dhb_ref, dw_ref):
        i = pl.program_id(0)
        loss, vjp = jax.vjp(_loss_rows, h_ref[...], f_ref[...], w_ref[...], t_ref[...])
        dh, _, dw, _ = vjp(jnp.ones((1, 1), F32))
        dh_ref[...] = dh
        dhb_ref[...] = dh.astype(BF16)
        loss = jnp.broadcast_to(loss, (1, LANES))

        @pl.when(i == 0)
        def _():
            dw_ref[...] = dw
            loss_ref[...] = loss

        @pl.when(i > 0)
        def _():
            dw_ref[...] += dw
            loss_ref[...] += loss

    return pl.pallas_call(
        body, name="loss_head", grid=(l // TR,),
        in_specs=[_row(D), _row(D), _vec_spec(), _row(D)],
        out_specs=[_vec_spec(LANES), _row(D), _row(D), _vec_spec()],
        out_shape=[jax.ShapeDtypeStruct((1, LANES), F32), jax.ShapeDtypeStruct((l, D), F32),
                   jax.ShapeDtypeStruct((l, D), BF16), jax.ShapeDtypeStruct((1, D), F32)],
        compiler_params=_params("arbitrary"),
    )(h1, f, w, tgt)


def _adamw(w, g, m, v):
    m = ADAM_B1 * m + (1.0 - ADAM_B1) * g
    v = ADAM_B2 * v + (1.0 - ADAM_B2) * (g * g)
    m_hat = m / (1.0 - ADAM_B1 ** ADAM_STEP)
    v_hat = v / (1.0 - ADAM_B2 ** ADAM_STEP)
    delta = -ADAM_LR * (m_hat / (jnp.sqrt(v_hat) + ADAM_EPS) + ADAM_WD * w)
    return delta, m, v


def adamw_shard(name, w, m, v, sums, recv, tile):
    r, c = w.shape
    tr, tc = tile
    assert r % tr == 0 and c % tc == 0

    def body(chip_ref, w_ref, m_ref, v_ref, own_ref, r0_ref, r1_ref, r2_ref, g_ref, d_ref, nm_ref, nv_ref):
        g = own_ref[...].astype(F32) + r0_ref[...].astype(F32)
        g = g + r1_ref[...].astype(F32)
        g = g + r2_ref[...].astype(F32)
        d, nm, nv = _adamw(w_ref[...], g, m_ref[...], v_ref[...])
        g_ref[...] = g
        d_ref[...] = d
        nm_ref[...] = nm
        nv_ref[...] = nv

    blk = pl.BlockSpec((tr, tc), lambda i, j, chip: (i, j))
    part = lambda s: pl.BlockSpec((None, tr, tc), lambda i, j, chip: (s, i, j))
    own = pl.BlockSpec((None, tr, tc), lambda i, j, chip: (chip[0], i, j))
    shape = jax.ShapeDtypeStruct((r, c), F32)
    my_chip = (2 * lax.axis_index("x") + lax.axis_index("y")).astype(jnp.int32).reshape(1)
    return pl.pallas_call(
        body, name=name, out_shape=[shape, shape, shape, shape],
        grid_spec=pltpu.PrefetchScalarGridSpec(num_scalar_prefetch=1, grid=(r // tr, c // tc),
                                               in_specs=[blk, blk, blk, own, part(0), part(1), part(2)],
                                               out_specs=[blk, blk, blk, blk]),
        compiler_params=_params("parallel", "parallel"),
    )(my_chip, w, m, v, sums, recv, recv, recv)


def adamw_small(w, g, m, v):
    def body(w_ref, g_ref, m_ref, v_ref, d_ref, nm_ref, nv_ref):
        d, nm, nv = _adamw(w_ref[...], g_ref[...], m_ref[...], v_ref[...])
        d_ref[...] = d
        nm_ref[...] = nm
        nv_ref[...] = nv

    shape = jax.ShapeDtypeStruct(w.shape, F32)
    return pl.pallas_call(body, name="adamw_small", out_shape=[shape, shape, shape])(w, g, m, v)


def pair_add(name, a, b, tile):
    q, r, c = b.shape
    tr, tc = tile
    assert r % tr == 0 and c % tc == 0

    def body(core_ref, a_ref, b_ref, o_ref):
        o_ref[...] = (a_ref[...].astype(F32) + b_ref[...].astype(F32)).astype(o_ref.dtype)

    blk = pl.BlockSpec((None, tr, tc), lambda s, i, j, core: (s, i, j))
    mine = pl.BlockSpec((None, None, tr, tc), lambda s, i, j, core: (s, core[0], i, j))
    return pl.pallas_call(
        body, name=name, out_shape=jax.ShapeDtypeStruct((q, r, c), BF16),
        grid_spec=pltpu.PrefetchScalarGridSpec(num_scalar_prefetch=1, grid=(q, r // tr, c // tc),
                                               in_specs=[mine, blk], out_specs=blk),
        compiler_params=_params("parallel", "parallel", "parallel"),
    )(lax.axis_index("c").astype(jnp.int32).reshape(1), a, b)


def _w_in_pieces():
    c = W_IN // N_DEV
    ranges = ((0, 8 * HW, 0, 0), (8 * HW, 8 * HW + 2 * HEADS, 1, 8 * HW), (8 * HW + 2 * HEADS, W_IN, 0, 2 * HEADS))
    out = []
    for d in range(N_DEV):
        for lo, hi, target, shift in ranges:
            s, e = max(lo, c * d), min(hi, c * (d + 1))
            if s < e:
                out.append((d, s - c * d, e - s, target, s - shift))
    return out


def _ffn_in_pieces():
    c = 2 * D_FF // N_DEV
    out = []
    for d in range(N_DEV):
        s = c * d
        while s < c * (d + 1):
            e = min((s // FF_BLK + 1) * FF_BLK, c * (d + 1))
            half, blk = divmod(s // FF_BLK, D_FF // FF_BLK)
            out.append((d, s - c * d, e - s, 0, (2 * blk + half) * FF_BLK + s % FF_BLK))
            s = e
    return out


def _plain_pieces(c):
    return [(d, 0, c, 0, c * d) for d in range(N_DEV)]


def shards_to_cols(name, g, widths, pieces, tr):
    _, r, c = g.shape
    covered = [sum(p[2] for p in pieces if p[3] == t) for t in range(len(widths))]

    def body(g_ref, *outs):
        for t, o in enumerate(outs):
            if covered[t] < widths[t]:
                o[...] = jnp.zeros(o.shape, o.dtype)
        for d, s, w, t, ts in pieces:
            outs[t][:, ts:ts + w] = g_ref[d, :, s:s + w]

    return pl.pallas_call(
        body, name=name, grid=(r // tr,),
        in_specs=[pl.BlockSpec((N_DEV, tr, c), lambda i: (0, i, 0))],
        out_specs=[pl.BlockSpec((tr, w), lambda i: (i, 0)) for w in widths],
        out_shape=[jax.ShapeDtypeStruct((r, w), g.dtype) for w in widths],
        compiler_params=_params("parallel"),
    )(g)


def cols_to_shards(name, srcs, c, pieces, tr):
    r = srcs[0].shape[0]

    def body(*refs):
        o = refs[-1]
        for d, s, w, t, ts in pieces:
            o[d, :, s:s + w] = refs[t][:, ts:ts + w]

    return pl.pallas_call(
        body, name=name, grid=(r // tr,),
        in_specs=[pl.BlockSpec((tr, t.shape[1]), lambda i: (i, 0)) for t in srcs],
        out_specs=pl.BlockSpec((N_DEV, tr, c), lambda i: (0, i, 0)),
        out_shape=jax.ShapeDtypeStruct((N_DEV, r, c), srcs[0].dtype),
        compiler_params=_params("parallel"),
    )(*srcs)


def rows_to_shards(name, srcs, c, pieces, tc):
    r = srcs[0].shape[1]

    def body(*refs):
        o = refs[-1]
        for d, s, w, t, ts in pieces:
            o[d, s:s + w, :] = refs[t][ts:ts + w, :]

    return pl.pallas_call(
        body, name=name, grid=(r // tc,),
        in_specs=[pl.BlockSpec((t.shape[0], tc), lambda i: (0, i)) for t in srcs],
        out_specs=pl.BlockSpec((N_DEV, c, tc), lambda i: (0, 0, i)),
        out_shape=jax.ShapeDtypeStruct((N_DEV, c, r), srcs[0].dtype),
        compiler_params=_params("parallel"),
    )(*srcs)


def _place():
    return lax.axis_index("x"), lax.axis_index("y"), lax.axis_index("c")


def _dev(px, py, pc):
    return 4 * px + 2 * py + pc


class Exchange:
    def __init__(self, arrays, out_shapes, sems, start, finish):
        self.arrays, self.out_shapes, self.sems, self.start, self.finish = arrays, out_shapes, sems, start, finish


def gather_exchange(arrays):
    n = len(arrays)

    def plan(ins, outs, sems):
        send_sems, recv_sems, local_sems = sems
        x, y, c = _place()
        me, sibling = (x, y, c), (x, y, 1 - c)
        chips = [(1 - x, y), (x, 1 - y), (1 - x, 1 - y)]

        def copy(a, k, block, to, src=None):
            dst = outs[a].at[_dev(*block)]
            return pltpu.make_async_remote_copy(
                src_ref=dst if src is None else src, dst_ref=dst, send_sem=send_sems.at[a, k],
                recv_sem=recv_sems.at[a, k], device_id=to, device_id_type=MESH)

        mine = [pltpu.make_async_copy(ins[a], outs[a].at[_dev(*me)], local_sems.at[a]) for a in range(n)]
        first = []
        for a in range(n):
            first.append(copy(a, 0, me, sibling, src=ins[a]))
            first += [copy(a, 1 + j, me, (*chip, c), src=ins[a]) for j, chip in enumerate(chips)]
        return me, sibling, chips, c, copy, mine, first

    def start(ins, outs, sems):
        *_, mine, first = plan(ins, outs, sems)
        for cp in mine + first:
            cp.start()

    def finish(ins, outs, sems):
        me, sibling, chips, c, copy, mine, first = plan(ins, outs, sems)
        passed = []
        for j, chip in enumerate(chips):
            for a in range(n):
                copy(a, 1 + j, (*chip, c), me).wait_recv()
                fwd = copy(a, 4 + j, (*chip, c), sibling)
                fwd.start()
                passed.append(fwd)
        for a in range(n):
            copy(a, 0, sibling, me).wait_recv()
        for j, chip in enumerate(chips):
            for a in range(n):
                copy(a, 4 + j, (*chip, 1 - c), me).wait_recv()
        for cp in first + passed:
            cp.wait_send()
        for cp in mine:
            cp.wait()

    return Exchange(arrays, [jax.ShapeDtypeStruct((N_DEV,) + t.shape, t.dtype) for t in arrays],
                    [pltpu.SemaphoreType.DMA((n, 7)), pltpu.SemaphoreType.DMA((n, 7)), pltpu.SemaphoreType.DMA((n,))],
                    start, finish)


def core_exchange(arrays):
    n = len(arrays)

    def copies(ins, outs, sems):
        send_sems, recv_sems = sems
        x, y, c = _place()
        return [pltpu.make_async_remote_copy(
            src_ref=ins[a].at[q, 1 - c], dst_ref=outs[a].at[q], send_sem=send_sems.at[a, q],
            recv_sem=recv_sems.at[a, q], device_id=(x, y, 1 - c), device_id_type=MESH)
            for a in range(n) for q in range(4)]

    def start(ins, outs, sems):
        for cp in copies(ins, outs, sems):
            cp.start()

    def finish(ins, outs, sems):
        for cp in copies(ins, outs, sems):
            cp.wait()

    return Exchange(arrays, [jax.ShapeDtypeStruct((4,) + t.shape[2:], t.dtype) for t in arrays],
                    [pltpu.SemaphoreType.DMA((n, 4)), pltpu.SemaphoreType.DMA((n, 4))], start, finish)


def chips_exchange(arrays):
    n = len(arrays)

    def copies(ins, outs, sems):
        send_sems, recv_sems = sems
        x, y, c = _place()
        chips = [(1 - x, y), (x, 1 - y), (1 - x, 1 - y)]
        return [pltpu.make_async_remote_copy(
            src_ref=ins[a].at[2 * qx + qy], dst_ref=outs[a].at[j], send_sem=send_sems.at[a, j],
            recv_sem=recv_sems.at[a, j], device_id=(qx, qy, c), device_id_type=MESH)
            for a in range(n) for j, (qx, qy) in enumerate(chips)]

    def start(ins, outs, sems):
        for cp in copies(ins, outs, sems):
            cp.start()

    def finish(ins, outs, sems):
        for cp in copies(ins, outs, sems):
            cp.wait()

    return Exchange(arrays, [jax.ShapeDtypeStruct((3,) + t.shape[1:], t.dtype) for t in arrays],
                    [pltpu.SemaphoreType.DMA((n, 3)), pltpu.SemaphoreType.DMA((n, 3))], start, finish)


def run_exchange(name, exch):
    n_in, n_out = len(exch.arrays), len(exch.out_shapes)
    any_spec = pl.BlockSpec(memory_space=pl.ANY)

    def body(*refs):
        ins, outs, sems = refs[:n_in], refs[n_in:n_in + n_out], refs[n_in + n_out:]
        exch.start(ins, outs, sems)
        exch.finish(ins, outs, sems)

    return pl.pallas_call(
        body, name=name, in_specs=[any_spec] * n_in, out_specs=[any_spec] * n_out, out_shape=exch.out_shapes,
        scratch_shapes=exch.sems,
    )(*exch.arrays)


def host_exchange(body, n_in, n_out, exch, first, last):
    ne_in, ne_out, ne_sem = len(exch.arrays), len(exch.out_shapes), len(exch.sems)

    def wrapped(*refs):
        ins, refs = refs[:n_in], refs[n_in:]
        e_ins, refs = refs[:ne_in], refs[ne_in:]
        outs, refs = refs[:n_out], refs[n_out:]
        e_outs, refs = refs[:ne_out], refs[ne_out:]
        scratch, e_sems = refs[:len(refs) - ne_sem], refs[len(refs) - ne_sem:]

        @pl.when(first())
        def _():
            exch.start(e_ins, e_outs, e_sems)

        body(*ins, *outs, *scratch)

        @pl.when(last())
        def _():
            exch.finish(e_ins, e_outs, e_sems)

    return wrapped


def _hosted(exch, in_specs, out_specs, out_shape, scratch_shapes):
    any_spec = pl.BlockSpec(memory_space=pl.ANY)
    return dict(in_specs=list(in_specs) + [any_spec] * len(exch.arrays),
                out_specs=list(out_specs) + [any_spec] * len(exch.out_shapes),
                out_shape=list(out_shape) + list(exch.out_shapes),
                scratch_shapes=list(scratch_shapes) + list(exch.sems))


def allreduce_small(buf):
    r = buf.shape[0]
    vmem = pl.BlockSpec(memory_space=pltpu.VMEM)

    def body(buf_ref, out_ref, all_ref, send_sems, recv_sems):
        x, y, c = _place()
        me = _dev(x, y, c)
        copies = []
        for k in range(1, N_DEV):
            peer = (x ^ (k >> 2), y ^ ((k >> 1) & 1), c ^ (k & 1))
            copies.append(pltpu.make_async_remote_copy(
                src_ref=buf_ref, dst_ref=all_ref.at[me], send_sem=send_sems.at[k - 1], recv_sem=recv_sems.at[k - 1],
                device_id=peer, device_id_type=MESH))
        for cp in copies:
            cp.start()
        all_ref[me] = buf_ref[...]
        for cp in copies:
            cp.wait()
        total = all_ref[0]
        for d in range(1, N_DEV):
            total = total + all_ref[d]
        out_ref[...] = total

    return pl.pallas_call(
        body, name="allreduce_small", in_specs=[vmem], out_specs=vmem,
        out_shape=jax.ShapeDtypeStruct((r, LANES), F32),
        scratch_shapes=[pltpu.VMEM((N_DEV, r, LANES), F32), pltpu.SemaphoreType.DMA((N_DEV - 1,)),
                        pltpu.SemaphoreType.DMA((N_DEV - 1,))],
    )(buf)


def _cols_from_shards(g):
    return jnp.transpose(g, (1, 0, 2)).reshape(g.shape[1], N_DEV * g.shape[2])


def _pack(t):
    flat = t.reshape(-1)
    return jnp.pad(flat, (0, -flat.size % (8 * LANES))).reshape(-1, LANES)


def _pad_lanes(t):
    t = t.reshape(1, -1)
    return jnp.pad(t, ((0, 0), (0, LANES - t.shape[1])))


def kernel(x, meta_tokens, lb_logits, mix_norm_w, w_in, hg_norm_w, gd_conv_w, gd_a_log, gd_dt_bias, gd_norm_w, w_branch_a, w_branch_b, w_out, ffn_norm_w, w_ffn_in, w_ffn_out, final_norm_w, loss_target, m_meta_tokens, m_lb_logits, m_mix_norm_w, m_w_in, m_hg_norm_w, m_gd_conv_w, m_gd_a_log, m_gd_dt_bias, m_gd_norm_w, m_w_branch_a, m_w_branch_b, m_w_out, m_ffn_norm_w, m_w_ffn_in, m_w_ffn_out, m_final_norm_w, v_meta_tokens, v_lb_logits, v_mix_norm_w, v_w_in, v_hg_norm_w, v_gd_conv_w, v_gd_a_log, v_gd_dt_bias, v_gd_norm_w, v_w_branch_a, v_w_branch_b, v_w_out, v_ffn_norm_w, v_w_ffn_in, v_w_ffn_out, v_final_norm_w):
    xs = x[0]
    tgt = loss_target[0]
    l = xs.shape[0]
    lp = PRE + l
    me = _dev(*_place())

    big = [w_in[0], w_branch_a[0], w_branch_b[0], w_out[0], w_ffn_in[0], w_ffn_out[0]]
    big16 = [t.astype(BF16) for t in big]
    g_in, g_meta, g_conv = run_exchange("gather_first", gather_exchange([big16[0], meta_tokens, gd_conv_w[0]]))
    w_main, w_ab = shards_to_cols("w_in_cols", g_in, [W_MAIN, LANES], _w_in_pieces(), 256)
    meta_full = _cols_from_shards(g_meta)
    cw = _cols_from_shards(g_conv)
    pre = jnp.concatenate([jnp.zeros((PRE - N_META, D), F32), meta_full], axis=0)
    l0, l1 = lb_logits[0:1], lb_logits[1:2]
    alog, dtb = _pad_lanes(gd_a_log), _pad_lanes(gd_dt_bias)

    xn = norm1_fwd(pre, xs, mix_norm_w)
    proj, (g_ba, g_bb, g_out, g_ffi, g_ffo) = matmul("proj_main", xn, w_main, "nn", lp, W_MAIN, D, F32, tm=lp // 8,
                                                      exch=gather_exchange(big16[1:]))
    wa, = shards_to_cols("w_branch_a_cols", g_ba, [D], _plain_pieces(D // N_DEV), 256)
    wb, = shards_to_cols("w_branch_b_cols", g_bb, [D], _plain_pieces(D // N_DEV), 256)
    wo = g_out.reshape(D, D)
    w_gu, = shards_to_cols("w_ffn_in_cols", g_ffi, [2 * D_FF], _ffn_in_pieces(), 256)
    w_fo = g_ffo.reshape(D_FF, D)
    proj_ab = matmul("proj_ab", xn, w_ab, "nn", lp, LANES, D, F32)
    o_a, st_a, o_b, st_b, t_inv = mixer_fwd(proj, proj_ab, l0, l1, hg_norm_w, cw, alog, dtb, gd_norm_w)
    ya = matmul("branch_a", o_a, wa, "nn", l, D, HW, F32, a_off=(PRE // 512, 0))
    yb = matmul("branch_b", o_b, wb, "nn", l, D, HW, F32, a_off=(PRE // 512, 0))
    merged, h1, xn2 = merge_mix_out(proj, ya, yb, wo, xs, ffn_norm_w)
    gu, act = ffn_in_fused(xn2, w_gu, tm=min(1024, l))
    f = matmul("ffn_out", act, w_fo, "nn", l, D, D_FF, F32, tn=512)
    loss_part, dh2, dh2_b, d_final_w = loss_head(h1, f, final_norm_w.reshape(1, D), tgt)

    tiles = {"w_in": (W_IN // N_DEV, 256), "w_branch_a": (256, D // N_DEV), "w_branch_b": (256, D // N_DEV),
             "w_out": (64, D), "w_ffn_in": (256, 2 * D_FF // N_DEV), "w_ffn_out": (176, D)}

    def chip_sums(tag, parts, host=None):
        blocks = [p.reshape((4, 2) + p.shape[1:]) for p in parts.values()]
        exch = core_exchange(blocks)
        hosted, from_core = (None, run_exchange("exchange_core_" + tag, exch)) if host is None else host(exch)
        return hosted, {nm: pair_add("pair_add_" + nm, p, r, tiles[nm]) for (nm, p, r) in zip(parts, blocks, from_core)}

    dgu = d_act_fused(dh2_b, w_fo, gu)
    dw_fo = matmul("dw_ffn_out", act, dh2_b, "tn", D_FF, D, l, BF16, tm=512, tn=512)
    dxn2 = matmul("d_xn2", dgu, w_gu, "nt", l, D, 2 * D_FF, F32, tn=256, rows_outer=True)
    dw_gu = matmul("dw_ffn_in", xn2, dgu, "tn", D, 2 * D_FF, l, BF16, tm=512, tn=512)
    dh1, dh1_b, d_ffn_norm_w = norm2_bwd(h1, ffn_norm_w, dxn2, dh2)
    (dproj, dya, dyb), sums = chip_sums("ffn", {
        "w_ffn_in": cols_to_shards("dw_ffn_in_shards", [dw_gu], 2 * D_FF // N_DEV, _ffn_in_pieces(), 256),
        "w_ffn_out": dw_fo.reshape(N_DEV, D_FF // N_DEV, D)},
        host=lambda exch: merge_bwd(dh1_b, wo, proj, ya, yb, exch))
    dw_o = matmul("dw_out", merged, dh1_b, "tn", D, D, l, BF16, tm=512, tn=512)
    do_a = matmul("d_o_a", dya, wa, "nt", lp, HW, D, F32)
    do_b = matmul("d_o_b", dyb, wb, "nt", lp, HW, D, F32)
    dw_a = matmul("dw_branch_a", o_a, dya, "tn", HW, D, lp, BF16, tm=512, tn=512)
    dw_b = matmul("dw_branch_b", o_b, dyb, "tn", HW, D, lp, BF16, tm=512, tn=512)
    sums.update(chip_sums("mix", {
        "w_branch_a": cols_to_shards("dw_branch_a_shards", [dw_a], D // N_DEV, _plain_pieces(D // N_DEV), 256),
        "w_branch_b": cols_to_shards("dw_branch_b_shards", [dw_b], D // N_DEV, _plain_pieces(D // N_DEV), 256),
        "w_out": dw_o.reshape(N_DEV, D // N_DEV, D)})[1])
    ffn_names, mix_names = ["w_ffn_in", "w_ffn_out"], ["w_branch_a", "w_branch_b", "w_out"]
    (dproj, dl0, dl1, d_hg_nw, dproj_ab, d_conv, d_alog, d_dtb, d_gd_nw), from_chips_early = mixer_bwd(
        proj, proj_ab, l0, l1, hg_norm_w, cw, alog, dtb, gd_norm_w, st_a, st_b, t_inv, do_a, do_b, dproj,
        chips_exchange([sums[nm] for nm in ffn_names + mix_names]))
    dwt_main = matmul("dw_in_main", dproj, xn, "tn", W_MAIN, D, lp, BF16, tm=512, tn=512)
    dwt_ab = matmul("dw_in_ab", dproj_ab, xn, "tn", LANES, D, lp, BF16, tn=512)
    dxn_ab, sums_in = chip_sums("w_in", {
        "w_in": rows_to_shards("dw_in_shards", [dwt_main, dwt_ab], W_IN // N_DEV, _w_in_pieces(), 256)},
        host=lambda exch: matmul("d_xn_ab", dproj_ab, w_ab, "nt", lp, D, LANES, F32, exch=exch))
    sums.update(sums_in)
    from_chips = dict(zip(ffn_names + mix_names, from_chips_early))
    dxn, (from_chips["w_in"],) = matmul("d_xn", dproj, w_main, "nt", lp, D, W_MAIN, F32, tn=256, rows_outer=True,
                                        exch=chips_exchange([sums["w_in"]]))
    grad_x, dpre, d_mix_norm_w = norm1_bwd(pre, xs, mix_norm_w, dxn, dxn_ab, dh1)
    names = ["w_in", "w_branch_a", "w_branch_b", "w_out", "w_ffn_in", "w_ffn_out"]
    moms = [(m_w_in, v_w_in), (m_w_branch_a, v_w_branch_a), (m_w_branch_b, v_w_branch_b), (m_w_out, v_w_out),
            (m_w_ffn_in, v_w_ffn_in), (m_w_ffn_out, v_w_ffn_out)]
    upd = {}
    for nm, w, (m, v) in zip(names, big, moms):
        flip = (lambda t: t.T) if nm == "w_in" else (lambda t: t)
        res = adamw_shard("adamw_" + nm, flip(w), flip(m[0]), flip(v[0]), sums[nm], from_chips[nm], tiles[nm])
        upd[nm] = tuple(flip(t_)[None] for t_ in res)

    d_lb = jnp.concatenate([dl0, dl1], axis=0)
    rep = [_pack(t) for t in (d_lb, d_mix_norm_w, d_hg_nw, d_alog, d_dtb, d_gd_nw, d_ffn_norm_w, d_final_w)]
    n_rep = sum(t.shape[0] for t in rep)
    n_meta = N_META * D // LANES
    tot = allreduce_small(jnp.concatenate(rep + [_pack(loss_part), _pack(dpre[PRE - N_META:]), _pack(d_conv)], axis=0))
    loss = tot[n_rep, 0]
    g_meta_full = tot[n_rep + 8:n_rep + 8 + n_meta].reshape(N_META, D)
    g_conv_full = tot[n_rep + 8 + n_meta:].reshape(CONV_K, 3 * HW)
    g_meta_mine = lax.dynamic_slice_in_dim(g_meta_full, me * (D // N_DEV), D // N_DEV, axis=1)
    g_conv_mine = lax.dynamic_slice_in_dim(g_conv_full, me * (3 * DH), 3 * DH, axis=1)

    small = [("lb_logits", lb_logits, m_lb_logits, v_lb_logits), ("mix_norm_w", mix_norm_w, m_mix_norm_w, v_mix_norm_w),
             ("hg_norm_w", hg_norm_w, m_hg_norm_w, v_hg_norm_w), ("gd_a_log", gd_a_log, m_gd_a_log, v_gd_a_log),
             ("gd_dt_bias", gd_dt_bias, m_gd_dt_bias, v_gd_dt_bias), ("gd_norm_w", gd_norm_w, m_gd_norm_w, v_gd_norm_w),
             ("ffn_norm_w", ffn_norm_w, m_ffn_norm_w, v_ffn_norm_w),
             ("final_norm_w", final_norm_w, m_final_norm_w, v_final_norm_w),
             ("meta_tokens", meta_tokens, m_meta_tokens, v_meta_tokens), ("gd_conv_w", gd_conv_w, m_gd_conv_w, v_gd_conv_w)]

    sizes = [_pack(w).shape[0] for _, w, _, _ in small]
    g_small = jnp.concatenate([tot[:n_rep], _pack(g_meta_mine), _pack(g_conv_mine)], axis=0)
    assert g_small.shape[0] == sum(sizes)
    w_small = jnp.concatenate([_pack(w) for _, w, _, _ in small], axis=0)
    m_small = jnp.concatenate([_pack(m) for _, _, m, _ in small], axis=0)
    v_small = jnp.concatenate([_pack(v) for _, _, _, v in small], axis=0)
    d_small, nm_small, nv_small = adamw_small(w_small, g_small, m_small, v_small)
    row = 0
    for (nm, w, _, _), sz in zip(small, sizes):
        def unpack(t):
            return t[row:row + sz].reshape(-1)[:w.size].reshape(w.shape)
        upd[nm] = (unpack(g_small), unpack(d_small), unpack(nm_small), unpack(nv_small))
        row += sz

    order = ["meta_tokens", "lb_logits", "mix_norm_w", "w_in", "hg_norm_w", "gd_conv_w", "gd_a_log", "gd_dt_bias",
             "gd_norm_w", "w_branch_a", "w_branch_b", "w_out", "ffn_norm_w", "w_ffn_in", "w_ffn_out", "final_norm_w"]
    outs = [loss, grad_x[None]]
    for j in range(4):
        outs += [upd[nm][j] for nm in order]
    return tuple(outs)
```

```python
import functools

import jax
import jax.numpy as jnp
from jax import lax
from jax.experimental import pallas as pl
from jax.experimental.pallas import tpu as pltpu

F32 = jnp.float32
BF16 = jnp.bfloat16
MESH = pl.DeviceIdType.MESH

EPS = 1e-6
D = 2048
N_META = 16
CHUNK = 64
SUB = 16
HEADS = 8
DH = 128
HW = HEADS * DH
CONV_K = 4
TAIL = 8
D_FF = 5632
PRE = 512
N_SKIP = PRE // CHUNK - 1
N_DEV = 8
LANES = 128
FF_BLK = 512
W_IN = 4 * HW + 4 * HW + 2 * HEADS + 2 * D
W_MAIN = 4 * HW + 4 * HW + 2 * D

ADAM_LR = 0.001
ADAM_B1 = 0.9
ADAM_B2 = 0.999
ADAM_EPS = 1e-08
ADAM_WD = 0.01
ADAM_STEP = 10

VMEM_LIMIT = 52 * 1024 * 1024

_NN = ((1,), (0,))
_NT = ((1,), (1,))
_TN = ((0,), (0,))


def _params(*sem):
    return pltpu.CompilerParams(dimension_semantics=sem, vmem_limit_bytes=VMEM_LIMIT)


BF16_ONCE, SPLIT, EXACT_LHS = 0, 1, 2


def _dot_bf16(a, b, dims):
    if a.ndim == 3:
        dn = (((dims[0][0] + 1,), (dims[1][0] + 1,)), ((0,), (0,)))
    else:
        dn = (dims, ((), ()))
    return lax.dot_general(a, b, dn, preferred_element_type=F32)


def _split(t):
    hi = t.astype(BF16)
    return hi, (t - hi.astype(F32)).astype(BF16)


def _raw_dot(a, b, dims, mode):
    if mode == BF16_ONCE:
        return _dot_bf16(a.astype(BF16), b.astype(BF16), dims)
    if mode == SPLIT:
        a_hi, a_lo = _split(a)
        b_hi, b_lo = _split(b)
        return _dot_bf16(a_hi, b_hi, dims) + (_dot_bf16(a_hi, b_lo, dims) + _dot_bf16(a_lo, b_hi, dims))
    a = a.astype(BF16)
    b_hi = b.astype(BF16)
    rest = b - b_hi.astype(F32)
    b_mid = rest.astype(BF16)
    b_lo = (rest - b_mid.astype(F32)).astype(BF16)
    return _dot_bf16(a, b_hi, dims) + (_dot_bf16(a, b_mid, dims) + _dot_bf16(a, b_lo, dims))


@functools.partial(jax.custom_vjp, nondiff_argnums=(2,))
def mm_nn(a, b, mode=BF16_ONCE):
    return _raw_dot(a, b, _NN, mode)


@functools.partial(jax.custom_vjp, nondiff_argnums=(2,))
def mm_nt(a, b, mode=BF16_ONCE):
    return _raw_dot(a, b, _NT, mode)


@functools.partial(jax.custom_vjp, nondiff_argnums=(2,))
def mm_tn(a, b, mode=BF16_ONCE):
    return _raw_dot(a, b, _TN, mode)


def _general(mode):
    return SPLIT if mode == EXACT_LHS else mode


mm_nn.defvjp(lambda a, b, p: (_raw_dot(a, b, _NN, p), (a, b)),
             lambda p, res, g: (mm_nt(g, res[1], _general(p)), mm_tn(res[0], g, p)))
mm_nt.defvjp(lambda a, b, p: (_raw_dot(a, b, _NT, p), (a, b)),
             lambda p, res, g: (mm_nn(g, res[1], p), mm_tn(g, res[0], p)))
mm_tn.defvjp(lambda a, b, p: (_raw_dot(a, b, _TN, p), (a, b)),
             lambda p, res, g: (mm_nt(res[1], g, _general(p)), mm_nn(res[0], g, _general(p))))


def _sigmoid(x):
    return jax.nn.sigmoid(x)


def _silu(x):
    return x * _sigmoid(x)


def _softplus(x):
    return jnp.maximum(x, 0.0) + jnp.log(1.0 + jnp.exp(-jnp.abs(x)))


def _l2n(t):
    return t * lax.rsqrt(jnp.sum(t * t, axis=-1, keepdims=True) + EPS)


def _rms_norm(x, w):
    return x * lax.rsqrt(jnp.mean(x * x, axis=-1, keepdims=True) + EPS) * w


def _gated_norm(o, gate, nw):
    o = o * lax.rsqrt(jnp.mean(o * o, axis=-1, keepdims=True) + EPS) * nw
    return o * _silu(gate)


def _heads(ref, piece):
    return jnp.stack([ref[:, piece * HW + DH * j:piece * HW + DH * (j + 1)] for j in range(HEADS)])


def _put_heads(ref, piece, value, accumulate=False):
    for j in range(HEADS):
        cols = slice(piece * HW + DH * j, piece * HW + DH * (j + 1))
        if accumulate:
            ref[:, cols] += value[j]
        else:
            ref[:, cols] = value[j].astype(ref.dtype)


def _diag_scores(q_i, k_i, f_i):
    nb, s, d = q_i.shape
    row = lax.broadcasted_iota(jnp.int32, (nb, s, d), 1)
    q4, f4 = q_i[:, :, None, :], f_i[:, :, None, :]
    kk = k_i
    rows_out = []
    for t in range(s):
        if t > 0:
            kk = jnp.where(row < t, kk * f4[:, t], kk)
        rows_out.append(jnp.sum(kk * q4[:, t], axis=-1)[:, None, :])
    s_ii = jnp.concatenate(rows_out, axis=1)
    tri = lax.broadcasted_iota(jnp.int32, (nb, s, s), 1) >= lax.broadcasted_iota(jnp.int32, (nb, s, s), 2)
    return jnp.where(tri, s_ii, 0.0)


def hg_chunk(hq, hf, hi, hg, l0, l1, nw, st):
    nb = hq.shape[0]
    m = jnp.maximum(l0, l1)
    e0 = jnp.exp(l0 - m)
    e1 = jnp.exp(l1 - m)
    lb = e0 / (e0 + e1)
    sig = _sigmoid(hf)
    q = _silu(hq)
    f = lb + (1.0 - lb) * sig
    log_f = jnp.log(f)
    k = (1.0 - lb) * _sigmoid(-hf)
    v = hi
    rows = lax.broadcasted_iota(jnp.int32, (nb, CHUNK, CHUNK), 1)
    cols = lax.broadcasted_iota(jnp.int32, (nb, CHUNK, CHUNK), 2)
    b = mm_nn((rows >= cols).astype(F32), log_f, EXACT_LHS)
    o_inter = mm_nt(q * jnp.exp(b), st)
    outs = []
    for i in range(CHUNK // SUB):
        lo = i * SUB
        b_i, q_i, k_i, v_i = b[:, lo:lo + SUB], q[:, lo:lo + SUB], k[:, lo:lo + SUB], v[:, lo:lo + SUB]
        o_i = mm_nn(_diag_scores(q_i, k_i, f[:, lo:lo + SUB]), v_i)
        if i > 0:
            b_ref = b[:, :, None, :][:, lo - 1]
            q_t = q_i * jnp.exp(b_i - b_ref)
            k_t = k[:, :lo] * jnp.exp(b_ref - b[:, :lo])
            o_i = o_i + mm_nn(mm_nt(q_t, k_t), v[:, :lo])
        outs.append(o_i)
    o = o_inter + jnp.concatenate(outs, axis=1)
    b_last = b[:, :, None, :][:, CHUNK - 1]
    st_new = st * jnp.exp(b_last) + mm_tn(v, k * jnp.exp(b_last - b))
    return _gated_norm(o, hg, nw), st_new


@jax.custom_vjp
def _unit_lower_inverse(a):
    n = a.shape[-1]
    eye = (lax.broadcasted_iota(jnp.int32, a.shape, 1) == lax.broadcasted_iota(jnp.int32, a.shape, 2)).astype(F32)
    x = eye - a
    p = mm_nn(a, a, SPLIT)
    x = x + mm_nn(x, p, SPLIT)
    power = 4
    while power < n:
        p = mm_nn(p, p, SPLIT)
        x = x + mm_nn(x, p, SPLIT)
        power *= 2
    return x


def _unit_lower_inverse_fwd(a):
    t = _unit_lower_inverse(a)
    return t, t


def _unit_lower_inverse_bwd(t, dt):
    return (-mm_tn(t, mm_nt(dt, t, SPLIT), SPLIT),)


_unit_lower_inverse.defvjp(_unit_lower_inverse_fwd, _unit_lower_inverse_bwd)


@jax.custom_vjp
def _kept_inverse(a, t):
    return t


_kept_inverse.defvjp(lambda a, t: (t, t),
                     lambda t, dt: (-mm_tn(t, mm_nt(dt, t, SPLIT), SPLIT), jnp.zeros_like(t)))


def gd_chunk(cq, ck, cv, z, ab, alog, dtb, nw, s, kept_t=None, keep_t=False):
    nb, c, _ = cq.shape
    lane = lax.broadcasted_iota(jnp.int32, (nb, 1, DH), 2)
    head = lax.broadcasted_iota(jnp.int32, (nb, 1, DH), 0)
    pick = lambda t, off: jnp.sum(jnp.where(lane == head + off, t[None], 0.0), axis=2, keepdims=True)
    a_raw, b_raw = pick(ab, 0), pick(ab, HEADS)
    beta = _sigmoid(b_raw)
    g = -jnp.exp(pick(alog, 0)) * _softplus(a_raw + pick(dtb, 0))
    q = _l2n(_silu(cq)) * (DH ** -0.5)
    k = _l2n(_silu(ck))
    v = _silu(cv)
    rows = lax.broadcasted_iota(jnp.int32, (nb, c, c), 1)
    cols = lax.broadcasted_iota(jnp.int32, (nb, c, c), 2)
    tril = (rows >= cols).astype(F32)
    gc_w = mm_nn(tril, jnp.broadcast_to(g, (nb, c, DH)), EXACT_LHS)
    g_sq = jnp.broadcast_to(g, (nb, c, c))
    gc_col = mm_nn(tril, g_sq, EXACT_LHS)
    gc_row = mm_nn(jnp.ones((nb, c, c), F32), g_sq * (rows <= cols).astype(F32), EXACT_LHS)
    rel = jnp.exp(jnp.minimum(gc_col - gc_row, 0.0))
    a = jnp.where(rows > cols, beta * mm_nt(k, k) * rel, 0.0)
    t_inv = _unit_lower_inverse(a) if kept_t is None else _kept_inverse(a, kept_t)
    e_gc = jnp.exp(gc_w)
    w = mm_nn(t_inv, beta * e_gc * k, SPLIT)
    u = mm_nn(t_inv, beta * v, SPLIT)
    v_new = u - mm_nn(w, s)
    attn = jnp.where(rows >= cols, mm_nt(q, k) * rel, 0.0)
    o = mm_nn(q * e_gc, s) + mm_nn(attn, v_new)
    g_last = jnp.sum(g, axis=1, keepdims=True)
    s_new = jnp.exp(g_last) * s + mm_tn(k * jnp.exp(g_last - gc_w), v_new)
    out = _gated_norm(o, z, nw)
    return (out, s_new, t_inv) if keep_t else (out, s_new)


def _gd_conv(x_ref, tail_ref, cw_ref, xe_ref, first):
    xe_ref[0:TAIL, :] = jnp.where(first, 0.0, tail_ref[:, 0:3 * HW])
    xe_ref[TAIL:TAIL + CHUNK, :] = x_ref[:, 0:3 * HW]
    y = cw_ref[pl.ds(0, 1), :] * xe_ref[pl.ds(TAIL - CONV_K + 1, CHUNK), :]
    for k in range(1, CONV_K):
        y = y + cw_ref[pl.ds(k, 1), :] * xe_ref[pl.ds(TAIL - CONV_K + 1 + k, CHUNK), :]
    return y


def _conv_heads(y, piece):
    return jnp.stack([y[:, piece * HW + DH * j:piece * HW + DH * (j + 1)] for j in range(HEADS)])


def mixer_fwd(proj, proj_ab, l0, l1, hg_nw, cw, alog, dtb, gd_nw):
    lp = proj.shape[0]
    nc = lp // CHUNK

    def body(xh_ref, l0_ref, l1_ref, hnw_ref, xg_ref, tail_ref, ab_ref, cw_ref, alog_ref, dtb_ref, gnw_ref,
             oa_ref, sta_out_ref, ob_ref, stb_out_ref, tinv_ref, sta_ref, stb_ref, xe_ref):
        c = pl.program_id(0)

        @pl.when(c == 0)
        def _():
            sta_ref[...] = jnp.zeros(sta_ref.shape, F32)
            stb_ref[...] = jnp.zeros(stb_ref.shape, F32)

        @pl.when(c < N_SKIP)
        def _():
            for ref in (oa_ref, sta_out_ref, ob_ref, stb_out_ref, tinv_ref):
                ref[...] = jnp.zeros(ref.shape, ref.dtype)

        @pl.when(c >= N_SKIP)
        def _():
            stb = stb_ref[...]
            stb_out_ref[0] = stb
            y = _gd_conv(xg_ref, tail_ref, cw_ref, xe_ref, c == 0)
            ob, stb_new, tinv_ref[0] = gd_chunk(
                _conv_heads(y, 0), _conv_heads(y, 1), _conv_heads(y, 2), _heads(xg_ref, 3),
                ab_ref[...], alog_ref[...], dtb_ref[...], gnw_ref[...], stb, keep_t=True)
            _put_heads(ob_ref, 0, ob)
            stb_ref[...] = stb_new
            sta = sta_ref[...]
            sta_out_ref[0] = sta
            oa, sta_new = hg_chunk(_heads(xh_ref, 0), _heads(xh_ref, 1), _heads(xh_ref, 2), _heads(xh_ref, 3),
                                   _heads(l0_ref, 0), _heads(l1_ref, 0), hnw_ref[...], sta)
            _put_heads(oa_ref, 0, oa)
            sta_ref[...] = sta_new

    vec = pl.BlockSpec((1, HW), lambda c: (0, 0))
    one = pl.BlockSpec((1, DH), lambda c: (0, 0))
    st_spec = pl.BlockSpec((1, HEADS, DH, DH), lambda c: (c, 0, 0, 0))
    o_spec = pl.BlockSpec((CHUNK, HW), lambda c: (c, 0))
    o_shape = jax.ShapeDtypeStruct((lp, HW), BF16)
    st_shape = jax.ShapeDtypeStruct((nc, HEADS, DH, DH), F32)
    return pl.pallas_call(
        body, name="mixer_fwd", grid=(nc,),
        in_specs=[pl.BlockSpec((CHUNK, 4 * HW), lambda c: (c, 0)), vec, vec, one,
                  pl.BlockSpec((CHUNK, 4 * HW), lambda c: (c, 1)),
                  pl.BlockSpec((TAIL, 4 * HW), lambda c: (jnp.maximum(c * (CHUNK // TAIL) - 1, 0), 1)),
                  pl.BlockSpec((CHUNK, DH), lambda c: (c, 0)),
                  pl.BlockSpec((CONV_K, 3 * HW), lambda c: (0, 0)), one, one, one],
        out_specs=[o_spec, st_spec, o_spec, st_spec,
                   pl.BlockSpec((1, HEADS, CHUNK, CHUNK), lambda c: (c, 0, 0, 0))],
        out_shape=[o_shape, st_shape, o_shape, st_shape, jax.ShapeDtypeStruct((nc, HEADS, CHUNK, CHUNK), F32)],
        scratch_shapes=[pltpu.VMEM((HEADS, DH, DH), F32), pltpu.VMEM((HEADS, DH, DH), F32),
                        pltpu.VMEM((TAIL + CHUNK, 3 * HW), F32)],
        compiler_params=_params("arbitrary"),
    )(proj, l0, l1, hg_nw, proj, proj, proj_ab, cw, alog, dtb, gd_nw)


def mixer_bwd(proj, proj_ab, l0, l1, hg_nw, cw, alog, dtb, gd_nw, st_a, st_b, t_inv, do_a, do_b, dproj, exch):
    lp = proj.shape[0]
    nc = lp // CHUNK

    def body(xh_ref, l0_ref, l1_ref, hnw_ref, xg_ref, tail_ref, ab_ref, cw_ref, alog_ref, dtb_ref, gnw_ref,
             sta_in_ref, stb_in_ref, tinv_ref, doa_ref, dob_ref, _,
             dx_ref, dl0_ref, dl1_ref, dhnw_ref, dab_ref, dcw_ref, dalog_ref, ddtb_ref, dgnw_ref,
             dsta_ref, dstb_ref, xe_ref, dye_ref, head_ref):
        c = pl.program_id(0)
        cc = nc - 1 - c

        @pl.when(c == 0)
        def _():
            for ref in (dl0_ref, dl1_ref, dhnw_ref, dcw_ref, dalog_ref, ddtb_ref, dgnw_ref, dsta_ref, dstb_ref,
                        head_ref):
                ref[...] = jnp.zeros(ref.shape, F32)

        @pl.when(cc < N_SKIP)
        def _():
            dx_ref[...] = jnp.zeros(dx_ref.shape, dx_ref.dtype)
            dab_ref[...] = jnp.zeros(dab_ref.shape, F32)

        @pl.when(cc >= N_SKIP)
        def _():
            y = _gd_conv(xg_ref, tail_ref, cw_ref, xe_ref, cc == 0)
            _, gd_vjp = jax.vjp(functools.partial(gd_chunk, kept_t=tinv_ref[0]),
                                _conv_heads(y, 0), _conv_heads(y, 1), _conv_heads(y, 2), _heads(xg_ref, 3),
                                ab_ref[...], alog_ref[...], dtb_ref[...], gnw_ref[...], stb_in_ref[0])
            dcq, dck, dcv, dz, dab, dalog, ddtb, dgnw, dstb = gd_vjp((_heads(dob_ref, 0), dstb_ref[...]))
            _, hg_vjp = jax.vjp(hg_chunk, _heads(xh_ref, 0), _heads(xh_ref, 1), _heads(xh_ref, 2), _heads(xh_ref, 3),
                                _heads(l0_ref, 0), _heads(l1_ref, 0), hnw_ref[...], sta_in_ref[0])
            hg_grads = hg_vjp((_heads(doa_ref, 0), dsta_ref[...]))
            dstb_ref[...] = dstb
            dab_ref[...] = dab
            dalog_ref[...] += dalog
            ddtb_ref[...] += ddtb
            dgnw_ref[...] += dgnw
            _put_heads(dx_ref, 4 + 3, dz)
            for i, t in enumerate((dcq, dck, dcv)):
                for j in range(HEADS):
                    dye_ref[0:CHUNK, i * HW + DH * j:i * HW + DH * (j + 1)] = t[j]
            dye_ref[CHUNK:CHUNK + TAIL, :] = head_ref[...]
            head_ref[...] = dye_ref[0:TAIL, :]
            dy = dye_ref[0:CHUNK, :]
            dx = None
            for k in range(CONV_K):
                term = cw_ref[pl.ds(k, 1), :] * dye_ref[pl.ds(CONV_K - 1 - k, CHUNK), :]
                dx = term if dx is None else dx + term
                dcw_ref[pl.ds(k, 1), :] += jnp.sum(dy * xe_ref[pl.ds(TAIL - CONV_K + 1 + k, CHUNK), :],
                                                   axis=0, keepdims=True)
            dx_ref[:, 4 * HW:7 * HW] = dx.astype(dx_ref.dtype)
            for i in range(4):
                _put_heads(dx_ref, i, hg_grads[i])
            _put_heads(dl0_ref, 0, hg_grads[4], accumulate=True)
            _put_heads(dl1_ref, 0, hg_grads[5], accumulate=True)
            dhnw_ref[...] += hg_grads[6]
            dsta_ref[...] = hg_grads[7]

    rc = lambda c: nc - 1 - c
    vec = pl.BlockSpec((1, HW), lambda c: (0, 0))
    one = pl.BlockSpec((1, DH), lambda c: (0, 0))
    one_shape = jax.ShapeDtypeStruct((1, DH), F32)
    vec_shape = jax.ShapeDtypeStruct((1, HW), F32)
    st_spec = pl.BlockSpec((1, HEADS, DH, DH), lambda c: (rc(c), 0, 0, 0))
    do_spec = pl.BlockSpec((CHUNK, HW), lambda c: (rc(c), 0))
    n_in, n_out = 17, 9
    outs = pl.pallas_call(
        host_exchange(body, n_in, n_out, exch, lambda: pl.program_id(0) == 0, lambda: pl.program_id(0) == nc - 1),
        name="mixer_bwd", grid=(nc,), input_output_aliases={n_in - 1: 0}, compiler_params=_params("arbitrary"),
        **_hosted(exch,
                  [pl.BlockSpec((CHUNK, 4 * HW), lambda c: (rc(c), 0)), vec, vec, one,
                   pl.BlockSpec((CHUNK, 4 * HW), lambda c: (rc(c), 1)),
                   pl.BlockSpec((TAIL, 4 * HW), lambda c: (jnp.maximum(rc(c) * (CHUNK // TAIL) - 1, 0), 1)),
                   pl.BlockSpec((CHUNK, DH), lambda c: (rc(c), 0)),
                   pl.BlockSpec((CONV_K, 3 * HW), lambda c: (0, 0)), one, one, one,
                   st_spec, st_spec, pl.BlockSpec((1, HEADS, CHUNK, CHUNK), lambda c: (rc(c), 0, 0, 0)),
                   do_spec, do_spec, pl.BlockSpec(memory_space=pl.ANY)],
                  [pl.BlockSpec((CHUNK, 8 * HW), lambda c: (rc(c), 0)), vec, vec, one,
                   pl.BlockSpec((CHUNK, DH), lambda c: (rc(c), 0)),
                   pl.BlockSpec((CONV_K, 3 * HW), lambda c: (0, 0)), one, one, one],
                  [jax.ShapeDtypeStruct((lp, W_MAIN), BF16), vec_shape, vec_shape, one_shape,
                   jax.ShapeDtypeStruct((lp, DH), F32), jax.ShapeDtypeStruct((CONV_K, 3 * HW), F32),
                   one_shape, one_shape, one_shape],
                  [pltpu.VMEM((HEADS, DH, DH), F32), pltpu.VMEM((HEADS, DH, DH), F32),
                   pltpu.VMEM((TAIL + CHUNK, 3 * HW), F32), pltpu.VMEM((CHUNK + TAIL, 3 * HW), F32),
                   pltpu.VMEM((TAIL, 3 * HW), F32)]),
    )(proj, l0, l1, hg_nw, proj, proj, proj_ab, cw, alog, dtb, gd_nw, st_a, st_b, t_inv, do_a, do_b, dproj,
      *exch.arrays)
    return outs[:n_out], outs[n_out:]


def matmul(name, a, b, mode, m, n, k, out_dtype, tm=512, tn=1024, tk=None, a_off=(0, 0), b_off=(0, 0), exch=None,
           rows_outer=False):
    tm, tn = min(tm, m), min(tn, n)
    tk = k if tk is None else min(tk, k)
    assert m % tm == 0 and n % tn == 0 and k % tk == 0, (name, m, n, k, tm, tn, tk)
    gi, gj, gk = m // tm, n // tn, k // tk
    grid = (gi, gj, gk) if rows_outer else (gj, gi, gk)

    def at(index):
        return (lambda i, j, kk: index(j, i, kk)) if rows_outer else index

    if mode == "nn":
        a_spec = pl.BlockSpec((tm, tk), at(lambda j, i, kk: (i + a_off[0], kk + a_off[1])))
        b_spec = pl.BlockSpec((tk, tn), at(lambda j, i, kk: (kk + b_off[0], j + b_off[1])))
        dims = _NN
    elif mode == "nt":
        a_spec = pl.BlockSpec((tm, tk), at(lambda j, i, kk: (i + a_off[0], kk + a_off[1])))
        b_spec = pl.BlockSpec((tn, tk), at(lambda j, i, kk: (j + b_off[0], kk + b_off[1])))
        dims = _NT
    else:
        a_spec = pl.BlockSpec((tk, tm), at(lambda j, i, kk: (kk + a_off[0], i + a_off[1])))
        b_spec = pl.BlockSpec((tk, tn), at(lambda j, i, kk: (kk + b_off[0], j + b_off[1])))
        dims = _TN

    def body(a_ref, b_ref, o_ref, *acc):
        d = lax.dot_general(a_ref[...].astype(BF16), b_ref[...].astype(BF16), (dims, ((), ())),
                            preferred_element_type=F32)
        if gk == 1:
            o_ref[...] = d.astype(o_ref.dtype)
        else:
            acc_ref, = acc
            kk = pl.program_id(2)

            @pl.when(kk == 0)
            def _():
                acc_ref[...] = d

            @pl.when(kk > 0)
            def _():
                acc_ref[...] += d

            @pl.when(kk == gk - 1)
            def _():
                o_ref[...] = acc_ref[...].astype(o_ref.dtype)

    o_spec = pl.BlockSpec((tm, tn), at(lambda j, i, kk: (i, j)))
    o_shape = jax.ShapeDtypeStruct((m, n), out_dtype)
    scratch = [] if gk == 1 else [pltpu.VMEM((tm, tn), F32)]
    if exch is None:
        return pl.pallas_call(
            body, name=name, grid=grid, in_specs=[a_spec, b_spec], out_specs=o_spec, out_shape=o_shape,
            scratch_shapes=scratch, compiler_params=_params("parallel", "parallel", "arbitrary"),
        )(a, b)
    step = lambda: (pl.program_id(0), pl.program_id(1), pl.program_id(2))
    first = lambda: jnp.logical_and(jnp.logical_and(step()[0] == 0, step()[1] == 0), step()[2] == 0)
    last = lambda: jnp.logical_and(jnp.logical_and(step()[0] == grid[0] - 1, step()[1] == grid[1] - 1),
                                   step()[2] == gk - 1)
    outs = pl.pallas_call(
        host_exchange(body, 2, 1, exch, first, last), name=name, grid=grid,
        compiler_params=_params("arbitrary", "arbitrary", "arbitrary"),
        **_hosted(exch, [a_spec, b_spec], [o_spec], [o_shape], scratch),
    )(a, b, *exch.arrays)
    return outs[0], outs[1:]


def _swiglu(gu):
    return _silu(gu[:, :FF_BLK]) * gu[:, FF_BLK:]


def ffn_in_fused(xn2, w_gu, tm=512):
    l = xn2.shape[0]
    tn = 2 * FF_BLK

    def body(a_ref, b_ref, gu_ref, act_ref):
        gu = lax.dot_general(a_ref[...], b_ref[...], (_NN, ((), ())), preferred_element_type=F32)
        gu_ref[...] = gu
        act_ref[...] = _swiglu(gu).astype(act_ref.dtype)

    return pl.pallas_call(
        body, name="ffn_in", grid=(2 * D_FF // tn, l // tm),
        in_specs=[pl.BlockSpec((tm, D), lambda j, i: (i, 0)), pl.BlockSpec((D, tn), lambda j, i: (0, j))],
        out_specs=[pl.BlockSpec((tm, tn), lambda j, i: (i, j)), pl.BlockSpec((tm, FF_BLK), lambda j, i: (i, j))],
        out_shape=[jax.ShapeDtypeStruct((l, 2 * D_FF), F32), jax.ShapeDtypeStruct((l, D_FF), BF16)],
        compiler_params=_params("parallel", "parallel"),
    )(xn2, w_gu)


def d_act_fused(dh2, w_fo, gu, tm=512):
    l = dh2.shape[0]

    def body(a_ref, b_ref, gu_ref, o_ref):
        da = lax.dot_general(a_ref[...], b_ref[...], (_NT, ((), ())), preferred_element_type=F32)
        _, vjp = jax.vjp(_swiglu, gu_ref[...])
        dgu, = vjp(da)
        o_ref[...] = dgu.astype(o_ref.dtype)

    return pl.pallas_call(
        body, name="d_act", grid=(D_FF // FF_BLK, l // tm),
        in_specs=[pl.BlockSpec((tm, D), lambda j, i: (i, 0)), pl.BlockSpec((FF_BLK, D), lambda j, i: (j, 0)),
                  pl.BlockSpec((tm, 2 * FF_BLK), lambda j, i: (i, j))],
        out_specs=pl.BlockSpec((tm, 2 * FF_BLK), lambda j, i: (i, j)),
        out_shape=jax.ShapeDtypeStruct((l, 2 * D_FF), BF16),
        compiler_params=_params("parallel", "parallel"),
    )(dh2, w_fo, gu)


TR = 256
N_PRE = PRE // TR


def _row(width, off=0, col=0):
    return pl.BlockSpec((TR, width), lambda i: (i + off, col))


def _pre_spec():
    return pl.BlockSpec((TR, D), lambda i: (jnp.minimum(i, N_PRE - 1), 0))


def _seq_spec(width=D, col=0):
    return pl.BlockSpec((TR, width), lambda i: (jnp.maximum(i - N_PRE, 0), col))


def _vec_spec(width=D):
    return pl.BlockSpec((1, width), lambda i: (0, 0))


def norm1_fwd(pre, x, w):
    lp = PRE + x.shape[0]

    def body(pre_ref, x_ref, w_ref, o_ref):
        h = jnp.where(pl.program_id(0) < N_PRE, pre_ref[...], x_ref[...])
        o_ref[...] = _rms_norm(h, w_ref[...]).astype(o_ref.dtype)

    return pl.pallas_call(
        body, name="norm1_fwd", grid=(lp // TR,),
        in_specs=[_pre_spec(), _seq_spec(), _vec_spec()],
        out_specs=_row(D), out_shape=jax.ShapeDtypeStruct((lp, D), BF16),
        compiler_params=_params("parallel"),
    )(pre, x, w)


def norm1_bwd(pre, x, w, dxn, dxn_ab, dh1):
    l = x.shape[0]
    lp = PRE + l

    def body(pre_ref, x_ref, w_ref, dxn_ref, dxn_ab_ref, dh1_ref, dx_ref, dpre_ref, dw_ref):
        i = pl.program_id(0)
        h = jnp.where(i < N_PRE, pre_ref[...], x_ref[...])
        _, vjp = jax.vjp(_rms_norm, h, w_ref[...])
        dh, dw = vjp(dxn_ref[...] + dxn_ab_ref[...])

        @pl.when(i == 0)
        def _():
            dw_ref[...] = dw

        @pl.when(i > 0)
        def _():
            dw_ref[...] += dw

        @pl.when(i < N_PRE)
        def _():
            dpre_ref[...] = dh

        @pl.when(i >= N_PRE)
        def _():
            dx_ref[...] = dh + dh1_ref[...]

    return pl.pallas_call(
        body, name="norm1_bwd", grid=(lp // TR,),
        in_specs=[_pre_spec(), _seq_spec(), _vec_spec(), _row(D), _row(D), _seq_spec()],
        out_specs=[_seq_spec(), _pre_spec(), _vec_spec()],
        out_shape=[jax.ShapeDtypeStruct((l, D), F32), jax.ShapeDtypeStruct((PRE, D), F32),
                   jax.ShapeDtypeStruct((1, D), F32)],
        compiler_params=_params("arbitrary"),
    )(pre, x, w, dxn, dxn_ab, dh1)


def _merge(gates, ya, yb):
    return _sigmoid(gates[:, :D]) * ya + _sigmoid(gates[:, D:]) * yb


def merge_bwd(dh1, wo, proj, ya, yb, exch):
    l = ya.shape[0]
    lp = PRE + l

    def body(a_ref, b_ref, g_ref, ya_ref, yb_ref, dg_ref, dya_ref, dyb_ref):
        i = pl.program_id(0)

        @pl.when(i < N_PRE)
        def _():
            dg_ref[...] = jnp.zeros(dg_ref.shape, dg_ref.dtype)
            dya_ref[...] = jnp.zeros(dya_ref.shape, dya_ref.dtype)
            dyb_ref[...] = jnp.zeros(dyb_ref.shape, dyb_ref.dtype)

        @pl.when(i >= N_PRE)
        def _():
            dm = lax.dot_general(a_ref[...], b_ref[...], (_NT, ((), ())), preferred_element_type=F32)
            _, vjp = jax.vjp(_merge, g_ref[...], ya_ref[...], yb_ref[...])
            dg, dya, dyb = vjp(dm)
            dg_ref[...] = dg.astype(dg_ref.dtype)
            dya_ref[...] = dya.astype(dya_ref.dtype)
            dyb_ref[...] = dyb.astype(dyb_ref.dtype)

    last = lp // TR - 1
    outs = pl.pallas_call(
        host_exchange(body, 5, 3, exch, lambda: pl.program_id(0) == 0, lambda: pl.program_id(0) == last),
        name="merge_bwd", grid=(lp // TR,), compiler_params=_params("arbitrary"),
        **_hosted(exch,
                  [_seq_spec(), pl.BlockSpec((D, D), lambda i: (0, 0)),
                   pl.BlockSpec((TR, 2 * D), lambda i: (jnp.maximum(i, N_PRE), 2)), _seq_spec(), _seq_spec()],
                  [_row(2 * D, 0, 2), _row(D), _row(D)],
                  [jax.ShapeDtypeStruct((lp, W_MAIN), BF16), jax.ShapeDtypeStruct((lp, D), BF16),
                   jax.ShapeDtypeStruct((lp, D), BF16)], []),
    )(dh1, wo, proj, ya, yb, *exch.arrays)
    return outs[:3], outs[3:]


def merge_mix_out(proj, ya, yb, wo, x, w):
    l = x.shape[0]

    def body(g_ref, ya_ref, yb_ref, b_ref, x_ref, w_ref, m_ref, h_ref, o_ref):
        merged = _merge(g_ref[...], ya_ref[...], yb_ref[...]).astype(BF16)
        m_ref[...] = merged
        h = x_ref[...] + lax.dot_general(merged, b_ref[...], (_NN, ((), ())), preferred_element_type=F32)
        h_ref[...] = h
        o_ref[...] = _rms_norm(h, w_ref[...]).astype(o_ref.dtype)

    return pl.pallas_call(
        body, name="mix_out", grid=(l // TR,),
        in_specs=[_row(2 * D, N_PRE, 2), _row(D), _row(D), pl.BlockSpec((D, D), lambda i: (0, 0)), _row(D),
                  _vec_spec()],
        out_specs=[_row(D), _row(D), _row(D)],
        out_shape=[jax.ShapeDtypeStruct((l, D), BF16), jax.ShapeDtypeStruct((l, D), F32),
                   jax.ShapeDtypeStruct((l, D), BF16)],
        compiler_params=_params("parallel"),
    )(proj, ya, yb, wo, x, w)


def norm2_bwd(h1, w, dxn2, dh2):
    l = h1.shape[0]

    def body(h_ref, w_ref, dxn_ref, dh2_ref, dh1_ref, dh1b_ref, dw_ref):
        i = pl.program_id(0)
        _, vjp = jax.vjp(_rms_norm, h_ref[...], w_ref[...])
        dh, dw = vjp(dxn_ref[...])
        dh1 = dh + dh2_ref[...]
        dh1_ref[...] = dh1
        dh1b_ref[...] = dh1.astype(BF16)

        @pl.when(i == 0)
        def _():
            dw_ref[...] = dw

        @pl.when(i > 0)
        def _():
            dw_ref[...] += dw

    return pl.pallas_call(
        body, name="norm2_bwd", grid=(l // TR,),
        in_specs=[_row(D), _vec_spec(), _row(D), _row(D)],
        out_specs=[_row(D), _row(D), _vec_spec()],
        out_shape=[jax.ShapeDtypeStruct((l, D), F32), jax.ShapeDtypeStruct((l, D), BF16),
                   jax.ShapeDtypeStruct((1, D), F32)],
        compiler_params=_params("arbitrary"),
    )(h1, w, dxn2, dh2)


def _loss_rows(h1, f, w, tgt):
    y = _rms_norm(h1 + f, w)
    err = (y - tgt) * (y - tgt)
    return 0.5 * jnp.sum(jnp.mean(err, axis=-1, keepdims=True), axis=0, keepdims=True)


def loss_head(h1, f, w, tgt):
    l = h1.shape[0]

    def body(h_ref, f_ref, w_ref, t_ref, loss_ref, dh_ref, dhb_ref, dw_ref):
        i = pl.program_id(0)
        loss, vjp = jax.vjp(_loss_rows, h_ref[...], f_ref[...], w_ref[...], t_ref[...])
        dh, _, dw, _ = vjp(jnp.ones((1, 1), F32))
        dh_ref[...] = dh
        dhb_ref[...] = dh.astype(BF16)
        loss = jnp.broadcast_to(loss, (1, LANES))

        @pl.when(i == 0)
        def _():
            dw_ref[...] = dw
            loss_ref[...] = loss

        @pl.when(i > 0)
        def _():
            dw_ref[...] += dw
            loss_ref[...] += loss

    return pl.pallas_call(
        body, name="loss_head", grid=(l // TR,),
        in_specs=[_row(D), _row(D), _vec_spec(), _row(D)],
        out_specs=[_vec_spec(LANES), _row(D), _row(D), _vec_spec()],
        out_shape=[jax.ShapeDtypeStruct((1, LANES), F32), jax.ShapeDtypeStruct((l, D), F32),
                   jax.ShapeDtypeStruct((l, D), BF16), jax.ShapeDtypeStruct((1, D), F32)],
        compiler_params=_params("arbitrary"),
    )(h1, f, w, tgt)


def _adamw(w, g, m, v):
    m = ADAM_B1 * m + (1.0 - ADAM_B1) * g
    v = ADAM_B2 * v + (1.0 - ADAM_B2) * (g * g)
    m_hat = m / (1.0 - ADAM_B1 ** ADAM_STEP)
    v_hat = v / (1.0 - ADAM_B2 ** ADAM_STEP)
    delta = -ADAM_LR * (m_hat / (jnp.sqrt(v_hat) + ADAM_EPS) + ADAM_WD * w)
    return delta, m, v


def adamw_shard(name, w, m, v, sums, recv, tile):
    r, c = w.shape
    tr, tc = tile
    assert r % tr == 0 and c % tc == 0

    def body(chip_ref, w_ref, m_ref, v_ref, own_ref, r0_ref, r1_ref, r2_ref, g_ref, d_ref, nm_ref, nv_ref):
        g = own_ref[...].astype(F32) + r0_ref[...].astype(F32)
        g = g + r1_ref[...].astype(F32)
        g = g + r2_ref[...].astype(F32)
        d, nm, nv = _adamw(w_ref[...], g, m_ref[...], v_ref[...])
        g_ref[...] = g
        d_ref[...] = d
        nm_ref[...] = nm
        nv_ref[...] = nv

    blk = pl.BlockSpec((tr, tc), lambda i, j, chip: (i, j))
    part = lambda s: pl.BlockSpec((None, tr, tc), lambda i, j, chip: (s, i, j))
    own = pl.BlockSpec((None, tr, tc), lambda i, j, chip: (chip[0], i, j))
    shape = jax.ShapeDtypeStruct((r, c), F32)
    my_chip = (2 * lax.axis_index("x") + lax.axis_index("y")).astype(jnp.int32).reshape(1)
    return pl.pallas_call(
        body, name=name, out_shape=[shape, shape, shape, shape],
        grid_spec=pltpu.PrefetchScalarGridSpec(num_scalar_prefetch=1, grid=(r // tr, c // tc),
                                               in_specs=[blk, blk, blk, own, part(0), part(1), part(2)],
                                               out_specs=[blk, blk, blk, blk]),
        compiler_params=_params("parallel", "parallel"),
    )(my_chip, w, m, v, sums, recv, recv, recv)


def adamw_small(w, g, m, v):
    def body(w_ref, g_ref, m_ref, v_ref, d_ref, nm_ref, nv_ref):
        d, nm, nv = _adamw(w_ref[...], g_ref[...], m_ref[...], v_ref[...])
        d_ref[...] = d
        nm_ref[...] = nm
        nv_ref[...] = nv

    shape = jax.ShapeDtypeStruct(w.shape, F32)
    return pl.pallas_call(body, name="adamw_small", out_shape=[shape, shape, shape])(w, g, m, v)


def pair_add(name, a, b, tile):
    q, r, c = b.shape
    tr, tc = tile
    assert r % tr == 0 and c % tc == 0

    def body(core_ref, a_ref, b_ref, o_ref):
        o_ref[...] = (a_ref[...].astype(F32) + b_ref[...].astype(F32)).astype(o_ref.dtype)

    blk = pl.BlockSpec((None, tr, tc), lambda s, i, j, core: (s, i, j))
    mine = pl.BlockSpec((None, None, tr, tc), lambda s, i, j, core: (s, core[0], i, j))
    return pl.pallas_call(
        body, name=name, out_shape=jax.ShapeDtypeStruct((q, r, c), BF16),
        grid_spec=pltpu.PrefetchScalarGridSpec(num_scalar_prefetch=1, grid=(q, r // tr, c // tc),
                                               in_specs=[mine, blk], out_specs=blk),
        compiler_params=_params("parallel", "parallel", "parallel"),
    )(lax.axis_index("c").astype(jnp.int32).reshape(1), a, b)


def _w_in_pieces():
    c = W_IN // N_DEV
    ranges = ((0, 8 * HW, 0, 0), (8 * HW, 8 * HW + 2 * HEADS, 1, 8 * HW), (8 * HW + 2 * HEADS, W_IN, 0, 2 * HEADS))
    out = []
    for d in range(N_DEV):
        for lo, hi, target, shift in ranges:
            s, e = max(lo, c * d), min(hi, c * (d + 1))
            if s < e:
                out.append((d, s - c * d, e - s, target, s - shift))
    return out


def _ffn_in_pieces():
    c = 2 * D_FF // N_DEV
    out = []
    for d in range(N_DEV):
        s = c * d
        while s < c * (d + 1):
            e = min((s // FF_BLK + 1) * FF_BLK, c * (d + 1))
            half, blk = divmod(s // FF_BLK, D_FF // FF_BLK)
            out.append((d, s - c * d, e - s, 0, (2 * blk + half) * FF_BLK + s % FF_BLK))
            s = e
    return out


def _plain_pieces(c):
    return [(d, 0, c, 0, c * d) for d in range(N_DEV)]


def shards_to_cols(name, g, widths, pieces, tr):
    _, r, c = g.shape
    covered = [sum(p[2] for p in pieces if p[3] == t) for t in range(len(widths))]

    def body(g_ref, *outs):
        for t, o in enumerate(outs):
            if covered[t] < widths[t]:
                o[...] = jnp.zeros(o.shape, o.dtype)
        for d, s, w, t, ts in pieces:
            outs[t][:, ts:ts + w] = g_ref[d, :, s:s + w]

    return pl.pallas_call(
        body, name=name, grid=(r // tr,),
        in_specs=[pl.BlockSpec((N_DEV, tr, c), lambda i: (0, i, 0))],
        out_specs=[pl.BlockSpec((tr, w), lambda i: (i, 0)) for w in widths],
        out_shape=[jax.ShapeDtypeStruct((r, w), g.dtype) for w in widths],
        compiler_params=_params("parallel"),
    )(g)


def cols_to_shards(name, srcs, c, pieces, tr):
    r = srcs[0].shape[0]

    def body(*refs):
        o = refs[-1]
        for d, s, w, t, ts in pieces:
            o[d, :, s:s + w] = refs[t][:, ts:ts + w]

    return pl.pallas_call(
        body, name=name, grid=(r // tr,),
        in_specs=[pl.BlockSpec((tr, t.shape[1]), lambda i: (i, 0)) for t in srcs],
        out_specs=pl.BlockSpec((N_DEV, tr, c), lambda i: (0, i, 0)),
        out_shape=jax.ShapeDtypeStruct((N_DEV, r, c), srcs[0].dtype),
        compiler_params=_params("parallel"),
    )(*srcs)


def rows_to_shards(name, srcs, c, pieces, tc):
    r = srcs[0].shape[1]

    def body(*refs):
        o = refs[-1]
        for d, s, w, t, ts in pieces:
            o[d, s:s + w, :] = refs[t][ts:ts + w, :]

    return pl.pallas_call(
        body, name=name, grid=(r // tc,),
        in_specs=[pl.BlockSpec((t.shape[0], tc), lambda i: (0, i)) for t in srcs],
        out_specs=pl.BlockSpec((N_DEV, c, tc), lambda i: (0, 0, i)),
        out_shape=jax.ShapeDtypeStruct((N_DEV, c, r), srcs[0].dtype),
        compiler_params=_params("parallel"),
    )(*srcs)


def _place():
    return lax.axis_index("x"), lax.axis_index("y"), lax.axis_index("c")


def _dev(px, py, pc):
    return 4 * px + 2 * py + pc


class Exchange:
    def __init__(self, arrays, out_shapes, sems, start, finish):
        self.arrays, self.out_shapes, self.sems, self.start, self.finish = arrays, out_shapes, sems, start, finish


def gather_exchange(arrays):
    n = len(arrays)

    def plan(ins, outs, sems):
        send_sems, recv_sems, local_sems = sems
        x, y, c = _place()
        me, sibling = (x, y, c), (x, y, 1 - c)
        chips = [(1 - x, y), (x, 1 - y), (1 - x, 1 - y)]

        def copy(a, k, block, to, src=None):
            dst = outs[a].at[_dev(*block)]
            return pltpu.make_async_remote_copy(
                src_ref=dst if src is None else src, dst_ref=dst, send_sem=send_sems.at[a, k],
                recv_sem=recv_sems.at[a, k], device_id=to, device_id_type=MESH)

        mine = [pltpu.make_async_copy(ins[a], outs[a].at[_dev(*me)], local_sems.at[a]) for a in range(n)]
        first = []
        for a in range(n):
            first.append(copy(a, 0, me, sibling, src=ins[a]))
            first += [copy(a, 1 + j, me, (*chip, c), src=ins[a]) for j, chip in enumerate(chips)]
        return me, sibling, chips, c, copy, mine, first

    def start(ins, outs, sems):
        *_, mine, first = plan(ins, outs, sems)
        for cp in mine + first:
            cp.start()

    def finish(ins, outs, sems):
        me, sibling, chips, c, copy, mine, first = plan(ins, outs, sems)
        passed = []
        for j, chip in enumerate(chips):
            for a in range(n):
                copy(a, 1 + j, (*chip, c), me).wait_recv()
                fwd = copy(a, 4 + j, (*chip, c), sibling)
                fwd.start()
                passed.append(fwd)
        for a in range(n):
            copy(a, 0, sibling, me).wait_recv()
        for j, chip in enumerate(chips):
            for a in range(n):
                copy(a, 4 + j, (*chip, 1 - c), me).wait_recv()
        for cp in first + passed:
            cp.wait_send()
        for cp in mine:
            cp.wait()

    return Exchange(arrays, [jax.ShapeDtypeStruct((N_DEV,) + t.shape, t.dtype) for t in arrays],
                    [pltpu.SemaphoreType.DMA((n, 7)), pltpu.SemaphoreType.DMA((n, 7)), pltpu.SemaphoreType.DMA((n,))],
                    start, finish)


def core_exchange(arrays):
    n = len(arrays)

    def copies(ins, outs, sems):
        send_sems, recv_sems = sems
        x, y, c = _place()
        return [pltpu.make_async_remote_copy(
            src_ref=ins[a].at[q, 1 - c], dst_ref=outs[a].at[q], send_sem=send_sems.at[a, q],
            recv_sem=recv_sems.at[a, q], device_id=(x, y, 1 - c), device_id_type=MESH)
            for a in range(n) for q in range(4)]

    def start(ins, outs, sems):
        for cp in copies(ins, outs, sems):
            cp.start()

    def finish(ins, outs, sems):
        for cp in copies(ins, outs, sems):
            cp.wait()

    return Exchange(arrays, [jax.ShapeDtypeStruct((4,) + t.shape[2:], t.dtype) for t in arrays],
                    [pltpu.SemaphoreType.DMA((n, 4)), pltpu.SemaphoreType.DMA((n, 4))], start, finish)


def chips_exchange(arrays):
    n = len(arrays)

    def copies(ins, outs, sems):
        send_sems, recv_sems = sems
        x, y, c = _place()
        chips = [(1 - x, y), (x, 1 - y), (1 - x, 1 - y)]
        return [pltpu.make_async_remote_copy(
            src_ref=ins[a].at[2 * qx + qy], dst_ref=outs[a].at[j], send_sem=send_sems.at[a, j],
            recv_sem=recv_sems.at[a, j], device_id=(qx, qy, c), device_id_type=MESH)
            for a in range(n) for j, (qx, qy) in enumerate(chips)]

    def start(ins, outs, sems):
        for cp in copies(ins, outs, sems):
            cp.start()

    def finish(ins, outs, sems):
        for cp in copies(ins, outs, sems):
            cp.wait()

    return Exchange(arrays, [jax.ShapeDtypeStruct((3,) + t.shape[1:], t.dtype) for t in arrays],
                    [pltpu.SemaphoreType.DMA((n, 3)), pltpu.SemaphoreType.DMA((n, 3))], start, finish)


def run_exchange(name, exch):
    n_in, n_out = len(exch.arrays), len(exch.out_shapes)
    any_spec = pl.BlockSpec(memory_space=pl.ANY)

    def body(*refs):
        ins, outs, sems = refs[:n_in], refs[n_in:n_in + n_out], refs[n_in + n_out:]
        exch.start(ins, outs, sems)
        exch.finish(ins, outs, sems)

    return pl.pallas_call(
        body, name=name, in_specs=[any_spec] * n_in, out_specs=[any_spec] * n_out, out_shape=exch.out_shapes,
        scratch_shapes=exch.sems,
    )(*exch.arrays)


def host_exchange(body, n_in, n_out, exch, first, last):
    ne_in, ne_out, ne_sem = len(exch.arrays), len(exch.out_shapes), len(exch.sems)

    def wrapped(*refs):
        ins, refs = refs[:n_in], refs[n_in:]
        e_ins, refs = refs[:ne_in], refs[ne_in:]
        outs, refs = refs[:n_out], refs[n_out:]
        e_outs, refs = refs[:ne_out], refs[ne_out:]
        scratch, e_sems = refs[:len(refs) - ne_sem], refs[len(refs) - ne_sem:]

        @pl.when(first())
        def _():
            exch.start(e_ins, e_outs, e_sems)

        body(*ins, *outs, *scratch)

        @pl.when(last())
        def _():
            exch.finish(e_ins, e_outs, e_sems)

    return wrapped


def _hosted(exch, in_specs, out_specs, out_shape, scratch_shapes):
    any_spec = pl.BlockSpec(memory_space=pl.ANY)
    return dict(in_specs=list(in_specs) + [any_spec] * len(exch.arrays),
                out_specs=list(out_specs) + [any_spec] * len(exch.out_shapes),
                out_shape=list(out_shape) + list(exch.out_shapes),
                scratch_shapes=list(scratch_shapes) + list(exch.sems))


def allreduce_small(buf):
    r = buf.shape[0]
    vmem = pl.BlockSpec(memory_space=pltpu.VMEM)

    def body(buf_ref, out_ref, all_ref, send_sems, recv_sems):
        x, y, c = _place()
        me = _dev(x, y, c)
        copies = []
        for k in range(1, N_DEV):
            peer = (x ^ (k >> 2), y ^ ((k >> 1) & 1), c ^ (k & 1))
            copies.append(pltpu.make_async_remote_copy(
                src_ref=buf_ref, dst_ref=all_ref.at[me], send_sem=send_sems.at[k - 1], recv_sem=recv_sems.at[k - 1],
                device_id=peer, device_id_type=MESH))
        for cp in copies:
            cp.start()
        all_ref[me] = buf_ref[...]
        for cp in copies:
            cp.wait()
        total = all_ref[0]
        for d in range(1, N_DEV):
            total = total + all_ref[d]
        out_ref[...] = total

    return pl.pallas_call(
        body, name="allreduce_small", in_specs=[vmem], out_specs=vmem,
        out_shape=jax.ShapeDtypeStruct((r, LANES), F32),
        scratch_shapes=[pltpu.VMEM((N_DEV, r, LANES), F32), pltpu.SemaphoreType.DMA((N_DEV - 1,)),
                        pltpu.SemaphoreType.DMA((N_DEV - 1,))],
    )(buf)


def _cols_from_shards(g):
    return jnp.transpose(g, (1, 0, 2)).reshape(g.shape[1], N_DEV * g.shape[2])


def _pack(t):
    flat = t.reshape(-1)
    return jnp.pad(flat, (0, -flat.size % (8 * LANES))).reshape(-1, LANES)


def _pad_lanes(t):
    t = t.reshape(1, -1)
    return jnp.pad(t, ((0, 0), (0, LANES - t.shape[1])))


def kernel(x, meta_tokens, lb_logits, mix_norm_w, w_in, hg_norm_w, gd_conv_w, gd_a_log, gd_dt_bias, gd_norm_w, w_branch_a, w_branch_b, w_out, ffn_norm_w, w_ffn_in, w_ffn_out, final_norm_w, loss_target, m_meta_tokens, m_lb_logits, m_mix_norm_w, m_w_in, m_hg_norm_w, m_gd_conv_w, m_gd_a_log, m_gd_dt_bias, m_gd_norm_w, m_w_branch_a, m_w_branch_b, m_w_out, m_ffn_norm_w, m_w_ffn_in, m_w_ffn_out, m_final_norm_w, v_meta_tokens, v_lb_logits, v_mix_norm_w, v_w_in, v_hg_norm_w, v_gd_conv_w, v_gd_a_log, v_gd_dt_bias, v_gd_norm_w, v_w_branch_a, v_w_branch_b, v_w_out, v_ffn_norm_w, v_w_ffn_in, v_w_ffn_out, v_final_norm_w):
    xs = x[0]
    tgt = loss_target[0]
    l = xs.shape[0]
    lp = PRE + l
    me = _dev(*_place())

    big = [w_in[0], w_branch_a[0], w_branch_b[0], w_out[0], w_ffn_in[0], w_ffn_out[0]]
    big16 = [t.astype(BF16) for t in big]
    g_in, g_meta, g_conv = run_exchange("gather_first", gather_exchange([big16[0], meta_tokens, gd_conv_w[0]]))
    w_main, w_ab = shards_to_cols("w_in_cols", g_in, [W_MAIN, LANES], _w_in_pieces(), 256)
    meta_full = _cols_from_shards(g_meta)
    cw = _cols_from_shards(g_conv)
    pre = jnp.concatenate([jnp.zeros((PRE - N_META, D), F32), meta_full], axis=0)
    l0, l1 = lb_logits[0:1], lb_logits[1:2]
    alog, dtb = _pad_lanes(gd_a_log), _pad_lanes(gd_dt_bias)

    xn = norm1_fwd(pre, xs, mix_norm_w)
    proj, (g_ba, g_bb, g_out, g_ffi, g_ffo) = matmul("proj_main", xn, w_main, "nn", lp, W_MAIN, D, F32, tm=lp // 8,
                                                      exch=gather_exchange(big16[1:]))
    wa, = shards_to_cols("w_branch_a_cols", g_ba, [D], _plain_pieces(D // N_DEV), 256)
    wb, = shards_to_cols("w_branch_b_cols", g_bb, [D], _plain_pieces(D // N_DEV), 256)
    wo = g_out.reshape(D, D)
    w_gu, = shards_to_cols("w_ffn_in_cols", g_ffi, [2 * D_FF], _ffn_in_pieces(), 256)
    w_fo = g_ffo.reshape(D_FF, D)
    proj_ab = matmul("proj_ab", xn, w_ab, "nn", lp, LANES, D, F32)
    o_a, st_a, o_b, st_b, t_inv = mixer_fwd(proj, proj_ab, l0, l1, hg_norm_w, cw, alog, dtb, gd_norm_w)
    ya = matmul("branch_a", o_a, wa, "nn", l, D, HW, F32, tn=D, a_off=(PRE // 512, 0))
    yb = matmul("branch_b", o_b, wb, "nn", l, D, HW, F32, tn=D, a_off=(PRE // 512, 0))
    merged, h1, xn2 = merge_mix_out(proj, ya, yb, wo, xs, ffn_norm_w)
    gu, act = ffn_in_fused(xn2, w_gu, tm=min(1024, l))
    f = matmul("ffn_out", act, w_fo, "nn", l, D, D_FF, F32, tm=1024, tn=512)
    loss_part, dh2, dh2_b, d_final_w = loss_head(h1, f, final_norm_w.reshape(1, D), tgt)

    tiles = {"w_in": (W_IN // N_DEV, 256), "w_branch_a": (256, D // N_DEV), "w_branch_b": (256, D // N_DEV),
             "w_out": (64, D), "w_ffn_in": (256, 2 * D_FF // N_DEV), "w_ffn_out": (176, D)}

    def chip_sums(tag, parts, host=None):
        blocks = [p.reshape((4, 2) + p.shape[1:]) for p in parts.values()]
        exch = core_exchange(blocks)
        hosted, from_core = (None, run_exchange("exchange_core_" + tag, exch)) if host is None else host(exch)
        return hosted, {nm: pair_add("pair_add_" + nm, p, r, tiles[nm]) for (nm, p, r) in zip(parts, blocks, from_core)}

    dgu = d_act_fused(dh2_b, w_fo, gu, tm=min(1024, l))
    dw_fo = matmul("dw_ffn_out", act, dh2_b, "tn", D_FF, D, l, BF16, tm=512, tn=512)
    dxn2 = matmul("d_xn2", dgu, w_gu, "nt", l, D, 2 * D_FF, F32, tn=256, rows_outer=True)
    dw_gu = matmul("dw_ffn_in", xn2, dgu, "tn", D, 2 * D_FF, l, BF16, tm=512, tn=512)
    dh1, dh1_b, d_ffn_norm_w = norm2_bwd(h1, ffn_norm_w, dxn2, dh2)
    (dproj, dya, dyb), sums = chip_sums("ffn", {
        "w_ffn_in": cols_to_shards("dw_ffn_in_shards", [dw_gu], 2 * D_FF // N_DEV, _ffn_in_pieces(), 256),
        "w_ffn_out": dw_fo.reshape(N_DEV, D_FF // N_DEV, D)},
        host=lambda exch: merge_bwd(dh1_b, wo, proj, ya, yb, exch))
    dw_o = matmul("dw_out", merged, dh1_b, "tn", D, D, l, BF16, tm=512, tn=512)
    do_a = matmul("d_o_a", dya, wa, "nt", lp, HW, D, F32, tm=lp // 8)
    do_b = matmul("d_o_b", dyb, wb, "nt", lp, HW, D, F32, tm=lp // 8)
    dw_a = matmul("dw_branch_a", o_a, dya, "tn", HW, D, lp, BF16, tm=512, tn=512)
    dw_b = matmul("dw_branch_b", o_b, dyb, "tn", HW, D, lp, BF16, tm=512, tn=512)
    sums.update(chip_sums("mix", {
        "w_branch_a": cols_to_shards("dw_branch_a_shards", [dw_a], D // N_DEV, _plain_pieces(D // N_DEV), 256),
        "w_branch_b": cols_to_shards("dw_branch_b_shards", [dw_b], D // N_DEV, _plain_pieces(D // N_DEV), 256),
        "w_out": dw_o.reshape(N_DEV, D // N_DEV, D)})[1])
    ffn_names, mix_names = ["w_ffn_in", "w_ffn_out"], ["w_branch_a", "w_branch_b", "w_out"]
    (dproj, dl0, dl1, d_hg_nw, dproj_ab, d_conv, d_alog, d_dtb, d_gd_nw), from_chips_early = mixer_bwd(
        proj, proj_ab, l0, l1, hg_norm_w, cw, alog, dtb, gd_norm_w, st_a, st_b, t_inv, do_a, do_b, dproj,
        chips_exchange([sums[nm] for nm in ffn_names + mix_names]))
    dwt_main = matmul("dw_in_main", dproj, xn, "tn", W_MAIN, D, lp, BF16, tm=512, tn=512)
    dwt_ab = matmul("dw_in_ab", dproj_ab, xn, "tn", LANES, D, lp, BF16, tn=512)
    dxn_ab, sums_in = chip_sums("w_in", {
        "w_in": rows_to_shards("dw_in_shards", [dwt_main, dwt_ab], W_IN // N_DEV, _w_in_pieces(), 256)},
        host=lambda exch: matmul("d_xn_ab", dproj_ab, w_ab, "nt", lp, D, LANES, F32, exch=exch))
    sums.update(sums_in)
    from_chips = dict(zip(ffn_names + mix_names, from_chips_early))
    dxn, (from_chips["w_in"],) = matmul("d_xn", dproj, w_main, "nt", lp, D, W_MAIN, F32, tn=256, rows_outer=True,
                                        exch=chips_exchange([sums["w_in"]]))
    grad_x, dpre, d_mix_norm_w = norm1_bwd(pre, xs, mix_norm_w, dxn, dxn_ab, dh1)
    names = ["w_in", "w_branch_a", "w_branch_b", "w_out", "w_ffn_in", "w_ffn_out"]
    moms = [(m_w_in, v_w_in), (m_w_branch_a, v_w_branch_a), (m_w_branch_b, v_w_branch_b), (m_w_out, v_w_out),
            (m_w_ffn_in, v_w_ffn_in), (m_w_ffn_out, v_w_ffn_out)]
    upd = {}
    for nm, w, (m, v) in zip(names, big, moms):
        flip = (lambda t: t.T) if nm == "w_in" else (lambda t: t)
        res = adamw_shard("adamw_" + nm, flip(w), flip(m[0]), flip(v[0]), sums[nm], from_chips[nm], tiles[nm])
        upd[nm] = tuple(flip(t_)[None] for t_ in res)

    d_lb = jnp.concatenate([dl0, dl1], axis=0)
    rep = [_pack(t) for t in (d_lb, d_mix_norm_w, d_hg_nw, d_alog, d_dtb, d_gd_nw, d_ffn_norm_w, d_final_w)]
    n_rep = sum(t.shape[0] for t in rep)
    n_meta = N_META * D // LANES
    tot = allreduce_small(jnp.concatenate(rep + [_pack(loss_part), _pack(dpre[PRE - N_META:]), _pack(d_conv)], axis=0))
    loss = tot[n_rep, 0]
    g_meta_full = tot[n_rep + 8:n_rep + 8 + n_meta].reshape(N_META, D)
    g_conv_full = tot[n_rep + 8 + n_meta:].reshape(CONV_K, 3 * HW)
    g_meta_mine = lax.dynamic_slice_in_dim(g_meta_full, me * (D // N_DEV), D // N_DEV, axis=1)
    g_conv_mine = lax.dynamic_slice_in_dim(g_conv_full, me * (3 * DH), 3 * DH, axis=1)

    small = [("lb_logits", lb_logits, m_lb_logits, v_lb_logits), ("mix_norm_w", mix_norm_w, m_mix_norm_w, v_mix_norm_w),
             ("hg_norm_w", hg_norm_w, m_hg_norm_w, v_hg_norm_w), ("gd_a_log", gd_a_log, m_gd_a_log, v_gd_a_log),
             ("gd_dt_bias", gd_dt_bias, m_gd_dt_bias, v_gd_dt_bias), ("gd_norm_w", gd_norm_w, m_gd_norm_w, v_gd_norm_w),
             ("ffn_norm_w", ffn_norm_w, m_ffn_norm_w, v_ffn_norm_w),
             ("final_norm_w", final_norm_w, m_final_norm_w, v_final_norm_w),
             ("meta_tokens", meta_tokens, m_meta_tokens, v_meta_tokens), ("gd_conv_w", gd_conv_w, m_gd_conv_w, v_gd_conv_w)]

    sizes = [_pack(w).shape[0] for _, w, _, _ in small]
    g_small = jnp.concatenate([tot[:n_rep], _pack(g_meta_mine), _pack(g_conv_mine)], axis=0)
    assert g_small.shape[0] == sum(sizes)
    w_small = jnp.concatenate([_pack(w) for _, w, _, _ in small], axis=0)
    m_small = jnp.concatenate([_pack(m) for _, _, m, _ in small], axis=0)
    v_small = jnp.concatenate([_pack(v) for _, _, _, v in small], axis=0)
    d_small, nm_small, nv_small = adamw_small(w_small, g_small, m_small, v_small)
    row = 0
    for (nm, w, _, _), sz in zip(small, sizes):
        def unpack(t):
            return t[row:row + sz].reshape(-1)[:w.size].reshape(w.shape)
        upd[nm] = (unpack(g_small), unpack(d_small), unpack(nm_small), unpack(nv_small))
        row += sz

    order = ["meta_tokens", "lb_logits", "mix_norm_w", "w_in", "hg_norm_w", "gd_conv_w", "gd_a_log", "gd_dt_bias",
             "gd_norm_w", "w_branch_a", "w_branch_b", "w_out", "ffn_norm_w", "w_ffn_in", "w_ffn_out", "final_norm_w"]
    outs = [loss, grad_x[None]]
    for j in range(4):
        outs += [upd[nm][j] for nm in order]
    return tuple(outs)
```

```python
import functools

import jax
import jax.numpy as jnp
from jax import lax
from jax.experimental import pallas as pl
from jax.experimental.pallas import tpu as pltpu

F32 = jnp.float32
BF16 = jnp.bfloat16
MESH = pl.DeviceIdType.MESH

EPS = 1e-6
D = 2048
N_META = 16
CHUNK = 64
SUB = 16
HEADS = 8
DH = 128
HW = HEADS * DH
CONV_K = 4
TAIL = 8
D_FF = 5632
PRE = 512
N_SKIP = PRE // CHUNK - 1
N_DEV = 8
LANES = 128
FF_BLK = 512
W_IN = 4 * HW + 4 * HW + 2 * HEADS + 2 * D
W_MAIN = 4 * HW + 4 * HW + 2 * D

ADAM_LR = 0.001
ADAM_B1 = 0.9
ADAM_B2 = 0.999
ADAM_EPS = 1e-08
ADAM_WD = 0.01
ADAM_STEP = 10

VMEM_LIMIT = 52 * 1024 * 1024

_NN = ((1,), (0,))
_NT = ((1,), (1,))
_TN = ((0,), (0,))


def _params(*sem):
    return pltpu.CompilerParams(dimension_semantics=sem, vmem_limit_bytes=VMEM_LIMIT)


BF16_ONCE, SPLIT, EXACT_LHS = 0, 1, 2


def _dot_bf16(a, b, dims):
    if a.ndim == 3:
        dn = (((dims[0][0] + 1,), (dims[1][0] + 1,)), ((0,), (0,)))
    else:
        dn = (dims, ((), ()))
    return lax.dot_general(a, b, dn, preferred_element_type=F32)


def _split(t):
    hi = t.astype(BF16)
    return hi, (t - hi.astype(F32)).astype(BF16)


def _raw_dot(a, b, dims, mode):
    if mode == BF16_ONCE:
        return _dot_bf16(a.astype(BF16), b.astype(BF16), dims)
    if mode == SPLIT:
        a_hi, a_lo = _split(a)
        b_hi, b_lo = _split(b)
        return _dot_bf16(a_hi, b_hi, dims) + (_dot_bf16(a_hi, b_lo, dims) + _dot_bf16(a_lo, b_hi, dims))
    a = a.astype(BF16)
    b_hi = b.astype(BF16)
    rest = b - b_hi.astype(F32)
    b_mid = rest.astype(BF16)
    b_lo = (rest - b_mid.astype(F32)).astype(BF16)
    return _dot_bf16(a, b_hi, dims) + (_dot_bf16(a, b_mid, dims) + _dot_bf16(a, b_lo, dims))


@functools.partial(jax.custom_vjp, nondiff_argnums=(2,))
def mm_nn(a, b, mode=BF16_ONCE):
    return _raw_dot(a, b, _NN, mode)


@functools.partial(jax.custom_vjp, nondiff_argnums=(2,))
def mm_nt(a, b, mode=BF16_ONCE):
    return _raw_dot(a, b, _NT, mode)


@functools.partial(jax.custom_vjp, nondiff_argnums=(2,))
def mm_tn(a, b, mode=BF16_ONCE):
    return _raw_dot(a, b, _TN, mode)


def _general(mode):
    return SPLIT if mode == EXACT_LHS else mode


mm_nn.defvjp(lambda a, b, p: (_raw_dot(a, b, _NN, p), (a, b)),
             lambda p, res, g: (mm_nt(g, res[1], _general(p)), mm_tn(res[0], g, p)))
mm_nt.defvjp(lambda a, b, p: (_raw_dot(a, b, _NT, p), (a, b)),
             lambda p, res, g: (mm_nn(g, res[1], p), mm_tn(g, res[0], p)))
mm_tn.defvjp(lambda a, b, p: (_raw_dot(a, b, _TN, p), (a, b)),
             lambda p, res, g: (mm_nt(res[1], g, _general(p)), mm_nn(res[0], g, _general(p))))


def _sigmoid(x):
    return jax.nn.sigmoid(x)


def _silu(x):
    return x * _sigmoid(x)


def _softplus(x):
    return jnp.maximum(x, 0.0) + jnp.log(1.0 + jnp.exp(-jnp.abs(x)))


def _l2n(t):
    return t * lax.rsqrt(jnp.sum(t * t, axis=-1, keepdims=True) + EPS)


def _rms_norm(x, w):
    return x * lax.rsqrt(jnp.mean(x * x, axis=-1, keepdims=True) + EPS) * w


def _gated_norm(o, gate, nw):
    o = o * lax.rsqrt(jnp.mean(o * o, axis=-1, keepdims=True) + EPS) * nw
    return o * _silu(gate)


def _heads(ref, piece):
    return jnp.stack([ref[:, piece * HW + DH * j:piece * HW + DH * (j + 1)] for j in range(HEADS)])


def _put_heads(ref, piece, value, accumulate=False):
    for j in range(HEADS):
        cols = slice(piece * HW + DH * j, piece * HW + DH * (j + 1))
        if accumulate:
            ref[:, cols] += value[j]
        else:
            ref[:, cols] = value[j].astype(ref.dtype)


def _diag_scores(q_i, k_i, f_i):
    nb, s, d = q_i.shape
    row = lax.broadcasted_iota(jnp.int32, (nb, s, d), 1)
    q4, f4 = q_i[:, :, None, :], f_i[:, :, None, :]
    kk = k_i
    rows_out = []
    for t in range(s):
        if t > 0:
            kk = jnp.where(row < t, kk * f4[:, t], kk)
        rows_out.append(jnp.sum(kk * q4[:, t], axis=-1)[:, None, :])
    s_ii = jnp.concatenate(rows_out, axis=1)
    tri = lax.broadcasted_iota(jnp.int32, (nb, s, s), 1) >= lax.broadcasted_iota(jnp.int32, (nb, s, s), 2)
    return jnp.where(tri, s_ii, 0.0)


def hg_chunk(hq, hf, hi, hg, l0, l1, nw, st):
    nb = hq.shape[0]
    m = jnp.maximum(l0, l1)
    e0 = jnp.exp(l0 - m)
    e1 = jnp.exp(l1 - m)
    lb = e0 / (e0 + e1)
    sig = _sigmoid(hf)
    q = _silu(hq)
    f = lb + (1.0 - lb) * sig
    log_f = jnp.log(f)
    k = (1.0 - lb) * _sigmoid(-hf)
    v = hi
    rows = lax.broadcasted_iota(jnp.int32, (nb, CHUNK, CHUNK), 1)
    cols = lax.broadcasted_iota(jnp.int32, (nb, CHUNK, CHUNK), 2)
    b = mm_nn((rows >= cols).astype(F32), log_f, EXACT_LHS)
    o_inter = mm_nt(q * jnp.exp(b), st)
    outs = []
    for i in range(CHUNK // SUB):
        lo = i * SUB
        b_i, q_i, k_i, v_i = b[:, lo:lo + SUB], q[:, lo:lo + SUB], k[:, lo:lo + SUB], v[:, lo:lo + SUB]
        o_i = mm_nn(_diag_scores(q_i, k_i, f[:, lo:lo + SUB]), v_i)
        if i > 0:
            b_ref = b[:, :, None, :][:, lo - 1]
            q_t = q_i * jnp.exp(b_i - b_ref)
            k_t = k[:, :lo] * jnp.exp(b_ref - b[:, :lo])
            o_i = o_i + mm_nn(mm_nt(q_t, k_t), v[:, :lo])
        outs.append(o_i)
    o = o_inter + jnp.concatenate(outs, axis=1)
    b_last = b[:, :, None, :][:, CHUNK - 1]
    st_new = st * jnp.exp(b_last) + mm_tn(v, k * jnp.exp(b_last - b))
    return _gated_norm(o, hg, nw), st_new


@jax.custom_vjp
def _unit_lower_inverse(a):
    n = a.shape[-1]
    eye = (lax.broadcasted_iota(jnp.int32, a.shape, 1) == lax.broadcasted_iota(jnp.int32, a.shape, 2)).astype(F32)
    x = eye - a
    p = mm_nn(a, a, SPLIT)
    x = x + mm_nn(x, p, SPLIT)
    power = 4
    while power < n:
        p = mm_nn(p, p, SPLIT)
        x = x + mm_nn(x, p, SPLIT)
        power *= 2
    return x


def _unit_lower_inverse_fwd(a):
    t = _unit_lower_inverse(a)
    return t, t


def _unit_lower_inverse_bwd(t, dt):
    return (-mm_tn(t, mm_nt(dt, t, SPLIT), SPLIT),)


_unit_lower_inverse.defvjp(_unit_lower_inverse_fwd, _unit_lower_inverse_bwd)


@jax.custom_vjp
def _kept_inverse(a, t):
    return t


_kept_inverse.defvjp(lambda a, t: (t, t),
                     lambda t, dt: (-mm_tn(t, mm_nt(dt, t, SPLIT), SPLIT), jnp.zeros_like(t)))


def gd_chunk(cq, ck, cv, z, ab, alog, dtb, nw, s, kept_t=None, keep_t=False):
    nb, c, _ = cq.shape
    lane = lax.broadcasted_iota(jnp.int32, (nb, 1, DH), 2)
    head = lax.broadcasted_iota(jnp.int32, (nb, 1, DH), 0)
    pick = lambda t, off: jnp.sum(jnp.where(lane == head + off, t[None], 0.0), axis=2, keepdims=True)
    a_raw, b_raw = pick(ab, 0), pick(ab, HEADS)
    beta = _sigmoid(b_raw)
    g = -jnp.exp(pick(alog, 0)) * _softplus(a_raw + pick(dtb, 0))
    q = _l2n(_silu(cq)) * (DH ** -0.5)
    k = _l2n(_silu(ck))
    v = _silu(cv)
    rows = lax.broadcasted_iota(jnp.int32, (nb, c, c), 1)
    cols = lax.broadcasted_iota(jnp.int32, (nb, c, c), 2)
    tril = (rows >= cols).astype(F32)
    gc_w = mm_nn(tril, jnp.broadcast_to(g, (nb, c, DH)), EXACT_LHS)
    g_sq = jnp.broadcast_to(g, (nb, c, c))
    gc_col = mm_nn(tril, g_sq, EXACT_LHS)
    gc_row = mm_nn(jnp.ones((nb, c, c), F32), g_sq * (rows <= cols).astype(F32), EXACT_LHS)
    rel = jnp.exp(jnp.minimum(gc_col - gc_row, 0.0))
    a = jnp.where(rows > cols, beta * mm_nt(k, k) * rel, 0.0)
    t_inv = _unit_lower_inverse(a) if kept_t is None else _kept_inverse(a, kept_t)
    e_gc = jnp.exp(gc_w)
    w = mm_nn(t_inv, beta * e_gc * k, SPLIT)
    u = mm_nn(t_inv, beta * v, SPLIT)
    v_new = u - mm_nn(w, s)
    attn = jnp.where(rows >= cols, mm_nt(q, k) * rel, 0.0)
    o = mm_nn(q * e_gc, s) + mm_nn(attn, v_new)
    g_last = jnp.sum(g, axis=1, keepdims=True)
    s_new = jnp.exp(g_last) * s + mm_tn(k * jnp.exp(g_last - gc_w), v_new)
    out = _gated_norm(o, z, nw)
    return (out, s_new, t_inv) if keep_t else (out, s_new)


def _gd_conv(x_ref, tail_ref, cw_ref, xe_ref, first):
    xe_ref[0:TAIL, :] = jnp.where(first, 0.0, tail_ref[:, 0:3 * HW])
    xe_ref[TAIL:TAIL + CHUNK, :] = x_ref[:, 0:3 * HW]
    y = cw_ref[pl.ds(0, 1), :] * xe_ref[pl.ds(TAIL - CONV_K + 1, CHUNK), :]
    for k in range(1, CONV_K):
        y = y + cw_ref[pl.ds(k, 1), :] * xe_ref[pl.ds(TAIL - CONV_K + 1 + k, CHUNK), :]
    return y


def _gd_stage(x_ref, tail_ref, xe_ref, first):
    xe_ref[0:TAIL, :] = jnp.where(first, 0.0, tail_ref[:, 0:3 * HW])
    xe_ref[TAIL:TAIL + CHUNK, :] = x_ref[:, 0:3 * HW]


def _conv_heads(y, piece):
    return jnp.stack([y[:, piece * HW + DH * j:piece * HW + DH * (j + 1)] for j in range(HEADS)])


def mixer_fwd(proj, proj_ab, l0, l1, hg_nw, cw, alog, dtb, gd_nw):
    lp = proj.shape[0]
    nc = lp // CHUNK

    def body(xh_ref, l0_ref, l1_ref, hnw_ref, xg_ref, tail_ref, ab_ref, cw_ref, alog_ref, dtb_ref, gnw_ref,
             oa_ref, sta_out_ref, ob_ref, stb_out_ref, tinv_ref, y_ref, sta_ref, stb_ref, xe_ref):
        c = pl.program_id(0)

        @pl.when(c == 0)
        def _():
            sta_ref[...] = jnp.zeros(sta_ref.shape, F32)
            stb_ref[...] = jnp.zeros(stb_ref.shape, F32)

        @pl.when(c < N_SKIP)
        def _():
            for ref in (oa_ref, sta_out_ref, ob_ref, stb_out_ref, tinv_ref, y_ref):
                ref[...] = jnp.zeros(ref.shape, ref.dtype)

        @pl.when(c >= N_SKIP)
        def _():
            stb = stb_ref[...]
            stb_out_ref[0] = stb
            y = _gd_conv(xg_ref, tail_ref, cw_ref, xe_ref, c == 0)
            y_ref[...] = y
            ob, stb_new, tinv_ref[0] = gd_chunk(
                _conv_heads(y, 0), _conv_heads(y, 1), _conv_heads(y, 2), _heads(xg_ref, 3),
                ab_ref[...], alog_ref[...], dtb_ref[...], gnw_ref[...], stb, keep_t=True)
            _put_heads(ob_ref, 0, ob)
            stb_ref[...] = stb_new
            sta = sta_ref[...]
            sta_out_ref[0] = sta
            oa, sta_new = hg_chunk(_heads(xh_ref, 0), _heads(xh_ref, 1), _heads(xh_ref, 2), _heads(xh_ref, 3),
                                   _heads(l0_ref, 0), _heads(l1_ref, 0), hnw_ref[...], sta)
            _put_heads(oa_ref, 0, oa)
            sta_ref[...] = sta_new

    vec = pl.BlockSpec((1, HW), lambda c: (0, 0))
    one = pl.BlockSpec((1, DH), lambda c: (0, 0))
    st_spec = pl.BlockSpec((1, HEADS, DH, DH), lambda c: (c, 0, 0, 0))
    o_spec = pl.BlockSpec((CHUNK, HW), lambda c: (c, 0))
    o_shape = jax.ShapeDtypeStruct((lp, HW), BF16)
    st_shape = jax.ShapeDtypeStruct((nc, HEADS, DH, DH), F32)
    return pl.pallas_call(
        body, name="mixer_fwd", grid=(nc,),
        in_specs=[pl.BlockSpec((CHUNK, 4 * HW), lambda c: (c, 0)), vec, vec, one,
                  pl.BlockSpec((CHUNK, 4 * HW), lambda c: (c, 1)),
                  pl.BlockSpec((TAIL, 4 * HW), lambda c: (jnp.maximum(c * (CHUNK // TAIL) - 1, 0), 1)),
                  pl.BlockSpec((CHUNK, DH), lambda c: (c, 0)),
                  pl.BlockSpec((CONV_K, 3 * HW), lambda c: (0, 0)), one, one, one],
        out_specs=[o_spec, st_spec, o_spec, st_spec,
                   pl.BlockSpec((1, HEADS, CHUNK, CHUNK), lambda c: (c, 0, 0, 0)),
                   pl.BlockSpec((CHUNK, 3 * HW), lambda c: (c, 0))],
        out_shape=[o_shape, st_shape, o_shape, st_shape, jax.ShapeDtypeStruct((nc, HEADS, CHUNK, CHUNK), F32),
                   jax.ShapeDtypeStruct((lp, 3 * HW), F32)],
        scratch_shapes=[pltpu.VMEM((HEADS, DH, DH), F32), pltpu.VMEM((HEADS, DH, DH), F32),
                        pltpu.VMEM((TAIL + CHUNK, 3 * HW), F32)],
        compiler_params=_params("arbitrary"),
    )(proj, l0, l1, hg_nw, proj, proj, proj_ab, cw, alog, dtb, gd_nw)


def mixer_bwd(proj, proj_ab, l0, l1, hg_nw, cw, alog, dtb, gd_nw, st_a, st_b, t_inv, conv_y, do_a, do_b, dproj, exch):
    lp = proj.shape[0]
    nc = lp // CHUNK

    def body(xh_ref, l0_ref, l1_ref, hnw_ref, xg_ref, tail_ref, ab_ref, cw_ref, alog_ref, dtb_ref, gnw_ref,
             sta_in_ref, stb_in_ref, tinv_ref, y_ref, doa_ref, dob_ref, _,
             dx_ref, dl0_ref, dl1_ref, dhnw_ref, dab_ref, dcw_ref, dalog_ref, ddtb_ref, dgnw_ref,
             dsta_ref, dstb_ref, xe_ref, dye_ref, head_ref):
        c = pl.program_id(0)
        cc = nc - 1 - c

        @pl.when(c == 0)
        def _():
            for ref in (dl0_ref, dl1_ref, dhnw_ref, dcw_ref, dalog_ref, ddtb_ref, dgnw_ref, dsta_ref, dstb_ref,
                        head_ref):
                ref[...] = jnp.zeros(ref.shape, F32)

        @pl.when(cc < N_SKIP)
        def _():
            dx_ref[...] = jnp.zeros(dx_ref.shape, dx_ref.dtype)
            dab_ref[...] = jnp.zeros(dab_ref.shape, F32)

        @pl.when(cc >= N_SKIP)
        def _():
            _gd_stage(xg_ref, tail_ref, xe_ref, cc == 0)
            y = y_ref[...]
            _, gd_vjp = jax.vjp(functools.partial(gd_chunk, kept_t=tinv_ref[0]),
                                _conv_heads(y, 0), _conv_heads(y, 1), _conv_heads(y, 2), _heads(xg_ref, 3),
                                ab_ref[...], alog_ref[...], dtb_ref[...], gnw_ref[...], stb_in_ref[0])
            dcq, dck, dcv, dz, dab, dalog, ddtb, dgnw, dstb = gd_vjp((_heads(dob_ref, 0), dstb_ref[...]))
            _, hg_vjp = jax.vjp(hg_chunk, _heads(xh_ref, 0), _heads(xh_ref, 1), _heads(xh_ref, 2), _heads(xh_ref, 3),
                                _heads(l0_ref, 0), _heads(l1_ref, 0), hnw_ref[...], sta_in_ref[0])
            hg_grads = hg_vjp((_heads(doa_ref, 0), dsta_ref[...]))
            dstb_ref[...] = dstb
            dab_ref[...] = dab
            dalog_ref[...] += dalog
            ddtb_ref[...] += ddtb
            dgnw_ref[...] += dgnw
            _put_heads(dx_ref, 4 + 3, dz)
            for i, t in enumerate((dcq, dck, dcv)):
                for j in range(HEADS):
                    dye_ref[0:CHUNK, i * HW + DH * j:i * HW + DH * (j + 1)] = t[j]
            dye_ref[CHUNK:CHUNK + TAIL, :] = head_ref[...]
            head_ref[...] = dye_ref[0:TAIL, :]
            dy = dye_ref[0:CHUNK, :]
            dx = None
            for k in range(CONV_K):
                term = cw_ref[pl.ds(k, 1), :] * dye_ref[pl.ds(CONV_K - 1 - k, CHUNK), :]
                dx = term if dx is None else dx + term
                dcw_ref[pl.ds(k, 1), :] += jnp.sum(dy * xe_ref[pl.ds(TAIL - CONV_K + 1 + k, CHUNK), :],
                                                   axis=0, keepdims=True)
            dx_ref[:, 4 * HW:7 * HW] = dx.astype(dx_ref.dtype)
            for i in range(4):
                _put_heads(dx_ref, i, hg_grads[i])
            _put_heads(dl0_ref, 0, hg_grads[4], accumulate=True)
            _put_heads(dl1_ref, 0, hg_grads[5], accumulate=True)
            dhnw_ref[...] += hg_grads[6]
            dsta_ref[...] = hg_grads[7]

    rc = lambda c: nc - 1 - c
    vec = pl.BlockSpec((1, HW), lambda c: (0, 0))
    one = pl.BlockSpec((1, DH), lambda c: (0, 0))
    one_shape = jax.ShapeDtypeStruct((1, DH), F32)
    vec_shape = jax.ShapeDtypeStruct((1, HW), F32)
    st_spec = pl.BlockSpec((1, HEADS, DH, DH), lambda c: (rc(c), 0, 0, 0))
    do_spec = pl.BlockSpec((CHUNK, HW), lambda c: (rc(c), 0))
    n_in, n_out = 18, 9
    outs = pl.pallas_call(
        host_exchange(body, n_in, n_out, exch, lambda: pl.program_id(0) == 0, lambda: pl.program_id(0) == nc - 1),
        name="mixer_bwd", grid=(nc,), input_output_aliases={n_in - 1: 0}, compiler_params=_params("arbitrary"),
        **_hosted(exch,
                  [pl.BlockSpec((CHUNK, 4 * HW), lambda c: (rc(c), 0)), vec, vec, one,
                   pl.BlockSpec((CHUNK, 4 * HW), lambda c: (rc(c), 1)),
                   pl.BlockSpec((TAIL, 4 * HW), lambda c: (jnp.maximum(rc(c) * (CHUNK // TAIL) - 1, 0), 1)),
                   pl.BlockSpec((CHUNK, DH), lambda c: (rc(c), 0)),
                   pl.BlockSpec((CONV_K, 3 * HW), lambda c: (0, 0)), one, one, one,
                   st_spec, st_spec, pl.BlockSpec((1, HEADS, CHUNK, CHUNK), lambda c: (rc(c), 0, 0, 0)),
                   pl.BlockSpec((CHUNK, 3 * HW), lambda c: (rc(c), 0)),
                   do_spec, do_spec, pl.BlockSpec(memory_space=pl.ANY)],
                  [pl.BlockSpec((CHUNK, 8 * HW), lambda c: (rc(c), 0)), vec, vec, one,
                   pl.BlockSpec((CHUNK, DH), lambda c: (rc(c), 0)),
                   pl.BlockSpec((CONV_K, 3 * HW), lambda c: (0, 0)), one, one, one],
                  [jax.ShapeDtypeStruct((lp, W_MAIN), BF16), vec_shape, vec_shape, one_shape,
                   jax.ShapeDtypeStruct((lp, DH), F32), jax.ShapeDtypeStruct((CONV_K, 3 * HW), F32),
                   one_shape, one_shape, one_shape],
                  [pltpu.VMEM((HEADS, DH, DH), F32), pltpu.VMEM((HEADS, DH, DH), F32),
                   pltpu.VMEM((TAIL + CHUNK, 3 * HW), F32), pltpu.VMEM((CHUNK + TAIL, 3 * HW), F32),
                   pltpu.VMEM((TAIL, 3 * HW), F32)]),
    )(proj, l0, l1, hg_nw, proj, proj, proj_ab, cw, alog, dtb, gd_nw, st_a, st_b, t_inv, conv_y, do_a, do_b, dproj,
      *exch.arrays)
    return outs[:n_out], outs[n_out:]


def matmul(name, a, b, mode, m, n, k, out_dtype, tm=512, tn=1024, tk=None, a_off=(0, 0), b_off=(0, 0), exch=None,
           rows_outer=False):
    tm, tn = min(tm, m), min(tn, n)
    tk = k if tk is None else min(tk, k)
    assert m % tm == 0 and n % tn == 0 and k % tk == 0, (name, m, n, k, tm, tn, tk)
    gi, gj, gk = m // tm, n // tn, k // tk
    grid = (gi, gj, gk) if rows_outer else (gj, gi, gk)

    def at(index):
        return (lambda i, j, kk: index(j, i, kk)) if rows_outer else index

    if mode == "nn":
        a_spec = pl.BlockSpec((tm, tk), at(lambda j, i, kk: (i + a_off[0], kk + a_off[1])))
        b_spec = pl.BlockSpec((tk, tn), at(lambda j, i, kk: (kk + b_off[0], j + b_off[1])))
        dims = _NN
    elif mode == "nt":
        a_spec = pl.BlockSpec((tm, tk), at(lambda j, i, kk: (i + a_off[0], kk + a_off[1])))
        b_spec = pl.BlockSpec((tn, tk), at(lambda j, i, kk: (j + b_off[0], kk + b_off[1])))
        dims = _NT
    else:
        a_spec = pl.BlockSpec((tk, tm), at(lambda j, i, kk: (kk + a_off[0], i + a_off[1])))
        b_spec = pl.BlockSpec((tk, tn), at(lambda j, i, kk: (kk + b_off[0], j + b_off[1])))
        dims = _TN

    def body(a_ref, b_ref, o_ref, *acc):
        d = lax.dot_general(a_ref[...].astype(BF16), b_ref[...].astype(BF16), (dims, ((), ())),
                            preferred_element_type=F32)
        if gk == 1:
            o_ref[...] = d.astype(o_ref.dtype)
        else:
            acc_ref, = acc
            kk = pl.program_id(2)

            @pl.when(kk == 0)
            def _():
                acc_ref[...] = d

            @pl.when(kk > 0)
            def _():
                acc_ref[...] += d

            @pl.when(kk == gk - 1)
            def _():
                o_ref[...] = acc_ref[...].astype(o_ref.dtype)

    o_spec = pl.BlockSpec((tm, tn), at(lambda j, i, kk: (i, j)))
    o_shape = jax.ShapeDtypeStruct((m, n), out_dtype)
    scratch = [] if gk == 1 else [pltpu.VMEM((tm, tn), F32)]
    if exch is None:
        return pl.pallas_call(
            body, name=name, grid=grid, in_specs=[a_spec, b_spec], out_specs=o_spec, out_shape=o_shape,
            scratch_shapes=scratch, compiler_params=_params("parallel", "parallel", "arbitrary"),
        )(a, b)
    step = lambda: (pl.program_id(0), pl.program_id(1), pl.program_id(2))
    first = lambda: jnp.logical_and(jnp.logical_and(step()[0] == 0, step()[1] == 0), step()[2] == 0)
    last = lambda: jnp.logical_and(jnp.logical_and(step()[0] == grid[0] - 1, step()[1] == grid[1] - 1),
                                   step()[2] == gk - 1)
    outs = pl.pallas_call(
        host_exchange(body, 2, 1, exch, first, last), name=name, grid=grid,
        compiler_params=_params("arbitrary", "arbitrary", "arbitrary"),
        **_hosted(exch, [a_spec, b_spec], [o_spec], [o_shape], scratch),
    )(a, b, *exch.arrays)
    return outs[0], outs[1:]


def _swiglu(gu):
    return _silu(gu[:, :FF_BLK]) * gu[:, FF_BLK:]


def ffn_in_fused(xn2, w_gu, tm=512):
    l = xn2.shape[0]
    tn = 2 * FF_BLK

    def body(a_ref, b_ref, gu_ref, act_ref):
        gu = lax.dot_general(a_ref[...], b_ref[...], (_NN, ((), ())), preferred_element_type=F32)
        gu_ref[...] = gu
        act_ref[...] = _swiglu(gu).astype(act_ref.dtype)

    return pl.pallas_call(
        body, name="ffn_in", grid=(2 * D_FF // tn, l // tm),
        in_specs=[pl.BlockSpec((tm, D), lambda j, i: (i, 0)), pl.BlockSpec((D, tn), lambda j, i: (0, j))],
        out_specs=[pl.BlockSpec((tm, tn), lambda j, i: (i, j)), pl.BlockSpec((tm, FF_BLK), lambda j, i: (i, j))],
        out_shape=[jax.ShapeDtypeStruct((l, 2 * D_FF), F32), jax.ShapeDtypeStruct((l, D_FF), BF16)],
        compiler_params=_params("parallel", "parallel"),
    )(xn2, w_gu)


def d_act_fused(dh2, w_fo, gu, tm=512):
    l = dh2.shape[0]

    def body(a_ref, b_ref, gu_ref, o_ref):
        da = lax.dot_general(a_ref[...], b_ref[...], (_NT, ((), ())), preferred_element_type=F32)
        _, vjp = jax.vjp(_swiglu, gu_ref[...])
        dgu, = vjp(da)
        o_ref[...] = dgu.astype(o_ref.dtype)

    return pl.pallas_call(
        body, name="d_act", grid=(D_FF // FF_BLK, l // tm),
        in_specs=[pl.BlockSpec((tm, D), lambda j, i: (i, 0)), pl.BlockSpec((FF_BLK, D), lambda j, i: (j, 0)),
                  pl.BlockSpec((tm, 2 * FF_BLK), lambda j, i: (i, j))],
        out_specs=pl.BlockSpec((tm, 2 * FF_BLK), lambda j, i: (i, j)),
        out_shape=jax.ShapeDtypeStruct((l, 2 * D_FF), BF16),
        compiler_params=_params("parallel", "parallel"),
    )(dh2, w_fo, gu)


TR = 256
N_PRE = PRE // TR


def _row(width, off=0, col=0):
    return pl.BlockSpec((TR, width), lambda i: (i + off, col))


def _pre_spec():
    return pl.BlockSpec((TR, D), lambda i: (jnp.minimum(i, N_PRE - 1), 0))


def _seq_spec(width=D, col=0):
    return pl.BlockSpec((TR, width), lambda i: (jnp.maximum(i - N_PRE, 0), col))


def _vec_spec(width=D):
    return pl.BlockSpec((1, width), lambda i: (0, 0))


def norm1_fwd(pre, x, w):
    lp = PRE + x.shape[0]

    def body(pre_ref, x_ref, w_ref, o_ref):
        h = jnp.where(pl.program_id(0) < N_PRE, pre_ref[...], x_ref[...])
        o_ref[...] = _rms_norm(h, w_ref[...]).astype(o_ref.dtype)

    return pl.pallas_call(
        body, name="norm1_fwd", grid=(lp // TR,),
        in_specs=[_pre_spec(), _seq_spec(), _vec_spec()],
        out_specs=_row(D), out_shape=jax.ShapeDtypeStruct((lp, D), BF16),
        compiler_params=_params("parallel"),
    )(pre, x, w)


def norm1_bwd(pre, x, w, dxn, dxn_ab, dh1):
    l = x.shape[0]
    lp = PRE + l

    def body(pre_ref, x_ref, w_ref, dxn_ref, dxn_ab_ref, dh1_ref, dx_ref, dpre_ref, dw_ref):
        i = pl.program_id(0)
        h = jnp.where(i < N_PRE, pre_ref[...], x_ref[...])
        _, vjp = jax.vjp(_rms_norm, h, w_ref[...])
        dh, dw = vjp(dxn_ref[...] + dxn_ab_ref[...])

        @pl.when(i == 0)
        def _():
            dw_ref[...] = dw

        @pl.when(i > 0)
        def _():
            dw_ref[...] += dw

        @pl.when(i < N_PRE)
        def _():
            dpre_ref[...] = dh

        @pl.when(i >= N_PRE)
        def _():
            dx_ref[...] = dh + dh1_ref[...]

    return pl.pallas_call(
        body, name="norm1_bwd", grid=(lp // TR,),
        in_specs=[_pre_spec(), _seq_spec(), _vec_spec(), _row(D), _row(D), _seq_spec()],
        out_specs=[_seq_spec(), _pre_spec(), _vec_spec()],
        out_shape=[jax.ShapeDtypeStruct((l, D), F32), jax.ShapeDtypeStruct((PRE, D), F32),
                   jax.ShapeDtypeStruct((1, D), F32)],
        compiler_params=_params("arbitrary"),
    )(pre, x, w, dxn, dxn_ab, dh1)


def _merge(gates, ya, yb):
    return _sigmoid(gates[:, :D]) * ya + _sigmoid(gates[:, D:]) * yb


def merge_bwd(dh1, wo, proj, ya, yb, exch):
    l = ya.shape[0]
    lp = PRE + l

    def body(a_ref, b_ref, g_ref, ya_ref, yb_ref, dg_ref, dya_ref, dyb_ref):
        i = pl.program_id(0)

        @pl.when(i < N_PRE)
        def _():
            dg_ref[...] = jnp.zeros(dg_ref.shape, dg_ref.dtype)
            dya_ref[...] = jnp.zeros(dya_ref.shape, dya_ref.dtype)
            dyb_ref[...] = jnp.zeros(dyb_ref.shape, dyb_ref.dtype)

        @pl.when(i >= N_PRE)
        def _():
            dm = lax.dot_general(a_ref[...], b_ref[...], (_NT, ((), ())), preferred_element_type=F32)
            _, vjp = jax.vjp(_merge, g_ref[...], ya_ref[...], yb_ref[...])
            dg, dya, dyb = vjp(dm)
            dg_ref[...] = dg.astype(dg_ref.dtype)
            dya_ref[...] = dya.astype(dya_ref.dtype)
            dyb_ref[...] = dyb.astype(dyb_ref.dtype)

    last = lp // TR - 1
    outs = pl.pallas_call(
        host_exchange(body, 5, 3, exch, lambda: pl.program_id(0) == 0, lambda: pl.program_id(0) == last),
        name="merge_bwd", grid=(lp // TR,), compiler_params=_params("arbitrary"),
        **_hosted(exch,
                  [_seq_spec(), pl.BlockSpec((D, D), lambda i: (0, 0)),
                   pl.BlockSpec((TR, 2 * D), lambda i: (jnp.maximum(i, N_PRE), 2)), _seq_spec(), _seq_spec()],
                  [_row(2 * D, 0, 2), _row(D), _row(D)],
                  [jax.ShapeDtypeStruct((lp, W_MAIN), BF16), jax.ShapeDtypeStruct((lp, D), BF16),
                   jax.ShapeDtypeStruct((lp, D), BF16)], []),
    )(dh1, wo, proj, ya, yb, *exch.arrays)
    return outs[:3], outs[3:]


def merge_mix_out(proj, ya, yb, wo, x, w):
    l = x.shape[0]

    def body(g_ref, ya_ref, yb_ref, b_ref, x_ref, w_ref, m_ref, h_ref, o_ref):
        merged = _merge(g_ref[...], ya_ref[...], yb_ref[...]).astype(BF16)
        m_ref[...] = merged
        h = x_ref[...] + lax.dot_general(merged, b_ref[...], (_NN, ((), ())), preferred_element_type=F32)
        h_ref[...] = h
        o_ref[...] = _rms_norm(h, w_ref[...]).astype(o_ref.dtype)

    return pl.pallas_call(
        body, name="mix_out", grid=(l // TR,),
        in_specs=[_row(2 * D, N_PRE, 2), _row(D), _row(D), pl.BlockSpec((D, D), lambda i: (0, 0)), _row(D),
                  _vec_spec()],
        out_specs=[_row(D), _row(D), _row(D)],
        out_shape=[jax.ShapeDtypeStruct((l, D), BF16), jax.ShapeDtypeStruct((l, D), F32),
                   jax.ShapeDtypeStruct((l, D), BF16)],
        compiler_params=_params("parallel"),
    )(proj, ya, yb, wo, x, w)


def norm2_bwd(h1, w, dxn2, dh2):
    l = h1.shape[0]

    def body(h_ref, w_ref, dxn_ref, dh2_ref, dh1_ref, dh1b_ref, dw_ref):
        i = pl.program_id(0)
        _, vjp = jax.vjp(_rms_norm, h_ref[...], w_ref[...])
        dh, dw = vjp(dxn_ref[...])
        dh1 = dh + dh2_ref[...]
        dh1_ref[...] = dh1
        dh1b_ref[...] = dh1.astype(BF16)

        @pl.when(i == 0)
        def _():
            dw_ref[...] = dw

        @pl.when(i > 0)
        def _():
            dw_ref[...] += dw

    return pl.pallas_call(
        body, name="norm2_bwd", grid=(l // TR,),
        in_specs=[_row(D), _vec_spec(), _row(D), _row(D)],
        out_specs=[_row(D), _row(D), _vec_spec()],
        out_shape=[jax.ShapeDtypeStruct((l, D), F32), jax.ShapeDtypeStruct((l, D), BF16),
                   jax.ShapeDtypeStruct((1, D), F32)],
        compiler_params=_params("arbitrary"),
    )(h1, w, dxn2, dh2)


def _loss_rows(h1, f, w, tgt):
    y = _rms_norm(h1 + f, w)
    err = (y - tgt) * (y - tgt)
    return 0.5 * jnp.sum(jnp.mean(err, axis=-1, keepdims=True), axis=0, keepdims=True)


def loss_head(h1, f, w, tgt):
    l = h1.shape[0]

    def body(h_ref, f_ref, w_ref, t_ref, loss_ref, dh_ref, dhb_ref, dw_ref):
        i = pl.program_id(0)
        loss, vjp = jax.vjp(_loss_rows, h_ref[...], f_ref[...], w_ref[...], t_ref[...])
        dh, _, dw, _ = vjp(jnp.ones((1, 1), F32))
        dh_ref[...] = dh
        dhb_ref[...] = dh.astype(BF16)
        loss = jnp.broadcast_to(loss, (1, LANES))

        @pl.when(i == 0)
        def _():
            dw_ref[...] = dw
            loss_ref[...] = loss

        @pl.when(i > 0)
        def _():
            dw_ref[...] += dw
            loss_ref[...] += loss

    return pl.pallas_call(
        body, name="loss_head", grid=(l // TR,),
        in_specs=[_row(D), _row(D), _vec_spec(), _row(D)],
        out_specs=[_vec_spec(LANES), _row(D), _row(D), _vec_spec()],
        out_shape=[jax.ShapeDtypeStruct((1, LANES), F32), jax.ShapeDtypeStruct((l, D), F32),
                   jax.ShapeDtypeStruct((l, D), BF16), jax.ShapeDtypeStruct((1, D), F32)],
        compiler_params=_params("arbitrary"),
    )(h1, f, w, tgt)


def _adamw(w, g, m, v):
    m = ADAM_B1 * m + (1.0 - ADAM_B1) * g
    v = ADAM_B2 * v + (1.0 - ADAM_B2) * (g * g)
    m_hat = m / (1.0 - ADAM_B1 ** ADAM_STEP)
    v_hat = v / (1.0 - ADAM_B2 ** ADAM_STEP)
    delta = -ADAM_LR * (m_hat / (jnp.sqrt(v_hat) + ADAM_EPS) + ADAM_WD * w)
    return delta, m, v


def adamw_shard(name, w, m, v, sums, recv, tile):
    r, c = w.shape
    tr, tc = tile
    assert r % tr == 0 and c % tc == 0

    def body(chip_ref, w_ref, m_ref, v_ref, own_ref, r0_ref, r1_ref, r2_ref, g_ref, d_ref, nm_ref, nv_ref):
        g = own_ref[...].astype(F32) + r0_ref[...].astype(F32)
        g = g + r1_ref[...].astype(F32)
        g = g + r2_ref[...].astype(F32)
        d, nm, nv = _adamw(w_ref[...], g, m_ref[...], v_ref[...])
        g_ref[...] = g
        d_ref[...] = d
        nm_ref[...] = nm
        nv_ref[...] = nv

    blk = pl.BlockSpec((tr, tc), lambda i, j, chip: (i, j))
    part = lambda s: pl.BlockSpec((None, tr, tc), lambda i, j, chip: (s, i, j))
    own = pl.BlockSpec((None, tr, tc), lambda i, j, chip: (chip[0], i, j))
    shape = jax.ShapeDtypeStruct((r, c), F32)
    my_chip = (2 * lax.axis_index("x") + lax.axis_index("y")).astype(jnp.int32).reshape(1)
    return pl.pallas_call(
        body, name=name, out_shape=[shape, shape, shape, shape],
        grid_spec=pltpu.PrefetchScalarGridSpec(num_scalar_prefetch=1, grid=(r // tr, c // tc),
                                               in_specs=[blk, blk, blk, own, part(0), part(1), part(2)],
                                               out_specs=[blk, blk, blk, blk]),
        compiler_params=_params("parallel", "parallel"),
    )(my_chip, w, m, v, sums, recv, recv, recv)


def adamw_small(w, g, m, v):
    def body(w_ref, g_ref, m_ref, v_ref, d_ref, nm_ref, nv_ref):
        d, nm, nv = _adamw(w_ref[...], g_ref[...], m_ref[...], v_ref[...])
        d_ref[...] = d
        nm_ref[...] = nm
        nv_ref[...] = nv

    shape = jax.ShapeDtypeStruct(w.shape, F32)
    return pl.pallas_call(body, name="adamw_small", out_shape=[shape, shape, shape])(w, g, m, v)


def pair_add(name, a, b, tile):
    q, r, c = b.shape
    tr, tc = tile
    assert r % tr == 0 and c % tc == 0

    def body(core_ref, a_ref, b_ref, o_ref):
        o_ref[...] = (a_ref[...].astype(F32) + b_ref[...].astype(F32)).astype(o_ref.dtype)

    blk = pl.BlockSpec((None, tr, tc), lambda s, i, j, core: (s, i, j))
    mine = pl.BlockSpec((None, None, tr, tc), lambda s, i, j, core: (s, core[0], i, j))
    return pl.pallas_call(
        body, name=name, out_shape=jax.ShapeDtypeStruct((q, r, c), BF16),
        grid_spec=pltpu.PrefetchScalarGridSpec(num_scalar_prefetch=1, grid=(q, r // tr, c // tc),
                                               in_specs=[mine, blk], out_specs=blk),
        compiler_params=_params("parallel", "parallel", "parallel"),
    )(lax.axis_index("c").astype(jnp.int32).reshape(1), a, b)


def _w_in_pieces():
    c = W_IN // N_DEV
    ranges = ((0, 8 * HW, 0, 0), (8 * HW, 8 * HW + 2 * HEADS, 1, 8 * HW), (8 * HW + 2 * HEADS, W_IN, 0, 2 * HEADS))
    out = []
    for d in range(N_DEV):
        for lo, hi, target, shift in ranges:
            s, e = max(lo, c * d), min(hi, c * (d + 1))
            if s < e:
                out.append((d, s - c * d, e - s, target, s - shift))
    return out


def _ffn_in_pieces():
    c = 2 * D_FF // N_DEV
    out = []
    for d in range(N_DEV):
        s = c * d
        while s < c * (d + 1):
            e = min((s // FF_BLK + 1) * FF_BLK, c * (d + 1))
            half, blk = divmod(s // FF_BLK, D_FF // FF_BLK)
            out.append((d, s - c * d, e - s, 0, (2 * blk + half) * FF_BLK + s % FF_BLK))
            s = e
    return out


def _plain_pieces(c):
    return [(d, 0, c, 0, c * d) for d in range(N_DEV)]


def shards_to_cols(name, g, widths, pieces, tr):
    _, r, c = g.shape
    covered = [sum(p[2] for p in pieces if p[3] == t) for t in range(len(widths))]

    def body(g_ref, *outs):
        for t, o in enumerate(outs):
            if covered[t] < widths[t]:
                o[...] = jnp.zeros(o.shape, o.dtype)
        for d, s, w, t, ts in pieces:
            outs[t][:, ts:ts + w] = g_ref[d, :, s:s + w]

    return pl.pallas_call(
        body, name=name, grid=(r // tr,),
        in_specs=[pl.BlockSpec((N_DEV, tr, c), lambda i: (0, i, 0))],
        out_specs=[pl.BlockSpec((tr, w), lambda i: (i, 0)) for w in widths],
        out_shape=[jax.ShapeDtypeStruct((r, w), g.dtype) for w in widths],
        compiler_params=_params("parallel"),
    )(g)


def cols_to_shards(name, srcs, c, pieces, tr):
    r = srcs[0].shape[0]

    def body(*refs):
        o = refs[-1]
        for d, s, w, t, ts in pieces:
            o[d, :, s:s + w] = refs[t][:, ts:ts + w]

    return pl.pallas_call(
        body, name=name, grid=(r // tr,),
        in_specs=[pl.BlockSpec((tr, t.shape[1]), lambda i: (i, 0)) for t in srcs],
        out_specs=pl.BlockSpec((N_DEV, tr, c), lambda i: (0, i, 0)),
        out_shape=jax.ShapeDtypeStruct((N_DEV, r, c), srcs[0].dtype),
        compiler_params=_params("parallel"),
    )(*srcs)


def rows_to_shards(name, srcs, c, pieces, tc):
    r = srcs[0].shape[1]

    def body(*refs):
        o = refs[-1]
        for d, s, w, t, ts in pieces:
            o[d, s:s + w, :] = refs[t][ts:ts + w, :]

    return pl.pallas_call(
        body, name=name, grid=(r // tc,),
        in_specs=[pl.BlockSpec((t.shape[0], tc), lambda i: (0, i)) for t in srcs],
        out_specs=pl.BlockSpec((N_DEV, c, tc), lambda i: (0, 0, i)),
        out_shape=jax.ShapeDtypeStruct((N_DEV, c, r), srcs[0].dtype),
        compiler_params=_params("parallel"),
    )(*srcs)


def _place():
    return lax.axis_index("x"), lax.axis_index("y"), lax.axis_index("c")


def _dev(px, py, pc):
    return 4 * px + 2 * py + pc


class Exchange:
    def __init__(self, arrays, out_shapes, sems, start, finish):
        self.arrays, self.out_shapes, self.sems, self.start, self.finish = arrays, out_shapes, sems, start, finish


def gather_exchange(arrays):
    n = len(arrays)

    def plan(ins, outs, sems):
        send_sems, recv_sems, local_sems = sems
        x, y, c = _place()
        me, sibling = (x, y, c), (x, y, 1 - c)
        chips = [(1 - x, y), (x, 1 - y), (1 - x, 1 - y)]

        def copy(a, k, block, to, src=None):
            dst = outs[a].at[_dev(*block)]
            return pltpu.make_async_remote_copy(
                src_ref=dst if src is None else src, dst_ref=dst, send_sem=send_sems.at[a, k],
                recv_sem=recv_sems.at[a, k], device_id=to, device_id_type=MESH)

        mine = [pltpu.make_async_copy(ins[a], outs[a].at[_dev(*me)], local_sems.at[a]) for a in range(n)]
        first = []
        for a in range(n):
            first.append(copy(a, 0, me, sibling, src=ins[a]))
            first += [copy(a, 1 + j, me, (*chip, c), src=ins[a]) for j, chip in enumerate(chips)]
        return me, sibling, chips, c, copy, mine, first

    def start(ins, outs, sems):
        *_, mine, first = plan(ins, outs, sems)
        for cp in mine + first:
            cp.start()

    def finish(ins, outs, sems):
        me, sibling, chips, c, copy, mine, first = plan(ins, outs, sems)
        passed = []
        for j, chip in enumerate(chips):
            for a in range(n):
                copy(a, 1 + j, (*chip, c), me).wait_recv()
                fwd = copy(a, 4 + j, (*chip, c), sibling)
                fwd.start()
                passed.append(fwd)
        for a in range(n):
            copy(a, 0, sibling, me).wait_recv()
        for j, chip in enumerate(chips):
            for a in range(n):
                copy(a, 4 + j, (*chip, 1 - c), me).wait_recv()
        for cp in first + passed:
            cp.wait_send()
        for cp in mine:
            cp.wait()

    return Exchange(arrays, [jax.ShapeDtypeStruct((N_DEV,) + t.shape, t.dtype) for t in arrays],
                    [pltpu.SemaphoreType.DMA((n, 7)), pltpu.SemaphoreType.DMA((n, 7)), pltpu.SemaphoreType.DMA((n,))],
                    start, finish)


def core_exchange(arrays):
    n = len(arrays)

    def copies(ins, outs, sems):
        send_sems, recv_sems = sems
        x, y, c = _place()
        return [pltpu.make_async_remote_copy(
            src_ref=ins[a].at[q, 1 - c], dst_ref=outs[a].at[q], send_sem=send_sems.at[a, q],
            recv_sem=recv_sems.at[a, q], device_id=(x, y, 1 - c), device_id_type=MESH)
            for a in range(n) for q in range(4)]

    def start(ins, outs, sems):
        for cp in copies(ins, outs, sems):
            cp.start()

    def finish(ins, outs, sems):
        for cp in copies(ins, outs, sems):
            cp.wait()

    return Exchange(arrays, [jax.ShapeDtypeStruct((4,) + t.shape[2:], t.dtype) for t in arrays],
                    [pltpu.SemaphoreType.DMA((n, 4)), pltpu.SemaphoreType.DMA((n, 4))], start, finish)


def chips_exchange(arrays):
    n = len(arrays)

    def copies(ins, outs, sems):
        send_sems, recv_sems = sems
        x, y, c = _place()
        chips = [(1 - x, y), (x, 1 - y), (1 - x, 1 - y)]
        return [pltpu.make_async_remote_copy(
            src_ref=ins[a].at[2 * qx + qy], dst_ref=outs[a].at[j], send_sem=send_sems.at[a, j],
            recv_sem=recv_sems.at[a, j], device_id=(qx, qy, c), device_id_type=MESH)
            for a in range(n) for j, (qx, qy) in enumerate(chips)]

    def start(ins, outs, sems):
        for cp in copies(ins, outs, sems):
            cp.start()

    def finish(ins, outs, sems):
        for cp in copies(ins, outs, sems):
            cp.wait()

    return Exchange(arrays, [jax.ShapeDtypeStruct((3,) + t.shape[1:], t.dtype) for t in arrays],
                    [pltpu.SemaphoreType.DMA((n, 3)), pltpu.SemaphoreType.DMA((n, 3))], start, finish)


def run_exchange(name, exch):
    n_in, n_out = len(exch.arrays), len(exch.out_shapes)
    any_spec = pl.BlockSpec(memory_space=pl.ANY)

    def body(*refs):
        ins, outs, sems = refs[:n_in], refs[n_in:n_in + n_out], refs[n_in + n_out:]
        exch.start(ins, outs, sems)
        exch.finish(ins, outs, sems)

    return pl.pallas_call(
        body, name=name, in_specs=[any_spec] * n_in, out_specs=[any_spec] * n_out, out_shape=exch.out_shapes,
        scratch_shapes=exch.sems,
    )(*exch.arrays)


def host_exchange(body, n_in, n_out, exch, first, last):
    ne_in, ne_out, ne_sem = len(exch.arrays), len(exch.out_shapes), len(exch.sems)

    def wrapped(*refs):
        ins, refs = refs[:n_in], refs[n_in:]
        e_ins, refs = refs[:ne_in], refs[ne_in:]
        outs, refs = refs[:n_out], refs[n_out:]
        e_outs, refs = refs[:ne_out], refs[ne_out:]
        scratch, e_sems = refs[:len(refs) - ne_sem], refs[len(refs) - ne_sem:]

        @pl.when(first())
        def _():
            exch.start(e_ins, e_outs, e_sems)

        body(*ins, *outs, *scratch)

        @pl.when(last())
        def _():
            exch.finish(e_ins, e_outs, e_sems)

    return wrapped


def _hosted(exch, in_specs, out_specs, out_shape, scratch_shapes):
    any_spec = pl.BlockSpec(memory_space=pl.ANY)
    return dict(in_specs=list(in_specs) + [any_spec] * len(exch.arrays),
                out_specs=list(out_specs) + [any_spec] * len(exch.out_shapes),
                out_shape=list(out_shape) + list(exch.out_shapes),
                scratch_shapes=list(scratch_shapes) + list(exch.sems))


def allreduce_small(buf):
    r = buf.shape[0]
    vmem = pl.BlockSpec(memory_space=pltpu.VMEM)

    def body(buf_ref, out_ref, all_ref, send_sems, recv_sems):
        x, y, c = _place()
        me = _dev(x, y, c)
        copies = []
        for k in range(1, N_DEV):
            peer = (x ^ (k >> 2), y ^ ((k >> 1) & 1), c ^ (k & 1))
            copies.append(pltpu.make_async_remote_copy(
                src_ref=buf_ref, dst_ref=all_ref.at[me], send_sem=send_sems.at[k - 1], recv_sem=recv_sems.at[k - 1],
                device_id=peer, device_id_type=MESH))
        for cp in copies:
            cp.start()
        all_ref[me] = buf_ref[...]
        for cp in copies:
            cp.wait()
        total = all_ref[0]
        for d in range(1, N_DEV):
            total = total + all_ref[d]
        out_ref[...] = total

    return pl.pallas_call(
        body, name="allreduce_small", in_specs=[vmem], out_specs=vmem,
        out_shape=jax.ShapeDtypeStruct((r, LANES), F32),
        scratch_shapes=[pltpu.VMEM((N_DEV, r, LANES), F32), pltpu.SemaphoreType.DMA((N_DEV - 1,)),
                        pltpu.SemaphoreType.DMA((N_DEV - 1,))],
    )(buf)


def _cols_from_shards(g):
    return jnp.transpose(g, (1, 0, 2)).reshape(g.shape[1], N_DEV * g.shape[2])


def _pack(t):
    flat = t.reshape(-1)
    return jnp.pad(flat, (0, -flat.size % (8 * LANES))).reshape(-1, LANES)


def _pad_lanes(t):
    t = t.reshape(1, -1)
    return jnp.pad(t, ((0, 0), (0, LANES - t.shape[1])))


def kernel(x, meta_tokens, lb_logits, mix_norm_w, w_in, hg_norm_w, gd_conv_w, gd_a_log, gd_dt_bias, gd_norm_w, w_branch_a, w_branch_b, w_out, ffn_norm_w, w_ffn_in, w_ffn_out, final_norm_w, loss_target, m_meta_tokens, m_lb_logits, m_mix_norm_w, m_w_in, m_hg_norm_w, m_gd_conv_w, m_gd_a_log, m_gd_dt_bias, m_gd_norm_w, m_w_branch_a, m_w_branch_b, m_w_out, m_ffn_norm_w, m_w_ffn_in, m_w_ffn_out, m_final_norm_w, v_meta_tokens, v_lb_logits, v_mix_norm_w, v_w_in, v_hg_norm_w, v_gd_conv_w, v_gd_a_log, v_gd_dt_bias, v_gd_norm_w, v_w_branch_a, v_w_branch_b, v_w_out, v_ffn_norm_w, v_w_ffn_in, v_w_ffn_out, v_final_norm_w):
    xs = x[0]
    tgt = loss_target[0]
    l = xs.shape[0]
    lp = PRE + l
    me = _dev(*_place())

    big = [w_in[0], w_branch_a[0], w_branch_b[0], w_out[0], w_ffn_in[0], w_ffn_out[0]]
    big16 = [t.astype(BF16) for t in big]
    g_in, g_meta, g_conv = run_exchange("gather_first", gather_exchange([big16[0], meta_tokens, gd_conv_w[0]]))
    w_main, w_ab = shards_to_cols("w_in_cols", g_in, [W_MAIN, LANES], _w_in_pieces(), 256)
    meta_full = _cols_from_shards(g_meta)
    cw = _cols_from_shards(g_conv)
    pre = jnp.concatenate([jnp.zeros((PRE - N_META, D), F32), meta_full], axis=0)
    l0, l1 = lb_logits[0:1], lb_logits[1:2]
    alog, dtb = _pad_lanes(gd_a_log), _pad_lanes(gd_dt_bias)

    xn = norm1_fwd(pre, xs, mix_norm_w)
    proj, (g_ba, g_bb, g_out, g_ffi, g_ffo) = matmul("proj_main", xn, w_main, "nn", lp, W_MAIN, D, F32, tm=lp // 8,
                                                      exch=gather_exchange(big16[1:]))
    wa, = shards_to_cols("w_branch_a_cols", g_ba, [D], _plain_pieces(D // N_DEV), 256)
    wb, = shards_to_cols("w_branch_b_cols", g_bb, [D], _plain_pieces(D // N_DEV), 256)
    wo = g_out.reshape(D, D)
    w_gu, = shards_to_cols("w_ffn_in_cols", g_ffi, [2 * D_FF], _ffn_in_pieces(), 256)
    w_fo = g_ffo.reshape(D_FF, D)
    proj_ab = matmul("proj_ab", xn, w_ab, "nn", lp, LANES, D, F32)
    o_a, st_a, o_b, st_b, t_inv, conv_y = mixer_fwd(proj, proj_ab, l0, l1, hg_norm_w, cw, alog, dtb, gd_norm_w)
    ya = matmul("branch_a", o_a, wa, "nn", l, D, HW, F32, tn=D, a_off=(PRE // 512, 0))
    yb = matmul("branch_b", o_b, wb, "nn", l, D, HW, F32, tn=D, a_off=(PRE // 512, 0))
    merged, h1, xn2 = merge_mix_out(proj, ya, yb, wo, xs, ffn_norm_w)
    gu, act = ffn_in_fused(xn2, w_gu, tm=min(1024, l))
    f = matmul("ffn_out", act, w_fo, "nn", l, D, D_FF, F32, tm=1024, tn=512)
    loss_part, dh2, dh2_b, d_final_w = loss_head(h1, f, final_norm_w.reshape(1, D), tgt)

    tiles = {"w_in": (W_IN // N_DEV, 256), "w_branch_a": (256, D // N_DEV), "w_branch_b": (256, D // N_DEV),
             "w_out": (64, D), "w_ffn_in": (256, 2 * D_FF // N_DEV), "w_ffn_out": (176, D)}

    def chip_sums(tag, parts, host=None):
        blocks = [p.reshape((4, 2) + p.shape[1:]) for p in parts.values()]
        exch = core_exchange(blocks)
        hosted, from_core = (None, run_exchange("exchange_core_" + tag, exch)) if host is None else host(exch)
        return hosted, {nm: pair_add("pair_add_" + nm, p, r, tiles[nm]) for (nm, p, r) in zip(parts, blocks, from_core)}

    dgu = d_act_fused(dh2_b, w_fo, gu, tm=min(1024, l))
    dw_fo = matmul("dw_ffn_out", act, dh2_b, "tn", D_FF, D, l, BF16, tm=512, tn=512)
    dxn2 = matmul("d_xn2", dgu, w_gu, "nt", l, D, 2 * D_FF, F32, tn=256, rows_outer=True)
    dw_gu = matmul("dw_ffn_in", xn2, dgu, "tn", D, 2 * D_FF, l, BF16, tm=512, tn=512)
    dh1, dh1_b, d_ffn_norm_w = norm2_bwd(h1, ffn_norm_w, dxn2, dh2)
    (dproj, dya, dyb), sums = chip_sums("ffn", {
        "w_ffn_in": cols_to_shards("dw_ffn_in_shards", [dw_gu], 2 * D_FF // N_DEV, _ffn_in_pieces(), 256),
        "w_ffn_out": dw_fo.reshape(N_DEV, D_FF // N_DEV, D)},
        host=lambda exch: merge_bwd(dh1_b, wo, proj, ya, yb, exch))
    dw_o = matmul("dw_out", merged, dh1_b, "tn", D, D, l, BF16, tm=512, tn=512)
    do_a = matmul("d_o_a", dya, wa, "nt", lp, HW, D, F32, tm=lp // 8)
    do_b = matmul("d_o_b", dyb, wb, "nt", lp, HW, D, F32, tm=lp // 8)
    dw_a = matmul("dw_branch_a", o_a, dya, "tn", HW, D, lp, BF16, tm=512, tn=512)
    dw_b = matmul("dw_branch_b", o_b, dyb, "tn", HW, D, lp, BF16, tm=512, tn=512)
    sums.update(chip_sums("mix", {
        "w_branch_a": cols_to_shards("dw_branch_a_shards", [dw_a], D // N_DEV, _plain_pieces(D // N_DEV), 256),
        "w_branch_b": cols_to_shards("dw_branch_b_shards", [dw_b], D // N_DEV, _plain_pieces(D // N_DEV), 256),
        "w_out": dw_o.reshape(N_DEV, D // N_DEV, D)})[1])
    ffn_names, mix_names = ["w_ffn_in", "w_ffn_out"], ["w_branch_a", "w_branch_b", "w_out"]
    (dproj, dl0, dl1, d_hg_nw, dproj_ab, d_conv, d_alog, d_dtb, d_gd_nw), from_chips_early = mixer_bwd(
        proj, proj_ab, l0, l1, hg_norm_w, cw, alog, dtb, gd_norm_w, st_a, st_b, t_inv, conv_y, do_a, do_b, dproj,
        chips_exchange([sums[nm] for nm in ffn_names + mix_names]))
    dwt_main = matmul("dw_in_main", dproj, xn, "tn", W_MAIN, D, lp, BF16, tm=512, tn=512)
    dwt_ab = matmul("dw_in_ab", dproj_ab, xn, "tn", LANES, D, lp, BF16, tn=512)
    dxn_ab, sums_in = chip_sums("w_in", {
        "w_in": rows_to_shards("dw_in_shards", [dwt_main, dwt_ab], W_IN // N_DEV, _w_in_pieces(), 256)},
        host=lambda exch: matmul("d_xn_ab", dproj_ab, w_ab, "nt", lp, D, LANES, F32, exch=exch))
    sums.update(sums_in)
    from_chips = dict(zip(ffn_names + mix_names, from_chips_early))
    dxn, (from_chips["w_in"],) = matmul("d_xn", dproj, w_main, "nt", lp, D, W_MAIN, F32, tn=256, rows_outer=True,
                                        exch=chips_exchange([sums["w_in"]]))
    grad_x, dpre, d_mix_norm_w = norm1_bwd(pre, xs, mix_norm_w, dxn, dxn_ab, dh1)
    names = ["w_in", "w_branch_a", "w_branch_b", "w_out", "w_ffn_in", "w_ffn_out"]
    moms = [(m_w_in, v_w_in), (m_w_branch_a, v_w_branch_a), (m_w_branch_b, v_w_branch_b), (m_w_out, v_w_out),
            (m_w_ffn_in, v_w_ffn_in), (m_w_ffn_out, v_w_ffn_out)]
    upd = {}
    for nm, w, (m, v) in zip(names, big, moms):
        flip = (lambda t: t.T) if nm == "w_in" else (lambda t: t)
        res = adamw_shard("adamw_" + nm, flip(w), flip(m[0]), flip(v[0]), sums[nm], from_chips[nm], tiles[nm])
        upd[nm] = tuple(flip(t_)[None] for t_ in res)

    d_lb = jnp.concatenate([dl0, dl1], axis=0)
    rep = [_pack(t) for t in (d_lb, d_mix_norm_w, d_hg_nw, d_alog, d_dtb, d_gd_nw, d_ffn_norm_w, d_final_w)]
    n_rep = sum(t.shape[0] for t in rep)
    n_meta = N_META * D // LANES
    tot = allreduce_small(jnp.concatenate(rep + [_pack(loss_part), _pack(dpre[PRE - N_META:]), _pack(d_conv)], axis=0))
    loss = tot[n_rep, 0]
    g_meta_full = tot[n_rep + 8:n_rep + 8 + n_meta].reshape(N_META, D)
    g_conv_full = tot[n_rep + 8 + n_meta:].reshape(CONV_K, 3 * HW)
    g_meta_mine = lax.dynamic_slice_in_dim(g_meta_full, me * (D // N_DEV), D // N_DEV, axis=1)
    g_conv_mine = lax.dynamic_slice_in_dim(g_conv_full, me * (3 * DH), 3 * DH, axis=1)

    small = [("lb_logits", lb_logits, m_lb_logits, v_lb_logits), ("mix_norm_w", mix_norm_w, m_mix_norm_w, v_mix_norm_w),
             ("hg_norm_w", hg_norm_w, m_hg_norm_w, v_hg_norm_w), ("gd_a_log", gd_a_log, m_gd_a_log, v_gd_a_log),
             ("gd_dt_bias", gd_dt_bias, m_gd_dt_bias, v_gd_dt_bias), ("gd_norm_w", gd_norm_w, m_gd_norm_w, v_gd_norm_w),
             ("ffn_norm_w", ffn_norm_w, m_ffn_norm_w, v_ffn_norm_w),
             ("final_norm_w", final_norm_w, m_final_norm_w, v_final_norm_w),
             ("meta_tokens", meta_tokens, m_meta_tokens, v_meta_tokens), ("gd_conv_w", gd_conv_w, m_gd_conv_w, v_gd_conv_w)]

    sizes = [_pack(w).shape[0] for _, w, _, _ in small]
    g_small = jnp.concatenate([tot[:n_rep], _pack(g_meta_mine), _pack(g_conv_mine)], axis=0)
    assert g_small.shape[0] == sum(sizes)
    w_small = jnp.concatenate([_pack(w) for _, w, _, _ in small], axis=0)
    m_small = jnp.concatenate([_pack(m) for _, _, m, _ in small], axis=0)
    v_small = jnp.concatenate([_pack(v) for _, _, _, v in small], axis=0)
    d_small, nm_small, nv_small = adamw_small(w_small, g_small, m_small, v_small)
    row = 0
    for (nm, w, _, _), sz in zip(small, sizes):
        def unpack(t):
            return t[row:row + sz].reshape(-1)[:w.size].reshape(w.shape)
        upd[nm] = (unpack(g_small), unpack(d_small), unpack(nm_small), unpack(nv_small))
        row += sz

    order = ["meta_tokens", "lb_logits", "mix_norm_w", "w_in", "hg_norm_w", "gd_conv_w", "gd_a_log", "gd_dt_bias",
             "gd_norm_w", "w_branch_a", "w_branch_b", "w_out", "ffn_norm_w", "w_ffn_in", "w_ffn_out", "final_norm_w"]
    outs = [loss, grad_x[None]]
    for j in range(4):
        outs += [upd[nm][j] for nm in order]
    return tuple(outs)
```
